```python
import jax, jax.numpy as jnp
from jax import lax
import numpy as np

D_MODEL = 2048
BATCH = 8
SEQ = 4096
DEPTH = 4

N_MEM = 256
N_MEM_HEADS = 4
MEM_HEAD_DIM = D_MODEL // N_MEM_HEADS

HEAD_DIM_A = 128
N_HEADS_A = (D_MODEL // 2) // HEAD_DIM_A
WIDTH_A = N_HEADS_A * HEAD_DIM_A
DILATED_BRANCHES = ((128, 1), (512, 4), (2048, 16))

HEAD_DIM_B = 64
N_Q_HEADS_B = (D_MODEL // 2) // HEAD_DIM_B
N_KV_HEADS_B = max(1, N_Q_HEADS_B // 8)
WIDTH_B = N_Q_HEADS_B * HEAD_DIM_B
KV_WIDTH_B = N_KV_HEADS_B * HEAD_DIM_B
WINDOW_B = 128

MIX_WIDTH = WIDTH_A + WIDTH_B
SPLITS = (WIDTH_A, 2 * WIDTH_A, 3 * WIDTH_A, 3 * WIDTH_A + WIDTH_B,
          3 * WIDTH_A + WIDTH_B + KV_WIDTH_B)
IN_WIDTH = 3 * WIDTH_A + WIDTH_B + 2 * KV_WIDTH_B

ROPE_THETA = 500000.0
ROT_DIM_A = HEAD_DIM_A // 4
ROT_DIM_B = HEAD_DIM_B // 4

D_FF = 4 * D_MODEL
BLOCK = 128
ALPHA = (2 * DEPTH) ** 0.25
BETA = (8 * DEPTH) ** -0.25
LN_EPS = 1e-5
RMS_EPS = 1e-6
NEG_INF = -1e30

kernel_name = 'hybrid_dilated_swa_sink_deepnorm'


def _layernorm(x, g, b):
    xf = x.astype(jnp.float32)
    mu = xf.mean(-1, keepdims=True)
    var = jnp.square(xf - mu).mean(-1, keepdims=True)
    return ((xf - mu) * lax.rsqrt(var + LN_EPS) * g + b).astype(x.dtype)


def _rms_gain(y, g):
    yf = y.astype(jnp.float32)
    return yf * lax.rsqrt(jnp.mean(yf * yf, -1, keepdims=True) + RMS_EPS) * g


def _rope_tables(positions, rot_dim):
    inv_freq = ROPE_THETA ** (-jnp.arange(0, rot_dim, 2, dtype=jnp.float32) / rot_dim)
    ang = positions.astype(jnp.float32)[:, :, None] * inv_freq
    return jnp.cos(ang)[:, :, None, :], jnp.sin(ang)[:, :, None, :]


def _partial_rope(t, cos, sin):
    half = cos.shape[-1]
    t1 = t[..., :half].astype(jnp.float32)
    t2 = t[..., half:2 * half].astype(jnp.float32)
    rot = jnp.concatenate([t1 * cos - t2 * sin, t2 * cos + t1 * sin], -1).astype(t.dtype)
    return jnp.concatenate([rot, t[..., 2 * half:]], -1)


def _banded_stats(q, k, v, max_dist, scale):
    B, L, G, R, Dh = q.shape
    nb = L // BLOCK
    qb = q.reshape(B, nb, BLOCK, G, R, Dh).astype(jnp.float32)

    def two_blocks(t):
        tb = t.reshape(B, nb, BLOCK, G, Dh).astype(jnp.float32)
        prev = jnp.pad(tb, ((0, 0), (1, 0), (0, 0), (0, 0), (0, 0)))[:, :-1]
        return jnp.concatenate([prev, tb], axis=2)

    k2, v2 = two_blocks(k), two_blocks(v)
    s = jnp.einsum('bnqgrd,bnkgd->bngrqk', qb, k2) * scale
    q_idx = jnp.arange(BLOCK)[:, None] + BLOCK
    k_idx = jnp.arange(2 * BLOCK)[None, :]
    dist = q_idx - k_idx
    in_band = (dist >= 0) & (dist <= max_dist)
    has_prev = (jnp.arange(nb)[:, None, None] > 0) | (k_idx[None] >= BLOCK)
    valid = in_band[None] & has_prev
    s = jnp.where(valid[None, :, None, None], s, NEG_INF)
    m = s.max(-1)
    p = jnp.exp(s - m[..., None])
    l = p.sum(-1)
    acc = jnp.einsum('bngrqk,bnkgd->bnqgrd', p, v2)
    m = m.transpose(0, 1, 4, 2, 3).reshape(B, L, G, R)
    l = l.transpose(0, 1, 4, 2, 3).reshape(B, L, G, R)
    return m, l, acc.reshape(B, L, G, R, Dh)


def _to_strided(t, d, padded_len):
    B, S, H, Dh = t.shape
    Ld = S // d
    t = t.reshape(B, Ld, d, H, Dh).transpose(0, 2, 1, 3, 4).reshape(B * d, Ld, H, Dh)
    return jnp.pad(t, ((0, 0), (0, padded_len - Ld), (0, 0), (0, 0)))


def _from_strided(t, B, d, Ld):
    t = t[:, :Ld]
    t = t.reshape((B, d, Ld) + t.shape[2:])
    t = jnp.swapaxes(t, 1, 2)
    return t.reshape((B, d * Ld) + t.shape[3:])


def _dilated_attention(q, k, v):
    B, S, H, Dh = q.shape
    ms, ls, accs = [], [], []
    for window, d in DILATED_BRANCHES:
        Ld = S // d
        Lp = -(-Ld // BLOCK) * BLOCK
        m, l, acc = _banded_stats(_to_strided(q, d, Lp)[:, :, :, None, :],
                                  _to_strided(k, d, Lp), _to_strided(v, d, Lp),
                                  window // d, Dh ** -0.5)
        ms.append(_from_strided(m, B, d, Ld))
        ls.append(_from_strided(l, B, d, Ld))
        accs.append(_from_strided(acc, B, d, Ld))
    m = jnp.stack(ms)
    l = jnp.stack(ls)
    acc = jnp.stack(accs)
    c = jnp.exp(m - m.max(0, keepdims=True))
    num = (acc * c[..., None]).sum(0)
    den = (l * c).sum(0)
    return (num / den[..., None]).reshape(B, S, H * Dh)


def _sink_window_gqa(q, k, v, sinks):
    B, S, Hq, Dh = q.shape
    G = k.shape[2]
    R = Hq // G
    m, l, acc = _banded_stats(q.reshape(B, S, G, R, Dh), k, v, WINDOW_B - 1, Dh ** -0.5)
    sink = sinks.astype(jnp.float32).reshape(G, R)
    m2 = jnp.maximum(m, sink)
    c = jnp.exp(m - m2)
    den = l * c + jnp.exp(sink - m2)
    return (acc * (c / den)[..., None]).reshape(B, S, Hq * Dh)


def _hybrid_mixer(x, cos_a, sin_a, cos_b, sin_b, w_in, gn_a, gn_b, sinks, w_out):
    B, S, _ = x.shape
    h = x @ w_in
    qa, ka, va, qb, kb, vb = jnp.split(h, SPLITS, axis=-1)
    qa = _partial_rope(qa.reshape(B, S, N_HEADS_A, HEAD_DIM_A), cos_a, sin_a)
    ka = _partial_rope(ka.reshape(B, S, N_HEADS_A, HEAD_DIM_A), cos_a, sin_a)
    va = va.reshape(B, S, N_HEADS_A, HEAD_DIM_A)
    qb = _partial_rope(qb.reshape(B, S, N_Q_HEADS_B, HEAD_DIM_B), cos_b, sin_b)
    kb = _partial_rope(kb.reshape(B, S, N_KV_HEADS_B, HEAD_DIM_B), cos_b, sin_b)
    vb = vb.reshape(B, S, N_KV_HEADS_B, HEAD_DIM_B)
    ya = _dilated_attention(qa, ka, va)
    yb = _sink_window_gqa(qb, kb, vb, sinks)
    y = jnp.concatenate([_rms_gain(ya, gn_a), _rms_gain(yb, gn_b)], -1).astype(x.dtype)
    return y @ w_out


def _memory_attention(x, mem, w_q, w_kv, w_o):
    B, S, _ = x.shape
    q = (x @ w_q).reshape(B, S, N_MEM_HEADS, MEM_HEAD_DIM).astype(jnp.float32)
    k, v = jnp.split(mem @ w_kv, 2, axis=-1)
    k = k.reshape(B, -1, N_MEM_HEADS, MEM_HEAD_DIM).astype(jnp.float32)
    v = v.reshape(B, -1, N_MEM_HEADS, MEM_HEAD_DIM).astype(jnp.float32)
    p = jax.nn.softmax(jnp.einsum('bshd,bmhd->bhsm', q, k) * MEM_HEAD_DIM ** -0.5, axis=-1)
    o = jnp.einsum('bhsm,bmhd->bshd', p, v).reshape(B, S, D_MODEL).astype(x.dtype)
    return o @ w_o


def _sq_relu_mlp(x, w_up, w_down):
    h = jax.nn.relu(x @ w_up)
    return (h * h) @ w_down


def _fwd_setup_inputs(seed: int = 0) -> dict:
    key = jax.random.key(seed)
    ks = jax.random.split(key, 20)
    f32 = jnp.float32

    def nrm(k, shape, scale):
        return jax.random.normal(k, shape, f32) * scale

    def gain(k, shape):
        return 1.0 + 0.02 * jax.random.normal(k, shape, f32)

    x = nrm(ks[0], (BATCH, SEQ, D_MODEL), 1.0)
    mem = nrm(ks[1], (BATCH, N_MEM, D_MODEL), 1.0)
    positions = (jnp.arange(SEQ, dtype=jnp.int32)[None, :]
                 + jax.random.randint(ks[2], (BATCH, 1), 0, 4096, dtype=jnp.int32))
    in_col_scale = jnp.concatenate([
        jnp.ones((2 * WIDTH_A,), f32), jnp.full((WIDTH_A,), BETA, f32),
        jnp.ones((WIDTH_B + KV_WIDTH_B,), f32), jnp.full((KV_WIDTH_B,), BETA, f32)])
    w_in = nrm(ks[3], (DEPTH, D_MODEL, IN_WIDTH), D_MODEL ** -0.5) * in_col_scale
    gn_a = gain(ks[4], (DEPTH, WIDTH_A))
    gn_b = gain(ks[5], (DEPTH, WIDTH_B))
    sinks = nrm(ks[6], (DEPTH, N_Q_HEADS_B), 0.5)
    w_out = nrm(ks[7], (DEPTH, MIX_WIDTH, D_MODEL), BETA * MIX_WIDTH ** -0.5)
    ln_mix_g = gain(ks[8], (DEPTH, D_MODEL))
    ln_mix_b = nrm(ks[9], (DEPTH, D_MODEL), 0.02)
    w_mq = nrm(ks[10], (DEPTH, D_MODEL, D_MODEL), D_MODEL ** -0.5)
    kv_col_scale = jnp.concatenate([jnp.ones((D_MODEL,), f32), jnp.full((D_MODEL,), BETA, f32)])
    w_mkv = nrm(ks[11], (DEPTH, D_MODEL, 2 * D_MODEL), D_MODEL ** -0.5) * kv_col_scale
    w_mo = nrm(ks[12], (DEPTH, D_MODEL, D_MODEL), BETA * D_MODEL ** -0.5)
    ln_mem_g = gain(ks[13], (DEPTH, D_MODEL))
    ln_mem_b = nrm(ks[14], (DEPTH, D_MODEL), 0.02)
    w_up = nrm(ks[15], (DEPTH, D_MODEL, D_FF), D_MODEL ** -0.5)
    w_down = nrm(ks[16], (DEPTH, D_FF, D_MODEL), BETA * D_FF ** -0.5)
    ln_ff_g = gain(ks[17], (DEPTH, D_MODEL))
    ln_ff_b = nrm(ks[18], (DEPTH, D_MODEL), 0.02)
    return {'x': x, 'mem': mem, 'positions': positions, 'w_in': w_in, 'gn_a': gn_a,
            'gn_b': gn_b, 'sinks': sinks, 'w_out': w_out, 'ln_mix_g': ln_mix_g,
            'ln_mix_b': ln_mix_b, 'w_mq': w_mq, 'w_mkv': w_mkv, 'w_mo': w_mo,
            'ln_mem_g': ln_mem_g, 'ln_mem_b': ln_mem_b, 'w_up': w_up, 'w_down': w_down,
            'ln_ff_g': ln_ff_g, 'ln_ff_b': ln_ff_b}


def _fwd_reference(x, mem, positions, w_in, gn_a, gn_b, sinks, w_out, ln_mix_g, ln_mix_b,
              w_mq, w_mkv, w_mo, ln_mem_g, ln_mem_b, w_up, w_down, ln_ff_g, ln_ff_b):
    cos_a, sin_a = _rope_tables(positions, ROT_DIM_A)
    cos_b, sin_b = _rope_tables(positions, ROT_DIM_B)
    for i in range(DEPTH):
        y = _hybrid_mixer(x, cos_a, sin_a, cos_b, sin_b, w_in[i], gn_a[i], gn_b[i],
                          sinks[i], w_out[i])
        x = _layernorm(ALPHA * x + y, ln_mix_g[i], ln_mix_b[i])
        y = _memory_attention(x, mem, w_mq[i], w_mkv[i], w_mo[i])
        x = _layernorm(ALPHA * x + y, ln_mem_g[i], ln_mem_b[i])
        y = _sq_relu_mlp(x, w_up[i], w_down[i])
        x = _layernorm(ALPHA * x + y, ln_ff_g[i], ln_ff_b[i])
    return x


import jax as _jax
import jax.numpy as _jnp

TWIN_FORMAT = 'train_step'
FWD_PARAMS = ['x', 'mem', 'positions', 'w_in', 'gn_a', 'gn_b', 'sinks', 'w_out', 'ln_mix_g', 'ln_mix_b', 'w_mq', 'w_mkv', 'w_mo', 'ln_mem_g', 'ln_mem_b', 'w_up', 'w_down', 'ln_ff_g', 'ln_ff_b']
TWIN_WEIGHTS = ['w_in', 'gn_a', 'gn_b', 'sinks', 'w_out', 'ln_mix_g', 'ln_mix_b', 'w_mq', 'w_mkv', 'w_mo', 'ln_mem_g', 'ln_mem_b', 'w_up', 'w_down', 'ln_ff_g', 'ln_ff_b']
TWIN_DIFF_INPUT = 'x'
TWIN_INPUTS = ['x', 'mem', 'positions', 'w_in', 'gn_a', 'gn_b', 'sinks', 'w_out', 'ln_mix_g', 'ln_mix_b', 'w_mq', 'w_mkv', 'w_mo', 'ln_mem_g', 'ln_mem_b', 'w_up', 'w_down', 'ln_ff_g', 'ln_ff_b', 'loss_target', 'm_w_in', 'm_gn_a', 'm_gn_b', 'm_sinks', 'm_w_out', 'm_ln_mix_g', 'm_ln_mix_b', 'm_w_mq', 'm_w_mkv', 'm_w_mo', 'm_ln_mem_g', 'm_ln_mem_b', 'm_w_up', 'm_w_down', 'm_ln_ff_g', 'm_ln_ff_b', 'v_w_in', 'v_gn_a', 'v_gn_b', 'v_sinks', 'v_w_out', 'v_ln_mix_g', 'v_ln_mix_b', 'v_w_mq', 'v_w_mkv', 'v_w_mo', 'v_ln_mem_g', 'v_ln_mem_b', 'v_w_up', 'v_w_down', 'v_ln_ff_g', 'v_ln_ff_b']
TWIN_OUTPUTS = ['loss', 'grad_x', 'grad_w_in', 'grad_gn_a', 'grad_gn_b', 'grad_sinks', 'grad_w_out', 'grad_ln_mix_g', 'grad_ln_mix_b', 'grad_w_mq', 'grad_w_mkv', 'grad_w_mo', 'grad_ln_mem_g', 'grad_ln_mem_b', 'grad_w_up', 'grad_w_down', 'grad_ln_ff_g', 'grad_ln_ff_b', 'delta_w_in', 'delta_gn_a', 'delta_gn_b', 'delta_sinks', 'delta_w_out', 'delta_ln_mix_g', 'delta_ln_mix_b', 'delta_w_mq', 'delta_w_mkv', 'delta_w_mo', 'delta_ln_mem_g', 'delta_ln_mem_b', 'delta_w_up', 'delta_w_down', 'delta_ln_ff_g', 'delta_ln_ff_b', 'new_m_w_in', 'new_m_gn_a', 'new_m_gn_b', 'new_m_sinks', 'new_m_w_out', 'new_m_ln_mix_g', 'new_m_ln_mix_b', 'new_m_w_mq', 'new_m_w_mkv', 'new_m_w_mo', 'new_m_ln_mem_g', 'new_m_ln_mem_b', 'new_m_w_up', 'new_m_w_down', 'new_m_ln_ff_g', 'new_m_ln_ff_b', 'new_v_w_in', 'new_v_gn_a', 'new_v_gn_b', 'new_v_sinks', 'new_v_w_out', 'new_v_ln_mix_g', 'new_v_ln_mix_b', 'new_v_w_mq', 'new_v_w_mkv', 'new_v_w_mo', 'new_v_ln_mem_g', 'new_v_ln_mem_b', 'new_v_w_up', 'new_v_w_down', 'new_v_ln_ff_g', 'new_v_ln_ff_b']
TWIN_LEAF_KINDS = {'loss': 'loss', 'grad_x': 'grad_x', 'grad_w_in': 'grad_w', 'grad_gn_a': 'grad_w', 'grad_gn_b': 'grad_w', 'grad_sinks': 'grad_w', 'grad_w_out': 'grad_w', 'grad_ln_mix_g': 'grad_w', 'grad_ln_mix_b': 'grad_w', 'grad_w_mq': 'grad_w', 'grad_w_mkv': 'grad_w', 'grad_w_mo': 'grad_w', 'grad_ln_mem_g': 'grad_w', 'grad_ln_mem_b': 'grad_w', 'grad_w_up': 'grad_w', 'grad_w_down': 'grad_w', 'grad_ln_ff_g': 'grad_w', 'grad_ln_ff_b': 'grad_w', 'delta_w_in': 'delta_w', 'delta_gn_a': 'delta_w', 'delta_gn_b': 'delta_w', 'delta_sinks': 'delta_w', 'delta_w_out': 'delta_w', 'delta_ln_mix_g': 'delta_w', 'delta_ln_mix_b': 'delta_w', 'delta_w_mq': 'delta_w', 'delta_w_mkv': 'delta_w', 'delta_w_mo': 'delta_w', 'delta_ln_mem_g': 'delta_w', 'delta_ln_mem_b': 'delta_w', 'delta_w_up': 'delta_w', 'delta_w_down': 'delta_w', 'delta_ln_ff_g': 'delta_w', 'delta_ln_ff_b': 'delta_w', 'new_m_w_in': 'new_m', 'new_m_gn_a': 'new_m', 'new_m_gn_b': 'new_m', 'new_m_sinks': 'new_m', 'new_m_w_out': 'new_m', 'new_m_ln_mix_g': 'new_m', 'new_m_ln_mix_b': 'new_m', 'new_m_w_mq': 'new_m', 'new_m_w_mkv': 'new_m', 'new_m_w_mo': 'new_m', 'new_m_ln_mem_g': 'new_m', 'new_m_ln_mem_b': 'new_m', 'new_m_w_up': 'new_m', 'new_m_w_down': 'new_m', 'new_m_ln_ff_g': 'new_m', 'new_m_ln_ff_b': 'new_m', 'new_v_w_in': 'new_v', 'new_v_gn_a': 'new_v', 'new_v_gn_b': 'new_v', 'new_v_sinks': 'new_v', 'new_v_w_out': 'new_v', 'new_v_ln_mix_g': 'new_v', 'new_v_ln_mix_b': 'new_v', 'new_v_w_mq': 'new_v', 'new_v_w_mkv': 'new_v', 'new_v_w_mo': 'new_v', 'new_v_ln_mem_g': 'new_v', 'new_v_ln_mem_b': 'new_v', 'new_v_w_up': 'new_v', 'new_v_w_down': 'new_v', 'new_v_ln_ff_g': 'new_v', 'new_v_ln_ff_b': 'new_v'}


def _forward(args):
    return _fwd_reference(*[args[k] for k in FWD_PARAMS])


def _output_shape():
    def fwd():
        inp = _fwd_setup_inputs(0)
        return _fwd_reference(*[inp[k] for k in FWD_PARAMS])
    out = _jax.eval_shape(fwd)
    return out.shape, out.dtype

N_MICROBATCH = 1
ADAM_LR = 0.001
ADAM_B1 = 0.9
ADAM_B2 = 0.999
ADAM_EPS = 1e-08
ADAM_WD = 0.01
ADAM_STEP = 10
PER_EXAMPLE_BATCH_AXIS = {'x': 0, 'mem': 0, 'positions': 0, 'loss_target': 0}
SHARED_INPUTS = []
_WEIGHT_DTYPES = {'w_in': _jnp.float32, 'gn_a': _jnp.float32, 'gn_b': _jnp.float32, 'sinks': _jnp.float32, 'w_out': _jnp.float32, 'ln_mix_g': _jnp.float32, 'ln_mix_b': _jnp.float32, 'w_mq': _jnp.float32, 'w_mkv': _jnp.float32, 'w_mo': _jnp.float32, 'ln_mem_g': _jnp.float32, 'ln_mem_b': _jnp.float32, 'w_up': _jnp.float32, 'w_down': _jnp.float32, 'ln_ff_g': _jnp.float32, 'ln_ff_b': _jnp.float32}
MOMENT_SCALE = {'w_in': 9.805817e-02, 'gn_a': 6.585650e-02, 'gn_b': 5.914032e-02, 'sinks': 8.340851e-03, 'w_out': 1.489739e-01, 'ln_mix_g': 4.392092e-01, 'ln_mix_b': 5.381912e-01, 'w_mq': 9.966762e-04, 'w_mkv': 2.766461e-03, 'w_mo': 3.805594e-03, 'ln_mem_g': 4.399340e-01, 'ln_mem_b': 5.384474e-01, 'w_up': 1.685278e-02, 'w_down': 1.552074e-01, 'ln_ff_g': 8.195691e+00, 'ln_ff_b': 2.870414e+00}


def _to_microbatches(a, axis):
    t = _jnp.moveaxis(a, axis, 0)
    t = t.reshape((N_MICROBATCH, t.shape[0] // N_MICROBATCH) + t.shape[1:])
    return _jnp.moveaxis(t, 1, axis + 1)


def setup_inputs(seed: int = 0) -> dict:
    inp = _fwd_setup_inputs(seed)
    key = _jax.random.fold_in(_jax.random.key(seed), 7919)
    shape, _ = _output_shape()
    out = dict(inp)
    out["loss_target"] = _jax.random.normal(_jax.random.fold_in(key, 0), shape, _jnp.float32)
    for i, name in enumerate(TWIN_WEIGHTS):
        w = inp[name].astype(_jnp.float32)
        if MOMENT_SCALE is None:
            s = _jnp.sqrt(_jnp.mean(_jnp.square(w)) + 1e-30)
        else:
            s = MOMENT_SCALE[name]
        km, kv = _jax.random.split(_jax.random.fold_in(key, i + 1))
        out[name] = w
        out["m_" + name] = s * _jax.random.normal(km, w.shape, _jnp.float32)
        out["v_" + name] = (s * s) * _jax.random.uniform(kv, w.shape, _jnp.float32, 0.5, 1.5)
    if N_MICROBATCH > 1:
        for name, axis in PER_EXAMPLE_BATCH_AXIS.items():
            out[name] = _to_microbatches(out[name], axis)
    return {'x': out['x'], 'mem': out['mem'], 'positions': out['positions'], 'w_in': out['w_in'], 'gn_a': out['gn_a'], 'gn_b': out['gn_b'], 'sinks': out['sinks'], 'w_out': out['w_out'], 'ln_mix_g': out['ln_mix_g'], 'ln_mix_b': out['ln_mix_b'], 'w_mq': out['w_mq'], 'w_mkv': out['w_mkv'], 'w_mo': out['w_mo'], 'ln_mem_g': out['ln_mem_g'], 'ln_mem_b': out['ln_mem_b'], 'w_up': out['w_up'], 'w_down': out['w_down'], 'ln_ff_g': out['ln_ff_g'], 'ln_ff_b': out['ln_ff_b'], 'loss_target': out['loss_target'], 'm_w_in': out['m_w_in'], 'm_gn_a': out['m_gn_a'], 'm_gn_b': out['m_gn_b'], 'm_sinks': out['m_sinks'], 'm_w_out': out['m_w_out'], 'm_ln_mix_g': out['m_ln_mix_g'], 'm_ln_mix_b': out['m_ln_mix_b'], 'm_w_mq': out['m_w_mq'], 'm_w_mkv': out['m_w_mkv'], 'm_w_mo': out['m_w_mo'], 'm_ln_mem_g': out['m_ln_mem_g'], 'm_ln_mem_b': out['m_ln_mem_b'], 'm_w_up': out['m_w_up'], 'm_w_down': out['m_w_down'], 'm_ln_ff_g': out['m_ln_ff_g'], 'm_ln_ff_b': out['m_ln_ff_b'], 'v_w_in': out['v_w_in'], 'v_gn_a': out['v_gn_a'], 'v_gn_b': out['v_gn_b'], 'v_sinks': out['v_sinks'], 'v_w_out': out['v_w_out'], 'v_ln_mix_g': out['v_ln_mix_g'], 'v_ln_mix_b': out['v_ln_mix_b'], 'v_w_mq': out['v_w_mq'], 'v_w_mkv': out['v_w_mkv'], 'v_w_mo': out['v_w_mo'], 'v_ln_mem_g': out['v_ln_mem_g'], 'v_ln_mem_b': out['v_ln_mem_b'], 'v_w_up': out['v_w_up'], 'v_w_down': out['v_w_down'], 'v_ln_ff_g': out['v_ln_ff_g'], 'v_ln_ff_b': out['v_ln_ff_b']}


def _loss(weights, diff, rest, loss_target):
    with _jax.named_scope("forward"):
        args = {**rest, TWIN_DIFF_INPUT: diff, **{k: w.astype(_WEIGHT_DTYPES[k]) for k, w in weights.items()}}
        y = _forward(args)
    with _jax.named_scope("loss_head"):
        err = _jnp.square(y.astype(_jnp.float32) - loss_target)
        return 0.5 * _jnp.sum(_jnp.mean(err, axis=-1)) if err.ndim else 0.5 * err


def _adamw(w, g, m, v):
    m = ADAM_B1 * m + (1.0 - ADAM_B1) * g
    v = ADAM_B2 * v + (1.0 - ADAM_B2) * _jnp.square(g)
    m_hat = m / (1.0 - ADAM_B1 ** ADAM_STEP)
    v_hat = v / (1.0 - ADAM_B2 ** ADAM_STEP)
    delta = -ADAM_LR * (m_hat / (_jnp.sqrt(v_hat) + ADAM_EPS) + ADAM_WD * w)
    return delta, m, v


def reference(x, mem, positions, w_in, gn_a, gn_b, sinks, w_out, ln_mix_g, ln_mix_b, w_mq, w_mkv, w_mo, ln_mem_g, ln_mem_b, w_up, w_down, ln_ff_g, ln_ff_b, loss_target, m_w_in, m_gn_a, m_gn_b, m_sinks, m_w_out, m_ln_mix_g, m_ln_mix_b, m_w_mq, m_w_mkv, m_w_mo, m_ln_mem_g, m_ln_mem_b, m_w_up, m_w_down, m_ln_ff_g, m_ln_ff_b, v_w_in, v_gn_a, v_gn_b, v_sinks, v_w_out, v_ln_mix_g, v_ln_mix_b, v_w_mq, v_w_mkv, v_w_mo, v_ln_mem_g, v_ln_mem_b, v_w_up, v_w_down, v_ln_ff_g, v_ln_ff_b):
    given = dict(x=x, mem=mem, positions=positions, w_in=w_in, gn_a=gn_a, gn_b=gn_b, sinks=sinks, w_out=w_out, ln_mix_g=ln_mix_g, ln_mix_b=ln_mix_b, w_mq=w_mq, w_mkv=w_mkv, w_mo=w_mo, ln_mem_g=ln_mem_g, ln_mem_b=ln_mem_b, w_up=w_up, w_down=w_down, ln_ff_g=ln_ff_g, ln_ff_b=ln_ff_b, loss_target=loss_target, m_w_in=m_w_in, m_gn_a=m_gn_a, m_gn_b=m_gn_b, m_sinks=m_sinks, m_w_out=m_w_out, m_ln_mix_g=m_ln_mix_g, m_ln_mix_b=m_ln_mix_b, m_w_mq=m_w_mq, m_w_mkv=m_w_mkv, m_w_mo=m_w_mo, m_ln_mem_g=m_ln_mem_g, m_ln_mem_b=m_ln_mem_b, m_w_up=m_w_up, m_w_down=m_w_down, m_ln_ff_g=m_ln_ff_g, m_ln_ff_b=m_ln_ff_b, v_w_in=v_w_in, v_gn_a=v_gn_a, v_gn_b=v_gn_b, v_sinks=v_sinks, v_w_out=v_w_out, v_ln_mix_g=v_ln_mix_g, v_ln_mix_b=v_ln_mix_b, v_w_mq=v_w_mq, v_w_mkv=v_w_mkv, v_w_mo=v_w_mo, v_ln_mem_g=v_ln_mem_g, v_ln_mem_b=v_ln_mem_b, v_w_up=v_w_up, v_w_down=v_w_down, v_ln_ff_g=v_ln_ff_g, v_ln_ff_b=v_ln_ff_b)
    weights = {n: given[n] for n in TWIN_WEIGHTS}
    shared = {n: given[n] for n in SHARED_INPUTS}
    per_example = {n: given[n] for n in ['x', 'mem', 'positions']}
    grad_fn = _jax.value_and_grad(_loss, argnums=(0, 1))

    def one_microbatch(ex, loss_target):
        ex = dict(ex)
        diff = ex.pop(TWIN_DIFF_INPUT)
        return grad_fn(weights, diff, {**shared, **ex}, loss_target)

    if N_MICROBATCH == 1:
        loss, (grad_w, grad_x) = one_microbatch(per_example, given["loss_target"])
    else:
        def body(carry, xs):
            loss_sum, grad_sum = carry
            l_k, (gw_k, gx_k) = one_microbatch(xs[0], xs[1])
            with _jax.named_scope("update"):
                return (loss_sum + l_k, _jax.tree.map(_jnp.add, grad_sum, gw_k)), gx_k

        init = (_jnp.zeros((), _jnp.float32), _jax.tree.map(_jnp.zeros_like, weights))
        (loss, grad_w), grad_x = _jax.lax.scan(body, init, (per_example, given["loss_target"]))
    with _jax.named_scope("update"):
        delta_w, new_m, new_v = {}, {}, {}
        for n in TWIN_WEIGHTS:
            delta_w[n], new_m[n], new_v[n] = _adamw(weights[n], grad_w[n], given["m_" + n], given["v_" + n])
    return (loss, grad_x, *[grad_w[n] for n in TWIN_WEIGHTS], *[delta_w[n] for n in TWIN_WEIGHTS],
            *[new_m[n] for n in TWIN_WEIGHTS], *[new_v[n] for n in TWIN_WEIGHTS])
```

```python
import functools

import jax
import jax.numpy as jnp
from jax import lax
from jax.experimental import pallas as pl
from jax.experimental.pallas import tpu as pltpu

F32 = jnp.float32
BF16 = jnp.bfloat16
MESH = pl.DeviceIdType.MESH

HEAD_DIM_A = 128
HEAD_DIM_B = 64
LANES = 128
BLOCK = 128
DILATED_BRANCHES = ((128, 1), (512, 4), (2048, 16))
WINDOW_B = 128
N_MEM_HEADS = 4
ROPE_THETA = 500000.0
LN_EPS = 1e-5
RMS_EPS = 1e-6
NEG_INF = -1e30
ADAM_LR = 0.001
ADAM_B1 = 0.9
ADAM_B2 = 0.999
ADAM_EPS = 1e-08
ADAM_WD = 0.01
ADAM_STEP = 10
N_CHIPS = 4
VMEM_LIMIT = 56 * 1024 * 1024

BIG = ("w_in", "w_out", "w_mq", "w_mkv", "w_mo", "w_up", "w_down")
COL_SHARDED = ("w_in", "w_mkv", "w_up")


def _tile(n, target, mult=LANES):
    best = None
    t = mult
    while t <= min(n, target):
        if n % t == 0:
            best = t
        t += mult
    return best if best is not None else n


def _params(sem=None):
    return pltpu.CompilerParams(dimension_semantics=sem, vmem_limit_bytes=VMEM_LIMIT)


def _dot(a, b):
    return jnp.dot(a, b, preferred_element_type=F32)


def _dot_nt(a, b):
    return lax.dot_general(a, b, (((1,), (1,)), ((), ())), preferred_element_type=F32)


def _dot_tn(a, b):
    return lax.dot_general(a, b, (((0,), (0,)), ((), ())), preferred_element_type=F32)


def mm_nn(a, b3, *, name, out_dtype, relu2=False, tm=512, tn=1024, tk=2048):
    M, K = a.shape
    nsh, _, nk = b3.shape
    tm, tn, tk = _tile(M, tm, 8), _tile(nk, tn), _tile(K, tk)
    nb, ksteps = nk // tn, K // tk

    def body(a_ref, b_ref, *rest):
        outs, scr = rest[:2 if relu2 else 1], rest[2 if relu2 else 1:]

        def finish(acc):
            if relu2:
                outs[0][...] = acc.astype(outs[0].dtype)
                r = jnp.maximum(acc, 0.0)
                outs[1][...] = (r * r).astype(outs[1].dtype)
            else:
                outs[0][...] = acc.astype(outs[0].dtype)

        if ksteps == 1:
            finish(_dot(a_ref[...], b_ref[...]))
        else:
            acc_ref = scr[0]
            k = pl.program_id(2)

            @pl.when(k == 0)
            def _():
                acc_ref[...] = jnp.zeros_like(acc_ref)

            acc_ref[...] += _dot(a_ref[...], b_ref[...])

            @pl.when(k == ksteps - 1)
            def _():
                finish(acc_ref[...])

    o_spec = pl.BlockSpec((tm, tn), lambda i, j, k: (i, j))
    o_shape = jax.ShapeDtypeStruct((M, nsh * nk), out_dtype)
    return pl.pallas_call(
        body, name=name,
        grid=(M // tm, nsh * nb, ksteps),
        in_specs=[pl.BlockSpec((tm, tk), lambda i, j, k: (i, k)),
                  pl.BlockSpec((None, tk, tn), lambda i, j, k: (j // nb, k, j % nb))],
        out_specs=(o_spec, o_spec) if relu2 else o_spec,
        out_shape=(o_shape, o_shape) if relu2 else o_shape,
        scratch_shapes=[] if ksteps == 1 else [pltpu.VMEM((tm, tn), F32)],
        compiler_params=_params(("parallel", "parallel", "arbitrary")),
    )(a, b3)


def mm_ln(a, w, resid, g, b, *, name, alpha, tm=512, tk=512):
    M, K = a.shape
    D = w.shape[1]
    tm, tk = _tile(M, tm, 8), _tile(K, tk)
    ksteps = K // tk

    def body(a_ref, w_ref, r_ref, g_ref, b_ref, z_ref, xn_ref, xb_ref, acc_ref):
        k = pl.program_id(1)

        @pl.when(k == 0)
        def _():
            acc_ref[...] = jnp.zeros_like(acc_ref)

        acc_ref[...] += _dot(a_ref[...], w_ref[...])

        @pl.when(k == ksteps - 1)
        def _():
            z = alpha * r_ref[...] + acc_ref[...]
            mu = jnp.mean(z, axis=-1, keepdims=True)
            zc = z - mu
            var = jnp.mean(zc * zc, axis=-1, keepdims=True)
            xn = zc * lax.rsqrt(var + LN_EPS) * g_ref[...] + b_ref[...]
            z_ref[...] = z
            xn_ref[...] = xn
            xb_ref[...] = xn.astype(BF16)

    row = pl.BlockSpec((tm, D), lambda i, k: (i, 0))
    vec = pl.BlockSpec((1, D), lambda i, k: (0, 0))
    return pl.pallas_call(
        body, name=name,
        grid=(M // tm, ksteps),
        in_specs=[pl.BlockSpec((tm, tk), lambda i, k: (i, k)),
                  pl.BlockSpec((tk, D), lambda i, k: (k, 0)), row, vec, vec],
        out_specs=(row, row, row),
        out_shape=(jax.ShapeDtypeStruct((M, D), F32), jax.ShapeDtypeStruct((M, D), F32),
                   jax.ShapeDtypeStruct((M, D), BF16)),
        scratch_shapes=[pltpu.VMEM((tm, D), F32)],
        compiler_params=_params(("parallel", "arbitrary")),
    )(a, w, resid, g, b)


def mm_nt(a, b3, *, name, out_dtype, resid=None, alpha=1.0, umul=None, tm=512, tko=1024, tr=2048):
    M, N = a.shape
    nsh, K, nk = b3.shape
    tm, tko, tr = _tile(M, tm, 8), _tile(K, tko), _tile(nk, tr)
    nb = nk // tr
    rsteps = nsh * nb

    def body(a_ref, b_ref, *rest):
        rest = list(rest)
        r_ref = rest.pop(0) if resid is not None else None
        u_ref = rest.pop(0) if umul is not None else None
        o_ref = rest.pop(0)

        def finish(acc):
            if r_ref is not None:
                acc = acc + alpha * r_ref[...]
            if u_ref is not None:
                acc = acc * (2.0 * jnp.maximum(u_ref[...].astype(F32), 0.0))
            o_ref[...] = acc.astype(o_ref.dtype)

        if rsteps == 1:
            finish(_dot_nt(a_ref[...], b_ref[...]))
        else:
            acc_ref = rest[0]
            r = pl.program_id(2)

            @pl.when(r == 0)
            def _():
                acc_ref[...] = jnp.zeros_like(acc_ref)

            acc_ref[...] += _dot_nt(a_ref[...], b_ref[...])

            @pl.when(r == rsteps - 1)
            def _():
                finish(acc_ref[...])

    o_spec = pl.BlockSpec((tm, tko), lambda i, j, r: (i, j))
    in_specs = [pl.BlockSpec((tm, tr), lambda i, j, r: (i, r)),
                pl.BlockSpec((None, tko, tr), lambda i, j, r: (r // nb, j, r % nb))]
    args = [a, b3]
    for extra in (resid, umul):
        if extra is not None:
            in_specs.append(o_spec)
            args.append(extra)
    return pl.pallas_call(
        body, name=name,
        grid=(M // tm, K // tko, rsteps),
        in_specs=in_specs, out_specs=o_spec,
        out_shape=jax.ShapeDtypeStruct((M, K), out_dtype),
        scratch_shapes=[] if rsteps == 1 else [pltpu.VMEM((tm, tko), F32)],
        compiler_params=_params(("parallel", "parallel", "arbitrary")),
    )(*args)


def mm_tn(a, g, nsh, *, name, tk=1024, tn=1024, tm=512):
    M, K = a.shape
    N = g.shape[1]
    nk = N // nsh
    tk, tn, tm = _tile(K, tk), _tile(nk, tn), _tile(M, tm, 8)
    nb, msteps = nk // tn, M // tm

    def body(a_ref, g_ref, o_ref, acc_ref):
        m = pl.program_id(2)

        @pl.when(m == 0)
        def _():
            acc_ref[...] = jnp.zeros_like(acc_ref)

        acc_ref[...] += _dot_tn(a_ref[...], g_ref[...])

        @pl.when(m == msteps - 1)
        def _():
            o_ref[...] = acc_ref[...].astype(o_ref.dtype)

    return pl.pallas_call(
        body, name=name,
        grid=(K // tk, nsh * nb, msteps),
        in_specs=[pl.BlockSpec((tm, tk), lambda i, j, m: (m, i)),
                  pl.BlockSpec((tm, tn), lambda i, j, m: (m, j))],
        out_specs=pl.BlockSpec((None, tk, tn), lambda i, j, m: (j // nb, i, j % nb)),
        out_shape=jax.ShapeDtypeStruct((nsh, K, nk), BF16),
        scratch_shapes=[pltpu.VMEM((tk, tn), F32)],
        compiler_params=_params(("parallel", "parallel", "arbitrary")),
    )(a, g)


def _fold8(t):
    return t.reshape(t.shape[0] // 8, 8, t.shape[1]).sum(axis=0)


def ln_bwd(dy, z, g, *, name, tm=256):
    M, D = z.shape
    tm = _tile(M, tm, 8)

    def body(dy_ref, z_ref, g_ref, dz_ref, dzb_ref, dg_ref, db_ref):
        i = pl.program_id(0)
        z_ = z_ref[...]
        dy_ = dy_ref[...]
        mu = jnp.mean(z_, axis=-1, keepdims=True)
        zc = z_ - mu
        var = jnp.mean(zc * zc, axis=-1, keepdims=True)
        rstd = lax.rsqrt(var + LN_EPS)
        xh = zc * rstd
        dxh = dy_ * g_ref[...]
        m1 = jnp.mean(dxh, axis=-1, keepdims=True)
        m2 = jnp.mean(dxh * xh, axis=-1, keepdims=True)
        dz = rstd * (dxh - m1 - xh * m2)
        dz_ref[...] = dz
        dzb_ref[...] = dz.astype(BF16)

        @pl.when(i == 0)
        def _():
            dg_ref[...] = jnp.zeros_like(dg_ref)
            db_ref[...] = jnp.zeros_like(db_ref)

        dg_ref[...] += _fold8(dy_ * xh)
        db_ref[...] += _fold8(dy_)

    row = pl.BlockSpec((tm, D), lambda i: (i, 0))
    acc = pl.BlockSpec((8, D), lambda i: (0, 0))
    return pl.pallas_call(
        body, name=name, grid=(M // tm,),
        in_specs=[row, row, pl.BlockSpec((1, D), lambda i: (0, 0))],
        out_specs=(row, row, acc, acc),
        out_shape=(jax.ShapeDtypeStruct((M, D), F32), jax.ShapeDtypeStruct((M, D), BF16),
                   jax.ShapeDtypeStruct((8, D), F32), jax.ShapeDtypeStruct((8, D), F32)),
        compiler_params=_params(("arbitrary",)),
    )(dy, z, g)


def rms_fwd(ya, yb, ga, gb, *, name, tm=512):
    M, WA = ya.shape
    WB = yb.shape[1]
    tm = _tile(M, tm, 8)

    def body(ya_ref, yb_ref, ga_ref, gb_ref, o_ref):
        for y_ref, g_ref, lo, w in ((ya_ref, ga_ref, 0, WA), (yb_ref, gb_ref, WA, WB)):
            y = y_ref[...]
            r = lax.rsqrt(jnp.mean(y * y, axis=-1, keepdims=True) + RMS_EPS)
            o_ref[:, lo:lo + w] = (y * r * g_ref[...]).astype(o_ref.dtype)

    return pl.pallas_call(
        body, name=name, grid=(M // tm,),
        in_specs=[pl.BlockSpec((tm, WA), lambda i: (i, 0)), pl.BlockSpec((tm, WB), lambda i: (i, 0)),
                  pl.BlockSpec((1, WA), lambda i: (0, 0)), pl.BlockSpec((1, WB), lambda i: (0, 0))],
        out_specs=pl.BlockSpec((tm, WA + WB), lambda i: (i, 0)),
        out_shape=jax.ShapeDtypeStruct((M, WA + WB), BF16),
        compiler_params=_params(("parallel",)),
    )(ya, yb, ga, gb)


def rms_bwd(dy, ya, yb, ga, gb, *, name, tm=512):
    M, WA = ya.shape
    WB = yb.shape[1]
    tm = _tile(M, tm, 8)

    def body(dy_ref, ya_ref, yb_ref, ga_ref, gb_ref, dya_ref, dyb_ref, dga_ref, dgb_ref):
        i = pl.program_id(0)

        @pl.when(i == 0)
        def _():
            dga_ref[...] = jnp.zeros_like(dga_ref)
            dgb_ref[...] = jnp.zeros_like(dgb_ref)

        for y_ref, g_ref, d_ref, dgr, lo, w in ((ya_ref, ga_ref, dya_ref, dga_ref, 0, WA),
                                                (yb_ref, gb_ref, dyb_ref, dgb_ref, WA, WB)):
            y = y_ref[...]
            d = dy_ref[:, lo:lo + w]
            r = lax.rsqrt(jnp.mean(y * y, axis=-1, keepdims=True) + RMS_EPS)
            n = y * r
            dn = d * g_ref[...]
            d_ref[...] = r * (dn - n * jnp.mean(dn * n, axis=-1, keepdims=True))
            dgr[...] += _fold8(d * n)

    return pl.pallas_call(
        body, name=name, grid=(M // tm,),
        in_specs=[pl.BlockSpec((tm, WA + WB), lambda i: (i, 0)),
                  pl.BlockSpec((tm, WA), lambda i: (i, 0)), pl.BlockSpec((tm, WB), lambda i: (i, 0)),
                  pl.BlockSpec((1, WA), lambda i: (0, 0)), pl.BlockSpec((1, WB), lambda i: (0, 0))],
        out_specs=(pl.BlockSpec((tm, WA), lambda i: (i, 0)), pl.BlockSpec((tm, WB), lambda i: (i, 0)),
                   pl.BlockSpec((8, WA), lambda i: (0, 0)), pl.BlockSpec((8, WB), lambda i: (0, 0))),
        out_shape=(jax.ShapeDtypeStruct((M, WA), F32), jax.ShapeDtypeStruct((M, WB), F32),
                   jax.ShapeDtypeStruct((8, WA), F32), jax.ShapeDtypeStruct((8, WB), F32)),
        compiler_params=_params(("arbitrary",)),
    )(dy, ya, yb, ga, gb)


def loss_head(y, target, *, name, tm=512):
    M, D = y.shape
    tm = _tile(M, tm, 8)

    def body(y_ref, t_ref, dy_ref, l_ref):
        i = pl.program_id(0)

        @pl.when(i == 0)
        def _():
            l_ref[...] = jnp.zeros_like(l_ref)

        e = y_ref[...] - t_ref[...]
        dy_ref[...] = e * (1.0 / D)
        l_ref[...] += _fold8(e * e)

    row = pl.BlockSpec((tm, D), lambda i: (i, 0))
    return pl.pallas_call(
        body, name=name, grid=(M // tm,),
        in_specs=[row, row],
        out_specs=(row, pl.BlockSpec((8, D), lambda i: (0, 0))),
        out_shape=(jax.ShapeDtypeStruct((M, D), F32), jax.ShapeDtypeStruct((8, D), F32)),
        compiler_params=_params(("arbitrary",)),
    )(y, target)


def _lane(shape):
    return lax.broadcasted_iota(jnp.int32, shape, len(shape) - 1)


def _swap(t, half, period):
    first = (_lane(t.shape) % period) < half
    return jnp.where(first, pltpu.roll(t, LANES - half, 1), pltpu.roll(t, half, 1))


def _rope(t, c, s, half, period):
    return t * c + _swap(t, half, period) * s


def _rope_t(g, c, s, half, period):
    return g * c - _swap(g, half, period) * s


def rope_tables(positions, rot_dim, period):
    half = rot_dim // 2
    inv_freq = ROPE_THETA ** (-jnp.arange(0, rot_dim, 2, dtype=F32) / rot_dim)
    ang = positions.astype(F32)[:, None] * inv_freq
    cos, sin = jnp.cos(ang), jnp.sin(ang)
    ones = jnp.ones((positions.shape[0], period - rot_dim), F32)
    c = jnp.concatenate([cos, cos, ones], axis=1)
    s = jnp.concatenate([-sin, sin, 0.0 * ones], axis=1)
    reps = LANES // period
    return jnp.tile(c, (1, reps)), jnp.tile(s, (1, reps))


def _branch_blocks(S):
    out = []
    for window, d in DILATED_BRANCHES:
        assert window // d == BLOCK and S % (d * BLOCK) == 0
        out.append((d, (S // d) // BLOCK))
    return out


def _rows(r, n, d):
    return pl.ds(r + n * (BLOCK * d), BLOCK, stride=d) if d > 1 else pl.ds(pl.multiple_of(n * BLOCK, BLOCK), BLOCK)


def _band_masks(n, strict_prev):
    qi = lax.broadcasted_iota(jnp.int32, (BLOCK, BLOCK), 0)
    kj = lax.broadcasted_iota(jnp.int32, (BLOCK, BLOCK), 1)
    cur = kj <= qi
    prev = ((kj > qi) if strict_prev else (kj >= qi)) & (n > 0)
    return cur, prev


def attn_a_fwd(h, ca, sa, *, n_heads, name):
    S = h.shape[0]
    scale = HEAD_DIM_A ** -0.5
    half = HEAD_DIM_A // 8
    branches = _branch_blocks(S)

    def body(q_ref, k_ref, v_ref, c_ref, s_ref, y_ref, lse_ref, qs, ks, m_s, l_s, acc_s):
        qs[...] = _rope(q_ref[...], c_ref[...], s_ref[...], half, HEAD_DIM_A)
        ks[...] = _rope(k_ref[...], c_ref[...], s_ref[...], half, HEAD_DIM_A)
        for bi, (d, nb) in enumerate(branches):
            def blk(idx, carry, bi=bi, d=d, nb=nb):
                r, n = idx // nb, idx % nb
                rc, rp = _rows(r, n, d), _rows(r, jnp.maximum(n - 1, 0), d)
                q = qs[rc, :].astype(BF16)
                cur, prev = _band_masks(n, False)
                sc = jnp.where(cur, _dot_nt(q, ks[rc, :].astype(BF16)) * scale, NEG_INF)
                sp = jnp.where(prev, _dot_nt(q, ks[rp, :].astype(BF16)) * scale, NEG_INF)
                m = jnp.maximum(jnp.max(sc, axis=-1, keepdims=True), jnp.max(sp, axis=-1, keepdims=True))
                pc, pp = jnp.exp(sc - m), jnp.exp(sp - m)
                l = jnp.sum(pc, axis=-1, keepdims=True) + jnp.sum(pp, axis=-1, keepdims=True)
                acc = _dot(pc.astype(BF16), v_ref[rc, :].astype(BF16)) + _dot(pp.astype(BF16), v_ref[rp, :].astype(BF16))
                mb = jnp.broadcast_to(m, (BLOCK, LANES))
                lb = jnp.broadcast_to(l, (BLOCK, LANES))
                if bi == 0:
                    m_s[rc, :], l_s[rc, :], acc_s[rc, :] = mb, lb, acc
                else:
                    m0 = m_s[rc, :]
                    mn = jnp.maximum(m0, mb)
                    a0, a1 = jnp.exp(m0 - mn), jnp.exp(mb - mn)
                    m_s[rc, :] = mn
                    l_s[rc, :] = l_s[rc, :] * a0 + lb * a1
                    acc_s[rc, :] = acc_s[rc, :] * a0 + acc * a1
                return carry

            lax.fori_loop(0, d * nb, blk, 0)
        y_ref[...] = acc_s[...] / l_s[...]
        lse_ref[...] = m_s[...] + jnp.log(l_s[...])

    col = lambda off: pl.BlockSpec((S, LANES), lambda hd: (0, off + hd))
    full = pl.BlockSpec((S, LANES), lambda hd: (0, 0))
    out = pl.BlockSpec((S, LANES), lambda hd: (0, hd))
    o_shape = jax.ShapeDtypeStruct((S, n_heads * LANES), F32)
    return pl.pallas_call(
        body, name=name, grid=(n_heads,),
        in_specs=[col(0), col(n_heads), col(2 * n_heads), full, full],
        out_specs=(out, out), out_shape=(o_shape, o_shape),
        scratch_shapes=[pltpu.VMEM((S, LANES), F32) for _ in range(5)],
        compiler_params=_params(("arbitrary",)),
    )(h, h, h, ca, sa)


def attn_a_bwd(h, ca, sa, ya, lse, dya, *, n_heads, name):
    S = h.shape[0]
    scale = HEAD_DIM_A ** -0.5
    half = HEAD_DIM_A // 8
    branches = _branch_blocks(S)

    def body(q_ref, k_ref, v_ref, c_ref, s_ref, y_ref, lse_ref, dy_ref, o_ref, qs, ks, dq_s, dk_s, dv_s):
        part = pl.program_id(1)

        @pl.when(part == 0)
        def _():
            qs[...] = _rope(q_ref[...], c_ref[...], s_ref[...], half, HEAD_DIM_A)
            ks[...] = _rope(k_ref[...], c_ref[...], s_ref[...], half, HEAD_DIM_A)
            dq_s[...] = jnp.zeros_like(dq_s)
            dk_s[...] = jnp.zeros_like(dk_s)
            dv_s[...] = jnp.zeros_like(dv_s)
            for d, nb in branches:
                def blk(idx, carry, d=d, nb=nb):
                    r, n = idx // nb, idx % nb
                    rc, rp = _rows(r, n, d), _rows(r, jnp.maximum(n - 1, 0), d)
                    q = qs[rc, :].astype(BF16)
                    dy = dy_ref[rc, :]
                    dsum = jnp.sum(dy * y_ref[rc, :], axis=-1, keepdims=True)
                    dyb = dy.astype(BF16)
                    lse_b = lse_ref[rc, :]
                    cur, prev = _band_masks(n, False)
                    dq = jnp.zeros((BLOCK, LANES), F32)
                    for rows, mask in ((rc, cur), (rp, prev)):
                        kb = ks[rows, :].astype(BF16)
                        vb = v_ref[rows, :].astype(BF16)
                        s = jnp.where(mask, _dot_nt(q, kb) * scale, NEG_INF)
                        p = jnp.exp(s - lse_b)
                        ds = (p * (_dot_nt(dyb, vb) - dsum) * scale).astype(BF16)
                        dv_s[rows, :] += _dot_tn(p.astype(BF16), dyb)
                        dk_s[rows, :] += _dot_tn(ds, q)
                        dq = dq + _dot(ds, kb)
                    dq_s[rc, :] += dq
                    return carry

                lax.fori_loop(0, d * nb, blk, 0)
            o_ref[...] = _rope_t(dq_s[...], c_ref[...], s_ref[...], half, HEAD_DIM_A).astype(o_ref.dtype)

        @pl.when(part == 1)
        def _():
            o_ref[...] = _rope_t(dk_s[...], c_ref[...], s_ref[...], half, HEAD_DIM_A).astype(o_ref.dtype)

        @pl.when(part == 2)
        def _():
            o_ref[...] = dv_s[...].astype(o_ref.dtype)

    col = lambda off: pl.BlockSpec((S, LANES), lambda hd, p: (0, off + hd))
    full = pl.BlockSpec((S, LANES), lambda hd, p: (0, 0))
    per_head = pl.BlockSpec((S, LANES), lambda hd, p: (0, hd))
    return pl.pallas_call(
        body, name=name, grid=(n_heads, 3),
        in_specs=[col(0), col(n_heads), col(2 * n_heads), full, full, per_head, per_head, per_head],
        out_specs=pl.BlockSpec((S, LANES), lambda hd, p: (0, p * n_heads + hd)),
        out_shape=jax.ShapeDtypeStruct((S, 3 * n_heads * LANES), BF16),
        scratch_shapes=[pltpu.VMEM((S, LANES), F32) for _ in range(5)],
        compiler_params=_params(("arbitrary", "arbitrary")),
    )(h, h, h, ca, sa, ya, lse, dya)


def _both_halves(t, g):
    low = _lane(t.shape) < HEAD_DIM_B
    return jnp.where(low == (g == 0), t, pltpu.roll(t, HEAD_DIM_B, 1))


def _stack_heads(t):
    low = _lane(t.shape) < HEAD_DIM_B
    return jnp.concatenate([jnp.where(low, t, 0.0), jnp.where(low, 0.0, t)], axis=0)


def _unstack_heads(t2):
    low = _lane((BLOCK, LANES)) < HEAD_DIM_B
    return jnp.where(low, t2[:BLOCK], t2[BLOCK:])


def _fold_halves(t):
    return t + pltpu.roll(t, HEAD_DIM_B, 1)


def _head_rows(ref, tile):
    a = ref[pl.ds(2 * tile, 1), :][:, :1]
    b = ref[pl.ds(2 * tile + 1, 1), :][:, :1]
    return jnp.concatenate([jnp.broadcast_to(a, (BLOCK, 1)), jnp.broadcast_to(b, (BLOCK, 1))], axis=0)


def attn_b_fwd(h, cb, sb, sinks_l, *, off_q, n_qtiles, name):
    S = h.shape[0]
    nblk = S // BLOCK
    scale = HEAD_DIM_B ** -0.5
    half = HEAD_DIM_B // 8
    tiles_per_group = n_qtiles // 2

    def body(q_ref, k_ref, v_ref, c_ref, s_ref, sink_ref, y_ref, lse_ref, qs, kg, vg):
        t = pl.program_id(0)
        g = t // tiles_per_group
        qs[...] = _rope(q_ref[...], c_ref[...], s_ref[...], half, HEAD_DIM_B)
        kg[...] = _both_halves(_rope(k_ref[...], c_ref[...], s_ref[...], half, HEAD_DIM_B), g)
        vg[...] = _both_halves(v_ref[...], g)
        sink = _head_rows(sink_ref, t)

        def blk(n, carry):
            rc = pl.ds(pl.multiple_of(n * BLOCK, BLOCK), BLOCK)
            rp = pl.ds(pl.multiple_of(jnp.maximum(n - 1, 0) * BLOCK, BLOCK), BLOCK)
            q2 = _stack_heads(qs[rc, :]).astype(BF16)
            cur, prev = _band_masks(n, True)
            cur2, prev2 = jnp.concatenate([cur, cur], 0), jnp.concatenate([prev, prev], 0)
            sc = jnp.where(cur2, _dot_nt(q2, kg[rc, :].astype(BF16)) * scale, NEG_INF)
            sp = jnp.where(prev2, _dot_nt(q2, kg[rp, :].astype(BF16)) * scale, NEG_INF)
            m = jnp.maximum(jnp.max(sc, axis=-1, keepdims=True), jnp.max(sp, axis=-1, keepdims=True))
            pc, pp = jnp.exp(sc - m), jnp.exp(sp - m)
            l = jnp.sum(pc, axis=-1, keepdims=True) + jnp.sum(pp, axis=-1, keepdims=True)
            acc = _dot(pc.astype(BF16), vg[rc, :].astype(BF16)) + _dot(pp.astype(BF16), vg[rp, :].astype(BF16))
            m2 = jnp.maximum(m, sink)
            c = jnp.exp(m - m2)
            den = l * c + jnp.exp(sink - m2)
            y_ref[rc, :] = _unstack_heads(acc * (c / den))
            lse_ref[rc, :] = _unstack_heads(jnp.broadcast_to(m2 + jnp.log(den), (2 * BLOCK, LANES)))
            return carry

        lax.fori_loop(0, nblk, blk, 0)

    full = lambda col: pl.BlockSpec((S, LANES), lambda t: (0, col))
    out = pl.BlockSpec((S, LANES), lambda t: (0, t))
    o_shape = jax.ShapeDtypeStruct((S, n_qtiles * LANES), F32)
    return pl.pallas_call(
        body, name=name, grid=(n_qtiles,),
        in_specs=[pl.BlockSpec((S, LANES), lambda t: (0, off_q + t)), full(off_q + n_qtiles), full(off_q + n_qtiles + 1),
                  full(0), full(0), pl.BlockSpec(sinks_l.shape, lambda t: (0, 0))],
        out_specs=(out, out), out_shape=(o_shape, o_shape),
        scratch_shapes=[pltpu.VMEM((S, LANES), F32) for _ in range(3)],
        compiler_params=_params(("arbitrary",)),
    )(h, h, h, cb, sb, sinks_l)


def attn_b_bwd(h, cb, sb, sinks_l, yb, lse, dyb, *, off_q, n_qtiles, name):
    S = h.shape[0]
    nblk = S // BLOCK
    scale = HEAD_DIM_B ** -0.5
    half = HEAD_DIM_B // 8
    tiles_per_group = n_qtiles // 2
    n_steps = n_qtiles + 2

    def body(q_ref, k_ref, v_ref, c_ref, s_ref, sink_ref, y_ref, lse_ref, dy_ref,
             dq_ref, dkv_ref, dsink_ref, qs, kg, vg, dk_acc, dv_acc):
        t = pl.program_id(0)

        @pl.when(t == 0)
        def _():
            dk_acc[...] = jnp.zeros_like(dk_acc)
            dv_acc[...] = jnp.zeros_like(dv_acc)
            dsink_ref[...] = jnp.zeros_like(dsink_ref)

        @pl.when(t < n_qtiles)
        def _():
            g = t // tiles_per_group
            qs[...] = _rope(q_ref[...], c_ref[...], s_ref[...], half, HEAD_DIM_B)
            kg[...] = _both_halves(_rope(k_ref[...], c_ref[...], s_ref[...], half, HEAD_DIM_B), g)
            vg[...] = _both_halves(v_ref[...], g)
            sink = _head_rows(sink_ref, t)

            def blk(n, dsink):
                rc = pl.ds(pl.multiple_of(n * BLOCK, BLOCK), BLOCK)
                rp = pl.ds(pl.multiple_of(jnp.maximum(n - 1, 0) * BLOCK, BLOCK), BLOCK)
                q2 = _stack_heads(qs[rc, :]).astype(BF16)
                dy2 = _stack_heads(dy_ref[rc, :])
                dsum = jnp.sum(dy2 * _stack_heads(y_ref[rc, :]), axis=-1, keepdims=True)
                dy2b = dy2.astype(BF16)
                lse_t = lse_ref[rc, :]
                lse2 = jnp.concatenate([lse_t[:, :1], lse_t[:, HEAD_DIM_B:HEAD_DIM_B + 1]], axis=0)
                cur, prev = _band_masks(n, True)
                dq2 = jnp.zeros((2 * BLOCK, LANES), F32)
                for rows, mask in ((rc, cur), (rp, prev)):
                    kb = kg[rows, :].astype(BF16)
                    vb = vg[rows, :].astype(BF16)
                    mask2 = jnp.concatenate([mask, mask], 0)
                    s = jnp.where(mask2, _dot_nt(q2, kb) * scale, NEG_INF)
                    p = jnp.exp(s - lse2)
                    ds = (p * (_dot_nt(dy2b, vb) - dsum) * scale).astype(BF16)
                    dv_acc[g, rows, :] += _fold_halves(_dot_tn(p.astype(BF16), dy2b))
                    dk_acc[g, rows, :] += _fold_halves(_dot_tn(ds, q2))
                    dq2 = dq2 + _dot(ds, kb)
                dq_ref[rc, :] = _unstack_heads(dq2)
                return dsink - jnp.exp(sink - lse2) * dsum

            dsink = lax.fori_loop(0, nblk, blk, jnp.zeros((2 * BLOCK, 1), F32))
            dq_ref[...] = _rope_t(dq_ref[...], c_ref[...], s_ref[...], half, HEAD_DIM_B)
            d0 = jnp.sum(dsink[:BLOCK], axis=0, keepdims=True)
            d1 = jnp.sum(dsink[BLOCK:], axis=0, keepdims=True)
            dsink_ref[pl.ds(2 * t, 1), :] = jnp.broadcast_to(d0, (1, LANES))
            dsink_ref[pl.ds(2 * t + 1, 1), :] = jnp.broadcast_to(d1, (1, LANES))

        low = _lane((S, LANES)) < HEAD_DIM_B

        @pl.when(t == n_qtiles)
        def _():
            dk = jnp.where(low, dk_acc[0], dk_acc[1])
            dkv_ref[...] = _rope_t(dk, c_ref[...], s_ref[...], half, HEAD_DIM_B)

        @pl.when(t == n_qtiles + 1)
        def _():
            dkv_ref[...] = jnp.where(low, dv_acc[0], dv_acc[1])

    qt = lambda t: jnp.minimum(t, n_qtiles - 1)
    full = lambda col: pl.BlockSpec((S, LANES), lambda t: (0, col))
    per_tile = pl.BlockSpec((S, LANES), lambda t: (0, qt(t)))
    return pl.pallas_call(
        body, name=name, grid=(n_steps,),
        in_specs=[pl.BlockSpec((S, LANES), lambda t: (0, off_q + qt(t))), full(off_q + n_qtiles),
                  full(off_q + n_qtiles + 1), full(0), full(0), pl.BlockSpec(sinks_l.shape, lambda t: (0, 0)),
                  per_tile, per_tile, per_tile],
        out_specs=(per_tile, pl.BlockSpec((S, LANES), lambda t: (0, jnp.maximum(t - n_qtiles, 0))),
                   pl.BlockSpec(sinks_l.shape, lambda t: (0, 0))),
        out_shape=(jax.ShapeDtypeStruct((S, n_qtiles * LANES), F32), jax.ShapeDtypeStruct((S, 2 * LANES), F32),
                   jax.ShapeDtypeStruct(sinks_l.shape, F32)),
        scratch_shapes=[pltpu.VMEM((S, LANES), F32) for _ in range(3)]
        + [pltpu.VMEM((2, S, LANES), F32), pltpu.VMEM((2, S, LANES), F32)],
        compiler_params=_params(("arbitrary",)),
    )(h, h, h, cb, sb, sinks_l, yb, lse, dyb)


def mem_attn_fwd(q, kv, *, name, tm=512):
    S, D = q.shape
    n_mem = kv.shape[0]
    hd = D // N_MEM_HEADS
    scale = hd ** -0.5
    tm = _tile(S, tm, 8)

    def body(q_ref, kv_ref, o_ref):
        for hh in range(N_MEM_HEADS):
            cols = slice(hh * hd, (hh + 1) * hd)
            s = _dot_nt(q_ref[:, cols], kv_ref[:, cols]) * scale
            s = s - jnp.max(s, axis=-1, keepdims=True)
            e = jnp.exp(s)
            p = e / jnp.sum(e, axis=-1, keepdims=True)
            o_ref[:, cols] = _dot(p.astype(BF16), kv_ref[:, D + hh * hd:D + (hh + 1) * hd]).astype(o_ref.dtype)

    return pl.pallas_call(
        body, name=name, grid=(S // tm,),
        in_specs=[pl.BlockSpec((tm, D), lambda i: (i, 0)), pl.BlockSpec((n_mem, 2 * D), lambda i: (0, 0))],
        out_specs=pl.BlockSpec((tm, D), lambda i: (i, 0)),
        out_shape=jax.ShapeDtypeStruct((S, D), BF16),
        compiler_params=_params(("parallel",)),
    )(q, kv)


def mem_attn_bwd(q, kv, do, *, name, tm=512):
    S, D = q.shape
    n_mem = kv.shape[0]
    hd = D // N_MEM_HEADS
    scale = hd ** -0.5
    tm = _tile(S, tm, 8)

    def body(q_ref, kv_ref, do_ref, dq_ref, dkv_ref):
        i = pl.program_id(0)

        @pl.when(i == 0)
        def _():
            dkv_ref[...] = jnp.zeros_like(dkv_ref)

        for hh in range(N_MEM_HEADS):
            cols = slice(hh * hd, (hh + 1) * hd)
            vcols = slice(D + hh * hd, D + (hh + 1) * hd)
            qh, kh, vh, doh = q_ref[:, cols], kv_ref[:, cols], kv_ref[:, vcols], do_ref[:, cols]
            s = _dot_nt(qh, kh) * scale
            s = s - jnp.max(s, axis=-1, keepdims=True)
            e = jnp.exp(s)
            p = e / jnp.sum(e, axis=-1, keepdims=True)
            dp = _dot_nt(doh, vh)
            ds = (p * (dp - jnp.sum(dp * p, axis=-1, keepdims=True)) * scale).astype(BF16)
            dq_ref[:, cols] = _dot(ds, kh).astype(dq_ref.dtype)
            dkv_ref[:, cols] += _dot_tn(ds, qh)
            dkv_ref[:, vcols] += _dot_tn(p.astype(BF16), doh)

    row = pl.BlockSpec((tm, D), lambda i: (i, 0))
    kvs = pl.BlockSpec((n_mem, 2 * D), lambda i: (0, 0))
    return pl.pallas_call(
        body, name=name, grid=(S // tm,),
        in_specs=[row, kvs, row], out_specs=(row, kvs),
        out_shape=(jax.ShapeDtypeStruct((S, D), BF16), jax.ShapeDtypeStruct((n_mem, 2 * D), F32)),
        compiler_params=_params(("arbitrary",)),
    )(q, kv, do)


def _rows_view(t):
    return t.reshape(-1, t.shape[-1])


def _row_tile(rows, cols, target_elems=512 * 1024):
    return _tile(rows, max(8, target_elems // cols), 8)


def cast_bf16(w, *, name):
    v = _rows_view(w)
    R, C = v.shape
    tr = _row_tile(R, C)

    def body(w_ref, o_ref):
        o_ref[...] = w_ref[...].astype(BF16)

    spec = pl.BlockSpec((tr, C), lambda i: (i, 0))
    out = pl.pallas_call(body, name=name, grid=(R // tr,), in_specs=[spec], out_specs=spec,
                         out_shape=jax.ShapeDtypeStruct((R, C), BF16), compiler_params=_params(("parallel",)))(v)
    return out.reshape(w.shape)


def pair_sum(p, r1, core, *, name):
    nsh, r, c = p.shape
    hr = r // 2
    tr = _row_tile(hr, c)
    nt = hr // tr

    def body(core_ref, p_ref, r_ref, o_ref):
        o_ref[...] = (p_ref[...].astype(F32) + r_ref[...].astype(F32)).astype(BF16)

    return pl.pallas_call(
        body, name=name,
        grid_spec=pltpu.PrefetchScalarGridSpec(
            num_scalar_prefetch=1, grid=(nsh, nt),
            in_specs=[pl.BlockSpec((None, tr, c), lambda s, i, core_ref: (s, core_ref[0] * nt + i, 0)),
                      pl.BlockSpec((None, tr, c), lambda s, i, core_ref: (s, i, 0))],
            out_specs=pl.BlockSpec((None, tr, c), lambda s, i, core_ref: (s, i, 0))),
        out_shape=jax.ShapeDtypeStruct((nsh, hr, c), BF16),
        compiler_params=_params(("parallel", "parallel")),
    )(core, p, r1)


def chip_sum(r2, *, name):
    nsh, hr, c = r2.shape
    tr = _row_tile(hr, c, 256 * 1024)

    def body(r_ref, o_ref):
        acc = r_ref[0].astype(F32)
        for k in range(1, nsh):
            acc = acc + r_ref[k].astype(F32)
        o_ref[...] = acc

    return pl.pallas_call(
        body, name=name, grid=(hr // tr,),
        in_specs=[pl.BlockSpec((nsh, tr, c), lambda i: (0, i, 0))],
        out_specs=pl.BlockSpec((tr, c), lambda i: (i, 0)),
        out_shape=jax.ShapeDtypeStruct((hr, c), F32),
        compiler_params=_params(("parallel",)),
    )(r2)


def adamw(w, g, m, v, *, name):
    shape = w.shape
    wv, gv, mv, vv = (_rows_view(t) for t in (w, g, m, v))
    R, C = wv.shape
    tr = _row_tile(R, C, 256 * 1024)
    c1 = 1.0 / (1.0 - ADAM_B1 ** ADAM_STEP)
    c2 = 1.0 / (1.0 - ADAM_B2 ** ADAM_STEP)

    def body(w_ref, g_ref, m_ref, v_ref, d_ref, nm_ref, nv_ref):
        g_ = g_ref[...]
        nm = ADAM_B1 * m_ref[...] + (1.0 - ADAM_B1) * g_
        nv = ADAM_B2 * v_ref[...] + (1.0 - ADAM_B2) * (g_ * g_)
        m_hat = nm * c1
        v_hat = nv * c2
        d_ref[...] = -ADAM_LR * (m_hat / (jnp.sqrt(v_hat) + ADAM_EPS) + ADAM_WD * w_ref[...])
        nm_ref[...] = nm
        nv_ref[...] = nv

    spec = pl.BlockSpec((tr, C), lambda i: (i, 0))
    o = jax.ShapeDtypeStruct((R, C), F32)
    outs = pl.pallas_call(body, name=name, grid=(R // tr,), in_specs=[spec] * 4, out_specs=(spec,) * 3,
                          out_shape=(o, o, o), compiler_params=_params(("parallel",)))(wv, gv, mv, vv)
    return tuple(t.reshape(shape) for t in outs)


def _place():
    x, y, c = lax.axis_index("x"), lax.axis_index("y"), lax.axis_index("c")
    others = [(1 - x, y), (x, 1 - y), (1 - x, 1 - y)]
    return x, y, c, others


def _any_specs(n):
    return [pl.BlockSpec(memory_space=pl.ANY) for _ in range(n)]


def gather_weights(shards, *, name):
    n = len(shards)

    def body(*refs):
        src, dst = refs[:n], refs[n:2 * n]
        send, recv, fsend, frecv, lsem = refs[2 * n:]
        x, y, c, others = _place()
        me = 2 * x + y
        locals_, sends, fwds = [], [], []
        for t in range(n):
            hr = src[t].shape[0] // 2
            mine = pl.ds(c * hr, hr)
            lc = pltpu.make_async_copy(src[t], dst[t].at[me], lsem.at[t])
            lc.start()
            locals_.append(lc)
            for j, (px, py) in enumerate(others):
                cp = pltpu.make_async_remote_copy(src[t].at[mine], dst[t].at[me, mine], send.at[t, j], recv.at[t, j],
                                                  device_id=(px, py, c), device_id_type=MESH)
                cp.start()
                sends.append(cp)
        for t in range(n):
            hr = src[t].shape[0] // 2
            mine = pl.ds(c * hr, hr)
            for j, (px, py) in enumerate(others):
                k = 2 * px + py
                pltpu.make_async_remote_copy(src[t].at[mine], dst[t].at[k, mine], send.at[t, j], recv.at[t, j],
                                             device_id=(px, py, c), device_id_type=MESH).wait_recv()
                fw = pltpu.make_async_remote_copy(dst[t].at[k, mine], dst[t].at[k, mine], fsend.at[t, j], frecv.at[t, j],
                                                  device_id=(x, y, 1 - c), device_id_type=MESH)
                fw.start()
                fwds.append(fw)
        for t in range(n):
            hr = src[t].shape[0] // 2
            theirs = pl.ds((1 - c) * hr, hr)
            for j, (px, py) in enumerate(others):
                k = 2 * px + py
                pltpu.make_async_remote_copy(dst[t].at[k, theirs], dst[t].at[k, theirs], fsend.at[t, j], frecv.at[t, j],
                                             device_id=(x, y, 1 - c), device_id_type=MESH).wait_recv()
        for cp in sends + fwds:
            cp.wait_send()
        for lc in locals_:
            lc.wait()

    return pl.pallas_call(
        body, name=name,
        in_specs=_any_specs(n), out_specs=_any_specs(n),
        out_shape=[jax.ShapeDtypeStruct((N_CHIPS,) + s.shape, s.dtype) for s in shards],
        scratch_shapes=[pltpu.SemaphoreType.DMA((n, 3)), pltpu.SemaphoreType.DMA((n, 3)),
                        pltpu.SemaphoreType.DMA((n, 3)), pltpu.SemaphoreType.DMA((n, 3)),
                        pltpu.SemaphoreType.DMA((n,))],
        compiler_params=pltpu.CompilerParams(has_side_effects=True),
    )(*shards)


def sibling_halves(parts, *, name):
    n = len(parts)

    def body(*refs):
        src, dst = refs[:n], refs[n:2 * n]
        send, recv = refs[2 * n:]
        x, y, c, _ = _place()
        cps = []
        for t in range(n):
            hr = src[t].shape[1] // 2
            cp = pltpu.make_async_remote_copy(src[t].at[:, pl.ds((1 - c) * hr, hr)], dst[t], send.at[t], recv.at[t],
                                              device_id=(x, y, 1 - c), device_id_type=MESH)
            cp.start()
            cps.append(cp)
        for cp in cps:
            cp.wait()

    return pl.pallas_call(
        body, name=name,
        in_specs=_any_specs(n), out_specs=_any_specs(n),
        out_shape=[jax.ShapeDtypeStruct((p.shape[0], p.shape[1] // 2, p.shape[2]), p.dtype) for p in parts],
        scratch_shapes=[pltpu.SemaphoreType.DMA((n,)), pltpu.SemaphoreType.DMA((n,))],
        compiler_params=pltpu.CompilerParams(has_side_effects=True),
    )(*parts)


def scatter_to_chips(qs, *, name):
    n = len(qs)

    def body(*refs):
        src, dst = refs[:n], refs[n:2 * n]
        send, recv, lsem = refs[2 * n:]
        x, y, c, others = _place()
        me = 2 * x + y
        cps, locals_ = [], []
        for t in range(n):
            lc = pltpu.make_async_copy(src[t].at[me], dst[t].at[me], lsem.at[t])
            lc.start()
            locals_.append(lc)
            for j, (px, py) in enumerate(others):
                cp = pltpu.make_async_remote_copy(src[t].at[2 * px + py], dst[t].at[me], send.at[t, j], recv.at[t, j],
                                                  device_id=(px, py, c), device_id_type=MESH)
                cp.start()
                cps.append(cp)
        for t in range(n):
            for j, (px, py) in enumerate(others):
                k = 2 * px + py
                pltpu.make_async_remote_copy(src[t].at[k], dst[t].at[k], send.at[t, j], recv.at[t, j],
                                             device_id=(px, py, c), device_id_type=MESH).wait_recv()
        for cp in cps:
            cp.wait_send()
        for lc in locals_:
            lc.wait()

    return pl.pallas_call(
        body, name=name,
        in_specs=_any_specs(n), out_specs=_any_specs(n),
        out_shape=[jax.ShapeDtypeStruct(q.shape, q.dtype) for q in qs],
        scratch_shapes=[pltpu.SemaphoreType.DMA((n, 3)), pltpu.SemaphoreType.DMA((n, 3)),
                        pltpu.SemaphoreType.DMA((n,))],
        compiler_params=pltpu.CompilerParams(has_side_effects=True),
    )(*qs)


def join_halves(halves, *, name):
    n = len(halves)

    def body(*refs):
        src, dst = refs[:n], refs[n:2 * n]
        send, recv, lsem = refs[2 * n:]
        x, y, c, _ = _place()
        cps, locals_ = [], []
        for t in range(n):
            hr = src[t].shape[0]
            mine = pl.ds(c * hr, hr)
            lc = pltpu.make_async_copy(src[t], dst[t].at[mine], lsem.at[t])
            lc.start()
            locals_.append(lc)
            cp = pltpu.make_async_remote_copy(src[t], dst[t].at[mine], send.at[t], recv.at[t],
                                              device_id=(x, y, 1 - c), device_id_type=MESH)
            cp.start()
            cps.append(cp)
        for t in range(n):
            hr = src[t].shape[0]
            theirs = pl.ds((1 - c) * hr, hr)
            pltpu.make_async_remote_copy(src[t], dst[t].at[theirs], send.at[t], recv.at[t],
                                         device_id=(x, y, 1 - c), device_id_type=MESH).wait_recv()
        for cp in cps:
            cp.wait_send()
        for lc in locals_:
            lc.wait()

    return pl.pallas_call(
        body, name=name,
        in_specs=_any_specs(n), out_specs=_any_specs(n),
        out_shape=[jax.ShapeDtypeStruct((2 * s.shape[0], s.shape[1]), s.dtype) for s in halves],
        scratch_shapes=[pltpu.SemaphoreType.DMA((n,)), pltpu.SemaphoreType.DMA((n,)), pltpu.SemaphoreType.DMA((n,))],
        compiler_params=pltpu.CompilerParams(has_side_effects=True),
    )(*halves)


def allreduce_small(t, *, name):
    R, C = t.shape

    def body(t_ref, o_ref, land, send, recv):
        x, y, c, _ = _place()
        me = 4 * x + 2 * y + c
        land[me] = t_ref[...]
        cps = []
        for j in range(1, 8):
            px, py, pc = (x + (j >> 2)) % 2, (y + ((j >> 1) & 1)) % 2, (c + (j & 1)) % 2
            cp = pltpu.make_async_remote_copy(t_ref, land.at[me], send.at[j - 1], recv.at[j - 1],
                                              device_id=(px, py, pc), device_id_type=MESH)
            cp.start()
            cps.append(cp)
        for j in range(1, 8):
            px, py, pc = (x + (j >> 2)) % 2, (y + ((j >> 1) & 1)) % 2, (c + (j & 1)) % 2
            pltpu.make_async_remote_copy(t_ref, land.at[4 * px + 2 * py + pc], send.at[j - 1], recv.at[j - 1],
                                         device_id=(px, py, pc), device_id_type=MESH).wait_recv()
        for cp in cps:
            cp.wait_send()
        acc = land[0]
        for k in range(1, 8):
            acc = acc + land[k]
        o_ref[...] = acc

    return pl.pallas_call(
        body, name=name,
        in_specs=[pl.BlockSpec(memory_space=pltpu.VMEM)], out_specs=pl.BlockSpec(memory_space=pltpu.VMEM),
        out_shape=jax.ShapeDtypeStruct((R, C), F32),
        scratch_shapes=[pltpu.VMEM((8, R, C), F32), pltpu.SemaphoreType.DMA((7,)), pltpu.SemaphoreType.DMA((7,))],
        compiler_params=pltpu.CompilerParams(has_side_effects=True),
    )(t)


def _layer_fwd(x, xb, memb, W, P, tabs, alpha, li):
    ca, sa, cb, sb = tabs
    nA = P["gn_a"].shape[1] // HEAD_DIM_A
    nQ = P["gn_b"].shape[1] // LANES
    nm = lambda s: f"L{li}_{s}"
    h = mm_nn(xb, W["w_in"], name=nm("h"), out_dtype=F32, tn=2304)
    ya, lse_a = attn_a_fwd(h, ca, sa, n_heads=nA, name=nm("attn_a"))
    yb, lse_b = attn_b_fwd(h, cb, sb, P["sinks_l"], off_q=3 * nA, n_qtiles=nQ, name=nm("attn_b"))
    ymix = rms_fwd(ya, yb, P["gn_a"], P["gn_b"], name=nm("rms"))
    z1, x1, x1b = mm_ln(ymix, W["w_out"][0], x, P["ln_mix_g"], P["ln_mix_b"], name=nm("out_ln"), alpha=alpha)
    qm = mm_nn(x1b, W["w_mq"], name=nm("mq"), out_dtype=BF16)
    kv = mm_nn(memb, W["w_mkv"], name=nm("mkv"), out_dtype=BF16, tm=256)
    o = mem_attn_fwd(qm, kv, name=nm("mem_attn"))
    z2, x2, x2b = mm_ln(o, W["w_mo"][0], x1, P["ln_mem_g"], P["ln_mem_b"], name=nm("mo_ln"), alpha=alpha)
    u, a = mm_nn(x2b, W["w_up"], name=nm("up"), out_dtype=BF16, relu2=True)
    z3, x3, x3b = mm_ln(a, W["w_down"][0], x2, P["ln_ff_g"], P["ln_ff_b"], name=nm("down_ln"), alpha=alpha)
    saved = dict(xb=xb, h=h, ya=ya, lse_a=lse_a, yb=yb, lse_b=lse_b, ymix=ymix, z1=z1, x1b=x1b, qm=qm, kv=kv, o=o,
                 z2=z2, x2b=x2b, u=u, a=a, z3=z3)
    return x3, x3b, saved


def _layer_bwd(dx3, sv, memb, W, P, tabs, alpha, li):
    ca, sa, cb, sb = tabs
    nA = P["gn_a"].shape[1] // HEAD_DIM_A
    nQ = P["gn_b"].shape[1] // LANES
    nm = lambda s: f"L{li}_b_{s}"
    nsh = lambda k: W[k].shape[0]
    gw, gs = {}, {}
    dz3, dz3b, gs["ln_ff_g"], gs["ln_ff_b"] = ln_bwd(dx3, sv["z3"], P["ln_ff_g"], name=nm("ln_ff"))
    gw["w_down"] = mm_tn(sv["a"], dz3b, nsh("w_down"), name=nm("dw_down"))
    du = mm_nt(dz3b, W["w_down"], name=nm("du"), out_dtype=BF16, umul=sv["u"])
    gw["w_up"] = mm_tn(sv["x2b"], du, nsh("w_up"), name=nm("dw_up"))
    dx2 = mm_nt(du, W["w_up"], name=nm("dx2"), out_dtype=F32, resid=dz3, alpha=alpha)
    dz2, dz2b, gs["ln_mem_g"], gs["ln_mem_b"] = ln_bwd(dx2, sv["z2"], P["ln_mem_g"], name=nm("ln_mem"))
    gw["w_mo"] = mm_tn(sv["o"], dz2b, nsh("w_mo"), name=nm("dw_mo"))
    do = mm_nt(dz2b, W["w_mo"], name=nm("do"), out_dtype=BF16)
    dqm, dkv = mem_attn_bwd(sv["qm"], sv["kv"], do, name=nm("mem_attn"))
    gw["w_mq"] = mm_tn(sv["x1b"], dqm, nsh("w_mq"), name=nm("dw_mq"))
    gw["w_mkv"] = mm_tn(memb, cast_bf16(dkv, name=nm("dkv_cast")), nsh("w_mkv"), name=nm("dw_mkv"), tm=256)
    dx1 = mm_nt(dqm, W["w_mq"], name=nm("dx1"), out_dtype=F32, resid=dz2, alpha=alpha)
    dz1, dz1b, gs["ln_mix_g"], gs["ln_mix_b"] = ln_bwd(dx1, sv["z1"], P["ln_mix_g"], name=nm("ln_mix"))
    gw["w_out"] = mm_tn(sv["ymix"], dz1b, nsh("w_out"), name=nm("dw_out"))
    dymix = mm_nt(dz1b, W["w_out"], name=nm("dymix"), out_dtype=F32)
    dya, dyb, gs["gn_a"], gs["gn_b"] = rms_bwd(dymix, sv["ya"], sv["yb"], P["gn_a"], P["gn_b"], name=nm("rms"))
    dha = attn_a_bwd(sv["h"], ca, sa, sv["ya"], sv["lse_a"], dya, n_heads=nA, name=nm("attn_a"))
    dqb, dkvb, gs["sinks"] = attn_b_bwd(sv["h"], cb, sb, P["sinks_l"], sv["yb"], sv["lse_b"], dyb,
                                        off_q=3 * nA, n_qtiles=nQ, name=nm("attn_b"))
    dh = jnp.concatenate([dha, dqb.astype(BF16), dkvb.astype(BF16)], axis=1)
    gw["w_in"] = mm_tn(sv["xb"], dh, nsh("w_in"), name=nm("dw_in"), tn=2304)
    dx0 = mm_nt(dh, W["w_in"], name=nm("dx0"), out_dtype=F32, resid=dz1, alpha=alpha, tr=2304)
    return dx0, gw, gs


def _gathered_view(name, g):
    if name == "w_in":
        return jnp.concatenate([g[k] for k in range(N_CHIPS)], axis=1)[None]
    if name in COL_SHARDED:
        return g
    return g.reshape(1, g.shape[0] * g.shape[1], g.shape[2])


def _to_shards(name, gw):
    if name == "w_in":
        n = gw.shape[2] // N_CHIPS
        return jnp.stack([gw[0, :, k * n:(k + 1) * n] for k in range(N_CHIPS)])
    if name in COL_SHARDED:
        return gw
    return gw.reshape(N_CHIPS, gw.shape[1] // N_CHIPS, gw.shape[2])


def _step(x, mem, positions, loss_target, w, m, v):
    S, D = x.shape[1], x.shape[2]
    depth = w["w_in"].shape[0]
    alpha = (2 * depth) ** 0.25
    x0 = x[0]
    memb = cast_bf16(mem[0], name="mem_cast")
    pos = positions[0]
    tabs = rope_tables(pos, HEAD_DIM_A // 4, HEAD_DIM_A) + rope_tables(pos, HEAD_DIM_B // 4, HEAD_DIM_B)
    core = lax.axis_index("c").astype(jnp.int32).reshape(1)

    Ws = []
    for li in range(depth):
        shards = [cast_bf16(w[k][li], name=f"L{li}_cast_{k}") for k in BIG]
        g = gather_weights(shards, name=f"L{li}_gather")
        Ws.append({k: _gathered_view(k, gk) for k, gk in zip(BIG, g)})

    def small(li):
        P = {k: w[k][li][None] for k in ("gn_a", "gn_b", "ln_mix_g", "ln_mix_b", "ln_mem_g", "ln_mem_b", "ln_ff_g", "ln_ff_b")}
        P["sinks_l"] = jnp.broadcast_to(w["sinks"][li][:, None], (w["sinks"].shape[1], LANES))
        return P

    xs, xbs, saved = x0, cast_bf16(x0, name="x_cast"), []
    for li in range(depth):
        xs, xbs, sv = _layer_fwd(xs, xbs, memb, Ws[li], small(li), tabs, alpha, li)
        saved.append(sv)
    dy, loss_part = loss_head(xs, loss_target[0], name="loss")
    loss = lax.psum(0.5 / D * jnp.sum(loss_part), ("x", "y", "c"))

    g_big = {k: [None] * depth for k in BIG}
    g_small = [None] * depth
    for li in reversed(range(depth)):
        dy, gw, gs = _layer_bwd(dy, saved[li], memb, Ws[li], small(li), tabs, alpha, li)
        parts = [_to_shards(k, gw[k]) for k in BIG]
        r1 = sibling_halves(parts, name=f"L{li}_rs_sibling")
        qs = [pair_sum(p, r, core, name=f"L{li}_rs_pair_{k}") for k, p, r in zip(BIG, parts, r1)]
        r2 = scatter_to_chips(qs, name=f"L{li}_rs_chips")
        halves = [chip_sum(r, name=f"L{li}_rs_sum_{k}") for k, r in zip(BIG, r2)]
        full = join_halves(halves, name=f"L{li}_rs_join")
        for k, f in zip(BIG, full):
            g_big[k][li] = f
        g_small[li] = gs
    grad_x = dy[None]

    rows = []
    for li in range(depth):
        gs = g_small[li]
        for k in ("ln_mix_g", "ln_mix_b", "ln_mem_g", "ln_mem_b", "ln_ff_g", "ln_ff_b"):
            rows.append(jnp.sum(gs[k], axis=0, keepdims=True))
        rows.append(jnp.concatenate([jnp.sum(gs["gn_a"], axis=0, keepdims=True), jnp.sum(gs["gn_b"], axis=0, keepdims=True)], axis=1))
        sk = gs["sinks"][:, 0][None]
        rows.append(jnp.pad(sk, ((0, 0), (0, D - sk.shape[1]))))
    red = allreduce_small(jnp.concatenate(rows, axis=0), name="small_allreduce").reshape(depth, 8, D)
    wa = w["gn_a"].shape[1]
    grads = {k: jnp.stack(g_big[k]) for k in BIG}
    for j, k in enumerate(("ln_mix_g", "ln_mix_b", "ln_mem_g", "ln_mem_b", "ln_ff_g", "ln_ff_b")):
        grads[k] = red[:, j]
    grads["gn_a"] = red[:, 6, :wa]
    grads["gn_b"] = red[:, 6, wa:]
    grads["sinks"] = red[:, 7, :w["sinks"].shape[1]]

    delta, new_m, new_v = {}, {}, {}
    small_names = [k for k in w if k not in BIG]
    for k in BIG:
        delta[k], new_m[k], new_v[k] = adamw(w[k], grads[k], m[k], v[k], name=f"adamw_{k}")
    pack = lambda d: jnp.concatenate([jnp.pad(d[k], ((0, 0), (0, D - d[k].shape[1]))) for k in small_names], axis=0)
    ds, ms, vs = adamw(pack(w), pack(grads), pack(m), pack(v), name="adamw_small")
    for j, k in enumerate(small_names):
        sl = (slice(j * depth, (j + 1) * depth), slice(0, w[k].shape[1]))
        delta[k], new_m[k], new_v[k] = ds[sl], ms[sl], vs[sl]
    return loss, grad_x, grads, delta, new_m, new_v


WEIGHTS = ("w_in", "gn_a", "gn_b", "sinks", "w_out", "ln_mix_g", "ln_mix_b", "w_mq", "w_mkv", "w_mo",
           "ln_mem_g", "ln_mem_b", "w_up", "w_down", "ln_ff_g", "ln_ff_b")


def kernel(x, mem, positions, w_in, gn_a, gn_b, sinks, w_out, ln_mix_g, ln_mix_b, w_mq, w_mkv, w_mo, ln_mem_g, ln_mem_b, w_up, w_down, ln_ff_g, ln_ff_b, loss_target, m_w_in, m_gn_a, m_gn_b, m_sinks, m_w_out, m_ln_mix_g, m_ln_mix_b, m_w_mq, m_w_mkv, m_w_mo, m_ln_mem_g, m_ln_mem_b, m_w_up, m_w_down, m_ln_ff_g, m_ln_ff_b, v_w_in, v_gn_a, v_gn_b, v_sinks, v_w_out, v_ln_mix_g, v_ln_mix_b, v_w_mq, v_w_mkv, v_w_mo, v_ln_mem_g, v_ln_mem_b, v_w_up, v_w_down, v_ln_ff_g, v_ln_ff_b):
    w = dict(zip(WEIGHTS, (w_in, gn_a, gn_b, sinks, w_out, ln_mix_g, ln_mix_b, w_mq, w_mkv, w_mo, ln_mem_g, ln_mem_b, w_up, w_down, ln_ff_g, ln_ff_b)))
    m = dict(zip(WEIGHTS, (m_w_in, m_gn_a, m_gn_b, m_sinks, m_w_out, m_ln_mix_g, m_ln_mix_b, m_w_mq, m_w_mkv, m_w_mo, m_ln_mem_g, m_ln_mem_b, m_w_up, m_w_down, m_ln_ff_g, m_ln_ff_b)))
    v = dict(zip(WEIGHTS, (v_w_in, v_gn_a, v_gn_b, v_sinks, v_w_out, v_ln_mix_g, v_ln_mix_b, v_w_mq, v_w_mkv, v_w_mo, v_ln_mem_g, v_ln_mem_b, v_w_up, v_w_down, v_ln_ff_g, v_ln_ff_b)))
    loss, grad_x, grads, delta, new_m, new_v = _step(x, mem, positions, loss_target, w, m, v)
    return (loss, grad_x, *[grads[k] for k in WEIGHTS], *[delta[k] for k in WEIGHTS],
            *[new_m[k] for k in WEIGHTS], *[new_v[k] for k in WEIGHTS])
```

```python
import functools

import jax
import jax.numpy as jnp
from jax import lax
from jax.experimental import pallas as pl
from jax.experimental.pallas import tpu as pltpu

F32 = jnp.float32
BF16 = jnp.bfloat16
MESH = pl.DeviceIdType.MESH

HEAD_DIM_A = 128
HEAD_DIM_B = 64
LANES = 128
BLOCK = 128
DILATED_BRANCHES = ((128, 1), (512, 4), (2048, 16))
WINDOW_B = 128
N_MEM_HEADS = 4
ROPE_THETA = 500000.0
LN_EPS = 1e-5
RMS_EPS = 1e-6
NEG_INF = -1e30
ADAM_LR = 0.001
ADAM_B1 = 0.9
ADAM_B2 = 0.999
ADAM_EPS = 1e-08
ADAM_WD = 0.01
ADAM_STEP = 10
N_CHIPS = 4
VMEM_LIMIT = 56 * 1024 * 1024

BIG = ("w_in", "w_out", "w_mq", "w_mkv", "w_mo", "w_up", "w_down")
COL_SHARDED = ("w_in", "w_mkv", "w_up")


def _tile(n, target, mult=LANES):
    best = None
    t = mult
    while t <= min(n, target):
        if n % t == 0:
            best = t
        t += mult
    return best if best is not None else n


def _params(sem=None):
    return pltpu.CompilerParams(dimension_semantics=sem, vmem_limit_bytes=VMEM_LIMIT)


def _dot(a, b):
    return jnp.dot(a, b, preferred_element_type=F32)


def _dot_nt(a, b):
    return lax.dot_general(a, b, (((1,), (1,)), ((), ())), preferred_element_type=F32)


def _dot_tn(a, b):
    return lax.dot_general(a, b, (((0,), (0,)), ((), ())), preferred_element_type=F32)


def mm_nn(a, b3, *, name, out_dtype, relu2=False, tm=512, tn=1024, tk=2048):
    M, K = a.shape
    nsh, _, nk = b3.shape
    tm, tn, tk = _tile(M, tm, 8), _tile(nk, tn), _tile(K, tk)
    nb, ksteps = nk // tn, K // tk

    def body(a_ref, b_ref, *rest):
        outs, scr = rest[:2 if relu2 else 1], rest[2 if relu2 else 1:]

        def finish(acc):
            if relu2:
                outs[0][...] = acc.astype(outs[0].dtype)
                r = jnp.maximum(acc, 0.0)
                outs[1][...] = (r * r).astype(outs[1].dtype)
            else:
                outs[0][...] = acc.astype(outs[0].dtype)

        if ksteps == 1:
            finish(_dot(a_ref[...], b_ref[...]))
        else:
            acc_ref = scr[0]
            k = pl.program_id(2)

            @pl.when(k == 0)
            def _():
                acc_ref[...] = jnp.zeros_like(acc_ref)

            acc_ref[...] += _dot(a_ref[...], b_ref[...])

            @pl.when(k == ksteps - 1)
            def _():
                finish(acc_ref[...])

    o_spec = pl.BlockSpec((tm, tn), lambda i, j, k: (i, j))
    o_shape = jax.ShapeDtypeStruct((M, nsh * nk), out_dtype)
    return pl.pallas_call(
        body, name=name,
        grid=(M // tm, nsh * nb, ksteps),
        in_specs=[pl.BlockSpec((tm, tk), lambda i, j, k: (i, k)),
                  pl.BlockSpec((None, tk, tn), lambda i, j, k: (j // nb, k, j % nb))],
        out_specs=(o_spec, o_spec) if relu2 else o_spec,
        out_shape=(o_shape, o_shape) if relu2 else o_shape,
        scratch_shapes=[] if ksteps == 1 else [pltpu.VMEM((tm, tn), F32)],
        compiler_params=_params(("parallel", "parallel", "arbitrary")),
    )(a, b3)


def mm_ln(a, w, resid, g, b, *, name, alpha, tm=512, tk=512):
    M, K = a.shape
    D = w.shape[1]
    tm, tk = _tile(M, tm, 8), _tile(K, tk)
    ksteps = K // tk

    def body(a_ref, w_ref, r_ref, g_ref, b_ref, z_ref, xn_ref, xb_ref, acc_ref):
        k = pl.program_id(1)

        @pl.when(k == 0)
        def _():
            acc_ref[...] = jnp.zeros_like(acc_ref)

        acc_ref[...] += _dot(a_ref[...], w_ref[...])

        @pl.when(k == ksteps - 1)
        def _():
            z = alpha * r_ref[...] + acc_ref[...]
            mu = jnp.mean(z, axis=-1, keepdims=True)
            zc = z - mu
            var = jnp.mean(zc * zc, axis=-1, keepdims=True)
            xn = zc * lax.rsqrt(var + LN_EPS) * g_ref[...] + b_ref[...]
            z_ref[...] = z
            xn_ref[...] = xn
            xb_ref[...] = xn.astype(BF16)

    row = pl.BlockSpec((tm, D), lambda i, k: (i, 0))
    vec = pl.BlockSpec((1, D), lambda i, k: (0, 0))
    return pl.pallas_call(
        body, name=name,
        grid=(M // tm, ksteps),
        in_specs=[pl.BlockSpec((tm, tk), lambda i, k: (i, k)),
                  pl.BlockSpec((tk, D), lambda i, k: (k, 0)), row, vec, vec],
        out_specs=(row, row, row),
        out_shape=(jax.ShapeDtypeStruct((M, D), F32), jax.ShapeDtypeStruct((M, D), F32),
                   jax.ShapeDtypeStruct((M, D), BF16)),
        scratch_shapes=[pltpu.VMEM((tm, D), F32)],
        compiler_params=_params(("parallel", "arbitrary")),
    )(a, w, resid, g, b)


def mm_nt(a, b3, *, name, out_dtype, resid=None, alpha=1.0, umul=None, tm=512, tko=1024, tr=2048):
    M, N = a.shape
    nsh, K, nk = b3.shape
    tm, tko, tr = _tile(M, tm, 8), _tile(K, tko), _tile(nk, tr)
    nb = nk // tr
    rsteps = nsh * nb

    def body(a_ref, b_ref, *rest):
        rest = list(rest)
        r_ref = rest.pop(0) if resid is not None else None
        u_ref = rest.pop(0) if umul is not None else None
        o_ref = rest.pop(0)

        def finish(acc):
            if r_ref is not None:
                acc = acc + alpha * r_ref[...]
            if u_ref is not None:
                acc = acc * (2.0 * jnp.maximum(u_ref[...].astype(F32), 0.0))
            o_ref[...] = acc.astype(o_ref.dtype)

        if rsteps == 1:
            finish(_dot_nt(a_ref[...], b_ref[...]))
        else:
            acc_ref = rest[0]
            r = pl.program_id(2)

            @pl.when(r == 0)
            def _():
                acc_ref[...] = jnp.zeros_like(acc_ref)

            acc_ref[...] += _dot_nt(a_ref[...], b_ref[...])

            @pl.when(r == rsteps - 1)
            def _():
                finish(acc_ref[...])

    o_spec = pl.BlockSpec((tm, tko), lambda i, j, r: (i, j))
    in_specs = [pl.BlockSpec((tm, tr), lambda i, j, r: (i, r)),
                pl.BlockSpec((None, tko, tr), lambda i, j, r: (r // nb, j, r % nb))]
    args = [a, b3]
    for extra in (resid, umul):
        if extra is not None:
            in_specs.append(o_spec)
            args.append(extra)
    return pl.pallas_call(
        body, name=name,
        grid=(M // tm, K // tko, rsteps),
        in_specs=in_specs, out_specs=o_spec,
        out_shape=jax.ShapeDtypeStruct((M, K), out_dtype),
        scratch_shapes=[] if rsteps == 1 else [pltpu.VMEM((tm, tko), F32)],
        compiler_params=_params(("parallel", "parallel", "arbitrary")),
    )(*args)


def mm_tn(a, g, nsh, *, name, tk=1024, tn=1024, tm=512):
    M, K = a.shape
    N = g.shape[1]
    nk = N // nsh
    tk, tn, tm = _tile(K, tk), _tile(nk, tn), _tile(M, tm, 8)
    nb, msteps = nk // tn, M // tm

    def body(a_ref, g_ref, o_ref, acc_ref):
        m = pl.program_id(2)

        @pl.when(m == 0)
        def _():
            acc_ref[...] = jnp.zeros_like(acc_ref)

        acc_ref[...] += _dot_tn(a_ref[...], g_ref[...])

        @pl.when(m == msteps - 1)
        def _():
            o_ref[...] = acc_ref[...].astype(o_ref.dtype)

    return pl.pallas_call(
        body, name=name,
        grid=(K // tk, nsh * nb, msteps),
        in_specs=[pl.BlockSpec((tm, tk), lambda i, j, m: (m, i)),
                  pl.BlockSpec((tm, tn), lambda i, j, m: (m, j))],
        out_specs=pl.BlockSpec((None, tk, tn), lambda i, j, m: (j // nb, i, j % nb)),
        out_shape=jax.ShapeDtypeStruct((nsh, K, nk), BF16),
        scratch_shapes=[pltpu.VMEM((tk, tn), F32)],
        compiler_params=_params(("parallel", "parallel", "arbitrary")),
    )(a, g)


def _fold8(t):
    return t.reshape(t.shape[0] // 8, 8, t.shape[1]).sum(axis=0)


def ln_bwd(dy, z, g, *, name, tm=256):
    M, D = z.shape
    tm = _tile(M, tm, 8)

    def body(dy_ref, z_ref, g_ref, dz_ref, dzb_ref, dg_ref, db_ref):
        i = pl.program_id(0)
        z_ = z_ref[...]
        dy_ = dy_ref[...]
        mu = jnp.mean(z_, axis=-1, keepdims=True)
        zc = z_ - mu
        var = jnp.mean(zc * zc, axis=-1, keepdims=True)
        rstd = lax.rsqrt(var + LN_EPS)
        xh = zc * rstd
        dxh = dy_ * g_ref[...]
        m1 = jnp.mean(dxh, axis=-1, keepdims=True)
        m2 = jnp.mean(dxh * xh, axis=-1, keepdims=True)
        dz = rstd * (dxh - m1 - xh * m2)
        dz_ref[...] = dz
        dzb_ref[...] = dz.astype(BF16)

        @pl.when(i == 0)
        def _():
            dg_ref[...] = jnp.zeros_like(dg_ref)
            db_ref[...] = jnp.zeros_like(db_ref)

        dg_ref[...] += _fold8(dy_ * xh)
        db_ref[...] += _fold8(dy_)

    row = pl.BlockSpec((tm, D), lambda i: (i, 0))
    acc = pl.BlockSpec((8, D), lambda i: (0, 0))
    return pl.pallas_call(
        body, name=name, grid=(M // tm,),
        in_specs=[row, row, pl.BlockSpec((1, D), lambda i: (0, 0))],
        out_specs=(row, row, acc, acc),
        out_shape=(jax.ShapeDtypeStruct((M, D), F32), jax.ShapeDtypeStruct((M, D), BF16),
                   jax.ShapeDtypeStruct((8, D), F32), jax.ShapeDtypeStruct((8, D), F32)),
        compiler_params=_params(("arbitrary",)),
    )(dy, z, g)


def rms_fwd(ya, yb, ga, gb, *, name, tm=512):
    M, WA = ya.shape
    WB = yb.shape[1]
    tm = _tile(M, tm, 8)

    def body(ya_ref, yb_ref, ga_ref, gb_ref, o_ref):
        for y_ref, g_ref, lo, w in ((ya_ref, ga_ref, 0, WA), (yb_ref, gb_ref, WA, WB)):
            y = y_ref[...]
            r = lax.rsqrt(jnp.mean(y * y, axis=-1, keepdims=True) + RMS_EPS)
            o_ref[:, lo:lo + w] = (y * r * g_ref[...]).astype(o_ref.dtype)

    return pl.pallas_call(
        body, name=name, grid=(M // tm,),
        in_specs=[pl.BlockSpec((tm, WA), lambda i: (i, 0)), pl.BlockSpec((tm, WB), lambda i: (i, 0)),
                  pl.BlockSpec((1, WA), lambda i: (0, 0)), pl.BlockSpec((1, WB), lambda i: (0, 0))],
        out_specs=pl.BlockSpec((tm, WA + WB), lambda i: (i, 0)),
        out_shape=jax.ShapeDtypeStruct((M, WA + WB), BF16),
        compiler_params=_params(("parallel",)),
    )(ya, yb, ga, gb)


def rms_bwd(dy, ya, yb, ga, gb, *, name, tm=512):
    M, WA = ya.shape
    WB = yb.shape[1]
    tm = _tile(M, tm, 8)

    def body(dy_ref, ya_ref, yb_ref, ga_ref, gb_ref, dya_ref, dyb_ref, dga_ref, dgb_ref):
        i = pl.program_id(0)

        @pl.when(i == 0)
        def _():
            dga_ref[...] = jnp.zeros_like(dga_ref)
            dgb_ref[...] = jnp.zeros_like(dgb_ref)

        for y_ref, g_ref, d_ref, dgr, lo, w in ((ya_ref, ga_ref, dya_ref, dga_ref, 0, WA),
                                                (yb_ref, gb_ref, dyb_ref, dgb_ref, WA, WB)):
            y = y_ref[...]
            d = dy_ref[:, lo:lo + w]
            r = lax.rsqrt(jnp.mean(y * y, axis=-1, keepdims=True) + RMS_EPS)
            n = y * r
            dn = d * g_ref[...]
            d_ref[...] = r * (dn - n * jnp.mean(dn * n, axis=-1, keepdims=True))
            dgr[...] += _fold8(d * n)

    return pl.pallas_call(
        body, name=name, grid=(M // tm,),
        in_specs=[pl.BlockSpec((tm, WA + WB), lambda i: (i, 0)),
                  pl.BlockSpec((tm, WA), lambda i: (i, 0)), pl.BlockSpec((tm, WB), lambda i: (i, 0)),
                  pl.BlockSpec((1, WA), lambda i: (0, 0)), pl.BlockSpec((1, WB), lambda i: (0, 0))],
        out_specs=(pl.BlockSpec((tm, WA), lambda i: (i, 0)), pl.BlockSpec((tm, WB), lambda i: (i, 0)),
                   pl.BlockSpec((8, WA), lambda i: (0, 0)), pl.BlockSpec((8, WB), lambda i: (0, 0))),
        out_shape=(jax.ShapeDtypeStruct((M, WA), F32), jax.ShapeDtypeStruct((M, WB), F32),
                   jax.ShapeDtypeStruct((8, WA), F32), jax.ShapeDtypeStruct((8, WB), F32)),
        compiler_params=_params(("arbitrary",)),
    )(dy, ya, yb, ga, gb)


def loss_head(y, target, *, name, tm=512):
    M, D = y.shape
    tm = _tile(M, tm, 8)

    def body(y_ref, t_ref, dy_ref, l_ref):
        i = pl.program_id(0)

        @pl.when(i == 0)
        def _():
            l_ref[...] = jnp.zeros_like(l_ref)

        e = y_ref[...] - t_ref[...]
        dy_ref[...] = e * (1.0 / D)
        l_ref[...] += _fold8(e * e)

    row = pl.BlockSpec((tm, D), lambda i: (i, 0))
    return pl.pallas_call(
        body, name=name, grid=(M // tm,),
        in_specs=[row, row],
        out_specs=(row, pl.BlockSpec((8, D), lambda i: (0, 0))),
        out_shape=(jax.ShapeDtypeStruct((M, D), F32), jax.ShapeDtypeStruct((8, D), F32)),
        compiler_params=_params(("arbitrary",)),
    )(y, target)


def _lane(shape):
    return lax.broadcasted_iota(jnp.int32, shape, len(shape) - 1)


def _swap(t, half, period):
    first = (_lane(t.shape) % period) < half
    return jnp.where(first, pltpu.roll(t, LANES - half, 1), pltpu.roll(t, half, 1))


def _rope(t, c, s, half, period):
    return t * c + _swap(t, half, period) * s


def _rope_t(g, c, s, half, period):
    return g * c - _swap(g, half, period) * s


def rope_tables(positions, rot_dim, period):
    half = rot_dim // 2
    inv_freq = ROPE_THETA ** (-jnp.arange(0, rot_dim, 2, dtype=F32) / rot_dim)
    ang = positions.astype(F32)[:, None] * inv_freq
    cos, sin = jnp.cos(ang), jnp.sin(ang)
    ones = jnp.ones((positions.shape[0], period - rot_dim), F32)
    c = jnp.concatenate([cos, cos, ones], axis=1)
    s = jnp.concatenate([-sin, sin, 0.0 * ones], axis=1)
    reps = LANES // period
    return jnp.tile(c, (1, reps)), jnp.tile(s, (1, reps))


def _branch_blocks(S):
    out = []
    for window, d in DILATED_BRANCHES:
        assert window // d == BLOCK and S % (d * BLOCK) == 0
        out.append((d, (S // d) // BLOCK))
    return out


def _rows(r, n, d):
    return pl.ds(r + n * (BLOCK * d), BLOCK, stride=d) if d > 1 else pl.ds(pl.multiple_of(n * BLOCK, BLOCK), BLOCK)


def _band_masks(n, strict_prev):
    qi = lax.broadcasted_iota(jnp.int32, (BLOCK, BLOCK), 0)
    kj = lax.broadcasted_iota(jnp.int32, (BLOCK, BLOCK), 1)
    cur = kj <= qi
    prev = ((kj > qi) if strict_prev else (kj >= qi)) & (n > 0)
    return cur, prev


def attn_a_fwd(h, ca, sa, *, n_heads, name):
    S = h.shape[0]
    scale = HEAD_DIM_A ** -0.5
    half = HEAD_DIM_A // 8
    branches = _branch_blocks(S)

    def body(q_ref, k_ref, v_ref, c_ref, s_ref, y_ref, lse_ref, qs, ks, m_s, l_s, acc_s):
        qs[...] = _rope(q_ref[...], c_ref[...], s_ref[...], half, HEAD_DIM_A)
        ks[...] = _rope(k_ref[...], c_ref[...], s_ref[...], half, HEAD_DIM_A)
        for bi, (d, nb) in enumerate(branches):
            def blk(idx, carry, bi=bi, d=d, nb=nb):
                r, n = idx // nb, idx % nb
                rc, rp = _rows(r, n, d), _rows(r, jnp.maximum(n - 1, 0), d)
                q = qs[rc, :].astype(BF16)
                cur, prev = _band_masks(n, False)
                sc = jnp.where(cur, _dot_nt(q, ks[rc, :].astype(BF16)) * scale, NEG_INF)
                sp = jnp.where(prev, _dot_nt(q, ks[rp, :].astype(BF16)) * scale, NEG_INF)
                m = jnp.maximum(jnp.max(sc, axis=-1, keepdims=True), jnp.max(sp, axis=-1, keepdims=True))
                pc, pp = jnp.exp(sc - m), jnp.exp(sp - m)
                l = jnp.sum(pc, axis=-1, keepdims=True) + jnp.sum(pp, axis=-1, keepdims=True)
                acc = _dot(pc.astype(BF16), v_ref[rc, :].astype(BF16)) + _dot(pp.astype(BF16), v_ref[rp, :].astype(BF16))
                mb = jnp.broadcast_to(m, (BLOCK, LANES))
                lb = jnp.broadcast_to(l, (BLOCK, LANES))
                if bi == 0:
                    m_s[rc, :], l_s[rc, :], acc_s[rc, :] = mb, lb, acc
                else:
                    m0 = m_s[rc, :]
                    mn = jnp.maximum(m0, mb)
                    a0, a1 = jnp.exp(m0 - mn), jnp.exp(mb - mn)
                    m_s[rc, :] = mn
                    l_s[rc, :] = l_s[rc, :] * a0 + lb * a1
                    acc_s[rc, :] = acc_s[rc, :] * a0 + acc * a1
                return carry

            lax.fori_loop(0, d * nb, blk, 0)
        y_ref[...] = acc_s[...] / l_s[...]
        lse_ref[...] = m_s[...] + jnp.log(l_s[...])

    col = lambda off: pl.BlockSpec((S, LANES), lambda hd: (0, off + hd))
    full = pl.BlockSpec((S, LANES), lambda hd: (0, 0))
    out = pl.BlockSpec((S, LANES), lambda hd: (0, hd))
    o_shape = jax.ShapeDtypeStruct((S, n_heads * LANES), F32)
    return pl.pallas_call(
        body, name=name, grid=(n_heads,),
        in_specs=[col(0), col(n_heads), col(2 * n_heads), full, full],
        out_specs=(out, out), out_shape=(o_shape, o_shape),
        scratch_shapes=[pltpu.VMEM((S, LANES), F32) for _ in range(5)],
        compiler_params=_params(("arbitrary",)),
    )(h, h, h, ca, sa)


def attn_a_bwd(h, ca, sa, ya, lse, dya, *, n_heads, name):
    S = h.shape[0]
    scale = HEAD_DIM_A ** -0.5
    half = HEAD_DIM_A // 8
    branches = _branch_blocks(S)

    def body(q_ref, k_ref, v_ref, c_ref, s_ref, y_ref, lse_ref, dy_ref, o_ref, qs, ks, dq_s, dk_s, dv_s):
        part = pl.program_id(1)

        @pl.when(part == 0)
        def _():
            qs[...] = _rope(q_ref[...], c_ref[...], s_ref[...], half, HEAD_DIM_A)
            ks[...] = _rope(k_ref[...], c_ref[...], s_ref[...], half, HEAD_DIM_A)
            dq_s[...] = jnp.zeros_like(dq_s)
            dk_s[...] = jnp.zeros_like(dk_s)
            dv_s[...] = jnp.zeros_like(dv_s)
            for d, nb in branches:
                def blk(idx, carry, d=d, nb=nb):
                    r, n = idx // nb, idx % nb
                    rc, rp = _rows(r, n, d), _rows(r, jnp.maximum(n - 1, 0), d)
                    q = qs[rc, :].astype(BF16)
                    dy = dy_ref[rc, :]
                    dsum = jnp.sum(dy * y_ref[rc, :], axis=-1, keepdims=True)
                    dyb = dy.astype(BF16)
                    lse_b = lse_ref[rc, :]
                    cur, prev = _band_masks(n, False)
                    dq = jnp.zeros((BLOCK, LANES), F32)
                    for rows, mask in ((rc, cur), (rp, prev)):
                        kb = ks[rows, :].astype(BF16)
                        vb = v_ref[rows, :].astype(BF16)
                        s = jnp.where(mask, _dot_nt(q, kb) * scale, NEG_INF)
                        p = jnp.exp(s - lse_b)
                        ds = (p * (_dot_nt(dyb, vb) - dsum) * scale).astype(BF16)
                        dv_s[rows, :] += _dot_tn(p.astype(BF16), dyb)
                        dk_s[rows, :] += _dot_tn(ds, q)
                        dq = dq + _dot(ds, kb)
                    dq_s[rc, :] += dq
                    return carry

                lax.fori_loop(0, d * nb, blk, 0)
            o_ref[...] = _rope_t(dq_s[...], c_ref[...], s_ref[...], half, HEAD_DIM_A).astype(o_ref.dtype)

        @pl.when(part == 1)
        def _():
            o_ref[...] = _rope_t(dk_s[...], c_ref[...], s_ref[...], half, HEAD_DIM_A).astype(o_ref.dtype)

        @pl.when(part == 2)
        def _():
            o_ref[...] = dv_s[...].astype(o_ref.dtype)

    col = lambda off: pl.BlockSpec((S, LANES), lambda hd, p: (0, off + hd))
    full = pl.BlockSpec((S, LANES), lambda hd, p: (0, 0))
    per_head = pl.BlockSpec((S, LANES), lambda hd, p: (0, hd))
    return pl.pallas_call(
        body, name=name, grid=(n_heads, 3),
        in_specs=[col(0), col(n_heads), col(2 * n_heads), full, full, per_head, per_head, per_head],
        out_specs=pl.BlockSpec((S, LANES), lambda hd, p: (0, p * n_heads + hd)),
        out_shape=jax.ShapeDtypeStruct((S, 3 * n_heads * LANES), BF16),
        scratch_shapes=[pltpu.VMEM((S, LANES), F32) for _ in range(5)],
        compiler_params=_params(("arbitrary", "arbitrary")),
    )(h, h, h, ca, sa, ya, lse, dya)


def _both_halves(t, g):
    low = _lane(t.shape) < HEAD_DIM_B
    return jnp.where(low == (g == 0), t, pltpu.roll(t, HEAD_DIM_B, 1))


def _stack_heads(t):
    low = _lane(t.shape) < HEAD_DIM_B
    return jnp.concatenate([jnp.where(low, t, 0.0), jnp.where(low, 0.0, t)], axis=0)


def _unstack_heads(t2):
    low = _lane((BLOCK, LANES)) < HEAD_DIM_B
    return jnp.where(low, t2[:BLOCK], t2[BLOCK:])


def _fold_halves(t):
    return t + pltpu.roll(t, HEAD_DIM_B, 1)


def _head_rows(ref, tile):
    a = ref[pl.ds(2 * tile, 1), :][:, :1]
    b = ref[pl.ds(2 * tile + 1, 1), :][:, :1]
    return jnp.concatenate([jnp.broadcast_to(a, (BLOCK, 1)), jnp.broadcast_to(b, (BLOCK, 1))], axis=0)


def attn_b_fwd(h, cb, sb, sinks_l, *, off_q, n_qtiles, name):
    S = h.shape[0]
    nblk = S // BLOCK
    scale = HEAD_DIM_B ** -0.5
    half = HEAD_DIM_B // 8
    tiles_per_group = n_qtiles // 2

    def body(q_ref, k_ref, v_ref, c_ref, s_ref, sink_ref, y_ref, lse_ref, qs, kg, vg):
        t = pl.program_id(0)
        g = t // tiles_per_group
        qs[...] = _rope(q_ref[...], c_ref[...], s_ref[...], half, HEAD_DIM_B)
        kg[...] = _both_halves(_rope(k_ref[...], c_ref[...], s_ref[...], half, HEAD_DIM_B), g)
        vg[...] = _both_halves(v_ref[...], g)
        sink = _head_rows(sink_ref, t)

        def blk(n, carry):
            rc = pl.ds(pl.multiple_of(n * BLOCK, BLOCK), BLOCK)
            rp = pl.ds(pl.multiple_of(jnp.maximum(n - 1, 0) * BLOCK, BLOCK), BLOCK)
            q2 = _stack_heads(qs[rc, :]).astype(BF16)
            cur, prev = _band_masks(n, True)
            cur2, prev2 = jnp.concatenate([cur, cur], 0), jnp.concatenate([prev, prev], 0)
            sc = jnp.where(cur2, _dot_nt(q2, kg[rc, :].astype(BF16)) * scale, NEG_INF)
            sp = jnp.where(prev2, _dot_nt(q2, kg[rp, :].astype(BF16)) * scale, NEG_INF)
            m = jnp.maximum(jnp.max(sc, axis=-1, keepdims=True), jnp.max(sp, axis=-1, keepdims=True))
            pc, pp = jnp.exp(sc - m), jnp.exp(sp - m)
            l = jnp.sum(pc, axis=-1, keepdims=True) + jnp.sum(pp, axis=-1, keepdims=True)
            acc = _dot(pc.astype(BF16), vg[rc, :].astype(BF16)) + _dot(pp.astype(BF16), vg[rp, :].astype(BF16))
            m2 = jnp.maximum(m, sink)
            c = jnp.exp(m - m2)
            den = l * c + jnp.exp(sink - m2)
            y_ref[rc, :] = _unstack_heads(acc * (c / den))
            lse_ref[rc, :] = _unstack_heads(jnp.broadcast_to(m2 + jnp.log(den), (2 * BLOCK, LANES)))
            return carry

        lax.fori_loop(0, nblk, blk, 0)

    full = lambda col: pl.BlockSpec((S, LANES), lambda t: (0, col))
    out = pl.BlockSpec((S, LANES), lambda t: (0, t))
    o_shape = jax.ShapeDtypeStruct((S, n_qtiles * LANES), F32)
    return pl.pallas_call(
        body, name=name, grid=(n_qtiles,),
        in_specs=[pl.BlockSpec((S, LANES), lambda t: (0, off_q + t)), full(off_q + n_qtiles), full(off_q + n_qtiles + 1),
                  full(0), full(0), pl.BlockSpec(sinks_l.shape, lambda t: (0, 0))],
        out_specs=(out, out), out_shape=(o_shape, o_shape),
        scratch_shapes=[pltpu.VMEM((S, LANES), F32) for _ in range(3)],
        compiler_params=_params(("arbitrary",)),
    )(h, h, h, cb, sb, sinks_l)


def attn_b_bwd(h, cb, sb, sinks_l, yb, lse, dyb, *, off_q, n_qtiles, name):
    S = h.shape[0]
    nblk = S // BLOCK
    scale = HEAD_DIM_B ** -0.5
    half = HEAD_DIM_B // 8
    tiles_per_group = n_qtiles // 2
    n_steps = n_qtiles + 2

    def body(q_ref, k_ref, v_ref, c_ref, s_ref, sink_ref, y_ref, lse_ref, dy_ref,
             dq_ref, dkv_ref, dsink_ref, qs, kg, vg, dk_acc, dv_acc):
        t = pl.program_id(0)

        @pl.when(t == 0)
        def _():
            dk_acc[...] = jnp.zeros_like(dk_acc)
            dv_acc[...] = jnp.zeros_like(dv_acc)
            dsink_ref[...] = jnp.zeros_like(dsink_ref)

        @pl.when(t < n_qtiles)
        def _():
            g = t // tiles_per_group
            qs[...] = _rope(q_ref[...], c_ref[...], s_ref[...], half, HEAD_DIM_B)
            kg[...] = _both_halves(_rope(k_ref[...], c_ref[...], s_ref[...], half, HEAD_DIM_B), g)
            vg[...] = _both_halves(v_ref[...], g)
            sink = _head_rows(sink_ref, t)

            def blk(n, dsink):
                rc = pl.ds(pl.multiple_of(n * BLOCK, BLOCK), BLOCK)
                rp = pl.ds(pl.multiple_of(jnp.maximum(n - 1, 0) * BLOCK, BLOCK), BLOCK)
                q2 = _stack_heads(qs[rc, :]).astype(BF16)
                dy2 = _stack_heads(dy_ref[rc, :])
                dsum = jnp.sum(dy2 * _stack_heads(y_ref[rc, :]), axis=-1, keepdims=True)
                dy2b = dy2.astype(BF16)
                lse_t = lse_ref[rc, :]
                lse2 = jnp.concatenate([lse_t[:, :1], lse_t[:, HEAD_DIM_B:HEAD_DIM_B + 1]], axis=0)
                cur, prev = _band_masks(n, True)
                dq2 = jnp.zeros((2 * BLOCK, LANES), F32)
                for rows, mask in ((rc, cur), (rp, prev)):
                    kb = kg[rows, :].astype(BF16)
                    vb = vg[rows, :].astype(BF16)
                    mask2 = jnp.concatenate([mask, mask], 0)
                    s = jnp.where(mask2, _dot_nt(q2, kb) * scale, NEG_INF)
                    p = jnp.exp(s - lse2)
                    ds = (p * (_dot_nt(dy2b, vb) - dsum) * scale).astype(BF16)
                    dv_acc[g, rows, :] += _fold_halves(_dot_tn(p.astype(BF16), dy2b))
                    dk_acc[g, rows, :] += _fold_halves(_dot_tn(ds, q2))
                    dq2 = dq2 + _dot(ds, kb)
                dq_ref[rc, :] = _unstack_heads(dq2)
                return dsink - jnp.exp(sink - lse2) * dsum

            dsink = lax.fori_loop(0, nblk, blk, jnp.zeros((2 * BLOCK, 1), F32))
            dq_ref[...] = _rope_t(dq_ref[...], c_ref[...], s_ref[...], half, HEAD_DIM_B)
            d0 = jnp.sum(dsink[:BLOCK], axis=0, keepdims=True)
            d1 = jnp.sum(dsink[BLOCK:], axis=0, keepdims=True)
            dsink_ref[pl.ds(2 * t, 1), :] = jnp.broadcast_to(d0, (1, LANES))
            dsink_ref[pl.ds(2 * t + 1, 1), :] = jnp.broadcast_to(d1, (1, LANES))

        low = _lane((S, LANES)) < HEAD_DIM_B

        @pl.when(t == n_qtiles)
        def _():
            dk = jnp.where(low, dk_acc[0], dk_acc[1])
            dkv_ref[...] = _rope_t(dk, c_ref[...], s_ref[...], half, HEAD_DIM_B)

        @pl.when(t == n_qtiles + 1)
        def _():
            dkv_ref[...] = jnp.where(low, dv_acc[0], dv_acc[1])

    qt = lambda t: jnp.minimum(t, n_qtiles - 1)
    full = lambda col: pl.BlockSpec((S, LANES), lambda t: (0, col))
    per_tile = pl.BlockSpec((S, LANES), lambda t: (0, qt(t)))
    return pl.pallas_call(
        body, name=name, grid=(n_steps,),
        in_specs=[pl.BlockSpec((S, LANES), lambda t: (0, off_q + qt(t))), full(off_q + n_qtiles),
                  full(off_q + n_qtiles + 1), full(0), full(0), pl.BlockSpec(sinks_l.shape, lambda t: (0, 0)),
                  per_tile, per_tile, per_tile],
        out_specs=(per_tile, pl.BlockSpec((S, LANES), lambda t: (0, jnp.maximum(t - n_qtiles, 0))),
                   pl.BlockSpec(sinks_l.shape, lambda t: (0, 0))),
        out_shape=(jax.ShapeDtypeStruct((S, n_qtiles * LANES), F32), jax.ShapeDtypeStruct((S, 2 * LANES), F32),
                   jax.ShapeDtypeStruct(sinks_l.shape, F32)),
        scratch_shapes=[pltpu.VMEM((S, LANES), F32) for _ in range(3)]
        + [pltpu.VMEM((2, S, LANES), F32), pltpu.VMEM((2, S, LANES), F32)],
        compiler_params=_params(("arbitrary",)),
    )(h, h, h, cb, sb, sinks_l, yb, lse, dyb)


def mem_attn_fwd(q, kv, *, name, tm=512):
    S, D = q.shape
    n_mem = kv.shape[0]
    hd = D // N_MEM_HEADS
    scale = hd ** -0.5
    tm = _tile(S, tm, 8)

    def body(q_ref, kv_ref, o_ref):
        for hh in range(N_MEM_HEADS):
            cols = slice(hh * hd, (hh + 1) * hd)
            s = _dot_nt(q_ref[:, cols], kv_ref[:, cols]) * scale
            s = s - jnp.max(s, axis=-1, keepdims=True)
            e = jnp.exp(s)
            p = e / jnp.sum(e, axis=-1, keepdims=True)
            o_ref[:, cols] = _dot(p.astype(BF16), kv_ref[:, D + hh * hd:D + (hh + 1) * hd]).astype(o_ref.dtype)

    return pl.pallas_call(
        body, name=name, grid=(S // tm,),
        in_specs=[pl.BlockSpec((tm, D), lambda i: (i, 0)), pl.BlockSpec((n_mem, 2 * D), lambda i: (0, 0))],
        out_specs=pl.BlockSpec((tm, D), lambda i: (i, 0)),
        out_shape=jax.ShapeDtypeStruct((S, D), BF16),
        compiler_params=_params(("parallel",)),
    )(q, kv)


def mem_attn_bwd(q, kv, do, *, name, tm=512):
    S, D = q.shape
    n_mem = kv.shape[0]
    hd = D // N_MEM_HEADS
    scale = hd ** -0.5
    tm = _tile(S, tm, 8)

    def body(q_ref, kv_ref, do_ref, dq_ref, dkv_ref):
        i = pl.program_id(0)

        @pl.when(i == 0)
        def _():
            dkv_ref[...] = jnp.zeros_like(dkv_ref)

        for hh in range(N_MEM_HEADS):
            cols = slice(hh * hd, (hh + 1) * hd)
            vcols = slice(D + hh * hd, D + (hh + 1) * hd)
            qh, kh, vh, doh = q_ref[:, cols], kv_ref[:, cols], kv_ref[:, vcols], do_ref[:, cols]
            s = _dot_nt(qh, kh) * scale
            s = s - jnp.max(s, axis=-1, keepdims=True)
            e = jnp.exp(s)
            p = e / jnp.sum(e, axis=-1, keepdims=True)
            dp = _dot_nt(doh, vh)
            ds = (p * (dp - jnp.sum(dp * p, axis=-1, keepdims=True)) * scale).astype(BF16)
            dq_ref[:, cols] = _dot(ds, kh).astype(dq_ref.dtype)
            dkv_ref[:, cols] += _dot_tn(ds, qh)
            dkv_ref[:, vcols] += _dot_tn(p.astype(BF16), doh)

    row = pl.BlockSpec((tm, D), lambda i: (i, 0))
    kvs = pl.BlockSpec((n_mem, 2 * D), lambda i: (0, 0))
    return pl.pallas_call(
        body, name=name, grid=(S // tm,),
        in_specs=[row, kvs, row], out_specs=(row, kvs),
        out_shape=(jax.ShapeDtypeStruct((S, D), BF16), jax.ShapeDtypeStruct((n_mem, 2 * D), F32)),
        compiler_params=_params(("arbitrary",)),
    )(q, kv, do)


def _rows_view(t):
    return t.reshape(-1, t.shape[-1])


def _row_tile(rows, cols, target_elems=512 * 1024):
    return _tile(rows, max(8, target_elems // cols), 8)


def cast_bf16(w, *, name):
    v = _rows_view(w)
    R, C = v.shape
    tr = _row_tile(R, C)

    def body(w_ref, o_ref):
        o_ref[...] = w_ref[...].astype(BF16)

    spec = pl.BlockSpec((tr, C), lambda i: (i, 0))
    out = pl.pallas_call(body, name=name, grid=(R // tr,), in_specs=[spec], out_specs=spec,
                         out_shape=jax.ShapeDtypeStruct((R, C), BF16), compiler_params=_params(("parallel",)))(v)
    return out.reshape(w.shape)


def mesh_place():
    return tuple(lax.axis_index(a).astype(jnp.int32).reshape(1) for a in ("x", "y", "c"))


def pair_sum(p, r1, place, *, name):
    nsh, r, c = p.shape
    hr = r // 2
    tr = _row_tile(hr, c)
    nt = hr // tr

    def body(x_ref, y_ref, c_ref, p_ref, r_ref, o_ref):
        o_ref[...] = (p_ref[...].astype(F32) + r_ref[...].astype(F32)).astype(BF16)

    return pl.pallas_call(
        body, name=name,
        grid_spec=pltpu.PrefetchScalarGridSpec(
            num_scalar_prefetch=3, grid=(nsh, nt),
            in_specs=[pl.BlockSpec((None, tr, c), lambda s, i, x, y, cc: (s, cc[0] * nt + i, 0)),
                      pl.BlockSpec((None, tr, c), lambda s, i, x, y, cc: (s, i, 0))],
            out_specs=pl.BlockSpec((None, tr, c), lambda s, i, x, y, cc: (s, i, 0))),
        out_shape=jax.ShapeDtypeStruct((nsh, hr, c), BF16),
        compiler_params=_params(("parallel", "parallel")),
    )(*place, p, r1)


def cast_into_slot(w, place, *, name):
    R, C = w.shape
    tr = _row_tile(R, C)

    def body(x_ref, y_ref, c_ref, w_ref, o_ref):
        o_ref[...] = w_ref[...].astype(BF16)

    return pl.pallas_call(
        body, name=name,
        grid_spec=pltpu.PrefetchScalarGridSpec(
            num_scalar_prefetch=3, grid=(R // tr,),
            in_specs=[pl.BlockSpec((tr, C), lambda i, x, y, cc: (i, 0))],
            out_specs=pl.BlockSpec((None, tr, C), lambda i, x, y, cc: (2 * x[0] + y[0], i, 0))),
        out_shape=jax.ShapeDtypeStruct((N_CHIPS, R, C), BF16),
        compiler_params=_params(("parallel",)),
    )(*place, w)


def chip_sum(q, r2, place, *, name):
    _, hr, c = q.shape
    tr = _row_tile(hr, c, 256 * 1024)
    nt = hr // tr

    def body(x_ref, y_ref, c_ref, q_ref, r_ref, o_ref):
        acc = q_ref[...].astype(F32)
        for k in range(r_ref.shape[0]):
            acc = acc + r_ref[k].astype(F32)
        o_ref[...] = acc

    return pl.pallas_call(
        body, name=name,
        grid_spec=pltpu.PrefetchScalarGridSpec(
            num_scalar_prefetch=3, grid=(nt,),
            in_specs=[pl.BlockSpec((None, tr, c), lambda i, x, y, cc: (2 * x[0] + y[0], i, 0)),
                      pl.BlockSpec((r2.shape[0], tr, c), lambda i, x, y, cc: (0, i, 0))],
            out_specs=pl.BlockSpec((tr, c), lambda i, x, y, cc: (cc[0] * nt + i, 0))),
        out_shape=jax.ShapeDtypeStruct((2 * hr, c), F32),
        compiler_params=_params(("parallel",)),
    )(*place, q, r2)


def adamw(w, g, m, v, *, name):
    shape = w.shape
    wv, gv, mv, vv = (_rows_view(t) for t in (w, g, m, v))
    R, C = wv.shape
    tr = _row_tile(R, C, 256 * 1024)
    c1 = 1.0 / (1.0 - ADAM_B1 ** ADAM_STEP)
    c2 = 1.0 / (1.0 - ADAM_B2 ** ADAM_STEP)

    def body(w_ref, g_ref, m_ref, v_ref, d_ref, nm_ref, nv_ref):
        g_ = g_ref[...]
        nm = ADAM_B1 * m_ref[...] + (1.0 - ADAM_B1) * g_
        nv = ADAM_B2 * v_ref[...] + (1.0 - ADAM_B2) * (g_ * g_)
        m_hat = nm * c1
        v_hat = nv * c2
        d_ref[...] = -ADAM_LR * (m_hat / (jnp.sqrt(v_hat) + ADAM_EPS) + ADAM_WD * w_ref[...])
        nm_ref[...] = nm
        nv_ref[...] = nv

    spec = pl.BlockSpec((tr, C), lambda i: (i, 0))
    o = jax.ShapeDtypeStruct((R, C), F32)
    outs = pl.pallas_call(body, name=name, grid=(R // tr,), in_specs=[spec] * 4, out_specs=(spec,) * 3,
                          out_shape=(o, o, o), compiler_params=_params(("parallel",)))(wv, gv, mv, vv)
    return tuple(t.reshape(shape) for t in outs)


def _place():
    x, y, c = lax.axis_index("x"), lax.axis_index("y"), lax.axis_index("c")
    others = [(1 - x, y), (x, 1 - y), (1 - x, 1 - y)]
    return x, y, c, others


def _any_specs(n):
    return [pl.BlockSpec(memory_space=pl.ANY) for _ in range(n)]


HBM_SPEC = pl.BlockSpec(memory_space=pltpu.HBM)
SEM_SPEC = pl.BlockSpec(memory_space=pltpu.SEMAPHORE)
DATAFLOW = pltpu.SideEffectType.DATAFLOW_SIDE_EFFECTING
TOKEN = jax.ShapeDtypeStruct((8, LANES), F32)


def _in_hbm(arrays):
    return [pltpu.with_memory_space_constraint(a, pltpu.HBM) for a in arrays]


def _gather_copy(g, t, j, slot, px, py, c, send, recv):
    hr = g[t].shape[1] // 2
    rows = g[t].at[slot, pl.ds(c * hr, hr)]
    return pltpu.make_async_remote_copy(rows, rows, send.at[3 * t + j], recv.at[3 * t + j], device_id=(px, py, c), device_id_type=MESH)


def gather_start(gs, *, name):
    n = len(gs)

    def body(*refs):
        g, token = refs[:n], refs[-1]
        send, recv = refs[n], refs[n + 1]
        x, y, c, others = _place()
        for t in range(n):
            for j, (px, py) in enumerate(others):
                _gather_copy(g, t, j, 2 * x + y, px, py, c, send, recv).start()
        token[...] = jnp.zeros_like(token)

    outs = pl.pallas_call(
        body, name=name,
        in_specs=[HBM_SPEC] * n,
        out_specs=(SEM_SPEC, SEM_SPEC, *[HBM_SPEC] * n, pl.BlockSpec(memory_space=pltpu.VMEM)),
        out_shape=(pltpu.SemaphoreType.DMA((3 * n,)), pltpu.SemaphoreType.DMA((3 * n,)),
                   *[pltpu.HBM(g.shape, g.dtype) for g in gs], TOKEN),
        input_output_aliases={t: 2 + t for t in range(n)},
        compiler_params=pltpu.CompilerParams(has_side_effects=DATAFLOW),
    )(*_in_hbm(gs))
    return outs[0], outs[1], list(outs[2:2 + n]), outs[-1]


def gather_wait(send, recv, gs, after, *, name):
    n = len(gs)

    def body(*refs):
        g = refs[:n]
        send, recv = refs[n], refs[n + 1]
        x, y, c, others = _place()
        for t in range(n):
            for j, (px, py) in enumerate(others):
                _gather_copy(g, t, j, 2 * x + y, px, py, c, send, recv).wait_send()
                _gather_copy(g, t, j, 2 * px + py, px, py, c, send, recv).wait_recv()

    outs = pl.pallas_call(
        body, name=name,
        in_specs=[HBM_SPEC] * n + [SEM_SPEC, SEM_SPEC, pl.BlockSpec(memory_space=pl.ANY)],
        out_specs=tuple([HBM_SPEC] * n),
        out_shape=tuple(pltpu.HBM(g.shape, g.dtype) for g in gs),
        input_output_aliases={t: t for t in range(n)},
        compiler_params=pltpu.CompilerParams(has_side_effects=DATAFLOW),
    )(*gs, send, recv, after)
    return list(outs)


def gather_forward(gs, *, name):
    n = len(gs)

    def body(*refs):
        g = refs[n:2 * n]
        send, recv = refs[2 * n:]
        x, y, c, others = _place()
        cps = []
        for t in range(n):
            hr = g[t].shape[1] // 2
            for j, (px, py) in enumerate(others):
                rows = g[t].at[2 * px + py, pl.ds(c * hr, hr)]
                cp = pltpu.make_async_remote_copy(rows, rows, send.at[3 * t + j], recv.at[3 * t + j],
                                                  device_id=(x, y, 1 - c), device_id_type=MESH)
                cp.start()
                cps.append(cp)
        for t in range(n):
            hr = g[t].shape[1] // 2
            for j, (px, py) in enumerate(others):
                rows = g[t].at[2 * px + py, pl.ds((1 - c) * hr, hr)]
                pltpu.make_async_remote_copy(rows, rows, send.at[3 * t + j], recv.at[3 * t + j],
                                             device_id=(x, y, 1 - c), device_id_type=MESH).wait_recv()
        for cp in cps:
            cp.wait_send()

    return pl.pallas_call(
        body, name=name,
        in_specs=_any_specs(n), out_specs=_any_specs(n),
        out_shape=[jax.ShapeDtypeStruct(g.shape, g.dtype) for g in gs],
        input_output_aliases={t: t for t in range(n)},
        scratch_shapes=[pltpu.SemaphoreType.DMA((3 * n,)), pltpu.SemaphoreType.DMA((3 * n,))],
        compiler_params=pltpu.CompilerParams(has_side_effects=True),
    )(*gs)


def sibling_halves(parts, *, name):
    n = len(parts)

    def body(*refs):
        src, dst = refs[:n], refs[n:2 * n]
        send, recv = refs[2 * n:]
        x, y, c, _ = _place()
        cps = []
        for t in range(n):
            hr = src[t].shape[1] // 2
            cp = pltpu.make_async_remote_copy(src[t].at[:, pl.ds((1 - c) * hr, hr)], dst[t], send.at[t], recv.at[t],
                                              device_id=(x, y, 1 - c), device_id_type=MESH)
            cp.start()
            cps.append(cp)
        for cp in cps:
            cp.wait()

    return pl.pallas_call(
        body, name=name,
        in_specs=_any_specs(n), out_specs=_any_specs(n),
        out_shape=[jax.ShapeDtypeStruct((p.shape[0], p.shape[1] // 2, p.shape[2]), p.dtype) for p in parts],
        scratch_shapes=[pltpu.SemaphoreType.DMA((n,)), pltpu.SemaphoreType.DMA((n,))],
        compiler_params=pltpu.CompilerParams(has_side_effects=True),
    )(*parts)


def _chips_copy(q, land, t, j, px, py, c, send, recv):
    return pltpu.make_async_remote_copy(q[t].at[2 * px + py], land[t].at[j], send.at[3 * t + j], recv.at[3 * t + j],
                                        device_id=(px, py, c), device_id_type=MESH)


def chips_start(qs, *, name):
    n = len(qs)

    def body(*refs):
        q, land, token = refs[:n], refs[n:2 * n], refs[-1]
        send, recv = refs[2 * n], refs[2 * n + 1]
        x, y, c, others = _place()
        for t in range(n):
            for j, (px, py) in enumerate(others):
                _chips_copy(q, land, t, j, px, py, c, send, recv).start()
        token[...] = jnp.zeros_like(token)

    lands = [lax.empty((3,) + q.shape[1:], q.dtype) for q in qs]
    outs = pl.pallas_call(
        body, name=name,
        in_specs=[HBM_SPEC] * (2 * n),
        out_specs=(SEM_SPEC, SEM_SPEC, *[HBM_SPEC] * (2 * n), pl.BlockSpec(memory_space=pltpu.VMEM)),
        out_shape=(pltpu.SemaphoreType.DMA((3 * n,)), pltpu.SemaphoreType.DMA((3 * n,)),
                   *[pltpu.HBM(a.shape, a.dtype) for a in qs + lands], TOKEN),
        input_output_aliases={t: 2 + t for t in range(2 * n)},
        compiler_params=pltpu.CompilerParams(has_side_effects=DATAFLOW),
    )(*_in_hbm(qs + lands))
    return outs[0], outs[1], list(outs[2:2 + n]), list(outs[2 + n:2 + 2 * n]), outs[-1]


def chips_wait(send, recv, qs, lands, after, *, name):
    n = len(qs)

    def body(*refs):
        q, land = refs[:n], refs[n:2 * n]
        send, recv = refs[2 * n], refs[2 * n + 1]
        x, y, c, others = _place()
        for t in range(n):
            for j, (px, py) in enumerate(others):
                cp = _chips_copy(q, land, t, j, px, py, c, send, recv)
                cp.wait_send()
                cp.wait_recv()

    outs = pl.pallas_call(
        body, name=name,
        in_specs=[HBM_SPEC] * (2 * n) + [SEM_SPEC, SEM_SPEC, pl.BlockSpec(memory_space=pl.ANY)],
        out_specs=tuple([HBM_SPEC] * (2 * n)),
        out_shape=tuple(pltpu.HBM(a.shape, a.dtype) for a in qs + lands),
        input_output_aliases={t: t for t in range(2 * n)},
        compiler_params=pltpu.CompilerParams(has_side_effects=DATAFLOW),
    )(*qs, *lands, send, recv, after)
    return list(outs[:n]), list(outs[n:])


def join_halves(fulls, *, name):
    n = len(fulls)

    def body(*refs):
        g = refs[n:2 * n]
        send, recv = refs[2 * n:]
        x, y, c, _ = _place()
        cps = []
        for t in range(n):
            hr = g[t].shape[0] // 2
            rows = g[t].at[pl.ds(c * hr, hr)]
            cp = pltpu.make_async_remote_copy(rows, rows, send.at[t], recv.at[t], device_id=(x, y, 1 - c), device_id_type=MESH)
            cp.start()
            cps.append(cp)
        for t in range(n):
            hr = g[t].shape[0] // 2
            rows = g[t].at[pl.ds((1 - c) * hr, hr)]
            pltpu.make_async_remote_copy(rows, rows, send.at[t], recv.at[t],
                                         device_id=(x, y, 1 - c), device_id_type=MESH).wait_recv()
        for cp in cps:
            cp.wait_send()

    return pl.pallas_call(
        body, name=name,
        in_specs=_any_specs(n), out_specs=_any_specs(n),
        out_shape=[jax.ShapeDtypeStruct(g.shape, g.dtype) for g in fulls],
        input_output_aliases={t: t for t in range(n)},
        scratch_shapes=[pltpu.SemaphoreType.DMA((n,)), pltpu.SemaphoreType.DMA((n,))],
        compiler_params=pltpu.CompilerParams(has_side_effects=True),
    )(*fulls)


def allreduce_small(t, *, name):
    R, C = t.shape

    def body(t_ref, o_ref, land, send, recv):
        x, y, c, _ = _place()
        me = 4 * x + 2 * y + c
        land[me] = t_ref[...]
        cps = []
        for j in range(1, 8):
            px, py, pc = (x + (j >> 2)) % 2, (y + ((j >> 1) & 1)) % 2, (c + (j & 1)) % 2
            cp = pltpu.make_async_remote_copy(t_ref, land.at[me], send.at[j - 1], recv.at[j - 1],
                                              device_id=(px, py, pc), device_id_type=MESH)
            cp.start()
            cps.append(cp)
        for j in range(1, 8):
            px, py, pc = (x + (j >> 2)) % 2, (y + ((j >> 1) & 1)) % 2, (c + (j & 1)) % 2
            pltpu.make_async_remote_copy(t_ref, land.at[4 * px + 2 * py + pc], send.at[j - 1], recv.at[j - 1],
                                         device_id=(px, py, pc), device_id_type=MESH).wait_recv()
        for cp in cps:
            cp.wait_send()
        acc = land[0]
        for k in range(1, 8):
            acc = acc + land[k]
        o_ref[...] = acc

    return pl.pallas_call(
        body, name=name,
        in_specs=[pl.BlockSpec(memory_space=pltpu.VMEM)], out_specs=pl.BlockSpec(memory_space=pltpu.VMEM),
        out_shape=jax.ShapeDtypeStruct((R, C), F32),
        scratch_shapes=[pltpu.VMEM((8, R, C), F32), pltpu.SemaphoreType.DMA((7,)), pltpu.SemaphoreType.DMA((7,))],
        compiler_params=pltpu.CompilerParams(has_side_effects=True),
    )(t)


def _layer_fwd(x, xb, memb, W, P, tabs, alpha, li):
    ca, sa, cb, sb = tabs
    nA = P["gn_a"].shape[1] // HEAD_DIM_A
    nQ = P["gn_b"].shape[1] // LANES
    nm = lambda s: f"L{li}_{s}"
    h = mm_nn(xb, W["w_in"], name=nm("h"), out_dtype=F32, tn=2304)
    ya, lse_a = attn_a_fwd(h, ca, sa, n_heads=nA, name=nm("attn_a"))
    yb, lse_b = attn_b_fwd(h, cb, sb, P["sinks_l"], off_q=3 * nA, n_qtiles=nQ, name=nm("attn_b"))
    ymix = rms_fwd(ya, yb, P["gn_a"], P["gn_b"], name=nm("rms"))
    z1, x1, x1b = mm_ln(ymix, W["w_out"][0], x, P["ln_mix_g"], P["ln_mix_b"], name=nm("out_ln"), alpha=alpha)
    qm = mm_nn(x1b, W["w_mq"], name=nm("mq"), out_dtype=BF16)
    kv = mm_nn(memb, W["w_mkv"], name=nm("mkv"), out_dtype=BF16, tm=256)
    o = mem_attn_fwd(qm, kv, name=nm("mem_attn"))
    z2, x2, x2b = mm_ln(o, W["w_mo"][0], x1, P["ln_mem_g"], P["ln_mem_b"], name=nm("mo_ln"), alpha=alpha)
    u, a = mm_nn(x2b, W["w_up"], name=nm("up"), out_dtype=BF16, relu2=True)
    z3, x3, x3b = mm_ln(a, W["w_down"][0], x2, P["ln_ff_g"], P["ln_ff_b"], name=nm("down_ln"), alpha=alpha)
    saved = dict(xb=xb, h=h, ya=ya, lse_a=lse_a, yb=yb, lse_b=lse_b, ymix=ymix, z1=z1, x1b=x1b, qm=qm, kv=kv, o=o,
                 z2=z2, x2b=x2b, u=u, a=a, z3=z3)
    return x3, x3b, saved


def _layer_bwd(dx3, sv, memb, W, P, tabs, alpha, li):
    ca, sa, cb, sb = tabs
    nA = P["gn_a"].shape[1] // HEAD_DIM_A
    nQ = P["gn_b"].shape[1] // LANES
    nm = lambda s: f"L{li}_b_{s}"
    nsh = lambda k: W[k].shape[0]
    gw, gs = {}, {}
    dz3, dz3b, gs["ln_ff_g"], gs["ln_ff_b"] = ln_bwd(dx3, sv["z3"], P["ln_ff_g"], name=nm("ln_ff"))
    gw["w_down"] = mm_tn(sv["a"], dz3b, nsh("w_down"), name=nm("dw_down"))
    du = mm_nt(dz3b, W["w_down"], name=nm("du"), out_dtype=BF16, umul=sv["u"])
    gw["w_up"] = mm_tn(sv["x2b"], du, nsh("w_up"), name=nm("dw_up"))
    dx2 = mm_nt(du, W["w_up"], name=nm("dx2"), out_dtype=F32, resid=dz3, alpha=alpha)
    dz2, dz2b, gs["ln_mem_g"], gs["ln_mem_b"] = ln_bwd(dx2, sv["z2"], P["ln_mem_g"], name=nm("ln_mem"))
    gw["w_mo"] = mm_tn(sv["o"], dz2b, nsh("w_mo"), name=nm("dw_mo"))
    do = mm_nt(dz2b, W["w_mo"], name=nm("do"), out_dtype=BF16)
    dqm, dkv = mem_attn_bwd(sv["qm"], sv["kv"], do, name=nm("mem_attn"))
    gw["w_mq"] = mm_tn(sv["x1b"], dqm, nsh("w_mq"), name=nm("dw_mq"))
    gw["w_mkv"] = mm_tn(memb, cast_bf16(dkv, name=nm("dkv_cast")), nsh("w_mkv"), name=nm("dw_mkv"), tm=256)
    dx1 = mm_nt(dqm, W["w_mq"], name=nm("dx1"), out_dtype=F32, resid=dz2, alpha=alpha)
    dz1, dz1b, gs["ln_mix_g"], gs["ln_mix_b"] = ln_bwd(dx1, sv["z1"], P["ln_mix_g"], name=nm("ln_mix"))
    gw["w_out"] = mm_tn(sv["ymix"], dz1b, nsh("w_out"), name=nm("dw_out"))
    dymix = mm_nt(dz1b, W["w_out"], name=nm("dymix"), out_dtype=F32)
    dya, dyb, gs["gn_a"], gs["gn_b"] = rms_bwd(dymix, sv["ya"], sv["yb"], P["gn_a"], P["gn_b"], name=nm("rms"))
    dha = attn_a_bwd(sv["h"], ca, sa, sv["ya"], sv["lse_a"], dya, n_heads=nA, name=nm("attn_a"))
    dqb, dkvb, gs["sinks"] = attn_b_bwd(sv["h"], cb, sb, P["sinks_l"], sv["yb"], sv["lse_b"], dyb,
                                        off_q=3 * nA, n_qtiles=nQ, name=nm("attn_b"))
    dh = jnp.concatenate([dha, dqb.astype(BF16), dkvb.astype(BF16)], axis=1)
    gw["w_in"] = mm_tn(sv["xb"], dh, nsh("w_in"), name=nm("dw_in"), tn=2304)
    dx0 = mm_nt(dh, W["w_in"], name=nm("dx0"), out_dtype=F32, resid=dz1, alpha=alpha, tr=2304)
    return dx0, gw, gs


def _gathered_view(name, g):
    if name == "w_in":
        return jnp.concatenate([g[k] for k in range(N_CHIPS)], axis=1)[None]
    if name in COL_SHARDED:
        return g
    return g.reshape(1, g.shape[0] * g.shape[1], g.shape[2])


def _to_shards(name, gw):
    if name == "w_in":
        n = gw.shape[2] // N_CHIPS
        return jnp.stack([gw[0, :, k * n:(k + 1) * n] for k in range(N_CHIPS)])
    if name in COL_SHARDED:
        return gw
    return gw.reshape(N_CHIPS, gw.shape[1] // N_CHIPS, gw.shape[2])


def _step(x, mem, positions, loss_target, w, m, v):
    S, D = x.shape[1], x.shape[2]
    depth = w["w_in"].shape[0]
    alpha = (2 * depth) ** 0.25
    x0 = x[0]
    memb = cast_bf16(mem[0], name="mem_cast")
    pos = positions[0]
    tabs = rope_tables(pos, HEAD_DIM_A // 4, HEAD_DIM_A) + rope_tables(pos, HEAD_DIM_B // 4, HEAD_DIM_B)
    place = mesh_place()

    started = []
    for li in range(depth):
        gs = [cast_into_slot(w[k][li], place, name=f"L{li}_cast_{k}") for k in BIG]
        started.append(gather_start(gs, name=f"L{li}_gather_start"))
    all_started = functools.reduce(jnp.add, [s[3] for s in started])

    def small(li):
        P = {k: w[k][li][None] for k in ("gn_a", "gn_b", "ln_mix_g", "ln_mix_b", "ln_mem_g", "ln_mem_b", "ln_ff_g", "ln_ff_b")}
        P["sinks_l"] = jnp.broadcast_to(w["sinks"][li][:, None], (w["sinks"].shape[1], LANES))
        return P

    xs, xbs, saved, Ws = x0, cast_bf16(x0, name="x_cast"), [], []
    for li in range(depth):
        send, recv, gs, _ = started[li]
        gs = gather_wait(send, recv, gs, all_started if li == 0 else xs, name=f"L{li}_gather_wait")
        gs = gather_forward(gs, name=f"L{li}_gather_fwd")
        Ws.append({k: _gathered_view(k, gk) for k, gk in zip(BIG, gs)})
        xs, xbs, sv = _layer_fwd(xs, xbs, memb, Ws[li], small(li), tabs, alpha, li)
        saved.append(sv)
    dy, loss_part = loss_head(xs, loss_target[0], name="loss")
    loss = lax.psum(0.5 / D * jnp.sum(loss_part), ("x", "y", "c"))

    g_big = {k: [None] * depth for k in BIG}
    g_small = [None] * depth

    def finish(li, send, recv, qs, lands, after):
        qs, lands = chips_wait(send, recv, qs, lands, after, name=f"L{li}_rs_chips_wait")
        fulls = [chip_sum(q, r, place, name=f"L{li}_rs_sum_{k}") for k, q, r in zip(BIG, qs, lands)]
        for k, f in zip(BIG, join_halves(fulls, name=f"L{li}_rs_join")):
            g_big[k][li] = f

    pending, token = None, None
    for li in reversed(range(depth)):
        P = small(li)
        if token is not None:
            P["ln_ff_g"] = P["ln_ff_g"] + token[:1, :1]
        dy, gw, g_small[li] = _layer_bwd(dy, saved[li], memb, Ws[li], P, tabs, alpha, li)
        if pending is not None:
            finish(*pending, dy)
        parts = [_to_shards(k, gw[k]) for k in BIG]
        r1 = sibling_halves(parts, name=f"L{li}_rs_sibling")
        qs = [pair_sum(p, r, place, name=f"L{li}_rs_pair_{k}") for k, p, r in zip(BIG, parts, r1)]
        send, recv, qs, lands, token = chips_start(qs, name=f"L{li}_rs_chips_start")
        pending = (li, send, recv, qs, lands)
    finish(*pending, token)
    grad_x = dy[None]

    rows = []
    for li in range(depth):
        gs = g_small[li]
        for k in ("ln_mix_g", "ln_mix_b", "ln_mem_g", "ln_mem_b", "ln_ff_g", "ln_ff_b"):
            rows.append(jnp.sum(gs[k], axis=0, keepdims=True))
        rows.append(jnp.concatenate([jnp.sum(gs["gn_a"], axis=0, keepdims=True), jnp.sum(gs["gn_b"], axis=0, keepdims=True)], axis=1))
        sk = gs["sinks"][:, 0][None]
        rows.append(jnp.pad(sk, ((0, 0), (0, D - sk.shape[1]))))
    red = allreduce_small(jnp.concatenate(rows, axis=0), name="small_allreduce").reshape(depth, 8, D)
    wa = w["gn_a"].shape[1]
    grads = {k: jnp.stack(g_big[k]) for k in BIG}
    for j, k in enumerate(("ln_mix_g", "ln_mix_b", "ln_mem_g", "ln_mem_b", "ln_ff_g", "ln_ff_b")):
        grads[k] = red[:, j]
    grads["gn_a"] = red[:, 6, :wa]
    grads["gn_b"] = red[:, 6, wa:]
    grads["sinks"] = red[:, 7, :w["sinks"].shape[1]]

    delta, new_m, new_v = {}, {}, {}
    small_names = [k for k in w if k not in BIG]
    for k in BIG:
        delta[k], new_m[k], new_v[k] = adamw(w[k], grads[k], m[k], v[k], name=f"adamw_{k}")
    pack = lambda d: jnp.concatenate([jnp.pad(d[k], ((0, 0), (0, D - d[k].shape[1]))) for k in small_names], axis=0)
    ds, ms, vs = adamw(pack(w), pack(grads), pack(m), pack(v), name="adamw_small")
    for j, k in enumerate(small_names):
        sl = (slice(j * depth, (j + 1) * depth), slice(0, w[k].shape[1]))
        delta[k], new_m[k], new_v[k] = ds[sl], ms[sl], vs[sl]
    return loss, grad_x, grads, delta, new_m, new_v


WEIGHTS = ("w_in", "gn_a", "gn_b", "sinks", "w_out", "ln_mix_g", "ln_mix_b", "w_mq", "w_mkv", "w_mo",
           "ln_mem_g", "ln_mem_b", "w_up", "w_down", "ln_ff_g", "ln_ff_b")


def kernel(x, mem, positions, w_in, gn_a, gn_b, sinks, w_out, ln_mix_g, ln_mix_b, w_mq, w_mkv, w_mo, ln_mem_g, ln_mem_b, w_up, w_down, ln_ff_g, ln_ff_b, loss_target, m_w_in, m_gn_a, m_gn_b, m_sinks, m_w_out, m_ln_mix_g, m_ln_mix_b, m_w_mq, m_w_mkv, m_w_mo, m_ln_mem_g, m_ln_mem_b, m_w_up, m_w_down, m_ln_ff_g, m_ln_ff_b, v_w_in, v_gn_a, v_gn_b, v_sinks, v_w_out, v_ln_mix_g, v_ln_mix_b, v_w_mq, v_w_mkv, v_w_mo, v_ln_mem_g, v_ln_mem_b, v_w_up, v_w_down, v_ln_ff_g, v_ln_ff_b):
    w = dict(zip(WEIGHTS, (w_in, gn_a, gn_b, sinks, w_out, ln_mix_g, ln_mix_b, w_mq, w_mkv, w_mo, ln_mem_g, ln_mem_b, w_up, w_down, ln_ff_g, ln_ff_b)))
    m = dict(zip(WEIGHTS, (m_w_in, m_gn_a, m_gn_b, m_sinks, m_w_out, m_ln_mix_g, m_ln_mix_b, m_w_mq, m_w_mkv, m_w_mo, m_ln_mem_g, m_ln_mem_b, m_w_up, m_w_down, m_ln_ff_g, m_ln_ff_b)))
    v = dict(zip(WEIGHTS, (v_w_in, v_gn_a, v_gn_b, v_sinks, v_w_out, v_ln_mix_g, v_ln_mix_b, v_w_mq, v_w_mkv, v_w_mo, v_ln_mem_g, v_ln_mem_b, v_w_up, v_w_down, v_ln_ff_g, v_ln_ff_b)))
    loss, grad_x, grads, delta, new_m, new_v = _step(x, mem, positions, loss_target, w, m, v)
    return (loss, grad_x, *[grads[k] for k in WEIGHTS], *[delta[k] for k in WEIGHTS],
            *[new_m[k] for k in WEIGHTS], *[new_v[k] for k in WEIGHTS])
```

```python
import functools

import jax
import jax.numpy as jnp
from jax import lax
from jax.experimental import pallas as pl
from jax.experimental.pallas import tpu as pltpu

F32 = jnp.float32
BF16 = jnp.bfloat16
MESH = pl.DeviceIdType.MESH

HEAD_DIM_A = 128
HEAD_DIM_B = 64
LANES = 128
BLOCK = 128
DILATED_BRANCHES = ((128, 1), (512, 4), (2048, 16))
WINDOW_B = 128
N_MEM_HEADS = 4
ROPE_THETA = 500000.0
LN_EPS = 1e-5
RMS_EPS = 1e-6
NEG_INF = -1e30
ADAM_LR = 0.001
ADAM_B1 = 0.9
ADAM_B2 = 0.999
ADAM_EPS = 1e-08
ADAM_WD = 0.01
ADAM_STEP = 10
N_CHIPS = 4
VMEM_LIMIT = 56 * 1024 * 1024

BIG = ("w_in", "w_out", "w_mq", "w_mkv", "w_mo", "w_up", "w_down")
COL_SHARDED = ("w_in", "w_mkv", "w_up")


def _tile(n, target, mult=LANES):
    best = None
    t = mult
    while t <= min(n, target):
        if n % t == 0:
            best = t
        t += mult
    return best if best is not None else n


def _params(sem=None):
    return pltpu.CompilerParams(dimension_semantics=sem, vmem_limit_bytes=VMEM_LIMIT)


def _dot(a, b):
    return jnp.dot(a, b, preferred_element_type=F32)


def _dot_nt(a, b):
    return lax.dot_general(a, b, (((1,), (1,)), ((), ())), preferred_element_type=F32)


def _dot_tn(a, b):
    return lax.dot_general(a, b, (((0,), (0,)), ((), ())), preferred_element_type=F32)


def mm_nn(a, b3, *, name, out_dtype, relu2=False, tm=512, tn=1024, tk=2048):
    M, K = a.shape
    nsh, _, nk = b3.shape
    tm, tn, tk = _tile(M, tm, 8), _tile(nk, tn), _tile(K, tk)
    nb, ksteps = nk // tn, K // tk

    def body(a_ref, b_ref, *rest):
        outs, scr = rest[:2 if relu2 else 1], rest[2 if relu2 else 1:]

        def finish(acc):
            if relu2:
                outs[0][...] = acc.astype(outs[0].dtype)
                r = jnp.maximum(acc, 0.0)
                outs[1][...] = (r * r).astype(outs[1].dtype)
            else:
                outs[0][...] = acc.astype(outs[0].dtype)

        if ksteps == 1:
            finish(_dot(a_ref[...], b_ref[...]))
        else:
            acc_ref = scr[0]
            k = pl.program_id(2)

            @pl.when(k == 0)
            def _():
                acc_ref[...] = jnp.zeros_like(acc_ref)

            acc_ref[...] += _dot(a_ref[...], b_ref[...])

            @pl.when(k == ksteps - 1)
            def _():
                finish(acc_ref[...])

    o_spec = pl.BlockSpec((tm, tn), lambda i, j, k: (i, j))
    o_shape = jax.ShapeDtypeStruct((M, nsh * nk), out_dtype)
    return pl.pallas_call(
        body, name=name,
        grid=(M // tm, nsh * nb, ksteps),
        in_specs=[pl.BlockSpec((tm, tk), lambda i, j, k: (i, k)),
                  pl.BlockSpec((None, tk, tn), lambda i, j, k: (j // nb, k, j % nb))],
        out_specs=(o_spec, o_spec) if relu2 else o_spec,
        out_shape=(o_shape, o_shape) if relu2 else o_shape,
        scratch_shapes=[] if ksteps == 1 else [pltpu.VMEM((tm, tn), F32)],
        compiler_params=_params(("parallel", "parallel", "arbitrary")),
    )(a, b3)


def mm_ln(a, w, resid, g, b, *, name, alpha, tm=512, tk=512):
    M, K = a.shape
    D = w.shape[1]
    tm, tk = _tile(M, tm, 8), _tile(K, tk)
    ksteps = K // tk

    def body(a_ref, w_ref, r_ref, g_ref, b_ref, z_ref, xn_ref, xb_ref, acc_ref):
        k = pl.program_id(1)

        @pl.when(k == 0)
        def _():
            acc_ref[...] = jnp.zeros_like(acc_ref)

        acc_ref[...] += _dot(a_ref[...], w_ref[...])

        @pl.when(k == ksteps - 1)
        def _():
            z = alpha * r_ref[...] + acc_ref[...]
            mu = jnp.mean(z, axis=-1, keepdims=True)
            zc = z - mu
            var = jnp.mean(zc * zc, axis=-1, keepdims=True)
            xn = zc * lax.rsqrt(var + LN_EPS) * g_ref[...] + b_ref[...]
            z_ref[...] = z
            xn_ref[...] = xn
            xb_ref[...] = xn.astype(BF16)

    row = pl.BlockSpec((tm, D), lambda i, k: (i, 0))
    vec = pl.BlockSpec((1, D), lambda i, k: (0, 0))
    return pl.pallas_call(
        body, name=name,
        grid=(M // tm, ksteps),
        in_specs=[pl.BlockSpec((tm, tk), lambda i, k: (i, k)),
                  pl.BlockSpec((tk, D), lambda i, k: (k, 0)), row, vec, vec],
        out_specs=(row, row, row),
        out_shape=(jax.ShapeDtypeStruct((M, D), F32), jax.ShapeDtypeStruct((M, D), F32),
                   jax.ShapeDtypeStruct((M, D), BF16)),
        scratch_shapes=[pltpu.VMEM((tm, D), F32)],
        compiler_params=_params(("parallel", "arbitrary")),
    )(a, w, resid, g, b)


def mm_nt(a, b3, *, name, out_dtype, resid=None, alpha=1.0, umul=None, tm=512, tko=1024, tr=2048):
    M, N = a.shape
    nsh, K, nk = b3.shape
    tm, tko, tr = _tile(M, tm, 8), _tile(K, tko), _tile(nk, tr)
    nb = nk // tr
    rsteps = nsh * nb

    def body(a_ref, b_ref, *rest):
        rest = list(rest)
        r_ref = rest.pop(0) if resid is not None else None
        u_ref = rest.pop(0) if umul is not None else None
        o_ref = rest.pop(0)

        def finish(acc):
            if r_ref is not None:
                acc = acc + alpha * r_ref[...]
            if u_ref is not None:
                acc = acc * (2.0 * jnp.maximum(u_ref[...].astype(F32), 0.0))
            o_ref[...] = acc.astype(o_ref.dtype)

        if rsteps == 1:
            finish(_dot_nt(a_ref[...], b_ref[...]))
        else:
            acc_ref = rest[0]
            r = pl.program_id(2)

            @pl.when(r == 0)
            def _():
                acc_ref[...] = jnp.zeros_like(acc_ref)

            acc_ref[...] += _dot_nt(a_ref[...], b_ref[...])

            @pl.when(r == rsteps - 1)
            def _():
                finish(acc_ref[...])

    o_spec = pl.BlockSpec((tm, tko), lambda i, j, r: (i, j))
    in_specs = [pl.BlockSpec((tm, tr), lambda i, j, r: (i, r)),
                pl.BlockSpec((None, tko, tr), lambda i, j, r: (r // nb, j, r % nb))]
    args = [a, b3]
    for extra in (resid, umul):
        if extra is not None:
            in_specs.append(o_spec)
            args.append(extra)
    return pl.pallas_call(
        body, name=name,
        grid=(M // tm, K // tko, rsteps),
        in_specs=in_specs, out_specs=o_spec,
        out_shape=jax.ShapeDtypeStruct((M, K), out_dtype),
        scratch_shapes=[] if rsteps == 1 else [pltpu.VMEM((tm, tko), F32)],
        compiler_params=_params(("parallel", "parallel", "arbitrary")),
    )(*args)


def mm_tn(a, g, nsh, *, name, tk=1024, tn=1024, tm=512):
    M, K = a.shape
    N = g.shape[1]
    nk = N // nsh
    tk, tn, tm = _tile(K, tk), _tile(nk, tn), _tile(M, tm, 8)
    nb, msteps = nk // tn, M // tm

    def body(a_ref, g_ref, o_ref, acc_ref):
        m = pl.program_id(2)

        @pl.when(m == 0)
        def _():
            acc_ref[...] = jnp.zeros_like(acc_ref)

        acc_ref[...] += _dot_tn(a_ref[...], g_ref[...])

        @pl.when(m == msteps - 1)
        def _():
            o_ref[...] = acc_ref[...].astype(o_ref.dtype)

    return pl.pallas_call(
        body, name=name,
        grid=(K // tk, nsh * nb, msteps),
        in_specs=[pl.BlockSpec((tm, tk), lambda i, j, m: (m, i)),
                  pl.BlockSpec((tm, tn), lambda i, j, m: (m, j))],
        out_specs=pl.BlockSpec((None, tk, tn), lambda i, j, m: (j // nb, i, j % nb)),
        out_shape=jax.ShapeDtypeStruct((nsh, K, nk), BF16),
        scratch_shapes=[pltpu.VMEM((tk, tn), F32)],
        compiler_params=_params(("parallel", "parallel", "arbitrary")),
    )(a, g)


def _fold8(t):
    return t.reshape(t.shape[0] // 8, 8, t.shape[1]).sum(axis=0)


def ln_bwd(dy, z, g, *, name, tm=256):
    M, D = z.shape
    tm = _tile(M, tm, 8)

    def body(dy_ref, z_ref, g_ref, dz_ref, dzb_ref, dg_ref, db_ref):
        i = pl.program_id(0)
        z_ = z_ref[...]
        dy_ = dy_ref[...]
        mu = jnp.mean(z_, axis=-1, keepdims=True)
        zc = z_ - mu
        var = jnp.mean(zc * zc, axis=-1, keepdims=True)
        rstd = lax.rsqrt(var + LN_EPS)
        xh = zc * rstd
        dxh = dy_ * g_ref[...]
        m1 = jnp.mean(dxh, axis=-1, keepdims=True)
        m2 = jnp.mean(dxh * xh, axis=-1, keepdims=True)
        dz = rstd * (dxh - m1 - xh * m2)
        dz_ref[...] = dz
        dzb_ref[...] = dz.astype(BF16)

        @pl.when(i == 0)
        def _():
            dg_ref[...] = jnp.zeros_like(dg_ref)
            db_ref[...] = jnp.zeros_like(db_ref)

        dg_ref[...] += _fold8(dy_ * xh)
        db_ref[...] += _fold8(dy_)

    row = pl.BlockSpec((tm, D), lambda i: (i, 0))
    acc = pl.BlockSpec((8, D), lambda i: (0, 0))
    return pl.pallas_call(
        body, name=name, grid=(M // tm,),
        in_specs=[row, row, pl.BlockSpec((1, D), lambda i: (0, 0))],
        out_specs=(row, row, acc, acc),
        out_shape=(jax.ShapeDtypeStruct((M, D), F32), jax.ShapeDtypeStruct((M, D), BF16),
                   jax.ShapeDtypeStruct((8, D), F32), jax.ShapeDtypeStruct((8, D), F32)),
        compiler_params=_params(("arbitrary",)),
    )(dy, z, g)


def rms_fwd(ya, yb, ga, gb, *, name, tm=512):
    M, WA = ya.shape
    WB = yb.shape[1]
    tm = _tile(M, tm, 8)

    def body(ya_ref, yb_ref, ga_ref, gb_ref, o_ref):
        for y_ref, g_ref, lo, w in ((ya_ref, ga_ref, 0, WA), (yb_ref, gb_ref, WA, WB)):
            y = y_ref[...]
            r = lax.rsqrt(jnp.mean(y * y, axis=-1, keepdims=True) + RMS_EPS)
            o_ref[:, lo:lo + w] = (y * r * g_ref[...]).astype(o_ref.dtype)

    return pl.pallas_call(
        body, name=name, grid=(M // tm,),
        in_specs=[pl.BlockSpec((tm, WA), lambda i: (i, 0)), pl.BlockSpec((tm, WB), lambda i: (i, 0)),
                  pl.BlockSpec((1, WA), lambda i: (0, 0)), pl.BlockSpec((1, WB), lambda i: (0, 0))],
        out_specs=pl.BlockSpec((tm, WA + WB), lambda i: (i, 0)),
        out_shape=jax.ShapeDtypeStruct((M, WA + WB), BF16),
        compiler_params=_params(("parallel",)),
    )(ya, yb, ga, gb)


def rms_bwd(dy, ya, yb, ga, gb, *, name, tm=512):
    M, WA = ya.shape
    WB = yb.shape[1]
    tm = _tile(M, tm, 8)

    def body(dy_ref, ya_ref, yb_ref, ga_ref, gb_ref, dya_ref, dyb_ref, dga_ref, dgb_ref):
        i = pl.program_id(0)

        @pl.when(i == 0)
        def _():
            dga_ref[...] = jnp.zeros_like(dga_ref)
            dgb_ref[...] = jnp.zeros_like(dgb_ref)

        for y_ref, g_ref, d_ref, dgr, lo, w in ((ya_ref, ga_ref, dya_ref, dga_ref, 0, WA),
                                                (yb_ref, gb_ref, dyb_ref, dgb_ref, WA, WB)):
            y = y_ref[...]
            d = dy_ref[:, lo:lo + w]
            r = lax.rsqrt(jnp.mean(y * y, axis=-1, keepdims=True) + RMS_EPS)
            n = y * r
            dn = d * g_ref[...]
            d_ref[...] = r * (dn - n * jnp.mean(dn * n, axis=-1, keepdims=True))
            dgr[...] += _fold8(d * n)

    return pl.pallas_call(
        body, name=name, grid=(M // tm,),
        in_specs=[pl.BlockSpec((tm, WA + WB), lambda i: (i, 0)),
                  pl.BlockSpec((tm, WA), lambda i: (i, 0)), pl.BlockSpec((tm, WB), lambda i: (i, 0)),
                  pl.BlockSpec((1, WA), lambda i: (0, 0)), pl.BlockSpec((1, WB), lambda i: (0, 0))],
        out_specs=(pl.BlockSpec((tm, WA), lambda i: (i, 0)), pl.BlockSpec((tm, WB), lambda i: (i, 0)),
                   pl.BlockSpec((8, WA), lambda i: (0, 0)), pl.BlockSpec((8, WB), lambda i: (0, 0))),
        out_shape=(jax.ShapeDtypeStruct((M, WA), F32), jax.ShapeDtypeStruct((M, WB), F32),
                   jax.ShapeDtypeStruct((8, WA), F32), jax.ShapeDtypeStruct((8, WB), F32)),
        compiler_params=_params(("arbitrary",)),
    )(dy, ya, yb, ga, gb)


def loss_head(y, target, *, name, tm=512):
    M, D = y.shape
    tm = _tile(M, tm, 8)

    def body(y_ref, t_ref, dy_ref, l_ref):
        i = pl.program_id(0)

        @pl.when(i == 0)
        def _():
            l_ref[...] = jnp.zeros_like(l_ref)

        e = y_ref[...] - t_ref[...]
        dy_ref[...] = e * (1.0 / D)
        l_ref[...] += _fold8(e * e)

    row = pl.BlockSpec((tm, D), lambda i: (i, 0))
    return pl.pallas_call(
        body, name=name, grid=(M // tm,),
        in_specs=[row, row],
        out_specs=(row, pl.BlockSpec((8, D), lambda i: (0, 0))),
        out_shape=(jax.ShapeDtypeStruct((M, D), F32), jax.ShapeDtypeStruct((8, D), F32)),
        compiler_params=_params(("arbitrary",)),
    )(y, target)


def _lane(shape):
    return lax.broadcasted_iota(jnp.int32, shape, len(shape) - 1)


def _swap(t, half, period):
    first = (_lane(t.shape) % period) < half
    return jnp.where(first, pltpu.roll(t, LANES - half, 1), pltpu.roll(t, half, 1))


def _rope(t, c, s, half, period):
    return t * c + _swap(t, half, period) * s


def _rope_t(g, c, s, half, period):
    return g * c - _swap(g, half, period) * s


def rope_tables(positions, rot_dim, period):
    half = rot_dim // 2
    inv_freq = ROPE_THETA ** (-jnp.arange(0, rot_dim, 2, dtype=F32) / rot_dim)
    ang = positions.astype(F32)[:, None] * inv_freq
    cos, sin = jnp.cos(ang), jnp.sin(ang)
    ones = jnp.ones((positions.shape[0], period - rot_dim), F32)
    c = jnp.concatenate([cos, cos, ones], axis=1)
    s = jnp.concatenate([-sin, sin, 0.0 * ones], axis=1)
    reps = LANES // period
    return jnp.tile(c, (1, reps)), jnp.tile(s, (1, reps))


def _branch_blocks(S):
    out = []
    for window, d in DILATED_BRANCHES:
        assert window // d == BLOCK and S % (d * BLOCK) == 0
        out.append((d, (S // d) // BLOCK))
    return out


def _rows(r, n, d):
    return pl.ds(r + n * (BLOCK * d), BLOCK, stride=d) if d > 1 else pl.ds(pl.multiple_of(n * BLOCK, BLOCK), BLOCK)


def _band_masks(n, strict_prev):
    qi = lax.broadcasted_iota(jnp.int32, (BLOCK, BLOCK), 0)
    kj = lax.broadcasted_iota(jnp.int32, (BLOCK, BLOCK), 1)
    cur = kj <= qi
    prev = ((kj > qi) if strict_prev else (kj >= qi)) & (n > 0)
    return cur, prev


def attn_a_fwd(h, ca, sa, *, n_heads, name):
    S = h.shape[0]
    scale = HEAD_DIM_A ** -0.5
    half = HEAD_DIM_A // 8
    branches = _branch_blocks(S)

    def body(q_ref, k_ref, v_ref, c_ref, s_ref, y_ref, lse_ref, qs, ks, m_s, l_s, acc_s):
        qs[...] = _rope(q_ref[...], c_ref[...], s_ref[...], half, HEAD_DIM_A)
        ks[...] = _rope(k_ref[...], c_ref[...], s_ref[...], half, HEAD_DIM_A)
        for bi, (d, nb) in enumerate(branches):
            def blk(idx, carry, bi=bi, d=d, nb=nb):
                r, n = idx // nb, idx % nb
                rc, rp = _rows(r, n, d), _rows(r, jnp.maximum(n - 1, 0), d)
                q = qs[rc, :].astype(BF16)
                cur, prev = _band_masks(n, False)
                sc = jnp.where(cur, _dot_nt(q, ks[rc, :].astype(BF16)) * scale, NEG_INF)
                sp = jnp.where(prev, _dot_nt(q, ks[rp, :].astype(BF16)) * scale, NEG_INF)
                m = jnp.maximum(jnp.max(sc, axis=-1, keepdims=True), jnp.max(sp, axis=-1, keepdims=True))
                pc, pp = jnp.exp(sc - m), jnp.exp(sp - m)
                l = jnp.sum(pc, axis=-1, keepdims=True) + jnp.sum(pp, axis=-1, keepdims=True)
                acc = _dot(pc.astype(BF16), v_ref[rc, :].astype(BF16)) + _dot(pp.astype(BF16), v_ref[rp, :].astype(BF16))
                mb = jnp.broadcast_to(m, (BLOCK, LANES))
                lb = jnp.broadcast_to(l, (BLOCK, LANES))
                if bi == 0:
                    m_s[rc, :], l_s[rc, :], acc_s[rc, :] = mb, lb, acc
                else:
                    m0 = m_s[rc, :]
                    mn = jnp.maximum(m0, mb)
                    a0, a1 = jnp.exp(m0 - mn), jnp.exp(mb - mn)
                    m_s[rc, :] = mn
                    l_s[rc, :] = l_s[rc, :] * a0 + lb * a1
                    acc_s[rc, :] = acc_s[rc, :] * a0 + acc * a1
                return carry

            lax.fori_loop(0, d * nb, blk, 0)
        y_ref[...] = acc_s[...] / l_s[...]
        lse_ref[...] = m_s[...] + jnp.log(l_s[...])

    col = lambda off: pl.BlockSpec((S, LANES), lambda hd: (0, off + hd))
    full = pl.BlockSpec((S, LANES), lambda hd: (0, 0))
    out = pl.BlockSpec((S, LANES), lambda hd: (0, hd))
    o_shape = jax.ShapeDtypeStruct((S, n_heads * LANES), F32)
    return pl.pallas_call(
        body, name=name, grid=(n_heads,),
        in_specs=[col(0), col(n_heads), col(2 * n_heads), full, full],
        out_specs=(out, out), out_shape=(o_shape, o_shape),
        scratch_shapes=[pltpu.VMEM((S, LANES), F32) for _ in range(5)],
        compiler_params=_params(("arbitrary",)),
    )(h, h, h, ca, sa)


def attn_a_bwd(h, ca, sa, ya, lse, dya, *, n_heads, name):
    S = h.shape[0]
    scale = HEAD_DIM_A ** -0.5
    half = HEAD_DIM_A // 8
    branches = _branch_blocks(S)

    def body(q_ref, k_ref, v_ref, c_ref, s_ref, y_ref, lse_ref, dy_ref, o_ref, qs, ks, dq_s, dk_s, dv_s):
        part = pl.program_id(1)

        @pl.when(part == 0)
        def _():
            qs[...] = _rope(q_ref[...], c_ref[...], s_ref[...], half, HEAD_DIM_A)
            ks[...] = _rope(k_ref[...], c_ref[...], s_ref[...], half, HEAD_DIM_A)
            dq_s[...] = jnp.zeros_like(dq_s)
            dk_s[...] = jnp.zeros_like(dk_s)
            dv_s[...] = jnp.zeros_like(dv_s)
            for d, nb in branches:
                def blk(idx, carry, d=d, nb=nb):
                    r, n = idx // nb, idx % nb
                    rc, rp = _rows(r, n, d), _rows(r, jnp.maximum(n - 1, 0), d)
                    q = qs[rc, :].astype(BF16)
                    dy = dy_ref[rc, :]
                    dsum = jnp.sum(dy * y_ref[rc, :], axis=-1, keepdims=True)
                    dyb = dy.astype(BF16)
                    lse_b = lse_ref[rc, :]
                    cur, prev = _band_masks(n, False)
                    dq = jnp.zeros((BLOCK, LANES), F32)
                    for rows, mask in ((rc, cur), (rp, prev)):
                        kb = ks[rows, :].astype(BF16)
                        vb = v_ref[rows, :].astype(BF16)
                        s = jnp.where(mask, _dot_nt(q, kb) * scale, NEG_INF)
                        p = jnp.exp(s - lse_b)
                        ds = (p * (_dot_nt(dyb, vb) - dsum) * scale).astype(BF16)
                        dv_s[rows, :] += _dot_tn(p.astype(BF16), dyb)
                        dk_s[rows, :] += _dot_tn(ds, q)
                        dq = dq + _dot(ds, kb)
                    dq_s[rc, :] += dq
                    return carry

                lax.fori_loop(0, d * nb, blk, 0)
            o_ref[...] = _rope_t(dq_s[...], c_ref[...], s_ref[...], half, HEAD_DIM_A).astype(o_ref.dtype)

        @pl.when(part == 1)
        def _():
            o_ref[...] = _rope_t(dk_s[...], c_ref[...], s_ref[...], half, HEAD_DIM_A).astype(o_ref.dtype)

        @pl.when(part == 2)
        def _():
            o_ref[...] = dv_s[...].astype(o_ref.dtype)

    col = lambda off: pl.BlockSpec((S, LANES), lambda hd, p: (0, off + hd))
    full = pl.BlockSpec((S, LANES), lambda hd, p: (0, 0))
    per_head = pl.BlockSpec((S, LANES), lambda hd, p: (0, hd))
    return pl.pallas_call(
        body, name=name, grid=(n_heads, 3),
        in_specs=[col(0), col(n_heads), col(2 * n_heads), full, full, per_head, per_head, per_head],
        out_specs=pl.BlockSpec((S, LANES), lambda hd, p: (0, p * n_heads + hd)),
        out_shape=jax.ShapeDtypeStruct((S, 3 * n_heads * LANES), BF16),
        scratch_shapes=[pltpu.VMEM((S, LANES), F32) for _ in range(5)],
        compiler_params=_params(("arbitrary", "arbitrary")),
    )(h, h, h, ca, sa, ya, lse, dya)


def _both_halves(t, g):
    low = _lane(t.shape) < HEAD_DIM_B
    return jnp.where(low == (g == 0), t, pltpu.roll(t, HEAD_DIM_B, 1))


def _stack_heads(t):
    low = _lane(t.shape) < HEAD_DIM_B
    return jnp.concatenate([jnp.where(low, t, 0.0), jnp.where(low, 0.0, t)], axis=0)


def _unstack_heads(t2):
    low = _lane((BLOCK, LANES)) < HEAD_DIM_B
    return jnp.where(low, t2[:BLOCK], t2[BLOCK:])


def _fold_halves(t):
    return t + pltpu.roll(t, HEAD_DIM_B, 1)


def _head_rows(ref, tile):
    a = ref[pl.ds(2 * tile, 1), :][:, :1]
    b = ref[pl.ds(2 * tile + 1, 1), :][:, :1]
    return jnp.concatenate([jnp.broadcast_to(a, (BLOCK, 1)), jnp.broadcast_to(b, (BLOCK, 1))], axis=0)


def attn_b_fwd(h, cb, sb, sinks_l, *, off_q, n_qtiles, name):
    S = h.shape[0]
    nblk = S // BLOCK
    scale = HEAD_DIM_B ** -0.5
    half = HEAD_DIM_B // 8
    tiles_per_group = n_qtiles // 2

    def body(q_ref, k_ref, v_ref, c_ref, s_ref, sink_ref, y_ref, lse_ref, qs, kg, vg):
        t = pl.program_id(0)
        g = t // tiles_per_group
        qs[...] = _rope(q_ref[...], c_ref[...], s_ref[...], half, HEAD_DIM_B)
        kg[...] = _both_halves(_rope(k_ref[...], c_ref[...], s_ref[...], half, HEAD_DIM_B), g)
        vg[...] = _both_halves(v_ref[...], g)
        sink = _head_rows(sink_ref, t)

        def blk(n, carry):
            rc = pl.ds(pl.multiple_of(n * BLOCK, BLOCK), BLOCK)
            rp = pl.ds(pl.multiple_of(jnp.maximum(n - 1, 0) * BLOCK, BLOCK), BLOCK)
            q2 = _stack_heads(qs[rc, :]).astype(BF16)
            cur, prev = _band_masks(n, True)
            cur2, prev2 = jnp.concatenate([cur, cur], 0), jnp.concatenate([prev, prev], 0)
            sc = jnp.where(cur2, _dot_nt(q2, kg[rc, :].astype(BF16)) * scale, NEG_INF)
            sp = jnp.where(prev2, _dot_nt(q2, kg[rp, :].astype(BF16)) * scale, NEG_INF)
            m = jnp.maximum(jnp.max(sc, axis=-1, keepdims=True), jnp.max(sp, axis=-1, keepdims=True))
            pc, pp = jnp.exp(sc - m), jnp.exp(sp - m)
            l = jnp.sum(pc, axis=-1, keepdims=True) + jnp.sum(pp, axis=-1, keepdims=True)
            acc = _dot(pc.astype(BF16), vg[rc, :].astype(BF16)) + _dot(pp.astype(BF16), vg[rp, :].astype(BF16))
            m2 = jnp.maximum(m, sink)
            c = jnp.exp(m - m2)
            den = l * c + jnp.exp(sink - m2)
            y_ref[rc, :] = _unstack_heads(acc * (c / den))
            lse_ref[rc, :] = _unstack_heads(jnp.broadcast_to(m2 + jnp.log(den), (2 * BLOCK, LANES)))
            return carry

        lax.fori_loop(0, nblk, blk, 0)

    full = lambda col: pl.BlockSpec((S, LANES), lambda t: (0, col))
    out = pl.BlockSpec((S, LANES), lambda t: (0, t))
    o_shape = jax.ShapeDtypeStruct((S, n_qtiles * LANES), F32)
    return pl.pallas_call(
        body, name=name, grid=(n_qtiles,),
        in_specs=[pl.BlockSpec((S, LANES), lambda t: (0, off_q + t)), full(off_q + n_qtiles), full(off_q + n_qtiles + 1),
                  full(0), full(0), pl.BlockSpec(sinks_l.shape, lambda t: (0, 0))],
        out_specs=(out, out), out_shape=(o_shape, o_shape),
        scratch_shapes=[pltpu.VMEM((S, LANES), F32) for _ in range(3)],
        compiler_params=_params(("arbitrary",)),
    )(h, h, h, cb, sb, sinks_l)


def attn_b_bwd(h, cb, sb, sinks_l, yb, lse, dyb, *, off_q, n_qtiles, name):
    S = h.shape[0]
    nblk = S // BLOCK
    scale = HEAD_DIM_B ** -0.5
    half = HEAD_DIM_B // 8
    tiles_per_group = n_qtiles // 2
    n_steps = n_qtiles + 2

    def body(q_ref, k_ref, v_ref, c_ref, s_ref, sink_ref, y_ref, lse_ref, dy_ref,
             dq_ref, dkv_ref, dsink_ref, qs, kg, vg, dk_acc, dv_acc):
        t = pl.program_id(0)

        @pl.when(t == 0)
        def _():
            dk_acc[...] = jnp.zeros_like(dk_acc)
            dv_acc[...] = jnp.zeros_like(dv_acc)
            dsink_ref[...] = jnp.zeros_like(dsink_ref)

        @pl.when(t < n_qtiles)
        def _():
            g = t // tiles_per_group
            qs[...] = _rope(q_ref[...], c_ref[...], s_ref[...], half, HEAD_DIM_B)
            kg[...] = _both_halves(_rope(k_ref[...], c_ref[...], s_ref[...], half, HEAD_DIM_B), g)
            vg[...] = _both_halves(v_ref[...], g)
            sink = _head_rows(sink_ref, t)

            def blk(n, dsink):
                rc = pl.ds(pl.multiple_of(n * BLOCK, BLOCK), BLOCK)
                rp = pl.ds(pl.multiple_of(jnp.maximum(n - 1, 0) * BLOCK, BLOCK), BLOCK)
                q2 = _stack_heads(qs[rc, :]).astype(BF16)
                dy2 = _stack_heads(dy_ref[rc, :])
                dsum = jnp.sum(dy2 * _stack_heads(y_ref[rc, :]), axis=-1, keepdims=True)
                dy2b = dy2.astype(BF16)
                lse_t = lse_ref[rc, :]
                lse2 = jnp.concatenate([lse_t[:, :1], lse_t[:, HEAD_DIM_B:HEAD_DIM_B + 1]], axis=0)
                cur, prev = _band_masks(n, True)
                dq2 = jnp.zeros((2 * BLOCK, LANES), F32)
                for rows, mask in ((rc, cur), (rp, prev)):
                    kb = kg[rows, :].astype(BF16)
                    vb = vg[rows, :].astype(BF16)
                    mask2 = jnp.concatenate([mask, mask], 0)
                    s = jnp.where(mask2, _dot_nt(q2, kb) * scale, NEG_INF)
                    p = jnp.exp(s - lse2)
                    ds = (p * (_dot_nt(dy2b, vb) - dsum) * scale).astype(BF16)
                    dv_acc[g, rows, :] += _fold_halves(_dot_tn(p.astype(BF16), dy2b))
                    dk_acc[g, rows, :] += _fold_halves(_dot_tn(ds, q2))
                    dq2 = dq2 + _dot(ds, kb)
                dq_ref[rc, :] = _unstack_heads(dq2)
                return dsink - jnp.exp(sink - lse2) * dsum

            dsink = lax.fori_loop(0, nblk, blk, jnp.zeros((2 * BLOCK, 1), F32))
            dq_ref[...] = _rope_t(dq_ref[...], c_ref[...], s_ref[...], half, HEAD_DIM_B)
            d0 = jnp.sum(dsink[:BLOCK], axis=0, keepdims=True)
            d1 = jnp.sum(dsink[BLOCK:], axis=0, keepdims=True)
            dsink_ref[pl.ds(2 * t, 1), :] = jnp.broadcast_to(d0, (1, LANES))
            dsink_ref[pl.ds(2 * t + 1, 1), :] = jnp.broadcast_to(d1, (1, LANES))

        low = _lane((S, LANES)) < HEAD_DIM_B

        @pl.when(t == n_qtiles)
        def _():
            dk = jnp.where(low, dk_acc[0], dk_acc[1])
            dkv_ref[...] = _rope_t(dk, c_ref[...], s_ref[...], half, HEAD_DIM_B)

        @pl.when(t == n_qtiles + 1)
        def _():
            dkv_ref[...] = jnp.where(low, dv_acc[0], dv_acc[1])

    qt = lambda t: jnp.minimum(t, n_qtiles - 1)
    full = lambda col: pl.BlockSpec((S, LANES), lambda t: (0, col))
    per_tile = pl.BlockSpec((S, LANES), lambda t: (0, qt(t)))
    return pl.pallas_call(
        body, name=name, grid=(n_steps,),
        in_specs=[pl.BlockSpec((S, LANES), lambda t: (0, off_q + qt(t))), full(off_q + n_qtiles),
                  full(off_q + n_qtiles + 1), full(0), full(0), pl.BlockSpec(sinks_l.shape, lambda t: (0, 0)),
                  per_tile, per_tile, per_tile],
        out_specs=(per_tile, pl.BlockSpec((S, LANES), lambda t: (0, jnp.maximum(t - n_qtiles, 0))),
                   pl.BlockSpec(sinks_l.shape, lambda t: (0, 0))),
        out_shape=(jax.ShapeDtypeStruct((S, n_qtiles * LANES), F32), jax.ShapeDtypeStruct((S, 2 * LANES), F32),
                   jax.ShapeDtypeStruct(sinks_l.shape, F32)),
        scratch_shapes=[pltpu.VMEM((S, LANES), F32) for _ in range(3)]
        + [pltpu.VMEM((2, S, LANES), F32), pltpu.VMEM((2, S, LANES), F32)],
        compiler_params=_params(("arbitrary",)),
    )(h, h, h, cb, sb, sinks_l, yb, lse, dyb)


def _win(r, nw, d):
    if d > 1:
        return pl.ds(r + nw * (BLOCK * d), 2 * BLOCK, stride=d)
    return pl.ds(pl.multiple_of(nw * BLOCK, BLOCK), 2 * BLOCK)


def _fwd_bias(strict_prev):
    qi = lax.broadcasted_iota(jnp.int32, (BLOCK, 2 * BLOCK), 0)
    kj = lax.broadcasted_iota(jnp.int32, (BLOCK, 2 * BLOCK), 1)
    first = kj < BLOCK
    kk = jnp.where(first, kj, kj - BLOCK)
    prev_ok = (kk > qi) if strict_prev else (kk >= qi)
    zero = first & (kk <= qi)
    mid = (first & prev_ok) | (jnp.logical_not(first) & (kk <= qi))
    return jnp.stack([jnp.where(zero, 0.0, NEG_INF), jnp.where(mid, 0.0, NEG_INF)])


def _bwd_bias(strict_prev):
    qi = lax.broadcasted_iota(jnp.int32, (2 * BLOCK, BLOCK), 0)
    kj = lax.broadcasted_iota(jnp.int32, (2 * BLOCK, BLOCK), 1)
    first = qi < BLOCK
    qq = jnp.where(first, qi, qi - BLOCK)
    prev_ok = (kj > qq) if strict_prev else (kj >= qq)
    mid = (first & (kj <= qq)) | (jnp.logical_not(first) & prev_ok)
    last = jnp.logical_not(first) & (kj <= qq)
    return jnp.stack([jnp.where(mid, 0.0, NEG_INF), jnp.where(last, 0.0, NEG_INF)])


UNROLL = 4


def attn_a_fwd(h, ca, sa, *, n_heads, name):
    S = h.shape[0]
    scale = HEAD_DIM_A ** -0.5
    half = HEAD_DIM_A // 8
    branches = _branch_blocks(S)

    def body(q_ref, k_ref, v_ref, c_ref, s_ref, y_ref, lse_ref, qs, ks, m_s, l_s, acc_s, bias):
        qs[...] = _rope(q_ref[...], c_ref[...], s_ref[...], half, HEAD_DIM_A)
        ks[...] = _rope(k_ref[...], c_ref[...], s_ref[...], half, HEAD_DIM_A)
        bias[...] = _fwd_bias(False)
        for bi, (d, nb) in enumerate(branches):
            assert nb >= 2 and (d * nb) % UNROLL == 0

            def blk(it, carry, bi=bi, d=d, nb=nb):
                rn = [((it * UNROLL + u) // nb, (it * UNROLL + u) % nb) for u in range(UNROLL)]
                rcs = [_rows(r, n, d) for r, n in rn]
                rws = [_win(r, jnp.maximum(n - 1, 0), d) for r, n in rn]
                ss = [_dot_nt(qs[rc, :].astype(BF16), ks[rw, :].astype(BF16)) * scale + bias[jnp.minimum(n, 1)]
                      for (r, n), rc, rw in zip(rn, rcs, rws)]
                ms = [jnp.max(s, axis=-1, keepdims=True) for s in ss]
                ps = [jnp.exp(s - m) for s, m in zip(ss, ms)]
                ls = [jnp.sum(p, axis=-1, keepdims=True) for p in ps]
                accs = [_dot(p.astype(BF16), v_ref[rw, :].astype(BF16)) for p, rw in zip(ps, rws)]
                for rc, m, l, acc in zip(rcs, ms, ls, accs):
                    mb = jnp.broadcast_to(m, (BLOCK, LANES))
                    lb = jnp.broadcast_to(l, (BLOCK, LANES))
                    if bi == 0:
                        m_s[rc, :], l_s[rc, :], acc_s[rc, :] = mb, lb, acc
                    else:
                        m0 = m_s[rc, :]
                        mn = jnp.maximum(m0, mb)
                        a0, a1 = jnp.exp(m0 - mn), jnp.exp(mb - mn)
                        m_s[rc, :] = mn
                        l_s[rc, :] = l_s[rc, :] * a0 + lb * a1
                        acc_s[rc, :] = acc_s[rc, :] * a0 + acc * a1
                return carry

            lax.fori_loop(0, d * nb // UNROLL, blk, 0)
        y_ref[...] = acc_s[...] / l_s[...]
        lse_ref[...] = m_s[...] + jnp.log(l_s[...])

    col = lambda off: pl.BlockSpec((S, LANES), lambda hd: (0, off + hd))
    full = pl.BlockSpec((S, LANES), lambda hd: (0, 0))
    out = pl.BlockSpec((S, LANES), lambda hd: (0, hd))
    o_shape = jax.ShapeDtypeStruct((S, n_heads * LANES), F32)
    return pl.pallas_call(
        body, name=name, grid=(n_heads,),
        in_specs=[col(0), col(n_heads), col(2 * n_heads), full, full],
        out_specs=(out, out), out_shape=(o_shape, o_shape),
        scratch_shapes=[pltpu.VMEM((S, LANES), F32) for _ in range(5)] + [pltpu.VMEM((2, BLOCK, 2 * BLOCK), F32)],
        compiler_params=_params(("arbitrary",)),
    )(h, h, h, ca, sa)


def attn_a_bwd(h, ca, sa, ya, lse, dya, *, n_heads, name):
    S = h.shape[0]
    scale = HEAD_DIM_A ** -0.5
    half = HEAD_DIM_A // 8
    branches = _branch_blocks(S)

    def body(q_ref, k_ref, v_ref, c_ref, s_ref, y_ref, lse_ref, dy_ref, o_ref, qs, ks, dq_s, dk_s, dv_s, bias):
        part = pl.program_id(1)

        @pl.when(part == 0)
        def _():
            qs[...] = _rope(q_ref[...], c_ref[...], s_ref[...], half, HEAD_DIM_A)
            ks[...] = _rope(k_ref[...], c_ref[...], s_ref[...], half, HEAD_DIM_A)
            dq_s[...] = jnp.zeros_like(dq_s)
            bias[...] = _bwd_bias(False)
            for bi, (d, nb) in enumerate(branches):
                assert nb >= 2 and (d * nb) % UNROLL == 0

                def blk(it, carry, bi=bi, d=d, nb=nb):
                    rj = [((it * UNROLL + u) // nb, (it * UNROLL + u) % nb) for u in range(UNROLL)]
                    rks = [_rows(r, j, d) for r, j in rj]
                    rws = [_win(r, jnp.minimum(j, nb - 2), d) for r, j in rj]
                    q2 = [qs[rw, :].astype(BF16) for rw in rws]
                    kb = [ks[rk, :].astype(BF16) for rk in rks]
                    dy2 = [dy_ref[rw, :] for rw in rws]
                    dsum = [jnp.sum(dy * y_ref[rw, :], axis=-1, keepdims=True) for dy, rw in zip(dy2, rws)]
                    dy2b = [dy.astype(BF16) for dy in dy2]
                    ss = [_dot_nt(q, k) * scale + bias[(j == nb - 1).astype(jnp.int32)] for q, k, (r, j) in zip(q2, kb, rj)]
                    dps = [_dot_nt(dy, v_ref[rk, :].astype(BF16)) for dy, rk in zip(dy2b, rks)]
                    ps = [jnp.exp(s - lse_ref[rw, :]) for s, rw in zip(ss, rws)]
                    dss = [(p * (dp - dm) * scale).astype(BF16) for p, dp, dm in zip(ps, dps, dsum)]
                    dvs = [_dot_tn(p.astype(BF16), dy) for p, dy in zip(ps, dy2b)]
                    dks = [_dot_tn(ds, q) for ds, q in zip(dss, q2)]
                    dqs = [_dot(ds, k) for ds, k in zip(dss, kb)]
                    for rk, rw, dv, dk, dq in zip(rks, rws, dvs, dks, dqs):
                        if bi == 0:
                            dv_s[rk, :], dk_s[rk, :] = dv, dk
                        else:
                            dv_s[rk, :] += dv
                            dk_s[rk, :] += dk
                        dq_s[rw, :] += dq
                    return carry

                lax.fori_loop(0, d * nb // UNROLL, blk, 0)
            o_ref[...] = _rope_t(dq_s[...], c_ref[...], s_ref[...], half, HEAD_DIM_A).astype(o_ref.dtype)

        @pl.when(part == 1)
        def _():
            o_ref[...] = _rope_t(dk_s[...], c_ref[...], s_ref[...], half, HEAD_DIM_A).astype(o_ref.dtype)

        @pl.when(part == 2)
        def _():
            o_ref[...] = dv_s[...].astype(o_ref.dtype)

    col = lambda off: pl.BlockSpec((S, LANES), lambda hd, p: (0, off + hd))
    full = pl.BlockSpec((S, LANES), lambda hd, p: (0, 0))
    per_head = pl.BlockSpec((S, LANES), lambda hd, p: (0, hd))
    return pl.pallas_call(
        body, name=name, grid=(n_heads, 3),
        in_specs=[col(0), col(n_heads), col(2 * n_heads), full, full, per_head, per_head, per_head],
        out_specs=pl.BlockSpec((S, LANES), lambda hd, p: (0, p * n_heads + hd)),
        out_shape=jax.ShapeDtypeStruct((S, 3 * n_heads * LANES), BF16),
        scratch_shapes=[pltpu.VMEM((S, LANES), F32) for _ in range(5)] + [pltpu.VMEM((2, 2 * BLOCK, BLOCK), F32)],
        compiler_params=_params(("arbitrary", "arbitrary")),
    )(h, h, h, ca, sa, ya, lse, dya)


def _unstack_heads(t2):
    rows = t2.shape[0] // 2
    low = _lane((rows, LANES)) < HEAD_DIM_B
    return jnp.where(low, t2[:rows], t2[rows:])


def _head_rows(ref, tile, rows):
    a = ref[pl.ds(2 * tile, 1), :][:, :1]
    b = ref[pl.ds(2 * tile + 1, 1), :][:, :1]
    return jnp.concatenate([jnp.broadcast_to(a, (rows, 1)), jnp.broadcast_to(b, (rows, 1))], axis=0)


UNROLL_B = 2


def attn_b_fwd(h, cb, sb, sinks_l, *, off_q, n_qtiles, name):
    S = h.shape[0]
    nblk = S // BLOCK
    scale = HEAD_DIM_B ** -0.5
    half = HEAD_DIM_B // 8
    tiles_per_group = n_qtiles // 2
    assert nblk >= 2 and nblk % UNROLL_B == 0

    def body(q_ref, k_ref, v_ref, c_ref, s_ref, sink_ref, y_ref, lse_ref, qs, kg, vg, bias):
        t = pl.program_id(0)
        g = t // tiles_per_group
        qs[...] = _rope(q_ref[...], c_ref[...], s_ref[...], half, HEAD_DIM_B)

        @pl.when(t % tiles_per_group == 0)
        def _():
            kg[...] = _both_halves(_rope(k_ref[...], c_ref[...], s_ref[...], half, HEAD_DIM_B), g)
            vg[...] = _both_halves(v_ref[...], g)

        @pl.when(t == 0)
        def _():
            fb = _fwd_bias(True)
            bias[...] = jnp.concatenate([fb, fb], axis=1)

        sink = _head_rows(sink_ref, t, BLOCK)

        def blk(it, carry):
            ns = [it * UNROLL_B + u for u in range(UNROLL_B)]
            rcs = [pl.ds(pl.multiple_of(n * BLOCK, BLOCK), BLOCK) for n in ns]
            rws = [pl.ds(pl.multiple_of(jnp.maximum(n - 1, 0) * BLOCK, BLOCK), 2 * BLOCK) for n in ns]
            ss = [_dot_nt(_stack_heads(qs[rc, :]).astype(BF16), kg[rw, :].astype(BF16)) * scale + bias[jnp.minimum(n, 1)]
                  for n, rc, rw in zip(ns, rcs, rws)]
            ms = [jnp.max(s, axis=-1, keepdims=True) for s in ss]
            ps = [jnp.exp(s - m) for s, m in zip(ss, ms)]
            ls = [jnp.sum(p, axis=-1, keepdims=True) for p in ps]
            accs = [_dot(p.astype(BF16), vg[rw, :].astype(BF16)) for p, rw in zip(ps, rws)]
            for rc, m, l, acc in zip(rcs, ms, ls, accs):
                m2 = jnp.maximum(m, sink)
                c = jnp.exp(m - m2)
                den = l * c + jnp.exp(sink - m2)
                y_ref[rc, :] = _unstack_heads(acc * (c / den))
                lse_ref[rc, :] = _unstack_heads(jnp.broadcast_to(m2 + jnp.log(den), (2 * BLOCK, LANES)))
            return carry

        lax.fori_loop(0, nblk // UNROLL_B, blk, 0)

    full = lambda col: pl.BlockSpec((S, LANES), lambda t: (0, col))
    out = pl.BlockSpec((S, LANES), lambda t: (0, t))
    o_shape = jax.ShapeDtypeStruct((S, n_qtiles * LANES), F32)
    return pl.pallas_call(
        body, name=name, grid=(n_qtiles,),
        in_specs=[pl.BlockSpec((S, LANES), lambda t: (0, off_q + t)), full(off_q + n_qtiles), full(off_q + n_qtiles + 1),
                  full(0), full(0), pl.BlockSpec(sinks_l.shape, lambda t: (0, 0))],
        out_specs=(out, out), out_shape=(o_shape, o_shape),
        scratch_shapes=[pltpu.VMEM((S, LANES), F32) for _ in range(3)] + [pltpu.VMEM((2, 2 * BLOCK, 2 * BLOCK), F32)],
        compiler_params=_params(("arbitrary",)),
    )(h, h, h, cb, sb, sinks_l)


def attn_b_bwd(h, cb, sb, sinks_l, yb, lse, dyb, *, off_q, n_qtiles, name):
    S = h.shape[0]
    nblk = S // BLOCK
    scale = HEAD_DIM_B ** -0.5
    half = HEAD_DIM_B // 8
    tiles_per_group = n_qtiles // 2
    n_steps = n_qtiles + 2
    W = 2 * BLOCK
    assert nblk >= 2 and nblk % UNROLL_B == 0

    def body(q_ref, k_ref, v_ref, c_ref, s_ref, sink_ref, y_ref, lse_ref, dy_ref,
             dq_ref, dkv_ref, dsink_ref, qs, kg, vg, dk_acc, dv_acc, bias):
        t = pl.program_id(0)

        @pl.when(t == 0)
        def _():
            dk_acc[...] = jnp.zeros_like(dk_acc)
            dv_acc[...] = jnp.zeros_like(dv_acc)
            dsink_ref[...] = jnp.zeros_like(dsink_ref)
            bb = _bwd_bias(True)
            bias[...] = jnp.concatenate([bb, bb], axis=1)

        @pl.when(t < n_qtiles)
        def _():
            g = t // tiles_per_group
            qs[...] = _rope(q_ref[...], c_ref[...], s_ref[...], half, HEAD_DIM_B)

            @pl.when(t % tiles_per_group == 0)
            def _():
                kg[...] = _both_halves(_rope(k_ref[...], c_ref[...], s_ref[...], half, HEAD_DIM_B), g)
                vg[...] = _both_halves(v_ref[...], g)

            dq_ref[...] = jnp.zeros_like(dq_ref)
            sink = _head_rows(sink_ref, t, W)
            row = lax.broadcasted_iota(jnp.int32, (2 * W, 1), 0) % W
            low = _lane((W, LANES)) < HEAD_DIM_B

            def blk(it, dsink):
                js = [it * UNROLL_B + u for u in range(UNROLL_B)]
                rks = [pl.ds(pl.multiple_of(j * BLOCK, BLOCK), BLOCK) for j in js]
                rws = [pl.ds(pl.multiple_of(jnp.minimum(j, nblk - 2) * BLOCK, BLOCK), W) for j in js]
                q2 = [_stack_heads(qs[rw, :]).astype(BF16) for rw in rws]
                dy2 = [_stack_heads(dy_ref[rw, :]) for rw in rws]
                dsum = [jnp.sum(dy * _stack_heads(y_ref[rw, :]), axis=-1, keepdims=True) for dy, rw in zip(dy2, rws)]
                dy2b = [dy.astype(BF16) for dy in dy2]
                lse2 = []
                for rw in rws:
                    lt = lse_ref[rw, :]
                    lr = pltpu.roll(lt, HEAD_DIM_B, 1)
                    lse2.append(jnp.concatenate([jnp.where(low, lt, lr), jnp.where(low, lr, lt)], axis=0))
                kb = [kg[rk, :].astype(BF16) for rk in rks]
                ss = [_dot_nt(q, k) * scale + bias[(j == nblk - 1).astype(jnp.int32)] for q, k, j in zip(q2, kb, js)]
                dps = [_dot_nt(dy, vg[rk, :].astype(BF16)) for dy, rk in zip(dy2b, rks)]
                ps = [jnp.exp(s - l2) for s, l2 in zip(ss, lse2)]
                dss = [(p * (dp - dm) * scale).astype(BF16) for p, dp, dm in zip(ps, dps, dsum)]
                dvs = [_fold_halves(_dot_tn(p.astype(BF16), dy)) for p, dy in zip(ps, dy2b)]
                dks = [_fold_halves(_dot_tn(ds, q)) for ds, q in zip(dss, q2)]
                dqs = [_dot(ds, k) for ds, k in zip(dss, kb)]
                for j, rk, rw, dv, dk, dq, l2, dm in zip(js, rks, rws, dvs, dks, dqs, lse2, dsum):
                    dv_acc[g, rk, :] += dv
                    dk_acc[g, rk, :] += dk
                    dq_ref[rw, :] += _unstack_heads(dq)
                    diag = (row >= BLOCK).astype(jnp.int32) == (j == nblk - 1).astype(jnp.int32)
                    dsink = dsink - jnp.where(diag, jnp.exp(sink - l2[:, :1]) * dm, 0.0)
                return dsink

            dsink = lax.fori_loop(0, nblk // UNROLL_B, blk, jnp.zeros((2 * W, 1), F32))
            dq_ref[...] = _rope_t(dq_ref[...], c_ref[...], s_ref[...], half, HEAD_DIM_B)
            d0 = jnp.sum(dsink[:W], axis=0, keepdims=True)
            d1 = jnp.sum(dsink[W:], axis=0, keepdims=True)
            dsink_ref[pl.ds(2 * t, 1), :] = jnp.broadcast_to(d0, (1, LANES))
            dsink_ref[pl.ds(2 * t + 1, 1), :] = jnp.broadcast_to(d1, (1, LANES))

        low_s = _lane((S, LANES)) < HEAD_DIM_B

        @pl.when(t == n_qtiles)
        def _():
            dk = jnp.where(low_s, dk_acc[0], dk_acc[1])
            dkv_ref[...] = _rope_t(dk, c_ref[...], s_ref[...], half, HEAD_DIM_B)

        @pl.when(t == n_qtiles + 1)
        def _():
            dkv_ref[...] = jnp.where(low_s, dv_acc[0], dv_acc[1])

    qt = lambda t: jnp.minimum(t, n_qtiles - 1)
    full = lambda col: pl.BlockSpec((S, LANES), lambda t: (0, col))
    per_tile = pl.BlockSpec((S, LANES), lambda t: (0, qt(t)))
    return pl.pallas_call(
        body, name=name, grid=(n_steps,),
        in_specs=[pl.BlockSpec((S, LANES), lambda t: (0, off_q + qt(t))), full(off_q + n_qtiles),
                  full(off_q + n_qtiles + 1), full(0), full(0), pl.BlockSpec(sinks_l.shape, lambda t: (0, 0)),
                  per_tile, per_tile, per_tile],
        out_specs=(per_tile, pl.BlockSpec((S, LANES), lambda t: (0, jnp.maximum(t - n_qtiles, 0))),
                   pl.BlockSpec(sinks_l.shape, lambda t: (0, 0))),
        out_shape=(jax.ShapeDtypeStruct((S, n_qtiles * LANES), F32), jax.ShapeDtypeStruct((S, 2 * LANES), F32),
                   jax.ShapeDtypeStruct(sinks_l.shape, F32)),
        scratch_shapes=[pltpu.VMEM((S, LANES), F32) for _ in range(3)]
        + [pltpu.VMEM((2, S, LANES), F32), pltpu.VMEM((2, S, LANES), F32), pltpu.VMEM((2, 2 * W, BLOCK), F32)],
        compiler_params=_params(("arbitrary",)),
    )(h, h, h, cb, sb, sinks_l, yb, lse, dyb)


def mem_attn_fwd(q, kv, *, name, tm=512):
    S, D = q.shape
    n_mem = kv.shape[0]
    hd = D // N_MEM_HEADS
    scale = hd ** -0.5
    tm = _tile(S, tm, 8)

    def body(q_ref, kv_ref, o_ref):
        for hh in range(N_MEM_HEADS):
            cols = slice(hh * hd, (hh + 1) * hd)
            s = _dot_nt(q_ref[:, cols], kv_ref[:, cols]) * scale
            s = s - jnp.max(s, axis=-1, keepdims=True)
            e = jnp.exp(s)
            p = e / jnp.sum(e, axis=-1, keepdims=True)
            o_ref[:, cols] = _dot(p.astype(BF16), kv_ref[:, D + hh * hd:D + (hh + 1) * hd]).astype(o_ref.dtype)

    return pl.pallas_call(
        body, name=name, grid=(S // tm,),
        in_specs=[pl.BlockSpec((tm, D), lambda i: (i, 0)), pl.BlockSpec((n_mem, 2 * D), lambda i: (0, 0))],
        out_specs=pl.BlockSpec((tm, D), lambda i: (i, 0)),
        out_shape=jax.ShapeDtypeStruct((S, D), BF16),
        compiler_params=_params(("parallel",)),
    )(q, kv)


def mem_attn_bwd(q, kv, do, *, name, tm=512):
    S, D = q.shape
    n_mem = kv.shape[0]
    hd = D // N_MEM_HEADS
    scale = hd ** -0.5
    tm = _tile(S, tm, 8)

    def body(q_ref, kv_ref, do_ref, dq_ref, dkv_ref):
        i = pl.program_id(0)

        @pl.when(i == 0)
        def _():
            dkv_ref[...] = jnp.zeros_like(dkv_ref)

        for hh in range(N_MEM_HEADS):
            cols = slice(hh * hd, (hh + 1) * hd)
            vcols = slice(D + hh * hd, D + (hh + 1) * hd)
            qh, kh, vh, doh = q_ref[:, cols], kv_ref[:, cols], kv_ref[:, vcols], do_ref[:, cols]
            s = _dot_nt(qh, kh) * scale
            s = s - jnp.max(s, axis=-1, keepdims=True)
            e = jnp.exp(s)
            p = e / jnp.sum(e, axis=-1, keepdims=True)
            dp = _dot_nt(doh, vh)
            ds = (p * (dp - jnp.sum(dp * p, axis=-1, keepdims=True)) * scale).astype(BF16)
            dq_ref[:, cols] = _dot(ds, kh).astype(dq_ref.dtype)
            dkv_ref[:, cols] += _dot_tn(ds, qh)
            dkv_ref[:, vcols] += _dot_tn(p.astype(BF16), doh)

    row = pl.BlockSpec((tm, D), lambda i: (i, 0))
    kvs = pl.BlockSpec((n_mem, 2 * D), lambda i: (0, 0))
    return pl.pallas_call(
        body, name=name, grid=(S // tm,),
        in_specs=[row, kvs, row], out_specs=(row, kvs),
        out_shape=(jax.ShapeDtypeStruct((S, D), BF16), jax.ShapeDtypeStruct((n_mem, 2 * D), F32)),
        compiler_params=_params(("arbitrary",)),
    )(q, kv, do)


def _rows_view(t):
    return t.reshape(-1, t.shape[-1])


def _row_tile(rows, cols, target_elems=512 * 1024):
    return _tile(rows, max(8, target_elems // cols), 8)


def cast_bf16(w, *, name):
    v = _rows_view(w)
    R, C = v.shape
    tr = _row_tile(R, C)

    def body(w_ref, o_ref):
        o_ref[...] = w_ref[...].astype(BF16)

    spec = pl.BlockSpec((tr, C), lambda i: (i, 0))
    out = pl.pallas_call(body, name=name, grid=(R // tr,), in_specs=[spec], out_specs=spec,
                         out_shape=jax.ShapeDtypeStruct((R, C), BF16), compiler_params=_params(("parallel",)))(v)
    return out.reshape(w.shape)


def mesh_place():
    return tuple(lax.axis_index(a).astype(jnp.int32).reshape(1) for a in ("x", "y", "c"))


def pair_sum(p, r1, place, *, name):
    nsh, r, c = p.shape
    hr = r // 2
    tr = _row_tile(hr, c)
    nt = hr // tr

    def body(x_ref, y_ref, c_ref, p_ref, r_ref, o_ref):
        o_ref[...] = (p_ref[...].astype(F32) + r_ref[...].astype(F32)).astype(BF16)

    return pl.pallas_call(
        body, name=name,
        grid_spec=pltpu.PrefetchScalarGridSpec(
            num_scalar_prefetch=3, grid=(nsh, nt),
            in_specs=[pl.BlockSpec((None, tr, c), lambda s, i, x, y, cc: (s, cc[0] * nt + i, 0)),
                      pl.BlockSpec((None, tr, c), lambda s, i, x, y, cc: (s, i, 0))],
            out_specs=pl.BlockSpec((None, tr, c), lambda s, i, x, y, cc: (s, i, 0))),
        out_shape=jax.ShapeDtypeStruct((nsh, hr, c), BF16),
        compiler_params=_params(("parallel", "parallel")),
    )(*place, p, r1)


def cast_into_slot(w, place, *, name):
    R, C = w.shape
    tr = _row_tile(R, C)

    def body(x_ref, y_ref, c_ref, w_ref, o_ref):
        o_ref[...] = w_ref[...].astype(BF16)

    return pl.pallas_call(
        body, name=name,
        grid_spec=pltpu.PrefetchScalarGridSpec(
            num_scalar_prefetch=3, grid=(R // tr,),
            in_specs=[pl.BlockSpec((tr, C), lambda i, x, y, cc: (i, 0))],
            out_specs=pl.BlockSpec((None, tr, C), lambda i, x, y, cc: (2 * x[0] + y[0], i, 0))),
        out_shape=jax.ShapeDtypeStruct((N_CHIPS, R, C), BF16),
        compiler_params=_params(("parallel",)),
    )(*place, w)


def chip_sum(q, r2, place, *, name):
    _, hr, c = q.shape
    tr = _row_tile(hr, c, 256 * 1024)
    nt = hr // tr

    def body(x_ref, y_ref, c_ref, q_ref, r_ref, o_ref):
        acc = q_ref[...].astype(F32)
        for k in range(r_ref.shape[0]):
            acc = acc + r_ref[k].astype(F32)
        o_ref[...] = acc

    return pl.pallas_call(
        body, name=name,
        grid_spec=pltpu.PrefetchScalarGridSpec(
            num_scalar_prefetch=3, grid=(nt,),
            in_specs=[pl.BlockSpec((None, tr, c), lambda i, x, y, cc: (2 * x[0] + y[0], i, 0)),
                      pl.BlockSpec((r2.shape[0], tr, c), lambda i, x, y, cc: (0, i, 0))],
            out_specs=pl.BlockSpec((tr, c), lambda i, x, y, cc: (cc[0] * nt + i, 0))),
        out_shape=jax.ShapeDtypeStruct((2 * hr, c), F32),
        compiler_params=_params(("parallel",)),
    )(*place, q, r2)


def adamw(w, g, m, v, *, name):
    shape = w.shape
    wv, gv, mv, vv = (_rows_view(t) for t in (w, g, m, v))
    R, C = wv.shape
    tr = _row_tile(R, C, 256 * 1024)
    c1 = 1.0 / (1.0 - ADAM_B1 ** ADAM_STEP)
    c2 = 1.0 / (1.0 - ADAM_B2 ** ADAM_STEP)

    def body(w_ref, g_ref, m_ref, v_ref, d_ref, nm_ref, nv_ref):
        g_ = g_ref[...]
        nm = ADAM_B1 * m_ref[...] + (1.0 - ADAM_B1) * g_
        nv = ADAM_B2 * v_ref[...] + (1.0 - ADAM_B2) * (g_ * g_)
        m_hat = nm * c1
        v_hat = nv * c2
        d_ref[...] = -ADAM_LR * (m_hat / (jnp.sqrt(v_hat) + ADAM_EPS) + ADAM_WD * w_ref[...])
        nm_ref[...] = nm
        nv_ref[...] = nv

    spec = pl.BlockSpec((tr, C), lambda i: (i, 0))
    o = jax.ShapeDtypeStruct((R, C), F32)
    outs = pl.pallas_call(body, name=name, grid=(R // tr,), in_specs=[spec] * 4, out_specs=(spec,) * 3,
                          out_shape=(o, o, o), compiler_params=_params(("parallel",)))(wv, gv, mv, vv)
    return tuple(t.reshape(shape) for t in outs)


def _place():
    x, y, c = lax.axis_index("x"), lax.axis_index("y"), lax.axis_index("c")
    others = [(1 - x, y), (x, 1 - y), (1 - x, 1 - y)]
    return x, y, c, others


def _any_specs(n):
    return [pl.BlockSpec(memory_space=pl.ANY) for _ in range(n)]


HBM_SPEC = pl.BlockSpec(memory_space=pltpu.HBM)
SEM_SPEC = pl.BlockSpec(memory_space=pltpu.SEMAPHORE)
DATAFLOW = pltpu.SideEffectType.DATAFLOW_SIDE_EFFECTING
TOKEN = jax.ShapeDtypeStruct((8, LANES), F32)


def _in_hbm(arrays):
    return [pltpu.with_memory_space_constraint(a, pltpu.HBM) for a in arrays]


def _gather_copy(g, t, j, slot, px, py, c, send, recv):
    hr = g[t].shape[1] // 2
    rows = g[t].at[slot, pl.ds(c * hr, hr)]
    return pltpu.make_async_remote_copy(rows, rows, send.at[3 * t + j], recv.at[3 * t + j], device_id=(px, py, c), device_id_type=MESH)


def gather_start(gs, *, name):
    n = len(gs)

    def body(*refs):
        g, token = refs[:n], refs[-1]
        send, recv = refs[n], refs[n + 1]
        x, y, c, others = _place()
        for t in range(n):
            for j, (px, py) in enumerate(others):
                _gather_copy(g, t, j, 2 * x + y, px, py, c, send, recv).start()
        token[...] = jnp.zeros_like(token)

    outs = pl.pallas_call(
        body, name=name,
        in_specs=[HBM_SPEC] * n,
        out_specs=(SEM_SPEC, SEM_SPEC, *[HBM_SPEC] * n, pl.BlockSpec(memory_space=pltpu.VMEM)),
        out_shape=(pltpu.SemaphoreType.DMA((3 * n,)), pltpu.SemaphoreType.DMA((3 * n,)),
                   *[pltpu.HBM(g.shape, g.dtype) for g in gs], TOKEN),
        input_output_aliases={t: 2 + t for t in range(n)},
        compiler_params=pltpu.CompilerParams(has_side_effects=DATAFLOW),
    )(*_in_hbm(gs))
    return outs[0], outs[1], list(outs[2:2 + n]), outs[-1]


def gather_wait(send, recv, gs, after, *, name):
    n = len(gs)

    def body(*refs):
        g = refs[:n]
        send, recv = refs[n], refs[n + 1]
        x, y, c, others = _place()
        for t in range(n):
            for j, (px, py) in enumerate(others):
                _gather_copy(g, t, j, 2 * x + y, px, py, c, send, recv).wait_send()
                _gather_copy(g, t, j, 2 * px + py, px, py, c, send, recv).wait_recv()

    outs = pl.pallas_call(
        body, name=name,
        in_specs=[HBM_SPEC] * n + [SEM_SPEC, SEM_SPEC, pl.BlockSpec(memory_space=pl.ANY)],
        out_specs=tuple([HBM_SPEC] * n),
        out_shape=tuple(pltpu.HBM(g.shape, g.dtype) for g in gs),
        input_output_aliases={t: t for t in range(n)},
        compiler_params=pltpu.CompilerParams(has_side_effects=DATAFLOW),
    )(*gs, send, recv, after)
    return list(outs)


def gather_forward(gs, *, name):
    n = len(gs)

    def body(*refs):
        g = refs[n:2 * n]
        send, recv = refs[2 * n:]
        x, y, c, others = _place()
        cps = []
        for t in range(n):
            hr = g[t].shape[1] // 2
            for j, (px, py) in enumerate(others):
                rows = g[t].at[2 * px + py, pl.ds(c * hr, hr)]
                cp = pltpu.make_async_remote_copy(rows, rows, send.at[3 * t + j], recv.at[3 * t + j],
                                                  device_id=(x, y, 1 - c), device_id_type=MESH)
                cp.start()
                cps.append(cp)
        for t in range(n):
            hr = g[t].shape[1] // 2
            for j, (px, py) in enumerate(others):
                rows = g[t].at[2 * px + py, pl.ds((1 - c) * hr, hr)]
                pltpu.make_async_remote_copy(rows, rows, send.at[3 * t + j], recv.at[3 * t + j],
                                             device_id=(x, y, 1 - c), device_id_type=MESH).wait_recv()
        for cp in cps:
            cp.wait_send()

    return pl.pallas_call(
        body, name=name,
        in_specs=_any_specs(n), out_specs=_any_specs(n),
        out_shape=[jax.ShapeDtypeStruct(g.shape, g.dtype) for g in gs],
        input_output_aliases={t: t for t in range(n)},
        scratch_shapes=[pltpu.SemaphoreType.DMA((3 * n,)), pltpu.SemaphoreType.DMA((3 * n,))],
        compiler_params=pltpu.CompilerParams(has_side_effects=True),
    )(*gs)


def sibling_halves(parts, *, name):
    n = len(parts)

    def body(*refs):
        src, dst = refs[:n], refs[n:2 * n]
        send, recv = refs[2 * n:]
        x, y, c, _ = _place()
        cps = []
        for t in range(n):
            hr = src[t].shape[1] // 2
            cp = pltpu.make_async_remote_copy(src[t].at[:, pl.ds((1 - c) * hr, hr)], dst[t], send.at[t], recv.at[t],
                                              device_id=(x, y, 1 - c), device_id_type=MESH)
            cp.start()
            cps.append(cp)
        for cp in cps:
            cp.wait()

    return pl.pallas_call(
        body, name=name,
        in_specs=_any_specs(n), out_specs=_any_specs(n),
        out_shape=[jax.ShapeDtypeStruct((p.shape[0], p.shape[1] // 2, p.shape[2]), p.dtype) for p in parts],
        scratch_shapes=[pltpu.SemaphoreType.DMA((n,)), pltpu.SemaphoreType.DMA((n,))],
        compiler_params=pltpu.CompilerParams(has_side_effects=True),
    )(*parts)


def _chips_copy(q, land, t, j, px, py, c, send, recv):
    return pltpu.make_async_remote_copy(q[t].at[2 * px + py], land[t].at[j], send.at[3 * t + j], recv.at[3 * t + j],
                                        device_id=(px, py, c), device_id_type=MESH)


def chips_start(qs, *, name):
    n = len(qs)

    def body(*refs):
        q, land, token = refs[:n], refs[n:2 * n], refs[-1]
        send, recv = refs[2 * n], refs[2 * n + 1]
        x, y, c, others = _place()
        for t in range(n):
            for j, (px, py) in enumerate(others):
                _chips_copy(q, land, t, j, px, py, c, send, recv).start()
        token[...] = jnp.zeros_like(token)

    lands = [lax.empty((3,) + q.shape[1:], q.dtype) for q in qs]
    outs = pl.pallas_call(
        body, name=name,
        in_specs=[HBM_SPEC] * (2 * n),
        out_specs=(SEM_SPEC, SEM_SPEC, *[HBM_SPEC] * (2 * n), pl.BlockSpec(memory_space=pltpu.VMEM)),
        out_shape=(pltpu.SemaphoreType.DMA((3 * n,)), pltpu.SemaphoreType.DMA((3 * n,)),
                   *[pltpu.HBM(a.shape, a.dtype) for a in qs + lands], TOKEN),
        input_output_aliases={t: 2 + t for t in range(2 * n)},
        compiler_params=pltpu.CompilerParams(has_side_effects=DATAFLOW),
    )(*_in_hbm(qs + lands))
    return outs[0], outs[1], list(outs[2:2 + n]), list(outs[2 + n:2 + 2 * n]), outs[-1]


def chips_wait(send, recv, qs, lands, after, *, name):
    n = len(qs)

    def body(*refs):
        q, land = refs[:n], refs[n:2 * n]
        send, recv = refs[2 * n], refs[2 * n + 1]
        x, y, c, others = _place()
        for t in range(n):
            for j, (px, py) in enumerate(others):
                cp = _chips_copy(q, land, t, j, px, py, c, send, recv)
                cp.wait_send()
                cp.wait_recv()

    outs = pl.pallas_call(
        body, name=name,
        in_specs=[HBM_SPEC] * (2 * n) + [SEM_SPEC, SEM_SPEC, pl.BlockSpec(memory_space=pl.ANY)],
        out_specs=tuple([HBM_SPEC] * (2 * n)),
        out_shape=tuple(pltpu.HBM(a.shape, a.dtype) for a in qs + lands),
        input_output_aliases={t: t for t in range(2 * n)},
        compiler_params=pltpu.CompilerParams(has_side_effects=DATAFLOW),
    )(*qs, *lands, send, recv, after)
    return list(outs[:n]), list(outs[n:])


def join_halves(fulls, *, name):
    n = len(fulls)

    def body(*refs):
        g = refs[n:2 * n]
        send, recv = refs[2 * n:]
        x, y, c, _ = _place()
        cps = []
        for t in range(n):
            hr = g[t].shape[0] // 2
            rows = g[t].at[pl.ds(c * hr, hr)]
            cp = pltpu.make_async_remote_copy(rows, rows, send.at[t], recv.at[t], device_id=(x, y, 1 - c), device_id_type=MESH)
            cp.start()
            cps.append(cp)
        for t in range(n):
            hr = g[t].shape[0] // 2
            rows = g[t].at[pl.ds((1 - c) * hr, hr)]
            pltpu.make_async_remote_copy(rows, rows, send.at[t], recv.at[t],
                                         device_id=(x, y, 1 - c), device_id_type=MESH).wait_recv()
        for cp in cps:
            cp.wait_send()

    return pl.pallas_call(
        body, name=name,
        in_specs=_any_specs(n), out_specs=_any_specs(n),
        out_shape=[jax.ShapeDtypeStruct(g.shape, g.dtype) for g in fulls],
        input_output_aliases={t: t for t in range(n)},
        scratch_shapes=[pltpu.SemaphoreType.DMA((n,)), pltpu.SemaphoreType.DMA((n,))],
        compiler_params=pltpu.CompilerParams(has_side_effects=True),
    )(*fulls)


def allreduce_small(t, *, name):
    R, C = t.shape

    def body(t_ref, o_ref, land, send, recv):
        x, y, c, _ = _place()
        me = 4 * x + 2 * y + c
        land[me] = t_ref[...]
        cps = []
        for j in range(1, 8):
            px, py, pc = (x + (j >> 2)) % 2, (y + ((j >> 1) & 1)) % 2, (c + (j & 1)) % 2
            cp = pltpu.make_async_remote_copy(t_ref, land.at[me], send.at[j - 1], recv.at[j - 1],
                                              device_id=(px, py, pc), device_id_type=MESH)
            cp.start()
            cps.append(cp)
        for j in range(1, 8):
            px, py, pc = (x + (j >> 2)) % 2, (y + ((j >> 1) & 1)) % 2, (c + (j & 1)) % 2
            pltpu.make_async_remote_copy(t_ref, land.at[4 * px + 2 * py + pc], send.at[j - 1], recv.at[j - 1],
                                         device_id=(px, py, pc), device_id_type=MESH).wait_recv()
        for cp in cps:
            cp.wait_send()
        acc = land[0]
        for k in range(1, 8):
            acc = acc + land[k]
        o_ref[...] = acc

    return pl.pallas_call(
        body, name=name,
        in_specs=[pl.BlockSpec(memory_space=pltpu.VMEM)], out_specs=pl.BlockSpec(memory_space=pltpu.VMEM),
        out_shape=jax.ShapeDtypeStruct((R, C), F32),
        scratch_shapes=[pltpu.VMEM((8, R, C), F32), pltpu.SemaphoreType.DMA((7,)), pltpu.SemaphoreType.DMA((7,))],
        compiler_params=pltpu.CompilerParams(has_side_effects=True),
    )(t)


def _layer_fwd(x, xb, memb, w_in, rest, P, tabs, alpha, li):
    ca, sa, cb, sb = tabs
    nA = P["gn_a"].shape[1] // HEAD_DIM_A
    nQ = P["gn_b"].shape[1] // LANES
    nm = lambda s: f"L{li}_{s}"
    h = mm_nn(xb, w_in, name=nm("h"), out_dtype=F32, tn=2304)
    ya, lse_a = attn_a_fwd(h, ca, sa, n_heads=nA, name=nm("attn_a"))
    yb, lse_b = attn_b_fwd(h, cb, sb, P["sinks_l"], off_q=3 * nA, n_qtiles=nQ, name=nm("attn_b"))
    ymix = rms_fwd(ya, yb, P["gn_a"], P["gn_b"], name=nm("rms"))
    W, P = rest(ymix, P)
    z1, x1, x1b = mm_ln(ymix, W["w_out"][0], x, P["ln_mix_g"], P["ln_mix_b"], name=nm("out_ln"), alpha=alpha)
    qm = mm_nn(x1b, W["w_mq"], name=nm("mq"), out_dtype=BF16)
    kv = mm_nn(memb, W["w_mkv"], name=nm("mkv"), out_dtype=BF16, tm=256)
    o = mem_attn_fwd(qm, kv, name=nm("mem_attn"))
    z2, x2, x2b = mm_ln(o, W["w_mo"][0], x1, P["ln_mem_g"], P["ln_mem_b"], name=nm("mo_ln"), alpha=alpha)
    u, a = mm_nn(x2b, W["w_up"], name=nm("up"), out_dtype=BF16, relu2=True)
    z3, x3, x3b = mm_ln(a, W["w_down"][0], x2, P["ln_ff_g"], P["ln_ff_b"], name=nm("down_ln"), alpha=alpha)
    saved = dict(xb=xb, h=h, ya=ya, lse_a=lse_a, yb=yb, lse_b=lse_b, ymix=ymix, z1=z1, x1b=x1b, qm=qm, kv=kv, o=o,
                 z2=z2, x2b=x2b, u=u, a=a, z3=z3)
    return x3, x3b, saved


def _layer_bwd(dx3, sv, memb, W, P, tabs, alpha, li, hook=None):
    ca, sa, cb, sb = tabs
    nA = P["gn_a"].shape[1] // HEAD_DIM_A
    nQ = P["gn_b"].shape[1] // LANES
    nm = lambda s: f"L{li}_b_{s}"
    nsh = lambda k: W[k].shape[0]
    gw, gs = {}, {}
    dz3, dz3b, gs["ln_ff_g"], gs["ln_ff_b"] = ln_bwd(dx3, sv["z3"], P["ln_ff_g"], name=nm("ln_ff"))
    gw["w_down"] = mm_tn(sv["a"], dz3b, nsh("w_down"), name=nm("dw_down"))
    du = mm_nt(dz3b, W["w_down"], name=nm("du"), out_dtype=BF16, umul=sv["u"])
    gw["w_up"] = mm_tn(sv["x2b"], du, nsh("w_up"), name=nm("dw_up"))
    dx2 = mm_nt(du, W["w_up"], name=nm("dx2"), out_dtype=F32, resid=dz3, alpha=alpha)
    dz2, dz2b, gs["ln_mem_g"], gs["ln_mem_b"] = ln_bwd(dx2, sv["z2"], P["ln_mem_g"], name=nm("ln_mem"))
    gw["w_mo"] = mm_tn(sv["o"], dz2b, nsh("w_mo"), name=nm("dw_mo"))
    do = mm_nt(dz2b, W["w_mo"], name=nm("do"), out_dtype=BF16)
    dqm, dkv = mem_attn_bwd(sv["qm"], sv["kv"], do, name=nm("mem_attn"))
    gw["w_mq"] = mm_tn(sv["x1b"], dqm, nsh("w_mq"), name=nm("dw_mq"))
    gw["w_mkv"] = mm_tn(memb, cast_bf16(dkv, name=nm("dkv_cast")), nsh("w_mkv"), name=nm("dw_mkv"), tm=256)
    dx1 = mm_nt(dqm, W["w_mq"], name=nm("dx1"), out_dtype=F32, resid=dz2, alpha=alpha)
    if hook is not None:
        P = hook(gw, dx1, P)
    dz1, dz1b, gs["ln_mix_g"], gs["ln_mix_b"] = ln_bwd(dx1, sv["z1"], P["ln_mix_g"], name=nm("ln_mix"))
    gw["w_out"] = mm_tn(sv["ymix"], dz1b, nsh("w_out"), name=nm("dw_out"))
    dymix = mm_nt(dz1b, W["w_out"], name=nm("dymix"), out_dtype=F32)
    dya, dyb, gs["gn_a"], gs["gn_b"] = rms_bwd(dymix, sv["ya"], sv["yb"], P["gn_a"], P["gn_b"], name=nm("rms"))
    dha = attn_a_bwd(sv["h"], ca, sa, sv["ya"], sv["lse_a"], dya, n_heads=nA, name=nm("attn_a"))
    dqb, dkvb, gs["sinks"] = attn_b_bwd(sv["h"], cb, sb, P["sinks_l"], sv["yb"], sv["lse_b"], dyb,
                                        off_q=3 * nA, n_qtiles=nQ, name=nm("attn_b"))
    dh = jnp.concatenate([dha, dqb.astype(BF16), dkvb.astype(BF16)], axis=1)
    gw["w_in"] = mm_tn(sv["xb"], dh, nsh("w_in"), name=nm("dw_in"), tn=2304)
    dx0 = mm_nt(dh, W["w_in"], name=nm("dx0"), out_dtype=F32, resid=dz1, alpha=alpha, tr=2304)
    return dx0, gw, gs


def _gathered_view(name, g):
    if name == "w_in":
        return jnp.concatenate([g[k] for k in range(N_CHIPS)], axis=1)[None]
    if name in COL_SHARDED:
        return g
    return g.reshape(1, g.shape[0] * g.shape[1], g.shape[2])


def _to_shards(name, gw):
    if name == "w_in":
        n = gw.shape[2] // N_CHIPS
        return jnp.stack([gw[0, :, k * n:(k + 1) * n] for k in range(N_CHIPS)])
    if name in COL_SHARDED:
        return gw
    return gw.reshape(N_CHIPS, gw.shape[1] // N_CHIPS, gw.shape[2])


def _step(x, mem, positions, loss_target, w, m, v):
    S, D = x.shape[1], x.shape[2]
    depth = w["w_in"].shape[0]
    alpha = (2 * depth) ** 0.25
    x0 = x[0]
    memb = cast_bf16(mem[0], name="mem_cast")
    pos = positions[0]
    tabs = rope_tables(pos, HEAD_DIM_A // 4, HEAD_DIM_A) + rope_tables(pos, HEAD_DIM_B // 4, HEAD_DIM_B)
    place = mesh_place()

    def small(li):
        P = {k: w[k][li][None] for k in ("gn_a", "gn_b", "ln_mix_g", "ln_mix_b", "ln_mem_g", "ln_mem_b", "ln_ff_g", "ln_ff_b")}
        P["sinks_l"] = jnp.broadcast_to(w["sinks"][li][:, None], (w["sinks"].shape[1], LANES))
        return P

    rest_names = tuple(k for k in BIG if k != "w_in")
    chain = [(0, ("w_in",)), (0, rest_names)] + [(li, BIG) for li in range(1, depth)]
    casts = [[cast_into_slot(w[k][li], place, name=f"L{li}_cast_{k}") for k in names] for li, names in chain]
    started = {0: gather_start(casts[0], name="G0_gather_start")}

    def land(gi, after):
        send, recv, gs, tok0 = started.pop(gi)
        gs = gather_wait(send, recv, gs, tok0 if after is None else after, name=f"G{gi}_gather_wait")
        token = None
        if gi + 1 < len(chain):
            started[gi + 1] = gather_start(casts[gi + 1], name=f"G{gi + 1}_gather_start")
            token = started[gi + 1][3]
        gs = gather_forward(gs, name=f"G{gi}_gather_fwd")
        return dict(zip(chain[gi][1], gs)), token

    def ordered(a, token):
        return a if token is None else a + token[:1, :1].astype(a.dtype)

    xs, xbs, saved, Ws = x0, cast_bf16(x0, name="x_cast"), [], []
    for li in range(depth):
        gi = 0 if li == 0 else li + 1
        got, token = land(gi, None if li == 0 else xs)
        W = {"w_in": ordered(_gathered_view("w_in", got["w_in"]), token)}

        def rest(after, P, li=li, got=got, W=W):
            if li == 0:
                got, token = land(1, after)
                P = dict(P, ln_mix_g=ordered(P["ln_mix_g"], token))
            W.update({k: _gathered_view(k, got[k]) for k in rest_names})
            return W, P

        xs, xbs, sv = _layer_fwd(xs, xbs, memb, W["w_in"], rest, small(li), tabs, alpha, li)
        saved.append(sv)
        Ws.append(W)
    dy, loss_part = loss_head(xs, loss_target[0], name="loss")
    loss = lax.psum(0.5 / D * jnp.sum(loss_part), ("x", "y", "c"))

    g_big = {k: [None] * depth for k in BIG}
    g_small = [None] * depth

    def begin(li, names, gw, tag):
        parts = [_to_shards(k, gw[k]) for k in names]
        r1 = sibling_halves(parts, name=f"L{li}{tag}_rs_sibling")
        qs = [pair_sum(p, r, place, name=f"L{li}_rs_pair_{k}") for k, p, r in zip(names, parts, r1)]
        send, recv, qs, lands, token = chips_start(qs, name=f"L{li}{tag}_rs_chips_start")
        return (li, names, tag, send, recv, qs, lands), token

    def finish(pending, after):
        li, names, tag, send, recv, qs, lands = pending
        qs, lands = chips_wait(send, recv, qs, lands, after, name=f"L{li}{tag}_rs_chips_wait")
        fulls = [chip_sum(q, r, place, name=f"L{li}_rs_sum_{k}") for k, q, r in zip(names, qs, lands)]
        for k, f in zip(names, join_halves(fulls, name=f"L{li}{tag}_rs_join")):
            g_big[k][li] = f

    early = ("w_mq", "w_mkv", "w_mo", "w_up", "w_down")
    late = tuple(k for k in BIG if k not in early)
    pendings, token = [], None
    for li in reversed(range(depth)):
        P = small(li)
        P["ln_ff_g"] = ordered(P["ln_ff_g"], token)
        hook = None
        if li == 0:
            def hook(gw, dx1, P):
                while pendings:
                    finish(pendings.pop(), dx1)
                pend, tok = begin(0, early, gw, "a")
                pendings.append(pend)
                return dict(P, ln_mix_g=ordered(P["ln_mix_g"], tok))
        dy, gw, g_small[li] = _layer_bwd(dy, saved[li], memb, Ws[li], P, tabs, alpha, li, hook)
        while pendings:
            finish(pendings.pop(), dy)
        pend, token = begin(li, late if li == 0 else BIG, gw, "b" if li == 0 else "")
        pendings.append(pend)
    finish(pendings.pop(), token)
    grad_x = dy[None]

    rows = []
    for li in range(depth):
        gs = g_small[li]
        for k in ("ln_mix_g", "ln_mix_b", "ln_mem_g", "ln_mem_b", "ln_ff_g", "ln_ff_b"):
            rows.append(jnp.sum(gs[k], axis=0, keepdims=True))
        rows.append(jnp.concatenate([jnp.sum(gs["gn_a"], axis=0, keepdims=True), jnp.sum(gs["gn_b"], axis=0, keepdims=True)], axis=1))
        sk = gs["sinks"][:, 0][None]
        rows.append(jnp.pad(sk, ((0, 0), (0, D - sk.shape[1]))))
    red = allreduce_small(jnp.concatenate(rows, axis=0), name="small_allreduce").reshape(depth, 8, D)
    wa = w["gn_a"].shape[1]
    grads = {k: jnp.stack(g_big[k]) for k in BIG}
    for j, k in enumerate(("ln_mix_g", "ln_mix_b", "ln_mem_g", "ln_mem_b", "ln_ff_g", "ln_ff_b")):
        grads[k] = red[:, j]
    grads["gn_a"] = red[:, 6, :wa]
    grads["gn_b"] = red[:, 6, wa:]
    grads["sinks"] = red[:, 7, :w["sinks"].shape[1]]

    delta, new_m, new_v = {}, {}, {}
    small_names = [k for k in w if k not in BIG]
    for k in BIG:
        delta[k], new_m[k], new_v[k] = adamw(w[k], grads[k], m[k], v[k], name=f"adamw_{k}")
    pack = lambda d: jnp.concatenate([jnp.pad(d[k], ((0, 0), (0, D - d[k].shape[1]))) for k in small_names], axis=0)
    ds, ms, vs = adamw(pack(w), pack(grads), pack(m), pack(v), name="adamw_small")
    for j, k in enumerate(small_names):
        sl = (slice(j * depth, (j + 1) * depth), slice(0, w[k].shape[1]))
        delta[k], new_m[k], new_v[k] = ds[sl], ms[sl], vs[sl]
    return loss, grad_x, grads, delta, new_m, new_v


WEIGHTS = ("w_in", "gn_a", "gn_b", "sinks", "w_out", "ln_mix_g", "ln_mix_b", "w_mq", "w_mkv", "w_mo",
           "ln_mem_g", "ln_mem_b", "w_up", "w_down", "ln_ff_g", "ln_ff_b")


def kernel(x, mem, positions, w_in, gn_a, gn_b, sinks, w_out, ln_mix_g, ln_mix_b, w_mq, w_mkv, w_mo, ln_mem_g, ln_mem_b, w_up, w_down, ln_ff_g, ln_ff_b, loss_target, m_w_in, m_gn_a, m_gn_b, m_sinks, m_w_out, m_ln_mix_g, m_ln_mix_b, m_w_mq, m_w_mkv, m_w_mo, m_ln_mem_g, m_ln_mem_b, m_w_up, m_w_down, m_ln_ff_g, m_ln_ff_b, v_w_in, v_gn_a, v_gn_b, v_sinks, v_w_out, v_ln_mix_g, v_ln_mix_b, v_w_mq, v_w_mkv, v_w_mo, v_ln_mem_g, v_ln_mem_b, v_w_up, v_w_down, v_ln_ff_g, v_ln_ff_b):
    w = dict(zip(WEIGHTS, (w_in, gn_a, gn_b, sinks, w_out, ln_mix_g, ln_mix_b, w_mq, w_mkv, w_mo, ln_mem_g, ln_mem_b, w_up, w_down, ln_ff_g, ln_ff_b)))
    m = dict(zip(WEIGHTS, (m_w_in, m_gn_a, m_gn_b, m_sinks, m_w_out, m_ln_mix_g, m_ln_mix_b, m_w_mq, m_w_mkv, m_w_mo, m_ln_mem_g, m_ln_mem_b, m_w_up, m_w_down, m_ln_ff_g, m_ln_ff_b)))
    v = dict(zip(WEIGHTS, (v_w_in, v_gn_a, v_gn_b, v_sinks, v_w_out, v_ln_mix_g, v_ln_mix_b, v_w_mq, v_w_mkv, v_w_mo, v_ln_mem_g, v_ln_mem_b, v_w_up, v_w_down, v_ln_ff_g, v_ln_ff_b)))
    loss, grad_x, grads, delta, new_m, new_v = _step(x, mem, positions, loss_target, w, m, v)
    return (loss, grad_x, *[grads[k] for k in WEIGHTS], *[delta[k] for k in WEIGHTS],
            *[new_m[k] for k in WEIGHTS], *[new_v[k] for k in WEIGHTS])
```

```python
import functools

import jax
import jax.numpy as jnp
from jax import lax
from jax.experimental import pallas as pl
from jax.experimental.pallas import tpu as pltpu

F32 = jnp.float32
BF16 = jnp.bfloat16
MESH = pl.DeviceIdType.MESH

HEAD_DIM_A = 128
HEAD_DIM_B = 64
LANES = 128
BLOCK = 128
DILATED_BRANCHES = ((128, 1), (512, 4), (2048, 16))
WINDOW_B = 128
N_MEM_HEADS = 4
ROPE_THETA = 500000.0
LN_EPS = 1e-5
RMS_EPS = 1e-6
NEG_INF = -1e30
ADAM_LR = 0.001
ADAM_B1 = 0.9
ADAM_B2 = 0.999
ADAM_EPS = 1e-08
ADAM_WD = 0.01
ADAM_STEP = 10
N_CHIPS = 4
VMEM_LIMIT = 56 * 1024 * 1024

BIG = ("w_in", "w_out", "w_mq", "w_mkv", "w_mo", "w_up", "w_down")
COL_SHARDED = ("w_in", "w_mkv", "w_up")


def _tile(n, target, mult=LANES):
    best = None
    t = mult
    while t <= min(n, target):
        if n % t == 0:
            best = t
        t += mult
    return best if best is not None else n


def _params(sem=None):
    return pltpu.CompilerParams(dimension_semantics=sem, vmem_limit_bytes=VMEM_LIMIT)


def _dot(a, b):
    return jnp.dot(a, b, preferred_element_type=F32)


def _dot_nt(a, b):
    return lax.dot_general(a, b, (((1,), (1,)), ((), ())), preferred_element_type=F32)


def _dot_tn(a, b):
    return lax.dot_general(a, b, (((0,), (0,)), ((), ())), preferred_element_type=F32)


def mm_nn(a, b3, *, name, out_dtype, relu2=False, tm=512, tn=1024, tk=2048):
    M, K = a.shape
    nsh, _, nk = b3.shape
    tm, tn, tk = _tile(M, tm, 8), _tile(nk, tn), _tile(K, tk)
    nb, ksteps = nk // tn, K // tk

    def body(a_ref, b_ref, *rest):
        outs, scr = rest[:2 if relu2 else 1], rest[2 if relu2 else 1:]

        def finish(acc):
            if relu2:
                outs[0][...] = acc.astype(outs[0].dtype)
                r = jnp.maximum(acc, 0.0)
                outs[1][...] = (r * r).astype(outs[1].dtype)
            else:
                outs[0][...] = acc.astype(outs[0].dtype)

        if ksteps == 1:
            finish(_dot(a_ref[...], b_ref[...]))
        else:
            acc_ref = scr[0]
            k = pl.program_id(2)

            @pl.when(k == 0)
            def _():
                acc_ref[...] = jnp.zeros_like(acc_ref)

            acc_ref[...] += _dot(a_ref[...], b_ref[...])

            @pl.when(k == ksteps - 1)
            def _():
                finish(acc_ref[...])

    o_spec = pl.BlockSpec((tm, tn), lambda i, j, k: (i, j))
    o_shape = jax.ShapeDtypeStruct((M, nsh * nk), out_dtype)
    return pl.pallas_call(
        body, name=name,
        grid=(M // tm, nsh * nb, ksteps),
        in_specs=[pl.BlockSpec((tm, tk), lambda i, j, k: (i, k)),
                  pl.BlockSpec((None, tk, tn), lambda i, j, k: (j // nb, k, j % nb))],
        out_specs=(o_spec, o_spec) if relu2 else o_spec,
        out_shape=(o_shape, o_shape) if relu2 else o_shape,
        scratch_shapes=[] if ksteps == 1 else [pltpu.VMEM((tm, tn), F32)],
        compiler_params=_params(("parallel", "parallel", "arbitrary")),
    )(a, b3)


def mm_ln(a, w, resid, g, b, *, name, alpha, tm=512, tk=1024):
    M, K = a.shape
    D = w.shape[1]
    tm, tk = _tile(M, tm, 8), _tile(K, tk)
    ksteps = K // tk

    def body(a_ref, w_ref, r_ref, g_ref, b_ref, z_ref, xn_ref, xb_ref, acc_ref):
        k = pl.program_id(1)

        @pl.when(k == 0)
        def _():
            acc_ref[...] = jnp.zeros_like(acc_ref)

        acc_ref[...] += _dot(a_ref[...], w_ref[...])

        @pl.when(k == ksteps - 1)
        def _():
            z = alpha * r_ref[...] + acc_ref[...]
            mu = jnp.mean(z, axis=-1, keepdims=True)
            zc = z - mu
            var = jnp.mean(zc * zc, axis=-1, keepdims=True)
            xn = zc * lax.rsqrt(var + LN_EPS) * g_ref[...] + b_ref[...]
            z_ref[...] = z
            xn_ref[...] = xn
            xb_ref[...] = xn.astype(BF16)

    row = pl.BlockSpec((tm, D), lambda i, k: (i, 0))
    vec = pl.BlockSpec((1, D), lambda i, k: (0, 0))
    return pl.pallas_call(
        body, name=name,
        grid=(M // tm, ksteps),
        in_specs=[pl.BlockSpec((tm, tk), lambda i, k: (i, k)),
                  pl.BlockSpec((tk, D), lambda i, k: (k, 0)), row, vec, vec],
        out_specs=(row, row, row),
        out_shape=(jax.ShapeDtypeStruct((M, D), F32), jax.ShapeDtypeStruct((M, D), F32),
                   jax.ShapeDtypeStruct((M, D), BF16)),
        scratch_shapes=[pltpu.VMEM((tm, D), F32)],
        compiler_params=_params(("parallel", "arbitrary")),
    )(a, w, resid, g, b)


def mm_nt(a, b3, *, name, out_dtype, resid=None, alpha=1.0, umul=None, tm=512, tko=1024, tr=2048):
    M, N = a.shape
    nsh, K, nk = b3.shape
    tm, tko, tr = _tile(M, tm, 8), _tile(K, tko), _tile(nk, tr)
    nb = nk // tr
    rsteps = nsh * nb

    def body(a_ref, b_ref, *rest):
        rest = list(rest)
        r_ref = rest.pop(0) if resid is not None else None
        u_ref = rest.pop(0) if umul is not None else None
        o_ref = rest.pop(0)

        def finish(acc):
            if r_ref is not None:
                acc = acc + alpha * r_ref[...]
            if u_ref is not None:
                acc = acc * (2.0 * jnp.maximum(u_ref[...].astype(F32), 0.0))
            o_ref[...] = acc.astype(o_ref.dtype)

        if rsteps == 1:
            finish(_dot_nt(a_ref[...], b_ref[...]))
        else:
            acc_ref = rest[0]
            r = pl.program_id(2)

            @pl.when(r == 0)
            def _():
                acc_ref[...] = jnp.zeros_like(acc_ref)

            acc_ref[...] += _dot_nt(a_ref[...], b_ref[...])

            @pl.when(r == rsteps - 1)
            def _():
                finish(acc_ref[...])

    o_spec = pl.BlockSpec((tm, tko), lambda i, j, r: (i, j))
    in_specs = [pl.BlockSpec((tm, tr), lambda i, j, r: (i, r)),
                pl.BlockSpec((None, tko, tr), lambda i, j, r: (r // nb, j, r % nb))]
    args = [a, b3]
    for extra in (resid, umul):
        if extra is not None:
            in_specs.append(o_spec)
            args.append(extra)
    return pl.pallas_call(
        body, name=name,
        grid=(M // tm, K // tko, rsteps),
        in_specs=in_specs, out_specs=o_spec,
        out_shape=jax.ShapeDtypeStruct((M, K), out_dtype),
        scratch_shapes=[] if rsteps == 1 else [pltpu.VMEM((tm, tko), F32)],
        compiler_params=_params(("parallel", "parallel", "arbitrary")),
    )(*args)


def mm_tn(a, g, nsh, *, name, tk=1024, tn=1024, tm=2048):
    M, K = a.shape
    N = g.shape[1]
    nk = N // nsh
    tk, tn, tm = _tile(K, tk), _tile(nk, tn), _tile(M, tm, 8)
    nb, msteps = nk // tn, M // tm

    def body(a_ref, g_ref, o_ref, acc_ref):
        m = pl.program_id(2)

        @pl.when(m == 0)
        def _():
            acc_ref[...] = jnp.zeros_like(acc_ref)

        acc_ref[...] += _dot_tn(a_ref[...], g_ref[...])

        @pl.when(m == msteps - 1)
        def _():
            o_ref[...] = acc_ref[...].astype(o_ref.dtype)

    return pl.pallas_call(
        body, name=name,
        grid=(K // tk, nsh * nb, msteps),
        in_specs=[pl.BlockSpec((tm, tk), lambda i, j, m: (m, i)),
                  pl.BlockSpec((tm, tn), lambda i, j, m: (m, j))],
        out_specs=pl.BlockSpec((None, tk, tn), lambda i, j, m: (j // nb, i, j % nb)),
        out_shape=jax.ShapeDtypeStruct((nsh, K, nk), BF16),
        scratch_shapes=[pltpu.VMEM((tk, tn), F32)],
        compiler_params=_params(("parallel", "parallel", "arbitrary")),
    )(a, g)


def _fold8(t):
    return t.reshape(t.shape[0] // 8, 8, t.shape[1]).sum(axis=0)


def ln_bwd(dy, z, g, *, name, tm=256):
    M, D = z.shape
    tm = _tile(M, tm, 8)

    def body(dy_ref, z_ref, g_ref, dz_ref, dzb_ref, dg_ref, db_ref):
        i = pl.program_id(0)
        z_ = z_ref[...]
        dy_ = dy_ref[...]
        mu = jnp.mean(z_, axis=-1, keepdims=True)
        zc = z_ - mu
        var = jnp.mean(zc * zc, axis=-1, keepdims=True)
        rstd = lax.rsqrt(var + LN_EPS)
        xh = zc * rstd
        dxh = dy_ * g_ref[...]
        m1 = jnp.mean(dxh, axis=-1, keepdims=True)
        m2 = jnp.mean(dxh * xh, axis=-1, keepdims=True)
        dz = rstd * (dxh - m1 - xh * m2)
        dz_ref[...] = dz
        dzb_ref[...] = dz.astype(BF16)

        @pl.when(i == 0)
        def _():
            dg_ref[...] = jnp.zeros_like(dg_ref)
            db_ref[...] = jnp.zeros_like(db_ref)

        dg_ref[...] += _fold8(dy_ * xh)
        db_ref[...] += _fold8(dy_)

    row = pl.BlockSpec((tm, D), lambda i: (i, 0))
    acc = pl.BlockSpec((8, D), lambda i: (0, 0))
    return pl.pallas_call(
        body, name=name, grid=(M // tm,),
        in_specs=[row, row, pl.BlockSpec((1, D), lambda i: (0, 0))],
        out_specs=(row, row, acc, acc),
        out_shape=(jax.ShapeDtypeStruct((M, D), F32), jax.ShapeDtypeStruct((M, D), BF16),
                   jax.ShapeDtypeStruct((8, D), F32), jax.ShapeDtypeStruct((8, D), F32)),
        compiler_params=_params(("arbitrary",)),
    )(dy, z, g)


def rms_fwd(ya, yb, ga, gb, *, name, tm=512):
    M, WA = ya.shape
    WB = yb.shape[1]
    tm = _tile(M, tm, 8)

    def body(ya_ref, yb_ref, ga_ref, gb_ref, o_ref):
        for y_ref, g_ref, lo, w in ((ya_ref, ga_ref, 0, WA), (yb_ref, gb_ref, WA, WB)):
            y = y_ref[...]
            r = lax.rsqrt(jnp.mean(y * y, axis=-1, keepdims=True) + RMS_EPS)
            o_ref[:, lo:lo + w] = (y * r * g_ref[...]).astype(o_ref.dtype)

    return pl.pallas_call(
        body, name=name, grid=(M // tm,),
        in_specs=[pl.BlockSpec((tm, WA), lambda i: (i, 0)), pl.BlockSpec((tm, WB), lambda i: (i, 0)),
                  pl.BlockSpec((1, WA), lambda i: (0, 0)), pl.BlockSpec((1, WB), lambda i: (0, 0))],
        out_specs=pl.BlockSpec((tm, WA + WB), lambda i: (i, 0)),
        out_shape=jax.ShapeDtypeStruct((M, WA + WB), BF16),
        compiler_params=_params(("parallel",)),
    )(ya, yb, ga, gb)


def rms_bwd(dy, ya, yb, ga, gb, *, name, tm=512):
    M, WA = ya.shape
    WB = yb.shape[1]
    tm = _tile(M, tm, 8)

    def body(dy_ref, ya_ref, yb_ref, ga_ref, gb_ref, dya_ref, dyb_ref, dga_ref, dgb_ref):
        i = pl.program_id(0)

        @pl.when(i == 0)
        def _():
            dga_ref[...] = jnp.zeros_like(dga_ref)
            dgb_ref[...] = jnp.zeros_like(dgb_ref)

        for y_ref, g_ref, d_ref, dgr, lo, w in ((ya_ref, ga_ref, dya_ref, dga_ref, 0, WA),
                                                (yb_ref, gb_ref, dyb_ref, dgb_ref, WA, WB)):
            y = y_ref[...]
            d = dy_ref[:, lo:lo + w]
            r = lax.rsqrt(jnp.mean(y * y, axis=-1, keepdims=True) + RMS_EPS)
            n = y * r
            dn = d * g_ref[...]
            d_ref[...] = r * (dn - n * jnp.mean(dn * n, axis=-1, keepdims=True))
            dgr[...] += _fold8(d * n)

    return pl.pallas_call(
        body, name=name, grid=(M // tm,),
        in_specs=[pl.BlockSpec((tm, WA + WB), lambda i: (i, 0)),
                  pl.BlockSpec((tm, WA), lambda i: (i, 0)), pl.BlockSpec((tm, WB), lambda i: (i, 0)),
                  pl.BlockSpec((1, WA), lambda i: (0, 0)), pl.BlockSpec((1, WB), lambda i: (0, 0))],
        out_specs=(pl.BlockSpec((tm, WA), lambda i: (i, 0)), pl.BlockSpec((tm, WB), lambda i: (i, 0)),
                   pl.BlockSpec((8, WA), lambda i: (0, 0)), pl.BlockSpec((8, WB), lambda i: (0, 0))),
        out_shape=(jax.ShapeDtypeStruct((M, WA), F32), jax.ShapeDtypeStruct((M, WB), F32),
                   jax.ShapeDtypeStruct((8, WA), F32), jax.ShapeDtypeStruct((8, WB), F32)),
        compiler_params=_params(("arbitrary",)),
    )(dy, ya, yb, ga, gb)


def loss_head(y, target, *, name, tm=512):
    M, D = y.shape
    tm = _tile(M, tm, 8)

    def body(y_ref, t_ref, dy_ref, l_ref):
        i = pl.program_id(0)

        @pl.when(i == 0)
        def _():
            l_ref[...] = jnp.zeros_like(l_ref)

        e = y_ref[...] - t_ref[...]
        dy_ref[...] = e * (1.0 / D)
        l_ref[...] += _fold8(e * e)

    row = pl.BlockSpec((tm, D), lambda i: (i, 0))
    return pl.pallas_call(
        body, name=name, grid=(M // tm,),
        in_specs=[row, row],
        out_specs=(row, pl.BlockSpec((8, D), lambda i: (0, 0))),
        out_shape=(jax.ShapeDtypeStruct((M, D), F32), jax.ShapeDtypeStruct((8, D), F32)),
        compiler_params=_params(("arbitrary",)),
    )(y, target)


def _lane(shape):
    return lax.broadcasted_iota(jnp.int32, shape, len(shape) - 1)


def _swap(t, half, period):
    first = (_lane(t.shape) % period) < half
    return jnp.where(first, pltpu.roll(t, LANES - half, 1), pltpu.roll(t, half, 1))


def _rope(t, c, s, half, period):
    return t * c + _swap(t, half, period) * s


def _rope_t(g, c, s, half, period):
    return g * c - _swap(g, half, period) * s


def rope_tables(positions, rot_dim, period):
    half = rot_dim // 2
    inv_freq = ROPE_THETA ** (-jnp.arange(0, rot_dim, 2, dtype=F32) / rot_dim)
    ang = positions.astype(F32)[:, None] * inv_freq
    cos, sin = jnp.cos(ang), jnp.sin(ang)
    ones = jnp.ones((positions.shape[0], period - rot_dim), F32)
    c = jnp.concatenate([cos, cos, ones], axis=1)
    s = jnp.concatenate([-sin, sin, 0.0 * ones], axis=1)
    reps = LANES // period
    return jnp.tile(c, (1, reps)), jnp.tile(s, (1, reps))


def _branch_blocks(S):
    out = []
    for window, d in DILATED_BRANCHES:
        assert window // d == BLOCK and S % (d * BLOCK) == 0
        out.append((d, (S // d) // BLOCK))
    return out


def _rows(r, n, d):
    return pl.ds(r + n * (BLOCK * d), BLOCK, stride=d) if d > 1 else pl.ds(pl.multiple_of(n * BLOCK, BLOCK), BLOCK)


def _band_masks(n, strict_prev):
    qi = lax.broadcasted_iota(jnp.int32, (BLOCK, BLOCK), 0)
    kj = lax.broadcasted_iota(jnp.int32, (BLOCK, BLOCK), 1)
    cur = kj <= qi
    prev = ((kj > qi) if strict_prev else (kj >= qi)) & (n > 0)
    return cur, prev


def attn_a_fwd(h, ca, sa, *, n_heads, name):
    S = h.shape[0]
    scale = HEAD_DIM_A ** -0.5
    half = HEAD_DIM_A // 8
    branches = _branch_blocks(S)

    def body(q_ref, k_ref, v_ref, c_ref, s_ref, y_ref, lse_ref, qs, ks, m_s, l_s, acc_s):
        qs[...] = _rope(q_ref[...], c_ref[...], s_ref[...], half, HEAD_DIM_A)
        ks[...] = _rope(k_ref[...], c_ref[...], s_ref[...], half, HEAD_DIM_A)
        for bi, (d, nb) in enumerate(branches):
            def blk(idx, carry, bi=bi, d=d, nb=nb):
                r, n = idx // nb, idx % nb
                rc, rp = _rows(r, n, d), _rows(r, jnp.maximum(n - 1, 0), d)
                q = qs[rc, :].astype(BF16)
                cur, prev = _band_masks(n, False)
                sc = jnp.where(cur, _dot_nt(q, ks[rc, :].astype(BF16)) * scale, NEG_INF)
                sp = jnp.where(prev, _dot_nt(q, ks[rp, :].astype(BF16)) * scale, NEG_INF)
                m = jnp.maximum(jnp.max(sc, axis=-1, keepdims=True), jnp.max(sp, axis=-1, keepdims=True))
                pc, pp = jnp.exp(sc - m), jnp.exp(sp - m)
                l = jnp.sum(pc, axis=-1, keepdims=True) + jnp.sum(pp, axis=-1, keepdims=True)
                acc = _dot(pc.astype(BF16), v_ref[rc, :].astype(BF16)) + _dot(pp.astype(BF16), v_ref[rp, :].astype(BF16))
                mb = jnp.broadcast_to(m, (BLOCK, LANES))
                lb = jnp.broadcast_to(l, (BLOCK, LANES))
                if bi == 0:
                    m_s[rc, :], l_s[rc, :], acc_s[rc, :] = mb, lb, acc
                else:
                    m0 = m_s[rc, :]
                    mn = jnp.maximum(m0, mb)
                    a0, a1 = jnp.exp(m0 - mn), jnp.exp(mb - mn)
                    m_s[rc, :] = mn
                    l_s[rc, :] = l_s[rc, :] * a0 + lb * a1
                    acc_s[rc, :] = acc_s[rc, :] * a0 + acc * a1
                return carry

            lax.fori_loop(0, d * nb, blk, 0)
        y_ref[...] = acc_s[...] / l_s[...]
        lse_ref[...] = m_s[...] + jnp.log(l_s[...])

    col = lambda off: pl.BlockSpec((S, LANES), lambda hd: (0, off + hd))
    full = pl.BlockSpec((S, LANES), lambda hd: (0, 0))
    out = pl.BlockSpec((S, LANES), lambda hd: (0, hd))
    o_shape = jax.ShapeDtypeStruct((S, n_heads * LANES), F32)
    return pl.pallas_call(
        body, name=name, grid=(n_heads,),
        in_specs=[col(0), col(n_heads), col(2 * n_heads), full, full],
        out_specs=(out, out), out_shape=(o_shape, o_shape),
        scratch_shapes=[pltpu.VMEM((S, LANES), F32) for _ in range(5)],
        compiler_params=_params(("arbitrary",)),
    )(h, h, h, ca, sa)


def attn_a_bwd(h, ca, sa, ya, lse, dya, *, n_heads, name):
    S = h.shape[0]
    scale = HEAD_DIM_A ** -0.5
    half = HEAD_DIM_A // 8
    branches = _branch_blocks(S)

    def body(q_ref, k_ref, v_ref, c_ref, s_ref, y_ref, lse_ref, dy_ref, o_ref, qs, ks, dq_s, dk_s, dv_s):
        part = pl.program_id(1)

        @pl.when(part == 0)
        def _():
            qs[...] = _rope(q_ref[...], c_ref[...], s_ref[...], half, HEAD_DIM_A)
            ks[...] = _rope(k_ref[...], c_ref[...], s_ref[...], half, HEAD_DIM_A)
            dq_s[...] = jnp.zeros_like(dq_s)
            dk_s[...] = jnp.zeros_like(dk_s)
            dv_s[...] = jnp.zeros_like(dv_s)
            for d, nb in branches:
                def blk(idx, carry, d=d, nb=nb):
                    r, n = idx // nb, idx % nb
                    rc, rp = _rows(r, n, d), _rows(r, jnp.maximum(n - 1, 0), d)
                    q = qs[rc, :].astype(BF16)
                    dy = dy_ref[rc, :]
                    dsum = jnp.sum(dy * y_ref[rc, :], axis=-1, keepdims=True)
                    dyb = dy.astype(BF16)
                    lse_b = lse_ref[rc, :]
                    cur, prev = _band_masks(n, False)
                    dq = jnp.zeros((BLOCK, LANES), F32)
                    for rows, mask in ((rc, cur), (rp, prev)):
                        kb = ks[rows, :].astype(BF16)
                        vb = v_ref[rows, :].astype(BF16)
                        s = jnp.where(mask, _dot_nt(q, kb) * scale, NEG_INF)
                        p = jnp.exp(s - lse_b)
                        ds = (p * (_dot_nt(dyb, vb) - dsum) * scale).astype(BF16)
                        dv_s[rows, :] += _dot_tn(p.astype(BF16), dyb)
                        dk_s[rows, :] += _dot_tn(ds, q)
                        dq = dq + _dot(ds, kb)
                    dq_s[rc, :] += dq
                    return carry

                lax.fori_loop(0, d * nb, blk, 0)
            o_ref[...] = _rope_t(dq_s[...], c_ref[...], s_ref[...], half, HEAD_DIM_A).astype(o_ref.dtype)

        @pl.when(part == 1)
        def _():
            o_ref[...] = _rope_t(dk_s[...], c_ref[...], s_ref[...], half, HEAD_DIM_A).astype(o_ref.dtype)

        @pl.when(part == 2)
        def _():
            o_ref[...] = dv_s[...].astype(o_ref.dtype)

    col = lambda off: pl.BlockSpec((S, LANES), lambda hd, p: (0, off + hd))
    full = pl.BlockSpec((S, LANES), lambda hd, p: (0, 0))
    per_head = pl.BlockSpec((S, LANES), lambda hd, p: (0, hd))
    return pl.pallas_call(
        body, name=name, grid=(n_heads, 3),
        in_specs=[col(0), col(n_heads), col(2 * n_heads), full, full, per_head, per_head, per_head],
        out_specs=pl.BlockSpec((S, LANES), lambda hd, p: (0, p * n_heads + hd)),
        out_shape=jax.ShapeDtypeStruct((S, 3 * n_heads * LANES), BF16),
        scratch_shapes=[pltpu.VMEM((S, LANES), F32) for _ in range(5)],
        compiler_params=_params(("arbitrary", "arbitrary")),
    )(h, h, h, ca, sa, ya, lse, dya)


def _both_halves(t, g):
    low = _lane(t.shape) < HEAD_DIM_B
    return jnp.where(low == (g == 0), t, pltpu.roll(t, HEAD_DIM_B, 1))


def _stack_heads(t):
    low = _lane(t.shape) < HEAD_DIM_B
    return jnp.concatenate([jnp.where(low, t, 0.0), jnp.where(low, 0.0, t)], axis=0)


def _unstack_heads(t2):
    low = _lane((BLOCK, LANES)) < HEAD_DIM_B
    return jnp.where(low, t2[:BLOCK], t2[BLOCK:])


def _fold_halves(t):
    return t + pltpu.roll(t, HEAD_DIM_B, 1)


def _head_rows(ref, tile):
    a = ref[pl.ds(2 * tile, 1), :][:, :1]
    b = ref[pl.ds(2 * tile + 1, 1), :][:, :1]
    return jnp.concatenate([jnp.broadcast_to(a, (BLOCK, 1)), jnp.broadcast_to(b, (BLOCK, 1))], axis=0)


def attn_b_fwd(h, cb, sb, sinks_l, *, off_q, n_qtiles, name):
    S = h.shape[0]
    nblk = S // BLOCK
    scale = HEAD_DIM_B ** -0.5
    half = HEAD_DIM_B // 8
    tiles_per_group = n_qtiles // 2

    def body(q_ref, k_ref, v_ref, c_ref, s_ref, sink_ref, y_ref, lse_ref, qs, kg, vg):
        t = pl.program_id(0)
        g = t // tiles_per_group
        qs[...] = _rope(q_ref[...], c_ref[...], s_ref[...], half, HEAD_DIM_B)
        kg[...] = _both_halves(_rope(k_ref[...], c_ref[...], s_ref[...], half, HEAD_DIM_B), g)
        vg[...] = _both_halves(v_ref[...], g)
        sink = _head_rows(sink_ref, t)

        def blk(n, carry):
            rc = pl.ds(pl.multiple_of(n * BLOCK, BLOCK), BLOCK)
            rp = pl.ds(pl.multiple_of(jnp.maximum(n - 1, 0) * BLOCK, BLOCK), BLOCK)
            q2 = _stack_heads(qs[rc, :]).astype(BF16)
            cur, prev = _band_masks(n, True)
            cur2, prev2 = jnp.concatenate([cur, cur], 0), jnp.concatenate([prev, prev], 0)
            sc = jnp.where(cur2, _dot_nt(q2, kg[rc, :].astype(BF16)) * scale, NEG_INF)
            sp = jnp.where(prev2, _dot_nt(q2, kg[rp, :].astype(BF16)) * scale, NEG_INF)
            m = jnp.maximum(jnp.max(sc, axis=-1, keepdims=True), jnp.max(sp, axis=-1, keepdims=True))
            pc, pp = jnp.exp(sc - m), jnp.exp(sp - m)
            l = jnp.sum(pc, axis=-1, keepdims=True) + jnp.sum(pp, axis=-1, keepdims=True)
            acc = _dot(pc.astype(BF16), vg[rc, :].astype(BF16)) + _dot(pp.astype(BF16), vg[rp, :].astype(BF16))
            m2 = jnp.maximum(m, sink)
            c = jnp.exp(m - m2)
            den = l * c + jnp.exp(sink - m2)
            y_ref[rc, :] = _unstack_heads(acc * (c / den))
            lse_ref[rc, :] = _unstack_heads(jnp.broadcast_to(m2 + jnp.log(den), (2 * BLOCK, LANES)))
            return carry

        lax.fori_loop(0, nblk, blk, 0)

    full = lambda col: pl.BlockSpec((S, LANES), lambda t: (0, col))
    out = pl.BlockSpec((S, LANES), lambda t: (0, t))
    o_shape = jax.ShapeDtypeStruct((S, n_qtiles * LANES), F32)
    return pl.pallas_call(
        body, name=name, grid=(n_qtiles,),
        in_specs=[pl.BlockSpec((S, LANES), lambda t: (0, off_q + t)), full(off_q + n_qtiles), full(off_q + n_qtiles + 1),
                  full(0), full(0), pl.BlockSpec(sinks_l.shape, lambda t: (0, 0))],
        out_specs=(out, out), out_shape=(o_shape, o_shape),
        scratch_shapes=[pltpu.VMEM((S, LANES), F32) for _ in range(3)],
        compiler_params=_params(("arbitrary",)),
    )(h, h, h, cb, sb, sinks_l)


def attn_b_bwd(h, cb, sb, sinks_l, yb, lse, dyb, *, off_q, n_qtiles, name):
    S = h.shape[0]
    nblk = S // BLOCK
    scale = HEAD_DIM_B ** -0.5
    half = HEAD_DIM_B // 8
    tiles_per_group = n_qtiles // 2
    n_steps = n_qtiles + 2

    def body(q_ref, k_ref, v_ref, c_ref, s_ref, sink_ref, y_ref, lse_ref, dy_ref,
             dq_ref, dkv_ref, dsink_ref, qs, kg, vg, dk_acc, dv_acc):
        t = pl.program_id(0)

        @pl.when(t == 0)
        def _():
            dk_acc[...] = jnp.zeros_like(dk_acc)
            dv_acc[...] = jnp.zeros_like(dv_acc)
            dsink_ref[...] = jnp.zeros_like(dsink_ref)

        @pl.when(t < n_qtiles)
        def _():
            g = t // tiles_per_group
            qs[...] = _rope(q_ref[...], c_ref[...], s_ref[...], half, HEAD_DIM_B)
            kg[...] = _both_halves(_rope(k_ref[...], c_ref[...], s_ref[...], half, HEAD_DIM_B), g)
            vg[...] = _both_halves(v_ref[...], g)
            sink = _head_rows(sink_ref, t)

            def blk(n, dsink):
                rc = pl.ds(pl.multiple_of(n * BLOCK, BLOCK), BLOCK)
                rp = pl.ds(pl.multiple_of(jnp.maximum(n - 1, 0) * BLOCK, BLOCK), BLOCK)
                q2 = _stack_heads(qs[rc, :]).astype(BF16)
                dy2 = _stack_heads(dy_ref[rc, :])
                dsum = jnp.sum(dy2 * _stack_heads(y_ref[rc, :]), axis=-1, keepdims=True)
                dy2b = dy2.astype(BF16)
                lse_t = lse_ref[rc, :]
                lse2 = jnp.concatenate([lse_t[:, :1], lse_t[:, HEAD_DIM_B:HEAD_DIM_B + 1]], axis=0)
                cur, prev = _band_masks(n, True)
                dq2 = jnp.zeros((2 * BLOCK, LANES), F32)
                for rows, mask in ((rc, cur), (rp, prev)):
                    kb = kg[rows, :].astype(BF16)
                    vb = vg[rows, :].astype(BF16)
                    mask2 = jnp.concatenate([mask, mask], 0)
                    s = jnp.where(mask2, _dot_nt(q2, kb) * scale, NEG_INF)
                    p = jnp.exp(s - lse2)
                    ds = (p * (_dot_nt(dy2b, vb) - dsum) * scale).astype(BF16)
                    dv_acc[g, rows, :] += _fold_halves(_dot_tn(p.astype(BF16), dy2b))
                    dk_acc[g, rows, :] += _fold_halves(_dot_tn(ds, q2))
                    dq2 = dq2 + _dot(ds, kb)
                dq_ref[rc, :] = _unstack_heads(dq2)
                return dsink - jnp.exp(sink - lse2) * dsum

            dsink = lax.fori_loop(0, nblk, blk, jnp.zeros((2 * BLOCK, 1), F32))
            dq_ref[...] = _rope_t(dq_ref[...], c_ref[...], s_ref[...], half, HEAD_DIM_B)
            d0 = jnp.sum(dsink[:BLOCK], axis=0, keepdims=True)
            d1 = jnp.sum(dsink[BLOCK:], axis=0, keepdims=True)
            dsink_ref[pl.ds(2 * t, 1), :] = jnp.broadcast_to(d0, (1, LANES))
            dsink_ref[pl.ds(2 * t + 1, 1), :] = jnp.broadcast_to(d1, (1, LANES))

        low = _lane((S, LANES)) < HEAD_DIM_B

        @pl.when(t == n_qtiles)
        def _():
            dk = jnp.where(low, dk_acc[0], dk_acc[1])
            dkv_ref[...] = _rope_t(dk, c_ref[...], s_ref[...], half, HEAD_DIM_B)

        @pl.when(t == n_qtiles + 1)
        def _():
            dkv_ref[...] = jnp.where(low, dv_acc[0], dv_acc[1])

    qt = lambda t: jnp.minimum(t, n_qtiles - 1)
    full = lambda col: pl.BlockSpec((S, LANES), lambda t: (0, col))
    per_tile = pl.BlockSpec((S, LANES), lambda t: (0, qt(t)))
    return pl.pallas_call(
        body, name=name, grid=(n_steps,),
        in_specs=[pl.BlockSpec((S, LANES), lambda t: (0, off_q + qt(t))), full(off_q + n_qtiles),
                  full(off_q + n_qtiles + 1), full(0), full(0), pl.BlockSpec(sinks_l.shape, lambda t: (0, 0)),
                  per_tile, per_tile, per_tile],
        out_specs=(per_tile, pl.BlockSpec((S, LANES), lambda t: (0, jnp.maximum(t - n_qtiles, 0))),
                   pl.BlockSpec(sinks_l.shape, lambda t: (0, 0))),
        out_shape=(jax.ShapeDtypeStruct((S, n_qtiles * LANES), F32), jax.ShapeDtypeStruct((S, 2 * LANES), F32),
                   jax.ShapeDtypeStruct(sinks_l.shape, F32)),
        scratch_shapes=[pltpu.VMEM((S, LANES), F32) for _ in range(3)]
        + [pltpu.VMEM((2, S, LANES), F32), pltpu.VMEM((2, S, LANES), F32)],
        compiler_params=_params(("arbitrary",)),
    )(h, h, h, cb, sb, sinks_l, yb, lse, dyb)


def _win(r, nw, d):
    if d > 1:
        return pl.ds(r + nw * (BLOCK * d), 2 * BLOCK, stride=d)
    return pl.ds(pl.multiple_of(nw * BLOCK, BLOCK), 2 * BLOCK)


def _fwd_bias(strict_prev):
    qi = lax.broadcasted_iota(jnp.int32, (BLOCK, 2 * BLOCK), 0)
    kj = lax.broadcasted_iota(jnp.int32, (BLOCK, 2 * BLOCK), 1)
    first = kj < BLOCK
    kk = jnp.where(first, kj, kj - BLOCK)
    prev_ok = (kk > qi) if strict_prev else (kk >= qi)
    zero = first & (kk <= qi)
    mid = (first & prev_ok) | (jnp.logical_not(first) & (kk <= qi))
    return jnp.stack([jnp.where(zero, 0.0, NEG_INF), jnp.where(mid, 0.0, NEG_INF)])


def _bwd_bias(strict_prev):
    qi = lax.broadcasted_iota(jnp.int32, (2 * BLOCK, BLOCK), 0)
    kj = lax.broadcasted_iota(jnp.int32, (2 * BLOCK, BLOCK), 1)
    first = qi < BLOCK
    qq = jnp.where(first, qi, qi - BLOCK)
    prev_ok = (kj > qq) if strict_prev else (kj >= qq)
    mid = (first & (kj <= qq)) | (jnp.logical_not(first) & prev_ok)
    last = jnp.logical_not(first) & (kj <= qq)
    return jnp.stack([jnp.where(mid, 0.0, NEG_INF), jnp.where(last, 0.0, NEG_INF)])


UNROLL = 4


def attn_a_fwd(h, ca, sa, *, n_heads, name):
    S = h.shape[0]
    scale = HEAD_DIM_A ** -0.5
    half = HEAD_DIM_A // 8
    branches = _branch_blocks(S)

    def body(q_ref, k_ref, v_ref, c_ref, s_ref, y_ref, lse_ref, qs, ks, m_s, l_s, acc_s, bias):
        qs[...] = _rope(q_ref[...], c_ref[...], s_ref[...], half, HEAD_DIM_A)
        ks[...] = _rope(k_ref[...], c_ref[...], s_ref[...], half, HEAD_DIM_A)
        bias[...] = _fwd_bias(False)
        for bi, (d, nb) in enumerate(branches):
            assert nb >= 2 and (d * nb) % UNROLL == 0

            def blk(it, carry, bi=bi, d=d, nb=nb):
                rn = [((it * UNROLL + u) // nb, (it * UNROLL + u) % nb) for u in range(UNROLL)]
                rcs = [_rows(r, n, d) for r, n in rn]
                rws = [_win(r, jnp.maximum(n - 1, 0), d) for r, n in rn]
                ss = [_dot_nt(qs[rc, :].astype(BF16), ks[rw, :].astype(BF16)) * scale + bias[jnp.minimum(n, 1)]
                      for (r, n), rc, rw in zip(rn, rcs, rws)]
                ms = [jnp.max(s, axis=-1, keepdims=True) for s in ss]
                ps = [jnp.exp(s - m) for s, m in zip(ss, ms)]
                ls = [jnp.sum(p, axis=-1, keepdims=True) for p in ps]
                accs = [_dot(p.astype(BF16), v_ref[rw, :].astype(BF16)) for p, rw in zip(ps, rws)]
                for rc, m, l, acc in zip(rcs, ms, ls, accs):
                    mb = jnp.broadcast_to(m, (BLOCK, LANES))
                    lb = jnp.broadcast_to(l, (BLOCK, LANES))
                    if bi == 0:
                        m_s[rc, :], l_s[rc, :], acc_s[rc, :] = mb, lb, acc
                    else:
                        m0 = m_s[rc, :]
                        mn = jnp.maximum(m0, mb)
                        a0, a1 = jnp.exp(m0 - mn), jnp.exp(mb - mn)
                        m_s[rc, :] = mn
                        l_s[rc, :] = l_s[rc, :] * a0 + lb * a1
                        acc_s[rc, :] = acc_s[rc, :] * a0 + acc * a1
                return carry

            lax.fori_loop(0, d * nb // UNROLL, blk, 0)
        y_ref[...] = acc_s[...] / l_s[...]
        lse_ref[...] = m_s[...] + jnp.log(l_s[...])

    col = lambda off: pl.BlockSpec((S, LANES), lambda hd: (0, off + hd))
    full = pl.BlockSpec((S, LANES), lambda hd: (0, 0))
    out = pl.BlockSpec((S, LANES), lambda hd: (0, hd))
    o_shape = jax.ShapeDtypeStruct((S, n_heads * LANES), F32)
    return pl.pallas_call(
        body, name=name, grid=(n_heads,),
        in_specs=[col(0), col(n_heads), col(2 * n_heads), full, full],
        out_specs=(out, out), out_shape=(o_shape, o_shape),
        scratch_shapes=[pltpu.VMEM((S, LANES), F32) for _ in range(5)] + [pltpu.VMEM((2, BLOCK, 2 * BLOCK), F32)],
        compiler_params=_params(("arbitrary",)),
    )(h, h, h, ca, sa)


def attn_a_bwd(h, ca, sa, ya, lse, dya, *, n_heads, name):
    S = h.shape[0]
    scale = HEAD_DIM_A ** -0.5
    half = HEAD_DIM_A // 8
    branches = _branch_blocks(S)

    def body(q_ref, k_ref, v_ref, c_ref, s_ref, y_ref, lse_ref, dy_ref, o_ref, qs, ks, dq_s, dk_s, dv_s, bias):
        part = pl.program_id(1)

        @pl.when(part == 0)
        def _():
            qs[...] = _rope(q_ref[...], c_ref[...], s_ref[...], half, HEAD_DIM_A)
            ks[...] = _rope(k_ref[...], c_ref[...], s_ref[...], half, HEAD_DIM_A)
            dq_s[...] = jnp.zeros_like(dq_s)
            bias[...] = _bwd_bias(False)
            for bi, (d, nb) in enumerate(branches):
                assert nb >= 2 and (d * nb) % UNROLL == 0

                def blk(it, carry, bi=bi, d=d, nb=nb):
                    rj = [((it * UNROLL + u) // nb, (it * UNROLL + u) % nb) for u in range(UNROLL)]
                    rks = [_rows(r, j, d) for r, j in rj]
                    rws = [_win(r, jnp.minimum(j, nb - 2), d) for r, j in rj]
                    q2 = [qs[rw, :].astype(BF16) for rw in rws]
                    kb = [ks[rk, :].astype(BF16) for rk in rks]
                    dy2 = [dy_ref[rw, :] for rw in rws]
                    dsum = [jnp.sum(dy * y_ref[rw, :], axis=-1, keepdims=True) for dy, rw in zip(dy2, rws)]
                    dy2b = [dy.astype(BF16) for dy in dy2]
                    ss = [_dot_nt(q, k) * scale + bias[(j == nb - 1).astype(jnp.int32)] for q, k, (r, j) in zip(q2, kb, rj)]
                    dps = [_dot_nt(dy, v_ref[rk, :].astype(BF16)) for dy, rk in zip(dy2b, rks)]
                    ps = [jnp.exp(s - lse_ref[rw, :]) for s, rw in zip(ss, rws)]
                    dss = [(p * (dp - dm) * scale).astype(BF16) for p, dp, dm in zip(ps, dps, dsum)]
                    dvs = [_dot_tn(p.astype(BF16), dy) for p, dy in zip(ps, dy2b)]
                    dks = [_dot_tn(ds, q) for ds, q in zip(dss, q2)]
                    dqs = [_dot(ds, k) for ds, k in zip(dss, kb)]
                    for rk, rw, dv, dk, dq in zip(rks, rws, dvs, dks, dqs):
                        if bi == 0:
                            dv_s[rk, :], dk_s[rk, :] = dv, dk
                        else:
                            dv_s[rk, :] += dv
                            dk_s[rk, :] += dk
                        dq_s[rw, :] += dq
                    return carry

                lax.fori_loop(0, d * nb // UNROLL, blk, 0)
            o_ref[...] = _rope_t(dq_s[...], c_ref[...], s_ref[...], half, HEAD_DIM_A).astype(o_ref.dtype)

        @pl.when(part == 1)
        def _():
            o_ref[...] = _rope_t(dk_s[...], c_ref[...], s_ref[...], half, HEAD_DIM_A).astype(o_ref.dtype)

        @pl.when(part == 2)
        def _():
            o_ref[...] = dv_s[...].astype(o_ref.dtype)

    col = lambda off: pl.BlockSpec((S, LANES), lambda hd, p: (0, off + hd))
    full = pl.BlockSpec((S, LANES), lambda hd, p: (0, 0))
    per_head = pl.BlockSpec((S, LANES), lambda hd, p: (0, hd))
    return pl.pallas_call(
        body, name=name, grid=(n_heads, 3),
        in_specs=[col(0), col(n_heads), col(2 * n_heads), full, full, per_head, per_head, per_head],
        out_specs=pl.BlockSpec((S, LANES), lambda hd, p: (0, p * n_heads + hd)),
        out_shape=jax.ShapeDtypeStruct((S, h.shape[1]), BF16),
        scratch_shapes=[pltpu.VMEM((S, LANES), F32) for _ in range(5)] + [pltpu.VMEM((2, 2 * BLOCK, BLOCK), F32)],
        compiler_params=_params(("arbitrary", "arbitrary")),
    )(h, h, h, ca, sa, ya, lse, dya)


def _unstack_heads(t2):
    rows = t2.shape[0] // 2
    low = _lane((rows, LANES)) < HEAD_DIM_B
    return jnp.where(low, t2[:rows], t2[rows:])


def _head_rows(ref, tile, rows):
    a = ref[pl.ds(2 * tile, 1), :][:, :1]
    b = ref[pl.ds(2 * tile + 1, 1), :][:, :1]
    return jnp.concatenate([jnp.broadcast_to(a, (rows, 1)), jnp.broadcast_to(b, (rows, 1))], axis=0)


UNROLL_B = 2


def attn_b_fwd(h, cb, sb, sinks_l, *, off_q, n_qtiles, name):
    S = h.shape[0]
    nblk = S // BLOCK
    scale = HEAD_DIM_B ** -0.5
    half = HEAD_DIM_B // 8
    tiles_per_group = n_qtiles // 2
    assert nblk >= 2 and nblk % UNROLL_B == 0

    def body(q_ref, k_ref, v_ref, c_ref, s_ref, sink_ref, y_ref, lse_ref, qs, kg, vg, bias):
        t = pl.program_id(0)
        g = t // tiles_per_group
        qs[...] = _rope(q_ref[...], c_ref[...], s_ref[...], half, HEAD_DIM_B)

        @pl.when(t % tiles_per_group == 0)
        def _():
            kg[...] = _both_halves(_rope(k_ref[...], c_ref[...], s_ref[...], half, HEAD_DIM_B), g)
            vg[...] = _both_halves(v_ref[...], g)

        @pl.when(t == 0)
        def _():
            fb = _fwd_bias(True)
            bias[...] = jnp.concatenate([fb, fb], axis=1)

        sink = _head_rows(sink_ref, t, BLOCK)

        def blk(it, carry):
            ns = [it * UNROLL_B + u for u in range(UNROLL_B)]
            rcs = [pl.ds(pl.multiple_of(n * BLOCK, BLOCK), BLOCK) for n in ns]
            rws = [pl.ds(pl.multiple_of(jnp.maximum(n - 1, 0) * BLOCK, BLOCK), 2 * BLOCK) for n in ns]
            ss = [_dot_nt(_stack_heads(qs[rc, :]).astype(BF16), kg[rw, :].astype(BF16)) * scale + bias[jnp.minimum(n, 1)]
                  for n, rc, rw in zip(ns, rcs, rws)]
            ms = [jnp.max(s, axis=-1, keepdims=True) for s in ss]
            ps = [jnp.exp(s - m) for s, m in zip(ss, ms)]
            ls = [jnp.sum(p, axis=-1, keepdims=True) for p in ps]
            accs = [_dot(p.astype(BF16), vg[rw, :].astype(BF16)) for p, rw in zip(ps, rws)]
            for rc, m, l, acc in zip(rcs, ms, ls, accs):
                m2 = jnp.maximum(m, sink)
                c = jnp.exp(m - m2)
                den = l * c + jnp.exp(sink - m2)
                y_ref[rc, :] = _unstack_heads(acc * (c / den))
                lse_ref[rc, :] = _unstack_heads(jnp.broadcast_to(m2 + jnp.log(den), (2 * BLOCK, LANES)))
            return carry

        lax.fori_loop(0, nblk // UNROLL_B, blk, 0)

    full = lambda col: pl.BlockSpec((S, LANES), lambda t: (0, col))
    out = pl.BlockSpec((S, LANES), lambda t: (0, t))
    o_shape = jax.ShapeDtypeStruct((S, n_qtiles * LANES), F32)
    return pl.pallas_call(
        body, name=name, grid=(n_qtiles,),
        in_specs=[pl.BlockSpec((S, LANES), lambda t: (0, off_q + t)), full(off_q + n_qtiles), full(off_q + n_qtiles + 1),
                  full(0), full(0), pl.BlockSpec(sinks_l.shape, lambda t: (0, 0))],
        out_specs=(out, out), out_shape=(o_shape, o_shape),
        scratch_shapes=[pltpu.VMEM((S, LANES), F32) for _ in range(3)] + [pltpu.VMEM((2, 2 * BLOCK, 2 * BLOCK), F32)],
        compiler_params=_params(("arbitrary",)),
    )(h, h, h, cb, sb, sinks_l)


def attn_b_bwd(h, cb, sb, sinks_l, yb, lse, dyb, dh, *, off_q, n_qtiles, name):
    S = h.shape[0]
    nblk = S // BLOCK
    scale = HEAD_DIM_B ** -0.5
    half = HEAD_DIM_B // 8
    tiles_per_group = n_qtiles // 2
    n_steps = n_qtiles + 2
    W = 2 * BLOCK
    assert nblk >= 2 and nblk % UNROLL_B == 0

    def body(q_ref, k_ref, v_ref, c_ref, s_ref, sink_ref, y_ref, lse_ref, dy_ref, dh_in,
             o_ref, dsink_ref, qs, kg, vg, dk_acc, dv_acc, bias, dq_ref):
        t = pl.program_id(0)

        @pl.when(t == 0)
        def _():
            dk_acc[...] = jnp.zeros_like(dk_acc)
            dv_acc[...] = jnp.zeros_like(dv_acc)
            dsink_ref[...] = jnp.zeros_like(dsink_ref)
            bb = _bwd_bias(True)
            bias[...] = jnp.concatenate([bb, bb], axis=1)

        @pl.when(t < n_qtiles)
        def _():
            g = t // tiles_per_group
            qs[...] = _rope(q_ref[...], c_ref[...], s_ref[...], half, HEAD_DIM_B)

            @pl.when(t % tiles_per_group == 0)
            def _():
                kg[...] = _both_halves(_rope(k_ref[...], c_ref[...], s_ref[...], half, HEAD_DIM_B), g)
                vg[...] = _both_halves(v_ref[...], g)

            dq_ref[...] = jnp.zeros_like(dq_ref)
            sink = _head_rows(sink_ref, t, W)
            row = lax.broadcasted_iota(jnp.int32, (2 * W, 1), 0) % W
            low = _lane((W, LANES)) < HEAD_DIM_B

            def blk(it, dsink):
                js = [it * UNROLL_B + u for u in range(UNROLL_B)]
                rks = [pl.ds(pl.multiple_of(j * BLOCK, BLOCK), BLOCK) for j in js]
                rws = [pl.ds(pl.multiple_of(jnp.minimum(j, nblk - 2) * BLOCK, BLOCK), W) for j in js]
                q2 = [_stack_heads(qs[rw, :]).astype(BF16) for rw in rws]
                dy2 = [_stack_heads(dy_ref[rw, :]) for rw in rws]
                dsum = [jnp.sum(dy * _stack_heads(y_ref[rw, :]), axis=-1, keepdims=True) for dy, rw in zip(dy2, rws)]
                dy2b = [dy.astype(BF16) for dy in dy2]
                lse2 = []
                for rw in rws:
                    lt = lse_ref[rw, :]
                    lr = pltpu.roll(lt, HEAD_DIM_B, 1)
                    lse2.append(jnp.concatenate([jnp.where(low, lt, lr), jnp.where(low, lr, lt)], axis=0))
                kb = [kg[rk, :].astype(BF16) for rk in rks]
                ss = [_dot_nt(q, k) * scale + bias[(j == nblk - 1).astype(jnp.int32)] for q, k, j in zip(q2, kb, js)]
                dps = [_dot_nt(dy, vg[rk, :].astype(BF16)) for dy, rk in zip(dy2b, rks)]
                ps = [jnp.exp(s - l2) for s, l2 in zip(ss, lse2)]
                dss = [(p * (dp - dm) * scale).astype(BF16) for p, dp, dm in zip(ps, dps, dsum)]
                dvs = [_fold_halves(_dot_tn(p.astype(BF16), dy)) for p, dy in zip(ps, dy2b)]
                dks = [_fold_halves(_dot_tn(ds, q)) for ds, q in zip(dss, q2)]
                dqs = [_dot(ds, k) for ds, k in zip(dss, kb)]
                for j, rk, rw, dv, dk, dq, l2, dm in zip(js, rks, rws, dvs, dks, dqs, lse2, dsum):
                    dv_acc[g, rk, :] += dv
                    dk_acc[g, rk, :] += dk
                    dq_ref[rw, :] += _unstack_heads(dq)
                    diag = (row >= BLOCK).astype(jnp.int32) == (j == nblk - 1).astype(jnp.int32)
                    dsink = dsink - jnp.where(diag, jnp.exp(sink - l2[:, :1]) * dm, 0.0)
                return dsink

            dsink = lax.fori_loop(0, nblk // UNROLL_B, blk, jnp.zeros((2 * W, 1), F32))
            o_ref[...] = _rope_t(dq_ref[...], c_ref[...], s_ref[...], half, HEAD_DIM_B).astype(o_ref.dtype)
            d0 = jnp.sum(dsink[:W], axis=0, keepdims=True)
            d1 = jnp.sum(dsink[W:], axis=0, keepdims=True)
            dsink_ref[pl.ds(2 * t, 1), :] = jnp.broadcast_to(d0, (1, LANES))
            dsink_ref[pl.ds(2 * t + 1, 1), :] = jnp.broadcast_to(d1, (1, LANES))

        low_s = _lane((S, LANES)) < HEAD_DIM_B

        @pl.when(t == n_qtiles)
        def _():
            dk = jnp.where(low_s, dk_acc[0], dk_acc[1])
            o_ref[...] = _rope_t(dk, c_ref[...], s_ref[...], half, HEAD_DIM_B).astype(o_ref.dtype)

        @pl.when(t == n_qtiles + 1)
        def _():
            o_ref[...] = jnp.where(low_s, dv_acc[0], dv_acc[1]).astype(o_ref.dtype)

    qt = lambda t: jnp.minimum(t, n_qtiles - 1)
    full = lambda col: pl.BlockSpec((S, LANES), lambda t: (0, col))
    per_tile = pl.BlockSpec((S, LANES), lambda t: (0, qt(t)))
    return pl.pallas_call(
        body, name=name, grid=(n_steps,),
        in_specs=[pl.BlockSpec((S, LANES), lambda t: (0, off_q + qt(t))), full(off_q + n_qtiles),
                  full(off_q + n_qtiles + 1), full(0), full(0), pl.BlockSpec(sinks_l.shape, lambda t: (0, 0)),
                  per_tile, per_tile, per_tile, pl.BlockSpec(memory_space=pl.ANY)],
        out_specs=(pl.BlockSpec((S, LANES), lambda t: (0, off_q + t)), pl.BlockSpec(sinks_l.shape, lambda t: (0, 0))),
        out_shape=(jax.ShapeDtypeStruct(dh.shape, dh.dtype), jax.ShapeDtypeStruct(sinks_l.shape, F32)),
        input_output_aliases={9: 0},
        scratch_shapes=[pltpu.VMEM((S, LANES), F32) for _ in range(3)]
        + [pltpu.VMEM((2, S, LANES), F32), pltpu.VMEM((2, S, LANES), F32), pltpu.VMEM((2, 2 * W, BLOCK), F32),
           pltpu.VMEM((S, LANES), F32)],
        compiler_params=_params(("arbitrary",)),
    )(h, h, h, cb, sb, sinks_l, yb, lse, dyb, dh)


def mem_attn_fwd(q, kv, *, name, tm=512):
    S, D = q.shape
    n_mem = kv.shape[0]
    hd = D // N_MEM_HEADS
    scale = hd ** -0.5
    tm = _tile(S, tm, 8)

    def body(q_ref, kv_ref, o_ref):
        for hh in range(N_MEM_HEADS):
            cols = slice(hh * hd, (hh + 1) * hd)
            s = _dot_nt(q_ref[:, cols], kv_ref[:, cols]) * scale
            s = s - jnp.max(s, axis=-1, keepdims=True)
            e = jnp.exp(s)
            p = e / jnp.sum(e, axis=-1, keepdims=True)
            o_ref[:, cols] = _dot(p.astype(BF16), kv_ref[:, D + hh * hd:D + (hh + 1) * hd]).astype(o_ref.dtype)

    return pl.pallas_call(
        body, name=name, grid=(S // tm,),
        in_specs=[pl.BlockSpec((tm, D), lambda i: (i, 0)), pl.BlockSpec((n_mem, 2 * D), lambda i: (0, 0))],
        out_specs=pl.BlockSpec((tm, D), lambda i: (i, 0)),
        out_shape=jax.ShapeDtypeStruct((S, D), BF16),
        compiler_params=_params(("parallel",)),
    )(q, kv)


def mem_attn_bwd(q, kv, do, *, name, tm=512):
    S, D = q.shape
    n_mem = kv.shape[0]
    hd = D // N_MEM_HEADS
    scale = hd ** -0.5
    tm = _tile(S, tm, 8)

    def body(q_ref, kv_ref, do_ref, dq_ref, dkv_ref):
        i = pl.program_id(0)

        @pl.when(i == 0)
        def _():
            dkv_ref[...] = jnp.zeros_like(dkv_ref)

        for hh in range(N_MEM_HEADS):
            cols = slice(hh * hd, (hh + 1) * hd)
            vcols = slice(D + hh * hd, D + (hh + 1) * hd)
            qh, kh, vh, doh = q_ref[:, cols], kv_ref[:, cols], kv_ref[:, vcols], do_ref[:, cols]
            s = _dot_nt(qh, kh) * scale
            s = s - jnp.max(s, axis=-1, keepdims=True)
            e = jnp.exp(s)
            p = e / jnp.sum(e, axis=-1, keepdims=True)
            dp = _dot_nt(doh, vh)
            ds = (p * (dp - jnp.sum(dp * p, axis=-1, keepdims=True)) * scale).astype(BF16)
            dq_ref[:, cols] = _dot(ds, kh).astype(dq_ref.dtype)
            dkv_ref[:, cols] += _dot_tn(ds, qh)
            dkv_ref[:, vcols] += _dot_tn(p.astype(BF16), doh)

    row = pl.BlockSpec((tm, D), lambda i: (i, 0))
    kvs = pl.BlockSpec((n_mem, 2 * D), lambda i: (0, 0))
    return pl.pallas_call(
        body, name=name, grid=(S // tm,),
        in_specs=[row, kvs, row], out_specs=(row, kvs),
        out_shape=(jax.ShapeDtypeStruct((S, D), BF16), jax.ShapeDtypeStruct((n_mem, 2 * D), F32)),
        compiler_params=_params(("arbitrary",)),
    )(q, kv, do)


def _rows_view(t):
    return t.reshape(-1, t.shape[-1])


def _row_tile(rows, cols, target_elems=512 * 1024):
    return _tile(rows, max(8, target_elems // cols), 8)


def cast_bf16(w, *, name):
    v = _rows_view(w)
    R, C = v.shape
    tr = _row_tile(R, C)

    def body(w_ref, o_ref):
        o_ref[...] = w_ref[...].astype(BF16)

    spec = pl.BlockSpec((tr, C), lambda i: (i, 0))
    out = pl.pallas_call(body, name=name, grid=(R // tr,), in_specs=[spec], out_specs=spec,
                         out_shape=jax.ShapeDtypeStruct((R, C), BF16), compiler_params=_params(("parallel",)))(v)
    return out.reshape(w.shape)


def mesh_place():
    return tuple(lax.axis_index(a).astype(jnp.int32).reshape(1) for a in ("x", "y", "c"))


def pair_sum(p, r1, place, *, name):
    nsh, r, c = p.shape
    hr = r // 2
    tr = _row_tile(hr, c)
    nt = hr // tr

    def body(x_ref, y_ref, c_ref, p_ref, r_ref, o_ref):
        o_ref[...] = (p_ref[...].astype(F32) + r_ref[...].astype(F32)).astype(BF16)

    return pl.pallas_call(
        body, name=name,
        grid_spec=pltpu.PrefetchScalarGridSpec(
            num_scalar_prefetch=3, grid=(nsh, nt),
            in_specs=[pl.BlockSpec((None, tr, c), lambda s, i, x, y, cc: (s, cc[0] * nt + i, 0)),
                      pl.BlockSpec((None, tr, c), lambda s, i, x, y, cc: (s, i, 0))],
            out_specs=pl.BlockSpec((None, tr, c), lambda s, i, x, y, cc: (s, i, 0))),
        out_shape=jax.ShapeDtypeStruct((nsh, hr, c), BF16),
        compiler_params=_params(("parallel", "parallel")),
    )(*place, p, r1)


def cast_into_slot(w, li, place, *, name):
    _, R, C = w.shape
    tr = _row_tile(R, C)

    def body(x_ref, y_ref, c_ref, w_ref, o_ref):
        o_ref[...] = w_ref[...].astype(BF16)

    return pl.pallas_call(
        body, name=name,
        grid_spec=pltpu.PrefetchScalarGridSpec(
            num_scalar_prefetch=3, grid=(R // tr,),
            in_specs=[pl.BlockSpec((None, tr, C), lambda i, x, y, cc: (li, i, 0))],
            out_specs=pl.BlockSpec((None, tr, C), lambda i, x, y, cc: (2 * x[0] + y[0], i, 0))),
        out_shape=jax.ShapeDtypeStruct((N_CHIPS, R, C), BF16),
        compiler_params=_params(("parallel",)),
    )(*place, w)


def chip_sum(q, r2, place, *, name):
    _, hr, c = q.shape
    tr = _row_tile(hr, c, 256 * 1024)
    nt = hr // tr

    def body(x_ref, y_ref, c_ref, q_ref, r_ref, o_ref):
        acc = q_ref[...].astype(F32)
        for k in range(r_ref.shape[0]):
            acc = acc + r_ref[k].astype(F32)
        o_ref[...] = acc

    return pl.pallas_call(
        body, name=name,
        grid_spec=pltpu.PrefetchScalarGridSpec(
            num_scalar_prefetch=3, grid=(nt,),
            in_specs=[pl.BlockSpec((None, tr, c), lambda i, x, y, cc: (2 * x[0] + y[0], i, 0)),
                      pl.BlockSpec((r2.shape[0], tr, c), lambda i, x, y, cc: (0, i, 0))],
            out_specs=pl.BlockSpec((tr, c), lambda i, x, y, cc: (cc[0] * nt + i, 0))),
        out_shape=jax.ShapeDtypeStruct((2 * hr, c), F32),
        compiler_params=_params(("parallel",)),
    )(*place, q, r2)


def adamw(w, g, m, v, *, name):
    shape = w.shape
    wv, gv, mv, vv = (_rows_view(t) for t in (w, g, m, v))
    R, C = wv.shape
    tr = _row_tile(R, C, 256 * 1024)
    c1 = 1.0 / (1.0 - ADAM_B1 ** ADAM_STEP)
    c2 = 1.0 / (1.0 - ADAM_B2 ** ADAM_STEP)

    def body(w_ref, g_ref, m_ref, v_ref, d_ref, nm_ref, nv_ref):
        g_ = g_ref[...]
        nm = ADAM_B1 * m_ref[...] + (1.0 - ADAM_B1) * g_
        nv = ADAM_B2 * v_ref[...] + (1.0 - ADAM_B2) * (g_ * g_)
        m_hat = nm * c1
        v_hat = nv * c2
        d_ref[...] = -ADAM_LR * (m_hat / (jnp.sqrt(v_hat) + ADAM_EPS) + ADAM_WD * w_ref[...])
        nm_ref[...] = nm
        nv_ref[...] = nv

    spec = pl.BlockSpec((tr, C), lambda i: (i, 0))
    o = jax.ShapeDtypeStruct((R, C), F32)
    outs = pl.pallas_call(body, name=name, grid=(R // tr,), in_specs=[spec] * 4, out_specs=(spec,) * 3,
                          out_shape=(o, o, o), compiler_params=_params(("parallel",)))(wv, gv, mv, vv)
    return tuple(t.reshape(shape) for t in outs)


def _place():
    x, y, c = lax.axis_index("x"), lax.axis_index("y"), lax.axis_index("c")
    others = [(1 - x, y), (x, 1 - y), (1 - x, 1 - y)]
    return x, y, c, others


def _any_specs(n):
    return [pl.BlockSpec(memory_space=pl.ANY) for _ in range(n)]


HBM_SPEC = pl.BlockSpec(memory_space=pltpu.HBM)
SEM_SPEC = pl.BlockSpec(memory_space=pltpu.SEMAPHORE)
DATAFLOW = pltpu.SideEffectType.DATAFLOW_SIDE_EFFECTING
TOKEN = jax.ShapeDtypeStruct((8, LANES), F32)


def _in_hbm(arrays):
    return [pltpu.with_memory_space_constraint(a, pltpu.HBM) for a in arrays]


def _gather_copy(g, t, j, slot, px, py, c, send, recv):
    hr = g[t].shape[1] // 2
    rows = g[t].at[slot, pl.ds(c * hr, hr)]
    return pltpu.make_async_remote_copy(rows, rows, send.at[3 * t + j], recv.at[3 * t + j], device_id=(px, py, c), device_id_type=MESH)


def gather_start(gs, *, name):
    n = len(gs)

    def body(*refs):
        g, token = refs[:n], refs[-1]
        send, recv = refs[n], refs[n + 1]
        x, y, c, others = _place()
        for t in range(n):
            for j, (px, py) in enumerate(others):
                _gather_copy(g, t, j, 2 * x + y, px, py, c, send, recv).start()
        token[...] = jnp.zeros_like(token)

    outs = pl.pallas_call(
        body, name=name,
        in_specs=[HBM_SPEC] * n,
        out_specs=(SEM_SPEC, SEM_SPEC, *[HBM_SPEC] * n, pl.BlockSpec(memory_space=pltpu.VMEM)),
        out_shape=(pltpu.SemaphoreType.DMA((3 * n,)), pltpu.SemaphoreType.DMA((3 * n,)),
                   *[pltpu.HBM(g.shape, g.dtype) for g in gs], TOKEN),
        input_output_aliases={t: 2 + t for t in range(n)},
        compiler_params=pltpu.CompilerParams(has_side_effects=DATAFLOW),
    )(*_in_hbm(gs))
    return outs[0], outs[1], list(outs[2:2 + n]), outs[-1]


def gather_wait(send, recv, gs, after, *, name):
    n = len(gs)

    def body(*refs):
        g = refs[:n]
        send, recv = refs[n], refs[n + 1]
        x, y, c, others = _place()
        for t in range(n):
            for j, (px, py) in enumerate(others):
                _gather_copy(g, t, j, 2 * x + y, px, py, c, send, recv).wait_send()
                _gather_copy(g, t, j, 2 * px + py, px, py, c, send, recv).wait_recv()

    outs = pl.pallas_call(
        body, name=name,
        in_specs=[HBM_SPEC] * n + [SEM_SPEC, SEM_SPEC, pl.BlockSpec(memory_space=pl.ANY)],
        out_specs=tuple([HBM_SPEC] * n),
        out_shape=tuple(pltpu.HBM(g.shape, g.dtype) for g in gs),
        input_output_aliases={t: t for t in range(n)},
        compiler_params=pltpu.CompilerParams(has_side_effects=DATAFLOW),
    )(*gs, send, recv, after)
    return list(outs)


def gather_forward(gs, *, name):
    n = len(gs)

    def body(*refs):
        g = refs[n:2 * n]
        send, recv = refs[2 * n:]
        x, y, c, others = _place()
        cps = []
        for t in range(n):
            hr = g[t].shape[1] // 2
            for j, (px, py) in enumerate(others):
                rows = g[t].at[2 * px + py, pl.ds(c * hr, hr)]
                cp = pltpu.make_async_remote_copy(rows, rows, send.at[3 * t + j], recv.at[3 * t + j],
                                                  device_id=(x, y, 1 - c), device_id_type=MESH)
                cp.start()
                cps.append(cp)
        for t in range(n):
            hr = g[t].shape[1] // 2
            for j, (px, py) in enumerate(others):
                rows = g[t].at[2 * px + py, pl.ds((1 - c) * hr, hr)]
                pltpu.make_async_remote_copy(rows, rows, send.at[3 * t + j], recv.at[3 * t + j],
                                             device_id=(x, y, 1 - c), device_id_type=MESH).wait_recv()
        for cp in cps:
            cp.wait_send()

    return pl.pallas_call(
        body, name=name,
        in_specs=_any_specs(n), out_specs=_any_specs(n),
        out_shape=[jax.ShapeDtypeStruct(g.shape, g.dtype) for g in gs],
        input_output_aliases={t: t for t in range(n)},
        scratch_shapes=[pltpu.SemaphoreType.DMA((3 * n,)), pltpu.SemaphoreType.DMA((3 * n,))],
        compiler_params=pltpu.CompilerParams(has_side_effects=True),
    )(*gs)


def sibling_halves(parts, *, name):
    n = len(parts)

    def body(*refs):
        src, dst = refs[:n], refs[n:2 * n]
        send, recv = refs[2 * n:]
        x, y, c, _ = _place()
        cps = []
        for t in range(n):
            hr = src[t].shape[1] // 2
            cp = pltpu.make_async_remote_copy(src[t].at[:, pl.ds((1 - c) * hr, hr)], dst[t], send.at[t], recv.at[t],
                                              device_id=(x, y, 1 - c), device_id_type=MESH)
            cp.start()
            cps.append(cp)
        for cp in cps:
            cp.wait()

    return pl.pallas_call(
        body, name=name,
        in_specs=_any_specs(n), out_specs=_any_specs(n),
        out_shape=[jax.ShapeDtypeStruct((p.shape[0], p.shape[1] // 2, p.shape[2]), p.dtype) for p in parts],
        scratch_shapes=[pltpu.SemaphoreType.DMA((n,)), pltpu.SemaphoreType.DMA((n,))],
        compiler_params=pltpu.CompilerParams(has_side_effects=True),
    )(*parts)


def _chips_copy(q, land, t, j, px, py, c, send, recv):
    return pltpu.make_async_remote_copy(q[t].at[2 * px + py], land[t].at[j], send.at[3 * t + j], recv.at[3 * t + j],
                                        device_id=(px, py, c), device_id_type=MESH)


def chips_start(qs, *, name):
    n = len(qs)

    def body(*refs):
        q, land, token = refs[:n], refs[n:2 * n], refs[-1]
        send, recv = refs[2 * n], refs[2 * n + 1]
        x, y, c, others = _place()
        for t in range(n):
            for j, (px, py) in enumerate(others):
                _chips_copy(q, land, t, j, px, py, c, send, recv).start()
        token[...] = jnp.zeros_like(token)

    lands = [lax.empty((3,) + q.shape[1:], q.dtype) for q in qs]
    outs = pl.pallas_call(
        body, name=name,
        in_specs=[HBM_SPEC] * (2 * n),
        out_specs=(SEM_SPEC, SEM_SPEC, *[HBM_SPEC] * (2 * n), pl.BlockSpec(memory_space=pltpu.VMEM)),
        out_shape=(pltpu.SemaphoreType.DMA((3 * n,)), pltpu.SemaphoreType.DMA((3 * n,)),
                   *[pltpu.HBM(a.shape, a.dtype) for a in qs + lands], TOKEN),
        input_output_aliases={t: 2 + t for t in range(2 * n)},
        compiler_params=pltpu.CompilerParams(has_side_effects=DATAFLOW),
    )(*_in_hbm(qs + lands))
    return outs[0], outs[1], list(outs[2:2 + n]), list(outs[2 + n:2 + 2 * n]), outs[-1]


def chips_wait(send, recv, qs, lands, after, *, name):
    n = len(qs)

    def body(*refs):
        q, land = refs[:n], refs[n:2 * n]
        send, recv = refs[2 * n], refs[2 * n + 1]
        x, y, c, others = _place()
        for t in range(n):
            for j, (px, py) in enumerate(others):
                cp = _chips_copy(q, land, t, j, px, py, c, send, recv)
                cp.wait_send()
                cp.wait_recv()

    outs = pl.pallas_call(
        body, name=name,
        in_specs=[HBM_SPEC] * (2 * n) + [SEM_SPEC, SEM_SPEC, pl.BlockSpec(memory_space=pl.ANY)],
        out_specs=tuple([HBM_SPEC] * (2 * n)),
        out_shape=tuple(pltpu.HBM(a.shape, a.dtype) for a in qs + lands),
        input_output_aliases={t: t for t in range(2 * n)},
        compiler_params=pltpu.CompilerParams(has_side_effects=DATAFLOW),
    )(*qs, *lands, send, recv, after)
    return list(outs[:n]), list(outs[n:])


def join_halves(fulls, *, name):
    n = len(fulls)

    def body(*refs):
        g = refs[n:2 * n]
        send, recv = refs[2 * n:]
        x, y, c, _ = _place()
        cps = []
        for t in range(n):
            hr = g[t].shape[0] // 2
            rows = g[t].at[pl.ds(c * hr, hr)]
            cp = pltpu.make_async_remote_copy(rows, rows, send.at[t], recv.at[t], device_id=(x, y, 1 - c), device_id_type=MESH)
            cp.start()
            cps.append(cp)
        for t in range(n):
            hr = g[t].shape[0] // 2
            rows = g[t].at[pl.ds((1 - c) * hr, hr)]
            pltpu.make_async_remote_copy(rows, rows, send.at[t], recv.at[t],
                                         device_id=(x, y, 1 - c), device_id_type=MESH).wait_recv()
        for cp in cps:
            cp.wait_send()

    return pl.pallas_call(
        body, name=name,
        in_specs=_any_specs(n), out_specs=_any_specs(n),
        out_shape=[jax.ShapeDtypeStruct(g.shape, g.dtype) for g in fulls],
        input_output_aliases={t: t for t in range(n)},
        scratch_shapes=[pltpu.SemaphoreType.DMA((n,)), pltpu.SemaphoreType.DMA((n,))],
        compiler_params=pltpu.CompilerParams(has_side_effects=True),
    )(*fulls)


def allreduce_small(t, *, name):
    R, C = t.shape

    def body(t_ref, o_ref, land, send, recv):
        x, y, c, _ = _place()
        me = 4 * x + 2 * y + c
        land[me] = t_ref[...]
        cps = []
        for j in range(1, 8):
            px, py, pc = (x + (j >> 2)) % 2, (y + ((j >> 1) & 1)) % 2, (c + (j & 1)) % 2
            cp = pltpu.make_async_remote_copy(t_ref, land.at[me], send.at[j - 1], recv.at[j - 1],
                                              device_id=(px, py, pc), device_id_type=MESH)
            cp.start()
            cps.append(cp)
        for j in range(1, 8):
            px, py, pc = (x + (j >> 2)) % 2, (y + ((j >> 1) & 1)) % 2, (c + (j & 1)) % 2
            pltpu.make_async_remote_copy(t_ref, land.at[4 * px + 2 * py + pc], send.at[j - 1], recv.at[j - 1],
                                         device_id=(px, py, pc), device_id_type=MESH).wait_recv()
        for cp in cps:
            cp.wait_send()
        acc = land[0]
        for k in range(1, 8):
            acc = acc + land[k]
        o_ref[...] = acc

    return pl.pallas_call(
        body, name=name,
        in_specs=[pl.BlockSpec(memory_space=pltpu.VMEM)], out_specs=pl.BlockSpec(memory_space=pltpu.VMEM),
        out_shape=jax.ShapeDtypeStruct((R, C), F32),
        scratch_shapes=[pltpu.VMEM((8, R, C), F32), pltpu.SemaphoreType.DMA((7,)), pltpu.SemaphoreType.DMA((7,))],
        compiler_params=pltpu.CompilerParams(has_side_effects=True),
    )(t)


def _layer_fwd(x, xb, memb, w_in, rest, P, tabs, alpha, li):
    ca, sa, cb, sb = tabs
    nA = P["gn_a"].shape[1] // HEAD_DIM_A
    nQ = P["gn_b"].shape[1] // LANES
    nm = lambda s: f"L{li}_{s}"
    h = mm_nn(xb, w_in, name=nm("h"), out_dtype=F32, tn=2304)
    ya, lse_a = attn_a_fwd(h, ca, sa, n_heads=nA, name=nm("attn_a"))
    yb, lse_b = attn_b_fwd(h, cb, sb, P["sinks_l"], off_q=3 * nA, n_qtiles=nQ, name=nm("attn_b"))
    ymix = rms_fwd(ya, yb, P["gn_a"], P["gn_b"], name=nm("rms"))
    W, P = rest(ymix, P)
    z1, x1, x1b = mm_ln(ymix, W["w_out"][0], x, P["ln_mix_g"], P["ln_mix_b"], name=nm("out_ln"), alpha=alpha)
    qm = mm_nn(x1b, W["w_mq"], name=nm("mq"), out_dtype=BF16)
    kv = mm_nn(memb, W["w_mkv"], name=nm("mkv"), out_dtype=BF16, tm=256)
    o = mem_attn_fwd(qm, kv, name=nm("mem_attn"))
    z2, x2, x2b = mm_ln(o, W["w_mo"][0], x1, P["ln_mem_g"], P["ln_mem_b"], name=nm("mo_ln"), alpha=alpha)
    u, a = mm_nn(x2b, W["w_up"], name=nm("up"), out_dtype=BF16, relu2=True)
    z3, x3, x3b = mm_ln(a, W["w_down"][0], x2, P["ln_ff_g"], P["ln_ff_b"], name=nm("down_ln"), alpha=alpha)
    saved = dict(xb=xb, h=h, ya=ya, lse_a=lse_a, yb=yb, lse_b=lse_b, ymix=ymix, z1=z1, x1b=x1b, qm=qm, kv=kv, o=o,
                 z2=z2, x2b=x2b, u=u, a=a, z3=z3)
    return x3, x3b, saved


def _layer_bwd(dx3, sv, memb, W, P, tabs, alpha, li, hook=None):
    ca, sa, cb, sb = tabs
    nA = P["gn_a"].shape[1] // HEAD_DIM_A
    nQ = P["gn_b"].shape[1] // LANES
    nm = lambda s: f"L{li}_b_{s}"
    nsh = lambda k: W[k].shape[0]
    gw, gs = {}, {}
    dz3, dz3b, gs["ln_ff_g"], gs["ln_ff_b"] = ln_bwd(dx3, sv["z3"], P["ln_ff_g"], name=nm("ln_ff"))
    gw["w_down"] = mm_tn(sv["a"], dz3b, nsh("w_down"), name=nm("dw_down"))
    du = mm_nt(dz3b, W["w_down"], name=nm("du"), out_dtype=BF16, umul=sv["u"])
    gw["w_up"] = mm_tn(sv["x2b"], du, nsh("w_up"), name=nm("dw_up"))
    dx2 = mm_nt(du, W["w_up"], name=nm("dx2"), out_dtype=F32, resid=dz3, alpha=alpha)
    dz2, dz2b, gs["ln_mem_g"], gs["ln_mem_b"] = ln_bwd(dx2, sv["z2"], P["ln_mem_g"], name=nm("ln_mem"))
    gw["w_mo"] = mm_tn(sv["o"], dz2b, nsh("w_mo"), name=nm("dw_mo"))
    do = mm_nt(dz2b, W["w_mo"], name=nm("do"), out_dtype=BF16)
    dqm, dkv = mem_attn_bwd(sv["qm"], sv["kv"], do, name=nm("mem_attn"))
    gw["w_mq"] = mm_tn(sv["x1b"], dqm, nsh("w_mq"), name=nm("dw_mq"))
    gw["w_mkv"] = mm_tn(memb, cast_bf16(dkv, name=nm("dkv_cast")), nsh("w_mkv"), name=nm("dw_mkv"), tm=256)
    dx1 = mm_nt(dqm, W["w_mq"], name=nm("dx1"), out_dtype=F32, resid=dz2, alpha=alpha)
    if hook is not None:
        P = hook(gw, dx1, P)
    dz1, dz1b, gs["ln_mix_g"], gs["ln_mix_b"] = ln_bwd(dx1, sv["z1"], P["ln_mix_g"], name=nm("ln_mix"))
    gw["w_out"] = mm_tn(sv["ymix"], dz1b, nsh("w_out"), name=nm("dw_out"))
    dymix = mm_nt(dz1b, W["w_out"], name=nm("dymix"), out_dtype=F32)
    dya, dyb, gs["gn_a"], gs["gn_b"] = rms_bwd(dymix, sv["ya"], sv["yb"], P["gn_a"], P["gn_b"], name=nm("rms"))
    dh = attn_a_bwd(sv["h"], ca, sa, sv["ya"], sv["lse_a"], dya, n_heads=nA, name=nm("attn_a"))
    dh, gs["sinks"] = attn_b_bwd(sv["h"], cb, sb, P["sinks_l"], sv["yb"], sv["lse_b"], dyb, dh,
                                 off_q=3 * nA, n_qtiles=nQ, name=nm("attn_b"))
    gw["w_in"] = mm_tn(sv["xb"], dh, nsh("w_in"), name=nm("dw_in"), tn=2304)
    dx0 = mm_nt(dh, W["w_in"], name=nm("dx0"), out_dtype=F32, resid=dz1, alpha=alpha, tr=2304)
    return dx0, gw, gs


def _gathered_view(name, g):
    if name == "w_in":
        return jnp.concatenate([g[k] for k in range(N_CHIPS)], axis=1)[None]
    if name in COL_SHARDED:
        return g
    return g.reshape(1, g.shape[0] * g.shape[1], g.shape[2])


def _to_shards(name, gw):
    if name == "w_in":
        n = gw.shape[2] // N_CHIPS
        return jnp.stack([gw[0, :, k * n:(k + 1) * n] for k in range(N_CHIPS)])
    if name in COL_SHARDED:
        return gw
    return gw.reshape(N_CHIPS, gw.shape[1] // N_CHIPS, gw.shape[2])


def _step(x, mem, positions, loss_target, w, m, v):
    S, D = x.shape[1], x.shape[2]
    depth = w["w_in"].shape[0]
    alpha = (2 * depth) ** 0.25
    x0 = x[0]
    memb = cast_bf16(mem[0], name="mem_cast")
    pos = positions[0]
    tabs = rope_tables(pos, HEAD_DIM_A // 4, HEAD_DIM_A) + rope_tables(pos, HEAD_DIM_B // 4, HEAD_DIM_B)
    place = mesh_place()

    def small(li):
        P = {k: w[k][li][None] for k in ("gn_a", "gn_b", "ln_mix_g", "ln_mix_b", "ln_mem_g", "ln_mem_b", "ln_ff_g", "ln_ff_b")}
        P["sinks_l"] = jnp.broadcast_to(w["sinks"][li][:, None], (w["sinks"].shape[1], LANES))
        return P

    rest_names = tuple(k for k in BIG if k != "w_in")
    chain = [(0, ("w_in",)), (0, rest_names)] + [(li, BIG) for li in range(1, depth)]
    casts = [[cast_into_slot(w[k], li, place, name=f"L{li}_cast_{k}") for k in names] for li, names in chain]
    started = {0: gather_start(casts[0], name="G0_gather_start")}

    def land(gi, after):
        send, recv, gs, tok0 = started.pop(gi)
        gs = gather_wait(send, recv, gs, tok0 if after is None else after, name=f"G{gi}_gather_wait")
        token = None
        if gi + 1 < len(chain):
            started[gi + 1] = gather_start(casts[gi + 1], name=f"G{gi + 1}_gather_start")
            token = started[gi + 1][3]
        gs = gather_forward(gs, name=f"G{gi}_gather_fwd")
        return dict(zip(chain[gi][1], gs)), token

    def ordered(a, token):
        return a if token is None else a + token[:1, :1].astype(a.dtype)

    xs, xbs, saved, Ws = x0, cast_bf16(x0, name="x_cast"), [], []
    for li in range(depth):
        gi = 0 if li == 0 else li + 1
        got, token = land(gi, None if li == 0 else xs)
        W = {"w_in": ordered(_gathered_view("w_in", got["w_in"]), token)}

        def rest(after, P, li=li, got=got, W=W):
            if li == 0:
                got, token = land(1, after)
                P = dict(P, ln_mix_g=ordered(P["ln_mix_g"], token))
            W.update({k: _gathered_view(k, got[k]) for k in rest_names})
            return W, P

        xs, xbs, sv = _layer_fwd(xs, xbs, memb, W["w_in"], rest, small(li), tabs, alpha, li)
        saved.append(sv)
        Ws.append(W)
    dy, loss_part = loss_head(xs, loss_target[0], name="loss")
    loss = lax.psum(0.5 / D * jnp.sum(loss_part), ("x", "y", "c"))

    g_big = {k: [None] * depth for k in BIG}
    g_small = [None] * depth

    def begin(li, names, gw, tag):
        parts = [_to_shards(k, gw[k]) for k in names]
        r1 = sibling_halves(parts, name=f"L{li}{tag}_rs_sibling")
        qs = [pair_sum(p, r, place, name=f"L{li}_rs_pair_{k}") for k, p, r in zip(names, parts, r1)]
        send, recv, qs, lands, token = chips_start(qs, name=f"L{li}{tag}_rs_chips_start")
        return (li, names, tag, send, recv, qs, lands), token

    def finish(pending, after):
        li, names, tag, send, recv, qs, lands = pending
        qs, lands = chips_wait(send, recv, qs, lands, after, name=f"L{li}{tag}_rs_chips_wait")
        fulls = [chip_sum(q, r, place, name=f"L{li}_rs_sum_{k}") for k, q, r in zip(names, qs, lands)]
        for k, f in zip(names, join_halves(fulls, name=f"L{li}{tag}_rs_join")):
            g_big[k][li] = f

    early = ("w_mq", "w_mkv", "w_mo", "w_up", "w_down")
    late = tuple(k for k in BIG if k not in early)
    pendings, token = [], None
    for li in reversed(range(depth)):
        P = small(li)
        P["ln_ff_g"] = ordered(P["ln_ff_g"], token)
        hook = None
        if li == 0:
            def hook(gw, dx1, P):
                while pendings:
                    finish(pendings.pop(), dx1)
                pend, tok = begin(0, early, gw, "a")
                pendings.append(pend)
                return dict(P, ln_mix_g=ordered(P["ln_mix_g"], tok))
        dy, gw, g_small[li] = _layer_bwd(dy, saved[li], memb, Ws[li], P, tabs, alpha, li, hook)
        while pendings:
            finish(pendings.pop(), dy)
        pend, token = begin(li, late if li == 0 else BIG, gw, "b" if li == 0 else "")
        pendings.append(pend)
    finish(pendings.pop(), token)
    grad_x = dy[None]

    rows = []
    for li in range(depth):
        gs = g_small[li]
        for k in ("ln_mix_g", "ln_mix_b", "ln_mem_g", "ln_mem_b", "ln_ff_g", "ln_ff_b"):
            rows.append(jnp.sum(gs[k], axis=0, keepdims=True))
        rows.append(jnp.concatenate([jnp.sum(gs["gn_a"], axis=0, keepdims=True), jnp.sum(gs["gn_b"], axis=0, keepdims=True)], axis=1))
        sk = gs["sinks"][:, 0][None]
        rows.append(jnp.pad(sk, ((0, 0), (0, D - sk.shape[1]))))
    red = allreduce_small(jnp.concatenate(rows, axis=0), name="small_allreduce").reshape(depth, 8, D)
    wa = w["gn_a"].shape[1]
    grads = {k: jnp.stack(g_big[k]) for k in BIG}
    for j, k in enumerate(("ln_mix_g", "ln_mix_b", "ln_mem_g", "ln_mem_b", "ln_ff_g", "ln_ff_b")):
        grads[k] = red[:, j]
    grads["gn_a"] = red[:, 6, :wa]
    grads["gn_b"] = red[:, 6, wa:]
    grads["sinks"] = red[:, 7, :w["sinks"].shape[1]]

    delta, new_m, new_v = {}, {}, {}
    small_names = [k for k in w if k not in BIG]
    for k in BIG:
        delta[k], new_m[k], new_v[k] = adamw(w[k], grads[k], m[k], v[k], name=f"adamw_{k}")
    pack = lambda d: jnp.concatenate([jnp.pad(d[k], ((0, 0), (0, D - d[k].shape[1]))) for k in small_names], axis=0)
    ds, ms, vs = adamw(pack(w), pack(grads), pack(m), pack(v), name="adamw_small")
    for j, k in enumerate(small_names):
        sl = (slice(j * depth, (j + 1) * depth), slice(0, w[k].shape[1]))
        delta[k], new_m[k], new_v[k] = ds[sl], ms[sl], vs[sl]
    return loss, grad_x, grads, delta, new_m, new_v


WEIGHTS = ("w_in", "gn_a", "gn_b", "sinks", "w_out", "ln_mix_g", "ln_mix_b", "w_mq", "w_mkv", "w_mo",
           "ln_mem_g", "ln_mem_b", "w_up", "w_down", "ln_ff_g", "ln_ff_b")


def kernel(x, mem, positions, w_in, gn_a, gn_b, sinks, w_out, ln_mix_g, ln_mix_b, w_mq, w_mkv, w_mo, ln_mem_g, ln_mem_b, w_up, w_down, ln_ff_g, ln_ff_b, loss_target, m_w_in, m_gn_a, m_gn_b, m_sinks, m_w_out, m_ln_mix_g, m_ln_mix_b, m_w_mq, m_w_mkv, m_w_mo, m_ln_mem_g, m_ln_mem_b, m_w_up, m_w_down, m_ln_ff_g, m_ln_ff_b, v_w_in, v_gn_a, v_gn_b, v_sinks, v_w_out, v_ln_mix_g, v_ln_mix_b, v_w_mq, v_w_mkv, v_w_mo, v_ln_mem_g, v_ln_mem_b, v_w_up, v_w_down, v_ln_ff_g, v_ln_ff_b):
    w = dict(zip(WEIGHTS, (w_in, gn_a, gn_b, sinks, w_out, ln_mix_g, ln_mix_b, w_mq, w_mkv, w_mo, ln_mem_g, ln_mem_b, w_up, w_down, ln_ff_g, ln_ff_b)))
    m = dict(zip(WEIGHTS, (m_w_in, m_gn_a, m_gn_b, m_sinks, m_w_out, m_ln_mix_g, m_ln_mix_b, m_w_mq, m_w_mkv, m_w_mo, m_ln_mem_g, m_ln_mem_b, m_w_up, m_w_down, m_ln_ff_g, m_ln_ff_b)))
    v = dict(zip(WEIGHTS, (v_w_in, v_gn_a, v_gn_b, v_sinks, v_w_out, v_ln_mix_g, v_ln_mix_b, v_w_mq, v_w_mkv, v_w_mo, v_ln_mem_g, v_ln_mem_b, v_w_up, v_w_down, v_ln_ff_g, v_ln_ff_b)))
    loss, grad_x, grads, delta, new_m, new_v = _step(x, mem, positions, loss_target, w, m, v)
    return (loss, grad_x, *[grads[k] for k in WEIGHTS], *[delta[k] for k in WEIGHTS],
            *[new_m[k] for k in WEIGHTS], *[new_v[k] for k in WEIGHTS])
```

```python
import functools

import jax
import jax.numpy as jnp
from jax import lax
from jax.experimental import pallas as pl
from jax.experimental.pallas import tpu as pltpu

F32 = jnp.float32
BF16 = jnp.bfloat16
MESH = pl.DeviceIdType.MESH

HEAD_DIM_A = 128
HEAD_DIM_B = 64
LANES = 128
BLOCK = 128
DILATED_BRANCHES = ((128, 1), (512, 4), (2048, 16))
WINDOW_B = 128
N_MEM_HEADS = 4
ROPE_THETA = 500000.0
LN_EPS = 1e-5
RMS_EPS = 1e-6
NEG_INF = -1e30
ADAM_LR = 0.001
ADAM_B1 = 0.9
ADAM_B2 = 0.999
ADAM_EPS = 1e-08
ADAM_WD = 0.01
ADAM_STEP = 10
N_CHIPS = 4
VMEM_LIMIT = 56 * 1024 * 1024

BIG = ("w_in", "w_out", "w_mq", "w_mkv", "w_mo", "w_up", "w_down")
COL_SHARDED = ("w_in", "w_mkv", "w_up")


def _tile(n, target, mult=LANES):
    best = None
    t = mult
    while t <= min(n, target):
        if n % t == 0:
            best = t
        t += mult
    return best if best is not None else n


def _params(sem=None):
    return pltpu.CompilerParams(dimension_semantics=sem, vmem_limit_bytes=VMEM_LIMIT)


def _dot(a, b):
    return jnp.dot(a, b, preferred_element_type=F32)


def _dot_nt(a, b):
    return lax.dot_general(a, b, (((1,), (1,)), ((), ())), preferred_element_type=F32)


def _dot_tn(a, b):
    return lax.dot_general(a, b, (((0,), (0,)), ((), ())), preferred_element_type=F32)


def mm_nn(a, b3, *, name, out_dtype, relu2=False, tm=1024, tn=1024, tk=2048):
    M, K = a.shape
    nsh, _, nk = b3.shape
    tm, tn, tk = _tile(M, tm, 8), _tile(nk, tn), _tile(K, tk)
    nb, ksteps = nk // tn, K // tk

    def body(a_ref, b_ref, *rest):
        outs, scr = rest[:2 if relu2 else 1], rest[2 if relu2 else 1:]

        def finish(acc):
            if relu2:
                outs[0][...] = acc.astype(outs[0].dtype)
                r = jnp.maximum(acc, 0.0)
                outs[1][...] = (r * r).astype(outs[1].dtype)
            else:
                outs[0][...] = acc.astype(outs[0].dtype)

        if ksteps == 1:
            finish(_dot(a_ref[...], b_ref[...]))
        else:
            acc_ref = scr[0]
            k = pl.program_id(2)

            @pl.when(k == 0)
            def _():
                acc_ref[...] = jnp.zeros_like(acc_ref)

            acc_ref[...] += _dot(a_ref[...], b_ref[...])

            @pl.when(k == ksteps - 1)
            def _():
                finish(acc_ref[...])

    o_spec = pl.BlockSpec((tm, tn), lambda i, j, k: (i, j))
    o_shape = jax.ShapeDtypeStruct((M, nsh * nk), out_dtype)
    return pl.pallas_call(
        body, name=name,
        grid=(M // tm, nsh * nb, ksteps),
        in_specs=[pl.BlockSpec((tm, tk), lambda i, j, k: (i, k)),
                  pl.BlockSpec((None, tk, tn), lambda i, j, k: (j // nb, k, j % nb))],
        out_specs=(o_spec, o_spec) if relu2 else o_spec,
        out_shape=(o_shape, o_shape) if relu2 else o_shape,
        scratch_shapes=[] if ksteps == 1 else [pltpu.VMEM((tm, tn), F32)],
        compiler_params=_params(("parallel", "parallel", "arbitrary")),
    )(a, b3)


def mm_ln(a, w, resid, g, b, *, name, alpha, tm=512, tk=1024):
    M, K = a.shape
    D = w.shape[1]
    tm, tk = _tile(M, tm, 8), _tile(K, tk)
    ksteps = K // tk

    def body(a_ref, w_ref, r_ref, g_ref, b_ref, z_ref, xn_ref, xb_ref, acc_ref):
        k = pl.program_id(1)

        @pl.when(k == 0)
        def _():
            acc_ref[...] = jnp.zeros_like(acc_ref)

        acc_ref[...] += _dot(a_ref[...], w_ref[...])

        @pl.when(k == ksteps - 1)
        def _():
            z = alpha * r_ref[...] + acc_ref[...]
            mu = jnp.mean(z, axis=-1, keepdims=True)
            zc = z - mu
            var = jnp.mean(zc * zc, axis=-1, keepdims=True)
            xn = zc * lax.rsqrt(var + LN_EPS) * g_ref[...] + b_ref[...]
            z_ref[...] = z
            xn_ref[...] = xn
            xb_ref[...] = xn.astype(BF16)

    row = pl.BlockSpec((tm, D), lambda i, k: (i, 0))
    vec = pl.BlockSpec((1, D), lambda i, k: (0, 0))
    return pl.pallas_call(
        body, name=name,
        grid=(M // tm, ksteps),
        in_specs=[pl.BlockSpec((tm, tk), lambda i, k: (i, k)),
                  pl.BlockSpec((tk, D), lambda i, k: (k, 0)), row, vec, vec],
        out_specs=(row, row, row),
        out_shape=(jax.ShapeDtypeStruct((M, D), F32), jax.ShapeDtypeStruct((M, D), F32),
                   jax.ShapeDtypeStruct((M, D), BF16)),
        scratch_shapes=[pltpu.VMEM((tm, D), F32)],
        compiler_params=_params(("parallel", "arbitrary")),
    )(a, w, resid, g, b)


def mm_nt(a, b3, *, name, out_dtype, resid=None, alpha=1.0, umul=None, tm=1024, tko=1024, tr=2048):
    M, N = a.shape
    nsh, K, nk = b3.shape
    tm, tko, tr = _tile(M, tm, 8), _tile(K, tko), _tile(nk, tr)
    nb = nk // tr
    rsteps = nsh * nb

    def body(a_ref, b_ref, *rest):
        rest = list(rest)
        r_ref = rest.pop(0) if resid is not None else None
        u_ref = rest.pop(0) if umul is not None else None
        o_ref = rest.pop(0)

        def finish(acc):
            if r_ref is not None:
                acc = acc + alpha * r_ref[...]
            if u_ref is not None:
                acc = acc * (2.0 * jnp.maximum(u_ref[...].astype(F32), 0.0))
            o_ref[...] = acc.astype(o_ref.dtype)

        if rsteps == 1:
            finish(_dot_nt(a_ref[...], b_ref[...]))
        else:
            acc_ref = rest[0]
            r = pl.program_id(2)

            @pl.when(r == 0)
            def _():
                acc_ref[...] = jnp.zeros_like(acc_ref)

            acc_ref[...] += _dot_nt(a_ref[...], b_ref[...])

            @pl.when(r == rsteps - 1)
            def _():
                finish(acc_ref[...])

    o_spec = pl.BlockSpec((tm, tko), lambda i, j, r: (i, j))
    in_specs = [pl.BlockSpec((tm, tr), lambda i, j, r: (i, r)),
                pl.BlockSpec((None, tko, tr), lambda i, j, r: (r // nb, j, r % nb))]
    args = [a, b3]
    for extra in (resid, umul):
        if extra is not None:
            in_specs.append(o_spec)
            args.append(extra)
    return pl.pallas_call(
        body, name=name,
        grid=(M // tm, K // tko, rsteps),
        in_specs=in_specs, out_specs=o_spec,
        out_shape=jax.ShapeDtypeStruct((M, K), out_dtype),
        scratch_shapes=[] if rsteps == 1 else [pltpu.VMEM((tm, tko), F32)],
        compiler_params=_params(("parallel", "parallel", "arbitrary")),
    )(*args)


def mm_tn(a, g, nsh, *, name, tk=1024, tn=1024, tm=2048):
    M, K = a.shape
    N = g.shape[1]
    nk = N // nsh
    tk, tn, tm = _tile(K, tk), _tile(nk, tn), _tile(M, tm, 8)
    nb, msteps = nk // tn, M // tm

    def body(a_ref, g_ref, o_ref, acc_ref):
        m = pl.program_id(2)

        @pl.when(m == 0)
        def _():
            acc_ref[...] = jnp.zeros_like(acc_ref)

        acc_ref[...] += _dot_tn(a_ref[...], g_ref[...])

        @pl.when(m == msteps - 1)
        def _():
            o_ref[...] = acc_ref[...].astype(o_ref.dtype)

    return pl.pallas_call(
        body, name=name,
        grid=(K // tk, nsh * nb, msteps),
        in_specs=[pl.BlockSpec((tm, tk), lambda i, j, m: (m, i)),
                  pl.BlockSpec((tm, tn), lambda i, j, m: (m, j))],
        out_specs=pl.BlockSpec((None, tk, tn), lambda i, j, m: (j // nb, i, j % nb)),
        out_shape=jax.ShapeDtypeStruct((nsh, K, nk), BF16),
        scratch_shapes=[pltpu.VMEM((tk, tn), F32)],
        compiler_params=_params(("parallel", "parallel", "arbitrary")),
    )(a, g)


def _fold8(t):
    return t.reshape(t.shape[0] // 8, 8, t.shape[1]).sum(axis=0)


def ln_bwd(dy, z, g, *, name, tm=256):
    M, D = z.shape
    tm = _tile(M, tm, 8)

    def body(dy_ref, z_ref, g_ref, dz_ref, dzb_ref, dg_ref, db_ref):
        i = pl.program_id(0)
        z_ = z_ref[...]
        dy_ = dy_ref[...]
        mu = jnp.mean(z_, axis=-1, keepdims=True)
        zc = z_ - mu
        var = jnp.mean(zc * zc, axis=-1, keepdims=True)
        rstd = lax.rsqrt(var + LN_EPS)
        xh = zc * rstd
        dxh = dy_ * g_ref[...]
        m1 = jnp.mean(dxh, axis=-1, keepdims=True)
        m2 = jnp.mean(dxh * xh, axis=-1, keepdims=True)
        dz = rstd * (dxh - m1 - xh * m2)
        dz_ref[...] = dz
        dzb_ref[...] = dz.astype(BF16)

        @pl.when(i == 0)
        def _():
            dg_ref[...] = jnp.zeros_like(dg_ref)
            db_ref[...] = jnp.zeros_like(db_ref)

        dg_ref[...] += _fold8(dy_ * xh)
        db_ref[...] += _fold8(dy_)

    row = pl.BlockSpec((tm, D), lambda i: (i, 0))
    acc = pl.BlockSpec((8, D), lambda i: (0, 0))
    return pl.pallas_call(
        body, name=name, grid=(M // tm,),
        in_specs=[row, row, pl.BlockSpec((1, D), lambda i: (0, 0))],
        out_specs=(row, row, acc, acc),
        out_shape=(jax.ShapeDtypeStruct((M, D), F32), jax.ShapeDtypeStruct((M, D), BF16),
                   jax.ShapeDtypeStruct((8, D), F32), jax.ShapeDtypeStruct((8, D), F32)),
        compiler_params=_params(("arbitrary",)),
    )(dy, z, g)


def rms_fwd(ya, yb, ga, gb, *, name, tm=512):
    M, WA = ya.shape
    WB = yb.shape[1]
    tm = _tile(M, tm, 8)

    def body(ya_ref, yb_ref, ga_ref, gb_ref, o_ref):
        for y_ref, g_ref, lo, w in ((ya_ref, ga_ref, 0, WA), (yb_ref, gb_ref, WA, WB)):
            y = y_ref[...]
            r = lax.rsqrt(jnp.mean(y * y, axis=-1, keepdims=True) + RMS_EPS)
            o_ref[:, lo:lo + w] = (y * r * g_ref[...]).astype(o_ref.dtype)

    return pl.pallas_call(
        body, name=name, grid=(M // tm,),
        in_specs=[pl.BlockSpec((tm, WA), lambda i: (i, 0)), pl.BlockSpec((tm, WB), lambda i: (i, 0)),
                  pl.BlockSpec((1, WA), lambda i: (0, 0)), pl.BlockSpec((1, WB), lambda i: (0, 0))],
        out_specs=pl.BlockSpec((tm, WA + WB), lambda i: (i, 0)),
        out_shape=jax.ShapeDtypeStruct((M, WA + WB), BF16),
        compiler_params=_params(("parallel",)),
    )(ya, yb, ga, gb)


def rms_bwd(dy, ya, yb, ga, gb, *, name, tm=512):
    M, WA = ya.shape
    WB = yb.shape[1]
    tm = _tile(M, tm, 8)

    def body(dy_ref, ya_ref, yb_ref, ga_ref, gb_ref, dya_ref, dyb_ref, dga_ref, dgb_ref):
        i = pl.program_id(0)

        @pl.when(i == 0)
        def _():
            dga_ref[...] = jnp.zeros_like(dga_ref)
            dgb_ref[...] = jnp.zeros_like(dgb_ref)

        for y_ref, g_ref, d_ref, dgr, lo, w in ((ya_ref, ga_ref, dya_ref, dga_ref, 0, WA),
                                                (yb_ref, gb_ref, dyb_ref, dgb_ref, WA, WB)):
            y = y_ref[...]
            d = dy_ref[:, lo:lo + w]
            r = lax.rsqrt(jnp.mean(y * y, axis=-1, keepdims=True) + RMS_EPS)
            n = y * r
            dn = d * g_ref[...]
            d_ref[...] = r * (dn - n * jnp.mean(dn * n, axis=-1, keepdims=True))
            dgr[...] += _fold8(d * n)

    return pl.pallas_call(
        body, name=name, grid=(M // tm,),
        in_specs=[pl.BlockSpec((tm, WA + WB), lambda i: (i, 0)),
                  pl.BlockSpec((tm, WA), lambda i: (i, 0)), pl.BlockSpec((tm, WB), lambda i: (i, 0)),
                  pl.BlockSpec((1, WA), lambda i: (0, 0)), pl.BlockSpec((1, WB), lambda i: (0, 0))],
        out_specs=(pl.BlockSpec((tm, WA), lambda i: (i, 0)), pl.BlockSpec((tm, WB), lambda i: (i, 0)),
                   pl.BlockSpec((8, WA), lambda i: (0, 0)), pl.BlockSpec((8, WB), lambda i: (0, 0))),
        out_shape=(jax.ShapeDtypeStruct((M, WA), F32), jax.ShapeDtypeStruct((M, WB), F32),
                   jax.ShapeDtypeStruct((8, WA), F32), jax.ShapeDtypeStruct((8, WB), F32)),
        compiler_params=_params(("arbitrary",)),
    )(dy, ya, yb, ga, gb)


def loss_head(y, target, *, name, tm=512):
    M, D = y.shape
    tm = _tile(M, tm, 8)

    def body(y_ref, t_ref, dy_ref, l_ref):
        i = pl.program_id(0)

        @pl.when(i == 0)
        def _():
            l_ref[...] = jnp.zeros_like(l_ref)

        e = y_ref[...] - t_ref[...]
        dy_ref[...] = e * (1.0 / D)
        l_ref[...] += _fold8(e * e)

    row = pl.BlockSpec((tm, D), lambda i: (i, 0))
    return pl.pallas_call(
        body, name=name, grid=(M // tm,),
        in_specs=[row, row],
        out_specs=(row, pl.BlockSpec((8, D), lambda i: (0, 0))),
        out_shape=(jax.ShapeDtypeStruct((M, D), F32), jax.ShapeDtypeStruct((8, D), F32)),
        compiler_params=_params(("arbitrary",)),
    )(y, target)


def _lane(shape):
    return lax.broadcasted_iota(jnp.int32, shape, len(shape) - 1)


def _swap(t, half, period):
    first = (_lane(t.shape) % period) < half
    return jnp.where(first, pltpu.roll(t, LANES - half, 1), pltpu.roll(t, half, 1))


def _rope(t, c, s, half, period):
    return t * c + _swap(t, half, period) * s


def _rope_t(g, c, s, half, period):
    return g * c - _swap(g, half, period) * s


def rope_tables(positions, rot_dim, period):
    half = rot_dim // 2
    inv_freq = ROPE_THETA ** (-jnp.arange(0, rot_dim, 2, dtype=F32) / rot_dim)
    ang = positions.astype(F32)[:, None] * inv_freq
    cos, sin = jnp.cos(ang), jnp.sin(ang)
    ones = jnp.ones((positions.shape[0], period - rot_dim), F32)
    c = jnp.concatenate([cos, cos, ones], axis=1)
    s = jnp.concatenate([-sin, sin, 0.0 * ones], axis=1)
    reps = LANES // period
    return jnp.tile(c, (1, reps)), jnp.tile(s, (1, reps))


def _branch_blocks(S):
    out = []
    for window, d in DILATED_BRANCHES:
        assert window // d == BLOCK and S % (d * BLOCK) == 0
        out.append((d, (S // d) // BLOCK))
    return out


def _rows(r, n, d):
    return pl.ds(r + n * (BLOCK * d), BLOCK, stride=d) if d > 1 else pl.ds(pl.multiple_of(n * BLOCK, BLOCK), BLOCK)


def _band_masks(n, strict_prev):
    qi = lax.broadcasted_iota(jnp.int32, (BLOCK, BLOCK), 0)
    kj = lax.broadcasted_iota(jnp.int32, (BLOCK, BLOCK), 1)
    cur = kj <= qi
    prev = ((kj > qi) if strict_prev else (kj >= qi)) & (n > 0)
    return cur, prev


def attn_a_fwd(h, ca, sa, *, n_heads, name):
    S = h.shape[0]
    scale = HEAD_DIM_A ** -0.5
    half = HEAD_DIM_A // 8
    branches = _branch_blocks(S)

    def body(q_ref, k_ref, v_ref, c_ref, s_ref, y_ref, lse_ref, qs, ks, m_s, l_s, acc_s):
        qs[...] = _rope(q_ref[...], c_ref[...], s_ref[...], half, HEAD_DIM_A)
        ks[...] = _rope(k_ref[...], c_ref[...], s_ref[...], half, HEAD_DIM_A)
        for bi, (d, nb) in enumerate(branches):
            def blk(idx, carry, bi=bi, d=d, nb=nb):
                r, n = idx // nb, idx % nb
                rc, rp = _rows(r, n, d), _rows(r, jnp.maximum(n - 1, 0), d)
                q = qs[rc, :].astype(BF16)
                cur, prev = _band_masks(n, False)
                sc = jnp.where(cur, _dot_nt(q, ks[rc, :].astype(BF16)) * scale, NEG_INF)
                sp = jnp.where(prev, _dot_nt(q, ks[rp, :].astype(BF16)) * scale, NEG_INF)
                m = jnp.maximum(jnp.max(sc, axis=-1, keepdims=True), jnp.max(sp, axis=-1, keepdims=True))
                pc, pp = jnp.exp(sc - m), jnp.exp(sp - m)
                l = jnp.sum(pc, axis=-1, keepdims=True) + jnp.sum(pp, axis=-1, keepdims=True)
                acc = _dot(pc.astype(BF16), v_ref[rc, :].astype(BF16)) + _dot(pp.astype(BF16), v_ref[rp, :].astype(BF16))
                mb = jnp.broadcast_to(m, (BLOCK, LANES))
                lb = jnp.broadcast_to(l, (BLOCK, LANES))
                if bi == 0:
                    m_s[rc, :], l_s[rc, :], acc_s[rc, :] = mb, lb, acc
                else:
                    m0 = m_s[rc, :]
                    mn = jnp.maximum(m0, mb)
                    a0, a1 = jnp.exp(m0 - mn), jnp.exp(mb - mn)
                    m_s[rc, :] = mn
                    l_s[rc, :] = l_s[rc, :] * a0 + lb * a1
                    acc_s[rc, :] = acc_s[rc, :] * a0 + acc * a1
                return carry

            lax.fori_loop(0, d * nb, blk, 0)
        y_ref[...] = acc_s[...] / l_s[...]
        lse_ref[...] = m_s[...] + jnp.log(l_s[...])

    col = lambda off: pl.BlockSpec((S, LANES), lambda hd: (0, off + hd))
    full = pl.BlockSpec((S, LANES), lambda hd: (0, 0))
    out = pl.BlockSpec((S, LANES), lambda hd: (0, hd))
    o_shape = jax.ShapeDtypeStruct((S, n_heads * LANES), F32)
    return pl.pallas_call(
        body, name=name, grid=(n_heads,),
        in_specs=[col(0), col(n_heads), col(2 * n_heads), full, full],
        out_specs=(out, out), out_shape=(o_shape, o_shape),
        scratch_shapes=[pltpu.VMEM((S, LANES), F32) for _ in range(5)],
        compiler_params=_params(("arbitrary",)),
    )(h, h, h, ca, sa)


def attn_a_bwd(h, ca, sa, ya, lse, dya, *, n_heads, name):
    S = h.shape[0]
    scale = HEAD_DIM_A ** -0.5
    half = HEAD_DIM_A // 8
    branches = _branch_blocks(S)

    def body(q_ref, k_ref, v_ref, c_ref, s_ref, y_ref, lse_ref, dy_ref, o_ref, qs, ks, dq_s, dk_s, dv_s):
        part = pl.program_id(1)

        @pl.when(part == 0)
        def _():
            qs[...] = _rope(q_ref[...], c_ref[...], s_ref[...], half, HEAD_DIM_A)
            ks[...] = _rope(k_ref[...], c_ref[...], s_ref[...], half, HEAD_DIM_A)
            dq_s[...] = jnp.zeros_like(dq_s)
            dk_s[...] = jnp.zeros_like(dk_s)
            dv_s[...] = jnp.zeros_like(dv_s)
            for d, nb in branches:
                def blk(idx, carry, d=d, nb=nb):
                    r, n = idx // nb, idx % nb
                    rc, rp = _rows(r, n, d), _rows(r, jnp.maximum(n - 1, 0), d)
                    q = qs[rc, :].astype(BF16)
                    dy = dy_ref[rc, :]
                    dsum = jnp.sum(dy * y_ref[rc, :], axis=-1, keepdims=True)
                    dyb = dy.astype(BF16)
                    lse_b = lse_ref[rc, :]
                    cur, prev = _band_masks(n, False)
                    dq = jnp.zeros((BLOCK, LANES), F32)
                    for rows, mask in ((rc, cur), (rp, prev)):
                        kb = ks[rows, :].astype(BF16)
                        vb = v_ref[rows, :].astype(BF16)
                        s = jnp.where(mask, _dot_nt(q, kb) * scale, NEG_INF)
                        p = jnp.exp(s - lse_b)
                        ds = (p * (_dot_nt(dyb, vb) - dsum) * scale).astype(BF16)
                        dv_s[rows, :] += _dot_tn(p.astype(BF16), dyb)
                        dk_s[rows, :] += _dot_tn(ds, q)
                        dq = dq + _dot(ds, kb)
                    dq_s[rc, :] += dq
                    return carry

                lax.fori_loop(0, d * nb, blk, 0)
            o_ref[...] = _rope_t(dq_s[...], c_ref[...], s_ref[...], half, HEAD_DIM_A).astype(o_ref.dtype)

        @pl.when(part == 1)
        def _():
            o_ref[...] = _rope_t(dk_s[...], c_ref[...], s_ref[...], half, HEAD_DIM_A).astype(o_ref.dtype)

        @pl.when(part == 2)
        def _():
            o_ref[...] = dv_s[...].astype(o_ref.dtype)

    col = lambda off: pl.BlockSpec((S, LANES), lambda hd, p: (0, off + hd))
    full = pl.BlockSpec((S, LANES), lambda hd, p: (0, 0))
    per_head = pl.BlockSpec((S, LANES), lambda hd, p: (0, hd))
    return pl.pallas_call(
        body, name=name, grid=(n_heads, 3),
        in_specs=[col(0), col(n_heads), col(2 * n_heads), full, full, per_head, per_head, per_head],
        out_specs=pl.BlockSpec((S, LANES), lambda hd, p: (0, p * n_heads + hd)),
        out_shape=jax.ShapeDtypeStruct((S, 3 * n_heads * LANES), BF16),
        scratch_shapes=[pltpu.VMEM((S, LANES), F32) for _ in range(5)],
        compiler_params=_params(("arbitrary", "arbitrary")),
    )(h, h, h, ca, sa, ya, lse, dya)


def _both_halves(t, g):
    low = _lane(t.shape) < HEAD_DIM_B
    return jnp.where(low == (g == 0), t, pltpu.roll(t, HEAD_DIM_B, 1))


def _stack_heads(t):
    low = _lane(t.shape) < HEAD_DIM_B
    return jnp.concatenate([jnp.where(low, t, 0.0), jnp.where(low, 0.0, t)], axis=0)


def _unstack_heads(t2):
    low = _lane((BLOCK, LANES)) < HEAD_DIM_B
    return jnp.where(low, t2[:BLOCK], t2[BLOCK:])


def _fold_halves(t):
    return t + pltpu.roll(t, HEAD_DIM_B, 1)


def _head_rows(ref, tile):
    a = ref[pl.ds(2 * tile, 1), :][:, :1]
    b = ref[pl.ds(2 * tile + 1, 1), :][:, :1]
    return jnp.concatenate([jnp.broadcast_to(a, (BLOCK, 1)), jnp.broadcast_to(b, (BLOCK, 1))], axis=0)


def attn_b_fwd(h, cb, sb, sinks_l, *, off_q, n_qtiles, name):
    S = h.shape[0]
    nblk = S // BLOCK
    scale = HEAD_DIM_B ** -0.5
    half = HEAD_DIM_B // 8
    tiles_per_group = n_qtiles // 2

    def body(q_ref, k_ref, v_ref, c_ref, s_ref, sink_ref, y_ref, lse_ref, qs, kg, vg):
        t = pl.program_id(0)
        g = t // tiles_per_group
        qs[...] = _rope(q_ref[...], c_ref[...], s_ref[...], half, HEAD_DIM_B)
        kg[...] = _both_halves(_rope(k_ref[...], c_ref[...], s_ref[...], half, HEAD_DIM_B), g)
        vg[...] = _both_halves(v_ref[...], g)
        sink = _head_rows(sink_ref, t)

        def blk(n, carry):
            rc = pl.ds(pl.multiple_of(n * BLOCK, BLOCK), BLOCK)
            rp = pl.ds(pl.multiple_of(jnp.maximum(n - 1, 0) * BLOCK, BLOCK), BLOCK)
            q2 = _stack_heads(qs[rc, :]).astype(BF16)
            cur, prev = _band_masks(n, True)
            cur2, prev2 = jnp.concatenate([cur, cur], 0), jnp.concatenate([prev, prev], 0)
            sc = jnp.where(cur2, _dot_nt(q2, kg[rc, :].astype(BF16)) * scale, NEG_INF)
            sp = jnp.where(prev2, _dot_nt(q2, kg[rp, :].astype(BF16)) * scale, NEG_INF)
            m = jnp.maximum(jnp.max(sc, axis=-1, keepdims=True), jnp.max(sp, axis=-1, keepdims=True))
            pc, pp = jnp.exp(sc - m), jnp.exp(sp - m)
            l = jnp.sum(pc, axis=-1, keepdims=True) + jnp.sum(pp, axis=-1, keepdims=True)
            acc = _dot(pc.astype(BF16), vg[rc, :].astype(BF16)) + _dot(pp.astype(BF16), vg[rp, :].astype(BF16))
            m2 = jnp.maximum(m, sink)
            c = jnp.exp(m - m2)
            den = l * c + jnp.exp(sink - m2)
            y_ref[rc, :] = _unstack_heads(acc * (c / den))
            lse_ref[rc, :] = _unstack_heads(jnp.broadcast_to(m2 + jnp.log(den), (2 * BLOCK, LANES)))
            return carry

        lax.fori_loop(0, nblk, blk, 0)

    full = lambda col: pl.BlockSpec((S, LANES), lambda t: (0, col))
    out = pl.BlockSpec((S, LANES), lambda t: (0, t))
    o_shape = jax.ShapeDtypeStruct((S, n_qtiles * LANES), F32)
    return pl.pallas_call(
        body, name=name, grid=(n_qtiles,),
        in_specs=[pl.BlockSpec((S, LANES), lambda t: (0, off_q + t)), full(off_q + n_qtiles), full(off_q + n_qtiles + 1),
                  full(0), full(0), pl.BlockSpec(sinks_l.shape, lambda t: (0, 0))],
        out_specs=(out, out), out_shape=(o_shape, o_shape),
        scratch_shapes=[pltpu.VMEM((S, LANES), F32) for _ in range(3)],
        compiler_params=_params(("arbitrary",)),
    )(h, h, h, cb, sb, sinks_l)


def attn_b_bwd(h, cb, sb, sinks_l, yb, lse, dyb, *, off_q, n_qtiles, name):
    S = h.shape[0]
    nblk = S // BLOCK
    scale = HEAD_DIM_B ** -0.5
    half = HEAD_DIM_B // 8
    tiles_per_group = n_qtiles // 2
    n_steps = n_qtiles + 2

    def body(q_ref, k_ref, v_ref, c_ref, s_ref, sink_ref, y_ref, lse_ref, dy_ref,
             dq_ref, dkv_ref, dsink_ref, qs, kg, vg, dk_acc, dv_acc):
        t = pl.program_id(0)

        @pl.when(t == 0)
        def _():
            dk_acc[...] = jnp.zeros_like(dk_acc)
            dv_acc[...] = jnp.zeros_like(dv_acc)
            dsink_ref[...] = jnp.zeros_like(dsink_ref)

        @pl.when(t < n_qtiles)
        def _():
            g = t // tiles_per_group
            qs[...] = _rope(q_ref[...], c_ref[...], s_ref[...], half, HEAD_DIM_B)
            kg[...] = _both_halves(_rope(k_ref[...], c_ref[...], s_ref[...], half, HEAD_DIM_B), g)
            vg[...] = _both_halves(v_ref[...], g)
            sink = _head_rows(sink_ref, t)

            def blk(n, dsink):
                rc = pl.ds(pl.multiple_of(n * BLOCK, BLOCK), BLOCK)
                rp = pl.ds(pl.multiple_of(jnp.maximum(n - 1, 0) * BLOCK, BLOCK), BLOCK)
                q2 = _stack_heads(qs[rc, :]).astype(BF16)
                dy2 = _stack_heads(dy_ref[rc, :])
                dsum = jnp.sum(dy2 * _stack_heads(y_ref[rc, :]), axis=-1, keepdims=True)
                dy2b = dy2.astype(BF16)
                lse_t = lse_ref[rc, :]
                lse2 = jnp.concatenate([lse_t[:, :1], lse_t[:, HEAD_DIM_B:HEAD_DIM_B + 1]], axis=0)
                cur, prev = _band_masks(n, True)
                dq2 = jnp.zeros((2 * BLOCK, LANES), F32)
                for rows, mask in ((rc, cur), (rp, prev)):
                    kb = kg[rows, :].astype(BF16)
                    vb = vg[rows, :].astype(BF16)
                    mask2 = jnp.concatenate([mask, mask], 0)
                    s = jnp.where(mask2, _dot_nt(q2, kb) * scale, NEG_INF)
                    p = jnp.exp(s - lse2)
                    ds = (p * (_dot_nt(dy2b, vb) - dsum) * scale).astype(BF16)
                    dv_acc[g, rows, :] += _fold_halves(_dot_tn(p.astype(BF16), dy2b))
                    dk_acc[g, rows, :] += _fold_halves(_dot_tn(ds, q2))
                    dq2 = dq2 + _dot(ds, kb)
                dq_ref[rc, :] = _unstack_heads(dq2)
                return dsink - jnp.exp(sink - lse2) * dsum

            dsink = lax.fori_loop(0, nblk, blk, jnp.zeros((2 * BLOCK, 1), F32))
            dq_ref[...] = _rope_t(dq_ref[...], c_ref[...], s_ref[...], half, HEAD_DIM_B)
            d0 = jnp.sum(dsink[:BLOCK], axis=0, keepdims=True)
            d1 = jnp.sum(dsink[BLOCK:], axis=0, keepdims=True)
            dsink_ref[pl.ds(2 * t, 1), :] = jnp.broadcast_to(d0, (1, LANES))
            dsink_ref[pl.ds(2 * t + 1, 1), :] = jnp.broadcast_to(d1, (1, LANES))

        low = _lane((S, LANES)) < HEAD_DIM_B

        @pl.when(t == n_qtiles)
        def _():
            dk = jnp.where(low, dk_acc[0], dk_acc[1])
            dkv_ref[...] = _rope_t(dk, c_ref[...], s_ref[...], half, HEAD_DIM_B)

        @pl.when(t == n_qtiles + 1)
        def _():
            dkv_ref[...] = jnp.where(low, dv_acc[0], dv_acc[1])

    qt = lambda t: jnp.minimum(t, n_qtiles - 1)
    full = lambda col: pl.BlockSpec((S, LANES), lambda t: (0, col))
    per_tile = pl.BlockSpec((S, LANES), lambda t: (0, qt(t)))
    return pl.pallas_call(
        body, name=name, grid=(n_steps,),
        in_specs=[pl.BlockSpec((S, LANES), lambda t: (0, off_q + qt(t))), full(off_q + n_qtiles),
                  full(off_q + n_qtiles + 1), full(0), full(0), pl.BlockSpec(sinks_l.shape, lambda t: (0, 0)),
                  per_tile, per_tile, per_tile],
        out_specs=(per_tile, pl.BlockSpec((S, LANES), lambda t: (0, jnp.maximum(t - n_qtiles, 0))),
                   pl.BlockSpec(sinks_l.shape, lambda t: (0, 0))),
        out_shape=(jax.ShapeDtypeStruct((S, n_qtiles * LANES), F32), jax.ShapeDtypeStruct((S, 2 * LANES), F32),
                   jax.ShapeDtypeStruct(sinks_l.shape, F32)),
        scratch_shapes=[pltpu.VMEM((S, LANES), F32) for _ in range(3)]
        + [pltpu.VMEM((2, S, LANES), F32), pltpu.VMEM((2, S, LANES), F32)],
        compiler_params=_params(("arbitrary",)),
    )(h, h, h, cb, sb, sinks_l, yb, lse, dyb)


def _win(r, nw, d):
    if d > 1:
        return pl.ds(r + nw * (BLOCK * d), 2 * BLOCK, stride=d)
    return pl.ds(pl.multiple_of(nw * BLOCK, BLOCK), 2 * BLOCK)


def _fwd_bias(strict_prev):
    qi = lax.broadcasted_iota(jnp.int32, (BLOCK, 2 * BLOCK), 0)
    kj = lax.broadcasted_iota(jnp.int32, (BLOCK, 2 * BLOCK), 1)
    first = kj < BLOCK
    kk = jnp.where(first, kj, kj - BLOCK)
    prev_ok = (kk > qi) if strict_prev else (kk >= qi)
    zero = first & (kk <= qi)
    mid = (first & prev_ok) | (jnp.logical_not(first) & (kk <= qi))
    return jnp.stack([jnp.where(zero, 0.0, NEG_INF), jnp.where(mid, 0.0, NEG_INF)])


def _bwd_bias(strict_prev):
    qi = lax.broadcasted_iota(jnp.int32, (2 * BLOCK, BLOCK), 0)
    kj = lax.broadcasted_iota(jnp.int32, (2 * BLOCK, BLOCK), 1)
    first = qi < BLOCK
    qq = jnp.where(first, qi, qi - BLOCK)
    prev_ok = (kj > qq) if strict_prev else (kj >= qq)
    mid = (first & (kj <= qq)) | (jnp.logical_not(first) & prev_ok)
    last = jnp.logical_not(first) & (kj <= qq)
    return jnp.stack([jnp.where(mid, 0.0, NEG_INF), jnp.where(last, 0.0, NEG_INF)])


UNROLL = 4


def attn_a_fwd(h, ca, sa, *, n_heads, name):
    S = h.shape[0]
    scale = HEAD_DIM_A ** -0.5
    half = HEAD_DIM_A // 8
    branches = _branch_blocks(S)

    def body(q_ref, k_ref, v_ref, c_ref, s_ref, y_ref, lse_ref, qs, ks, m_s, l_s, acc_s, bias):
        qs[...] = _rope(q_ref[...], c_ref[...], s_ref[...], half, HEAD_DIM_A)
        ks[...] = _rope(k_ref[...], c_ref[...], s_ref[...], half, HEAD_DIM_A)
        bias[...] = _fwd_bias(False)
        for bi, (d, nb) in enumerate(branches):
            assert nb >= 2 and (d * nb) % UNROLL == 0

            def blk(it, carry, bi=bi, d=d, nb=nb):
                rn = [((it * UNROLL + u) // nb, (it * UNROLL + u) % nb) for u in range(UNROLL)]
                rcs = [_rows(r, n, d) for r, n in rn]
                rws = [_win(r, jnp.maximum(n - 1, 0), d) for r, n in rn]
                ss = [_dot_nt(qs[rc, :].astype(BF16), ks[rw, :].astype(BF16)) * scale + bias[jnp.minimum(n, 1)]
                      for (r, n), rc, rw in zip(rn, rcs, rws)]
                ms = [jnp.max(s, axis=-1, keepdims=True) for s in ss]
                ps = [jnp.exp(s - m) for s, m in zip(ss, ms)]
                ls = [jnp.sum(p, axis=-1, keepdims=True) for p in ps]
                accs = [_dot(p.astype(BF16), v_ref[rw, :].astype(BF16)) for p, rw in zip(ps, rws)]
                for rc, m, l, acc in zip(rcs, ms, ls, accs):
                    mb = jnp.broadcast_to(m, (BLOCK, LANES))
                    lb = jnp.broadcast_to(l, (BLOCK, LANES))
                    if bi == 0:
                        m_s[rc, :], l_s[rc, :], acc_s[rc, :] = mb, lb, acc
                    else:
                        m0 = m_s[rc, :]
                        mn = jnp.maximum(m0, mb)
                        a0, a1 = jnp.exp(m0 - mn), jnp.exp(mb - mn)
                        m_s[rc, :] = mn
                        l_s[rc, :] = l_s[rc, :] * a0 + lb * a1
                        acc_s[rc, :] = acc_s[rc, :] * a0 + acc * a1
                return carry

            lax.fori_loop(0, d * nb // UNROLL, blk, 0)
        y_ref[...] = acc_s[...] / l_s[...]
        lse_ref[...] = m_s[...] + jnp.log(l_s[...])

    col = lambda off: pl.BlockSpec((S, LANES), lambda hd: (0, off + hd))
    full = pl.BlockSpec((S, LANES), lambda hd: (0, 0))
    out = pl.BlockSpec((S, LANES), lambda hd: (0, hd))
    o_shape = jax.ShapeDtypeStruct((S, n_heads * LANES), F32)
    return pl.pallas_call(
        body, name=name, grid=(n_heads,),
        in_specs=[col(0), col(n_heads), col(2 * n_heads), full, full],
        out_specs=(out, out), out_shape=(o_shape, o_shape),
        scratch_shapes=[pltpu.VMEM((S, LANES), F32) for _ in range(5)] + [pltpu.VMEM((2, BLOCK, 2 * BLOCK), F32)],
        compiler_params=_params(("arbitrary",)),
    )(h, h, h, ca, sa)


def attn_a_bwd(h, ca, sa, ya, lse, dya, *, n_heads, name):
    S = h.shape[0]
    scale = HEAD_DIM_A ** -0.5
    half = HEAD_DIM_A // 8
    branches = _branch_blocks(S)

    def body(q_ref, k_ref, v_ref, c_ref, s_ref, y_ref, lse_ref, dy_ref, o_ref, qs, ks, dq_s, dk_s, dv_s, bias):
        part = pl.program_id(1)

        @pl.when(part == 0)
        def _():
            qs[...] = _rope(q_ref[...], c_ref[...], s_ref[...], half, HEAD_DIM_A)
            ks[...] = _rope(k_ref[...], c_ref[...], s_ref[...], half, HEAD_DIM_A)
            dq_s[...] = jnp.zeros_like(dq_s)
            bias[...] = _bwd_bias(False)
            for bi, (d, nb) in enumerate(branches):
                assert nb >= 2 and (d * nb) % UNROLL == 0

                def blk(it, carry, bi=bi, d=d, nb=nb):
                    rj = [((it * UNROLL + u) // nb, (it * UNROLL + u) % nb) for u in range(UNROLL)]
                    rks = [_rows(r, j, d) for r, j in rj]
                    rws = [_win(r, jnp.minimum(j, nb - 2), d) for r, j in rj]
                    q2 = [qs[rw, :].astype(BF16) for rw in rws]
                    kb = [ks[rk, :].astype(BF16) for rk in rks]
                    dy2 = [dy_ref[rw, :] for rw in rws]
                    dsum = [jnp.sum(dy * y_ref[rw, :], axis=-1, keepdims=True) for dy, rw in zip(dy2, rws)]
                    dy2b = [dy.astype(BF16) for dy in dy2]
                    ss = [_dot_nt(q, k) * scale + bias[(j == nb - 1).astype(jnp.int32)] for q, k, (r, j) in zip(q2, kb, rj)]
                    dps = [_dot_nt(dy, v_ref[rk, :].astype(BF16)) for dy, rk in zip(dy2b, rks)]
                    ps = [jnp.exp(s - lse_ref[rw, :]) for s, rw in zip(ss, rws)]
                    dss = [(p * (dp - dm) * scale).astype(BF16) for p, dp, dm in zip(ps, dps, dsum)]
                    dvs = [_dot_tn(p.astype(BF16), dy) for p, dy in zip(ps, dy2b)]
                    dks = [_dot_tn(ds, q) for ds, q in zip(dss, q2)]
                    dqs = [_dot(ds, k) for ds, k in zip(dss, kb)]
                    for rk, rw, dv, dk, dq in zip(rks, rws, dvs, dks, dqs):
                        if bi == 0:
                            dv_s[rk, :], dk_s[rk, :] = dv, dk
                        else:
                            dv_s[rk, :] += dv
                            dk_s[rk, :] += dk
                        dq_s[rw, :] += dq
                    return carry

                lax.fori_loop(0, d * nb // UNROLL, blk, 0)
            o_ref[...] = _rope_t(dq_s[...], c_ref[...], s_ref[...], half, HEAD_DIM_A).astype(o_ref.dtype)

        @pl.when(part == 1)
        def _():
            o_ref[...] = _rope_t(dk_s[...], c_ref[...], s_ref[...], half, HEAD_DIM_A).astype(o_ref.dtype)

        @pl.when(part == 2)
        def _():
            o_ref[...] = dv_s[...].astype(o_ref.dtype)

    col = lambda off: pl.BlockSpec((S, LANES), lambda hd, p: (0, off + hd))
    full = pl.BlockSpec((S, LANES), lambda hd, p: (0, 0))
    per_head = pl.BlockSpec((S, LANES), lambda hd, p: (0, hd))
    return pl.pallas_call(
        body, name=name, grid=(n_heads, 3),
        in_specs=[col(0), col(n_heads), col(2 * n_heads), full, full, per_head, per_head, per_head],
        out_specs=pl.BlockSpec((S, LANES), lambda hd, p: (0, p * n_heads + hd)),
        out_shape=jax.ShapeDtypeStruct((S, h.shape[1]), BF16),
        scratch_shapes=[pltpu.VMEM((S, LANES), F32) for _ in range(5)] + [pltpu.VMEM((2, 2 * BLOCK, BLOCK), F32)],
        compiler_params=_params(("arbitrary", "arbitrary")),
    )(h, h, h, ca, sa, ya, lse, dya)


def _unstack_heads(t2):
    rows = t2.shape[0] // 2
    low = _lane((rows, LANES)) < HEAD_DIM_B
    return jnp.where(low, t2[:rows], t2[rows:])


def _head_rows(ref, tile, rows):
    a = ref[pl.ds(2 * tile, 1), :][:, :1]
    b = ref[pl.ds(2 * tile + 1, 1), :][:, :1]
    return jnp.concatenate([jnp.broadcast_to(a, (rows, 1)), jnp.broadcast_to(b, (rows, 1))], axis=0)


UNROLL_B = 2


def attn_b_fwd(h, cb, sb, sinks_l, *, off_q, n_qtiles, name):
    S = h.shape[0]
    nblk = S // BLOCK
    scale = HEAD_DIM_B ** -0.5
    half = HEAD_DIM_B // 8
    tiles_per_group = n_qtiles // 2
    assert nblk >= 2 and nblk % UNROLL_B == 0

    def body(q_ref, k_ref, v_ref, c_ref, s_ref, sink_ref, y_ref, lse_ref, qs, kg, vg, bias):
        t = pl.program_id(0)
        g = t // tiles_per_group
        qs[...] = _rope(q_ref[...], c_ref[...], s_ref[...], half, HEAD_DIM_B)

        @pl.when(t % tiles_per_group == 0)
        def _():
            kg[...] = _both_halves(_rope(k_ref[...], c_ref[...], s_ref[...], half, HEAD_DIM_B), g)
            vg[...] = _both_halves(v_ref[...], g)

        @pl.when(t == 0)
        def _():
            fb = _fwd_bias(True)
            bias[...] = jnp.concatenate([fb, fb], axis=1)

        sink = _head_rows(sink_ref, t, BLOCK)

        def blk(it, carry):
            ns = [it * UNROLL_B + u for u in range(UNROLL_B)]
            rcs = [pl.ds(pl.multiple_of(n * BLOCK, BLOCK), BLOCK) for n in ns]
            rws = [pl.ds(pl.multiple_of(jnp.maximum(n - 1, 0) * BLOCK, BLOCK), 2 * BLOCK) for n in ns]
            ss = [_dot_nt(_stack_heads(qs[rc, :]).astype(BF16), kg[rw, :].astype(BF16)) * scale + bias[jnp.minimum(n, 1)]
                  for n, rc, rw in zip(ns, rcs, rws)]
            ms = [jnp.max(s, axis=-1, keepdims=True) for s in ss]
            ps = [jnp.exp(s - m) for s, m in zip(ss, ms)]
            ls = [jnp.sum(p, axis=-1, keepdims=True) for p in ps]
            accs = [_dot(p.astype(BF16), vg[rw, :].astype(BF16)) for p, rw in zip(ps, rws)]
            for rc, m, l, acc in zip(rcs, ms, ls, accs):
                m2 = jnp.maximum(m, sink)
                c = jnp.exp(m - m2)
                den = l * c + jnp.exp(sink - m2)
                y_ref[rc, :] = _unstack_heads(acc * (c / den))
                lse_ref[rc, :] = _unstack_heads(jnp.broadcast_to(m2 + jnp.log(den), (2 * BLOCK, LANES)))
            return carry

        lax.fori_loop(0, nblk // UNROLL_B, blk, 0)

    full = lambda col: pl.BlockSpec((S, LANES), lambda t: (0, col))
    out = pl.BlockSpec((S, LANES), lambda t: (0, t))
    o_shape = jax.ShapeDtypeStruct((S, n_qtiles * LANES), F32)
    return pl.pallas_call(
        body, name=name, grid=(n_qtiles,),
        in_specs=[pl.BlockSpec((S, LANES), lambda t: (0, off_q + t)), full(off_q + n_qtiles), full(off_q + n_qtiles + 1),
                  full(0), full(0), pl.BlockSpec(sinks_l.shape, lambda t: (0, 0))],
        out_specs=(out, out), out_shape=(o_shape, o_shape),
        scratch_shapes=[pltpu.VMEM((S, LANES), F32) for _ in range(3)] + [pltpu.VMEM((2, 2 * BLOCK, 2 * BLOCK), F32)],
        compiler_params=_params(("arbitrary",)),
    )(h, h, h, cb, sb, sinks_l)


def attn_b_bwd(h, cb, sb, sinks_l, yb, lse, dyb, dh, *, off_q, n_qtiles, name):
    S = h.shape[0]
    nblk = S // BLOCK
    scale = HEAD_DIM_B ** -0.5
    half = HEAD_DIM_B // 8
    tiles_per_group = n_qtiles // 2
    n_steps = n_qtiles + 2
    W = 2 * BLOCK
    assert nblk >= 2 and nblk % UNROLL_B == 0

    def body(q_ref, k_ref, v_ref, c_ref, s_ref, sink_ref, y_ref, lse_ref, dy_ref, dh_in,
             o_ref, dsink_ref, qs, kg, vg, dk_acc, dv_acc, bias, dq_ref):
        t = pl.program_id(0)

        @pl.when(t == 0)
        def _():
            dk_acc[...] = jnp.zeros_like(dk_acc)
            dv_acc[...] = jnp.zeros_like(dv_acc)
            dsink_ref[...] = jnp.zeros_like(dsink_ref)
            bb = _bwd_bias(True)
            bias[...] = jnp.concatenate([bb, bb], axis=1)

        @pl.when(t < n_qtiles)
        def _():
            g = t // tiles_per_group
            qs[...] = _rope(q_ref[...], c_ref[...], s_ref[...], half, HEAD_DIM_B)

            @pl.when(t % tiles_per_group == 0)
            def _():
                kg[...] = _both_halves(_rope(k_ref[...], c_ref[...], s_ref[...], half, HEAD_DIM_B), g)
                vg[...] = _both_halves(v_ref[...], g)

            dq_ref[...] = jnp.zeros_like(dq_ref)
            sink = _head_rows(sink_ref, t, W)
            row = lax.broadcasted_iota(jnp.int32, (2 * W, 1), 0) % W
            low = _lane((W, LANES)) < HEAD_DIM_B

            def blk(it, dsink):
                js = [it * UNROLL_B + u for u in range(UNROLL_B)]
                rks = [pl.ds(pl.multiple_of(j * BLOCK, BLOCK), BLOCK) for j in js]
                rws = [pl.ds(pl.multiple_of(jnp.minimum(j, nblk - 2) * BLOCK, BLOCK), W) for j in js]
                q2 = [_stack_heads(qs[rw, :]).astype(BF16) for rw in rws]
                dy2 = [_stack_heads(dy_ref[rw, :]) for rw in rws]
                dsum = [jnp.sum(dy * _stack_heads(y_ref[rw, :]), axis=-1, keepdims=True) for dy, rw in zip(dy2, rws)]
                dy2b = [dy.astype(BF16) for dy in dy2]
                lse2 = []
                for rw in rws:
                    lt = lse_ref[rw, :]
                    lr = pltpu.roll(lt, HEAD_DIM_B, 1)
                    lse2.append(jnp.concatenate([jnp.where(low, lt, lr), jnp.where(low, lr, lt)], axis=0))
                kb = [kg[rk, :].astype(BF16) for rk in rks]
                ss = [_dot_nt(q, k) * scale + bias[(j == nblk - 1).astype(jnp.int32)] for q, k, j in zip(q2, kb, js)]
                dps = [_dot_nt(dy, vg[rk, :].astype(BF16)) for dy, rk in zip(dy2b, rks)]
                ps = [jnp.exp(s - l2) for s, l2 in zip(ss, lse2)]
                dss = [(p * (dp - dm) * scale).astype(BF16) for p, dp, dm in zip(ps, dps, dsum)]
                dvs = [_fold_halves(_dot_tn(p.astype(BF16), dy)) for p, dy in zip(ps, dy2b)]
                dks = [_fold_halves(_dot_tn(ds, q)) for ds, q in zip(dss, q2)]
                dqs = [_dot(ds, k) for ds, k in zip(dss, kb)]
                for j, rk, rw, dv, dk, dq, l2, dm in zip(js, rks, rws, dvs, dks, dqs, lse2, dsum):
                    dv_acc[g, rk, :] += dv
                    dk_acc[g, rk, :] += dk
                    dq_ref[rw, :] += _unstack_heads(dq)
                    diag = (row >= BLOCK).astype(jnp.int32) == (j == nblk - 1).astype(jnp.int32)
                    dsink = dsink - jnp.where(diag, jnp.exp(sink - l2[:, :1]) * dm, 0.0)
                return dsink

            dsink = lax.fori_loop(0, nblk // UNROLL_B, blk, jnp.zeros((2 * W, 1), F32))
            o_ref[...] = _rope_t(dq_ref[...], c_ref[...], s_ref[...], half, HEAD_DIM_B).astype(o_ref.dtype)
            d0 = jnp.sum(dsink[:W], axis=0, keepdims=True)
            d1 = jnp.sum(dsink[W:], axis=0, keepdims=True)
            dsink_ref[pl.ds(2 * t, 1), :] = jnp.broadcast_to(d0, (1, LANES))
            dsink_ref[pl.ds(2 * t + 1, 1), :] = jnp.broadcast_to(d1, (1, LANES))

        low_s = _lane((S, LANES)) < HEAD_DIM_B

        @pl.when(t == n_qtiles)
        def _():
            dk = jnp.where(low_s, dk_acc[0], dk_acc[1])
            o_ref[...] = _rope_t(dk, c_ref[...], s_ref[...], half, HEAD_DIM_B).astype(o_ref.dtype)

        @pl.when(t == n_qtiles + 1)
        def _():
            o_ref[...] = jnp.where(low_s, dv_acc[0], dv_acc[1]).astype(o_ref.dtype)

    qt = lambda t: jnp.minimum(t, n_qtiles - 1)
    full = lambda col: pl.BlockSpec((S, LANES), lambda t: (0, col))
    per_tile = pl.BlockSpec((S, LANES), lambda t: (0, qt(t)))
    return pl.pallas_call(
        body, name=name, grid=(n_steps,),
        in_specs=[pl.BlockSpec((S, LANES), lambda t: (0, off_q + qt(t))), full(off_q + n_qtiles),
                  full(off_q + n_qtiles + 1), full(0), full(0), pl.BlockSpec(sinks_l.shape, lambda t: (0, 0)),
                  per_tile, per_tile, per_tile, pl.BlockSpec(memory_space=pl.ANY)],
        out_specs=(pl.BlockSpec((S, LANES), lambda t: (0, off_q + t)), pl.BlockSpec(sinks_l.shape, lambda t: (0, 0))),
        out_shape=(jax.ShapeDtypeStruct(dh.shape, dh.dtype), jax.ShapeDtypeStruct(sinks_l.shape, F32)),
        input_output_aliases={9: 0},
        scratch_shapes=[pltpu.VMEM((S, LANES), F32) for _ in range(3)]
        + [pltpu.VMEM((2, S, LANES), F32), pltpu.VMEM((2, S, LANES), F32), pltpu.VMEM((2, 2 * W, BLOCK), F32),
           pltpu.VMEM((S, LANES), F32)],
        compiler_params=_params(("arbitrary",)),
    )(h, h, h, cb, sb, sinks_l, yb, lse, dyb, dh)


def mem_attn_fwd(q, kv, *, name, tm=512):
    S, D = q.shape
    n_mem = kv.shape[0]
    hd = D // N_MEM_HEADS
    scale = hd ** -0.5
    tm = _tile(S, tm, 8)

    def body(q_ref, kv_ref, o_ref):
        for hh in range(N_MEM_HEADS):
            cols = slice(hh * hd, (hh + 1) * hd)
            s = _dot_nt(q_ref[:, cols], kv_ref[:, cols]) * scale
            s = s - jnp.max(s, axis=-1, keepdims=True)
            e = jnp.exp(s)
            p = e / jnp.sum(e, axis=-1, keepdims=True)
            o_ref[:, cols] = _dot(p.astype(BF16), kv_ref[:, D + hh * hd:D + (hh + 1) * hd]).astype(o_ref.dtype)

    return pl.pallas_call(
        body, name=name, grid=(S // tm,),
        in_specs=[pl.BlockSpec((tm, D), lambda i: (i, 0)), pl.BlockSpec((n_mem, 2 * D), lambda i: (0, 0))],
        out_specs=pl.BlockSpec((tm, D), lambda i: (i, 0)),
        out_shape=jax.ShapeDtypeStruct((S, D), BF16),
        compiler_params=_params(("parallel",)),
    )(q, kv)


def mem_attn_bwd(q, kv, do, *, name, tm=512):
    S, D = q.shape
    n_mem = kv.shape[0]
    hd = D // N_MEM_HEADS
    scale = hd ** -0.5
    tm = _tile(S, tm, 8)

    def body(q_ref, kv_ref, do_ref, dq_ref, dkv_ref):
        i = pl.program_id(0)

        @pl.when(i == 0)
        def _():
            dkv_ref[...] = jnp.zeros_like(dkv_ref)

        for hh in range(N_MEM_HEADS):
            cols = slice(hh * hd, (hh + 1) * hd)
            vcols = slice(D + hh * hd, D + (hh + 1) * hd)
            qh, kh, vh, doh = q_ref[:, cols], kv_ref[:, cols], kv_ref[:, vcols], do_ref[:, cols]
            s = _dot_nt(qh, kh) * scale
            s = s - jnp.max(s, axis=-1, keepdims=True)
            e = jnp.exp(s)
            p = e / jnp.sum(e, axis=-1, keepdims=True)
            dp = _dot_nt(doh, vh)
            ds = (p * (dp - jnp.sum(dp * p, axis=-1, keepdims=True)) * scale).astype(BF16)
            dq_ref[:, cols] = _dot(ds, kh).astype(dq_ref.dtype)
            dkv_ref[:, cols] += _dot_tn(ds, qh)
            dkv_ref[:, vcols] += _dot_tn(p.astype(BF16), doh)

    row = pl.BlockSpec((tm, D), lambda i: (i, 0))
    kvs = pl.BlockSpec((n_mem, 2 * D), lambda i: (0, 0))
    return pl.pallas_call(
        body, name=name, grid=(S // tm,),
        in_specs=[row, kvs, row], out_specs=(row, kvs),
        out_shape=(jax.ShapeDtypeStruct((S, D), BF16), jax.ShapeDtypeStruct((n_mem, 2 * D), F32)),
        compiler_params=_params(("arbitrary",)),
    )(q, kv, do)


def _rows_view(t):
    return t.reshape(-1, t.shape[-1])


def _row_tile(rows, cols, target_elems=512 * 1024):
    return _tile(rows, max(8, target_elems // cols), 8)


def cast_bf16(w, *, name):
    v = _rows_view(w)
    R, C = v.shape
    tr = _row_tile(R, C)

    def body(w_ref, o_ref):
        o_ref[...] = w_ref[...].astype(BF16)

    spec = pl.BlockSpec((tr, C), lambda i: (i, 0))
    out = pl.pallas_call(body, name=name, grid=(R // tr,), in_specs=[spec], out_specs=spec,
                         out_shape=jax.ShapeDtypeStruct((R, C), BF16), compiler_params=_params(("parallel",)))(v)
    return out.reshape(w.shape)


def mesh_place():
    return tuple(lax.axis_index(a).astype(jnp.int32).reshape(1) for a in ("x", "y", "c"))


def pair_sum(p, r1, place, *, name):
    nsh, r, c = p.shape
    hr = r // 2
    tr = _row_tile(hr, c)
    nt = hr // tr

    def body(x_ref, y_ref, c_ref, p_ref, r_ref, o_ref):
        o_ref[...] = (p_ref[...].astype(F32) + r_ref[...].astype(F32)).astype(BF16)

    return pl.pallas_call(
        body, name=name,
        grid_spec=pltpu.PrefetchScalarGridSpec(
            num_scalar_prefetch=3, grid=(nsh, nt),
            in_specs=[pl.BlockSpec((None, tr, c), lambda s, i, x, y, cc: (s, cc[0] * nt + i, 0)),
                      pl.BlockSpec((None, tr, c), lambda s, i, x, y, cc: (s, i, 0))],
            out_specs=pl.BlockSpec((None, tr, c), lambda s, i, x, y, cc: (s, i, 0))),
        out_shape=jax.ShapeDtypeStruct((nsh, hr, c), BF16),
        compiler_params=_params(("parallel", "parallel")),
    )(*place, p, r1)


def cast_into_slot(w, li, place, *, name):
    _, R, C = w.shape
    tr = _row_tile(R, C)

    def body(x_ref, y_ref, c_ref, w_ref, o_ref):
        o_ref[...] = w_ref[...].astype(BF16)

    return pl.pallas_call(
        body, name=name,
        grid_spec=pltpu.PrefetchScalarGridSpec(
            num_scalar_prefetch=3, grid=(R // tr,),
            in_specs=[pl.BlockSpec((None, tr, C), lambda i, x, y, cc: (li, i, 0))],
            out_specs=pl.BlockSpec((None, tr, C), lambda i, x, y, cc: (2 * x[0] + y[0], i, 0))),
        out_shape=jax.ShapeDtypeStruct((N_CHIPS, R, C), BF16),
        compiler_params=_params(("parallel",)),
    )(*place, w)


def chip_sum(q, r2, place, gbuf, li, *, name):
    _, hr, c = q.shape
    tr = _row_tile(hr, c, 256 * 1024)
    nt = hr // tr

    def body(x_ref, y_ref, c_ref, q_ref, r_ref, g_in, o_ref):
        acc = q_ref[...].astype(F32)
        for k in range(r_ref.shape[0]):
            acc = acc + r_ref[k].astype(F32)
        o_ref[...] = acc

    return pl.pallas_call(
        body, name=name,
        grid_spec=pltpu.PrefetchScalarGridSpec(
            num_scalar_prefetch=3, grid=(nt,),
            in_specs=[pl.BlockSpec((None, tr, c), lambda i, x, y, cc: (2 * x[0] + y[0], i, 0)),
                      pl.BlockSpec((r2.shape[0], tr, c), lambda i, x, y, cc: (0, i, 0)),
                      pl.BlockSpec(memory_space=pl.ANY)],
            out_specs=pl.BlockSpec((None, tr, c), lambda i, x, y, cc: (li, cc[0] * nt + i, 0))),
        out_shape=jax.ShapeDtypeStruct(gbuf.shape, F32),
        input_output_aliases={5: 0},
        compiler_params=_params(("parallel",)),
    )(*place, q, r2, gbuf)


def adamw(w, g, m, v, *, name, emit_g=False):
    shape = w.shape
    wv, gv, mv, vv = (_rows_view(t) for t in (w, g, m, v))
    R, C = wv.shape
    tr = _row_tile(R, C, 256 * 1024)
    c1 = 1.0 / (1.0 - ADAM_B1 ** ADAM_STEP)
    c2 = 1.0 / (1.0 - ADAM_B2 ** ADAM_STEP)
    n_out = 4 if emit_g else 3

    def body(w_ref, g_ref, m_ref, v_ref, d_ref, nm_ref, nv_ref, *go_ref):
        g_ = g_ref[...]
        nm = ADAM_B1 * m_ref[...] + (1.0 - ADAM_B1) * g_
        nv = ADAM_B2 * v_ref[...] + (1.0 - ADAM_B2) * (g_ * g_)
        m_hat = nm * c1
        v_hat = nv * c2
        d_ref[...] = -ADAM_LR * (m_hat / (jnp.sqrt(v_hat) + ADAM_EPS) + ADAM_WD * w_ref[...])
        nm_ref[...] = nm
        nv_ref[...] = nv
        if emit_g:
            go_ref[0][...] = g_

    spec = pl.BlockSpec((tr, C), lambda i: (i, 0))
    o = jax.ShapeDtypeStruct((R, C), F32)
    outs = pl.pallas_call(body, name=name, grid=(R // tr,), in_specs=[spec] * 4, out_specs=(spec,) * n_out,
                          out_shape=(o,) * n_out, compiler_params=_params(("parallel",)))(wv, gv, mv, vv)
    return tuple(t.reshape(shape) for t in outs)


def _place():
    x, y, c = lax.axis_index("x"), lax.axis_index("y"), lax.axis_index("c")
    others = [(1 - x, y), (x, 1 - y), (1 - x, 1 - y)]
    return x, y, c, others


def _any_specs(n):
    return [pl.BlockSpec(memory_space=pl.ANY) for _ in range(n)]


HBM_SPEC = pl.BlockSpec(memory_space=pltpu.HBM)
SEM_SPEC = pl.BlockSpec(memory_space=pltpu.SEMAPHORE)
DATAFLOW = pltpu.SideEffectType.DATAFLOW_SIDE_EFFECTING
TOKEN = jax.ShapeDtypeStruct((8, LANES), F32)


def _in_hbm(arrays):
    return [pltpu.with_memory_space_constraint(a, pltpu.HBM) for a in arrays]


def _gather_copy(g, t, j, slot, px, py, c, send, recv):
    hr = g[t].shape[1] // 2
    rows = g[t].at[slot, pl.ds(c * hr, hr)]
    return pltpu.make_async_remote_copy(rows, rows, send.at[3 * t + j], recv.at[3 * t + j], device_id=(px, py, c), device_id_type=MESH)


def gather_start(gs, *, name):
    n = len(gs)

    def body(*refs):
        g, token = refs[:n], refs[-1]
        send, recv = refs[n], refs[n + 1]
        x, y, c, others = _place()
        for t in range(n):
            for j, (px, py) in enumerate(others):
                _gather_copy(g, t, j, 2 * x + y, px, py, c, send, recv).start()
        token[...] = jnp.zeros_like(token)

    outs = pl.pallas_call(
        body, name=name,
        in_specs=[HBM_SPEC] * n,
        out_specs=(SEM_SPEC, SEM_SPEC, *[HBM_SPEC] * n, pl.BlockSpec(memory_space=pltpu.VMEM)),
        out_shape=(pltpu.SemaphoreType.DMA((3 * n,)), pltpu.SemaphoreType.DMA((3 * n,)),
                   *[pltpu.HBM(g.shape, g.dtype) for g in gs], TOKEN),
        input_output_aliases={t: 2 + t for t in range(n)},
        compiler_params=pltpu.CompilerParams(has_side_effects=DATAFLOW),
    )(*_in_hbm(gs))
    return outs[0], outs[1], list(outs[2:2 + n]), outs[-1]


def gather_wait(send, recv, gs, after, *, name):
    n = len(gs)

    def body(*refs):
        g = refs[:n]
        send, recv = refs[n], refs[n + 1]
        x, y, c, others = _place()
        for t in range(n):
            for j, (px, py) in enumerate(others):
                _gather_copy(g, t, j, 2 * x + y, px, py, c, send, recv).wait_send()
                _gather_copy(g, t, j, 2 * px + py, px, py, c, send, recv).wait_recv()

    outs = pl.pallas_call(
        body, name=name,
        in_specs=[HBM_SPEC] * n + [SEM_SPEC, SEM_SPEC, pl.BlockSpec(memory_space=pl.ANY)],
        out_specs=tuple([HBM_SPEC] * n),
        out_shape=tuple(pltpu.HBM(g.shape, g.dtype) for g in gs),
        input_output_aliases={t: t for t in range(n)},
        compiler_params=pltpu.CompilerParams(has_side_effects=DATAFLOW),
    )(*gs, send, recv, after)
    return list(outs)


def gather_forward(gs, *, name):
    n = len(gs)

    def body(*refs):
        g = refs[n:2 * n]
        send, recv = refs[2 * n:]
        x, y, c, others = _place()
        cps = []
        for t in range(n):
            hr = g[t].shape[1] // 2
            for j, (px, py) in enumerate(others):
                rows = g[t].at[2 * px + py, pl.ds(c * hr, hr)]
                cp = pltpu.make_async_remote_copy(rows, rows, send.at[3 * t + j], recv.at[3 * t + j],
                                                  device_id=(x, y, 1 - c), device_id_type=MESH)
                cp.start()
                cps.append(cp)
        for t in range(n):
            hr = g[t].shape[1] // 2
            for j, (px, py) in enumerate(others):
                rows = g[t].at[2 * px + py, pl.ds((1 - c) * hr, hr)]
                pltpu.make_async_remote_copy(rows, rows, send.at[3 * t + j], recv.at[3 * t + j],
                                             device_id=(x, y, 1 - c), device_id_type=MESH).wait_recv()
        for cp in cps:
            cp.wait_send()

    return pl.pallas_call(
        body, name=name,
        in_specs=_any_specs(n), out_specs=_any_specs(n),
        out_shape=[jax.ShapeDtypeStruct(g.shape, g.dtype) for g in gs],
        input_output_aliases={t: t for t in range(n)},
        scratch_shapes=[pltpu.SemaphoreType.DMA((3 * n,)), pltpu.SemaphoreType.DMA((3 * n,))],
        compiler_params=pltpu.CompilerParams(has_side_effects=True),
    )(*gs)


def sibling_halves(parts, *, name):
    n = len(parts)

    def body(*refs):
        src, dst = refs[:n], refs[n:2 * n]
        send, recv = refs[2 * n:]
        x, y, c, _ = _place()
        cps = []
        for t in range(n):
            hr = src[t].shape[1] // 2
            cp = pltpu.make_async_remote_copy(src[t].at[:, pl.ds((1 - c) * hr, hr)], dst[t], send.at[t], recv.at[t],
                                              device_id=(x, y, 1 - c), device_id_type=MESH)
            cp.start()
            cps.append(cp)
        for cp in cps:
            cp.wait()

    return pl.pallas_call(
        body, name=name,
        in_specs=_any_specs(n), out_specs=_any_specs(n),
        out_shape=[jax.ShapeDtypeStruct((p.shape[0], p.shape[1] // 2, p.shape[2]), p.dtype) for p in parts],
        scratch_shapes=[pltpu.SemaphoreType.DMA((n,)), pltpu.SemaphoreType.DMA((n,))],
        compiler_params=pltpu.CompilerParams(has_side_effects=True),
    )(*parts)


def _chips_copy(q, land, t, j, px, py, c, send, recv):
    return pltpu.make_async_remote_copy(q[t].at[2 * px + py], land[t].at[j], send.at[3 * t + j], recv.at[3 * t + j],
                                        device_id=(px, py, c), device_id_type=MESH)


def chips_start(qs, *, name):
    n = len(qs)

    def body(*refs):
        q, land, token = refs[:n], refs[n:2 * n], refs[-1]
        send, recv = refs[2 * n], refs[2 * n + 1]
        x, y, c, others = _place()
        for t in range(n):
            for j, (px, py) in enumerate(others):
                _chips_copy(q, land, t, j, px, py, c, send, recv).start()
        token[...] = jnp.zeros_like(token)

    lands = [lax.empty((3,) + q.shape[1:], q.dtype) for q in qs]
    outs = pl.pallas_call(
        body, name=name,
        in_specs=[HBM_SPEC] * (2 * n),
        out_specs=(SEM_SPEC, SEM_SPEC, *[HBM_SPEC] * (2 * n), pl.BlockSpec(memory_space=pltpu.VMEM)),
        out_shape=(pltpu.SemaphoreType.DMA((3 * n,)), pltpu.SemaphoreType.DMA((3 * n,)),
                   *[pltpu.HBM(a.shape, a.dtype) for a in qs + lands], TOKEN),
        input_output_aliases={t: 2 + t for t in range(2 * n)},
        compiler_params=pltpu.CompilerParams(has_side_effects=DATAFLOW),
    )(*_in_hbm(qs + lands))
    return outs[0], outs[1], list(outs[2:2 + n]), list(outs[2 + n:2 + 2 * n]), outs[-1]


def chips_wait(send, recv, qs, lands, after, *, name):
    n = len(qs)

    def body(*refs):
        q, land = refs[:n], refs[n:2 * n]
        send, recv = refs[2 * n], refs[2 * n + 1]
        x, y, c, others = _place()
        for t in range(n):
            for j, (px, py) in enumerate(others):
                cp = _chips_copy(q, land, t, j, px, py, c, send, recv)
                cp.wait_send()
                cp.wait_recv()

    outs = pl.pallas_call(
        body, name=name,
        in_specs=[HBM_SPEC] * (2 * n) + [SEM_SPEC, SEM_SPEC, pl.BlockSpec(memory_space=pl.ANY)],
        out_specs=tuple([HBM_SPEC] * (2 * n)),
        out_shape=tuple(pltpu.HBM(a.shape, a.dtype) for a in qs + lands),
        input_output_aliases={t: t for t in range(2 * n)},
        compiler_params=pltpu.CompilerParams(has_side_effects=DATAFLOW),
    )(*qs, *lands, send, recv, after)
    return list(outs[:n]), list(outs[n:])


def join_halves(fulls, li, *, name):
    n = len(fulls)

    def body(*refs):
        g = refs[n:2 * n]
        send, recv = refs[2 * n:]
        x, y, c, _ = _place()
        cps = []
        for t in range(n):
            hr = g[t].shape[1] // 2
            rows = g[t].at[li, pl.ds(c * hr, hr)]
            cp = pltpu.make_async_remote_copy(rows, rows, send.at[t], recv.at[t], device_id=(x, y, 1 - c), device_id_type=MESH)
            cp.start()
            cps.append(cp)
        for t in range(n):
            hr = g[t].shape[1] // 2
            rows = g[t].at[li, pl.ds((1 - c) * hr, hr)]
            pltpu.make_async_remote_copy(rows, rows, send.at[t], recv.at[t],
                                         device_id=(x, y, 1 - c), device_id_type=MESH).wait_recv()
        for cp in cps:
            cp.wait_send()

    return pl.pallas_call(
        body, name=name,
        in_specs=_any_specs(n), out_specs=_any_specs(n),
        out_shape=[jax.ShapeDtypeStruct(g.shape, g.dtype) for g in fulls],
        input_output_aliases={t: t for t in range(n)},
        scratch_shapes=[pltpu.SemaphoreType.DMA((n,)), pltpu.SemaphoreType.DMA((n,))],
        compiler_params=pltpu.CompilerParams(has_side_effects=True),
    )(*fulls)


def allreduce_small(t, *, name):
    R, C = t.shape

    def body(t_ref, o_ref, land, send, recv):
        x, y, c, _ = _place()
        me = 4 * x + 2 * y + c
        land[me] = t_ref[...]
        cps = []
        for j in range(1, 8):
            px, py, pc = (x + (j >> 2)) % 2, (y + ((j >> 1) & 1)) % 2, (c + (j & 1)) % 2
            cp = pltpu.make_async_remote_copy(t_ref, land.at[me], send.at[j - 1], recv.at[j - 1],
                                              device_id=(px, py, pc), device_id_type=MESH)
            cp.start()
            cps.append(cp)
        for j in range(1, 8):
            px, py, pc = (x + (j >> 2)) % 2, (y + ((j >> 1) & 1)) % 2, (c + (j & 1)) % 2
            pltpu.make_async_remote_copy(t_ref, land.at[4 * px + 2 * py + pc], send.at[j - 1], recv.at[j - 1],
                                         device_id=(px, py, pc), device_id_type=MESH).wait_recv()
        for cp in cps:
            cp.wait_send()
        acc = land[0]
        for k in range(1, 8):
            acc = acc + land[k]
        o_ref[...] = acc

    return pl.pallas_call(
        body, name=name,
        in_specs=[pl.BlockSpec(memory_space=pltpu.VMEM)], out_specs=pl.BlockSpec(memory_space=pltpu.VMEM),
        out_shape=jax.ShapeDtypeStruct((R, C), F32),
        scratch_shapes=[pltpu.VMEM((8, R, C), F32), pltpu.SemaphoreType.DMA((7,)), pltpu.SemaphoreType.DMA((7,))],
        compiler_params=pltpu.CompilerParams(has_side_effects=True),
    )(t)


def _layer_fwd(x, xb, memb, w_in, rest, P, tabs, alpha, li):
    ca, sa, cb, sb = tabs
    nA = P["gn_a"].shape[1] // HEAD_DIM_A
    nQ = P["gn_b"].shape[1] // LANES
    nm = lambda s: f"L{li}_{s}"
    h = mm_nn(xb, w_in, name=nm("h"), out_dtype=F32, tn=2304)
    ya, lse_a = attn_a_fwd(h, ca, sa, n_heads=nA, name=nm("attn_a"))
    yb, lse_b = attn_b_fwd(h, cb, sb, P["sinks_l"], off_q=3 * nA, n_qtiles=nQ, name=nm("attn_b"))
    ymix = rms_fwd(ya, yb, P["gn_a"], P["gn_b"], name=nm("rms"))
    W, P = rest(ymix, P)
    z1, x1, x1b = mm_ln(ymix, W["w_out"][0], x, P["ln_mix_g"], P["ln_mix_b"], name=nm("out_ln"), alpha=alpha)
    qm = mm_nn(x1b, W["w_mq"], name=nm("mq"), out_dtype=BF16)
    kv = mm_nn(memb, W["w_mkv"], name=nm("mkv"), out_dtype=BF16, tm=256)
    o = mem_attn_fwd(qm, kv, name=nm("mem_attn"))
    z2, x2, x2b = mm_ln(o, W["w_mo"][0], x1, P["ln_mem_g"], P["ln_mem_b"], name=nm("mo_ln"), alpha=alpha)
    u, a = mm_nn(x2b, W["w_up"], name=nm("up"), out_dtype=BF16, relu2=True)
    z3, x3, x3b = mm_ln(a, W["w_down"][0], x2, P["ln_ff_g"], P["ln_ff_b"], name=nm("down_ln"), alpha=alpha)
    saved = dict(xb=xb, h=h, ya=ya, lse_a=lse_a, yb=yb, lse_b=lse_b, ymix=ymix, z1=z1, x1b=x1b, qm=qm, kv=kv, o=o,
                 z2=z2, x2b=x2b, u=u, a=a, z3=z3)
    return x3, x3b, saved


def _layer_bwd(dx3, sv, memb, W, P, tabs, alpha, li, hook=None):
    ca, sa, cb, sb = tabs
    nA = P["gn_a"].shape[1] // HEAD_DIM_A
    nQ = P["gn_b"].shape[1] // LANES
    nm = lambda s: f"L{li}_b_{s}"
    nsh = lambda k: W[k].shape[0]
    gw, gs = {}, {}
    dz3, dz3b, gs["ln_ff_g"], gs["ln_ff_b"] = ln_bwd(dx3, sv["z3"], P["ln_ff_g"], name=nm("ln_ff"))
    gw["w_down"] = mm_tn(sv["a"], dz3b, nsh("w_down"), name=nm("dw_down"))
    du = mm_nt(dz3b, W["w_down"], name=nm("du"), out_dtype=BF16, umul=sv["u"])
    gw["w_up"] = mm_tn(sv["x2b"], du, nsh("w_up"), name=nm("dw_up"))
    dx2 = mm_nt(du, W["w_up"], name=nm("dx2"), out_dtype=F32, resid=dz3, alpha=alpha)
    dz2, dz2b, gs["ln_mem_g"], gs["ln_mem_b"] = ln_bwd(dx2, sv["z2"], P["ln_mem_g"], name=nm("ln_mem"))
    gw["w_mo"] = mm_tn(sv["o"], dz2b, nsh("w_mo"), name=nm("dw_mo"))
    do = mm_nt(dz2b, W["w_mo"], name=nm("do"), out_dtype=BF16)
    dqm, dkv = mem_attn_bwd(sv["qm"], sv["kv"], do, name=nm("mem_attn"))
    gw["w_mq"] = mm_tn(sv["x1b"], dqm, nsh("w_mq"), name=nm("dw_mq"))
    gw["w_mkv"] = mm_tn(memb, cast_bf16(dkv, name=nm("dkv_cast")), nsh("w_mkv"), name=nm("dw_mkv"), tm=256)
    dx1 = mm_nt(dqm, W["w_mq"], name=nm("dx1"), out_dtype=F32, resid=dz2, alpha=alpha)
    if hook is not None:
        P = hook(gw, dx1, P)
    dz1, dz1b, gs["ln_mix_g"], gs["ln_mix_b"] = ln_bwd(dx1, sv["z1"], P["ln_mix_g"], name=nm("ln_mix"))
    gw["w_out"] = mm_tn(sv["ymix"], dz1b, nsh("w_out"), name=nm("dw_out"))
    dymix = mm_nt(dz1b, W["w_out"], name=nm("dymix"), out_dtype=F32)
    dya, dyb, gs["gn_a"], gs["gn_b"] = rms_bwd(dymix, sv["ya"], sv["yb"], P["gn_a"], P["gn_b"], name=nm("rms"))
    dh = attn_a_bwd(sv["h"], ca, sa, sv["ya"], sv["lse_a"], dya, n_heads=nA, name=nm("attn_a"))
    dh, gs["sinks"] = attn_b_bwd(sv["h"], cb, sb, P["sinks_l"], sv["yb"], sv["lse_b"], dyb, dh,
                                 off_q=3 * nA, n_qtiles=nQ, name=nm("attn_b"))
    gw["w_in"] = mm_tn(sv["xb"], dh, nsh("w_in"), name=nm("dw_in"), tn=2304)
    dx0 = mm_nt(dh, W["w_in"], name=nm("dx0"), out_dtype=F32, resid=dz1, alpha=alpha, tr=2304)
    return dx0, gw, gs


def _gathered_view(name, g):
    if name == "w_in":
        return jnp.concatenate([g[k] for k in range(N_CHIPS)], axis=1)[None]
    if name in COL_SHARDED:
        return g
    return g.reshape(1, g.shape[0] * g.shape[1], g.shape[2])


def _to_shards(name, gw):
    if name == "w_in":
        n = gw.shape[2] // N_CHIPS
        return jnp.stack([gw[0, :, k * n:(k + 1) * n] for k in range(N_CHIPS)])
    if name in COL_SHARDED:
        return gw
    return gw.reshape(N_CHIPS, gw.shape[1] // N_CHIPS, gw.shape[2])


def _step(x, mem, positions, loss_target, w, m, v):
    S, D = x.shape[1], x.shape[2]
    depth = w["w_in"].shape[0]
    alpha = (2 * depth) ** 0.25
    x0 = x[0]
    memb = cast_bf16(mem[0], name="mem_cast")
    pos = positions[0]
    tabs = rope_tables(pos, HEAD_DIM_A // 4, HEAD_DIM_A) + rope_tables(pos, HEAD_DIM_B // 4, HEAD_DIM_B)
    place = mesh_place()

    def small(li):
        P = {k: w[k][li][None] for k in ("gn_a", "gn_b", "ln_mix_g", "ln_mix_b", "ln_mem_g", "ln_mem_b", "ln_ff_g", "ln_ff_b")}
        P["sinks_l"] = jnp.broadcast_to(w["sinks"][li][:, None], (w["sinks"].shape[1], LANES))
        return P

    rest_names = tuple(k for k in BIG if k != "w_in")
    chain = [(0, ("w_in",)), (0, rest_names)] + [(li, BIG) for li in range(1, depth)]
    casts = [[cast_into_slot(w[k], li, place, name=f"L{li}_cast_{k}") for k in names] for li, names in chain]
    started = {0: gather_start(casts[0], name="G0_gather_start")}

    def land(gi, after):
        send, recv, gs, tok0 = started.pop(gi)
        gs = gather_wait(send, recv, gs, tok0 if after is None else after, name=f"G{gi}_gather_wait")
        token = None
        if gi + 1 < len(chain):
            started[gi + 1] = gather_start(casts[gi + 1], name=f"G{gi + 1}_gather_start")
            token = started[gi + 1][3]
        gs = gather_forward(gs, name=f"G{gi}_gather_fwd")
        return dict(zip(chain[gi][1], gs)), token

    def ordered(a, token):
        return a if token is None else a + token[:1, :1].astype(a.dtype)

    xs, xbs, saved, Ws = x0, cast_bf16(x0, name="x_cast"), [], []
    for li in range(depth):
        gi = 0 if li == 0 else li + 1
        got, token = land(gi, None if li == 0 else xs)
        W = {"w_in": ordered(_gathered_view("w_in", got["w_in"]), token)}

        def rest(after, P, li=li, got=got, W=W):
            if li == 0:
                got, token = land(1, after)
                P = dict(P, ln_mix_g=ordered(P["ln_mix_g"], token))
            W.update({k: _gathered_view(k, got[k]) for k in rest_names})
            return W, P

        xs, xbs, sv = _layer_fwd(xs, xbs, memb, W["w_in"], rest, small(li), tabs, alpha, li)
        saved.append(sv)
        Ws.append(W)
    dy, loss_part = loss_head(xs, loss_target[0], name="loss")
    loss = lax.psum(0.5 / D * jnp.sum(loss_part), ("x", "y", "c"))

    g_big = {k: lax.empty(w[k].shape, F32) for k in BIG}
    g_small = [None] * depth

    def begin(li, names, gw, tag):
        parts = [_to_shards(k, gw[k]) for k in names]
        r1 = sibling_halves(parts, name=f"L{li}{tag}_rs_sibling")
        qs = [pair_sum(p, r, place, name=f"L{li}_rs_pair_{k}") for k, p, r in zip(names, parts, r1)]
        send, recv, qs, lands, token = chips_start(qs, name=f"L{li}{tag}_rs_chips_start")
        return (li, names, tag, send, recv, qs, lands), token

    def finish(pending, after):
        li, names, tag, send, recv, qs, lands = pending
        qs, lands = chips_wait(send, recv, qs, lands, after, name=f"L{li}{tag}_rs_chips_wait")
        fulls = [chip_sum(q, r, place, g_big[k], li, name=f"L{li}_rs_sum_{k}") for k, q, r in zip(names, qs, lands)]
        for k, f in zip(names, join_halves(fulls, li, name=f"L{li}{tag}_rs_join")):
            g_big[k] = f

    early = ("w_mq", "w_mkv", "w_mo", "w_up", "w_down")
    late = tuple(k for k in BIG if k not in early)
    pendings, token = [], None
    for li in reversed(range(depth)):
        P = small(li)
        P["ln_ff_g"] = ordered(P["ln_ff_g"], token)
        hook = None
        if li == 0:
            def hook(gw, dx1, P):
                while pendings:
                    finish(pendings.pop(), dx1)
                pend, tok = begin(0, early, gw, "a")
                pendings.append(pend)
                return dict(P, ln_mix_g=ordered(P["ln_mix_g"], tok))
        dy, gw, g_small[li] = _layer_bwd(dy, saved[li], memb, Ws[li], P, tabs, alpha, li, hook)
        while pendings:
            finish(pendings.pop(), dy)
        pend, token = begin(li, late if li == 0 else BIG, gw, "b" if li == 0 else "")
        pendings.append(pend)
    finish(pendings.pop(), token)
    grad_x = dy[None]

    rows = []
    for li in range(depth):
        gs = g_small[li]
        for k in ("ln_mix_g", "ln_mix_b", "ln_mem_g", "ln_mem_b", "ln_ff_g", "ln_ff_b"):
            rows.append(jnp.sum(gs[k], axis=0, keepdims=True))
        rows.append(jnp.concatenate([jnp.sum(gs["gn_a"], axis=0, keepdims=True), jnp.sum(gs["gn_b"], axis=0, keepdims=True)], axis=1))
        sk = gs["sinks"][:, 0][None]
        rows.append(jnp.pad(sk, ((0, 0), (0, D - sk.shape[1]))))
    red = allreduce_small(jnp.concatenate(rows, axis=0), name="small_allreduce").reshape(depth, 8, D)
    wa = w["gn_a"].shape[1]
    grads = dict(g_big)
    for j, k in enumerate(("ln_mix_g", "ln_mix_b", "ln_mem_g", "ln_mem_b", "ln_ff_g", "ln_ff_b")):
        grads[k] = red[:, j]
    grads["gn_a"] = red[:, 6, :wa]
    grads["gn_b"] = red[:, 6, wa:]
    grads["sinks"] = red[:, 7, :w["sinks"].shape[1]]

    delta, new_m, new_v = {}, {}, {}
    small_names = [k for k in w if k not in BIG]
    for k in BIG:
        delta[k], new_m[k], new_v[k], grads[k] = adamw(w[k], grads[k], m[k], v[k], name=f"adamw_{k}", emit_g=True)
    pack = lambda d: jnp.concatenate([jnp.pad(d[k], ((0, 0), (0, D - d[k].shape[1]))) for k in small_names], axis=0)
    ds, ms, vs = adamw(pack(w), pack(grads), pack(m), pack(v), name="adamw_small")
    for j, k in enumerate(small_names):
        sl = (slice(j * depth, (j + 1) * depth), slice(0, w[k].shape[1]))
        delta[k], new_m[k], new_v[k] = ds[sl], ms[sl], vs[sl]
    return loss, grad_x, grads, delta, new_m, new_v


WEIGHTS = ("w_in", "gn_a", "gn_b", "sinks", "w_out", "ln_mix_g", "ln_mix_b", "w_mq", "w_mkv", "w_mo",
           "ln_mem_g", "ln_mem_b", "w_up", "w_down", "ln_ff_g", "ln_ff_b")


def kernel(x, mem, positions, w_in, gn_a, gn_b, sinks, w_out, ln_mix_g, ln_mix_b, w_mq, w_mkv, w_mo, ln_mem_g, ln_mem_b, w_up, w_down, ln_ff_g, ln_ff_b, loss_target, m_w_in, m_gn_a, m_gn_b, m_sinks, m_w_out, m_ln_mix_g, m_ln_mix_b, m_w_mq, m_w_mkv, m_w_mo, m_ln_mem_g, m_ln_mem_b, m_w_up, m_w_down, m_ln_ff_g, m_ln_ff_b, v_w_in, v_gn_a, v_gn_b, v_sinks, v_w_out, v_ln_mix_g, v_ln_mix_b, v_w_mq, v_w_mkv, v_w_mo, v_ln_mem_g, v_ln_mem_b, v_w_up, v_w_down, v_ln_ff_g, v_ln_ff_b):
    w = dict(zip(WEIGHTS, (w_in, gn_a, gn_b, sinks, w_out, ln_mix_g, ln_mix_b, w_mq, w_mkv, w_mo, ln_mem_g, ln_mem_b, w_up, w_down, ln_ff_g, ln_ff_b)))
    m = dict(zip(WEIGHTS, (m_w_in, m_gn_a, m_gn_b, m_sinks, m_w_out, m_ln_mix_g, m_ln_mix_b, m_w_mq, m_w_mkv, m_w_mo, m_ln_mem_g, m_ln_mem_b, m_w_up, m_w_down, m_ln_ff_g, m_ln_ff_b)))
    v = dict(zip(WEIGHTS, (v_w_in, v_gn_a, v_gn_b, v_sinks, v_w_out, v_ln_mix_g, v_ln_mix_b, v_w_mq, v_w_mkv, v_w_mo, v_ln_mem_g, v_ln_mem_b, v_w_up, v_w_down, v_ln_ff_g, v_ln_ff_b)))
    loss, grad_x, grads, delta, new_m, new_v = _step(x, mem, positions, loss_target, w, m, v)
    return (loss, grad_x, *[grads[k] for k in WEIGHTS], *[delta[k] for k in WEIGHTS],
            *[new_m[k] for k in WEIGHTS], *[new_v[k] for k in WEIGHTS])
```

```python
import functools

import jax
import jax.numpy as jnp
from jax import lax
from jax.experimental import pallas as pl
from jax.experimental.pallas import tpu as pltpu

F32 = jnp.float32
BF16 = jnp.bfloat16
MESH = pl.DeviceIdType.MESH

HEAD_DIM_A = 128
HEAD_DIM_B = 64
LANES = 128
BLOCK = 128
DILATED_BRANCHES = ((128, 1), (512, 4), (2048, 16))
WINDOW_B = 128
N_MEM_HEADS = 4
ROPE_THETA = 500000.0
LN_EPS = 1e-5
RMS_EPS = 1e-6
NEG_INF = -1e30
ADAM_LR = 0.001
ADAM_B1 = 0.9
ADAM_B2 = 0.999
ADAM_EPS = 1e-08
ADAM_WD = 0.01
ADAM_STEP = 10
N_CHIPS = 4
VMEM_LIMIT = 56 * 1024 * 1024

BIG = ("w_in", "w_out", "w_mq", "w_mkv", "w_mo", "w_up", "w_down")
COL_SHARDED = ("w_in", "w_mkv", "w_up")


def _tile(n, target, mult=LANES):
    best = None
    t = mult
    while t <= min(n, target):
        if n % t == 0:
            best = t
        t += mult
    return best if best is not None else n


def _params(sem=None):
    return pltpu.CompilerParams(dimension_semantics=sem, vmem_limit_bytes=VMEM_LIMIT)


def _dot(a, b):
    return jnp.dot(a, b, preferred_element_type=F32)


def _dot_nt(a, b):
    return lax.dot_general(a, b, (((1,), (1,)), ((), ())), preferred_element_type=F32)


def _dot_tn(a, b):
    return lax.dot_general(a, b, (((0,), (0,)), ((), ())), preferred_element_type=F32)


def mm_nn(a, b3, *, name, out_dtype, relu2=False, tm=1024, tn=1024, tk=2048):
    M, K = a.shape
    nsh, _, nk = b3.shape
    tm, tn, tk = _tile(M, tm, 8), _tile(nk, tn), _tile(K, tk)
    nb, ksteps = nk // tn, K // tk

    def body(a_ref, b_ref, *rest):
        outs, scr = rest[:2 if relu2 else 1], rest[2 if relu2 else 1:]

        def finish(acc):
            if relu2:
                outs[0][...] = acc.astype(outs[0].dtype)
                r = jnp.maximum(acc, 0.0)
                outs[1][...] = (r * r).astype(outs[1].dtype)
            else:
                outs[0][...] = acc.astype(outs[0].dtype)

        if ksteps == 1:
            finish(_dot(a_ref[...], b_ref[...]))
        else:
            acc_ref = scr[0]
            k = pl.program_id(2)

            @pl.when(k == 0)
            def _():
                acc_ref[...] = jnp.zeros_like(acc_ref)

            acc_ref[...] += _dot(a_ref[...], b_ref[...])

            @pl.when(k == ksteps - 1)
            def _():
                finish(acc_ref[...])

    o_spec = pl.BlockSpec((tm, tn), lambda i, j, k: (i, j))
    o_shape = jax.ShapeDtypeStruct((M, nsh * nk), out_dtype)
    return pl.pallas_call(
        body, name=name,
        grid=(M // tm, nsh * nb, ksteps),
        in_specs=[pl.BlockSpec((tm, tk), lambda i, j, k: (i, k)),
                  pl.BlockSpec((None, tk, tn), lambda i, j, k: (j // nb, k, j % nb))],
        out_specs=(o_spec, o_spec) if relu2 else o_spec,
        out_shape=(o_shape, o_shape) if relu2 else o_shape,
        scratch_shapes=[] if ksteps == 1 else [pltpu.VMEM((tm, tn), F32)],
        compiler_params=_params(("parallel", "parallel", "arbitrary")),
    )(a, b3)


def mm_ln(a, w, resid, g, b, *, name, alpha, tm=512, tk=1024):
    M, K = a.shape
    D = w.shape[1]
    tm, tk = _tile(M, tm, 8), _tile(K, tk)
    ksteps = K // tk

    def body(a_ref, w_ref, r_ref, g_ref, b_ref, z_ref, xn_ref, xb_ref, *scr):
        def finish(acc):
            z = alpha * r_ref[...] + acc
            mu = jnp.mean(z, axis=-1, keepdims=True)
            zc = z - mu
            var = jnp.mean(zc * zc, axis=-1, keepdims=True)
            xn = zc * lax.rsqrt(var + LN_EPS) * g_ref[...] + b_ref[...]
            z_ref[...] = z
            xn_ref[...] = xn
            xb_ref[...] = xn.astype(BF16)

        if ksteps == 1:
            finish(_dot(a_ref[...], w_ref[...]))
            return
        acc_ref = scr[0]
        k = pl.program_id(1)

        @pl.when(k == 0)
        def _():
            acc_ref[...] = jnp.zeros_like(acc_ref)

        acc_ref[...] += _dot(a_ref[...], w_ref[...])

        @pl.when(k == ksteps - 1)
        def _():
            finish(acc_ref[...])

    row = pl.BlockSpec((tm, D), lambda i, k: (i, 0))
    vec = pl.BlockSpec((1, D), lambda i, k: (0, 0))
    return pl.pallas_call(
        body, name=name,
        grid=(M // tm, ksteps),
        in_specs=[pl.BlockSpec((tm, tk), lambda i, k: (i, k)),
                  pl.BlockSpec((tk, D), lambda i, k: (k, 0)), row, vec, vec],
        out_specs=(row, row, row),
        out_shape=(jax.ShapeDtypeStruct((M, D), F32), jax.ShapeDtypeStruct((M, D), F32),
                   jax.ShapeDtypeStruct((M, D), BF16)),
        scratch_shapes=[] if ksteps == 1 else [pltpu.VMEM((tm, D), F32)],
        compiler_params=_params(("parallel", "arbitrary")),
    )(a, w, resid, g, b)


def mm_nt(a, b3, *, name, out_dtype, resid=None, alpha=1.0, umul=None, tm=1024, tko=1024, tr=2048):
    M, N = a.shape
    nsh, K, nk = b3.shape
    tm, tko, tr = _tile(M, tm, 8), _tile(K, tko), _tile(nk, tr)
    nb = nk // tr
    rsteps = nsh * nb

    def body(a_ref, b_ref, *rest):
        rest = list(rest)
        r_ref = rest.pop(0) if resid is not None else None
        u_ref = rest.pop(0) if umul is not None else None
        o_ref = rest.pop(0)

        def finish(acc):
            if r_ref is not None:
                acc = acc + alpha * r_ref[...]
            if u_ref is not None:
                acc = acc * (2.0 * jnp.maximum(u_ref[...].astype(F32), 0.0))
            o_ref[...] = acc.astype(o_ref.dtype)

        if rsteps == 1:
            finish(_dot_nt(a_ref[...], b_ref[...]))
        else:
            acc_ref = rest[0]
            r = pl.program_id(2)

            @pl.when(r == 0)
            def _():
                acc_ref[...] = jnp.zeros_like(acc_ref)

            acc_ref[...] += _dot_nt(a_ref[...], b_ref[...])

            @pl.when(r == rsteps - 1)
            def _():
                finish(acc_ref[...])

    o_spec = pl.BlockSpec((tm, tko), lambda i, j, r: (i, j))
    in_specs = [pl.BlockSpec((tm, tr), lambda i, j, r: (i, r)),
                pl.BlockSpec((None, tko, tr), lambda i, j, r: (r // nb, j, r % nb))]
    args = [a, b3]
    for extra in (resid, umul):
        if extra is not None:
            in_specs.append(o_spec)
            args.append(extra)
    return pl.pallas_call(
        body, name=name,
        grid=(M // tm, K // tko, rsteps),
        in_specs=in_specs, out_specs=o_spec,
        out_shape=jax.ShapeDtypeStruct((M, K), out_dtype),
        scratch_shapes=[] if rsteps == 1 else [pltpu.VMEM((tm, tko), F32)],
        compiler_params=_params(("parallel", "parallel", "arbitrary")),
    )(*args)


def mm_tn(a, g, nsh, *, name, tk=1024, tn=1024, tm=2048):
    M, K = a.shape
    N = g.shape[1]
    nk = N // nsh
    tk, tn, tm = _tile(K, tk), _tile(nk, tn), _tile(M, tm, 8)
    nb, msteps = nk // tn, M // tm

    def body(a_ref, g_ref, o_ref, acc_ref):
        m = pl.program_id(2)

        @pl.when(m == 0)
        def _():
            acc_ref[...] = jnp.zeros_like(acc_ref)

        acc_ref[...] += _dot_tn(a_ref[...], g_ref[...])

        @pl.when(m == msteps - 1)
        def _():
            o_ref[...] = acc_ref[...].astype(o_ref.dtype)

    return pl.pallas_call(
        body, name=name,
        grid=(K // tk, nsh * nb, msteps),
        in_specs=[pl.BlockSpec((tm, tk), lambda i, j, m: (m, i)),
                  pl.BlockSpec((tm, tn), lambda i, j, m: (m, j))],
        out_specs=pl.BlockSpec((None, tk, tn), lambda i, j, m: (j // nb, i, j % nb)),
        out_shape=jax.ShapeDtypeStruct((nsh, K, nk), BF16),
        scratch_shapes=[pltpu.VMEM((tk, tn), F32)],
        compiler_params=_params(("parallel", "parallel", "arbitrary")),
    )(a, g)


def _fold8(t):
    return t.reshape(t.shape[0] // 8, 8, t.shape[1]).sum(axis=0)


def ln_bwd(dy, z, g, *, name, tm=256):
    M, D = z.shape
    tm = _tile(M, tm, 8)

    def body(dy_ref, z_ref, g_ref, dz_ref, dzb_ref, dg_ref, db_ref):
        i = pl.program_id(0)
        z_ = z_ref[...]
        dy_ = dy_ref[...]
        mu = jnp.mean(z_, axis=-1, keepdims=True)
        zc = z_ - mu
        var = jnp.mean(zc * zc, axis=-1, keepdims=True)
        rstd = lax.rsqrt(var + LN_EPS)
        xh = zc * rstd
        dxh = dy_ * g_ref[...]
        m1 = jnp.mean(dxh, axis=-1, keepdims=True)
        m2 = jnp.mean(dxh * xh, axis=-1, keepdims=True)
        dz = rstd * (dxh - m1 - xh * m2)
        dz_ref[...] = dz
        dzb_ref[...] = dz.astype(BF16)

        @pl.when(i == 0)
        def _():
            dg_ref[...] = jnp.zeros_like(dg_ref)
            db_ref[...] = jnp.zeros_like(db_ref)

        dg_ref[...] += _fold8(dy_ * xh)
        db_ref[...] += _fold8(dy_)

    row = pl.BlockSpec((tm, D), lambda i: (i, 0))
    acc = pl.BlockSpec((8, D), lambda i: (0, 0))
    return pl.pallas_call(
        body, name=name, grid=(M // tm,),
        in_specs=[row, row, pl.BlockSpec((1, D), lambda i: (0, 0))],
        out_specs=(row, row, acc, acc),
        out_shape=(jax.ShapeDtypeStruct((M, D), F32), jax.ShapeDtypeStruct((M, D), BF16),
                   jax.ShapeDtypeStruct((8, D), F32), jax.ShapeDtypeStruct((8, D), F32)),
        compiler_params=_params(("arbitrary",)),
    )(dy, z, g)


def rms_fwd(ya, yb, ga, gb, *, name, tm=512):
    M, WA = ya.shape
    WB = yb.shape[1]
    tm = _tile(M, tm, 8)

    def body(ya_ref, yb_ref, ga_ref, gb_ref, o_ref):
        for y_ref, g_ref, lo, w in ((ya_ref, ga_ref, 0, WA), (yb_ref, gb_ref, WA, WB)):
            y = y_ref[...]
            r = lax.rsqrt(jnp.mean(y * y, axis=-1, keepdims=True) + RMS_EPS)
            o_ref[:, lo:lo + w] = (y * r * g_ref[...]).astype(o_ref.dtype)

    return pl.pallas_call(
        body, name=name, grid=(M // tm,),
        in_specs=[pl.BlockSpec((tm, WA), lambda i: (i, 0)), pl.BlockSpec((tm, WB), lambda i: (i, 0)),
                  pl.BlockSpec((1, WA), lambda i: (0, 0)), pl.BlockSpec((1, WB), lambda i: (0, 0))],
        out_specs=pl.BlockSpec((tm, WA + WB), lambda i: (i, 0)),
        out_shape=jax.ShapeDtypeStruct((M, WA + WB), BF16),
        compiler_params=_params(("parallel",)),
    )(ya, yb, ga, gb)


def rms_bwd(dy, ya, yb, ga, gb, *, name, tm=512):
    M, WA = ya.shape
    WB = yb.shape[1]
    tm = _tile(M, tm, 8)

    def body(dy_ref, ya_ref, yb_ref, ga_ref, gb_ref, dya_ref, dyb_ref, dga_ref, dgb_ref):
        i = pl.program_id(0)

        @pl.when(i == 0)
        def _():
            dga_ref[...] = jnp.zeros_like(dga_ref)
            dgb_ref[...] = jnp.zeros_like(dgb_ref)

        for y_ref, g_ref, d_ref, dgr, lo, w in ((ya_ref, ga_ref, dya_ref, dga_ref, 0, WA),
                                                (yb_ref, gb_ref, dyb_ref, dgb_ref, WA, WB)):
            y = y_ref[...]
            d = dy_ref[:, lo:lo + w]
            r = lax.rsqrt(jnp.mean(y * y, axis=-1, keepdims=True) + RMS_EPS)
            n = y * r
            dn = d * g_ref[...]
            d_ref[...] = r * (dn - n * jnp.mean(dn * n, axis=-1, keepdims=True))
            dgr[...] += _fold8(d * n)

    return pl.pallas_call(
        body, name=name, grid=(M // tm,),
        in_specs=[pl.BlockSpec((tm, WA + WB), lambda i: (i, 0)),
                  pl.BlockSpec((tm, WA), lambda i: (i, 0)), pl.BlockSpec((tm, WB), lambda i: (i, 0)),
                  pl.BlockSpec((1, WA), lambda i: (0, 0)), pl.BlockSpec((1, WB), lambda i: (0, 0))],
        out_specs=(pl.BlockSpec((tm, WA), lambda i: (i, 0)), pl.BlockSpec((tm, WB), lambda i: (i, 0)),
                   pl.BlockSpec((8, WA), lambda i: (0, 0)), pl.BlockSpec((8, WB), lambda i: (0, 0))),
        out_shape=(jax.ShapeDtypeStruct((M, WA), F32), jax.ShapeDtypeStruct((M, WB), F32),
                   jax.ShapeDtypeStruct((8, WA), F32), jax.ShapeDtypeStruct((8, WB), F32)),
        compiler_params=_params(("arbitrary",)),
    )(dy, ya, yb, ga, gb)


def loss_head(y, target, *, name, tm=512):
    M, D = y.shape
    tm = _tile(M, tm, 8)

    def body(y_ref, t_ref, dy_ref, l_ref):
        i = pl.program_id(0)

        @pl.when(i == 0)
        def _():
            l_ref[...] = jnp.zeros_like(l_ref)

        e = y_ref[...] - t_ref[...]
        dy_ref[...] = e * (1.0 / D)
        l_ref[...] += _fold8(e * e)

    row = pl.BlockSpec((tm, D), lambda i: (i, 0))
    return pl.pallas_call(
        body, name=name, grid=(M // tm,),
        in_specs=[row, row],
        out_specs=(row, pl.BlockSpec((8, D), lambda i: (0, 0))),
        out_shape=(jax.ShapeDtypeStruct((M, D), F32), jax.ShapeDtypeStruct((8, D), F32)),
        compiler_params=_params(("arbitrary",)),
    )(y, target)


def _lane(shape):
    return lax.broadcasted_iota(jnp.int32, shape, len(shape) - 1)


def _swap(t, half, period):
    first = (_lane(t.shape) % period) < half
    return jnp.where(first, pltpu.roll(t, LANES - half, 1), pltpu.roll(t, half, 1))


def _rope(t, c, s, half, period):
    return t * c + _swap(t, half, period) * s


def _rope_t(g, c, s, half, period):
    return g * c - _swap(g, half, period) * s


def rope_tables(positions, rot_dim, period):
    half = rot_dim // 2
    inv_freq = ROPE_THETA ** (-jnp.arange(0, rot_dim, 2, dtype=F32) / rot_dim)
    ang = positions.astype(F32)[:, None] * inv_freq
    cos, sin = jnp.cos(ang), jnp.sin(ang)
    ones = jnp.ones((positions.shape[0], period - rot_dim), F32)
    c = jnp.concatenate([cos, cos, ones], axis=1)
    s = jnp.concatenate([-sin, sin, 0.0 * ones], axis=1)
    reps = LANES // period
    return jnp.tile(c, (1, reps)), jnp.tile(s, (1, reps))


def _branch_blocks(S):
    out = []
    for window, d in DILATED_BRANCHES:
        assert window // d == BLOCK and S % (d * BLOCK) == 0
        out.append((d, (S // d) // BLOCK))
    return out


def _rows(r, n, d):
    return pl.ds(r + n * (BLOCK * d), BLOCK, stride=d) if d > 1 else pl.ds(pl.multiple_of(n * BLOCK, BLOCK), BLOCK)


def _band_masks(n, strict_prev):
    qi = lax.broadcasted_iota(jnp.int32, (BLOCK, BLOCK), 0)
    kj = lax.broadcasted_iota(jnp.int32, (BLOCK, BLOCK), 1)
    cur = kj <= qi
    prev = ((kj > qi) if strict_prev else (kj >= qi)) & (n > 0)
    return cur, prev


def attn_a_fwd(h, ca, sa, *, n_heads, name):
    S = h.shape[0]
    scale = HEAD_DIM_A ** -0.5
    half = HEAD_DIM_A // 8
    branches = _branch_blocks(S)

    def body(q_ref, k_ref, v_ref, c_ref, s_ref, y_ref, lse_ref, qs, ks, m_s, l_s, acc_s):
        qs[...] = _rope(q_ref[...], c_ref[...], s_ref[...], half, HEAD_DIM_A)
        ks[...] = _rope(k_ref[...], c_ref[...], s_ref[...], half, HEAD_DIM_A)
        for bi, (d, nb) in enumerate(branches):
            def blk(idx, carry, bi=bi, d=d, nb=nb):
                r, n = idx // nb, idx % nb
                rc, rp = _rows(r, n, d), _rows(r, jnp.maximum(n - 1, 0), d)
                q = qs[rc, :].astype(BF16)
                cur, prev = _band_masks(n, False)
                sc = jnp.where(cur, _dot_nt(q, ks[rc, :].astype(BF16)) * scale, NEG_INF)
                sp = jnp.where(prev, _dot_nt(q, ks[rp, :].astype(BF16)) * scale, NEG_INF)
                m = jnp.maximum(jnp.max(sc, axis=-1, keepdims=True), jnp.max(sp, axis=-1, keepdims=True))
                pc, pp = jnp.exp(sc - m), jnp.exp(sp - m)
                l = jnp.sum(pc, axis=-1, keepdims=True) + jnp.sum(pp, axis=-1, keepdims=True)
                acc = _dot(pc.astype(BF16), v_ref[rc, :].astype(BF16)) + _dot(pp.astype(BF16), v_ref[rp, :].astype(BF16))
                mb = jnp.broadcast_to(m, (BLOCK, LANES))
                lb = jnp.broadcast_to(l, (BLOCK, LANES))
                if bi == 0:
                    m_s[rc, :], l_s[rc, :], acc_s[rc, :] = mb, lb, acc
                else:
                    m0 = m_s[rc, :]
                    mn = jnp.maximum(m0, mb)
                    a0, a1 = jnp.exp(m0 - mn), jnp.exp(mb - mn)
                    m_s[rc, :] = mn
                    l_s[rc, :] = l_s[rc, :] * a0 + lb * a1
                    acc_s[rc, :] = acc_s[rc, :] * a0 + acc * a1
                return carry

            lax.fori_loop(0, d * nb, blk, 0)
        y_ref[...] = acc_s[...] / l_s[...]
        lse_ref[...] = m_s[...] + jnp.log(l_s[...])

    col = lambda off: pl.BlockSpec((S, LANES), lambda hd: (0, off + hd))
    full = pl.BlockSpec((S, LANES), lambda hd: (0, 0))
    out = pl.BlockSpec((S, LANES), lambda hd: (0, hd))
    o_shape = jax.ShapeDtypeStruct((S, n_heads * LANES), F32)
    return pl.pallas_call(
        body, name=name, grid=(n_heads,),
        in_specs=[col(0), col(n_heads), col(2 * n_heads), full, full],
        out_specs=(out, out), out_shape=(o_shape, o_shape),
        scratch_shapes=[pltpu.VMEM((S, LANES), F32) for _ in range(5)],
        compiler_params=_params(("arbitrary",)),
    )(h, h, h, ca, sa)


def attn_a_bwd(h, ca, sa, ya, lse, dya, *, n_heads, name):
    S = h.shape[0]
    scale = HEAD_DIM_A ** -0.5
    half = HEAD_DIM_A // 8
    branches = _branch_blocks(S)

    def body(q_ref, k_ref, v_ref, c_ref, s_ref, y_ref, lse_ref, dy_ref, o_ref, qs, ks, dq_s, dk_s, dv_s):
        part = pl.program_id(1)

        @pl.when(part == 0)
        def _():
            qs[...] = _rope(q_ref[...], c_ref[...], s_ref[...], half, HEAD_DIM_A)
            ks[...] = _rope(k_ref[...], c_ref[...], s_ref[...], half, HEAD_DIM_A)
            dq_s[...] = jnp.zeros_like(dq_s)
            dk_s[...] = jnp.zeros_like(dk_s)
            dv_s[...] = jnp.zeros_like(dv_s)
            for d, nb in branches:
                def blk(idx, carry, d=d, nb=nb):
                    r, n = idx // nb, idx % nb
                    rc, rp = _rows(r, n, d), _rows(r, jnp.maximum(n - 1, 0), d)
                    q = qs[rc, :].astype(BF16)
                    dy = dy_ref[rc, :]
                    dsum = jnp.sum(dy * y_ref[rc, :], axis=-1, keepdims=True)
                    dyb = dy.astype(BF16)
                    lse_b = lse_ref[rc, :]
                    cur, prev = _band_masks(n, False)
                    dq = jnp.zeros((BLOCK, LANES), F32)
                    for rows, mask in ((rc, cur), (rp, prev)):
                        kb = ks[rows, :].astype(BF16)
                        vb = v_ref[rows, :].astype(BF16)
                        s = jnp.where(mask, _dot_nt(q, kb) * scale, NEG_INF)
                        p = jnp.exp(s - lse_b)
                        ds = (p * (_dot_nt(dyb, vb) - dsum) * scale).astype(BF16)
                        dv_s[rows, :] += _dot_tn(p.astype(BF16), dyb)
                        dk_s[rows, :] += _dot_tn(ds, q)
                        dq = dq + _dot(ds, kb)
                    dq_s[rc, :] += dq
                    return carry

                lax.fori_loop(0, d * nb, blk, 0)
            o_ref[...] = _rope_t(dq_s[...], c_ref[...], s_ref[...], half, HEAD_DIM_A).astype(o_ref.dtype)

        @pl.when(part == 1)
        def _():
            o_ref[...] = _rope_t(dk_s[...], c_ref[...], s_ref[...], half, HEAD_DIM_A).astype(o_ref.dtype)

        @pl.when(part == 2)
        def _():
            o_ref[...] = dv_s[...].astype(o_ref.dtype)

    col = lambda off: pl.BlockSpec((S, LANES), lambda hd, p: (0, off + hd))
    full = pl.BlockSpec((S, LANES), lambda hd, p: (0, 0))
    per_head = pl.BlockSpec((S, LANES), lambda hd, p: (0, hd))
    return pl.pallas_call(
        body, name=name, grid=(n_heads, 3),
        in_specs=[col(0), col(n_heads), col(2 * n_heads), full, full, per_head, per_head, per_head],
        out_specs=pl.BlockSpec((S, LANES), lambda hd, p: (0, p * n_heads + hd)),
        out_shape=jax.ShapeDtypeStruct((S, 3 * n_heads * LANES), BF16),
        scratch_shapes=[pltpu.VMEM((S, LANES), F32) for _ in range(5)],
        compiler_params=_params(("arbitrary", "arbitrary")),
    )(h, h, h, ca, sa, ya, lse, dya)


def _both_halves(t, g):
    low = _lane(t.shape) < HEAD_DIM_B
    return jnp.where(low == (g == 0), t, pltpu.roll(t, HEAD_DIM_B, 1))


def _stack_heads(t):
    low = _lane(t.shape) < HEAD_DIM_B
    return jnp.concatenate([jnp.where(low, t, 0.0), jnp.where(low, 0.0, t)], axis=0)


def _unstack_heads(t2):
    low = _lane((BLOCK, LANES)) < HEAD_DIM_B
    return jnp.where(low, t2[:BLOCK], t2[BLOCK:])


def _fold_halves(t):
    return t + pltpu.roll(t, HEAD_DIM_B, 1)


def _head_rows(ref, tile):
    a = ref[pl.ds(2 * tile, 1), :][:, :1]
    b = ref[pl.ds(2 * tile + 1, 1), :][:, :1]
    return jnp.concatenate([jnp.broadcast_to(a, (BLOCK, 1)), jnp.broadcast_to(b, (BLOCK, 1))], axis=0)


def attn_b_fwd(h, cb, sb, sinks_l, *, off_q, n_qtiles, name):
    S = h.shape[0]
    nblk = S // BLOCK
    scale = HEAD_DIM_B ** -0.5
    half = HEAD_DIM_B // 8
    tiles_per_group = n_qtiles // 2

    def body(q_ref, k_ref, v_ref, c_ref, s_ref, sink_ref, y_ref, lse_ref, qs, kg, vg):
        t = pl.program_id(0)
        g = t // tiles_per_group
        qs[...] = _rope(q_ref[...], c_ref[...], s_ref[...], half, HEAD_DIM_B)
        kg[...] = _both_halves(_rope(k_ref[...], c_ref[...], s_ref[...], half, HEAD_DIM_B), g)
        vg[...] = _both_halves(v_ref[...], g)
        sink = _head_rows(sink_ref, t)

        def blk(n, carry):
            rc = pl.ds(pl.multiple_of(n * BLOCK, BLOCK), BLOCK)
            rp = pl.ds(pl.multiple_of(jnp.maximum(n - 1, 0) * BLOCK, BLOCK), BLOCK)
            q2 = _stack_heads(qs[rc, :]).astype(BF16)
            cur, prev = _band_masks(n, True)
            cur2, prev2 = jnp.concatenate([cur, cur], 0), jnp.concatenate([prev, prev], 0)
            sc = jnp.where(cur2, _dot_nt(q2, kg[rc, :].astype(BF16)) * scale, NEG_INF)
            sp = jnp.where(prev2, _dot_nt(q2, kg[rp, :].astype(BF16)) * scale, NEG_INF)
            m = jnp.maximum(jnp.max(sc, axis=-1, keepdims=True), jnp.max(sp, axis=-1, keepdims=True))
            pc, pp = jnp.exp(sc - m), jnp.exp(sp - m)
            l = jnp.sum(pc, axis=-1, keepdims=True) + jnp.sum(pp, axis=-1, keepdims=True)
            acc = _dot(pc.astype(BF16), vg[rc, :].astype(BF16)) + _dot(pp.astype(BF16), vg[rp, :].astype(BF16))
            m2 = jnp.maximum(m, sink)
            c = jnp.exp(m - m2)
            den = l * c + jnp.exp(sink - m2)
            y_ref[rc, :] = _unstack_heads(acc * (c / den))
            lse_ref[rc, :] = _unstack_heads(jnp.broadcast_to(m2 + jnp.log(den), (2 * BLOCK, LANES)))
            return carry

        lax.fori_loop(0, nblk, blk, 0)

    full = lambda col: pl.BlockSpec((S, LANES), lambda t: (0, col))
    out = pl.BlockSpec((S, LANES), lambda t: (0, t))
    o_shape = jax.ShapeDtypeStruct((S, n_qtiles * LANES), F32)
    return pl.pallas_call(
        body, name=name, grid=(n_qtiles,),
        in_specs=[pl.BlockSpec((S, LANES), lambda t: (0, off_q + t)), full(off_q + n_qtiles), full(off_q + n_qtiles + 1),
                  full(0), full(0), pl.BlockSpec(sinks_l.shape, lambda t: (0, 0))],
        out_specs=(out, out), out_shape=(o_shape, o_shape),
        scratch_shapes=[pltpu.VMEM((S, LANES), F32) for _ in range(3)],
        compiler_params=_params(("arbitrary",)),
    )(h, h, h, cb, sb, sinks_l)


def attn_b_bwd(h, cb, sb, sinks_l, yb, lse, dyb, *, off_q, n_qtiles, name):
    S = h.shape[0]
    nblk = S // BLOCK
    scale = HEAD_DIM_B ** -0.5
    half = HEAD_DIM_B // 8
    tiles_per_group = n_qtiles // 2
    n_steps = n_qtiles + 2

    def body(q_ref, k_ref, v_ref, c_ref, s_ref, sink_ref, y_ref, lse_ref, dy_ref,
             dq_ref, dkv_ref, dsink_ref, qs, kg, vg, dk_acc, dv_acc):
        t = pl.program_id(0)

        @pl.when(t == 0)
        def _():
            dk_acc[...] = jnp.zeros_like(dk_acc)
            dv_acc[...] = jnp.zeros_like(dv_acc)
            dsink_ref[...] = jnp.zeros_like(dsink_ref)

        @pl.when(t < n_qtiles)
        def _():
            g = t // tiles_per_group
            qs[...] = _rope(q_ref[...], c_ref[...], s_ref[...], half, HEAD_DIM_B)
            kg[...] = _both_halves(_rope(k_ref[...], c_ref[...], s_ref[...], half, HEAD_DIM_B), g)
            vg[...] = _both_halves(v_ref[...], g)
            sink = _head_rows(sink_ref, t)

            def blk(n, dsink):
                rc = pl.ds(pl.multiple_of(n * BLOCK, BLOCK), BLOCK)
                rp = pl.ds(pl.multiple_of(jnp.maximum(n - 1, 0) * BLOCK, BLOCK), BLOCK)
                q2 = _stack_heads(qs[rc, :]).astype(BF16)
                dy2 = _stack_heads(dy_ref[rc, :])
                dsum = jnp.sum(dy2 * _stack_heads(y_ref[rc, :]), axis=-1, keepdims=True)
                dy2b = dy2.astype(BF16)
                lse_t = lse_ref[rc, :]
                lse2 = jnp.concatenate([lse_t[:, :1], lse_t[:, HEAD_DIM_B:HEAD_DIM_B + 1]], axis=0)
                cur, prev = _band_masks(n, True)
                dq2 = jnp.zeros((2 * BLOCK, LANES), F32)
                for rows, mask in ((rc, cur), (rp, prev)):
                    kb = kg[rows, :].astype(BF16)
                    vb = vg[rows, :].astype(BF16)
                    mask2 = jnp.concatenate([mask, mask], 0)
                    s = jnp.where(mask2, _dot_nt(q2, kb) * scale, NEG_INF)
                    p = jnp.exp(s - lse2)
                    ds = (p * (_dot_nt(dy2b, vb) - dsum) * scale).astype(BF16)
                    dv_acc[g, rows, :] += _fold_halves(_dot_tn(p.astype(BF16), dy2b))
                    dk_acc[g, rows, :] += _fold_halves(_dot_tn(ds, q2))
                    dq2 = dq2 + _dot(ds, kb)
                dq_ref[rc, :] = _unstack_heads(dq2)
                return dsink - jnp.exp(sink - lse2) * dsum

            dsink = lax.fori_loop(0, nblk, blk, jnp.zeros((2 * BLOCK, 1), F32))
            dq_ref[...] = _rope_t(dq_ref[...], c_ref[...], s_ref[...], half, HEAD_DIM_B)
            d0 = jnp.sum(dsink[:BLOCK], axis=0, keepdims=True)
            d1 = jnp.sum(dsink[BLOCK:], axis=0, keepdims=True)
            dsink_ref[pl.ds(2 * t, 1), :] = jnp.broadcast_to(d0, (1, LANES))
            dsink_ref[pl.ds(2 * t + 1, 1), :] = jnp.broadcast_to(d1, (1, LANES))

        low = _lane((S, LANES)) < HEAD_DIM_B

        @pl.when(t == n_qtiles)
        def _():
            dk = jnp.where(low, dk_acc[0], dk_acc[1])
            dkv_ref[...] = _rope_t(dk, c_ref[...], s_ref[...], half, HEAD_DIM_B)

        @pl.when(t == n_qtiles + 1)
        def _():
            dkv_ref[...] = jnp.where(low, dv_acc[0], dv_acc[1])

    qt = lambda t: jnp.minimum(t, n_qtiles - 1)
    full = lambda col: pl.BlockSpec((S, LANES), lambda t: (0, col))
    per_tile = pl.BlockSpec((S, LANES), lambda t: (0, qt(t)))
    return pl.pallas_call(
        body, name=name, grid=(n_steps,),
        in_specs=[pl.BlockSpec((S, LANES), lambda t: (0, off_q + qt(t))), full(off_q + n_qtiles),
                  full(off_q + n_qtiles + 1), full(0), full(0), pl.BlockSpec(sinks_l.shape, lambda t: (0, 0)),
                  per_tile, per_tile, per_tile],
        out_specs=(per_tile, pl.BlockSpec((S, LANES), lambda t: (0, jnp.maximum(t - n_qtiles, 0))),
                   pl.BlockSpec(sinks_l.shape, lambda t: (0, 0))),
        out_shape=(jax.ShapeDtypeStruct((S, n_qtiles * LANES), F32), jax.ShapeDtypeStruct((S, 2 * LANES), F32),
                   jax.ShapeDtypeStruct(sinks_l.shape, F32)),
        scratch_shapes=[pltpu.VMEM((S, LANES), F32) for _ in range(3)]
        + [pltpu.VMEM((2, S, LANES), F32), pltpu.VMEM((2, S, LANES), F32)],
        compiler_params=_params(("arbitrary",)),
    )(h, h, h, cb, sb, sinks_l, yb, lse, dyb)


def _win(r, nw, d):
    if d > 1:
        return pl.ds(r + nw * (BLOCK * d), 2 * BLOCK, stride=d)
    return pl.ds(pl.multiple_of(nw * BLOCK, BLOCK), 2 * BLOCK)


def _fwd_bias(strict_prev):
    qi = lax.broadcasted_iota(jnp.int32, (BLOCK, 2 * BLOCK), 0)
    kj = lax.broadcasted_iota(jnp.int32, (BLOCK, 2 * BLOCK), 1)
    first = kj < BLOCK
    kk = jnp.where(first, kj, kj - BLOCK)
    prev_ok = (kk > qi) if strict_prev else (kk >= qi)
    zero = first & (kk <= qi)
    mid = (first & prev_ok) | (jnp.logical_not(first) & (kk <= qi))
    return jnp.stack([jnp.where(zero, 0.0, NEG_INF), jnp.where(mid, 0.0, NEG_INF)])


def _bwd_bias(strict_prev):
    qi = lax.broadcasted_iota(jnp.int32, (2 * BLOCK, BLOCK), 0)
    kj = lax.broadcasted_iota(jnp.int32, (2 * BLOCK, BLOCK), 1)
    first = qi < BLOCK
    qq = jnp.where(first, qi, qi - BLOCK)
    prev_ok = (kj > qq) if strict_prev else (kj >= qq)
    mid = (first & (kj <= qq)) | (jnp.logical_not(first) & prev_ok)
    last = jnp.logical_not(first) & (kj <= qq)
    return jnp.stack([jnp.where(mid, 0.0, NEG_INF), jnp.where(last, 0.0, NEG_INF)])


def _pair_bias():
    qi = lax.broadcasted_iota(jnp.int32, (2 * BLOCK, 2 * BLOCK), 0)
    kj = lax.broadcasted_iota(jnp.int32, (2 * BLOCK, 2 * BLOCK), 1)
    return jnp.where((kj <= qi) & (qi - kj <= BLOCK), 0.0, NEG_INF)


UNROLL_PAIR = 2


UNROLL = 4


def attn_a_fwd(h, ca, sa, *, n_heads, name):
    S = h.shape[0]
    scale = HEAD_DIM_A ** -0.5
    half = HEAD_DIM_A // 8
    branches = _branch_blocks(S)

    def body(q_ref, k_ref, v_ref, c_ref, s_ref, y_ref, lse_ref, qs, ks, m_s, l_s, acc_s, bias, pbias):
        qs[...] = _rope(q_ref[...], c_ref[...], s_ref[...], half, HEAD_DIM_A)
        ks[...] = _rope(k_ref[...], c_ref[...], s_ref[...], half, HEAD_DIM_A)
        bias[...] = _fwd_bias(False)
        pbias[...] = _pair_bias()
        for bi, (d, nb) in enumerate(branches):
            assert nb >= 2 and (d * nb) % UNROLL == 0
            if nb == 2 and bi > 0:
                assert d % UNROLL_PAIR == 0

                def pair(it, carry, d=d):
                    rws = [_win(it * UNROLL_PAIR + u, 0, d) for u in range(UNROLL_PAIR)]
                    ss = [_dot_nt(qs[rw, :].astype(BF16), ks[rw, :].astype(BF16)) * scale + pbias[...] for rw in rws]
                    ms = [jnp.max(s, axis=-1, keepdims=True) for s in ss]
                    ps = [jnp.exp(s - m) for s, m in zip(ss, ms)]
                    ls = [jnp.sum(p, axis=-1, keepdims=True) for p in ps]
                    accs = [_dot(p.astype(BF16), v_ref[rw, :].astype(BF16)) for p, rw in zip(ps, rws)]
                    for rw, m, l, acc in zip(rws, ms, ls, accs):
                        mb = jnp.broadcast_to(m, (2 * BLOCK, LANES))
                        m0 = m_s[rw, :]
                        mn = jnp.maximum(m0, mb)
                        a0, a1 = jnp.exp(m0 - mn), jnp.exp(mb - mn)
                        m_s[rw, :] = mn
                        l_s[rw, :] = l_s[rw, :] * a0 + jnp.broadcast_to(l, (2 * BLOCK, LANES)) * a1
                        acc_s[rw, :] = acc_s[rw, :] * a0 + acc * a1
                    return carry

                lax.fori_loop(0, d // UNROLL_PAIR, pair, 0)
                continue

            def blk(it, carry, bi=bi, d=d, nb=nb):
                rn = [((it * UNROLL + u) // nb, (it * UNROLL + u) % nb) for u in range(UNROLL)]
                rcs = [_rows(r, n, d) for r, n in rn]
                rws = [_win(r, jnp.maximum(n - 1, 0), d) for r, n in rn]
                ss = [_dot_nt(qs[rc, :].astype(BF16), ks[rw, :].astype(BF16)) * scale + bias[jnp.minimum(n, 1)]
                      for (r, n), rc, rw in zip(rn, rcs, rws)]
                ms = [jnp.max(s, axis=-1, keepdims=True) for s in ss]
                ps = [jnp.exp(s - m) for s, m in zip(ss, ms)]
                ls = [jnp.sum(p, axis=-1, keepdims=True) for p in ps]
                accs = [_dot(p.astype(BF16), v_ref[rw, :].astype(BF16)) for p, rw in zip(ps, rws)]
                for rc, m, l, acc in zip(rcs, ms, ls, accs):
                    mb = jnp.broadcast_to(m, (BLOCK, LANES))
                    lb = jnp.broadcast_to(l, (BLOCK, LANES))
                    if bi == 0:
                        m_s[rc, :], l_s[rc, :], acc_s[rc, :] = mb, lb, acc
                    else:
                        m0 = m_s[rc, :]
                        mn = jnp.maximum(m0, mb)
                        a0, a1 = jnp.exp(m0 - mn), jnp.exp(mb - mn)
                        m_s[rc, :] = mn
                        l_s[rc, :] = l_s[rc, :] * a0 + lb * a1
                        acc_s[rc, :] = acc_s[rc, :] * a0 + acc * a1
                return carry

            lax.fori_loop(0, d * nb // UNROLL, blk, 0)
        y_ref[...] = acc_s[...] / l_s[...]
        lse_ref[...] = m_s[...] + jnp.log(l_s[...])

    col = lambda off: pl.BlockSpec((S, LANES), lambda hd: (0, off + hd))
    full = pl.BlockSpec((S, LANES), lambda hd: (0, 0))
    out = pl.BlockSpec((S, LANES), lambda hd: (0, hd))
    o_shape = jax.ShapeDtypeStruct((S, n_heads * LANES), F32)
    return pl.pallas_call(
        body, name=name, grid=(n_heads,),
        in_specs=[col(0), col(n_heads), col(2 * n_heads), full, full],
        out_specs=(out, out), out_shape=(o_shape, o_shape),
        scratch_shapes=[pltpu.VMEM((S, LANES), F32) for _ in range(5)]
        + [pltpu.VMEM((2, BLOCK, 2 * BLOCK), F32), pltpu.VMEM((2 * BLOCK, 2 * BLOCK), F32)],
        compiler_params=_params(("arbitrary",)),
    )(h, h, h, ca, sa)


def attn_a_bwd(h, ca, sa, ya, lse, dya, *, n_heads, name):
    S = h.shape[0]
    scale = HEAD_DIM_A ** -0.5
    half = HEAD_DIM_A // 8
    branches = _branch_blocks(S)

    def body(q_ref, k_ref, v_ref, c_ref, s_ref, y_ref, lse_ref, dy_ref, o_ref, qs, ks, dq_s, dk_s, dv_s, bias, pbias):
        part = pl.program_id(1)

        @pl.when(part == 0)
        def _():
            qs[...] = _rope(q_ref[...], c_ref[...], s_ref[...], half, HEAD_DIM_A)
            ks[...] = _rope(k_ref[...], c_ref[...], s_ref[...], half, HEAD_DIM_A)
            dq_s[...] = jnp.zeros_like(dq_s)
            bias[...] = _bwd_bias(False)
            pbias[...] = _pair_bias()
            for bi, (d, nb) in enumerate(branches):
                assert nb >= 2 and (d * nb) % UNROLL == 0
                if nb == 2 and bi > 0:
                    assert d % UNROLL_PAIR == 0

                    def pair(it, carry, d=d):
                        rws = [_win(it * UNROLL_PAIR + u, 0, d) for u in range(UNROLL_PAIR)]
                        q2 = [qs[rw, :].astype(BF16) for rw in rws]
                        k2 = [ks[rw, :].astype(BF16) for rw in rws]
                        dy2 = [dy_ref[rw, :] for rw in rws]
                        dsum = [jnp.sum(dy * y_ref[rw, :], axis=-1, keepdims=True) for dy, rw in zip(dy2, rws)]
                        dy2b = [dy.astype(BF16) for dy in dy2]
                        ss = [_dot_nt(q, k) * scale + pbias[...] for q, k in zip(q2, k2)]
                        dps = [_dot_nt(dy, v_ref[rw, :].astype(BF16)) for dy, rw in zip(dy2b, rws)]
                        ps = [jnp.exp(s - lse_ref[rw, :][:, :1]) for s, rw in zip(ss, rws)]
                        dss = [(p * (dp - dm) * scale).astype(BF16) for p, dp, dm in zip(ps, dps, dsum)]
                        dvs = [_dot_tn(p.astype(BF16), dy) for p, dy in zip(ps, dy2b)]
                        dks = [_dot_tn(ds, q) for ds, q in zip(dss, q2)]
                        dqs = [_dot(ds, k) for ds, k in zip(dss, k2)]
                        for rw, dv, dk, dq in zip(rws, dvs, dks, dqs):
                            dv_s[rw, :] += dv
                            dk_s[rw, :] += dk
                            dq_s[rw, :] += dq
                        return carry

                    lax.fori_loop(0, d // UNROLL_PAIR, pair, 0)
                    continue

                def blk(it, carry, bi=bi, d=d, nb=nb):
                    rj = [((it * UNROLL + u) // nb, (it * UNROLL + u) % nb) for u in range(UNROLL)]
                    rks = [_rows(r, j, d) for r, j in rj]
                    rws = [_win(r, jnp.minimum(j, nb - 2), d) for r, j in rj]
                    q2 = [qs[rw, :].astype(BF16) for rw in rws]
                    kb = [ks[rk, :].astype(BF16) for rk in rks]
                    dy2 = [dy_ref[rw, :] for rw in rws]
                    dsum = [jnp.sum(dy * y_ref[rw, :], axis=-1, keepdims=True) for dy, rw in zip(dy2, rws)]
                    dy2b = [dy.astype(BF16) for dy in dy2]
                    ss = [_dot_nt(q, k) * scale + bias[(j == nb - 1).astype(jnp.int32)] for q, k, (r, j) in zip(q2, kb, rj)]
                    dps = [_dot_nt(dy, v_ref[rk, :].astype(BF16)) for dy, rk in zip(dy2b, rks)]
                    ps = [jnp.exp(s - lse_ref[rw, :]) for s, rw in zip(ss, rws)]
                    dss = [(p * (dp - dm) * scale).astype(BF16) for p, dp, dm in zip(ps, dps, dsum)]
                    dvs = [_dot_tn(p.astype(BF16), dy) for p, dy in zip(ps, dy2b)]
                    dks = [_dot_tn(ds, q) for ds, q in zip(dss, q2)]
                    dqs = [_dot(ds, k) for ds, k in zip(dss, kb)]
                    for rk, rw, dv, dk, dq in zip(rks, rws, dvs, dks, dqs):
                        if bi == 0:
                            dv_s[rk, :], dk_s[rk, :] = dv, dk
                        else:
                            dv_s[rk, :] += dv
                            dk_s[rk, :] += dk
                        dq_s[rw, :] += dq
                    return carry

                lax.fori_loop(0, d * nb // UNROLL, blk, 0)
            o_ref[...] = _rope_t(dq_s[...], c_ref[...], s_ref[...], half, HEAD_DIM_A).astype(o_ref.dtype)

        @pl.when(part == 1)
        def _():
            o_ref[...] = _rope_t(dk_s[...], c_ref[...], s_ref[...], half, HEAD_DIM_A).astype(o_ref.dtype)

        @pl.when(part == 2)
        def _():
            o_ref[...] = dv_s[...].astype(o_ref.dtype)

    col = lambda off: pl.BlockSpec((S, LANES), lambda hd, p: (0, off + hd))
    full = pl.BlockSpec((S, LANES), lambda hd, p: (0, 0))
    per_head = pl.BlockSpec((S, LANES), lambda hd, p: (0, hd))
    return pl.pallas_call(
        body, name=name, grid=(n_heads, 3),
        in_specs=[col(0), col(n_heads), col(2 * n_heads), full, full, per_head, per_head, per_head],
        out_specs=pl.BlockSpec((S, LANES), lambda hd, p: (0, p * n_heads + hd)),
        out_shape=jax.ShapeDtypeStruct((S, h.shape[1]), BF16),
        scratch_shapes=[pltpu.VMEM((S, LANES), F32) for _ in range(5)]
        + [pltpu.VMEM((2, 2 * BLOCK, BLOCK), F32), pltpu.VMEM((2 * BLOCK, 2 * BLOCK), F32)],
        compiler_params=_params(("arbitrary", "arbitrary")),
    )(h, h, h, ca, sa, ya, lse, dya)


def _unstack_heads(t2):
    rows = t2.shape[0] // 2
    low = _lane((rows, LANES)) < HEAD_DIM_B
    return jnp.where(low, t2[:rows], t2[rows:])


def _head_rows(ref, tile, rows):
    a = ref[pl.ds(2 * tile, 1), :][:, :1]
    b = ref[pl.ds(2 * tile + 1, 1), :][:, :1]
    return jnp.concatenate([jnp.broadcast_to(a, (rows, 1)), jnp.broadcast_to(b, (rows, 1))], axis=0)


UNROLL_B = 2


def attn_b_fwd(h, cb, sb, sinks_l, *, off_q, n_qtiles, name):
    S = h.shape[0]
    nblk = S // BLOCK
    scale = HEAD_DIM_B ** -0.5
    half = HEAD_DIM_B // 8
    tiles_per_group = n_qtiles // 2
    assert nblk >= 2 and nblk % UNROLL_B == 0

    def body(q_ref, k_ref, v_ref, c_ref, s_ref, sink_ref, y_ref, lse_ref, qs, kg, vg, bias):
        t = pl.program_id(0)
        g = t // tiles_per_group
        qs[...] = _rope(q_ref[...], c_ref[...], s_ref[...], half, HEAD_DIM_B)

        @pl.when(t % tiles_per_group == 0)
        def _():
            kg[...] = _both_halves(_rope(k_ref[...], c_ref[...], s_ref[...], half, HEAD_DIM_B), g)
            vg[...] = _both_halves(v_ref[...], g)

        @pl.when(t == 0)
        def _():
            fb = _fwd_bias(True)
            bias[...] = jnp.concatenate([fb, fb], axis=1)

        sink = _head_rows(sink_ref, t, BLOCK)

        def blk(it, carry):
            ns = [it * UNROLL_B + u for u in range(UNROLL_B)]
            rcs = [pl.ds(pl.multiple_of(n * BLOCK, BLOCK), BLOCK) for n in ns]
            rws = [pl.ds(pl.multiple_of(jnp.maximum(n - 1, 0) * BLOCK, BLOCK), 2 * BLOCK) for n in ns]
            ss = [_dot_nt(_stack_heads(qs[rc, :]).astype(BF16), kg[rw, :].astype(BF16)) * scale + bias[jnp.minimum(n, 1)]
                  for n, rc, rw in zip(ns, rcs, rws)]
            ms = [jnp.max(s, axis=-1, keepdims=True) for s in ss]
            ps = [jnp.exp(s - m) for s, m in zip(ss, ms)]
            ls = [jnp.sum(p, axis=-1, keepdims=True) for p in ps]
            accs = [_dot(p.astype(BF16), vg[rw, :].astype(BF16)) for p, rw in zip(ps, rws)]
            for rc, m, l, acc in zip(rcs, ms, ls, accs):
                m2 = jnp.maximum(m, sink)
                c = jnp.exp(m - m2)
                den = l * c + jnp.exp(sink - m2)
                y_ref[rc, :] = _unstack_heads(acc * (c / den))
                lse_ref[rc, :] = _unstack_heads(jnp.broadcast_to(m2 + jnp.log(den), (2 * BLOCK, LANES)))
            return carry

        lax.fori_loop(0, nblk // UNROLL_B, blk, 0)

    full = lambda col: pl.BlockSpec((S, LANES), lambda t: (0, col))
    out = pl.BlockSpec((S, LANES), lambda t: (0, t))
    o_shape = jax.ShapeDtypeStruct((S, n_qtiles * LANES), F32)
    return pl.pallas_call(
        body, name=name, grid=(n_qtiles,),
        in_specs=[pl.BlockSpec((S, LANES), lambda t: (0, off_q + t)), full(off_q + n_qtiles), full(off_q + n_qtiles + 1),
                  full(0), full(0), pl.BlockSpec(sinks_l.shape, lambda t: (0, 0))],
        out_specs=(out, out), out_shape=(o_shape, o_shape),
        scratch_shapes=[pltpu.VMEM((S, LANES), F32) for _ in range(3)] + [pltpu.VMEM((2, 2 * BLOCK, 2 * BLOCK), F32)],
        compiler_params=_params(("arbitrary",)),
    )(h, h, h, cb, sb, sinks_l)


def attn_b_bwd(h, cb, sb, sinks_l, yb, lse, dyb, dh, *, off_q, n_qtiles, name):
    S = h.shape[0]
    nblk = S // BLOCK
    scale = HEAD_DIM_B ** -0.5
    half = HEAD_DIM_B // 8
    tiles_per_group = n_qtiles // 2
    n_steps = n_qtiles + 2
    W = 2 * BLOCK
    assert nblk >= 2 and nblk % UNROLL_B == 0

    def body(q_ref, k_ref, v_ref, c_ref, s_ref, sink_ref, y_ref, lse_ref, dy_ref, dh_in,
             o_ref, dsink_ref, qs, kg, vg, dk_acc, dv_acc, bias, dq_ref):
        t = pl.program_id(0)

        @pl.when(t == 0)
        def _():
            dk_acc[...] = jnp.zeros_like(dk_acc)
            dv_acc[...] = jnp.zeros_like(dv_acc)
            dsink_ref[...] = jnp.zeros_like(dsink_ref)
            bb = _bwd_bias(True)
            bias[...] = jnp.concatenate([bb, bb], axis=1)

        @pl.when(t < n_qtiles)
        def _():
            g = t // tiles_per_group
            qs[...] = _rope(q_ref[...], c_ref[...], s_ref[...], half, HEAD_DIM_B)

            @pl.when(t % tiles_per_group == 0)
            def _():
                kg[...] = _both_halves(_rope(k_ref[...], c_ref[...], s_ref[...], half, HEAD_DIM_B), g)
                vg[...] = _both_halves(v_ref[...], g)

            dq_ref[...] = jnp.zeros_like(dq_ref)
            sink = _head_rows(sink_ref, t, W)
            row = lax.broadcasted_iota(jnp.int32, (2 * W, 1), 0) % W
            low = _lane((W, LANES)) < HEAD_DIM_B

            def blk(it, dsink):
                js = [it * UNROLL_B + u for u in range(UNROLL_B)]
                rks = [pl.ds(pl.multiple_of(j * BLOCK, BLOCK), BLOCK) for j in js]
                rws = [pl.ds(pl.multiple_of(jnp.minimum(j, nblk - 2) * BLOCK, BLOCK), W) for j in js]
                q2 = [_stack_heads(qs[rw, :]).astype(BF16) for rw in rws]
                dy2 = [_stack_heads(dy_ref[rw, :]) for rw in rws]
                dsum = [jnp.sum(dy * _stack_heads(y_ref[rw, :]), axis=-1, keepdims=True) for dy, rw in zip(dy2, rws)]
                dy2b = [dy.astype(BF16) for dy in dy2]
                lse2 = []
                for rw in rws:
                    lt = lse_ref[rw, :]
                    lr = pltpu.roll(lt, HEAD_DIM_B, 1)
                    lse2.append(jnp.concatenate([jnp.where(low, lt, lr), jnp.where(low, lr, lt)], axis=0))
                kb = [kg[rk, :].astype(BF16) for rk in rks]
                ss = [_dot_nt(q, k) * scale + bias[(j == nblk - 1).astype(jnp.int32)] for q, k, j in zip(q2, kb, js)]
                dps = [_dot_nt(dy, vg[rk, :].astype(BF16)) for dy, rk in zip(dy2b, rks)]
                ps = [jnp.exp(s - l2) for s, l2 in zip(ss, lse2)]
                dss = [(p * (dp - dm) * scale).astype(BF16) for p, dp, dm in zip(ps, dps, dsum)]
                dvs = [_fold_halves(_dot_tn(p.astype(BF16), dy)) for p, dy in zip(ps, dy2b)]
                dks = [_fold_halves(_dot_tn(ds, q)) for ds, q in zip(dss, q2)]
                dqs = [_dot(ds, k) for ds, k in zip(dss, kb)]
                for j, rk, rw, dv, dk, dq, l2, dm in zip(js, rks, rws, dvs, dks, dqs, lse2, dsum):
                    dv_acc[g, rk, :] += dv
                    dk_acc[g, rk, :] += dk
                    dq_ref[rw, :] += _unstack_heads(dq)
                    diag = (row >= BLOCK).astype(jnp.int32) == (j == nblk - 1).astype(jnp.int32)
                    dsink = dsink - jnp.where(diag, jnp.exp(sink - l2[:, :1]) * dm, 0.0)
                return dsink

            dsink = lax.fori_loop(0, nblk // UNROLL_B, blk, jnp.zeros((2 * W, 1), F32))
            o_ref[...] = _rope_t(dq_ref[...], c_ref[...], s_ref[...], half, HEAD_DIM_B).astype(o_ref.dtype)
            d0 = jnp.sum(dsink[:W], axis=0, keepdims=True)
            d1 = jnp.sum(dsink[W:], axis=0, keepdims=True)
            dsink_ref[pl.ds(2 * t, 1), :] = jnp.broadcast_to(d0, (1, LANES))
            dsink_ref[pl.ds(2 * t + 1, 1), :] = jnp.broadcast_to(d1, (1, LANES))

        low_s = _lane((S, LANES)) < HEAD_DIM_B

        @pl.when(t == n_qtiles)
        def _():
            dk = jnp.where(low_s, dk_acc[0], dk_acc[1])
            o_ref[...] = _rope_t(dk, c_ref[...], s_ref[...], half, HEAD_DIM_B).astype(o_ref.dtype)

        @pl.when(t == n_qtiles + 1)
        def _():
            o_ref[...] = jnp.where(low_s, dv_acc[0], dv_acc[1]).astype(o_ref.dtype)

    qt = lambda t: jnp.minimum(t, n_qtiles - 1)
    full = lambda col: pl.BlockSpec((S, LANES), lambda t: (0, col))
    per_tile = pl.BlockSpec((S, LANES), lambda t: (0, qt(t)))
    return pl.pallas_call(
        body, name=name, grid=(n_steps,),
        in_specs=[pl.BlockSpec((S, LANES), lambda t: (0, off_q + qt(t))), full(off_q + n_qtiles),
                  full(off_q + n_qtiles + 1), full(0), full(0), pl.BlockSpec(sinks_l.shape, lambda t: (0, 0)),
                  per_tile, per_tile, per_tile, pl.BlockSpec(memory_space=pl.ANY)],
        out_specs=(pl.BlockSpec((S, LANES), lambda t: (0, off_q + t)), pl.BlockSpec(sinks_l.shape, lambda t: (0, 0))),
        out_shape=(jax.ShapeDtypeStruct(dh.shape, dh.dtype), jax.ShapeDtypeStruct(sinks_l.shape, F32)),
        input_output_aliases={9: 0},
        scratch_shapes=[pltpu.VMEM((S, LANES), F32) for _ in range(3)]
        + [pltpu.VMEM((2, S, LANES), F32), pltpu.VMEM((2, S, LANES), F32), pltpu.VMEM((2, 2 * W, BLOCK), F32),
           pltpu.VMEM((S, LANES), F32)],
        compiler_params=_params(("arbitrary",)),
    )(h, h, h, cb, sb, sinks_l, yb, lse, dyb, dh)


def mem_attn_fwd(q, kv, *, name, tm=512):
    S, D = q.shape
    n_mem = kv.shape[0]
    hd = D // N_MEM_HEADS
    scale = hd ** -0.5
    tm = _tile(S, tm, 8)

    def body(q_ref, kv_ref, o_ref):
        for hh in range(N_MEM_HEADS):
            cols = slice(hh * hd, (hh + 1) * hd)
            s = _dot_nt(q_ref[:, cols], kv_ref[:, cols]) * scale
            s = s - jnp.max(s, axis=-1, keepdims=True)
            e = jnp.exp(s)
            p = e / jnp.sum(e, axis=-1, keepdims=True)
            o_ref[:, cols] = _dot(p.astype(BF16), kv_ref[:, D + hh * hd:D + (hh + 1) * hd]).astype(o_ref.dtype)

    return pl.pallas_call(
        body, name=name, grid=(S // tm,),
        in_specs=[pl.BlockSpec((tm, D), lambda i: (i, 0)), pl.BlockSpec((n_mem, 2 * D), lambda i: (0, 0))],
        out_specs=pl.BlockSpec((tm, D), lambda i: (i, 0)),
        out_shape=jax.ShapeDtypeStruct((S, D), BF16),
        compiler_params=_params(("parallel",)),
    )(q, kv)


def mem_attn_bwd(q, kv, do, *, name, tm=512):
    S, D = q.shape
    n_mem = kv.shape[0]
    hd = D // N_MEM_HEADS
    scale = hd ** -0.5
    tm = _tile(S, tm, 8)

    def body(q_ref, kv_ref, do_ref, dq_ref, dkv_ref):
        i = pl.program_id(0)

        @pl.when(i == 0)
        def _():
            dkv_ref[...] = jnp.zeros_like(dkv_ref)

        for hh in range(N_MEM_HEADS):
            cols = slice(hh * hd, (hh + 1) * hd)
            vcols = slice(D + hh * hd, D + (hh + 1) * hd)
            qh, kh, vh, doh = q_ref[:, cols], kv_ref[:, cols], kv_ref[:, vcols], do_ref[:, cols]
            s = _dot_nt(qh, kh) * scale
            s = s - jnp.max(s, axis=-1, keepdims=True)
            e = jnp.exp(s)
            p = e / jnp.sum(e, axis=-1, keepdims=True)
            dp = _dot_nt(doh, vh)
            ds = (p * (dp - jnp.sum(dp * p, axis=-1, keepdims=True)) * scale).astype(BF16)
            dq_ref[:, cols] = _dot(ds, kh).astype(dq_ref.dtype)
            dkv_ref[:, cols] += _dot_tn(ds, qh)
            dkv_ref[:, vcols] += _dot_tn(p.astype(BF16), doh)

    row = pl.BlockSpec((tm, D), lambda i: (i, 0))
    kvs = pl.BlockSpec((n_mem, 2 * D), lambda i: (0, 0))
    return pl.pallas_call(
        body, name=name, grid=(S // tm,),
        in_specs=[row, kvs, row], out_specs=(row, kvs),
        out_shape=(jax.ShapeDtypeStruct((S, D), BF16), jax.ShapeDtypeStruct((n_mem, 2 * D), F32)),
        compiler_params=_params(("arbitrary",)),
    )(q, kv, do)


def _rows_view(t):
    return t.reshape(-1, t.shape[-1])


def _row_tile(rows, cols, target_elems=512 * 1024):
    return _tile(rows, max(8, target_elems // cols), 8)


def cast_bf16(w, *, name):
    v = _rows_view(w)
    R, C = v.shape
    tr = _row_tile(R, C)

    def body(w_ref, o_ref):
        o_ref[...] = w_ref[...].astype(BF16)

    spec = pl.BlockSpec((tr, C), lambda i: (i, 0))
    out = pl.pallas_call(body, name=name, grid=(R // tr,), in_specs=[spec], out_specs=spec,
                         out_shape=jax.ShapeDtypeStruct((R, C), BF16), compiler_params=_params(("parallel",)))(v)
    return out.reshape(w.shape)


def mesh_place():
    return tuple(lax.axis_index(a).astype(jnp.int32).reshape(1) for a in ("x", "y", "c"))


def pair_sum(p, r1, place, *, name):
    nsh, r, c = p.shape
    hr = r // 2
    tr = _row_tile(hr, c)
    nt = hr // tr

    def body(x_ref, y_ref, c_ref, p_ref, r_ref, o_ref):
        o_ref[...] = (p_ref[...].astype(F32) + r_ref[...].astype(F32)).astype(BF16)

    return pl.pallas_call(
        body, name=name,
        grid_spec=pltpu.PrefetchScalarGridSpec(
            num_scalar_prefetch=3, grid=(nsh, nt),
            in_specs=[pl.BlockSpec((None, tr, c), lambda s, i, x, y, cc: (s, cc[0] * nt + i, 0)),
                      pl.BlockSpec((None, tr, c), lambda s, i, x, y, cc: (s, i, 0))],
            out_specs=pl.BlockSpec((None, tr, c), lambda s, i, x, y, cc: (s, i, 0))),
        out_shape=jax.ShapeDtypeStruct((nsh, hr, c), BF16),
        compiler_params=_params(("parallel", "parallel")),
    )(*place, p, r1)


def cast_into_slot(w, li, place, *, name):
    _, R, C = w.shape
    tr = _row_tile(R, C)

    def body(x_ref, y_ref, c_ref, w_ref, o_ref):
        o_ref[...] = w_ref[...].astype(BF16)

    return pl.pallas_call(
        body, name=name,
        grid_spec=pltpu.PrefetchScalarGridSpec(
            num_scalar_prefetch=3, grid=(R // tr,),
            in_specs=[pl.BlockSpec((None, tr, C), lambda i, x, y, cc: (li, i, 0))],
            out_specs=pl.BlockSpec((None, tr, C), lambda i, x, y, cc: (2 * x[0] + y[0], i, 0))),
        out_shape=jax.ShapeDtypeStruct((N_CHIPS, R, C), BF16),
        compiler_params=_params(("parallel",)),
    )(*place, w)


def chip_sum(q, r2, place, gbuf, li, *, name):
    _, hr, c = q.shape
    tr = _row_tile(hr, c, 256 * 1024)
    nt = hr // tr

    def body(x_ref, y_ref, c_ref, q_ref, r_ref, g_in, o_ref):
        acc = q_ref[...].astype(F32)
        for k in range(r_ref.shape[0]):
            acc = acc + r_ref[k].astype(F32)
        o_ref[...] = acc

    return pl.pallas_call(
        body, name=name,
        grid_spec=pltpu.PrefetchScalarGridSpec(
            num_scalar_prefetch=3, grid=(nt,),
            in_specs=[pl.BlockSpec((None, tr, c), lambda i, x, y, cc: (2 * x[0] + y[0], i, 0)),
                      pl.BlockSpec((r2.shape[0], tr, c), lambda i, x, y, cc: (0, i, 0)),
                      pl.BlockSpec(memory_space=pl.ANY)],
            out_specs=pl.BlockSpec((None, tr, c), lambda i, x, y, cc: (li, cc[0] * nt + i, 0))),
        out_shape=jax.ShapeDtypeStruct(gbuf.shape, F32),
        input_output_aliases={5: 0},
        compiler_params=_params(("parallel",)),
    )(*place, q, r2, gbuf)


def adamw(w, g, m, v, *, name, emit_g=False):
    shape = w.shape
    wv, gv, mv, vv = (_rows_view(t) for t in (w, g, m, v))
    R, C = wv.shape
    tr = _row_tile(R, C, 256 * 1024)
    c1 = 1.0 / (1.0 - ADAM_B1 ** ADAM_STEP)
    c2 = 1.0 / (1.0 - ADAM_B2 ** ADAM_STEP)
    n_out = 4 if emit_g else 3

    def body(w_ref, g_ref, m_ref, v_ref, d_ref, nm_ref, nv_ref, *go_ref):
        g_ = g_ref[...]
        nm = ADAM_B1 * m_ref[...] + (1.0 - ADAM_B1) * g_
        nv = ADAM_B2 * v_ref[...] + (1.0 - ADAM_B2) * (g_ * g_)
        m_hat = nm * c1
        v_hat = nv * c2
        d_ref[...] = -ADAM_LR * (m_hat / (jnp.sqrt(v_hat) + ADAM_EPS) + ADAM_WD * w_ref[...])
        nm_ref[...] = nm
        nv_ref[...] = nv
        if emit_g:
            go_ref[0][...] = g_

    spec = pl.BlockSpec((tr, C), lambda i: (i, 0))
    o = jax.ShapeDtypeStruct((R, C), F32)
    outs = pl.pallas_call(body, name=name, grid=(R // tr,), in_specs=[spec] * 4, out_specs=(spec,) * n_out,
                          out_shape=(o,) * n_out, compiler_params=_params(("parallel",)))(wv, gv, mv, vv)
    return tuple(t.reshape(shape) for t in outs)


def _place():
    x, y, c = lax.axis_index("x"), lax.axis_index("y"), lax.axis_index("c")
    others = [(1 - x, y), (x, 1 - y), (1 - x, 1 - y)]
    return x, y, c, others


def _any_specs(n):
    return [pl.BlockSpec(memory_space=pl.ANY) for _ in range(n)]


HBM_SPEC = pl.BlockSpec(memory_space=pltpu.HBM)
SEM_SPEC = pl.BlockSpec(memory_space=pltpu.SEMAPHORE)
DATAFLOW = pltpu.SideEffectType.DATAFLOW_SIDE_EFFECTING
TOKEN = jax.ShapeDtypeStruct((8, LANES), F32)


def _in_hbm(arrays):
    return [pltpu.with_memory_space_constraint(a, pltpu.HBM) for a in arrays]


def _gather_copy(g, t, j, slot, px, py, c, send, recv):
    hr = g[t].shape[1] // 2
    rows = g[t].at[slot, pl.ds(c * hr, hr)]
    return pltpu.make_async_remote_copy(rows, rows, send.at[3 * t + j], recv.at[3 * t + j], device_id=(px, py, c), device_id_type=MESH)


def gather_start(gs, *, name):
    n = len(gs)

    def body(*refs):
        g, token = refs[:n], refs[-1]
        send, recv = refs[n], refs[n + 1]
        x, y, c, others = _place()
        for t in range(n):
            for j, (px, py) in enumerate(others):
                _gather_copy(g, t, j, 2 * x + y, px, py, c, send, recv).start()
        token[...] = jnp.zeros_like(token)

    outs = pl.pallas_call(
        body, name=name,
        in_specs=[HBM_SPEC] * n,
        out_specs=(SEM_SPEC, SEM_SPEC, *[HBM_SPEC] * n, pl.BlockSpec(memory_space=pltpu.VMEM)),
        out_shape=(pltpu.SemaphoreType.DMA((3 * n,)), pltpu.SemaphoreType.DMA((3 * n,)),
                   *[pltpu.HBM(g.shape, g.dtype) for g in gs], TOKEN),
        input_output_aliases={t: 2 + t for t in range(n)},
        compiler_params=pltpu.CompilerParams(has_side_effects=DATAFLOW),
    )(*_in_hbm(gs))
    return outs[0], outs[1], list(outs[2:2 + n]), outs[-1]


def gather_wait(send, recv, gs, after, *, name):
    n = len(gs)

    def body(*refs):
        g = refs[:n]
        send, recv = refs[n], refs[n + 1]
        x, y, c, others = _place()
        for t in range(n):
            for j, (px, py) in enumerate(others):
                _gather_copy(g, t, j, 2 * x + y, px, py, c, send, recv).wait_send()
                _gather_copy(g, t, j, 2 * px + py, px, py, c, send, recv).wait_recv()

    outs = pl.pallas_call(
        body, name=name,
        in_specs=[HBM_SPEC] * n + [SEM_SPEC, SEM_SPEC, pl.BlockSpec(memory_space=pl.ANY)],
        out_specs=tuple([HBM_SPEC] * n),
        out_shape=tuple(pltpu.HBM(g.shape, g.dtype) for g in gs),
        input_output_aliases={t: t for t in range(n)},
        compiler_params=pltpu.CompilerParams(has_side_effects=DATAFLOW),
    )(*gs, send, recv, after)
    return list(outs)


def gather_forward(gs, *, name):
    n = len(gs)

    def body(*refs):
        g = refs[n:2 * n]
        send, recv = refs[2 * n:]
        x, y, c, others = _place()
        cps = []
        for t in range(n):
            hr = g[t].shape[1] // 2
            for j, (px, py) in enumerate(others):
                rows = g[t].at[2 * px + py, pl.ds(c * hr, hr)]
                cp = pltpu.make_async_remote_copy(rows, rows, send.at[3 * t + j], recv.at[3 * t + j],
                                                  device_id=(x, y, 1 - c), device_id_type=MESH)
                cp.start()
                cps.append(cp)
        for t in range(n):
            hr = g[t].shape[1] // 2
            for j, (px, py) in enumerate(others):
                rows = g[t].at[2 * px + py, pl.ds((1 - c) * hr, hr)]
                pltpu.make_async_remote_copy(rows, rows, send.at[3 * t + j], recv.at[3 * t + j],
                                             device_id=(x, y, 1 - c), device_id_type=MESH).wait_recv()
        for cp in cps:
            cp.wait_send()

    return pl.pallas_call(
        body, name=name,
        in_specs=_any_specs(n), out_specs=_any_specs(n),
        out_shape=[jax.ShapeDtypeStruct(g.shape, g.dtype) for g in gs],
        input_output_aliases={t: t for t in range(n)},
        scratch_shapes=[pltpu.SemaphoreType.DMA((3 * n,)), pltpu.SemaphoreType.DMA((3 * n,))],
        compiler_params=pltpu.CompilerParams(has_side_effects=True),
    )(*gs)


def sibling_halves(parts, *, name):
    n = len(parts)

    def body(*refs):
        src, dst = refs[:n], refs[n:2 * n]
        send, recv = refs[2 * n:]
        x, y, c, _ = _place()
        cps = []
        for t in range(n):
            hr = src[t].shape[1] // 2
            cp = pltpu.make_async_remote_copy(src[t].at[:, pl.ds((1 - c) * hr, hr)], dst[t], send.at[t], recv.at[t],
                                              device_id=(x, y, 1 - c), device_id_type=MESH)
            cp.start()
            cps.append(cp)
        for cp in cps:
            cp.wait()

    return pl.pallas_call(
        body, name=name,
        in_specs=_any_specs(n), out_specs=_any_specs(n),
        out_shape=[jax.ShapeDtypeStruct((p.shape[0], p.shape[1] // 2, p.shape[2]), p.dtype) for p in parts],
        scratch_shapes=[pltpu.SemaphoreType.DMA((n,)), pltpu.SemaphoreType.DMA((n,))],
        compiler_params=pltpu.CompilerParams(has_side_effects=True),
    )(*parts)


def _chips_copy(q, land, t, j, px, py, c, send, recv):
    return pltpu.make_async_remote_copy(q[t].at[2 * px + py], land[t].at[j], send.at[3 * t + j], recv.at[3 * t + j],
                                        device_id=(px, py, c), device_id_type=MESH)


def chips_start(qs, *, name):
    n = len(qs)

    def body(*refs):
        q, land, token = refs[:n], refs[n:2 * n], refs[-1]
        send, recv = refs[2 * n], refs[2 * n + 1]
        x, y, c, others = _place()
        for t in range(n):
            for j, (px, py) in enumerate(others):
                _chips_copy(q, land, t, j, px, py, c, send, recv).start()
        token[...] = jnp.zeros_like(token)

    lands = [lax.empty((3,) + q.shape[1:], q.dtype) for q in qs]
    outs = pl.pallas_call(
        body, name=name,
        in_specs=[HBM_SPEC] * (2 * n),
        out_specs=(SEM_SPEC, SEM_SPEC, *[HBM_SPEC] * (2 * n), pl.BlockSpec(memory_space=pltpu.VMEM)),
        out_shape=(pltpu.SemaphoreType.DMA((3 * n,)), pltpu.SemaphoreType.DMA((3 * n,)),
                   *[pltpu.HBM(a.shape, a.dtype) for a in qs + lands], TOKEN),
        input_output_aliases={t: 2 + t for t in range(2 * n)},
        compiler_params=pltpu.CompilerParams(has_side_effects=DATAFLOW),
    )(*_in_hbm(qs + lands))
    return outs[0], outs[1], list(outs[2:2 + n]), list(outs[2 + n:2 + 2 * n]), outs[-1]


def chips_wait(send, recv, qs, lands, after, *, name):
    n = len(qs)

    def body(*refs):
        q, land = refs[:n], refs[n:2 * n]
        send, recv = refs[2 * n], refs[2 * n + 1]
        x, y, c, others = _place()
        for t in range(n):
            for j, (px, py) in enumerate(others):
                cp = _chips_copy(q, land, t, j, px, py, c, send, recv)
                cp.wait_send()
                cp.wait_recv()

    outs = pl.pallas_call(
        body, name=name,
        in_specs=[HBM_SPEC] * (2 * n) + [SEM_SPEC, SEM_SPEC, pl.BlockSpec(memory_space=pl.ANY)],
        out_specs=tuple([HBM_SPEC] * (2 * n)),
        out_shape=tuple(pltpu.HBM(a.shape, a.dtype) for a in qs + lands),
        input_output_aliases={t: t for t in range(2 * n)},
        compiler_params=pltpu.CompilerParams(has_side_effects=DATAFLOW),
    )(*qs, *lands, send, recv, after)
    return list(outs[:n]), list(outs[n:])


def join_halves(fulls, li, *, name):
    n = len(fulls)

    def body(*refs):
        g = refs[n:2 * n]
        send, recv = refs[2 * n:]
        x, y, c, _ = _place()
        cps = []
        for t in range(n):
            hr = g[t].shape[1] // 2
            rows = g[t].at[li, pl.ds(c * hr, hr)]
            cp = pltpu.make_async_remote_copy(rows, rows, send.at[t], recv.at[t], device_id=(x, y, 1 - c), device_id_type=MESH)
            cp.start()
            cps.append(cp)
        for t in range(n):
            hr = g[t].shape[1] // 2
            rows = g[t].at[li, pl.ds((1 - c) * hr, hr)]
            pltpu.make_async_remote_copy(rows, rows, send.at[t], recv.at[t],
                                         device_id=(x, y, 1 - c), device_id_type=MESH).wait_recv()
        for cp in cps:
            cp.wait_send()

    return pl.pallas_call(
        body, name=name,
        in_specs=_any_specs(n), out_specs=_any_specs(n),
        out_shape=[jax.ShapeDtypeStruct(g.shape, g.dtype) for g in fulls],
        input_output_aliases={t: t for t in range(n)},
        scratch_shapes=[pltpu.SemaphoreType.DMA((n,)), pltpu.SemaphoreType.DMA((n,))],
        compiler_params=pltpu.CompilerParams(has_side_effects=True),
    )(*fulls)


def allreduce_small(t, *, name):
    R, C = t.shape

    def body(t_ref, o_ref, land, send, recv):
        x, y, c, _ = _place()
        me = 4 * x + 2 * y + c
        land[me] = t_ref[...]
        cps = []
        for j in range(1, 8):
            px, py, pc = (x + (j >> 2)) % 2, (y + ((j >> 1) & 1)) % 2, (c + (j & 1)) % 2
            cp = pltpu.make_async_remote_copy(t_ref, land.at[me], send.at[j - 1], recv.at[j - 1],
                                              device_id=(px, py, pc), device_id_type=MESH)
            cp.start()
            cps.append(cp)
        for j in range(1, 8):
            px, py, pc = (x + (j >> 2)) % 2, (y + ((j >> 1) & 1)) % 2, (c + (j & 1)) % 2
            pltpu.make_async_remote_copy(t_ref, land.at[4 * px + 2 * py + pc], send.at[j - 1], recv.at[j - 1],
                                         device_id=(px, py, pc), device_id_type=MESH).wait_recv()
        for cp in cps:
            cp.wait_send()
        acc = land[0]
        for k in range(1, 8):
            acc = acc + land[k]
        o_ref[...] = acc

    return pl.pallas_call(
        body, name=name,
        in_specs=[pl.BlockSpec(memory_space=pltpu.VMEM)], out_specs=pl.BlockSpec(memory_space=pltpu.VMEM),
        out_shape=jax.ShapeDtypeStruct((R, C), F32),
        scratch_shapes=[pltpu.VMEM((8, R, C), F32), pltpu.SemaphoreType.DMA((7,)), pltpu.SemaphoreType.DMA((7,))],
        compiler_params=pltpu.CompilerParams(has_side_effects=True),
    )(t)


def _layer_fwd(x, xb, memb, w_in, rest, P, tabs, alpha, li):
    ca, sa, cb, sb = tabs
    nA = P["gn_a"].shape[1] // HEAD_DIM_A
    nQ = P["gn_b"].shape[1] // LANES
    nm = lambda s: f"L{li}_{s}"
    h = mm_nn(xb, w_in, name=nm("h"), out_dtype=F32, tn=2304)
    ya, lse_a = attn_a_fwd(h, ca, sa, n_heads=nA, name=nm("attn_a"))
    yb, lse_b = attn_b_fwd(h, cb, sb, P["sinks_l"], off_q=3 * nA, n_qtiles=nQ, name=nm("attn_b"))
    ymix = rms_fwd(ya, yb, P["gn_a"], P["gn_b"], name=nm("rms"))
    W, P = rest(ymix, P)
    z1, x1, x1b = mm_ln(ymix, W["w_out"][0], x, P["ln_mix_g"], P["ln_mix_b"], name=nm("out_ln"), alpha=alpha,
                        tm=256, tk=ymix.shape[1])
    qm = mm_nn(x1b, W["w_mq"], name=nm("mq"), out_dtype=BF16)
    kv = mm_nn(memb, W["w_mkv"], name=nm("mkv"), out_dtype=BF16, tm=256)
    o = mem_attn_fwd(qm, kv, name=nm("mem_attn"))
    z2, x2, x2b = mm_ln(o, W["w_mo"][0], x1, P["ln_mem_g"], P["ln_mem_b"], name=nm("mo_ln"), alpha=alpha,
                        tm=256, tk=o.shape[1])
    u, a = mm_nn(x2b, W["w_up"], name=nm("up"), out_dtype=BF16, relu2=True)
    z3, x3, x3b = mm_ln(a, W["w_down"][0], x2, P["ln_ff_g"], P["ln_ff_b"], name=nm("down_ln"), alpha=alpha)
    saved = dict(xb=xb, h=h, ya=ya, lse_a=lse_a, yb=yb, lse_b=lse_b, ymix=ymix, z1=z1, x1b=x1b, qm=qm, kv=kv, o=o,
                 z2=z2, x2b=x2b, u=u, a=a, z3=z3)
    return x3, x3b, saved


def _layer_bwd(dx3, sv, memb, W, P, tabs, alpha, li, hook=None):
    ca, sa, cb, sb = tabs
    nA = P["gn_a"].shape[1] // HEAD_DIM_A
    nQ = P["gn_b"].shape[1] // LANES
    nm = lambda s: f"L{li}_b_{s}"
    nsh = lambda k: W[k].shape[0]
    gw, gs = {}, {}
    dz3, dz3b, gs["ln_ff_g"], gs["ln_ff_b"] = ln_bwd(dx3, sv["z3"], P["ln_ff_g"], name=nm("ln_ff"))
    gw["w_down"] = mm_tn(sv["a"], dz3b, nsh("w_down"), name=nm("dw_down"))
    du = mm_nt(dz3b, W["w_down"], name=nm("du"), out_dtype=BF16, umul=sv["u"])
    gw["w_up"] = mm_tn(sv["x2b"], du, nsh("w_up"), name=nm("dw_up"))
    dx2 = mm_nt(du, W["w_up"], name=nm("dx2"), out_dtype=F32, resid=dz3, alpha=alpha)
    dz2, dz2b, gs["ln_mem_g"], gs["ln_mem_b"] = ln_bwd(dx2, sv["z2"], P["ln_mem_g"], name=nm("ln_mem"))
    gw["w_mo"] = mm_tn(sv["o"], dz2b, nsh("w_mo"), name=nm("dw_mo"))
    do = mm_nt(dz2b, W["w_mo"], name=nm("do"), out_dtype=BF16)
    dqm, dkv = mem_attn_bwd(sv["qm"], sv["kv"], do, name=nm("mem_attn"))
    gw["w_mq"] = mm_tn(sv["x1b"], dqm, nsh("w_mq"), name=nm("dw_mq"))
    gw["w_mkv"] = mm_tn(memb, cast_bf16(dkv, name=nm("dkv_cast")), nsh("w_mkv"), name=nm("dw_mkv"), tm=256)
    dx1 = mm_nt(dqm, W["w_mq"], name=nm("dx1"), out_dtype=F32, resid=dz2, alpha=alpha)
    if hook is not None:
        P = hook(gw, dx1, P)
    dz1, dz1b, gs["ln_mix_g"], gs["ln_mix_b"] = ln_bwd(dx1, sv["z1"], P["ln_mix_g"], name=nm("ln_mix"))
    gw["w_out"] = mm_tn(sv["ymix"], dz1b, nsh("w_out"), name=nm("dw_out"))
    dymix = mm_nt(dz1b, W["w_out"], name=nm("dymix"), out_dtype=F32)
    dya, dyb, gs["gn_a"], gs["gn_b"] = rms_bwd(dymix, sv["ya"], sv["yb"], P["gn_a"], P["gn_b"], name=nm("rms"))
    dh = attn_a_bwd(sv["h"], ca, sa, sv["ya"], sv["lse_a"], dya, n_heads=nA, name=nm("attn_a"))
    dh, gs["sinks"] = attn_b_bwd(sv["h"], cb, sb, P["sinks_l"], sv["yb"], sv["lse_b"], dyb, dh,
                                 off_q=3 * nA, n_qtiles=nQ, name=nm("attn_b"))
    gw["w_in"] = mm_tn(sv["xb"], dh, nsh("w_in"), name=nm("dw_in"), tn=2304)
    dx0 = mm_nt(dh, W["w_in"], name=nm("dx0"), out_dtype=F32, resid=dz1, alpha=alpha, tr=2304)
    return dx0, gw, gs


def _gathered_view(name, g):
    if name == "w_in":
        return jnp.concatenate([g[k] for k in range(N_CHIPS)], axis=1)[None]
    if name in COL_SHARDED:
        return g
    return g.reshape(1, g.shape[0] * g.shape[1], g.shape[2])


def _to_shards(name, gw):
    if name == "w_in":
        n = gw.shape[2] // N_CHIPS
        return jnp.stack([gw[0, :, k * n:(k + 1) * n] for k in range(N_CHIPS)])
    if name in COL_SHARDED:
        return gw
    return gw.reshape(N_CHIPS, gw.shape[1] // N_CHIPS, gw.shape[2])


def _step(x, mem, positions, loss_target, w, m, v):
    S, D = x.shape[1], x.shape[2]
    depth = w["w_in"].shape[0]
    alpha = (2 * depth) ** 0.25
    x0 = x[0]
    memb = cast_bf16(mem[0], name="mem_cast")
    pos = positions[0]
    tabs = rope_tables(pos, HEAD_DIM_A // 4, HEAD_DIM_A) + rope_tables(pos, HEAD_DIM_B // 4, HEAD_DIM_B)
    place = mesh_place()

    def small(li):
        P = {k: w[k][li][None] for k in ("gn_a", "gn_b", "ln_mix_g", "ln_mix_b", "ln_mem_g", "ln_mem_b", "ln_ff_g", "ln_ff_b")}
        P["sinks_l"] = jnp.broadcast_to(w["sinks"][li][:, None], (w["sinks"].shape[1], LANES))
        return P

    rest_names = tuple(k for k in BIG if k != "w_in")
    chain = [(0, ("w_in",)), (0, rest_names)] + [(li, BIG) for li in range(1, depth)]
    casts = [[cast_into_slot(w[k], li, place, name=f"L{li}_cast_{k}") for k in names] for li, names in chain]
    started = {0: gather_start(casts[0], name="G0_gather_start")}

    def land(gi, after):
        send, recv, gs, tok0 = started.pop(gi)
        gs = gather_wait(send, recv, gs, tok0 if after is None else after, name=f"G{gi}_gather_wait")
        token = None
        if gi + 1 < len(chain):
            started[gi + 1] = gather_start(casts[gi + 1], name=f"G{gi + 1}_gather_start")
            token = started[gi + 1][3]
        gs = gather_forward(gs, name=f"G{gi}_gather_fwd")
        return dict(zip(chain[gi][1], gs)), token

    def ordered(a, token):
        return a if token is None else a + token[:1, :1].astype(a.dtype)

    xs, xbs, saved, Ws = x0, cast_bf16(x0, name="x_cast"), [], []
    for li in range(depth):
        gi = 0 if li == 0 else li + 1
        got, token = land(gi, None if li == 0 else xs)
        W = {"w_in": _gathered_view("w_in", got["w_in"])}
        tabs_l = (ordered(tabs[0], token),) + tabs[1:]

        def rest(after, P, li=li, got=got, W=W):
            if li == 0:
                got, token = land(1, after)
                P = dict(P, ln_mix_g=ordered(P["ln_mix_g"], token))
            W.update({k: _gathered_view(k, got[k]) for k in rest_names})
            return W, P

        xs, xbs, sv = _layer_fwd(xs, xbs, memb, W["w_in"], rest, small(li), tabs_l, alpha, li)
        saved.append(sv)
        Ws.append(W)
    dy, loss_part = loss_head(xs, loss_target[0], name="loss")
    loss = lax.psum(0.5 / D * jnp.sum(loss_part), ("x", "y", "c"))

    g_big = {k: lax.empty(w[k].shape, F32) for k in BIG}
    g_small = [None] * depth

    def begin(li, names, gw, tag):
        parts = [_to_shards(k, gw[k]) for k in names]
        r1 = sibling_halves(parts, name=f"L{li}{tag}_rs_sibling")
        qs = [pair_sum(p, r, place, name=f"L{li}_rs_pair_{k}") for k, p, r in zip(names, parts, r1)]
        send, recv, qs, lands, token = chips_start(qs, name=f"L{li}{tag}_rs_chips_start")
        return (li, names, tag, send, recv, qs, lands), token

    def finish(pending, after):
        li, names, tag, send, recv, qs, lands = pending
        qs, lands = chips_wait(send, recv, qs, lands, after, name=f"L{li}{tag}_rs_chips_wait")
        fulls = [chip_sum(q, r, place, g_big[k], li, name=f"L{li}_rs_sum_{k}") for k, q, r in zip(names, qs, lands)]
        for k, f in zip(names, join_halves(fulls, li, name=f"L{li}{tag}_rs_join")):
            g_big[k] = f

    early = ("w_mq", "w_mkv", "w_mo", "w_up", "w_down")
    late = tuple(k for k in BIG if k not in early)
    pendings, token = [], None
    for li in reversed(range(depth)):
        P = small(li)
        P["ln_ff_g"] = ordered(P["ln_ff_g"], token)
        hook = None
        if li == 0:
            def hook(gw, dx1, P):
                while pendings:
                    finish(pendings.pop(), dx1)
                pend, tok = begin(0, early, gw, "a")
                pendings.append(pend)
                return dict(P, ln_mix_g=ordered(P["ln_mix_g"], tok))
        dy, gw, g_small[li] = _layer_bwd(dy, saved[li], memb, Ws[li], P, tabs, alpha, li, hook)
        while pendings:
            finish(pendings.pop(), dy)
        pend, token = begin(li, late if li == 0 else BIG, gw, "b" if li == 0 else "")
        pendings.append(pend)
    finish(pendings.pop(), token)
    grad_x = dy[None]

    rows = []
    for li in range(depth):
        gs = g_small[li]
        for k in ("ln_mix_g", "ln_mix_b", "ln_mem_g", "ln_mem_b", "ln_ff_g", "ln_ff_b"):
            rows.append(jnp.sum(gs[k], axis=0, keepdims=True))
        rows.append(jnp.concatenate([jnp.sum(gs["gn_a"], axis=0, keepdims=True), jnp.sum(gs["gn_b"], axis=0, keepdims=True)], axis=1))
        sk = gs["sinks"][:, 0][None]
        rows.append(jnp.pad(sk, ((0, 0), (0, D - sk.shape[1]))))
    red = allreduce_small(jnp.concatenate(rows, axis=0), name="small_allreduce").reshape(depth, 8, D)
    wa = w["gn_a"].shape[1]
    grads = dict(g_big)
    for j, k in enumerate(("ln_mix_g", "ln_mix_b", "ln_mem_g", "ln_mem_b", "ln_ff_g", "ln_ff_b")):
        grads[k] = red[:, j]
    grads["gn_a"] = red[:, 6, :wa]
    grads["gn_b"] = red[:, 6, wa:]
    grads["sinks"] = red[:, 7, :w["sinks"].shape[1]]

    delta, new_m, new_v = {}, {}, {}
    small_names = [k for k in w if k not in BIG]
    for k in BIG:
        delta[k], new_m[k], new_v[k], grads[k] = adamw(w[k], grads[k], m[k], v[k], name=f"adamw_{k}", emit_g=True)
    pack = lambda d: jnp.concatenate([jnp.pad(d[k], ((0, 0), (0, D - d[k].shape[1]))) for k in small_names], axis=0)
    ds, ms, vs = adamw(pack(w), pack(grads), pack(m), pack(v), name="adamw_small")
    for j, k in enumerate(small_names):
        sl = (slice(j * depth, (j + 1) * depth), slice(0, w[k].shape[1]))
        delta[k], new_m[k], new_v[k] = ds[sl], ms[sl], vs[sl]
    return loss, grad_x, grads, delta, new_m, new_v


WEIGHTS = ("w_in", "gn_a", "gn_b", "sinks", "w_out", "ln_mix_g", "ln_mix_b", "w_mq", "w_mkv", "w_mo",
           "ln_mem_g", "ln_mem_b", "w_up", "w_down", "ln_ff_g", "ln_ff_b")


def kernel(x, mem, positions, w_in, gn_a, gn_b, sinks, w_out, ln_mix_g, ln_mix_b, w_mq, w_mkv, w_mo, ln_mem_g, ln_mem_b, w_up, w_down, ln_ff_g, ln_ff_b, loss_target, m_w_in, m_gn_a, m_gn_b, m_sinks, m_w_out, m_ln_mix_g, m_ln_mix_b, m_w_mq, m_w_mkv, m_w_mo, m_ln_mem_g, m_ln_mem_b, m_w_up, m_w_down, m_ln_ff_g, m_ln_ff_b, v_w_in, v_gn_a, v_gn_b, v_sinks, v_w_out, v_ln_mix_g, v_ln_mix_b, v_w_mq, v_w_mkv, v_w_mo, v_ln_mem_g, v_ln_mem_b, v_w_up, v_w_down, v_ln_ff_g, v_ln_ff_b):
    w = dict(zip(WEIGHTS, (w_in, gn_a, gn_b, sinks, w_out, ln_mix_g, ln_mix_b, w_mq, w_mkv, w_mo, ln_mem_g, ln_mem_b, w_up, w_down, ln_ff_g, ln_ff_b)))
    m = dict(zip(WEIGHTS, (m_w_in, m_gn_a, m_gn_b, m_sinks, m_w_out, m_ln_mix_g, m_ln_mix_b, m_w_mq, m_w_mkv, m_w_mo, m_ln_mem_g, m_ln_mem_b, m_w_up, m_w_down, m_ln_ff_g, m_ln_ff_b)))
    v = dict(zip(WEIGHTS, (v_w_in, v_gn_a, v_gn_b, v_sinks, v_w_out, v_ln_mix_g, v_ln_mix_b, v_w_mq, v_w_mkv, v_w_mo, v_ln_mem_g, v_ln_mem_b, v_w_up, v_w_down, v_ln_ff_g, v_ln_ff_b)))
    loss, grad_x, grads, delta, new_m, new_v = _step(x, mem, positions, loss_target, w, m, v)
    return (loss, grad_x, *[grads[k] for k in WEIGHTS], *[delta[k] for k in WEIGHTS],
            *[new_m[k] for k in WEIGHTS], *[new_v[k] for k in WEIGHTS])
```

```python
import functools

import jax
import jax.numpy as jnp
from jax import lax
from jax.experimental import pallas as pl
from jax.experimental.pallas import tpu as pltpu

F32 = jnp.float32
BF16 = jnp.bfloat16
MESH = pl.DeviceIdType.MESH

HEAD_DIM_A = 128
HEAD_DIM_B = 64
LANES = 128
BLOCK = 128
DILATED_BRANCHES = ((128, 1), (512, 4), (2048, 16))
WINDOW_B = 128
N_MEM_HEADS = 4
ROPE_THETA = 500000.0
LN_EPS = 1e-5
RMS_EPS = 1e-6
NEG_INF = -1e30
ADAM_LR = 0.001
ADAM_B1 = 0.9
ADAM_B2 = 0.999
ADAM_EPS = 1e-08
ADAM_WD = 0.01
ADAM_STEP = 10
N_CHIPS = 4
VMEM_LIMIT = 56 * 1024 * 1024

BIG = ("w_in", "w_out", "w_mq", "w_mkv", "w_mo", "w_up", "w_down")
COL_SHARDED = ("w_in", "w_mkv", "w_up")


def _tile(n, target, mult=LANES):
    best = None
    t = mult
    while t <= min(n, target):
        if n % t == 0:
            best = t
        t += mult
    return best if best is not None else n


def _params(sem=None):
    return pltpu.CompilerParams(dimension_semantics=sem, vmem_limit_bytes=VMEM_LIMIT)


def _dot(a, b):
    return jnp.dot(a, b, preferred_element_type=F32)


def _dot_nt(a, b):
    return lax.dot_general(a, b, (((1,), (1,)), ((), ())), preferred_element_type=F32)


def _dot_tn(a, b):
    return lax.dot_general(a, b, (((0,), (0,)), ((), ())), preferred_element_type=F32)


def mm_nn(a, b3, *, name, out_dtype, relu2=False, tm=1024, tn=1024, tk=2048):
    M, K = a.shape
    nsh, _, nk = b3.shape
    tm, tn, tk = _tile(M, tm, 8), _tile(nk, tn), _tile(K, tk)
    nb, ksteps = nk // tn, K // tk

    def body(a_ref, b_ref, *rest):
        outs, scr = rest[:2 if relu2 else 1], rest[2 if relu2 else 1:]

        def finish(acc):
            if relu2:
                outs[0][...] = acc.astype(outs[0].dtype)
                r = jnp.maximum(acc, 0.0)
                outs[1][...] = (r * r).astype(outs[1].dtype)
            else:
                outs[0][...] = acc.astype(outs[0].dtype)

        if ksteps == 1:
            finish(_dot(a_ref[...], b_ref[...]))
        else:
            acc_ref = scr[0]
            k = pl.program_id(2)

            @pl.when(k == 0)
            def _():
                acc_ref[...] = jnp.zeros_like(acc_ref)

            acc_ref[...] += _dot(a_ref[...], b_ref[...])

            @pl.when(k == ksteps - 1)
            def _():
                finish(acc_ref[...])

    o_spec = pl.BlockSpec((tm, tn), lambda i, j, k: (i, j))
    o_shape = jax.ShapeDtypeStruct((M, nsh * nk), out_dtype)
    return pl.pallas_call(
        body, name=name,
        grid=(M // tm, nsh * nb, ksteps),
        in_specs=[pl.BlockSpec((tm, tk), lambda i, j, k: (i, k)),
                  pl.BlockSpec((None, tk, tn), lambda i, j, k: (j // nb, k, j % nb))],
        out_specs=(o_spec, o_spec) if relu2 else o_spec,
        out_shape=(o_shape, o_shape) if relu2 else o_shape,
        scratch_shapes=[] if ksteps == 1 else [pltpu.VMEM((tm, tn), F32)],
        compiler_params=_params(("parallel", "parallel", "arbitrary")),
    )(a, b3)


def mm_ln(a, w, resid, g, b, *, name, alpha, tm=512, tk=1024):
    M, K = a.shape
    D = w.shape[1]
    tm, tk = _tile(M, tm, 8), _tile(K, tk)
    ksteps = K // tk

    def body(a_ref, w_ref, r_ref, g_ref, b_ref, z_ref, xn_ref, xb_ref, *scr):
        def finish(acc):
            z = alpha * r_ref[...] + acc
            mu = jnp.mean(z, axis=-1, keepdims=True)
            zc = z - mu
            var = jnp.mean(zc * zc, axis=-1, keepdims=True)
            xn = zc * lax.rsqrt(var + LN_EPS) * g_ref[...] + b_ref[...]
            z_ref[...] = z
            xn_ref[...] = xn
            xb_ref[...] = xn.astype(BF16)

        if ksteps == 1:
            finish(_dot(a_ref[...], w_ref[...]))
            return
        acc_ref = scr[0]
        k = pl.program_id(1)

        @pl.when(k == 0)
        def _():
            acc_ref[...] = jnp.zeros_like(acc_ref)

        acc_ref[...] += _dot(a_ref[...], w_ref[...])

        @pl.when(k == ksteps - 1)
        def _():
            finish(acc_ref[...])

    row = pl.BlockSpec((tm, D), lambda i, k: (i, 0))
    vec = pl.BlockSpec((1, D), lambda i, k: (0, 0))
    return pl.pallas_call(
        body, name=name,
        grid=(M // tm, ksteps),
        in_specs=[pl.BlockSpec((tm, tk), lambda i, k: (i, k)),
                  pl.BlockSpec((tk, D), lambda i, k: (k, 0)), row, vec, vec],
        out_specs=(row, row, row),
        out_shape=(jax.ShapeDtypeStruct((M, D), F32), jax.ShapeDtypeStruct((M, D), F32),
                   jax.ShapeDtypeStruct((M, D), BF16)),
        scratch_shapes=[] if ksteps == 1 else [pltpu.VMEM((tm, D), F32)],
        compiler_params=_params(("parallel", "arbitrary")),
    )(a, w, resid, g, b)


def mm_nt(a, b3, *, name, out_dtype, resid=None, alpha=1.0, umul=None, tm=1024, tko=1024, tr=2048):
    M, N = a.shape
    nsh, K, nk = b3.shape
    tm, tko, tr = _tile(M, tm, 8), _tile(K, tko), _tile(nk, tr)
    nb = nk // tr
    rsteps = nsh * nb

    def body(a_ref, b_ref, *rest):
        rest = list(rest)
        r_ref = rest.pop(0) if resid is not None else None
        u_ref = rest.pop(0) if umul is not None else None
        o_ref = rest.pop(0)

        def finish(acc):
            if r_ref is not None:
                acc = acc + alpha * r_ref[...]
            if u_ref is not None:
                acc = acc * (2.0 * jnp.maximum(u_ref[...].astype(F32), 0.0))
            o_ref[...] = acc.astype(o_ref.dtype)

        if rsteps == 1:
            finish(_dot_nt(a_ref[...], b_ref[...]))
        else:
            acc_ref = rest[0]
            r = pl.program_id(2)

            @pl.when(r == 0)
            def _():
                acc_ref[...] = jnp.zeros_like(acc_ref)

            acc_ref[...] += _dot_nt(a_ref[...], b_ref[...])

            @pl.when(r == rsteps - 1)
            def _():
                finish(acc_ref[...])

    o_spec = pl.BlockSpec((tm, tko), lambda i, j, r: (i, j))
    in_specs = [pl.BlockSpec((tm, tr), lambda i, j, r: (i, r)),
                pl.BlockSpec((None, tko, tr), lambda i, j, r: (r // nb, j, r % nb))]
    args = [a, b3]
    for extra in (resid, umul):
        if extra is not None:
            in_specs.append(o_spec)
            args.append(extra)
    return pl.pallas_call(
        body, name=name,
        grid=(M // tm, K // tko, rsteps),
        in_specs=in_specs, out_specs=o_spec,
        out_shape=jax.ShapeDtypeStruct((M, K), out_dtype),
        scratch_shapes=[] if rsteps == 1 else [pltpu.VMEM((tm, tko), F32)],
        compiler_params=_params(("parallel", "parallel", "arbitrary")),
    )(*args)


def mm_tn(a, g, nsh, *, name, tk=1024, tn=1024, tm=2048):
    M, K = a.shape
    N = g.shape[1]
    nk = N // nsh
    tk, tn, tm = _tile(K, tk), _tile(nk, tn), _tile(M, tm, 8)
    nb, msteps = nk // tn, M // tm

    def body(a_ref, g_ref, o_ref, acc_ref):
        m = pl.program_id(2)

        @pl.when(m == 0)
        def _():
            acc_ref[...] = jnp.zeros_like(acc_ref)

        acc_ref[...] += _dot_tn(a_ref[...], g_ref[...])

        @pl.when(m == msteps - 1)
        def _():
            o_ref[...] = acc_ref[...].astype(o_ref.dtype)

    return pl.pallas_call(
        body, name=name,
        grid=(K // tk, nsh * nb, msteps),
        in_specs=[pl.BlockSpec((tm, tk), lambda i, j, m: (m, i)),
                  pl.BlockSpec((tm, tn), lambda i, j, m: (m, j))],
        out_specs=pl.BlockSpec((None, tk, tn), lambda i, j, m: (j // nb, i, j % nb)),
        out_shape=jax.ShapeDtypeStruct((nsh, K, nk), BF16),
        scratch_shapes=[pltpu.VMEM((tk, tn), F32)],
        compiler_params=_params(("parallel", "parallel", "arbitrary")),
    )(a, g)


def _fold8(t):
    return t.reshape(t.shape[0] // 8, 8, t.shape[1]).sum(axis=0)


def ln_bwd(dy, z, g, *, name, tm=256):
    M, D = z.shape
    tm = _tile(M, tm, 8)

    def body(dy_ref, z_ref, g_ref, dz_ref, dzb_ref, dg_ref, db_ref):
        i = pl.program_id(0)
        z_ = z_ref[...]
        dy_ = dy_ref[...]
        mu = jnp.mean(z_, axis=-1, keepdims=True)
        zc = z_ - mu
        var = jnp.mean(zc * zc, axis=-1, keepdims=True)
        rstd = lax.rsqrt(var + LN_EPS)
        xh = zc * rstd
        dxh = dy_ * g_ref[...]
        m1 = jnp.mean(dxh, axis=-1, keepdims=True)
        m2 = jnp.mean(dxh * xh, axis=-1, keepdims=True)
        dz = rstd * (dxh - m1 - xh * m2)
        dz_ref[...] = dz
        dzb_ref[...] = dz.astype(BF16)

        @pl.when(i == 0)
        def _():
            dg_ref[...] = jnp.zeros_like(dg_ref)
            db_ref[...] = jnp.zeros_like(db_ref)

        dg_ref[...] += _fold8(dy_ * xh)
        db_ref[...] += _fold8(dy_)

    row = pl.BlockSpec((tm, D), lambda i: (i, 0))
    acc = pl.BlockSpec((8, D), lambda i: (0, 0))
    return pl.pallas_call(
        body, name=name, grid=(M // tm,),
        in_specs=[row, row, pl.BlockSpec((1, D), lambda i: (0, 0))],
        out_specs=(row, row, acc, acc),
        out_shape=(jax.ShapeDtypeStruct((M, D), F32), jax.ShapeDtypeStruct((M, D), BF16),
                   jax.ShapeDtypeStruct((8, D), F32), jax.ShapeDtypeStruct((8, D), F32)),
        compiler_params=_params(("arbitrary",)),
    )(dy, z, g)


def rms_fwd(ya, yb, ga, gb, *, name, tm=512):
    M, WA = ya.shape
    WB = yb.shape[1]
    tm = _tile(M, tm, 8)

    def body(ya_ref, yb_ref, ga_ref, gb_ref, o_ref):
        for y_ref, g_ref, lo, w in ((ya_ref, ga_ref, 0, WA), (yb_ref, gb_ref, WA, WB)):
            y = y_ref[...]
            r = lax.rsqrt(jnp.mean(y * y, axis=-1, keepdims=True) + RMS_EPS)
            o_ref[:, lo:lo + w] = (y * r * g_ref[...]).astype(o_ref.dtype)

    return pl.pallas_call(
        body, name=name, grid=(M // tm,),
        in_specs=[pl.BlockSpec((tm, WA), lambda i: (i, 0)), pl.BlockSpec((tm, WB), lambda i: (i, 0)),
                  pl.BlockSpec((1, WA), lambda i: (0, 0)), pl.BlockSpec((1, WB), lambda i: (0, 0))],
        out_specs=pl.BlockSpec((tm, WA + WB), lambda i: (i, 0)),
        out_shape=jax.ShapeDtypeStruct((M, WA + WB), BF16),
        compiler_params=_params(("parallel",)),
    )(ya, yb, ga, gb)


def rms_bwd(dy, ya, yb, ga, gb, *, name, tm=512):
    M, WA = ya.shape
    WB = yb.shape[1]
    tm = _tile(M, tm, 8)

    def body(dy_ref, ya_ref, yb_ref, ga_ref, gb_ref, dya_ref, dyb_ref, dga_ref, dgb_ref):
        i = pl.program_id(0)

        @pl.when(i == 0)
        def _():
            dga_ref[...] = jnp.zeros_like(dga_ref)
            dgb_ref[...] = jnp.zeros_like(dgb_ref)

        for y_ref, g_ref, d_ref, dgr, lo, w in ((ya_ref, ga_ref, dya_ref, dga_ref, 0, WA),
                                                (yb_ref, gb_ref, dyb_ref, dgb_ref, WA, WB)):
            y = y_ref[...]
            d = dy_ref[:, lo:lo + w]
            r = lax.rsqrt(jnp.mean(y * y, axis=-1, keepdims=True) + RMS_EPS)
            n = y * r
            dn = d * g_ref[...]
            d_ref[...] = r * (dn - n * jnp.mean(dn * n, axis=-1, keepdims=True))
            dgr[...] += _fold8(d * n)

    return pl.pallas_call(
        body, name=name, grid=(M // tm,),
        in_specs=[pl.BlockSpec((tm, WA + WB), lambda i: (i, 0)),
                  pl.BlockSpec((tm, WA), lambda i: (i, 0)), pl.BlockSpec((tm, WB), lambda i: (i, 0)),
                  pl.BlockSpec((1, WA), lambda i: (0, 0)), pl.BlockSpec((1, WB), lambda i: (0, 0))],
        out_specs=(pl.BlockSpec((tm, WA), lambda i: (i, 0)), pl.BlockSpec((tm, WB), lambda i: (i, 0)),
                   pl.BlockSpec((8, WA), lambda i: (0, 0)), pl.BlockSpec((8, WB), lambda i: (0, 0))),
        out_shape=(jax.ShapeDtypeStruct((M, WA), F32), jax.ShapeDtypeStruct((M, WB), F32),
                   jax.ShapeDtypeStruct((8, WA), F32), jax.ShapeDtypeStruct((8, WB), F32)),
        compiler_params=_params(("arbitrary",)),
    )(dy, ya, yb, ga, gb)


def loss_head(y, target, *, name, tm=512):
    M, D = y.shape
    tm = _tile(M, tm, 8)

    def body(y_ref, t_ref, dy_ref, l_ref):
        i = pl.program_id(0)

        @pl.when(i == 0)
        def _():
            l_ref[...] = jnp.zeros_like(l_ref)

        e = y_ref[...] - t_ref[...]
        dy_ref[...] = e * (1.0 / D)
        l_ref[...] += _fold8(e * e)

    row = pl.BlockSpec((tm, D), lambda i: (i, 0))
    return pl.pallas_call(
        body, name=name, grid=(M // tm,),
        in_specs=[row, row],
        out_specs=(row, pl.BlockSpec((8, D), lambda i: (0, 0))),
        out_shape=(jax.ShapeDtypeStruct((M, D), F32), jax.ShapeDtypeStruct((8, D), F32)),
        compiler_params=_params(("arbitrary",)),
    )(y, target)


def _lane(shape):
    return lax.broadcasted_iota(jnp.int32, shape, len(shape) - 1)


def _swap(t, half, period):
    first = (_lane(t.shape) % period) < half
    return jnp.where(first, pltpu.roll(t, LANES - half, 1), pltpu.roll(t, half, 1))


def _rope(t, c, s, half, period):
    return t * c + _swap(t, half, period) * s


def _rope_t(g, c, s, half, period):
    return g * c - _swap(g, half, period) * s


def rope_tables(positions, rot_dim, period):
    half = rot_dim // 2
    inv_freq = ROPE_THETA ** (-jnp.arange(0, rot_dim, 2, dtype=F32) / rot_dim)
    ang = positions.astype(F32)[:, None] * inv_freq
    cos, sin = jnp.cos(ang), jnp.sin(ang)
    ones = jnp.ones((positions.shape[0], period - rot_dim), F32)
    c = jnp.concatenate([cos, cos, ones], axis=1)
    s = jnp.concatenate([-sin, sin, 0.0 * ones], axis=1)
    reps = LANES // period
    return jnp.tile(c, (1, reps)), jnp.tile(s, (1, reps))


def _branch_blocks(S):
    out = []
    for window, d in DILATED_BRANCHES:
        assert window // d == BLOCK and S % (d * BLOCK) == 0
        out.append((d, (S // d) // BLOCK))
    return out


def _rows(r, n, d):
    return pl.ds(r + n * (BLOCK * d), BLOCK, stride=d) if d > 1 else pl.ds(pl.multiple_of(n * BLOCK, BLOCK), BLOCK)


def _band_masks(n, strict_prev):
    qi = lax.broadcasted_iota(jnp.int32, (BLOCK, BLOCK), 0)
    kj = lax.broadcasted_iota(jnp.int32, (BLOCK, BLOCK), 1)
    cur = kj <= qi
    prev = ((kj > qi) if strict_prev else (kj >= qi)) & (n > 0)
    return cur, prev


def attn_a_fwd(h, ca, sa, *, n_heads, name):
    S = h.shape[0]
    scale = HEAD_DIM_A ** -0.5
    half = HEAD_DIM_A // 8
    branches = _branch_blocks(S)

    def body(q_ref, k_ref, v_ref, c_ref, s_ref, y_ref, lse_ref, qs, ks, m_s, l_s, acc_s):
        qs[...] = _rope(q_ref[...], c_ref[...], s_ref[...], half, HEAD_DIM_A)
        ks[...] = _rope(k_ref[...], c_ref[...], s_ref[...], half, HEAD_DIM_A)
        for bi, (d, nb) in enumerate(branches):
            def blk(idx, carry, bi=bi, d=d, nb=nb):
                r, n = idx // nb, idx % nb
                rc, rp = _rows(r, n, d), _rows(r, jnp.maximum(n - 1, 0), d)
                q = qs[rc, :].astype(BF16)
                cur, prev = _band_masks(n, False)
                sc = jnp.where(cur, _dot_nt(q, ks[rc, :].astype(BF16)) * scale, NEG_INF)
                sp = jnp.where(prev, _dot_nt(q, ks[rp, :].astype(BF16)) * scale, NEG_INF)
                m = jnp.maximum(jnp.max(sc, axis=-1, keepdims=True), jnp.max(sp, axis=-1, keepdims=True))
                pc, pp = jnp.exp(sc - m), jnp.exp(sp - m)
                l = jnp.sum(pc, axis=-1, keepdims=True) + jnp.sum(pp, axis=-1, keepdims=True)
                acc = _dot(pc.astype(BF16), v_ref[rc, :].astype(BF16)) + _dot(pp.astype(BF16), v_ref[rp, :].astype(BF16))
                mb = jnp.broadcast_to(m, (BLOCK, LANES))
                lb = jnp.broadcast_to(l, (BLOCK, LANES))
                if bi == 0:
                    m_s[rc, :], l_s[rc, :], acc_s[rc, :] = mb, lb, acc
                else:
                    m0 = m_s[rc, :]
                    mn = jnp.maximum(m0, mb)
                    a0, a1 = jnp.exp(m0 - mn), jnp.exp(mb - mn)
                    m_s[rc, :] = mn
                    l_s[rc, :] = l_s[rc, :] * a0 + lb * a1
                    acc_s[rc, :] = acc_s[rc, :] * a0 + acc * a1
                return carry

            lax.fori_loop(0, d * nb, blk, 0)
        y_ref[...] = acc_s[...] / l_s[...]
        lse_ref[...] = m_s[...] + jnp.log(l_s[...])

    col = lambda off: pl.BlockSpec((S, LANES), lambda hd: (0, off + hd))
    full = pl.BlockSpec((S, LANES), lambda hd: (0, 0))
    out = pl.BlockSpec((S, LANES), lambda hd: (0, hd))
    o_shape = jax.ShapeDtypeStruct((S, n_heads * LANES), F32)
    return pl.pallas_call(
        body, name=name, grid=(n_heads,),
        in_specs=[col(0), col(n_heads), col(2 * n_heads), full, full],
        out_specs=(out, out), out_shape=(o_shape, o_shape),
        scratch_shapes=[pltpu.VMEM((S, LANES), F32) for _ in range(5)],
        compiler_params=_params(("arbitrary",)),
    )(h, h, h, ca, sa)


def attn_a_bwd(h, ca, sa, ya, lse, dya, *, n_heads, name):
    S = h.shape[0]
    scale = HEAD_DIM_A ** -0.5
    half = HEAD_DIM_A // 8
    branches = _branch_blocks(S)

    def body(q_ref, k_ref, v_ref, c_ref, s_ref, y_ref, lse_ref, dy_ref, o_ref, qs, ks, dq_s, dk_s, dv_s):
        part = pl.program_id(1)

        @pl.when(part == 0)
        def _():
            qs[...] = _rope(q_ref[...], c_ref[...], s_ref[...], half, HEAD_DIM_A)
            ks[...] = _rope(k_ref[...], c_ref[...], s_ref[...], half, HEAD_DIM_A)
            dq_s[...] = jnp.zeros_like(dq_s)
            dk_s[...] = jnp.zeros_like(dk_s)
            dv_s[...] = jnp.zeros_like(dv_s)
            for d, nb in branches:
                def blk(idx, carry, d=d, nb=nb):
                    r, n = idx // nb, idx % nb
                    rc, rp = _rows(r, n, d), _rows(r, jnp.maximum(n - 1, 0), d)
                    q = qs[rc, :].astype(BF16)
                    dy = dy_ref[rc, :]
                    dsum = jnp.sum(dy * y_ref[rc, :], axis=-1, keepdims=True)
                    dyb = dy.astype(BF16)
                    lse_b = lse_ref[rc, :]
                    cur, prev = _band_masks(n, False)
                    dq = jnp.zeros((BLOCK, LANES), F32)
                    for rows, mask in ((rc, cur), (rp, prev)):
                        kb = ks[rows, :].astype(BF16)
                        vb = v_ref[rows, :].astype(BF16)
                        s = jnp.where(mask, _dot_nt(q, kb) * scale, NEG_INF)
                        p = jnp.exp(s - lse_b)
                        ds = (p * (_dot_nt(dyb, vb) - dsum) * scale).astype(BF16)
                        dv_s[rows, :] += _dot_tn(p.astype(BF16), dyb)
                        dk_s[rows, :] += _dot_tn(ds, q)
                        dq = dq + _dot(ds, kb)
                    dq_s[rc, :] += dq
                    return carry

                lax.fori_loop(0, d * nb, blk, 0)
            o_ref[...] = _rope_t(dq_s[...], c_ref[...], s_ref[...], half, HEAD_DIM_A).astype(o_ref.dtype)

        @pl.when(part == 1)
        def _():
            o_ref[...] = _rope_t(dk_s[...], c_ref[...], s_ref[...], half, HEAD_DIM_A).astype(o_ref.dtype)

        @pl.when(part == 2)
        def _():
            o_ref[...] = dv_s[...].astype(o_ref.dtype)

    col = lambda off: pl.BlockSpec((S, LANES), lambda hd, p: (0, off + hd))
    full = pl.BlockSpec((S, LANES), lambda hd, p: (0, 0))
    per_head = pl.BlockSpec((S, LANES), lambda hd, p: (0, hd))
    return pl.pallas_call(
        body, name=name, grid=(n_heads, 3),
        in_specs=[col(0), col(n_heads), col(2 * n_heads), full, full, per_head, per_head, per_head],
        out_specs=pl.BlockSpec((S, LANES), lambda hd, p: (0, p * n_heads + hd)),
        out_shape=jax.ShapeDtypeStruct((S, 3 * n_heads * LANES), BF16),
        scratch_shapes=[pltpu.VMEM((S, LANES), F32) for _ in range(5)],
        compiler_params=_params(("arbitrary", "arbitrary")),
    )(h, h, h, ca, sa, ya, lse, dya)


def _both_halves(t, g):
    low = _lane(t.shape) < HEAD_DIM_B
    return jnp.where(low == (g == 0), t, pltpu.roll(t, HEAD_DIM_B, 1))


def _stack_heads(t):
    low = _lane(t.shape) < HEAD_DIM_B
    return jnp.concatenate([jnp.where(low, t, 0.0), jnp.where(low, 0.0, t)], axis=0)


def _unstack_heads(t2):
    low = _lane((BLOCK, LANES)) < HEAD_DIM_B
    return jnp.where(low, t2[:BLOCK], t2[BLOCK:])


def _fold_halves(t):
    return t + pltpu.roll(t, HEAD_DIM_B, 1)


def _head_rows(ref, tile):
    a = ref[pl.ds(2 * tile, 1), :][:, :1]
    b = ref[pl.ds(2 * tile + 1, 1), :][:, :1]
    return jnp.concatenate([jnp.broadcast_to(a, (BLOCK, 1)), jnp.broadcast_to(b, (BLOCK, 1))], axis=0)


def attn_b_fwd(h, cb, sb, sinks_l, *, off_q, n_qtiles, name):
    S = h.shape[0]
    nblk = S // BLOCK
    scale = HEAD_DIM_B ** -0.5
    half = HEAD_DIM_B // 8
    tiles_per_group = n_qtiles // 2

    def body(q_ref, k_ref, v_ref, c_ref, s_ref, sink_ref, y_ref, lse_ref, qs, kg, vg):
        t = pl.program_id(0)
        g = t // tiles_per_group
        qs[...] = _rope(q_ref[...], c_ref[...], s_ref[...], half, HEAD_DIM_B)
        kg[...] = _both_halves(_rope(k_ref[...], c_ref[...], s_ref[...], half, HEAD_DIM_B), g)
        vg[...] = _both_halves(v_ref[...], g)
        sink = _head_rows(sink_ref, t)

        def blk(n, carry):
            rc = pl.ds(pl.multiple_of(n * BLOCK, BLOCK), BLOCK)
            rp = pl.ds(pl.multiple_of(jnp.maximum(n - 1, 0) * BLOCK, BLOCK), BLOCK)
            q2 = _stack_heads(qs[rc, :]).astype(BF16)
            cur, prev = _band_masks(n, True)
            cur2, prev2 = jnp.concatenate([cur, cur], 0), jnp.concatenate([prev, prev], 0)
            sc = jnp.where(cur2, _dot_nt(q2, kg[rc, :].astype(BF16)) * scale, NEG_INF)
            sp = jnp.where(prev2, _dot_nt(q2, kg[rp, :].astype(BF16)) * scale, NEG_INF)
            m = jnp.maximum(jnp.max(sc, axis=-1, keepdims=True), jnp.max(sp, axis=-1, keepdims=True))
            pc, pp = jnp.exp(sc - m), jnp.exp(sp - m)
            l = jnp.sum(pc, axis=-1, keepdims=True) + jnp.sum(pp, axis=-1, keepdims=True)
            acc = _dot(pc.astype(BF16), vg[rc, :].astype(BF16)) + _dot(pp.astype(BF16), vg[rp, :].astype(BF16))
            m2 = jnp.maximum(m, sink)
            c = jnp.exp(m - m2)
            den = l * c + jnp.exp(sink - m2)
            y_ref[rc, :] = _unstack_heads(acc * (c / den))
            lse_ref[rc, :] = _unstack_heads(jnp.broadcast_to(m2 + jnp.log(den), (2 * BLOCK, LANES)))
            return carry

        lax.fori_loop(0, nblk, blk, 0)

    full = lambda col: pl.BlockSpec((S, LANES), lambda t: (0, col))
    out = pl.BlockSpec((S, LANES), lambda t: (0, t))
    o_shape = jax.ShapeDtypeStruct((S, n_qtiles * LANES), F32)
    return pl.pallas_call(
        body, name=name, grid=(n_qtiles,),
        in_specs=[pl.BlockSpec((S, LANES), lambda t: (0, off_q + t)), full(off_q + n_qtiles), full(off_q + n_qtiles + 1),
                  full(0), full(0), pl.BlockSpec(sinks_l.shape, lambda t: (0, 0))],
        out_specs=(out, out), out_shape=(o_shape, o_shape),
        scratch_shapes=[pltpu.VMEM((S, LANES), F32) for _ in range(3)],
        compiler_params=_params(("arbitrary",)),
    )(h, h, h, cb, sb, sinks_l)


def attn_b_bwd(h, cb, sb, sinks_l, yb, lse, dyb, *, off_q, n_qtiles, name):
    S = h.shape[0]
    nblk = S // BLOCK
    scale = HEAD_DIM_B ** -0.5
    half = HEAD_DIM_B // 8
    tiles_per_group = n_qtiles // 2
    n_steps = n_qtiles + 2

    def body(q_ref, k_ref, v_ref, c_ref, s_ref, sink_ref, y_ref, lse_ref, dy_ref,
             dq_ref, dkv_ref, dsink_ref, qs, kg, vg, dk_acc, dv_acc):
        t = pl.program_id(0)

        @pl.when(t == 0)
        def _():
            dk_acc[...] = jnp.zeros_like(dk_acc)
            dv_acc[...] = jnp.zeros_like(dv_acc)
            dsink_ref[...] = jnp.zeros_like(dsink_ref)

        @pl.when(t < n_qtiles)
        def _():
            g = t // tiles_per_group
            qs[...] = _rope(q_ref[...], c_ref[...], s_ref[...], half, HEAD_DIM_B)
            kg[...] = _both_halves(_rope(k_ref[...], c_ref[...], s_ref[...], half, HEAD_DIM_B), g)
            vg[...] = _both_halves(v_ref[...], g)
            sink = _head_rows(sink_ref, t)

            def blk(n, dsink):
                rc = pl.ds(pl.multiple_of(n * BLOCK, BLOCK), BLOCK)
                rp = pl.ds(pl.multiple_of(jnp.maximum(n - 1, 0) * BLOCK, BLOCK), BLOCK)
                q2 = _stack_heads(qs[rc, :]).astype(BF16)
                dy2 = _stack_heads(dy_ref[rc, :])
                dsum = jnp.sum(dy2 * _stack_heads(y_ref[rc, :]), axis=-1, keepdims=True)
                dy2b = dy2.astype(BF16)
                lse_t = lse_ref[rc, :]
                lse2 = jnp.concatenate([lse_t[:, :1], lse_t[:, HEAD_DIM_B:HEAD_DIM_B + 1]], axis=0)
                cur, prev = _band_masks(n, True)
                dq2 = jnp.zeros((2 * BLOCK, LANES), F32)
                for rows, mask in ((rc, cur), (rp, prev)):
                    kb = kg[rows, :].astype(BF16)
                    vb = vg[rows, :].astype(BF16)
                    mask2 = jnp.concatenate([mask, mask], 0)
                    s = jnp.where(mask2, _dot_nt(q2, kb) * scale, NEG_INF)
                    p = jnp.exp(s - lse2)
                    ds = (p * (_dot_nt(dy2b, vb) - dsum) * scale).astype(BF16)
                    dv_acc[g, rows, :] += _fold_halves(_dot_tn(p.astype(BF16), dy2b))
                    dk_acc[g, rows, :] += _fold_halves(_dot_tn(ds, q2))
                    dq2 = dq2 + _dot(ds, kb)
                dq_ref[rc, :] = _unstack_heads(dq2)
                return dsink - jnp.exp(sink - lse2) * dsum

            dsink = lax.fori_loop(0, nblk, blk, jnp.zeros((2 * BLOCK, 1), F32))
            dq_ref[...] = _rope_t(dq_ref[...], c_ref[...], s_ref[...], half, HEAD_DIM_B)
            d0 = jnp.sum(dsink[:BLOCK], axis=0, keepdims=True)
            d1 = jnp.sum(dsink[BLOCK:], axis=0, keepdims=True)
            dsink_ref[pl.ds(2 * t, 1), :] = jnp.broadcast_to(d0, (1, LANES))
            dsink_ref[pl.ds(2 * t + 1, 1), :] = jnp.broadcast_to(d1, (1, LANES))

        low = _lane((S, LANES)) < HEAD_DIM_B

        @pl.when(t == n_qtiles)
        def _():
            dk = jnp.where(low, dk_acc[0], dk_acc[1])
            dkv_ref[...] = _rope_t(dk, c_ref[...], s_ref[...], half, HEAD_DIM_B)

        @pl.when(t == n_qtiles + 1)
        def _():
            dkv_ref[...] = jnp.where(low, dv_acc[0], dv_acc[1])

    qt = lambda t: jnp.minimum(t, n_qtiles - 1)
    full = lambda col: pl.BlockSpec((S, LANES), lambda t: (0, col))
    per_tile = pl.BlockSpec((S, LANES), lambda t: (0, qt(t)))
    return pl.pallas_call(
        body, name=name, grid=(n_steps,),
        in_specs=[pl.BlockSpec((S, LANES), lambda t: (0, off_q + qt(t))), full(off_q + n_qtiles),
                  full(off_q + n_qtiles + 1), full(0), full(0), pl.BlockSpec(sinks_l.shape, lambda t: (0, 0)),
                  per_tile, per_tile, per_tile],
        out_specs=(per_tile, pl.BlockSpec((S, LANES), lambda t: (0, jnp.maximum(t - n_qtiles, 0))),
                   pl.BlockSpec(sinks_l.shape, lambda t: (0, 0))),
        out_shape=(jax.ShapeDtypeStruct((S, n_qtiles * LANES), F32), jax.ShapeDtypeStruct((S, 2 * LANES), F32),
                   jax.ShapeDtypeStruct(sinks_l.shape, F32)),
        scratch_shapes=[pltpu.VMEM((S, LANES), F32) for _ in range(3)]
        + [pltpu.VMEM((2, S, LANES), F32), pltpu.VMEM((2, S, LANES), F32)],
        compiler_params=_params(("arbitrary",)),
    )(h, h, h, cb, sb, sinks_l, yb, lse, dyb)


def _win(r, nw, d):
    if d > 1:
        return pl.ds(r + nw * (BLOCK * d), 2 * BLOCK, stride=d)
    return pl.ds(pl.multiple_of(nw * BLOCK, BLOCK), 2 * BLOCK)


def _fwd_bias(strict_prev):
    qi = lax.broadcasted_iota(jnp.int32, (BLOCK, 2 * BLOCK), 0)
    kj = lax.broadcasted_iota(jnp.int32, (BLOCK, 2 * BLOCK), 1)
    first = kj < BLOCK
    kk = jnp.where(first, kj, kj - BLOCK)
    prev_ok = (kk > qi) if strict_prev else (kk >= qi)
    zero = first & (kk <= qi)
    mid = (first & prev_ok) | (jnp.logical_not(first) & (kk <= qi))
    return jnp.stack([jnp.where(zero, 0.0, NEG_INF), jnp.where(mid, 0.0, NEG_INF)])


def _bwd_bias(strict_prev):
    qi = lax.broadcasted_iota(jnp.int32, (2 * BLOCK, BLOCK), 0)
    kj = lax.broadcasted_iota(jnp.int32, (2 * BLOCK, BLOCK), 1)
    first = qi < BLOCK
    qq = jnp.where(first, qi, qi - BLOCK)
    prev_ok = (kj > qq) if strict_prev else (kj >= qq)
    mid = (first & (kj <= qq)) | (jnp.logical_not(first) & prev_ok)
    last = jnp.logical_not(first) & (kj <= qq)
    return jnp.stack([jnp.where(mid, 0.0, NEG_INF), jnp.where(last, 0.0, NEG_INF)])


def _pair_bias():
    qi = lax.broadcasted_iota(jnp.int32, (2 * BLOCK, 2 * BLOCK), 0)
    kj = lax.broadcasted_iota(jnp.int32, (2 * BLOCK, 2 * BLOCK), 1)
    return jnp.where((kj <= qi) & (qi - kj <= BLOCK), 0.0, NEG_INF)


UNROLL_PAIR = 2


UNROLL = 4


def attn_a_fwd(h, ca, sa, *, n_heads, name):
    S = h.shape[0]
    scale = HEAD_DIM_A ** -0.5
    half = HEAD_DIM_A // 8
    branches = _branch_blocks(S)

    def body(q_ref, k_ref, v_ref, c_ref, s_ref, y_ref, lse_ref, qs, ks, m_s, l_s, acc_s, bias, pbias):
        qs[...] = _rope(q_ref[...], c_ref[...], s_ref[...], half, HEAD_DIM_A)
        ks[...] = _rope(k_ref[...], c_ref[...], s_ref[...], half, HEAD_DIM_A)
        bias[...] = _fwd_bias(False)
        pbias[...] = _pair_bias()
        for bi, (d, nb) in enumerate(branches):
            assert nb >= 2 and (d * nb) % UNROLL == 0
            if nb == 2 and bi > 0:
                assert d % UNROLL_PAIR == 0

                def pair(it, carry, d=d):
                    rws = [_win(it * UNROLL_PAIR + u, 0, d) for u in range(UNROLL_PAIR)]
                    ss = [_dot_nt(qs[rw, :].astype(BF16), ks[rw, :].astype(BF16)) * scale + pbias[...] for rw in rws]
                    ms = [jnp.max(s, axis=-1, keepdims=True) for s in ss]
                    ps = [jnp.exp(s - m) for s, m in zip(ss, ms)]
                    ls = [jnp.sum(p, axis=-1, keepdims=True) for p in ps]
                    accs = [_dot(p.astype(BF16), v_ref[rw, :].astype(BF16)) for p, rw in zip(ps, rws)]
                    for rw, m, l, acc in zip(rws, ms, ls, accs):
                        mb = jnp.broadcast_to(m, (2 * BLOCK, LANES))
                        m0 = m_s[rw, :]
                        mn = jnp.maximum(m0, mb)
                        a0, a1 = jnp.exp(m0 - mn), jnp.exp(mb - mn)
                        m_s[rw, :] = mn
                        l_s[rw, :] = l_s[rw, :] * a0 + jnp.broadcast_to(l, (2 * BLOCK, LANES)) * a1
                        acc_s[rw, :] = acc_s[rw, :] * a0 + acc * a1
                    return carry

                lax.fori_loop(0, d // UNROLL_PAIR, pair, 0)
                continue

            def blk(it, carry, bi=bi, d=d, nb=nb):
                rn = [((it * UNROLL + u) // nb, (it * UNROLL + u) % nb) for u in range(UNROLL)]
                rcs = [_rows(r, n, d) for r, n in rn]
                rws = [_win(r, jnp.maximum(n - 1, 0), d) for r, n in rn]
                ss = [_dot_nt(qs[rc, :].astype(BF16), ks[rw, :].astype(BF16)) * scale + bias[jnp.minimum(n, 1)]
                      for (r, n), rc, rw in zip(rn, rcs, rws)]
                ms = [jnp.max(s, axis=-1, keepdims=True) for s in ss]
                ps = [jnp.exp(s - m) for s, m in zip(ss, ms)]
                ls = [jnp.sum(p, axis=-1, keepdims=True) for p in ps]
                accs = [_dot(p.astype(BF16), v_ref[rw, :].astype(BF16)) for p, rw in zip(ps, rws)]
                for rc, m, l, acc in zip(rcs, ms, ls, accs):
                    mb = jnp.broadcast_to(m, (BLOCK, LANES))
                    lb = jnp.broadcast_to(l, (BLOCK, LANES))
                    if bi == 0:
                        m_s[rc, :], l_s[rc, :], acc_s[rc, :] = mb, lb, acc
                    else:
                        m0 = m_s[rc, :]
                        mn = jnp.maximum(m0, mb)
                        a0, a1 = jnp.exp(m0 - mn), jnp.exp(mb - mn)
                        m_s[rc, :] = mn
                        l_s[rc, :] = l_s[rc, :] * a0 + lb * a1
                        acc_s[rc, :] = acc_s[rc, :] * a0 + acc * a1
                return carry

            lax.fori_loop(0, d * nb // UNROLL, blk, 0)
        y_ref[...] = acc_s[...] / l_s[...]
        lse_ref[...] = m_s[...] + jnp.log(l_s[...])

    col = lambda off: pl.BlockSpec((S, LANES), lambda hd: (0, off + hd))
    full = pl.BlockSpec((S, LANES), lambda hd: (0, 0))
    out = pl.BlockSpec((S, LANES), lambda hd: (0, hd))
    o_shape = jax.ShapeDtypeStruct((S, n_heads * LANES), F32)
    return pl.pallas_call(
        body, name=name, grid=(n_heads,),
        in_specs=[col(0), col(n_heads), col(2 * n_heads), full, full],
        out_specs=(out, out), out_shape=(o_shape, o_shape),
        scratch_shapes=[pltpu.VMEM((S, LANES), F32) for _ in range(5)]
        + [pltpu.VMEM((2, BLOCK, 2 * BLOCK), F32), pltpu.VMEM((2 * BLOCK, 2 * BLOCK), F32)],
        compiler_params=_params(("arbitrary",)),
    )(h, h, h, ca, sa)


def attn_a_bwd(h, ca, sa, ya, lse, dya, *, n_heads, name):
    S = h.shape[0]
    scale = HEAD_DIM_A ** -0.5
    half = HEAD_DIM_A // 8
    branches = _branch_blocks(S)

    def body(q_ref, k_ref, v_ref, c_ref, s_ref, y_ref, lse_ref, dy_ref, o_ref, qs, ks, dq_s, dk_s, dv_s, bias, pbias):
        part = pl.program_id(1)

        @pl.when(part == 0)
        def _():
            qs[...] = _rope(q_ref[...], c_ref[...], s_ref[...], half, HEAD_DIM_A)
            ks[...] = _rope(k_ref[...], c_ref[...], s_ref[...], half, HEAD_DIM_A)
            dq_s[...] = jnp.zeros_like(dq_s)
            bias[...] = _bwd_bias(False)
            pbias[...] = _pair_bias()
            for bi, (d, nb) in enumerate(branches):
                assert nb >= 2 and (d * nb) % UNROLL == 0
                if nb == 2 and bi > 0:
                    assert d % UNROLL_PAIR == 0

                    def pair(it, carry, d=d):
                        rws = [_win(it * UNROLL_PAIR + u, 0, d) for u in range(UNROLL_PAIR)]
                        q2 = [qs[rw, :].astype(BF16) for rw in rws]
                        k2 = [ks[rw, :].astype(BF16) for rw in rws]
                        dy2 = [dy_ref[rw, :] for rw in rws]
                        dsum = [jnp.sum(dy * y_ref[rw, :], axis=-1, keepdims=True) for dy, rw in zip(dy2, rws)]
                        dy2b = [dy.astype(BF16) for dy in dy2]
                        ss = [_dot_nt(q, k) * scale + pbias[...] for q, k in zip(q2, k2)]
                        dps = [_dot_nt(dy, v_ref[rw, :].astype(BF16)) for dy, rw in zip(dy2b, rws)]
                        ps = [jnp.exp(s - lse_ref[rw, :][:, :1]) for s, rw in zip(ss, rws)]
                        dss = [(p * (dp - dm) * scale).astype(BF16) for p, dp, dm in zip(ps, dps, dsum)]
                        dvs = [_dot_tn(p.astype(BF16), dy) for p, dy in zip(ps, dy2b)]
                        dks = [_dot_tn(ds, q) for ds, q in zip(dss, q2)]
                        dqs = [_dot(ds, k) for ds, k in zip(dss, k2)]
                        for rw, dv, dk, dq in zip(rws, dvs, dks, dqs):
                            dv_s[rw, :] += dv
                            dk_s[rw, :] += dk
                            dq_s[rw, :] += dq
                        return carry

                    lax.fori_loop(0, d // UNROLL_PAIR, pair, 0)
                    continue

                def blk(it, carry, bi=bi, d=d, nb=nb):
                    rj = [((it * UNROLL + u) // nb, (it * UNROLL + u) % nb) for u in range(UNROLL)]
                    rks = [_rows(r, j, d) for r, j in rj]
                    rws = [_win(r, jnp.minimum(j, nb - 2), d) for r, j in rj]
                    q2 = [qs[rw, :].astype(BF16) for rw in rws]
                    kb = [ks[rk, :].astype(BF16) for rk in rks]
                    dy2 = [dy_ref[rw, :] for rw in rws]
                    dsum = [jnp.sum(dy * y_ref[rw, :], axis=-1, keepdims=True) for dy, rw in zip(dy2, rws)]
                    dy2b = [dy.astype(BF16) for dy in dy2]
                    ss = [_dot_nt(q, k) * scale + bias[(j == nb - 1).astype(jnp.int32)] for q, k, (r, j) in zip(q2, kb, rj)]
                    dps = [_dot_nt(dy, v_ref[rk, :].astype(BF16)) for dy, rk in zip(dy2b, rks)]
                    ps = [jnp.exp(s - lse_ref[rw, :]) for s, rw in zip(ss, rws)]
                    dss = [(p * (dp - dm) * scale).astype(BF16) for p, dp, dm in zip(ps, dps, dsum)]
                    dvs = [_dot_tn(p.astype(BF16), dy) for p, dy in zip(ps, dy2b)]
                    dks = [_dot_tn(ds, q) for ds, q in zip(dss, q2)]
                    dqs = [_dot(ds, k) for ds, k in zip(dss, kb)]
                    for rk, rw, dv, dk, dq in zip(rks, rws, dvs, dks, dqs):
                        if bi == 0:
                            dv_s[rk, :], dk_s[rk, :] = dv, dk
                        else:
                            dv_s[rk, :] += dv
                            dk_s[rk, :] += dk
                        dq_s[rw, :] += dq
                    return carry

                lax.fori_loop(0, d * nb // UNROLL, blk, 0)
            o_ref[...] = _rope_t(dq_s[...], c_ref[...], s_ref[...], half, HEAD_DIM_A).astype(o_ref.dtype)

        @pl.when(part == 1)
        def _():
            o_ref[...] = _rope_t(dk_s[...], c_ref[...], s_ref[...], half, HEAD_DIM_A).astype(o_ref.dtype)

        @pl.when(part == 2)
        def _():
            o_ref[...] = dv_s[...].astype(o_ref.dtype)

    col = lambda off: pl.BlockSpec((S, LANES), lambda hd, p: (0, off + hd))
    full = pl.BlockSpec((S, LANES), lambda hd, p: (0, 0))
    per_head = pl.BlockSpec((S, LANES), lambda hd, p: (0, hd))
    return pl.pallas_call(
        body, name=name, grid=(n_heads, 3),
        in_specs=[col(0), col(n_heads), col(2 * n_heads), full, full, per_head, per_head, per_head],
        out_specs=pl.BlockSpec((S, LANES), lambda hd, p: (0, p * n_heads + hd)),
        out_shape=jax.ShapeDtypeStruct((S, h.shape[1]), BF16),
        scratch_shapes=[pltpu.VMEM((S, LANES), F32) for _ in range(5)]
        + [pltpu.VMEM((2, 2 * BLOCK, BLOCK), F32), pltpu.VMEM((2 * BLOCK, 2 * BLOCK), F32)],
        compiler_params=_params(("arbitrary", "arbitrary")),
    )(h, h, h, ca, sa, ya, lse, dya)


def _unstack_heads(t2):
    rows = t2.shape[0] // 2
    low = _lane((rows, LANES)) < HEAD_DIM_B
    return jnp.where(low, t2[:rows], t2[rows:])


def _head_rows(ref, tile, rows):
    a = ref[pl.ds(2 * tile, 1), :][:, :1]
    b = ref[pl.ds(2 * tile + 1, 1), :][:, :1]
    return jnp.concatenate([jnp.broadcast_to(a, (rows, 1)), jnp.broadcast_to(b, (rows, 1))], axis=0)


UNROLL_B = 2


def attn_b_fwd(h, cb, sb, sinks_l, *, off_q, n_qtiles, name):
    S = h.shape[0]
    nblk = S // BLOCK
    scale = HEAD_DIM_B ** -0.5
    half = HEAD_DIM_B // 8
    tiles_per_group = n_qtiles // 2
    assert nblk >= 2 and nblk % UNROLL_B == 0

    def body(q_ref, k_ref, v_ref, c_ref, s_ref, sink_ref, y_ref, lse_ref, qs, kg, vg, bias):
        t = pl.program_id(0)
        g = t // tiles_per_group
        qs[...] = _rope(q_ref[...], c_ref[...], s_ref[...], half, HEAD_DIM_B)

        @pl.when(t % tiles_per_group == 0)
        def _():
            kg[...] = _both_halves(_rope(k_ref[...], c_ref[...], s_ref[...], half, HEAD_DIM_B), g)
            vg[...] = _both_halves(v_ref[...], g)

        @pl.when(t == 0)
        def _():
            fb = _fwd_bias(True)
            bias[...] = jnp.concatenate([fb, fb], axis=1)

        sink = _head_rows(sink_ref, t, BLOCK)

        def blk(it, carry):
            ns = [it * UNROLL_B + u for u in range(UNROLL_B)]
            rcs = [pl.ds(pl.multiple_of(n * BLOCK, BLOCK), BLOCK) for n in ns]
            rws = [pl.ds(pl.multiple_of(jnp.maximum(n - 1, 0) * BLOCK, BLOCK), 2 * BLOCK) for n in ns]
            ss = [_dot_nt(_stack_heads(qs[rc, :]).astype(BF16), kg[rw, :].astype(BF16)) * scale + bias[jnp.minimum(n, 1)]
                  for n, rc, rw in zip(ns, rcs, rws)]
            ms = [jnp.max(s, axis=-1, keepdims=True) for s in ss]
            ps = [jnp.exp(s - m) for s, m in zip(ss, ms)]
            ls = [jnp.sum(p, axis=-1, keepdims=True) for p in ps]
            accs = [_dot(p.astype(BF16), vg[rw, :].astype(BF16)) for p, rw in zip(ps, rws)]
            for rc, m, l, acc in zip(rcs, ms, ls, accs):
                m2 = jnp.maximum(m, sink)
                c = jnp.exp(m - m2)
                den = l * c + jnp.exp(sink - m2)
                y_ref[rc, :] = _unstack_heads(acc * (c / den))
                lse_ref[rc, :] = _unstack_heads(jnp.broadcast_to(m2 + jnp.log(den), (2 * BLOCK, LANES)))
            return carry

        lax.fori_loop(0, nblk // UNROLL_B, blk, 0)

    full = lambda col: pl.BlockSpec((S, LANES), lambda t: (0, col))
    out = pl.BlockSpec((S, LANES), lambda t: (0, t))
    o_shape = jax.ShapeDtypeStruct((S, n_qtiles * LANES), F32)
    return pl.pallas_call(
        body, name=name, grid=(n_qtiles,),
        in_specs=[pl.BlockSpec((S, LANES), lambda t: (0, off_q + t)), full(off_q + n_qtiles), full(off_q + n_qtiles + 1),
                  full(0), full(0), pl.BlockSpec(sinks_l.shape, lambda t: (0, 0))],
        out_specs=(out, out), out_shape=(o_shape, o_shape),
        scratch_shapes=[pltpu.VMEM((S, LANES), F32) for _ in range(3)] + [pltpu.VMEM((2, 2 * BLOCK, 2 * BLOCK), F32)],
        compiler_params=_params(("arbitrary",)),
    )(h, h, h, cb, sb, sinks_l)


def attn_b_bwd(h, cb, sb, sinks_l, yb, lse, dyb, dh, *, off_q, n_qtiles, name):
    S = h.shape[0]
    nblk = S // BLOCK
    scale = HEAD_DIM_B ** -0.5
    half = HEAD_DIM_B // 8
    tiles_per_group = n_qtiles // 2
    n_steps = n_qtiles + 2
    W = 2 * BLOCK
    assert nblk >= 2 and nblk % UNROLL_B == 0

    def body(q_ref, k_ref, v_ref, c_ref, s_ref, sink_ref, y_ref, lse_ref, dy_ref, dh_in,
             o_ref, dsink_ref, qs, kg, vg, dk_acc, dv_acc, bias, dq_ref):
        t = pl.program_id(0)

        @pl.when(t == 0)
        def _():
            dk_acc[...] = jnp.zeros_like(dk_acc)
            dv_acc[...] = jnp.zeros_like(dv_acc)
            dsink_ref[...] = jnp.zeros_like(dsink_ref)
            bb = _bwd_bias(True)
            bias[...] = jnp.concatenate([bb, bb], axis=1)

        @pl.when(t < n_qtiles)
        def _():
            g = t // tiles_per_group
            qs[...] = _rope(q_ref[...], c_ref[...], s_ref[...], half, HEAD_DIM_B)

            @pl.when(t % tiles_per_group == 0)
            def _():
                kg[...] = _both_halves(_rope(k_ref[...], c_ref[...], s_ref[...], half, HEAD_DIM_B), g)
                vg[...] = _both_halves(v_ref[...], g)

            dq_ref[...] = jnp.zeros_like(dq_ref)
            sink = _head_rows(sink_ref, t, W)
            row = lax.broadcasted_iota(jnp.int32, (2 * W, 1), 0) % W
            low = _lane((W, LANES)) < HEAD_DIM_B

            def blk(it, dsink):
                js = [it * UNROLL_B + u for u in range(UNROLL_B)]
                rks = [pl.ds(pl.multiple_of(j * BLOCK, BLOCK), BLOCK) for j in js]
                rws = [pl.ds(pl.multiple_of(jnp.minimum(j, nblk - 2) * BLOCK, BLOCK), W) for j in js]
                q2 = [_stack_heads(qs[rw, :]).astype(BF16) for rw in rws]
                dy2 = [_stack_heads(dy_ref[rw, :]) for rw in rws]
                dsum = [jnp.sum(dy * _stack_heads(y_ref[rw, :]), axis=-1, keepdims=True) for dy, rw in zip(dy2, rws)]
                dy2b = [dy.astype(BF16) for dy in dy2]
                lse2 = []
                for rw in rws:
                    lt = lse_ref[rw, :]
                    lr = pltpu.roll(lt, HEAD_DIM_B, 1)
                    lse2.append(jnp.concatenate([jnp.where(low, lt, lr), jnp.where(low, lr, lt)], axis=0))
                kb = [kg[rk, :].astype(BF16) for rk in rks]
                ss = [_dot_nt(q, k) * scale + bias[(j == nblk - 1).astype(jnp.int32)] for q, k, j in zip(q2, kb, js)]
                dps = [_dot_nt(dy, vg[rk, :].astype(BF16)) for dy, rk in zip(dy2b, rks)]
                ps = [jnp.exp(s - l2) for s, l2 in zip(ss, lse2)]
                dss = [(p * (dp - dm) * scale).astype(BF16) for p, dp, dm in zip(ps, dps, dsum)]
                dvs = [_fold_halves(_dot_tn(p.astype(BF16), dy)) for p, dy in zip(ps, dy2b)]
                dks = [_fold_halves(_dot_tn(ds, q)) for ds, q in zip(dss, q2)]
                dqs = [_dot(ds, k) for ds, k in zip(dss, kb)]
                for j, rk, rw, dv, dk, dq, l2, dm in zip(js, rks, rws, dvs, dks, dqs, lse2, dsum):
                    dv_acc[g, rk, :] += dv
                    dk_acc[g, rk, :] += dk
                    dq_ref[rw, :] += _unstack_heads(dq)
                    diag = (row >= BLOCK).astype(jnp.int32) == (j == nblk - 1).astype(jnp.int32)
                    dsink = dsink - jnp.where(diag, jnp.exp(sink - l2[:, :1]) * dm, 0.0)
                return dsink

            dsink = lax.fori_loop(0, nblk // UNROLL_B, blk, jnp.zeros((2 * W, 1), F32))
            o_ref[...] = _rope_t(dq_ref[...], c_ref[...], s_ref[...], half, HEAD_DIM_B).astype(o_ref.dtype)
            d0 = jnp.sum(dsink[:W], axis=0, keepdims=True)
            d1 = jnp.sum(dsink[W:], axis=0, keepdims=True)
            dsink_ref[pl.ds(2 * t, 1), :] = jnp.broadcast_to(d0, (1, LANES))
            dsink_ref[pl.ds(2 * t + 1, 1), :] = jnp.broadcast_to(d1, (1, LANES))

        low_s = _lane((S, LANES)) < HEAD_DIM_B

        @pl.when(t == n_qtiles)
        def _():
            dk = jnp.where(low_s, dk_acc[0], dk_acc[1])
            o_ref[...] = _rope_t(dk, c_ref[...], s_ref[...], half, HEAD_DIM_B).astype(o_ref.dtype)

        @pl.when(t == n_qtiles + 1)
        def _():
            o_ref[...] = jnp.where(low_s, dv_acc[0], dv_acc[1]).astype(o_ref.dtype)

    qt = lambda t: jnp.minimum(t, n_qtiles - 1)
    full = lambda col: pl.BlockSpec((S, LANES), lambda t: (0, col))
    per_tile = pl.BlockSpec((S, LANES), lambda t: (0, qt(t)))
    return pl.pallas_call(
        body, name=name, grid=(n_steps,),
        in_specs=[pl.BlockSpec((S, LANES), lambda t: (0, off_q + qt(t))), full(off_q + n_qtiles),
                  full(off_q + n_qtiles + 1), full(0), full(0), pl.BlockSpec(sinks_l.shape, lambda t: (0, 0)),
                  per_tile, per_tile, per_tile, pl.BlockSpec(memory_space=pl.ANY)],
        out_specs=(pl.BlockSpec((S, LANES), lambda t: (0, off_q + t)), pl.BlockSpec(sinks_l.shape, lambda t: (0, 0))),
        out_shape=(jax.ShapeDtypeStruct(dh.shape, dh.dtype), jax.ShapeDtypeStruct(sinks_l.shape, F32)),
        input_output_aliases={9: 0},
        scratch_shapes=[pltpu.VMEM((S, LANES), F32) for _ in range(3)]
        + [pltpu.VMEM((2, S, LANES), F32), pltpu.VMEM((2, S, LANES), F32), pltpu.VMEM((2, 2 * W, BLOCK), F32),
           pltpu.VMEM((S, LANES), F32)],
        compiler_params=_params(("arbitrary",)),
    )(h, h, h, cb, sb, sinks_l, yb, lse, dyb, dh)


def mem_attn_fwd(q, kv, *, name, tm=512):
    S, D = q.shape
    n_mem = kv.shape[0]
    hd = D // N_MEM_HEADS
    scale = hd ** -0.5
    tm = _tile(S, tm, 8)

    def body(q_ref, kv_ref, o_ref):
        for hh in range(N_MEM_HEADS):
            cols = slice(hh * hd, (hh + 1) * hd)
            s = _dot_nt(q_ref[:, cols], kv_ref[:, cols]) * scale
            s = s - jnp.max(s, axis=-1, keepdims=True)
            e = jnp.exp(s)
            p = e / jnp.sum(e, axis=-1, keepdims=True)
            o_ref[:, cols] = _dot(p.astype(BF16), kv_ref[:, D + hh * hd:D + (hh + 1) * hd]).astype(o_ref.dtype)

    return pl.pallas_call(
        body, name=name, grid=(S // tm,),
        in_specs=[pl.BlockSpec((tm, D), lambda i: (i, 0)), pl.BlockSpec((n_mem, 2 * D), lambda i: (0, 0))],
        out_specs=pl.BlockSpec((tm, D), lambda i: (i, 0)),
        out_shape=jax.ShapeDtypeStruct((S, D), BF16),
        compiler_params=_params(("parallel",)),
    )(q, kv)


def mem_attn_bwd(q, kv, do, *, name, tm=512):
    S, D = q.shape
    n_mem = kv.shape[0]
    hd = D // N_MEM_HEADS
    scale = hd ** -0.5
    tm = _tile(S, tm, 8)

    def body(q_ref, kv_ref, do_ref, dq_ref, dkv_ref):
        i = pl.program_id(0)

        @pl.when(i == 0)
        def _():
            dkv_ref[...] = jnp.zeros_like(dkv_ref)

        for hh in range(N_MEM_HEADS):
            cols = slice(hh * hd, (hh + 1) * hd)
            vcols = slice(D + hh * hd, D + (hh + 1) * hd)
            qh, kh, vh, doh = q_ref[:, cols], kv_ref[:, cols], kv_ref[:, vcols], do_ref[:, cols]
            s = _dot_nt(qh, kh) * scale
            s = s - jnp.max(s, axis=-1, keepdims=True)
            e = jnp.exp(s)
            p = e / jnp.sum(e, axis=-1, keepdims=True)
            dp = _dot_nt(doh, vh)
            ds = (p * (dp - jnp.sum(dp * p, axis=-1, keepdims=True)) * scale).astype(BF16)
            dq_ref[:, cols] = _dot(ds, kh).astype(dq_ref.dtype)
            dkv_ref[:, cols] += _dot_tn(ds, qh)
            dkv_ref[:, vcols] += _dot_tn(p.astype(BF16), doh)

    row = pl.BlockSpec((tm, D), lambda i: (i, 0))
    kvs = pl.BlockSpec((n_mem, 2 * D), lambda i: (0, 0))
    return pl.pallas_call(
        body, name=name, grid=(S // tm,),
        in_specs=[row, kvs, row], out_specs=(row, kvs),
        out_shape=(jax.ShapeDtypeStruct((S, D), BF16), jax.ShapeDtypeStruct((n_mem, 2 * D), F32)),
        compiler_params=_params(("arbitrary",)),
    )(q, kv, do)


def _rows_view(t):
    return t.reshape(-1, t.shape[-1])


def _row_tile(rows, cols, target_elems=512 * 1024):
    return _tile(rows, max(8, target_elems // cols), 8)


def cast_bf16(w, *, name):
    v = _rows_view(w)
    R, C = v.shape
    tr = _row_tile(R, C)

    def body(w_ref, o_ref):
        o_ref[...] = w_ref[...].astype(BF16)

    spec = pl.BlockSpec((tr, C), lambda i: (i, 0))
    out = pl.pallas_call(body, name=name, grid=(R // tr,), in_specs=[spec], out_specs=spec,
                         out_shape=jax.ShapeDtypeStruct((R, C), BF16), compiler_params=_params(("parallel",)))(v)
    return out.reshape(w.shape)


def mesh_place():
    return tuple(lax.axis_index(a).astype(jnp.int32).reshape(1) for a in ("x", "y", "c"))


def pair_sum(p, r1, place, *, name):
    nsh, r, c = p.shape
    hr = r // 2
    tr = _row_tile(hr, c)
    nt = hr // tr

    def body(x_ref, y_ref, c_ref, p_ref, r_ref, o_ref):
        o_ref[...] = (p_ref[...].astype(F32) + r_ref[...].astype(F32)).astype(BF16)

    return pl.pallas_call(
        body, name=name,
        grid_spec=pltpu.PrefetchScalarGridSpec(
            num_scalar_prefetch=3, grid=(nsh, nt),
            in_specs=[pl.BlockSpec((None, tr, c), lambda s, i, x, y, cc: (s, cc[0] * nt + i, 0)),
                      pl.BlockSpec((None, tr, c), lambda s, i, x, y, cc: (s, i, 0))],
            out_specs=pl.BlockSpec((None, tr, c), lambda s, i, x, y, cc: (s, i, 0))),
        out_shape=jax.ShapeDtypeStruct((nsh, hr, c), BF16),
        compiler_params=_params(("parallel", "parallel")),
    )(*place, p, r1)


def cast_into_slot(w, li, place, *, name):
    _, R, C = w.shape
    tr = _row_tile(R, C)

    def body(x_ref, y_ref, c_ref, w_ref, o_ref):
        o_ref[...] = w_ref[...].astype(BF16)

    return pl.pallas_call(
        body, name=name,
        grid_spec=pltpu.PrefetchScalarGridSpec(
            num_scalar_prefetch=3, grid=(R // tr,),
            in_specs=[pl.BlockSpec((None, tr, C), lambda i, x, y, cc: (li, i, 0))],
            out_specs=pl.BlockSpec((None, tr, C), lambda i, x, y, cc: (2 * x[0] + y[0], i, 0))),
        out_shape=jax.ShapeDtypeStruct((N_CHIPS, R, C), BF16),
        compiler_params=_params(("parallel",)),
    )(*place, w)


def chip_sum(q, r2, place, gbuf, li, *, name):
    _, hr, c = q.shape
    tr = _row_tile(hr, c, 256 * 1024)
    nt = hr // tr

    def body(x_ref, y_ref, c_ref, q_ref, r_ref, g_in, o_ref):
        acc = q_ref[...].astype(F32)
        for k in range(r_ref.shape[0]):
            acc = acc + r_ref[k].astype(F32)
        o_ref[...] = acc

    return pl.pallas_call(
        body, name=name,
        grid_spec=pltpu.PrefetchScalarGridSpec(
            num_scalar_prefetch=3, grid=(nt,),
            in_specs=[pl.BlockSpec((None, tr, c), lambda i, x, y, cc: (2 * x[0] + y[0], i, 0)),
                      pl.BlockSpec((r2.shape[0], tr, c), lambda i, x, y, cc: (0, i, 0)),
                      pl.BlockSpec(memory_space=pl.ANY)],
            out_specs=pl.BlockSpec((None, tr, c), lambda i, x, y, cc: (li, cc[0] * nt + i, 0))),
        out_shape=jax.ShapeDtypeStruct(gbuf.shape, F32),
        input_output_aliases={5: 0},
        compiler_params=_params(("parallel",)),
    )(*place, q, r2, gbuf)


def adamw(w, g, m, v, *, name, emit_g=False):
    shape = w.shape
    wv, gv, mv, vv = (_rows_view(t) for t in (w, g, m, v))
    R, C = wv.shape
    tr = _row_tile(R, C, 256 * 1024)
    c1 = 1.0 / (1.0 - ADAM_B1 ** ADAM_STEP)
    c2 = 1.0 / (1.0 - ADAM_B2 ** ADAM_STEP)
    n_out = 4 if emit_g else 3

    def body(w_ref, g_ref, m_ref, v_ref, d_ref, nm_ref, nv_ref, *go_ref):
        g_ = g_ref[...]
        nm = ADAM_B1 * m_ref[...] + (1.0 - ADAM_B1) * g_
        nv = ADAM_B2 * v_ref[...] + (1.0 - ADAM_B2) * (g_ * g_)
        m_hat = nm * c1
        v_hat = nv * c2
        d_ref[...] = -ADAM_LR * (m_hat / (jnp.sqrt(v_hat) + ADAM_EPS) + ADAM_WD * w_ref[...])
        nm_ref[...] = nm
        nv_ref[...] = nv
        if emit_g:
            go_ref[0][...] = g_

    spec = pl.BlockSpec((tr, C), lambda i: (i, 0))
    o = jax.ShapeDtypeStruct((R, C), F32)
    outs = pl.pallas_call(body, name=name, grid=(R // tr,), in_specs=[spec] * 4, out_specs=(spec,) * n_out,
                          out_shape=(o,) * n_out, compiler_params=_params(("parallel",)))(wv, gv, mv, vv)
    return tuple(t.reshape(shape) for t in outs)


def _place():
    x, y, c = lax.axis_index("x"), lax.axis_index("y"), lax.axis_index("c")
    others = [(1 - x, y), (x, 1 - y), (1 - x, 1 - y)]
    return x, y, c, others


def _any_specs(n):
    return [pl.BlockSpec(memory_space=pl.ANY) for _ in range(n)]


HBM_SPEC = pl.BlockSpec(memory_space=pltpu.HBM)
SEM_SPEC = pl.BlockSpec(memory_space=pltpu.SEMAPHORE)
DATAFLOW = pltpu.SideEffectType.DATAFLOW_SIDE_EFFECTING
TOKEN = jax.ShapeDtypeStruct((8, LANES), F32)


def _in_hbm(arrays):
    return [pltpu.with_memory_space_constraint(a, pltpu.HBM) for a in arrays]


def _gather_copy(g, t, j, slot, px, py, c, send, recv):
    hr = g[t].shape[1] // 2
    rows = g[t].at[slot, pl.ds(c * hr, hr)]
    return pltpu.make_async_remote_copy(rows, rows, send.at[3 * t + j], recv.at[3 * t + j], device_id=(px, py, c), device_id_type=MESH)


def gather_start(gs, after, *, name):
    n = len(gs)

    def body(*refs):
        g, token = refs[:n], refs[-1]
        send, recv = refs[n + 1], refs[n + 2]
        x, y, c, others = _place()
        for t in range(n):
            for j, (px, py) in enumerate(others):
                _gather_copy(g, t, j, 2 * x + y, px, py, c, send, recv).start()
        token[...] = jnp.zeros_like(token)

    outs = pl.pallas_call(
        body, name=name,
        in_specs=[HBM_SPEC] * n + [pl.BlockSpec(memory_space=pl.ANY)],
        out_specs=(SEM_SPEC, SEM_SPEC, *[HBM_SPEC] * n, pl.BlockSpec(memory_space=pltpu.VMEM)),
        out_shape=(pltpu.SemaphoreType.DMA((3 * n,)), pltpu.SemaphoreType.DMA((3 * n,)),
                   *[pltpu.HBM(g.shape, g.dtype) for g in gs], TOKEN),
        input_output_aliases={t: 2 + t for t in range(n)},
        compiler_params=pltpu.CompilerParams(has_side_effects=DATAFLOW),
    )(*_in_hbm(gs), after)
    return outs[0], outs[1], list(outs[2:2 + n]), outs[-1]


def gather_wait(send, recv, gs, after, *, name):
    n = len(gs)

    def body(*refs):
        g, token = refs[:n], refs[-1]
        send, recv = refs[n], refs[n + 1]
        x, y, c, others = _place()
        for t in range(n):
            for j, (px, py) in enumerate(others):
                _gather_copy(g, t, j, 2 * x + y, px, py, c, send, recv).wait_send()
                _gather_copy(g, t, j, 2 * px + py, px, py, c, send, recv).wait_recv()
        token[...] = jnp.zeros_like(token)

    outs = pl.pallas_call(
        body, name=name,
        in_specs=[HBM_SPEC] * n + [SEM_SPEC, SEM_SPEC, pl.BlockSpec(memory_space=pl.ANY)],
        out_specs=tuple([HBM_SPEC] * n) + (pl.BlockSpec(memory_space=pltpu.VMEM),),
        out_shape=tuple(pltpu.HBM(g.shape, g.dtype) for g in gs) + (TOKEN,),
        input_output_aliases={t: t for t in range(n)},
        compiler_params=pltpu.CompilerParams(has_side_effects=DATAFLOW),
    )(*gs, send, recv, after)
    return list(outs[:n]), outs[n]


def gather_forward(gs, *, name):
    n = len(gs)

    def body(*refs):
        g = refs[n:2 * n]
        send, recv = refs[2 * n:]
        x, y, c, others = _place()
        cps = []
        for t in range(n):
            hr = g[t].shape[1] // 2
            for j, (px, py) in enumerate(others):
                rows = g[t].at[2 * px + py, pl.ds(c * hr, hr)]
                cp = pltpu.make_async_remote_copy(rows, rows, send.at[3 * t + j], recv.at[3 * t + j],
                                                  device_id=(x, y, 1 - c), device_id_type=MESH)
                cp.start()
                cps.append(cp)
        for t in range(n):
            hr = g[t].shape[1] // 2
            for j, (px, py) in enumerate(others):
                rows = g[t].at[2 * px + py, pl.ds((1 - c) * hr, hr)]
                pltpu.make_async_remote_copy(rows, rows, send.at[3 * t + j], recv.at[3 * t + j],
                                             device_id=(x, y, 1 - c), device_id_type=MESH).wait_recv()
        for cp in cps:
            cp.wait_send()

    return pl.pallas_call(
        body, name=name,
        in_specs=_any_specs(n), out_specs=_any_specs(n),
        out_shape=[jax.ShapeDtypeStruct(g.shape, g.dtype) for g in gs],
        input_output_aliases={t: t for t in range(n)},
        scratch_shapes=[pltpu.SemaphoreType.DMA((3 * n,)), pltpu.SemaphoreType.DMA((3 * n,))],
        compiler_params=pltpu.CompilerParams(has_side_effects=True),
    )(*gs)


def sibling_halves(parts, *, name):
    n = len(parts)

    def body(*refs):
        src, dst = refs[:n], refs[n:2 * n]
        send, recv = refs[2 * n:]
        x, y, c, _ = _place()
        cps = []
        for t in range(n):
            hr = src[t].shape[1] // 2
            cp = pltpu.make_async_remote_copy(src[t].at[:, pl.ds((1 - c) * hr, hr)], dst[t], send.at[t], recv.at[t],
                                              device_id=(x, y, 1 - c), device_id_type=MESH)
            cp.start()
            cps.append(cp)
        for cp in cps:
            cp.wait()

    return pl.pallas_call(
        body, name=name,
        in_specs=_any_specs(n), out_specs=_any_specs(n),
        out_shape=[jax.ShapeDtypeStruct((p.shape[0], p.shape[1] // 2, p.shape[2]), p.dtype) for p in parts],
        scratch_shapes=[pltpu.SemaphoreType.DMA((n,)), pltpu.SemaphoreType.DMA((n,))],
        compiler_params=pltpu.CompilerParams(has_side_effects=True),
    )(*parts)


def _chips_copy(q, land, t, j, px, py, c, send, recv):
    return pltpu.make_async_remote_copy(q[t].at[2 * px + py], land[t].at[j], send.at[3 * t + j], recv.at[3 * t + j],
                                        device_id=(px, py, c), device_id_type=MESH)


def chips_start(qs, *, name):
    n = len(qs)

    def body(*refs):
        q, land, token = refs[:n], refs[n:2 * n], refs[-1]
        send, recv = refs[2 * n], refs[2 * n + 1]
        x, y, c, others = _place()
        for t in range(n):
            for j, (px, py) in enumerate(others):
                _chips_copy(q, land, t, j, px, py, c, send, recv).start()
        token[...] = jnp.zeros_like(token)

    lands = [lax.empty((3,) + q.shape[1:], q.dtype) for q in qs]
    outs = pl.pallas_call(
        body, name=name,
        in_specs=[HBM_SPEC] * (2 * n),
        out_specs=(SEM_SPEC, SEM_SPEC, *[HBM_SPEC] * (2 * n), pl.BlockSpec(memory_space=pltpu.VMEM)),
        out_shape=(pltpu.SemaphoreType.DMA((3 * n,)), pltpu.SemaphoreType.DMA((3 * n,)),
                   *[pltpu.HBM(a.shape, a.dtype) for a in qs + lands], TOKEN),
        input_output_aliases={t: 2 + t for t in range(2 * n)},
        compiler_params=pltpu.CompilerParams(has_side_effects=DATAFLOW),
    )(*_in_hbm(qs + lands))
    return outs[0], outs[1], list(outs[2:2 + n]), list(outs[2 + n:2 + 2 * n]), outs[-1]


def chips_wait(send, recv, qs, lands, after, *, name):
    n = len(qs)

    def body(*refs):
        q, land = refs[:n], refs[n:2 * n]
        send, recv = refs[2 * n], refs[2 * n + 1]
        x, y, c, others = _place()
        for t in range(n):
            for j, (px, py) in enumerate(others):
                cp = _chips_copy(q, land, t, j, px, py, c, send, recv)
                cp.wait_send()
                cp.wait_recv()

    outs = pl.pallas_call(
        body, name=name,
        in_specs=[HBM_SPEC] * (2 * n) + [SEM_SPEC, SEM_SPEC, pl.BlockSpec(memory_space=pl.ANY)],
        out_specs=tuple([HBM_SPEC] * (2 * n)),
        out_shape=tuple(pltpu.HBM(a.shape, a.dtype) for a in qs + lands),
        input_output_aliases={t: t for t in range(2 * n)},
        compiler_params=pltpu.CompilerParams(has_side_effects=DATAFLOW),
    )(*qs, *lands, send, recv, after)
    return list(outs[:n]), list(outs[n:])


def join_halves(fulls, li, *, name):
    n = len(fulls)

    def body(*refs):
        g = refs[n:2 * n]
        send, recv = refs[2 * n:]
        x, y, c, _ = _place()
        cps = []
        for t in range(n):
            hr = g[t].shape[1] // 2
            rows = g[t].at[li, pl.ds(c * hr, hr)]
            cp = pltpu.make_async_remote_copy(rows, rows, send.at[t], recv.at[t], device_id=(x, y, 1 - c), device_id_type=MESH)
            cp.start()
            cps.append(cp)
        for t in range(n):
            hr = g[t].shape[1] // 2
            rows = g[t].at[li, pl.ds((1 - c) * hr, hr)]
            pltpu.make_async_remote_copy(rows, rows, send.at[t], recv.at[t],
                                         device_id=(x, y, 1 - c), device_id_type=MESH).wait_recv()
        for cp in cps:
            cp.wait_send()

    return pl.pallas_call(
        body, name=name,
        in_specs=_any_specs(n), out_specs=_any_specs(n),
        out_shape=[jax.ShapeDtypeStruct(g.shape, g.dtype) for g in fulls],
        input_output_aliases={t: t for t in range(n)},
        scratch_shapes=[pltpu.SemaphoreType.DMA((n,)), pltpu.SemaphoreType.DMA((n,))],
        compiler_params=pltpu.CompilerParams(has_side_effects=True),
    )(*fulls)


def allreduce_small(t, *, name):
    R, C = t.shape

    def body(t_ref, o_ref, land, send, recv):
        x, y, c, _ = _place()
        me = 4 * x + 2 * y + c
        land[me] = t_ref[...]
        cps = []
        for j in range(1, 8):
            px, py, pc = (x + (j >> 2)) % 2, (y + ((j >> 1) & 1)) % 2, (c + (j & 1)) % 2
            cp = pltpu.make_async_remote_copy(t_ref, land.at[me], send.at[j - 1], recv.at[j - 1],
                                              device_id=(px, py, pc), device_id_type=MESH)
            cp.start()
            cps.append(cp)
        for j in range(1, 8):
            px, py, pc = (x + (j >> 2)) % 2, (y + ((j >> 1) & 1)) % 2, (c + (j & 1)) % 2
            pltpu.make_async_remote_copy(t_ref, land.at[4 * px + 2 * py + pc], send.at[j - 1], recv.at[j - 1],
                                         device_id=(px, py, pc), device_id_type=MESH).wait_recv()
        for cp in cps:
            cp.wait_send()
        acc = land[0]
        for k in range(1, 8):
            acc = acc + land[k]
        o_ref[...] = acc

    return pl.pallas_call(
        body, name=name,
        in_specs=[pl.BlockSpec(memory_space=pltpu.VMEM)], out_specs=pl.BlockSpec(memory_space=pltpu.VMEM),
        out_shape=jax.ShapeDtypeStruct((R, C), F32),
        scratch_shapes=[pltpu.VMEM((8, R, C), F32), pltpu.SemaphoreType.DMA((7,)), pltpu.SemaphoreType.DMA((7,))],
        compiler_params=pltpu.CompilerParams(has_side_effects=True),
    )(t)


def _layer_fwd(x, xb, memb, w_in, rest, P, tabs, alpha, li):
    ca, sa, cb, sb = tabs
    nA = P["gn_a"].shape[1] // HEAD_DIM_A
    nQ = P["gn_b"].shape[1] // LANES
    nm = lambda s: f"L{li}_{s}"
    h = mm_nn(xb, w_in, name=nm("h"), out_dtype=F32, tn=2304)
    ya, lse_a = attn_a_fwd(h, ca, sa, n_heads=nA, name=nm("attn_a"))
    yb, lse_b = attn_b_fwd(h, cb, sb, P["sinks_l"], off_q=3 * nA, n_qtiles=nQ, name=nm("attn_b"))
    ymix = rms_fwd(ya, yb, P["gn_a"], P["gn_b"], name=nm("rms"))
    W, P = rest(ymix, P)
    z1, x1, x1b = mm_ln(ymix, W["w_out"][0], x, P["ln_mix_g"], P["ln_mix_b"], name=nm("out_ln"), alpha=alpha,
                        tm=256, tk=ymix.shape[1])
    qm = mm_nn(x1b, W["w_mq"], name=nm("mq"), out_dtype=BF16)
    kv = mm_nn(memb, W["w_mkv"], name=nm("mkv"), out_dtype=BF16, tm=256)
    o = mem_attn_fwd(qm, kv, name=nm("mem_attn"))
    z2, x2, x2b = mm_ln(o, W["w_mo"][0], x1, P["ln_mem_g"], P["ln_mem_b"], name=nm("mo_ln"), alpha=alpha,
                        tm=256, tk=o.shape[1])
    u, a = mm_nn(x2b, W["w_up"], name=nm("up"), out_dtype=BF16, relu2=True)
    z3, x3, x3b = mm_ln(a, W["w_down"][0], x2, P["ln_ff_g"], P["ln_ff_b"], name=nm("down_ln"), alpha=alpha)
    saved = dict(xb=xb, h=h, ya=ya, lse_a=lse_a, yb=yb, lse_b=lse_b, ymix=ymix, z1=z1, x1b=x1b, qm=qm, kv=kv, o=o,
                 z2=z2, x2b=x2b, u=u, a=a, z3=z3)
    return x3, x3b, saved


def _layer_bwd(dx3, sv, memb, W, P, tabs, alpha, li, hook=None):
    ca, sa, cb, sb = tabs
    nA = P["gn_a"].shape[1] // HEAD_DIM_A
    nQ = P["gn_b"].shape[1] // LANES
    nm = lambda s: f"L{li}_b_{s}"
    nsh = lambda k: W[k].shape[0]
    gw, gs = {}, {}
    dz3, dz3b, gs["ln_ff_g"], gs["ln_ff_b"] = ln_bwd(dx3, sv["z3"], P["ln_ff_g"], name=nm("ln_ff"))
    gw["w_down"] = mm_tn(sv["a"], dz3b, nsh("w_down"), name=nm("dw_down"))
    du = mm_nt(dz3b, W["w_down"], name=nm("du"), out_dtype=BF16, umul=sv["u"])
    gw["w_up"] = mm_tn(sv["x2b"], du, nsh("w_up"), name=nm("dw_up"))
    dx2 = mm_nt(du, W["w_up"], name=nm("dx2"), out_dtype=F32, resid=dz3, alpha=alpha)
    dz2, dz2b, gs["ln_mem_g"], gs["ln_mem_b"] = ln_bwd(dx2, sv["z2"], P["ln_mem_g"], name=nm("ln_mem"))
    gw["w_mo"] = mm_tn(sv["o"], dz2b, nsh("w_mo"), name=nm("dw_mo"))
    do = mm_nt(dz2b, W["w_mo"], name=nm("do"), out_dtype=BF16)
    dqm, dkv = mem_attn_bwd(sv["qm"], sv["kv"], do, name=nm("mem_attn"))
    gw["w_mq"] = mm_tn(sv["x1b"], dqm, nsh("w_mq"), name=nm("dw_mq"))
    gw["w_mkv"] = mm_tn(memb, cast_bf16(dkv, name=nm("dkv_cast")), nsh("w_mkv"), name=nm("dw_mkv"), tm=256)
    dx1 = mm_nt(dqm, W["w_mq"], name=nm("dx1"), out_dtype=F32, resid=dz2, alpha=alpha)
    if hook is not None:
        P = hook(gw, dx1, P)
    dz1, dz1b, gs["ln_mix_g"], gs["ln_mix_b"] = ln_bwd(dx1, sv["z1"], P["ln_mix_g"], name=nm("ln_mix"))
    gw["w_out"] = mm_tn(sv["ymix"], dz1b, nsh("w_out"), name=nm("dw_out"))
    dymix = mm_nt(dz1b, W["w_out"], name=nm("dymix"), out_dtype=F32)
    dya, dyb, gs["gn_a"], gs["gn_b"] = rms_bwd(dymix, sv["ya"], sv["yb"], P["gn_a"], P["gn_b"], name=nm("rms"))
    dh = attn_a_bwd(sv["h"], ca, sa, sv["ya"], sv["lse_a"], dya, n_heads=nA, name=nm("attn_a"))
    dh, gs["sinks"] = attn_b_bwd(sv["h"], cb, sb, P["sinks_l"], sv["yb"], sv["lse_b"], dyb, dh,
                                 off_q=3 * nA, n_qtiles=nQ, name=nm("attn_b"))
    gw["w_in"] = mm_tn(sv["xb"], dh, nsh("w_in"), name=nm("dw_in"), tn=2304)
    dx0 = mm_nt(dh, W["w_in"], name=nm("dx0"), out_dtype=F32, resid=dz1, alpha=alpha, tr=2304)
    return dx0, gw, gs


def _gathered_view(name, g):
    if name == "w_in":
        return jnp.concatenate([g[k] for k in range(N_CHIPS)], axis=1)[None]
    if name in COL_SHARDED:
        return g
    return g.reshape(1, g.shape[0] * g.shape[1], g.shape[2])


def _to_shards(name, gw):
    if name == "w_in":
        n = gw.shape[2] // N_CHIPS
        return jnp.stack([gw[0, :, k * n:(k + 1) * n] for k in range(N_CHIPS)])
    if name in COL_SHARDED:
        return gw
    return gw.reshape(N_CHIPS, gw.shape[1] // N_CHIPS, gw.shape[2])


def _step(x, mem, positions, loss_target, w, m, v):
    S, D = x.shape[1], x.shape[2]
    depth = w["w_in"].shape[0]
    alpha = (2 * depth) ** 0.25
    x0 = x[0]
    memb = cast_bf16(mem[0], name="mem_cast")
    pos = positions[0]
    tabs = rope_tables(pos, HEAD_DIM_A // 4, HEAD_DIM_A) + rope_tables(pos, HEAD_DIM_B // 4, HEAD_DIM_B)
    place = mesh_place()

    def small(li):
        P = {k: w[k][li][None] for k in ("gn_a", "gn_b", "ln_mix_g", "ln_mix_b", "ln_mem_g", "ln_mem_b", "ln_ff_g", "ln_ff_b")}
        P["sinks_l"] = jnp.broadcast_to(w["sinks"][li][:, None], (w["sinks"].shape[1], LANES))
        return P

    rest_names = tuple(k for k in BIG if k != "w_in")
    chain = [(0, ("w_in",)), (0, rest_names)] + [(li, BIG) for li in range(1, depth)]
    casts = [[cast_into_slot(w[k], li, place, name=f"L{li}_cast_{k}") for k in names] for li, names in chain]
    started = {0: gather_start(casts[0], tabs[1], name="G0_gather_start")}

    def land(gi, after):
        send, recv, gs, tok0 = started.pop(gi)
        gs, landed = gather_wait(send, recv, gs, tok0 if after is None else after, name=f"G{gi}_gather_wait")
        token = None
        if gi + 1 < len(chain):
            started[gi + 1] = gather_start(casts[gi + 1], landed, name=f"G{gi + 1}_gather_start")
            token = started[gi + 1][3]
        gs = gather_forward(gs, name=f"G{gi}_gather_fwd")
        return dict(zip(chain[gi][1], gs)), token

    def ordered(a, token):
        return a if token is None else a + token[:1, :1].astype(a.dtype)

    xs, xbs, saved, Ws = x0, cast_bf16(x0, name="x_cast"), [], []
    for li in range(depth):
        gi = 0 if li == 0 else li + 1
        got, token = land(gi, None if li == 0 else xs)
        W = {"w_in": _gathered_view("w_in", got["w_in"])}
        tabs_l = (ordered(tabs[0], token),) + tabs[1:]

        def rest(after, P, li=li, got=got, W=W):
            if li == 0:
                got, token = land(1, after)
                P = dict(P, ln_mix_g=ordered(P["ln_mix_g"], token))
            W.update({k: _gathered_view(k, got[k]) for k in rest_names})
            return W, P

        xs, xbs, sv = _layer_fwd(xs, xbs, memb, W["w_in"], rest, small(li), tabs_l, alpha, li)
        saved.append(sv)
        Ws.append(W)
    dy, loss_part = loss_head(xs, loss_target[0], name="loss")
    loss = lax.psum(0.5 / D * jnp.sum(loss_part), ("x", "y", "c"))

    g_big = {k: lax.empty(w[k].shape, F32) for k in BIG}
    g_small = [None] * depth

    def begin(li, names, gw, tag):
        parts = [_to_shards(k, gw[k]) for k in names]
        r1 = sibling_halves(parts, name=f"L{li}{tag}_rs_sibling")
        qs = [pair_sum(p, r, place, name=f"L{li}_rs_pair_{k}") for k, p, r in zip(names, parts, r1)]
        send, recv, qs, lands, token = chips_start(qs, name=f"L{li}{tag}_rs_chips_start")
        return (li, names, tag, send, recv, qs, lands), token

    def finish(pending, after):
        li, names, tag, send, recv, qs, lands = pending
        qs, lands = chips_wait(send, recv, qs, lands, after, name=f"L{li}{tag}_rs_chips_wait")
        fulls = [chip_sum(q, r, place, g_big[k], li, name=f"L{li}_rs_sum_{k}") for k, q, r in zip(names, qs, lands)]
        for k, f in zip(names, join_halves(fulls, li, name=f"L{li}{tag}_rs_join")):
            g_big[k] = f

    early = ("w_mq", "w_mkv", "w_mo", "w_up", "w_down")
    late = tuple(k for k in BIG if k not in early)
    pendings, token = [], None
    for li in reversed(range(depth)):
        P = small(li)
        P["ln_ff_g"] = ordered(P["ln_ff_g"], token)
        hook = None
        if li == 0:
            def hook(gw, dx1, P):
                while pendings:
                    finish(pendings.pop(), dx1)
                pend, tok = begin(0, early, gw, "a")
                pendings.append(pend)
                return dict(P, ln_mix_g=ordered(P["ln_mix_g"], tok))
        dy, gw, g_small[li] = _layer_bwd(dy, saved[li], memb, Ws[li], P, tabs, alpha, li, hook)
        while pendings:
            finish(pendings.pop(), dy)
        pend, token = begin(li, late if li == 0 else BIG, gw, "b" if li == 0 else "")
        pendings.append(pend)
    finish(pendings.pop(), token)
    grad_x = dy[None]

    rows = []
    for li in range(depth):
        gs = g_small[li]
        for k in ("ln_mix_g", "ln_mix_b", "ln_mem_g", "ln_mem_b", "ln_ff_g", "ln_ff_b"):
            rows.append(jnp.sum(gs[k], axis=0, keepdims=True))
        rows.append(jnp.concatenate([jnp.sum(gs["gn_a"], axis=0, keepdims=True), jnp.sum(gs["gn_b"], axis=0, keepdims=True)], axis=1))
        sk = gs["sinks"][:, 0][None]
        rows.append(jnp.pad(sk, ((0, 0), (0, D - sk.shape[1]))))
    red = allreduce_small(jnp.concatenate(rows, axis=0), name="small_allreduce").reshape(depth, 8, D)
    wa = w["gn_a"].shape[1]
    grads = dict(g_big)
    for j, k in enumerate(("ln_mix_g", "ln_mix_b", "ln_mem_g", "ln_mem_b", "ln_ff_g", "ln_ff_b")):
        grads[k] = red[:, j]
    grads["gn_a"] = red[:, 6, :wa]
    grads["gn_b"] = red[:, 6, wa:]
    grads["sinks"] = red[:, 7, :w["sinks"].shape[1]]

    delta, new_m, new_v = {}, {}, {}
    small_names = [k for k in w if k not in BIG]
    for k in BIG:
        delta[k], new_m[k], new_v[k], grads[k] = adamw(w[k], grads[k], m[k], v[k], name=f"adamw_{k}", emit_g=True)
    pack = lambda d: jnp.concatenate([jnp.pad(d[k], ((0, 0), (0, D - d[k].shape[1]))) for k in small_names], axis=0)
    ds, ms, vs = adamw(pack(w), pack(grads), pack(m), pack(v), name="adamw_small")
    for j, k in enumerate(small_names):
        sl = (slice(j * depth, (j + 1) * depth), slice(0, w[k].shape[1]))
        delta[k], new_m[k], new_v[k] = ds[sl], ms[sl], vs[sl]
    return loss, grad_x, grads, delta, new_m, new_v


WEIGHTS = ("w_in", "gn_a", "gn_b", "sinks", "w_out", "ln_mix_g", "ln_mix_b", "w_mq", "w_mkv", "w_mo",
           "ln_mem_g", "ln_mem_b", "w_up", "w_down", "ln_ff_g", "ln_ff_b")


def kernel(x, mem, positions, w_in, gn_a, gn_b, sinks, w_out, ln_mix_g, ln_mix_b, w_mq, w_mkv, w_mo, ln_mem_g, ln_mem_b, w_up, w_down, ln_ff_g, ln_ff_b, loss_target, m_w_in, m_gn_a, m_gn_b, m_sinks, m_w_out, m_ln_mix_g, m_ln_mix_b, m_w_mq, m_w_mkv, m_w_mo, m_ln_mem_g, m_ln_mem_b, m_w_up, m_w_down, m_ln_ff_g, m_ln_ff_b, v_w_in, v_gn_a, v_gn_b, v_sinks, v_w_out, v_ln_mix_g, v_ln_mix_b, v_w_mq, v_w_mkv, v_w_mo, v_ln_mem_g, v_ln_mem_b, v_w_up, v_w_down, v_ln_ff_g, v_ln_ff_b):
    w = dict(zip(WEIGHTS, (w_in, gn_a, gn_b, sinks, w_out, ln_mix_g, ln_mix_b, w_mq, w_mkv, w_mo, ln_mem_g, ln_mem_b, w_up, w_down, ln_ff_g, ln_ff_b)))
    m = dict(zip(WEIGHTS, (m_w_in, m_gn_a, m_gn_b, m_sinks, m_w_out, m_ln_mix_g, m_ln_mix_b, m_w_mq, m_w_mkv, m_w_mo, m_ln_mem_g, m_ln_mem_b, m_w_up, m_w_down, m_ln_ff_g, m_ln_ff_b)))
    v = dict(zip(WEIGHTS, (v_w_in, v_gn_a, v_gn_b, v_sinks, v_w_out, v_ln_mix_g, v_ln_mix_b, v_w_mq, v_w_mkv, v_w_mo, v_ln_mem_g, v_ln_mem_b, v_w_up, v_w_down, v_ln_ff_g, v_ln_ff_b)))
    loss, grad_x, grads, delta, new_m, new_v = _step(x, mem, positions, loss_target, w, m, v)
    return (loss, grad_x, *[grads[k] for k in WEIGHTS], *[delta[k] for k in WEIGHTS],
            *[new_m[k] for k in WEIGHTS], *[new_v[k] for k in WEIGHTS])
```

```python
import functools

import jax
import jax.numpy as jnp
from jax import lax
from jax.experimental import pallas as pl
from jax.experimental.pallas import tpu as pltpu

F32 = jnp.float32
BF16 = jnp.bfloat16
MESH = pl.DeviceIdType.MESH

HEAD_DIM_A = 128
HEAD_DIM_B = 64
LANES = 128
BLOCK = 128
DILATED_BRANCHES = ((128, 1), (512, 4), (2048, 16))
WINDOW_B = 128
N_MEM_HEADS = 4
ROPE_THETA = 500000.0
LN_EPS = 1e-5
RMS_EPS = 1e-6
NEG_INF = -1e30
ADAM_LR = 0.001
ADAM_B1 = 0.9
ADAM_B2 = 0.999
ADAM_EPS = 1e-08
ADAM_WD = 0.01
ADAM_STEP = 10
N_CHIPS = 4
VMEM_LIMIT = 56 * 1024 * 1024

BIG = ("w_in", "w_out", "w_mq", "w_mkv", "w_mo", "w_up", "w_down")
COL_SHARDED = ("w_in", "w_mkv", "w_up")


def _tile(n, target, mult=LANES):
    best = None
    t = mult
    while t <= min(n, target):
        if n % t == 0:
            best = t
        t += mult
    return best if best is not None else n


def _params(sem=None):
    return pltpu.CompilerParams(dimension_semantics=sem, vmem_limit_bytes=VMEM_LIMIT)


def _dot(a, b):
    return jnp.dot(a, b, preferred_element_type=F32)


def _dot_nt(a, b):
    return lax.dot_general(a, b, (((1,), (1,)), ((), ())), preferred_element_type=F32)


def _dot_tn(a, b):
    return lax.dot_general(a, b, (((0,), (0,)), ((), ())), preferred_element_type=F32)


def mm_nn(a, b3, *, name, out_dtype, relu2=False, tm=1024, tn=1024, tk=2048):
    M, K = a.shape
    nsh, _, nk = b3.shape
    tm, tn, tk = _tile(M, tm, 8), _tile(nk, tn), _tile(K, tk)
    nb, ksteps = nk // tn, K // tk

    def body(a_ref, b_ref, *rest):
        outs, scr = rest[:2 if relu2 else 1], rest[2 if relu2 else 1:]

        def finish(acc):
            if relu2:
                outs[0][...] = acc.astype(outs[0].dtype)
                r = jnp.maximum(acc, 0.0)
                outs[1][...] = (r * r).astype(outs[1].dtype)
            else:
                outs[0][...] = acc.astype(outs[0].dtype)

        if ksteps == 1:
            finish(_dot(a_ref[...], b_ref[...]))
        else:
            acc_ref = scr[0]
            k = pl.program_id(2)

            @pl.when(k == 0)
            def _():
                acc_ref[...] = jnp.zeros_like(acc_ref)

            acc_ref[...] += _dot(a_ref[...], b_ref[...])

            @pl.when(k == ksteps - 1)
            def _():
                finish(acc_ref[...])

    o_spec = pl.BlockSpec((tm, tn), lambda i, j, k: (i, j))
    o_shape = jax.ShapeDtypeStruct((M, nsh * nk), out_dtype)
    return pl.pallas_call(
        body, name=name,
        grid=(M // tm, nsh * nb, ksteps),
        in_specs=[pl.BlockSpec((tm, tk), lambda i, j, k: (i, k)),
                  pl.BlockSpec((None, tk, tn), lambda i, j, k: (j // nb, k, j % nb))],
        out_specs=(o_spec, o_spec) if relu2 else o_spec,
        out_shape=(o_shape, o_shape) if relu2 else o_shape,
        scratch_shapes=[] if ksteps == 1 else [pltpu.VMEM((tm, tn), F32)],
        compiler_params=_params(("parallel", "parallel", "arbitrary")),
    )(a, b3)


def mm_ln(a, w, resid, g, b, *, name, alpha, tm=512, tk=1024):
    M, K = a.shape
    D = w.shape[1]
    tm, tk = _tile(M, tm, 8), _tile(K, tk)
    ksteps = K // tk

    def body(a_ref, w_ref, r_ref, g_ref, b_ref, z_ref, xn_ref, xb_ref, *scr):
        def finish(acc):
            z = alpha * r_ref[...] + acc
            mu = jnp.mean(z, axis=-1, keepdims=True)
            zc = z - mu
            var = jnp.mean(zc * zc, axis=-1, keepdims=True)
            xn = zc * lax.rsqrt(var + LN_EPS) * g_ref[...] + b_ref[...]
            z_ref[...] = z
            xn_ref[...] = xn
            xb_ref[...] = xn.astype(BF16)

        if ksteps == 1:
            finish(_dot(a_ref[...], w_ref[...]))
            return
        acc_ref = scr[0]
        k = pl.program_id(1)

        @pl.when(k == 0)
        def _():
            acc_ref[...] = jnp.zeros_like(acc_ref)

        acc_ref[...] += _dot(a_ref[...], w_ref[...])

        @pl.when(k == ksteps - 1)
        def _():
            finish(acc_ref[...])

    row = pl.BlockSpec((tm, D), lambda i, k: (i, 0))
    vec = pl.BlockSpec((1, D), lambda i, k: (0, 0))
    return pl.pallas_call(
        body, name=name,
        grid=(M // tm, ksteps),
        in_specs=[pl.BlockSpec((tm, tk), lambda i, k: (i, k)),
                  pl.BlockSpec((tk, D), lambda i, k: (k, 0)), row, vec, vec],
        out_specs=(row, row, row),
        out_shape=(jax.ShapeDtypeStruct((M, D), F32), jax.ShapeDtypeStruct((M, D), F32),
                   jax.ShapeDtypeStruct((M, D), BF16)),
        scratch_shapes=[] if ksteps == 1 else [pltpu.VMEM((tm, D), F32)],
        compiler_params=_params(("parallel", "arbitrary")),
    )(a, w, resid, g, b)


def mm_nt(a, b3, *, name, out_dtype, resid=None, alpha=1.0, umul=None, tm=1024, tko=1024, tr=2048):
    M, N = a.shape
    nsh, K, nk = b3.shape
    tm, tko, tr = _tile(M, tm, 8), _tile(K, tko), _tile(nk, tr)
    nb = nk // tr
    rsteps = nsh * nb

    def body(a_ref, b_ref, *rest):
        rest = list(rest)
        r_ref = rest.pop(0) if resid is not None else None
        u_ref = rest.pop(0) if umul is not None else None
        o_ref = rest.pop(0)

        def finish(acc):
            if r_ref is not None:
                acc = acc + alpha * r_ref[...]
            if u_ref is not None:
                acc = acc * (2.0 * jnp.maximum(u_ref[...].astype(F32), 0.0))
            o_ref[...] = acc.astype(o_ref.dtype)

        if rsteps == 1:
            finish(_dot_nt(a_ref[...], b_ref[...]))
        else:
            acc_ref = rest[0]
            r = pl.program_id(2)

            @pl.when(r == 0)
            def _():
                acc_ref[...] = jnp.zeros_like(acc_ref)

            acc_ref[...] += _dot_nt(a_ref[...], b_ref[...])

            @pl.when(r == rsteps - 1)
            def _():
                finish(acc_ref[...])

    o_spec = pl.BlockSpec((tm, tko), lambda i, j, r: (i, j))
    in_specs = [pl.BlockSpec((tm, tr), lambda i, j, r: (i, r)),
                pl.BlockSpec((None, tko, tr), lambda i, j, r: (r // nb, j, r % nb))]
    args = [a, b3]
    for extra in (resid, umul):
        if extra is not None:
            in_specs.append(o_spec)
            args.append(extra)
    return pl.pallas_call(
        body, name=name,
        grid=(M // tm, K // tko, rsteps),
        in_specs=in_specs, out_specs=o_spec,
        out_shape=jax.ShapeDtypeStruct((M, K), out_dtype),
        scratch_shapes=[] if rsteps == 1 else [pltpu.VMEM((tm, tko), F32)],
        compiler_params=_params(("parallel", "parallel", "arbitrary")),
    )(*args)


def mm_tn(a, g, nsh, *, name, tk=1024, tn=1024, tm=2048):
    M, K = a.shape
    N = g.shape[1]
    nk = N // nsh
    tk, tn, tm = _tile(K, tk), _tile(nk, tn), _tile(M, tm, 8)
    nb, msteps = nk // tn, M // tm

    def body(a_ref, g_ref, o_ref, acc_ref):
        m = pl.program_id(2)

        @pl.when(m == 0)
        def _():
            acc_ref[...] = jnp.zeros_like(acc_ref)

        acc_ref[...] += _dot_tn(a_ref[...], g_ref[...])

        @pl.when(m == msteps - 1)
        def _():
            o_ref[...] = acc_ref[...].astype(o_ref.dtype)

    return pl.pallas_call(
        body, name=name,
        grid=(K // tk, nsh * nb, msteps),
        in_specs=[pl.BlockSpec((tm, tk), lambda i, j, m: (m, i)),
                  pl.BlockSpec((tm, tn), lambda i, j, m: (m, j))],
        out_specs=pl.BlockSpec((None, tk, tn), lambda i, j, m: (j // nb, i, j % nb)),
        out_shape=jax.ShapeDtypeStruct((nsh, K, nk), BF16),
        scratch_shapes=[pltpu.VMEM((tk, tn), F32)],
        compiler_params=_params(("parallel", "parallel", "arbitrary")),
    )(a, g)


def _fold8(t):
    return t.reshape(t.shape[0] // 8, 8, t.shape[1]).sum(axis=0)


def ln_bwd(dy, z, g, *, name, tm=256):
    M, D = z.shape
    tm = _tile(M, tm, 8)

    def body(dy_ref, z_ref, g_ref, dz_ref, dzb_ref, dg_ref, db_ref):
        i = pl.program_id(0)
        z_ = z_ref[...]
        dy_ = dy_ref[...]
        mu = jnp.mean(z_, axis=-1, keepdims=True)
        zc = z_ - mu
        var = jnp.mean(zc * zc, axis=-1, keepdims=True)
        rstd = lax.rsqrt(var + LN_EPS)
        xh = zc * rstd
        dxh = dy_ * g_ref[...]
        m1 = jnp.mean(dxh, axis=-1, keepdims=True)
        m2 = jnp.mean(dxh * xh, axis=-1, keepdims=True)
        dz = rstd * (dxh - m1 - xh * m2)
        dz_ref[...] = dz
        dzb_ref[...] = dz.astype(BF16)

        @pl.when(i == 0)
        def _():
            dg_ref[...] = jnp.zeros_like(dg_ref)
            db_ref[...] = jnp.zeros_like(db_ref)

        dg_ref[...] += _fold8(dy_ * xh)
        db_ref[...] += _fold8(dy_)

    row = pl.BlockSpec((tm, D), lambda i: (i, 0))
    acc = pl.BlockSpec((8, D), lambda i: (0, 0))
    return pl.pallas_call(
        body, name=name, grid=(M // tm,),
        in_specs=[row, row, pl.BlockSpec((1, D), lambda i: (0, 0))],
        out_specs=(row, row, acc, acc),
        out_shape=(jax.ShapeDtypeStruct((M, D), F32), jax.ShapeDtypeStruct((M, D), BF16),
                   jax.ShapeDtypeStruct((8, D), F32), jax.ShapeDtypeStruct((8, D), F32)),
        compiler_params=_params(("arbitrary",)),
    )(dy, z, g)


def rms_fwd(ya, yb, ga, gb, *, name, tm=512):
    M, WA = ya.shape
    WB = yb.shape[1]
    tm = _tile(M, tm, 8)

    def body(ya_ref, yb_ref, ga_ref, gb_ref, o_ref):
        for y_ref, g_ref, lo, w in ((ya_ref, ga_ref, 0, WA), (yb_ref, gb_ref, WA, WB)):
            y = y_ref[...]
            r = lax.rsqrt(jnp.mean(y * y, axis=-1, keepdims=True) + RMS_EPS)
            o_ref[:, lo:lo + w] = (y * r * g_ref[...]).astype(o_ref.dtype)

    return pl.pallas_call(
        body, name=name, grid=(M // tm,),
        in_specs=[pl.BlockSpec((tm, WA), lambda i: (i, 0)), pl.BlockSpec((tm, WB), lambda i: (i, 0)),
                  pl.BlockSpec((1, WA), lambda i: (0, 0)), pl.BlockSpec((1, WB), lambda i: (0, 0))],
        out_specs=pl.BlockSpec((tm, WA + WB), lambda i: (i, 0)),
        out_shape=jax.ShapeDtypeStruct((M, WA + WB), BF16),
        compiler_params=_params(("parallel",)),
    )(ya, yb, ga, gb)


def rms_bwd(dy, ya, yb, ga, gb, *, name, tm=512):
    M, WA = ya.shape
    WB = yb.shape[1]
    tm = _tile(M, tm, 8)

    def body(dy_ref, ya_ref, yb_ref, ga_ref, gb_ref, dya_ref, dyb_ref, dga_ref, dgb_ref):
        i = pl.program_id(0)

        @pl.when(i == 0)
        def _():
            dga_ref[...] = jnp.zeros_like(dga_ref)
            dgb_ref[...] = jnp.zeros_like(dgb_ref)

        for y_ref, g_ref, d_ref, dgr, lo, w in ((ya_ref, ga_ref, dya_ref, dga_ref, 0, WA),
                                                (yb_ref, gb_ref, dyb_ref, dgb_ref, WA, WB)):
            y = y_ref[...]
            d = dy_ref[:, lo:lo + w]
            r = lax.rsqrt(jnp.mean(y * y, axis=-1, keepdims=True) + RMS_EPS)
            n = y * r
            dn = d * g_ref[...]
            d_ref[...] = r * (dn - n * jnp.mean(dn * n, axis=-1, keepdims=True))
            dgr[...] += _fold8(d * n)

    return pl.pallas_call(
        body, name=name, grid=(M // tm,),
        in_specs=[pl.BlockSpec((tm, WA + WB), lambda i: (i, 0)),
                  pl.BlockSpec((tm, WA), lambda i: (i, 0)), pl.BlockSpec((tm, WB), lambda i: (i, 0)),
                  pl.BlockSpec((1, WA), lambda i: (0, 0)), pl.BlockSpec((1, WB), lambda i: (0, 0))],
        out_specs=(pl.BlockSpec((tm, WA), lambda i: (i, 0)), pl.BlockSpec((tm, WB), lambda i: (i, 0)),
                   pl.BlockSpec((8, WA), lambda i: (0, 0)), pl.BlockSpec((8, WB), lambda i: (0, 0))),
        out_shape=(jax.ShapeDtypeStruct((M, WA), F32), jax.ShapeDtypeStruct((M, WB), F32),
                   jax.ShapeDtypeStruct((8, WA), F32), jax.ShapeDtypeStruct((8, WB), F32)),
        compiler_params=_params(("arbitrary",)),
    )(dy, ya, yb, ga, gb)


def loss_head(y, target, *, name, tm=512):
    M, D = y.shape
    tm = _tile(M, tm, 8)

    def body(y_ref, t_ref, dy_ref, l_ref):
        i = pl.program_id(0)

        @pl.when(i == 0)
        def _():
            l_ref[...] = jnp.zeros_like(l_ref)

        e = y_ref[...] - t_ref[...]
        dy_ref[...] = e * (1.0 / D)
        l_ref[...] += _fold8(e * e)

    row = pl.BlockSpec((tm, D), lambda i: (i, 0))
    return pl.pallas_call(
        body, name=name, grid=(M // tm,),
        in_specs=[row, row],
        out_specs=(row, pl.BlockSpec((8, D), lambda i: (0, 0))),
        out_shape=(jax.ShapeDtypeStruct((M, D), F32), jax.ShapeDtypeStruct((8, D), F32)),
        compiler_params=_params(("arbitrary",)),
    )(y, target)


def _lane(shape):
    return lax.broadcasted_iota(jnp.int32, shape, len(shape) - 1)


def _swap(t, half, period):
    first = (_lane(t.shape) % period) < half
    return jnp.where(first, pltpu.roll(t, LANES - half, 1), pltpu.roll(t, half, 1))


def _rope(t, c, s, half, period):
    return t * c + _swap(t, half, period) * s


def _rope_t(g, c, s, half, period):
    return g * c - _swap(g, half, period) * s


def rope_tables(positions, rot_dim, period):
    half = rot_dim // 2
    inv_freq = ROPE_THETA ** (-jnp.arange(0, rot_dim, 2, dtype=F32) / rot_dim)
    ang = positions.astype(F32)[:, None] * inv_freq
    cos, sin = jnp.cos(ang), jnp.sin(ang)
    ones = jnp.ones((positions.shape[0], period - rot_dim), F32)
    c = jnp.concatenate([cos, cos, ones], axis=1)
    s = jnp.concatenate([-sin, sin, 0.0 * ones], axis=1)
    reps = LANES // period
    return jnp.tile(c, (1, reps)), jnp.tile(s, (1, reps))


def _branch_blocks(S):
    out = []
    for window, d in DILATED_BRANCHES:
        assert window // d == BLOCK and S % (d * BLOCK) == 0
        out.append((d, (S // d) // BLOCK))
    return out


def _rows(r, n, d):
    return pl.ds(r + n * (BLOCK * d), BLOCK, stride=d) if d > 1 else pl.ds(pl.multiple_of(n * BLOCK, BLOCK), BLOCK)


def _band_masks(n, strict_prev):
    qi = lax.broadcasted_iota(jnp.int32, (BLOCK, BLOCK), 0)
    kj = lax.broadcasted_iota(jnp.int32, (BLOCK, BLOCK), 1)
    cur = kj <= qi
    prev = ((kj > qi) if strict_prev else (kj >= qi)) & (n > 0)
    return cur, prev


def attn_a_fwd(h, ca, sa, *, n_heads, name):
    S = h.shape[0]
    scale = HEAD_DIM_A ** -0.5
    half = HEAD_DIM_A // 8
    branches = _branch_blocks(S)

    def body(q_ref, k_ref, v_ref, c_ref, s_ref, y_ref, lse_ref, qs, ks, m_s, l_s, acc_s):
        qs[...] = _rope(q_ref[...], c_ref[...], s_ref[...], half, HEAD_DIM_A)
        ks[...] = _rope(k_ref[...], c_ref[...], s_ref[...], half, HEAD_DIM_A)
        for bi, (d, nb) in enumerate(branches):
            def blk(idx, carry, bi=bi, d=d, nb=nb):
                r, n = idx // nb, idx % nb
                rc, rp = _rows(r, n, d), _rows(r, jnp.maximum(n - 1, 0), d)
                q = qs[rc, :].astype(BF16)
                cur, prev = _band_masks(n, False)
                sc = jnp.where(cur, _dot_nt(q, ks[rc, :].astype(BF16)) * scale, NEG_INF)
                sp = jnp.where(prev, _dot_nt(q, ks[rp, :].astype(BF16)) * scale, NEG_INF)
                m = jnp.maximum(jnp.max(sc, axis=-1, keepdims=True), jnp.max(sp, axis=-1, keepdims=True))
                pc, pp = jnp.exp(sc - m), jnp.exp(sp - m)
                l = jnp.sum(pc, axis=-1, keepdims=True) + jnp.sum(pp, axis=-1, keepdims=True)
                acc = _dot(pc.astype(BF16), v_ref[rc, :].astype(BF16)) + _dot(pp.astype(BF16), v_ref[rp, :].astype(BF16))
                mb = jnp.broadcast_to(m, (BLOCK, LANES))
                lb = jnp.broadcast_to(l, (BLOCK, LANES))
                if bi == 0:
                    m_s[rc, :], l_s[rc, :], acc_s[rc, :] = mb, lb, acc
                else:
                    m0 = m_s[rc, :]
                    mn = jnp.maximum(m0, mb)
                    a0, a1 = jnp.exp(m0 - mn), jnp.exp(mb - mn)
                    m_s[rc, :] = mn
                    l_s[rc, :] = l_s[rc, :] * a0 + lb * a1
                    acc_s[rc, :] = acc_s[rc, :] * a0 + acc * a1
                return carry

            lax.fori_loop(0, d * nb, blk, 0)
        y_ref[...] = acc_s[...] / l_s[...]
        lse_ref[...] = m_s[...] + jnp.log(l_s[...])

    col = lambda off: pl.BlockSpec((S, LANES), lambda hd: (0, off + hd))
    full = pl.BlockSpec((S, LANES), lambda hd: (0, 0))
    out = pl.BlockSpec((S, LANES), lambda hd: (0, hd))
    o_shape = jax.ShapeDtypeStruct((S, n_heads * LANES), F32)
    return pl.pallas_call(
        body, name=name, grid=(n_heads,),
        in_specs=[col(0), col(n_heads), col(2 * n_heads), full, full],
        out_specs=(out, out), out_shape=(o_shape, o_shape),
        scratch_shapes=[pltpu.VMEM((S, LANES), F32) for _ in range(5)],
        compiler_params=_params(("arbitrary",)),
    )(h, h, h, ca, sa)


def attn_a_bwd(h, ca, sa, ya, lse, dya, *, n_heads, name):
    S = h.shape[0]
    scale = HEAD_DIM_A ** -0.5
    half = HEAD_DIM_A // 8
    branches = _branch_blocks(S)

    def body(q_ref, k_ref, v_ref, c_ref, s_ref, y_ref, lse_ref, dy_ref, o_ref, qs, ks, dq_s, dk_s, dv_s):
        part = pl.program_id(1)

        @pl.when(part == 0)
        def _():
            qs[...] = _rope(q_ref[...], c_ref[...], s_ref[...], half, HEAD_DIM_A)
            ks[...] = _rope(k_ref[...], c_ref[...], s_ref[...], half, HEAD_DIM_A)
            dq_s[...] = jnp.zeros_like(dq_s)
            dk_s[...] = jnp.zeros_like(dk_s)
            dv_s[...] = jnp.zeros_like(dv_s)
            for d, nb in branches:
                def blk(idx, carry, d=d, nb=nb):
                    r, n = idx // nb, idx % nb
                    rc, rp = _rows(r, n, d), _rows(r, jnp.maximum(n - 1, 0), d)
                    q = qs[rc, :].astype(BF16)
                    dy = dy_ref[rc, :]
                    dsum = jnp.sum(dy * y_ref[rc, :], axis=-1, keepdims=True)
                    dyb = dy.astype(BF16)
                    lse_b = lse_ref[rc, :]
                    cur, prev = _band_masks(n, False)
                    dq = jnp.zeros((BLOCK, LANES), F32)
                    for rows, mask in ((rc, cur), (rp, prev)):
                        kb = ks[rows, :].astype(BF16)
                        vb = v_ref[rows, :].astype(BF16)
                        s = jnp.where(mask, _dot_nt(q, kb) * scale, NEG_INF)
                        p = jnp.exp(s - lse_b)
                        ds = (p * (_dot_nt(dyb, vb) - dsum) * scale).astype(BF16)
                        dv_s[rows, :] += _dot_tn(p.astype(BF16), dyb)
                        dk_s[rows, :] += _dot_tn(ds, q)
                        dq = dq + _dot(ds, kb)
                    dq_s[rc, :] += dq
                    return carry

                lax.fori_loop(0, d * nb, blk, 0)
            o_ref[...] = _rope_t(dq_s[...], c_ref[...], s_ref[...], half, HEAD_DIM_A).astype(o_ref.dtype)

        @pl.when(part == 1)
        def _():
            o_ref[...] = _rope_t(dk_s[...], c_ref[...], s_ref[...], half, HEAD_DIM_A).astype(o_ref.dtype)

        @pl.when(part == 2)
        def _():
            o_ref[...] = dv_s[...].astype(o_ref.dtype)

    col = lambda off: pl.BlockSpec((S, LANES), lambda hd, p: (0, off + hd))
    full = pl.BlockSpec((S, LANES), lambda hd, p: (0, 0))
    per_head = pl.BlockSpec((S, LANES), lambda hd, p: (0, hd))
    return pl.pallas_call(
        body, name=name, grid=(n_heads, 3),
        in_specs=[col(0), col(n_heads), col(2 * n_heads), full, full, per_head, per_head, per_head],
        out_specs=pl.BlockSpec((S, LANES), lambda hd, p: (0, p * n_heads + hd)),
        out_shape=jax.ShapeDtypeStruct((S, 3 * n_heads * LANES), BF16),
        scratch_shapes=[pltpu.VMEM((S, LANES), F32) for _ in range(5)],
        compiler_params=_params(("arbitrary", "arbitrary")),
    )(h, h, h, ca, sa, ya, lse, dya)


def _both_halves(t, g):
    low = _lane(t.shape) < HEAD_DIM_B
    return jnp.where(low == (g == 0), t, pltpu.roll(t, HEAD_DIM_B, 1))


def _stack_heads(t):
    low = _lane(t.shape) < HEAD_DIM_B
    return jnp.concatenate([jnp.where(low, t, 0.0), jnp.where(low, 0.0, t)], axis=0)


def _unstack_heads(t2):
    low = _lane((BLOCK, LANES)) < HEAD_DIM_B
    return jnp.where(low, t2[:BLOCK], t2[BLOCK:])


def _fold_halves(t):
    return t + pltpu.roll(t, HEAD_DIM_B, 1)


def _head_rows(ref, tile):
    a = ref[pl.ds(2 * tile, 1), :][:, :1]
    b = ref[pl.ds(2 * tile + 1, 1), :][:, :1]
    return jnp.concatenate([jnp.broadcast_to(a, (BLOCK, 1)), jnp.broadcast_to(b, (BLOCK, 1))], axis=0)


def attn_b_fwd(h, cb, sb, sinks_l, *, off_q, n_qtiles, name):
    S = h.shape[0]
    nblk = S // BLOCK
    scale = HEAD_DIM_B ** -0.5
    half = HEAD_DIM_B // 8
    tiles_per_group = n_qtiles // 2

    def body(q_ref, k_ref, v_ref, c_ref, s_ref, sink_ref, y_ref, lse_ref, qs, kg, vg):
        t = pl.program_id(0)
        g = t // tiles_per_group
        qs[...] = _rope(q_ref[...], c_ref[...], s_ref[...], half, HEAD_DIM_B)
        kg[...] = _both_halves(_rope(k_ref[...], c_ref[...], s_ref[...], half, HEAD_DIM_B), g)
        vg[...] = _both_halves(v_ref[...], g)
        sink = _head_rows(sink_ref, t)

        def blk(n, carry):
            rc = pl.ds(pl.multiple_of(n * BLOCK, BLOCK), BLOCK)
            rp = pl.ds(pl.multiple_of(jnp.maximum(n - 1, 0) * BLOCK, BLOCK), BLOCK)
            q2 = _stack_heads(qs[rc, :]).astype(BF16)
            cur, prev = _band_masks(n, True)
            cur2, prev2 = jnp.concatenate([cur, cur], 0), jnp.concatenate([prev, prev], 0)
            sc = jnp.where(cur2, _dot_nt(q2, kg[rc, :].astype(BF16)) * scale, NEG_INF)
            sp = jnp.where(prev2, _dot_nt(q2, kg[rp, :].astype(BF16)) * scale, NEG_INF)
            m = jnp.maximum(jnp.max(sc, axis=-1, keepdims=True), jnp.max(sp, axis=-1, keepdims=True))
            pc, pp = jnp.exp(sc - m), jnp.exp(sp - m)
            l = jnp.sum(pc, axis=-1, keepdims=True) + jnp.sum(pp, axis=-1, keepdims=True)
            acc = _dot(pc.astype(BF16), vg[rc, :].astype(BF16)) + _dot(pp.astype(BF16), vg[rp, :].astype(BF16))
            m2 = jnp.maximum(m, sink)
            c = jnp.exp(m - m2)
            den = l * c + jnp.exp(sink - m2)
            y_ref[rc, :] = _unstack_heads(acc * (c / den))
            lse_ref[rc, :] = _unstack_heads(jnp.broadcast_to(m2 + jnp.log(den), (2 * BLOCK, LANES)))
            return carry

        lax.fori_loop(0, nblk, blk, 0)

    full = lambda col: pl.BlockSpec((S, LANES), lambda t: (0, col))
    out = pl.BlockSpec((S, LANES), lambda t: (0, t))
    o_shape = jax.ShapeDtypeStruct((S, n_qtiles * LANES), F32)
    return pl.pallas_call(
        body, name=name, grid=(n_qtiles,),
        in_specs=[pl.BlockSpec((S, LANES), lambda t: (0, off_q + t)), full(off_q + n_qtiles), full(off_q + n_qtiles + 1),
                  full(0), full(0), pl.BlockSpec(sinks_l.shape, lambda t: (0, 0))],
        out_specs=(out, out), out_shape=(o_shape, o_shape),
        scratch_shapes=[pltpu.VMEM((S, LANES), F32) for _ in range(3)],
        compiler_params=_params(("arbitrary",)),
    )(h, h, h, cb, sb, sinks_l)


def attn_b_bwd(h, cb, sb, sinks_l, yb, lse, dyb, *, off_q, n_qtiles, name):
    S = h.shape[0]
    nblk = S // BLOCK
    scale = HEAD_DIM_B ** -0.5
    half = HEAD_DIM_B // 8
    tiles_per_group = n_qtiles // 2
    n_steps = n_qtiles + 2

    def body(q_ref, k_ref, v_ref, c_ref, s_ref, sink_ref, y_ref, lse_ref, dy_ref,
             dq_ref, dkv_ref, dsink_ref, qs, kg, vg, dk_acc, dv_acc):
        t = pl.program_id(0)

        @pl.when(t == 0)
        def _():
            dk_acc[...] = jnp.zeros_like(dk_acc)
            dv_acc[...] = jnp.zeros_like(dv_acc)
            dsink_ref[...] = jnp.zeros_like(dsink_ref)

        @pl.when(t < n_qtiles)
        def _():
            g = t // tiles_per_group
            qs[...] = _rope(q_ref[...], c_ref[...], s_ref[...], half, HEAD_DIM_B)
            kg[...] = _both_halves(_rope(k_ref[...], c_ref[...], s_ref[...], half, HEAD_DIM_B), g)
            vg[...] = _both_halves(v_ref[...], g)
            sink = _head_rows(sink_ref, t)

            def blk(n, dsink):
                rc = pl.ds(pl.multiple_of(n * BLOCK, BLOCK), BLOCK)
                rp = pl.ds(pl.multiple_of(jnp.maximum(n - 1, 0) * BLOCK, BLOCK), BLOCK)
                q2 = _stack_heads(qs[rc, :]).astype(BF16)
                dy2 = _stack_heads(dy_ref[rc, :])
                dsum = jnp.sum(dy2 * _stack_heads(y_ref[rc, :]), axis=-1, keepdims=True)
                dy2b = dy2.astype(BF16)
                lse_t = lse_ref[rc, :]
                lse2 = jnp.concatenate([lse_t[:, :1], lse_t[:, HEAD_DIM_B:HEAD_DIM_B + 1]], axis=0)
                cur, prev = _band_masks(n, True)
                dq2 = jnp.zeros((2 * BLOCK, LANES), F32)
                for rows, mask in ((rc, cur), (rp, prev)):
                    kb = kg[rows, :].astype(BF16)
                    vb = vg[rows, :].astype(BF16)
                    mask2 = jnp.concatenate([mask, mask], 0)
                    s = jnp.where(mask2, _dot_nt(q2, kb) * scale, NEG_INF)
                    p = jnp.exp(s - lse2)
                    ds = (p * (_dot_nt(dy2b, vb) - dsum) * scale).astype(BF16)
                    dv_acc[g, rows, :] += _fold_halves(_dot_tn(p.astype(BF16), dy2b))
                    dk_acc[g, rows, :] += _fold_halves(_dot_tn(ds, q2))
                    dq2 = dq2 + _dot(ds, kb)
                dq_ref[rc, :] = _unstack_heads(dq2)
                return dsink - jnp.exp(sink - lse2) * dsum

            dsink = lax.fori_loop(0, nblk, blk, jnp.zeros((2 * BLOCK, 1), F32))
            dq_ref[...] = _rope_t(dq_ref[...], c_ref[...], s_ref[...], half, HEAD_DIM_B)
            d0 = jnp.sum(dsink[:BLOCK], axis=0, keepdims=True)
            d1 = jnp.sum(dsink[BLOCK:], axis=0, keepdims=True)
            dsink_ref[pl.ds(2 * t, 1), :] = jnp.broadcast_to(d0, (1, LANES))
            dsink_ref[pl.ds(2 * t + 1, 1), :] = jnp.broadcast_to(d1, (1, LANES))

        low = _lane((S, LANES)) < HEAD_DIM_B

        @pl.when(t == n_qtiles)
        def _():
            dk = jnp.where(low, dk_acc[0], dk_acc[1])
            dkv_ref[...] = _rope_t(dk, c_ref[...], s_ref[...], half, HEAD_DIM_B)

        @pl.when(t == n_qtiles + 1)
        def _():
            dkv_ref[...] = jnp.where(low, dv_acc[0], dv_acc[1])

    qt = lambda t: jnp.minimum(t, n_qtiles - 1)
    full = lambda col: pl.BlockSpec((S, LANES), lambda t: (0, col))
    per_tile = pl.BlockSpec((S, LANES), lambda t: (0, qt(t)))
    return pl.pallas_call(
        body, name=name, grid=(n_steps,),
        in_specs=[pl.BlockSpec((S, LANES), lambda t: (0, off_q + qt(t))), full(off_q + n_qtiles),
                  full(off_q + n_qtiles + 1), full(0), full(0), pl.BlockSpec(sinks_l.shape, lambda t: (0, 0)),
                  per_tile, per_tile, per_tile],
        out_specs=(per_tile, pl.BlockSpec((S, LANES), lambda t: (0, jnp.maximum(t - n_qtiles, 0))),
                   pl.BlockSpec(sinks_l.shape, lambda t: (0, 0))),
        out_shape=(jax.ShapeDtypeStruct((S, n_qtiles * LANES), F32), jax.ShapeDtypeStruct((S, 2 * LANES), F32),
                   jax.ShapeDtypeStruct(sinks_l.shape, F32)),
        scratch_shapes=[pltpu.VMEM((S, LANES), F32) for _ in range(3)]
        + [pltpu.VMEM((2, S, LANES), F32), pltpu.VMEM((2, S, LANES), F32)],
        compiler_params=_params(("arbitrary",)),
    )(h, h, h, cb, sb, sinks_l, yb, lse, dyb)


def _win(r, nw, d):
    if d > 1:
        return pl.ds(r + nw * (BLOCK * d), 2 * BLOCK, stride=d)
    return pl.ds(pl.multiple_of(nw * BLOCK, BLOCK), 2 * BLOCK)


def _fwd_bias(strict_prev):
    qi = lax.broadcasted_iota(jnp.int32, (BLOCK, 2 * BLOCK), 0)
    kj = lax.broadcasted_iota(jnp.int32, (BLOCK, 2 * BLOCK), 1)
    first = kj < BLOCK
    kk = jnp.where(first, kj, kj - BLOCK)
    prev_ok = (kk > qi) if strict_prev else (kk >= qi)
    zero = first & (kk <= qi)
    mid = (first & prev_ok) | (jnp.logical_not(first) & (kk <= qi))
    return jnp.stack([jnp.where(zero, 0.0, NEG_INF), jnp.where(mid, 0.0, NEG_INF)])


def _bwd_bias(strict_prev):
    qi = lax.broadcasted_iota(jnp.int32, (2 * BLOCK, BLOCK), 0)
    kj = lax.broadcasted_iota(jnp.int32, (2 * BLOCK, BLOCK), 1)
    first = qi < BLOCK
    qq = jnp.where(first, qi, qi - BLOCK)
    prev_ok = (kj > qq) if strict_prev else (kj >= qq)
    mid = (first & (kj <= qq)) | (jnp.logical_not(first) & prev_ok)
    last = jnp.logical_not(first) & (kj <= qq)
    return jnp.stack([jnp.where(mid, 0.0, NEG_INF), jnp.where(last, 0.0, NEG_INF)])


def _pair_bias():
    qi = lax.broadcasted_iota(jnp.int32, (2 * BLOCK, 2 * BLOCK), 0)
    kj = lax.broadcasted_iota(jnp.int32, (2 * BLOCK, 2 * BLOCK), 1)
    return jnp.where((kj <= qi) & (qi - kj <= BLOCK), 0.0, NEG_INF)


UNROLL_PAIR = 2


UNROLL = 4


def attn_a_fwd(h, ca, sa, *, n_heads, name):
    S = h.shape[0]
    scale = HEAD_DIM_A ** -0.5
    half = HEAD_DIM_A // 8
    branches = _branch_blocks(S)

    def body(q_ref, k_ref, v_ref, c_ref, s_ref, y_ref, lse_ref, qs, ks, m_s, l_s, acc_s, bias, pbias):
        qs[...] = _rope(q_ref[...], c_ref[...], s_ref[...], half, HEAD_DIM_A)
        ks[...] = _rope(k_ref[...], c_ref[...], s_ref[...], half, HEAD_DIM_A)
        bias[...] = _fwd_bias(False)
        pbias[...] = _pair_bias()
        for bi, (d, nb) in enumerate(branches):
            assert nb >= 2 and (d * nb) % UNROLL == 0
            if nb == 2 and bi > 0:
                assert d % UNROLL_PAIR == 0

                def pair(it, carry, d=d):
                    rws = [_win(it * UNROLL_PAIR + u, 0, d) for u in range(UNROLL_PAIR)]
                    ss = [_dot_nt(qs[rw, :].astype(BF16), ks[rw, :].astype(BF16)) * scale + pbias[...] for rw in rws]
                    ms = [jnp.max(s, axis=-1, keepdims=True) for s in ss]
                    ps = [jnp.exp(s - m) for s, m in zip(ss, ms)]
                    ls = [jnp.sum(p, axis=-1, keepdims=True) for p in ps]
                    accs = [_dot(p.astype(BF16), v_ref[rw, :].astype(BF16)) for p, rw in zip(ps, rws)]
                    for rw, m, l, acc in zip(rws, ms, ls, accs):
                        mb = jnp.broadcast_to(m, (2 * BLOCK, LANES))
                        m0 = m_s[rw, :]
                        mn = jnp.maximum(m0, mb)
                        a0, a1 = jnp.exp(m0 - mn), jnp.exp(mb - mn)
                        m_s[rw, :] = mn
                        l_s[rw, :] = l_s[rw, :] * a0 + jnp.broadcast_to(l, (2 * BLOCK, LANES)) * a1
                        acc_s[rw, :] = acc_s[rw, :] * a0 + acc * a1
                    return carry

                lax.fori_loop(0, d // UNROLL_PAIR, pair, 0)
                continue

            def blk(it, carry, bi=bi, d=d, nb=nb):
                rn = [((it * UNROLL + u) // nb, (it * UNROLL + u) % nb) for u in range(UNROLL)]
                rcs = [_rows(r, n, d) for r, n in rn]
                rws = [_win(r, jnp.maximum(n - 1, 0), d) for r, n in rn]
                ss = [_dot_nt(qs[rc, :].astype(BF16), ks[rw, :].astype(BF16)) * scale + bias[jnp.minimum(n, 1)]
                      for (r, n), rc, rw in zip(rn, rcs, rws)]
                ms = [jnp.max(s, axis=-1, keepdims=True) for s in ss]
                ps = [jnp.exp(s - m) for s, m in zip(ss, ms)]
                ls = [jnp.sum(p, axis=-1, keepdims=True) for p in ps]
                accs = [_dot(p.astype(BF16), v_ref[rw, :].astype(BF16)) for p, rw in zip(ps, rws)]
                for rc, m, l, acc in zip(rcs, ms, ls, accs):
                    mb = jnp.broadcast_to(m, (BLOCK, LANES))
                    lb = jnp.broadcast_to(l, (BLOCK, LANES))
                    if bi == 0:
                        m_s[rc, :], l_s[rc, :], acc_s[rc, :] = mb, lb, acc
                    else:
                        m0 = m_s[rc, :]
                        mn = jnp.maximum(m0, mb)
                        a0, a1 = jnp.exp(m0 - mn), jnp.exp(mb - mn)
                        m_s[rc, :] = mn
                        l_s[rc, :] = l_s[rc, :] * a0 + lb * a1
                        acc_s[rc, :] = acc_s[rc, :] * a0 + acc * a1
                return carry

            lax.fori_loop(0, d * nb // UNROLL, blk, 0)
        y_ref[...] = acc_s[...] / l_s[...]
        lse_ref[...] = m_s[...] + jnp.log(l_s[...])

    col = lambda off: pl.BlockSpec((S, LANES), lambda hd: (0, off + hd))
    full = pl.BlockSpec((S, LANES), lambda hd: (0, 0))
    out = pl.BlockSpec((S, LANES), lambda hd: (0, hd))
    o_shape = jax.ShapeDtypeStruct((S, n_heads * LANES), F32)
    return pl.pallas_call(
        body, name=name, grid=(n_heads,),
        in_specs=[col(0), col(n_heads), col(2 * n_heads), full, full],
        out_specs=(out, out), out_shape=(o_shape, o_shape),
        scratch_shapes=[pltpu.VMEM((S, LANES), F32) for _ in range(5)]
        + [pltpu.VMEM((2, BLOCK, 2 * BLOCK), F32), pltpu.VMEM((2 * BLOCK, 2 * BLOCK), F32)],
        compiler_params=_params(("arbitrary",)),
    )(h, h, h, ca, sa)


def attn_a_bwd(h, ca, sa, ya, lse, dya, *, n_heads, name):
    S = h.shape[0]
    scale = HEAD_DIM_A ** -0.5
    half = HEAD_DIM_A // 8
    branches = _branch_blocks(S)

    def body(q_ref, k_ref, v_ref, c_ref, s_ref, y_ref, lse_ref, dy_ref, o_ref, qs, ks, dq_s, dk_s, dv_s, bias, pbias):
        part = pl.program_id(1)

        @pl.when(part == 0)
        def _():
            qs[...] = _rope(q_ref[...], c_ref[...], s_ref[...], half, HEAD_DIM_A)
            ks[...] = _rope(k_ref[...], c_ref[...], s_ref[...], half, HEAD_DIM_A)
            dq_s[...] = jnp.zeros_like(dq_s)
            bias[...] = _bwd_bias(False)
            pbias[...] = _pair_bias()
            for bi, (d, nb) in enumerate(branches):
                assert nb >= 2 and (d * nb) % UNROLL == 0
                if nb == 2 and bi > 0:
                    assert d % UNROLL_PAIR == 0

                    def pair(it, carry, d=d):
                        rws = [_win(it * UNROLL_PAIR + u, 0, d) for u in range(UNROLL_PAIR)]
                        q2 = [qs[rw, :].astype(BF16) for rw in rws]
                        k2 = [ks[rw, :].astype(BF16) for rw in rws]
                        dy2 = [dy_ref[rw, :] for rw in rws]
                        dsum = [jnp.sum(dy * y_ref[rw, :], axis=-1, keepdims=True) for dy, rw in zip(dy2, rws)]
                        dy2b = [dy.astype(BF16) for dy in dy2]
                        ss = [_dot_nt(q, k) * scale + pbias[...] for q, k in zip(q2, k2)]
                        dps = [_dot_nt(dy, v_ref[rw, :].astype(BF16)) for dy, rw in zip(dy2b, rws)]
                        ps = [jnp.exp(s - lse_ref[rw, :][:, :1]) for s, rw in zip(ss, rws)]
                        dss = [(p * (dp - dm) * scale).astype(BF16) for p, dp, dm in zip(ps, dps, dsum)]
                        dvs = [_dot_tn(p.astype(BF16), dy) for p, dy in zip(ps, dy2b)]
                        dks = [_dot_tn(ds, q) for ds, q in zip(dss, q2)]
                        dqs = [_dot(ds, k) for ds, k in zip(dss, k2)]
                        for rw, dv, dk, dq in zip(rws, dvs, dks, dqs):
                            dv_s[rw, :] += dv
                            dk_s[rw, :] += dk
                            dq_s[rw, :] += dq
                        return carry

                    lax.fori_loop(0, d // UNROLL_PAIR, pair, 0)
                    continue

                def blk(it, carry, bi=bi, d=d, nb=nb):
                    rj = [((it * UNROLL + u) // nb, (it * UNROLL + u) % nb) for u in range(UNROLL)]
                    rks = [_rows(r, j, d) for r, j in rj]
                    rws = [_win(r, jnp.minimum(j, nb - 2), d) for r, j in rj]
                    q2 = [qs[rw, :].astype(BF16) for rw in rws]
                    kb = [ks[rk, :].astype(BF16) for rk in rks]
                    dy2 = [dy_ref[rw, :] for rw in rws]
                    dsum = [jnp.sum(dy * y_ref[rw, :], axis=-1, keepdims=True) for dy, rw in zip(dy2, rws)]
                    dy2b = [dy.astype(BF16) for dy in dy2]
                    ss = [_dot_nt(q, k) * scale + bias[(j == nb - 1).astype(jnp.int32)] for q, k, (r, j) in zip(q2, kb, rj)]
                    dps = [_dot_nt(dy, v_ref[rk, :].astype(BF16)) for dy, rk in zip(dy2b, rks)]
                    ps = [jnp.exp(s - lse_ref[rw, :]) for s, rw in zip(ss, rws)]
                    dss = [(p * (dp - dm) * scale).astype(BF16) for p, dp, dm in zip(ps, dps, dsum)]
                    dvs = [_dot_tn(p.astype(BF16), dy) for p, dy in zip(ps, dy2b)]
                    dks = [_dot_tn(ds, q) for ds, q in zip(dss, q2)]
                    dqs = [_dot(ds, k) for ds, k in zip(dss, kb)]
                    for rk, rw, dv, dk, dq in zip(rks, rws, dvs, dks, dqs):
                        if bi == 0:
                            dv_s[rk, :], dk_s[rk, :] = dv, dk
                        else:
                            dv_s[rk, :] += dv
                            dk_s[rk, :] += dk
                        dq_s[rw, :] += dq
                    return carry

                lax.fori_loop(0, d * nb // UNROLL, blk, 0)
            o_ref[...] = _rope_t(dq_s[...], c_ref[...], s_ref[...], half, HEAD_DIM_A).astype(o_ref.dtype)

        @pl.when(part == 1)
        def _():
            o_ref[...] = _rope_t(dk_s[...], c_ref[...], s_ref[...], half, HEAD_DIM_A).astype(o_ref.dtype)

        @pl.when(part == 2)
        def _():
            o_ref[...] = dv_s[...].astype(o_ref.dtype)

    col = lambda off: pl.BlockSpec((S, LANES), lambda hd, p: (0, off + hd))
    full = pl.BlockSpec((S, LANES), lambda hd, p: (0, 0))
    per_head = pl.BlockSpec((S, LANES), lambda hd, p: (0, hd))
    return pl.pallas_call(
        body, name=name, grid=(n_heads, 3),
        in_specs=[col(0), col(n_heads), col(2 * n_heads), full, full, per_head, per_head, per_head],
        out_specs=pl.BlockSpec((S, LANES), lambda hd, p: (0, p * n_heads + hd)),
        out_shape=jax.ShapeDtypeStruct((S, h.shape[1]), BF16),
        scratch_shapes=[pltpu.VMEM((S, LANES), F32) for _ in range(5)]
        + [pltpu.VMEM((2, 2 * BLOCK, BLOCK), F32), pltpu.VMEM((2 * BLOCK, 2 * BLOCK), F32)],
        compiler_params=_params(("arbitrary", "arbitrary")),
    )(h, h, h, ca, sa, ya, lse, dya)


def _unstack_heads(t2):
    rows = t2.shape[0] // 2
    low = _lane((rows, LANES)) < HEAD_DIM_B
    return jnp.where(low, t2[:rows], t2[rows:])


def _head_rows(ref, tile, rows):
    a = ref[pl.ds(2 * tile, 1), :][:, :1]
    b = ref[pl.ds(2 * tile + 1, 1), :][:, :1]
    return jnp.concatenate([jnp.broadcast_to(a, (rows, 1)), jnp.broadcast_to(b, (rows, 1))], axis=0)


UNROLL_B = 2
UNROLL_B_BWD = 4


def attn_b_fwd(h, cb, sb, sinks_l, *, off_q, n_qtiles, name):
    S = h.shape[0]
    nblk = S // BLOCK
    scale = HEAD_DIM_B ** -0.5
    half = HEAD_DIM_B // 8
    tiles_per_group = n_qtiles // 2
    assert nblk >= 2 and nblk % UNROLL_B == 0

    def body(q_ref, k_ref, v_ref, c_ref, s_ref, sink_ref, y_ref, lse_ref, qs, kg, vg, bias):
        t = pl.program_id(0)
        g = t // tiles_per_group
        qs[...] = _rope(q_ref[...], c_ref[...], s_ref[...], half, HEAD_DIM_B)

        @pl.when(t % tiles_per_group == 0)
        def _():
            kg[...] = _both_halves(_rope(k_ref[...], c_ref[...], s_ref[...], half, HEAD_DIM_B), g)
            vg[...] = _both_halves(v_ref[...], g)

        @pl.when(t == 0)
        def _():
            fb = _fwd_bias(True)
            bias[...] = jnp.concatenate([fb, fb], axis=1)

        sink = _head_rows(sink_ref, t, BLOCK)

        def blk(it, carry):
            ns = [it * UNROLL_B + u for u in range(UNROLL_B)]
            rcs = [pl.ds(pl.multiple_of(n * BLOCK, BLOCK), BLOCK) for n in ns]
            rws = [pl.ds(pl.multiple_of(jnp.maximum(n - 1, 0) * BLOCK, BLOCK), 2 * BLOCK) for n in ns]
            ss = [_dot_nt(_stack_heads(qs[rc, :]).astype(BF16), kg[rw, :].astype(BF16)) * scale + bias[jnp.minimum(n, 1)]
                  for n, rc, rw in zip(ns, rcs, rws)]
            ms = [jnp.max(s, axis=-1, keepdims=True) for s in ss]
            ps = [jnp.exp(s - m) for s, m in zip(ss, ms)]
            ls = [jnp.sum(p, axis=-1, keepdims=True) for p in ps]
            accs = [_dot(p.astype(BF16), vg[rw, :].astype(BF16)) for p, rw in zip(ps, rws)]
            for rc, m, l, acc in zip(rcs, ms, ls, accs):
                m2 = jnp.maximum(m, sink)
                c = jnp.exp(m - m2)
                den = l * c + jnp.exp(sink - m2)
                y_ref[rc, :] = _unstack_heads(acc * (c / den))
                lse_ref[rc, :] = _unstack_heads(jnp.broadcast_to(m2 + jnp.log(den), (2 * BLOCK, LANES)))
            return carry

        lax.fori_loop(0, nblk // UNROLL_B, blk, 0)

    full = lambda col: pl.BlockSpec((S, LANES), lambda t: (0, col))
    out = pl.BlockSpec((S, LANES), lambda t: (0, t))
    o_shape = jax.ShapeDtypeStruct((S, n_qtiles * LANES), F32)
    return pl.pallas_call(
        body, name=name, grid=(n_qtiles,),
        in_specs=[pl.BlockSpec((S, LANES), lambda t: (0, off_q + t)), full(off_q + n_qtiles), full(off_q + n_qtiles + 1),
                  full(0), full(0), pl.BlockSpec(sinks_l.shape, lambda t: (0, 0))],
        out_specs=(out, out), out_shape=(o_shape, o_shape),
        scratch_shapes=[pltpu.VMEM((S, LANES), F32) for _ in range(3)] + [pltpu.VMEM((2, 2 * BLOCK, 2 * BLOCK), F32)],
        compiler_params=_params(("arbitrary",)),
    )(h, h, h, cb, sb, sinks_l)


def attn_b_bwd(h, cb, sb, sinks_l, yb, lse, dyb, dh, *, off_q, n_qtiles, name):
    S = h.shape[0]
    nblk = S // BLOCK
    scale = HEAD_DIM_B ** -0.5
    half = HEAD_DIM_B // 8
    tiles_per_group = n_qtiles // 2
    n_steps = n_qtiles + 2
    W = 2 * BLOCK
    assert nblk >= 2 and nblk % UNROLL_B == 0

    def body(q_ref, k_ref, v_ref, c_ref, s_ref, sink_ref, y_ref, lse_ref, dy_ref, dh_in,
             o_ref, dsink_ref, qs, kg, vg, dk_acc, dv_acc, bias, dq_ref):
        t = pl.program_id(0)

        @pl.when(t == 0)
        def _():
            dk_acc[...] = jnp.zeros_like(dk_acc)
            dv_acc[...] = jnp.zeros_like(dv_acc)
            dsink_ref[...] = jnp.zeros_like(dsink_ref)
            bb = _bwd_bias(True)
            bias[...] = jnp.concatenate([bb, bb], axis=1)

        @pl.when(t < n_qtiles)
        def _():
            g = t // tiles_per_group
            qs[...] = _rope(q_ref[...], c_ref[...], s_ref[...], half, HEAD_DIM_B)

            @pl.when(t % tiles_per_group == 0)
            def _():
                kg[...] = _both_halves(_rope(k_ref[...], c_ref[...], s_ref[...], half, HEAD_DIM_B), g)
                vg[...] = _both_halves(v_ref[...], g)

            dq_ref[...] = jnp.zeros_like(dq_ref)
            sink = _head_rows(sink_ref, t, W)
            row = lax.broadcasted_iota(jnp.int32, (2 * W, 1), 0) % W
            low = _lane((W, LANES)) < HEAD_DIM_B

            def blk(it, dsink):
                js = [it * UNROLL_B_BWD + u for u in range(UNROLL_B_BWD)]
                rks = [pl.ds(pl.multiple_of(j * BLOCK, BLOCK), BLOCK) for j in js]
                rws = [pl.ds(pl.multiple_of(jnp.minimum(j, nblk - 2) * BLOCK, BLOCK), W) for j in js]
                q2 = [_stack_heads(qs[rw, :]).astype(BF16) for rw in rws]
                dy2 = [_stack_heads(dy_ref[rw, :]) for rw in rws]
                dsum = [jnp.sum(dy * _stack_heads(y_ref[rw, :]), axis=-1, keepdims=True) for dy, rw in zip(dy2, rws)]
                dy2b = [dy.astype(BF16) for dy in dy2]
                lse2 = []
                for rw in rws:
                    lt = lse_ref[rw, :]
                    lr = pltpu.roll(lt, HEAD_DIM_B, 1)
                    lse2.append(jnp.concatenate([jnp.where(low, lt, lr), jnp.where(low, lr, lt)], axis=0))
                kb = [kg[rk, :].astype(BF16) for rk in rks]
                ss = [_dot_nt(q, k) * scale + bias[(j == nblk - 1).astype(jnp.int32)] for q, k, j in zip(q2, kb, js)]
                dps = [_dot_nt(dy, vg[rk, :].astype(BF16)) for dy, rk in zip(dy2b, rks)]
                ps = [jnp.exp(s - l2) for s, l2 in zip(ss, lse2)]
                dss = [(p * (dp - dm) * scale).astype(BF16) for p, dp, dm in zip(ps, dps, dsum)]
                dvs = [_fold_halves(_dot_tn(p.astype(BF16), dy)) for p, dy in zip(ps, dy2b)]
                dks = [_fold_halves(_dot_tn(ds, q)) for ds, q in zip(dss, q2)]
                dqs = [_dot(ds, k) for ds, k in zip(dss, kb)]
                for j, rk, rw, dv, dk, dq, l2, dm in zip(js, rks, rws, dvs, dks, dqs, lse2, dsum):
                    dv_acc[g, rk, :] += dv
                    dk_acc[g, rk, :] += dk
                    dq_ref[rw, :] += _unstack_heads(dq)
                    diag = (row >= BLOCK).astype(jnp.int32) == (j == nblk - 1).astype(jnp.int32)
                    dsink = dsink - jnp.where(diag, jnp.exp(sink - l2[:, :1]) * dm, 0.0)
                return dsink

            assert nblk % UNROLL_B_BWD == 0
            dsink = lax.fori_loop(0, nblk // UNROLL_B_BWD, blk, jnp.zeros((2 * W, 1), F32))
            o_ref[...] = _rope_t(dq_ref[...], c_ref[...], s_ref[...], half, HEAD_DIM_B).astype(o_ref.dtype)
            d0 = jnp.sum(dsink[:W], axis=0, keepdims=True)
            d1 = jnp.sum(dsink[W:], axis=0, keepdims=True)
            dsink_ref[pl.ds(2 * t, 1), :] = jnp.broadcast_to(d0, (1, LANES))
            dsink_ref[pl.ds(2 * t + 1, 1), :] = jnp.broadcast_to(d1, (1, LANES))

        low_s = _lane((S, LANES)) < HEAD_DIM_B

        @pl.when(t == n_qtiles)
        def _():
            dk = jnp.where(low_s, dk_acc[0], dk_acc[1])
            o_ref[...] = _rope_t(dk, c_ref[...], s_ref[...], half, HEAD_DIM_B).astype(o_ref.dtype)

        @pl.when(t == n_qtiles + 1)
        def _():
            o_ref[...] = jnp.where(low_s, dv_acc[0], dv_acc[1]).astype(o_ref.dtype)

    qt = lambda t: jnp.minimum(t, n_qtiles - 1)
    full = lambda col: pl.BlockSpec((S, LANES), lambda t: (0, col))
    per_tile = pl.BlockSpec((S, LANES), lambda t: (0, qt(t)))
    return pl.pallas_call(
        body, name=name, grid=(n_steps,),
        in_specs=[pl.BlockSpec((S, LANES), lambda t: (0, off_q + qt(t))), full(off_q + n_qtiles),
                  full(off_q + n_qtiles + 1), full(0), full(0), pl.BlockSpec(sinks_l.shape, lambda t: (0, 0)),
                  per_tile, per_tile, per_tile, pl.BlockSpec(memory_space=pl.ANY)],
        out_specs=(pl.BlockSpec((S, LANES), lambda t: (0, off_q + t)), pl.BlockSpec(sinks_l.shape, lambda t: (0, 0))),
        out_shape=(jax.ShapeDtypeStruct(dh.shape, dh.dtype), jax.ShapeDtypeStruct(sinks_l.shape, F32)),
        input_output_aliases={9: 0},
        scratch_shapes=[pltpu.VMEM((S, LANES), F32) for _ in range(3)]
        + [pltpu.VMEM((2, S, LANES), F32), pltpu.VMEM((2, S, LANES), F32), pltpu.VMEM((2, 2 * W, BLOCK), F32),
           pltpu.VMEM((S, LANES), F32)],
        compiler_params=_params(("arbitrary",)),
    )(h, h, h, cb, sb, sinks_l, yb, lse, dyb, dh)


def mem_attn_fwd(q, kv, *, name, tm=512):
    S, D = q.shape
    n_mem = kv.shape[0]
    hd = D // N_MEM_HEADS
    scale = hd ** -0.5
    tm = _tile(S, tm, 8)

    def body(q_ref, kv_ref, o_ref):
        for hh in range(N_MEM_HEADS):
            cols = slice(hh * hd, (hh + 1) * hd)
            s = _dot_nt(q_ref[:, cols], kv_ref[:, cols]) * scale
            s = s - jnp.max(s, axis=-1, keepdims=True)
            e = jnp.exp(s)
            p = e / jnp.sum(e, axis=-1, keepdims=True)
            o_ref[:, cols] = _dot(p.astype(BF16), kv_ref[:, D + hh * hd:D + (hh + 1) * hd]).astype(o_ref.dtype)

    return pl.pallas_call(
        body, name=name, grid=(S // tm,),
        in_specs=[pl.BlockSpec((tm, D), lambda i: (i, 0)), pl.BlockSpec((n_mem, 2 * D), lambda i: (0, 0))],
        out_specs=pl.BlockSpec((tm, D), lambda i: (i, 0)),
        out_shape=jax.ShapeDtypeStruct((S, D), BF16),
        compiler_params=_params(("parallel",)),
    )(q, kv)


def mem_attn_bwd(q, kv, do, *, name, tm=512):
    S, D = q.shape
    n_mem = kv.shape[0]
    hd = D // N_MEM_HEADS
    scale = hd ** -0.5
    tm = _tile(S, tm, 8)

    def body(q_ref, kv_ref, do_ref, dq_ref, dkv_ref):
        i = pl.program_id(0)

        @pl.when(i == 0)
        def _():
            dkv_ref[...] = jnp.zeros_like(dkv_ref)

        for hh in range(N_MEM_HEADS):
            cols = slice(hh * hd, (hh + 1) * hd)
            vcols = slice(D + hh * hd, D + (hh + 1) * hd)
            qh, kh, vh, doh = q_ref[:, cols], kv_ref[:, cols], kv_ref[:, vcols], do_ref[:, cols]
            s = _dot_nt(qh, kh) * scale
            s = s - jnp.max(s, axis=-1, keepdims=True)
            e = jnp.exp(s)
            p = e / jnp.sum(e, axis=-1, keepdims=True)
            dp = _dot_nt(doh, vh)
            ds = (p * (dp - jnp.sum(dp * p, axis=-1, keepdims=True)) * scale).astype(BF16)
            dq_ref[:, cols] = _dot(ds, kh).astype(dq_ref.dtype)
            dkv_ref[:, cols] += _dot_tn(ds, qh)
            dkv_ref[:, vcols] += _dot_tn(p.astype(BF16), doh)

    row = pl.BlockSpec((tm, D), lambda i: (i, 0))
    kvs = pl.BlockSpec((n_mem, 2 * D), lambda i: (0, 0))
    return pl.pallas_call(
        body, name=name, grid=(S // tm,),
        in_specs=[row, kvs, row], out_specs=(row, kvs),
        out_shape=(jax.ShapeDtypeStruct((S, D), BF16), jax.ShapeDtypeStruct((n_mem, 2 * D), F32)),
        compiler_params=_params(("arbitrary",)),
    )(q, kv, do)


def _rows_view(t):
    return t.reshape(-1, t.shape[-1])


def _row_tile(rows, cols, target_elems=512 * 1024):
    return _tile(rows, max(8, target_elems // cols), 8)


def cast_bf16(w, *, name):
    v = _rows_view(w)
    R, C = v.shape
    tr = _row_tile(R, C)

    def body(w_ref, o_ref):
        o_ref[...] = w_ref[...].astype(BF16)

    spec = pl.BlockSpec((tr, C), lambda i: (i, 0))
    out = pl.pallas_call(body, name=name, grid=(R // tr,), in_specs=[spec], out_specs=spec,
                         out_shape=jax.ShapeDtypeStruct((R, C), BF16), compiler_params=_params(("parallel",)))(v)
    return out.reshape(w.shape)


def mesh_place():
    return tuple(lax.axis_index(a).astype(jnp.int32).reshape(1) for a in ("x", "y", "c"))


def pair_sum(p, r1, place, *, name):
    nsh, r, c = p.shape
    hr = r // 2
    tr = _row_tile(hr, c)
    nt = hr // tr

    def body(x_ref, y_ref, c_ref, p_ref, r_ref, o_ref):
        o_ref[...] = (p_ref[...].astype(F32) + r_ref[...].astype(F32)).astype(BF16)

    return pl.pallas_call(
        body, name=name,
        grid_spec=pltpu.PrefetchScalarGridSpec(
            num_scalar_prefetch=3, grid=(nsh, nt),
            in_specs=[pl.BlockSpec((None, tr, c), lambda s, i, x, y, cc: (s, cc[0] * nt + i, 0)),
                      pl.BlockSpec((None, tr, c), lambda s, i, x, y, cc: (s, i, 0))],
            out_specs=pl.BlockSpec((None, tr, c), lambda s, i, x, y, cc: (s, i, 0))),
        out_shape=jax.ShapeDtypeStruct((nsh, hr, c), BF16),
        compiler_params=_params(("parallel", "parallel")),
    )(*place, p, r1)


def cast_into_slot(w, li, place, *, name):
    _, R, C = w.shape
    tr = _row_tile(R, C)

    def body(x_ref, y_ref, c_ref, w_ref, o_ref):
        o_ref[...] = w_ref[...].astype(BF16)

    return pl.pallas_call(
        body, name=name,
        grid_spec=pltpu.PrefetchScalarGridSpec(
            num_scalar_prefetch=3, grid=(R // tr,),
            in_specs=[pl.BlockSpec((None, tr, C), lambda i, x, y, cc: (li, i, 0))],
            out_specs=pl.BlockSpec((None, tr, C), lambda i, x, y, cc: (2 * x[0] + y[0], i, 0))),
        out_shape=jax.ShapeDtypeStruct((N_CHIPS, R, C), BF16),
        compiler_params=_params(("parallel",)),
    )(*place, w)


def chip_sum(q, r2, place, gbuf, li, *, name):
    _, hr, c = q.shape
    tr = _row_tile(hr, c, 256 * 1024)
    nt = hr // tr

    def body(x_ref, y_ref, c_ref, q_ref, r_ref, g_in, o_ref):
        acc = q_ref[...].astype(F32)
        for k in range(r_ref.shape[0]):
            acc = acc + r_ref[k].astype(F32)
        o_ref[...] = acc

    return pl.pallas_call(
        body, name=name,
        grid_spec=pltpu.PrefetchScalarGridSpec(
            num_scalar_prefetch=3, grid=(nt,),
            in_specs=[pl.BlockSpec((None, tr, c), lambda i, x, y, cc: (2 * x[0] + y[0], i, 0)),
                      pl.BlockSpec((r2.shape[0], tr, c), lambda i, x, y, cc: (0, i, 0)),
                      pl.BlockSpec(memory_space=pl.ANY)],
            out_specs=pl.BlockSpec((None, tr, c), lambda i, x, y, cc: (li, cc[0] * nt + i, 0))),
        out_shape=jax.ShapeDtypeStruct(gbuf.shape, F32),
        input_output_aliases={5: 0},
        compiler_params=_params(("parallel",)),
    )(*place, q, r2, gbuf)


def adamw(w, g, m, v, *, name, emit_g=False):
    shape = w.shape
    wv, gv, mv, vv = (_rows_view(t) for t in (w, g, m, v))
    R, C = wv.shape
    tr = _row_tile(R, C, 256 * 1024)
    c1 = 1.0 / (1.0 - ADAM_B1 ** ADAM_STEP)
    c2 = 1.0 / (1.0 - ADAM_B2 ** ADAM_STEP)
    n_out = 4 if emit_g else 3

    def body(w_ref, g_ref, m_ref, v_ref, d_ref, nm_ref, nv_ref, *go_ref):
        g_ = g_ref[...]
        nm = ADAM_B1 * m_ref[...] + (1.0 - ADAM_B1) * g_
        nv = ADAM_B2 * v_ref[...] + (1.0 - ADAM_B2) * (g_ * g_)
        m_hat = nm * c1
        v_hat = nv * c2
        d_ref[...] = -ADAM_LR * (m_hat / (jnp.sqrt(v_hat) + ADAM_EPS) + ADAM_WD * w_ref[...])
        nm_ref[...] = nm
        nv_ref[...] = nv
        if emit_g:
            go_ref[0][...] = g_

    spec = pl.BlockSpec((tr, C), lambda i: (i, 0))
    o = jax.ShapeDtypeStruct((R, C), F32)
    outs = pl.pallas_call(body, name=name, grid=(R // tr,), in_specs=[spec] * 4, out_specs=(spec,) * n_out,
                          out_shape=(o,) * n_out, compiler_params=_params(("parallel",)))(wv, gv, mv, vv)
    return tuple(t.reshape(shape) for t in outs)


def _place():
    x, y, c = lax.axis_index("x"), lax.axis_index("y"), lax.axis_index("c")
    others = [(1 - x, y), (x, 1 - y), (1 - x, 1 - y)]
    return x, y, c, others


def _any_specs(n):
    return [pl.BlockSpec(memory_space=pl.ANY) for _ in range(n)]


HBM_SPEC = pl.BlockSpec(memory_space=pltpu.HBM)
SEM_SPEC = pl.BlockSpec(memory_space=pltpu.SEMAPHORE)
DATAFLOW = pltpu.SideEffectType.DATAFLOW_SIDE_EFFECTING
TOKEN = jax.ShapeDtypeStruct((8, LANES), F32)


def _in_hbm(arrays):
    return [pltpu.with_memory_space_constraint(a, pltpu.HBM) for a in arrays]


def _gather_copy(g, t, j, slot, px, py, c, send, recv):
    hr = g[t].shape[1] // 2
    rows = g[t].at[slot, pl.ds(c * hr, hr)]
    return pltpu.make_async_remote_copy(rows, rows, send.at[3 * t + j], recv.at[3 * t + j], device_id=(px, py, c), device_id_type=MESH)


def gather_start(gs, after, *, name):
    n = len(gs)

    def body(*refs):
        g, token = refs[:n], refs[-1]
        send, recv = refs[n + 1], refs[n + 2]
        x, y, c, others = _place()
        for t in range(n):
            for j, (px, py) in enumerate(others):
                _gather_copy(g, t, j, 2 * x + y, px, py, c, send, recv).start()
        token[...] = jnp.zeros_like(token)

    outs = pl.pallas_call(
        body, name=name,
        in_specs=[HBM_SPEC] * n + [pl.BlockSpec(memory_space=pl.ANY)],
        out_specs=(SEM_SPEC, SEM_SPEC, *[HBM_SPEC] * n, pl.BlockSpec(memory_space=pltpu.VMEM)),
        out_shape=(pltpu.SemaphoreType.DMA((3 * n,)), pltpu.SemaphoreType.DMA((3 * n,)),
                   *[pltpu.HBM(g.shape, g.dtype) for g in gs], TOKEN),
        input_output_aliases={t: 2 + t for t in range(n)},
        compiler_params=pltpu.CompilerParams(has_side_effects=DATAFLOW),
    )(*_in_hbm(gs), after)
    return outs[0], outs[1], list(outs[2:2 + n]), outs[-1]


def gather_wait(send, recv, gs, after, *, name):
    n = len(gs)

    def body(*refs):
        g, token = refs[:n], refs[-1]
        send, recv = refs[n], refs[n + 1]
        x, y, c, others = _place()
        for t in range(n):
            for j, (px, py) in enumerate(others):
                _gather_copy(g, t, j, 2 * x + y, px, py, c, send, recv).wait_send()
                _gather_copy(g, t, j, 2 * px + py, px, py, c, send, recv).wait_recv()
        token[...] = jnp.zeros_like(token)

    outs = pl.pallas_call(
        body, name=name,
        in_specs=[HBM_SPEC] * n + [SEM_SPEC, SEM_SPEC, pl.BlockSpec(memory_space=pl.ANY)],
        out_specs=tuple([HBM_SPEC] * n) + (pl.BlockSpec(memory_space=pltpu.VMEM),),
        out_shape=tuple(pltpu.HBM(g.shape, g.dtype) for g in gs) + (TOKEN,),
        input_output_aliases={t: t for t in range(n)},
        compiler_params=pltpu.CompilerParams(has_side_effects=DATAFLOW),
    )(*gs, send, recv, after)
    return list(outs[:n]), outs[n]


def gather_forward(gs, *, name):
    n = len(gs)

    def body(*refs):
        g = refs[n:2 * n]
        send, recv = refs[2 * n:]
        x, y, c, others = _place()
        cps = []
        for t in range(n):
            hr = g[t].shape[1] // 2
            for j, (px, py) in enumerate(others):
                rows = g[t].at[2 * px + py, pl.ds(c * hr, hr)]
                cp = pltpu.make_async_remote_copy(rows, rows, send.at[3 * t + j], recv.at[3 * t + j],
                                                  device_id=(x, y, 1 - c), device_id_type=MESH)
                cp.start()
                cps.append(cp)
        for t in range(n):
            hr = g[t].shape[1] // 2
            for j, (px, py) in enumerate(others):
                rows = g[t].at[2 * px + py, pl.ds((1 - c) * hr, hr)]
                pltpu.make_async_remote_copy(rows, rows, send.at[3 * t + j], recv.at[3 * t + j],
                                             device_id=(x, y, 1 - c), device_id_type=MESH).wait_recv()
        for cp in cps:
            cp.wait_send()

    return pl.pallas_call(
        body, name=name,
        in_specs=_any_specs(n), out_specs=_any_specs(n),
        out_shape=[jax.ShapeDtypeStruct(g.shape, g.dtype) for g in gs],
        input_output_aliases={t: t for t in range(n)},
        scratch_shapes=[pltpu.SemaphoreType.DMA((3 * n,)), pltpu.SemaphoreType.DMA((3 * n,))],
        compiler_params=pltpu.CompilerParams(has_side_effects=True),
    )(*gs)


def sibling_halves(parts, *, name):
    n = len(parts)

    def body(*refs):
        src, dst = refs[:n], refs[n:2 * n]
        send, recv = refs[2 * n:]
        x, y, c, _ = _place()
        cps = []
        for t in range(n):
            hr = src[t].shape[1] // 2
            cp = pltpu.make_async_remote_copy(src[t].at[:, pl.ds((1 - c) * hr, hr)], dst[t], send.at[t], recv.at[t],
                                              device_id=(x, y, 1 - c), device_id_type=MESH)
            cp.start()
            cps.append(cp)
        for cp in cps:
            cp.wait()

    return pl.pallas_call(
        body, name=name,
        in_specs=_any_specs(n), out_specs=_any_specs(n),
        out_shape=[jax.ShapeDtypeStruct((p.shape[0], p.shape[1] // 2, p.shape[2]), p.dtype) for p in parts],
        scratch_shapes=[pltpu.SemaphoreType.DMA((n,)), pltpu.SemaphoreType.DMA((n,))],
        compiler_params=pltpu.CompilerParams(has_side_effects=True),
    )(*parts)


def _chips_copy(q, land, t, j, px, py, c, send, recv):
    return pltpu.make_async_remote_copy(q[t].at[2 * px + py], land[t].at[j], send.at[3 * t + j], recv.at[3 * t + j],
                                        device_id=(px, py, c), device_id_type=MESH)


def chips_start(qs, *, name):
    n = len(qs)

    def body(*refs):
        q, land, token = refs[:n], refs[n:2 * n], refs[-1]
        send, recv = refs[2 * n], refs[2 * n + 1]
        x, y, c, others = _place()
        for t in range(n):
            for j, (px, py) in enumerate(others):
                _chips_copy(q, land, t, j, px, py, c, send, recv).start()
        token[...] = jnp.zeros_like(token)

    lands = [lax.empty((3,) + q.shape[1:], q.dtype) for q in qs]
    outs = pl.pallas_call(
        body, name=name,
        in_specs=[HBM_SPEC] * (2 * n),
        out_specs=(SEM_SPEC, SEM_SPEC, *[HBM_SPEC] * (2 * n), pl.BlockSpec(memory_space=pltpu.VMEM)),
        out_shape=(pltpu.SemaphoreType.DMA((3 * n,)), pltpu.SemaphoreType.DMA((3 * n,)),
                   *[pltpu.HBM(a.shape, a.dtype) for a in qs + lands], TOKEN),
        input_output_aliases={t: 2 + t for t in range(2 * n)},
        compiler_params=pltpu.CompilerParams(has_side_effects=DATAFLOW),
    )(*_in_hbm(qs + lands))
    return outs[0], outs[1], list(outs[2:2 + n]), list(outs[2 + n:2 + 2 * n]), outs[-1]


def chips_wait(send, recv, qs, lands, after, *, name):
    n = len(qs)

    def body(*refs):
        q, land = refs[:n], refs[n:2 * n]
        send, recv = refs[2 * n], refs[2 * n + 1]
        x, y, c, others = _place()
        for t in range(n):
            for j, (px, py) in enumerate(others):
                cp = _chips_copy(q, land, t, j, px, py, c, send, recv)
                cp.wait_send()
                cp.wait_recv()

    outs = pl.pallas_call(
        body, name=name,
        in_specs=[HBM_SPEC] * (2 * n) + [SEM_SPEC, SEM_SPEC, pl.BlockSpec(memory_space=pl.ANY)],
        out_specs=tuple([HBM_SPEC] * (2 * n)),
        out_shape=tuple(pltpu.HBM(a.shape, a.dtype) for a in qs + lands),
        input_output_aliases={t: t for t in range(2 * n)},
        compiler_params=pltpu.CompilerParams(has_side_effects=DATAFLOW),
    )(*qs, *lands, send, recv, after)
    return list(outs[:n]), list(outs[n:])


def join_halves(fulls, li, *, name):
    n = len(fulls)

    def body(*refs):
        g = refs[n:2 * n]
        send, recv = refs[2 * n:]
        x, y, c, _ = _place()
        cps = []
        for t in range(n):
            hr = g[t].shape[1] // 2
            rows = g[t].at[li, pl.ds(c * hr, hr)]
            cp = pltpu.make_async_remote_copy(rows, rows, send.at[t], recv.at[t], device_id=(x, y, 1 - c), device_id_type=MESH)
            cp.start()
            cps.append(cp)
        for t in range(n):
            hr = g[t].shape[1] // 2
            rows = g[t].at[li, pl.ds((1 - c) * hr, hr)]
            pltpu.make_async_remote_copy(rows, rows, send.at[t], recv.at[t],
                                         device_id=(x, y, 1 - c), device_id_type=MESH).wait_recv()
        for cp in cps:
            cp.wait_send()

    return pl.pallas_call(
        body, name=name,
        in_specs=_any_specs(n), out_specs=_any_specs(n),
        out_shape=[jax.ShapeDtypeStruct(g.shape, g.dtype) for g in fulls],
        input_output_aliases={t: t for t in range(n)},
        scratch_shapes=[pltpu.SemaphoreType.DMA((n,)), pltpu.SemaphoreType.DMA((n,))],
        compiler_params=pltpu.CompilerParams(has_side_effects=True),
    )(*fulls)


def _join_copy(g, t, li, half, c, x, y, send, recv):
    hr = g[t].shape[1] // 2
    rows = g[t].at[li, pl.ds(half * hr, hr)]
    return pltpu.make_async_remote_copy(rows, rows, send.at[t], recv.at[t], device_id=(x, y, 1 - c), device_id_type=MESH)


def join_start(fulls, li, *, name):
    n = len(fulls)

    def body(*refs):
        g, token = refs[:n], refs[-1]
        send, recv = refs[n], refs[n + 1]
        x, y, c, _ = _place()
        for t in range(n):
            _join_copy(g, t, li, c, c, x, y, send, recv).start()
        token[...] = jnp.zeros_like(token)

    outs = pl.pallas_call(
        body, name=name,
        in_specs=[HBM_SPEC] * n,
        out_specs=(SEM_SPEC, SEM_SPEC, *[HBM_SPEC] * n, pl.BlockSpec(memory_space=pltpu.VMEM)),
        out_shape=(pltpu.SemaphoreType.DMA((n,)), pltpu.SemaphoreType.DMA((n,)),
                   *[pltpu.HBM(g.shape, g.dtype) for g in fulls], TOKEN),
        input_output_aliases={t: 2 + t for t in range(n)},
        compiler_params=pltpu.CompilerParams(has_side_effects=DATAFLOW),
    )(*_in_hbm(fulls))
    return outs[0], outs[1], list(outs[2:2 + n])


def join_wait(send, recv, fulls, li, after, *, name):
    n = len(fulls)

    def body(*refs):
        g = refs[:n]
        send, recv = refs[n], refs[n + 1]
        x, y, c, _ = _place()
        for t in range(n):
            _join_copy(g, t, li, c, c, x, y, send, recv).wait_send()
            _join_copy(g, t, li, 1 - c, c, x, y, send, recv).wait_recv()

    outs = pl.pallas_call(
        body, name=name,
        in_specs=[HBM_SPEC] * n + [SEM_SPEC, SEM_SPEC, pl.BlockSpec(memory_space=pl.ANY)],
        out_specs=tuple([HBM_SPEC] * n),
        out_shape=tuple(pltpu.HBM(g.shape, g.dtype) for g in fulls),
        input_output_aliases={t: t for t in range(n)},
        compiler_params=pltpu.CompilerParams(has_side_effects=DATAFLOW),
    )(*fulls, send, recv, after)
    return list(outs)


def allreduce_small(t, *, name):
    R, C = t.shape

    def body(t_ref, o_ref, land, send, recv):
        x, y, c, _ = _place()
        me = 4 * x + 2 * y + c
        land[me] = t_ref[...]
        cps = []
        for j in range(1, 8):
            px, py, pc = (x + (j >> 2)) % 2, (y + ((j >> 1) & 1)) % 2, (c + (j & 1)) % 2
            cp = pltpu.make_async_remote_copy(t_ref, land.at[me], send.at[j - 1], recv.at[j - 1],
                                              device_id=(px, py, pc), device_id_type=MESH)
            cp.start()
            cps.append(cp)
        for j in range(1, 8):
            px, py, pc = (x + (j >> 2)) % 2, (y + ((j >> 1) & 1)) % 2, (c + (j & 1)) % 2
            pltpu.make_async_remote_copy(t_ref, land.at[4 * px + 2 * py + pc], send.at[j - 1], recv.at[j - 1],
                                         device_id=(px, py, pc), device_id_type=MESH).wait_recv()
        for cp in cps:
            cp.wait_send()
        acc = land[0]
        for k in range(1, 8):
            acc = acc + land[k]
        o_ref[...] = acc

    return pl.pallas_call(
        body, name=name,
        in_specs=[pl.BlockSpec(memory_space=pltpu.VMEM)], out_specs=pl.BlockSpec(memory_space=pltpu.VMEM),
        out_shape=jax.ShapeDtypeStruct((R, C), F32),
        scratch_shapes=[pltpu.VMEM((8, R, C), F32), pltpu.SemaphoreType.DMA((7,)), pltpu.SemaphoreType.DMA((7,))],
        compiler_params=pltpu.CompilerParams(has_side_effects=True),
    )(t)


def _layer_fwd(x, xb, memb, w_in, rest, P, tabs, alpha, li):
    ca, sa, cb, sb = tabs
    nA = P["gn_a"].shape[1] // HEAD_DIM_A
    nQ = P["gn_b"].shape[1] // LANES
    nm = lambda s: f"L{li}_{s}"
    h = mm_nn(xb, w_in, name=nm("h"), out_dtype=F32, tn=2304)
    ya, lse_a = attn_a_fwd(h, ca, sa, n_heads=nA, name=nm("attn_a"))
    yb, lse_b = attn_b_fwd(h, cb, sb, P["sinks_l"], off_q=3 * nA, n_qtiles=nQ, name=nm("attn_b"))
    ymix = rms_fwd(ya, yb, P["gn_a"], P["gn_b"], name=nm("rms"))
    W, P = rest(ymix, P)
    z1, x1, x1b = mm_ln(ymix, W["w_out"][0], x, P["ln_mix_g"], P["ln_mix_b"], name=nm("out_ln"), alpha=alpha,
                        tm=256, tk=ymix.shape[1])
    qm = mm_nn(x1b, W["w_mq"], name=nm("mq"), out_dtype=BF16)
    kv = mm_nn(memb, W["w_mkv"], name=nm("mkv"), out_dtype=BF16, tm=256)
    o = mem_attn_fwd(qm, kv, name=nm("mem_attn"))
    z2, x2, x2b = mm_ln(o, W["w_mo"][0], x1, P["ln_mem_g"], P["ln_mem_b"], name=nm("mo_ln"), alpha=alpha,
                        tm=256, tk=o.shape[1])
    u, a = mm_nn(x2b, W["w_up"], name=nm("up"), out_dtype=BF16, relu2=True)
    z3, x3, x3b = mm_ln(a, W["w_down"][0], x2, P["ln_ff_g"], P["ln_ff_b"], name=nm("down_ln"), alpha=alpha)
    saved = dict(xb=xb, h=h, ya=ya, lse_a=lse_a, yb=yb, lse_b=lse_b, ymix=ymix, z1=z1, x1b=x1b, qm=qm, kv=kv, o=o,
                 z2=z2, x2b=x2b, u=u, a=a, z3=z3)
    return x3, x3b, saved


def _layer_bwd(dx3, sv, memb, W, P, tabs, alpha, li, hook=None):
    ca, sa, cb, sb = tabs
    nA = P["gn_a"].shape[1] // HEAD_DIM_A
    nQ = P["gn_b"].shape[1] // LANES
    nm = lambda s: f"L{li}_b_{s}"
    nsh = lambda k: W[k].shape[0]
    gw, gs = {}, {}
    dz3, dz3b, gs["ln_ff_g"], gs["ln_ff_b"] = ln_bwd(dx3, sv["z3"], P["ln_ff_g"], name=nm("ln_ff"))
    gw["w_down"] = mm_tn(sv["a"], dz3b, nsh("w_down"), name=nm("dw_down"))
    du = mm_nt(dz3b, W["w_down"], name=nm("du"), out_dtype=BF16, umul=sv["u"])
    gw["w_up"] = mm_tn(sv["x2b"], du, nsh("w_up"), name=nm("dw_up"))
    dx2 = mm_nt(du, W["w_up"], name=nm("dx2"), out_dtype=F32, resid=dz3, alpha=alpha)
    dz2, dz2b, gs["ln_mem_g"], gs["ln_mem_b"] = ln_bwd(dx2, sv["z2"], P["ln_mem_g"], name=nm("ln_mem"))
    gw["w_mo"] = mm_tn(sv["o"], dz2b, nsh("w_mo"), name=nm("dw_mo"))
    do = mm_nt(dz2b, W["w_mo"], name=nm("do"), out_dtype=BF16)
    dqm, dkv = mem_attn_bwd(sv["qm"], sv["kv"], do, name=nm("mem_attn"))
    gw["w_mq"] = mm_tn(sv["x1b"], dqm, nsh("w_mq"), name=nm("dw_mq"))
    gw["w_mkv"] = mm_tn(memb, cast_bf16(dkv, name=nm("dkv_cast")), nsh("w_mkv"), name=nm("dw_mkv"), tm=256)
    dx1 = mm_nt(dqm, W["w_mq"], name=nm("dx1"), out_dtype=F32, resid=dz2, alpha=alpha)
    if hook is not None:
        P = hook(gw, dx1, P)
    dz1, dz1b, gs["ln_mix_g"], gs["ln_mix_b"] = ln_bwd(dx1, sv["z1"], P["ln_mix_g"], name=nm("ln_mix"))
    gw["w_out"] = mm_tn(sv["ymix"], dz1b, nsh("w_out"), name=nm("dw_out"))
    dymix = mm_nt(dz1b, W["w_out"], name=nm("dymix"), out_dtype=F32)
    dya, dyb, gs["gn_a"], gs["gn_b"] = rms_bwd(dymix, sv["ya"], sv["yb"], P["gn_a"], P["gn_b"], name=nm("rms"))
    dh = attn_a_bwd(sv["h"], ca, sa, sv["ya"], sv["lse_a"], dya, n_heads=nA, name=nm("attn_a"))
    dh, gs["sinks"] = attn_b_bwd(sv["h"], cb, sb, P["sinks_l"], sv["yb"], sv["lse_b"], dyb, dh,
                                 off_q=3 * nA, n_qtiles=nQ, name=nm("attn_b"))
    gw["w_in"] = mm_tn(sv["xb"], dh, nsh("w_in"), name=nm("dw_in"), tn=2304)
    dx0 = mm_nt(dh, W["w_in"], name=nm("dx0"), out_dtype=F32, resid=dz1, alpha=alpha, tr=2304)
    return dx0, gw, gs


def _gathered_view(name, g):
    if name == "w_in":
        return jnp.concatenate([g[k] for k in range(N_CHIPS)], axis=1)[None]
    if name in COL_SHARDED:
        return g
    return g.reshape(1, g.shape[0] * g.shape[1], g.shape[2])


def _to_shards(name, gw):
    if name == "w_in":
        n = gw.shape[2] // N_CHIPS
        return jnp.stack([gw[0, :, k * n:(k + 1) * n] for k in range(N_CHIPS)])
    if name in COL_SHARDED:
        return gw
    return gw.reshape(N_CHIPS, gw.shape[1] // N_CHIPS, gw.shape[2])


def _step(x, mem, positions, loss_target, w, m, v):
    S, D = x.shape[1], x.shape[2]
    depth = w["w_in"].shape[0]
    alpha = (2 * depth) ** 0.25
    x0 = x[0]
    memb = cast_bf16(mem[0], name="mem_cast")
    pos = positions[0]
    tabs = rope_tables(pos, HEAD_DIM_A // 4, HEAD_DIM_A) + rope_tables(pos, HEAD_DIM_B // 4, HEAD_DIM_B)
    place = mesh_place()

    def small(li):
        P = {k: w[k][li][None] for k in ("gn_a", "gn_b", "ln_mix_g", "ln_mix_b", "ln_mem_g", "ln_mem_b", "ln_ff_g", "ln_ff_b")}
        P["sinks_l"] = jnp.broadcast_to(w["sinks"][li][:, None], (w["sinks"].shape[1], LANES))
        return P

    rest_names = tuple(k for k in BIG if k != "w_in")
    chain = [(0, ("w_in",)), (0, rest_names)] + [(li, BIG) for li in range(1, depth)]
    casts = [[cast_into_slot(w[k], li, place, name=f"L{li}_cast_{k}") for k in names] for li, names in chain]
    started = {0: gather_start(casts[0], tabs[1], name="G0_gather_start")}

    def land(gi, after):
        send, recv, gs, tok0 = started.pop(gi)
        gs, landed = gather_wait(send, recv, gs, tok0 if after is None else after, name=f"G{gi}_gather_wait")
        token = None
        if gi + 1 < len(chain):
            started[gi + 1] = gather_start(casts[gi + 1], landed, name=f"G{gi + 1}_gather_start")
            token = started[gi + 1][3]
        gs = gather_forward(gs, name=f"G{gi}_gather_fwd")
        return dict(zip(chain[gi][1], gs)), token

    def ordered(a, token):
        return a if token is None else a + token[:1, :1].astype(a.dtype)

    xs, xbs, saved, Ws = x0, cast_bf16(x0, name="x_cast"), [], []
    for li in range(depth):
        gi = 0 if li == 0 else li + 1
        got, token = land(gi, None if li == 0 else xs)
        W = {"w_in": _gathered_view("w_in", got["w_in"])}
        tabs_l = (ordered(tabs[0], token),) + tabs[1:]

        def rest(after, P, li=li, got=got, W=W):
            if li == 0:
                got, token = land(1, after)
                P = dict(P, ln_mix_g=ordered(P["ln_mix_g"], token))
            W.update({k: _gathered_view(k, got[k]) for k in rest_names})
            return W, P

        xs, xbs, sv = _layer_fwd(xs, xbs, memb, W["w_in"], rest, small(li), tabs_l, alpha, li)
        saved.append(sv)
        Ws.append(W)
    dy, loss_part = loss_head(xs, loss_target[0], name="loss")
    loss = lax.psum(0.5 / D * jnp.sum(loss_part), ("x", "y", "c"))

    g_big = {k: lax.empty(w[k].shape, F32) for k in BIG}
    g_small = [None] * depth

    def begin(li, names, gw, tag):
        parts = [_to_shards(k, gw[k]) for k in names]
        r1 = sibling_halves(parts, name=f"L{li}{tag}_rs_sibling")
        qs = [pair_sum(p, r, place, name=f"L{li}_rs_pair_{k}") for k, p, r in zip(names, parts, r1)]
        send, recv, qs, lands, token = chips_start(qs, name=f"L{li}{tag}_rs_chips_start")
        return (li, names, tag, send, recv, qs, lands), token

    joins = []

    def drain_joins(after):
        while joins:
            li, names, tag, send, recv = joins.pop(0)
            bufs = join_wait(send, recv, [g_big[k] for k in names], li, after, name=f"L{li}{tag}_rs_join_wait")
            g_big.update(zip(names, bufs))

    def finish(pending, after):
        li, names, tag, send, recv, qs, lands = pending
        drain_joins(after)
        qs, lands = chips_wait(send, recv, qs, lands, after, name=f"L{li}{tag}_rs_chips_wait")
        fulls = [chip_sum(q, r, place, g_big[k], li, name=f"L{li}_rs_sum_{k}") for k, q, r in zip(names, qs, lands)]
        send, recv, bufs = join_start(fulls, li, name=f"L{li}{tag}_rs_join_start")
        g_big.update(zip(names, bufs))
        joins.append((li, names, tag, send, recv))

    early = ("w_mq", "w_mkv", "w_mo", "w_up", "w_down")
    late = tuple(k for k in BIG if k not in early)
    pendings, token = [], None
    for li in reversed(range(depth)):
        P = small(li)
        P["ln_ff_g"] = ordered(P["ln_ff_g"], token)
        hook = None
        if li == 0:
            def hook(gw, dx1, P):
                while pendings:
                    finish(pendings.pop(), dx1)
                pend, tok = begin(0, early, gw, "a")
                pendings.append(pend)
                return dict(P, ln_mix_g=ordered(P["ln_mix_g"], tok))
        dy, gw, g_small[li] = _layer_bwd(dy, saved[li], memb, Ws[li], P, tabs, alpha, li, hook)
        while pendings:
            finish(pendings.pop(), dy)
        pend, token = begin(li, late if li == 0 else BIG, gw, "b" if li == 0 else "")
        pendings.append(pend)
    finish(pendings.pop(), token)
    drain_joins(token)
    grad_x = dy[None]

    rows = []
    for li in range(depth):
        gs = g_small[li]
        for k in ("ln_mix_g", "ln_mix_b", "ln_mem_g", "ln_mem_b", "ln_ff_g", "ln_ff_b"):
            rows.append(jnp.sum(gs[k], axis=0, keepdims=True))
        rows.append(jnp.concatenate([jnp.sum(gs["gn_a"], axis=0, keepdims=True), jnp.sum(gs["gn_b"], axis=0, keepdims=True)], axis=1))
        sk = gs["sinks"][:, 0][None]
        rows.append(jnp.pad(sk, ((0, 0), (0, D - sk.shape[1]))))
    red = allreduce_small(jnp.concatenate(rows, axis=0), name="small_allreduce").reshape(depth, 8, D)
    wa = w["gn_a"].shape[1]
    grads = dict(g_big)
    for j, k in enumerate(("ln_mix_g", "ln_mix_b", "ln_mem_g", "ln_mem_b", "ln_ff_g", "ln_ff_b")):
        grads[k] = red[:, j]
    grads["gn_a"] = red[:, 6, :wa]
    grads["gn_b"] = red[:, 6, wa:]
    grads["sinks"] = red[:, 7, :w["sinks"].shape[1]]

    delta, new_m, new_v = {}, {}, {}
    small_names = [k for k in w if k not in BIG]
    for k in BIG:
        delta[k], new_m[k], new_v[k], grads[k] = adamw(w[k], grads[k], m[k], v[k], name=f"adamw_{k}", emit_g=True)
    pack = lambda d: jnp.concatenate([jnp.pad(d[k], ((0, 0), (0, D - d[k].shape[1]))) for k in small_names], axis=0)
    ds, ms, vs = adamw(pack(w), pack(grads), pack(m), pack(v), name="adamw_small")
    for j, k in enumerate(small_names):
        sl = (slice(j * depth, (j + 1) * depth), slice(0, w[k].shape[1]))
        delta[k], new_m[k], new_v[k] = ds[sl], ms[sl], vs[sl]
    return loss, grad_x, grads, delta, new_m, new_v


WEIGHTS = ("w_in", "gn_a", "gn_b", "sinks", "w_out", "ln_mix_g", "ln_mix_b", "w_mq", "w_mkv", "w_mo",
           "ln_mem_g", "ln_mem_b", "w_up", "w_down", "ln_ff_g", "ln_ff_b")


def kernel(x, mem, positions, w_in, gn_a, gn_b, sinks, w_out, ln_mix_g, ln_mix_b, w_mq, w_mkv, w_mo, ln_mem_g, ln_mem_b, w_up, w_down, ln_ff_g, ln_ff_b, loss_target, m_w_in, m_gn_a, m_gn_b, m_sinks, m_w_out, m_ln_mix_g, m_ln_mix_b, m_w_mq, m_w_mkv, m_w_mo, m_ln_mem_g, m_ln_mem_b, m_w_up, m_w_down, m_ln_ff_g, m_ln_ff_b, v_w_in, v_gn_a, v_gn_b, v_sinks, v_w_out, v_ln_mix_g, v_ln_mix_b, v_w_mq, v_w_mkv, v_w_mo, v_ln_mem_g, v_ln_mem_b, v_w_up, v_w_down, v_ln_ff_g, v_ln_ff_b):
    w = dict(zip(WEIGHTS, (w_in, gn_a, gn_b, sinks, w_out, ln_mix_g, ln_mix_b, w_mq, w_mkv, w_mo, ln_mem_g, ln_mem_b, w_up, w_down, ln_ff_g, ln_ff_b)))
    m = dict(zip(WEIGHTS, (m_w_in, m_gn_a, m_gn_b, m_sinks, m_w_out, m_ln_mix_g, m_ln_mix_b, m_w_mq, m_w_mkv, m_w_mo, m_ln_mem_g, m_ln_mem_b, m_w_up, m_w_down, m_ln_ff_g, m_ln_ff_b)))
    v = dict(zip(WEIGHTS, (v_w_in, v_gn_a, v_gn_b, v_sinks, v_w_out, v_ln_mix_g, v_ln_mix_b, v_w_mq, v_w_mkv, v_w_mo, v_ln_mem_g, v_ln_mem_b, v_w_up, v_w_down, v_ln_ff_g, v_ln_ff_b)))
    loss, grad_x, grads, delta, new_m, new_v = _step(x, mem, positions, loss_target, w, m, v)
    return (loss, grad_x, *[grads[k] for k in WEIGHTS], *[delta[k] for k in WEIGHTS],
            *[new_m[k] for k in WEIGHTS], *[new_v[k] for k in WEIGHTS])
```

```python
import functools

import jax
import jax.numpy as jnp
from jax import lax
from jax.experimental import pallas as pl
from jax.experimental.pallas import tpu as pltpu

F32 = jnp.float32
BF16 = jnp.bfloat16
MESH = pl.DeviceIdType.MESH

HEAD_DIM_A = 128
HEAD_DIM_B = 64
LANES = 128
BLOCK = 128
DILATED_BRANCHES = ((128, 1), (512, 4), (2048, 16))
WINDOW_B = 128
N_MEM_HEADS = 4
ROPE_THETA = 500000.0
LN_EPS = 1e-5
RMS_EPS = 1e-6
NEG_INF = -1e30
ADAM_LR = 0.001
ADAM_B1 = 0.9
ADAM_B2 = 0.999
ADAM_EPS = 1e-08
ADAM_WD = 0.01
ADAM_STEP = 10
N_CHIPS = 4
VMEM_LIMIT = 56 * 1024 * 1024

BIG = ("w_in", "w_out", "w_mq", "w_mkv", "w_mo", "w_up", "w_down")
COL_SHARDED = ("w_in", "w_mkv", "w_up")


def _tile(n, target, mult=LANES):
    best = None
    t = mult
    while t <= min(n, target):
        if n % t == 0:
            best = t
        t += mult
    return best if best is not None else n


def _params(sem=None):
    return pltpu.CompilerParams(dimension_semantics=sem, vmem_limit_bytes=VMEM_LIMIT)


def _dot(a, b):
    return jnp.dot(a, b, preferred_element_type=F32)


def _dot_nt(a, b):
    return lax.dot_general(a, b, (((1,), (1,)), ((), ())), preferred_element_type=F32)


def _dot_tn(a, b):
    return lax.dot_general(a, b, (((0,), (0,)), ((), ())), preferred_element_type=F32)


def mm_nn(a, b3, *, name, out_dtype, relu2=False, tm=1024, tn=1024, tk=2048):
    M, K = a.shape
    nsh, _, nk = b3.shape
    tm, tn, tk = _tile(M, tm, 8), _tile(nk, tn), _tile(K, tk)
    nb, ksteps = nk // tn, K // tk

    def body(a_ref, b_ref, *rest):
        outs, scr = rest[:2 if relu2 else 1], rest[2 if relu2 else 1:]

        def finish(acc):
            if relu2:
                outs[0][...] = acc.astype(outs[0].dtype)
                r = jnp.maximum(acc, 0.0)
                outs[1][...] = (r * r).astype(outs[1].dtype)
            else:
                outs[0][...] = acc.astype(outs[0].dtype)

        if ksteps == 1:
            finish(_dot(a_ref[...], b_ref[...]))
        else:
            acc_ref = scr[0]
            k = pl.program_id(2)

            @pl.when(k == 0)
            def _():
                acc_ref[...] = jnp.zeros_like(acc_ref)

            acc_ref[...] += _dot(a_ref[...], b_ref[...])

            @pl.when(k == ksteps - 1)
            def _():
                finish(acc_ref[...])

    o_spec = pl.BlockSpec((tm, tn), lambda i, j, k: (i, j))
    o_shape = jax.ShapeDtypeStruct((M, nsh * nk), out_dtype)
    return pl.pallas_call(
        body, name=name,
        grid=(M // tm, nsh * nb, ksteps),
        in_specs=[pl.BlockSpec((tm, tk), lambda i, j, k: (i, k)),
                  pl.BlockSpec((None, tk, tn), lambda i, j, k: (j // nb, k, j % nb))],
        out_specs=(o_spec, o_spec) if relu2 else o_spec,
        out_shape=(o_shape, o_shape) if relu2 else o_shape,
        scratch_shapes=[] if ksteps == 1 else [pltpu.VMEM((tm, tn), F32)],
        compiler_params=_params(("parallel", "parallel", "arbitrary")),
    )(a, b3)


def mm_ln(a, w, resid, g, b, *, name, alpha, tm=512, tk=1024):
    M, K = a.shape
    D = w.shape[1]
    tm, tk = _tile(M, tm, 8), _tile(K, tk)
    ksteps = K // tk

    def body(a_ref, w_ref, r_ref, g_ref, b_ref, z_ref, xn_ref, xb_ref, *scr):
        def finish(acc):
            z = alpha * r_ref[...] + acc
            mu = jnp.mean(z, axis=-1, keepdims=True)
            zc = z - mu
            var = jnp.mean(zc * zc, axis=-1, keepdims=True)
            xn = zc * lax.rsqrt(var + LN_EPS) * g_ref[...] + b_ref[...]
            z_ref[...] = z
            xn_ref[...] = xn
            xb_ref[...] = xn.astype(BF16)

        if ksteps == 1:
            finish(_dot(a_ref[...], w_ref[...]))
            return
        acc_ref = scr[0]
        k = pl.program_id(1)

        @pl.when(k == 0)
        def _():
            acc_ref[...] = jnp.zeros_like(acc_ref)

        acc_ref[...] += _dot(a_ref[...], w_ref[...])

        @pl.when(k == ksteps - 1)
        def _():
            finish(acc_ref[...])

    row = pl.BlockSpec((tm, D), lambda i, k: (i, 0))
    vec = pl.BlockSpec((1, D), lambda i, k: (0, 0))
    return pl.pallas_call(
        body, name=name,
        grid=(M // tm, ksteps),
        in_specs=[pl.BlockSpec((tm, tk), lambda i, k: (i, k)),
                  pl.BlockSpec((tk, D), lambda i, k: (k, 0)), row, vec, vec],
        out_specs=(row, row, row),
        out_shape=(jax.ShapeDtypeStruct((M, D), F32), jax.ShapeDtypeStruct((M, D), F32),
                   jax.ShapeDtypeStruct((M, D), BF16)),
        scratch_shapes=[] if ksteps == 1 else [pltpu.VMEM((tm, D), F32)],
        compiler_params=_params(("parallel", "arbitrary")),
    )(a, w, resid, g, b)


def mm_nt(a, b3, *, name, out_dtype, resid=None, alpha=1.0, umul=None, tm=1024, tko=1024, tr=2048):
    M, N = a.shape
    nsh, K, nk = b3.shape
    tm, tko, tr = _tile(M, tm, 8), _tile(K, tko), _tile(nk, tr)
    nb = nk // tr
    rsteps = nsh * nb

    def body(a_ref, b_ref, *rest):
        rest = list(rest)
        r_ref = rest.pop(0) if resid is not None else None
        u_ref = rest.pop(0) if umul is not None else None
        o_ref = rest.pop(0)

        def finish(acc):
            if r_ref is not None:
                acc = acc + alpha * r_ref[...]
            if u_ref is not None:
                acc = acc * (2.0 * jnp.maximum(u_ref[...].astype(F32), 0.0))
            o_ref[...] = acc.astype(o_ref.dtype)

        if rsteps == 1:
            finish(_dot_nt(a_ref[...], b_ref[...]))
        else:
            acc_ref = rest[0]
            r = pl.program_id(2)

            @pl.when(r == 0)
            def _():
                acc_ref[...] = jnp.zeros_like(acc_ref)

            acc_ref[...] += _dot_nt(a_ref[...], b_ref[...])

            @pl.when(r == rsteps - 1)
            def _():
                finish(acc_ref[...])

    o_spec = pl.BlockSpec((tm, tko), lambda i, j, r: (i, j))
    in_specs = [pl.BlockSpec((tm, tr), lambda i, j, r: (i, r)),
                pl.BlockSpec((None, tko, tr), lambda i, j, r: (r // nb, j, r % nb))]
    args = [a, b3]
    for extra in (resid, umul):
        if extra is not None:
            in_specs.append(o_spec)
            args.append(extra)
    return pl.pallas_call(
        body, name=name,
        grid=(M // tm, K // tko, rsteps),
        in_specs=in_specs, out_specs=o_spec,
        out_shape=jax.ShapeDtypeStruct((M, K), out_dtype),
        scratch_shapes=[] if rsteps == 1 else [pltpu.VMEM((tm, tko), F32)],
        compiler_params=_params(("parallel", "parallel", "arbitrary")),
    )(*args)


def mm_tn(a, g, nsh, *, name, tk=1024, tn=1024, tm=2048):
    M, K = a.shape
    N = g.shape[1]
    nk = N // nsh
    tk, tn, tm = _tile(K, tk), _tile(nk, tn), _tile(M, tm, 8)
    nb, msteps = nk // tn, M // tm

    def body(a_ref, g_ref, o_ref, acc_ref):
        m = pl.program_id(2)

        @pl.when(m == 0)
        def _():
            acc_ref[...] = jnp.zeros_like(acc_ref)

        acc_ref[...] += _dot_tn(a_ref[...], g_ref[...])

        @pl.when(m == msteps - 1)
        def _():
            o_ref[...] = acc_ref[...].astype(o_ref.dtype)

    return pl.pallas_call(
        body, name=name,
        grid=(K // tk, nsh * nb, msteps),
        in_specs=[pl.BlockSpec((tm, tk), lambda i, j, m: (m, i)),
                  pl.BlockSpec((tm, tn), lambda i, j, m: (m, j))],
        out_specs=pl.BlockSpec((None, tk, tn), lambda i, j, m: (j // nb, i, j % nb)),
        out_shape=jax.ShapeDtypeStruct((nsh, K, nk), BF16),
        scratch_shapes=[pltpu.VMEM((tk, tn), F32)],
        compiler_params=_params(("parallel", "parallel", "arbitrary")),
    )(a, g)


def _fold8(t):
    return t.reshape(t.shape[0] // 8, 8, t.shape[1]).sum(axis=0)


def ln_bwd(dy, z, g, *, name, tm=256):
    M, D = z.shape
    tm = _tile(M, tm, 8)

    def body(dy_ref, z_ref, g_ref, dz_ref, dzb_ref, dg_ref, db_ref):
        i = pl.program_id(0)
        z_ = z_ref[...]
        dy_ = dy_ref[...]
        mu = jnp.mean(z_, axis=-1, keepdims=True)
        zc = z_ - mu
        var = jnp.mean(zc * zc, axis=-1, keepdims=True)
        rstd = lax.rsqrt(var + LN_EPS)
        xh = zc * rstd
        dxh = dy_ * g_ref[...]
        m1 = jnp.mean(dxh, axis=-1, keepdims=True)
        m2 = jnp.mean(dxh * xh, axis=-1, keepdims=True)
        dz = rstd * (dxh - m1 - xh * m2)
        dz_ref[...] = dz
        dzb_ref[...] = dz.astype(BF16)

        @pl.when(i == 0)
        def _():
            dg_ref[...] = jnp.zeros_like(dg_ref)
            db_ref[...] = jnp.zeros_like(db_ref)

        dg_ref[...] += _fold8(dy_ * xh)
        db_ref[...] += _fold8(dy_)

    row = pl.BlockSpec((tm, D), lambda i: (i, 0))
    acc = pl.BlockSpec((8, D), lambda i: (0, 0))
    return pl.pallas_call(
        body, name=name, grid=(M // tm,),
        in_specs=[row, row, pl.BlockSpec((1, D), lambda i: (0, 0))],
        out_specs=(row, row, acc, acc),
        out_shape=(jax.ShapeDtypeStruct((M, D), F32), jax.ShapeDtypeStruct((M, D), BF16),
                   jax.ShapeDtypeStruct((8, D), F32), jax.ShapeDtypeStruct((8, D), F32)),
        compiler_params=_params(("arbitrary",)),
    )(dy, z, g)


def rms_fwd(ya, yb, ga, gb, *, name, tm=512):
    M, WA = ya.shape
    WB = yb.shape[1]
    tm = _tile(M, tm, 8)

    def body(ya_ref, yb_ref, ga_ref, gb_ref, o_ref):
        for y_ref, g_ref, lo, w in ((ya_ref, ga_ref, 0, WA), (yb_ref, gb_ref, WA, WB)):
            y = y_ref[...]
            r = lax.rsqrt(jnp.mean(y * y, axis=-1, keepdims=True) + RMS_EPS)
            o_ref[:, lo:lo + w] = (y * r * g_ref[...]).astype(o_ref.dtype)

    return pl.pallas_call(
        body, name=name, grid=(M // tm,),
        in_specs=[pl.BlockSpec((tm, WA), lambda i: (i, 0)), pl.BlockSpec((tm, WB), lambda i: (i, 0)),
                  pl.BlockSpec((1, WA), lambda i: (0, 0)), pl.BlockSpec((1, WB), lambda i: (0, 0))],
        out_specs=pl.BlockSpec((tm, WA + WB), lambda i: (i, 0)),
        out_shape=jax.ShapeDtypeStruct((M, WA + WB), BF16),
        compiler_params=_params(("parallel",)),
    )(ya, yb, ga, gb)


def rms_bwd(dy, ya, yb, ga, gb, *, name, tm=512):
    M, WA = ya.shape
    WB = yb.shape[1]
    tm = _tile(M, tm, 8)

    def body(dy_ref, ya_ref, yb_ref, ga_ref, gb_ref, dya_ref, dyb_ref, dga_ref, dgb_ref):
        i = pl.program_id(0)

        @pl.when(i == 0)
        def _():
            dga_ref[...] = jnp.zeros_like(dga_ref)
            dgb_ref[...] = jnp.zeros_like(dgb_ref)

        for y_ref, g_ref, d_ref, dgr, lo, w in ((ya_ref, ga_ref, dya_ref, dga_ref, 0, WA),
                                                (yb_ref, gb_ref, dyb_ref, dgb_ref, WA, WB)):
            y = y_ref[...]
            d = dy_ref[:, lo:lo + w]
            r = lax.rsqrt(jnp.mean(y * y, axis=-1, keepdims=True) + RMS_EPS)
            n = y * r
            dn = d * g_ref[...]
            d_ref[...] = r * (dn - n * jnp.mean(dn * n, axis=-1, keepdims=True))
            dgr[...] += _fold8(d * n)

    return pl.pallas_call(
        body, name=name, grid=(M // tm,),
        in_specs=[pl.BlockSpec((tm, WA + WB), lambda i: (i, 0)),
                  pl.BlockSpec((tm, WA), lambda i: (i, 0)), pl.BlockSpec((tm, WB), lambda i: (i, 0)),
                  pl.BlockSpec((1, WA), lambda i: (0, 0)), pl.BlockSpec((1, WB), lambda i: (0, 0))],
        out_specs=(pl.BlockSpec((tm, WA), lambda i: (i, 0)), pl.BlockSpec((tm, WB), lambda i: (i, 0)),
                   pl.BlockSpec((8, WA), lambda i: (0, 0)), pl.BlockSpec((8, WB), lambda i: (0, 0))),
        out_shape=(jax.ShapeDtypeStruct((M, WA), F32), jax.ShapeDtypeStruct((M, WB), F32),
                   jax.ShapeDtypeStruct((8, WA), F32), jax.ShapeDtypeStruct((8, WB), F32)),
        compiler_params=_params(("arbitrary",)),
    )(dy, ya, yb, ga, gb)


def loss_head(y, target, *, name, tm=512):
    M, D = y.shape
    tm = _tile(M, tm, 8)

    def body(y_ref, t_ref, dy_ref, l_ref):
        i = pl.program_id(0)

        @pl.when(i == 0)
        def _():
            l_ref[...] = jnp.zeros_like(l_ref)

        e = y_ref[...] - t_ref[...]
        dy_ref[...] = e * (1.0 / D)
        l_ref[...] += _fold8(e * e)

    row = pl.BlockSpec((tm, D), lambda i: (i, 0))
    return pl.pallas_call(
        body, name=name, grid=(M // tm,),
        in_specs=[row, row],
        out_specs=(row, pl.BlockSpec((8, D), lambda i: (0, 0))),
        out_shape=(jax.ShapeDtypeStruct((M, D), F32), jax.ShapeDtypeStruct((8, D), F32)),
        compiler_params=_params(("arbitrary",)),
    )(y, target)


def _lane(shape):
    return lax.broadcasted_iota(jnp.int32, shape, len(shape) - 1)


def _swap(t, half, period):
    first = (_lane(t.shape) % period) < half
    return jnp.where(first, pltpu.roll(t, LANES - half, 1), pltpu.roll(t, half, 1))


def _rope(t, c, s, half, period):
    return t * c + _swap(t, half, period) * s


def _rope_t(g, c, s, half, period):
    return g * c - _swap(g, half, period) * s


def rope_tables(positions, rot_dim, period):
    half = rot_dim // 2
    inv_freq = ROPE_THETA ** (-jnp.arange(0, rot_dim, 2, dtype=F32) / rot_dim)
    ang = positions.astype(F32)[:, None] * inv_freq
    cos, sin = jnp.cos(ang), jnp.sin(ang)
    ones = jnp.ones((positions.shape[0], period - rot_dim), F32)
    c = jnp.concatenate([cos, cos, ones], axis=1)
    s = jnp.concatenate([-sin, sin, 0.0 * ones], axis=1)
    reps = LANES // period
    return jnp.tile(c, (1, reps)), jnp.tile(s, (1, reps))


def _branch_blocks(S):
    out = []
    for window, d in DILATED_BRANCHES:
        assert window // d == BLOCK and S % (d * BLOCK) == 0
        out.append((d, (S // d) // BLOCK))
    return out


def _rows(r, n, d):
    return pl.ds(r + n * (BLOCK * d), BLOCK, stride=d) if d > 1 else pl.ds(pl.multiple_of(n * BLOCK, BLOCK), BLOCK)


def _band_masks(n, strict_prev):
    qi = lax.broadcasted_iota(jnp.int32, (BLOCK, BLOCK), 0)
    kj = lax.broadcasted_iota(jnp.int32, (BLOCK, BLOCK), 1)
    cur = kj <= qi
    prev = ((kj > qi) if strict_prev else (kj >= qi)) & (n > 0)
    return cur, prev


def attn_a_fwd(h, ca, sa, *, n_heads, name):
    S = h.shape[0]
    scale = HEAD_DIM_A ** -0.5
    half = HEAD_DIM_A // 8
    branches = _branch_blocks(S)

    def body(q_ref, k_ref, v_ref, c_ref, s_ref, y_ref, lse_ref, qs, ks, m_s, l_s, acc_s):
        qs[...] = _rope(q_ref[...], c_ref[...], s_ref[...], half, HEAD_DIM_A)
        ks[...] = _rope(k_ref[...], c_ref[...], s_ref[...], half, HEAD_DIM_A)
        for bi, (d, nb) in enumerate(branches):
            def blk(idx, carry, bi=bi, d=d, nb=nb):
                r, n = idx // nb, idx % nb
                rc, rp = _rows(r, n, d), _rows(r, jnp.maximum(n - 1, 0), d)
                q = qs[rc, :].astype(BF16)
                cur, prev = _band_masks(n, False)
                sc = jnp.where(cur, _dot_nt(q, ks[rc, :].astype(BF16)) * scale, NEG_INF)
                sp = jnp.where(prev, _dot_nt(q, ks[rp, :].astype(BF16)) * scale, NEG_INF)
                m = jnp.maximum(jnp.max(sc, axis=-1, keepdims=True), jnp.max(sp, axis=-1, keepdims=True))
                pc, pp = jnp.exp(sc - m), jnp.exp(sp - m)
                l = jnp.sum(pc, axis=-1, keepdims=True) + jnp.sum(pp, axis=-1, keepdims=True)
                acc = _dot(pc.astype(BF16), v_ref[rc, :].astype(BF16)) + _dot(pp.astype(BF16), v_ref[rp, :].astype(BF16))
                mb = jnp.broadcast_to(m, (BLOCK, LANES))
                lb = jnp.broadcast_to(l, (BLOCK, LANES))
                if bi == 0:
                    m_s[rc, :], l_s[rc, :], acc_s[rc, :] = mb, lb, acc
                else:
                    m0 = m_s[rc, :]
                    mn = jnp.maximum(m0, mb)
                    a0, a1 = jnp.exp(m0 - mn), jnp.exp(mb - mn)
                    m_s[rc, :] = mn
                    l_s[rc, :] = l_s[rc, :] * a0 + lb * a1
                    acc_s[rc, :] = acc_s[rc, :] * a0 + acc * a1
                return carry

            lax.fori_loop(0, d * nb, blk, 0)
        y_ref[...] = acc_s[...] / l_s[...]
        lse_ref[...] = m_s[...] + jnp.log(l_s[...])

    col = lambda off: pl.BlockSpec((S, LANES), lambda hd: (0, off + hd))
    full = pl.BlockSpec((S, LANES), lambda hd: (0, 0))
    out = pl.BlockSpec((S, LANES), lambda hd: (0, hd))
    o_shape = jax.ShapeDtypeStruct((S, n_heads * LANES), F32)
    return pl.pallas_call(
        body, name=name, grid=(n_heads,),
        in_specs=[col(0), col(n_heads), col(2 * n_heads), full, full],
        out_specs=(out, out), out_shape=(o_shape, o_shape),
        scratch_shapes=[pltpu.VMEM((S, LANES), F32) for _ in range(5)],
        compiler_params=_params(("arbitrary",)),
    )(h, h, h, ca, sa)


def attn_a_bwd(h, ca, sa, ya, lse, dya, *, n_heads, name):
    S = h.shape[0]
    scale = HEAD_DIM_A ** -0.5
    half = HEAD_DIM_A // 8
    branches = _branch_blocks(S)

    def body(q_ref, k_ref, v_ref, c_ref, s_ref, y_ref, lse_ref, dy_ref, o_ref, qs, ks, dq_s, dk_s, dv_s):
        part = pl.program_id(1)

        @pl.when(part == 0)
        def _():
            qs[...] = _rope(q_ref[...], c_ref[...], s_ref[...], half, HEAD_DIM_A)
            ks[...] = _rope(k_ref[...], c_ref[...], s_ref[...], half, HEAD_DIM_A)
            dq_s[...] = jnp.zeros_like(dq_s)
            dk_s[...] = jnp.zeros_like(dk_s)
            dv_s[...] = jnp.zeros_like(dv_s)
            for d, nb in branches:
                def blk(idx, carry, d=d, nb=nb):
                    r, n = idx // nb, idx % nb
                    rc, rp = _rows(r, n, d), _rows(r, jnp.maximum(n - 1, 0), d)
                    q = qs[rc, :].astype(BF16)
                    dy = dy_ref[rc, :]
                    dsum = jnp.sum(dy * y_ref[rc, :], axis=-1, keepdims=True)
                    dyb = dy.astype(BF16)
                    lse_b = lse_ref[rc, :]
                    cur, prev = _band_masks(n, False)
                    dq = jnp.zeros((BLOCK, LANES), F32)
                    for rows, mask in ((rc, cur), (rp, prev)):
                        kb = ks[rows, :].astype(BF16)
                        vb = v_ref[rows, :].astype(BF16)
                        s = jnp.where(mask, _dot_nt(q, kb) * scale, NEG_INF)
                        p = jnp.exp(s - lse_b)
                        ds = (p * (_dot_nt(dyb, vb) - dsum) * scale).astype(BF16)
                        dv_s[rows, :] += _dot_tn(p.astype(BF16), dyb)
                        dk_s[rows, :] += _dot_tn(ds, q)
                        dq = dq + _dot(ds, kb)
                    dq_s[rc, :] += dq
                    return carry

                lax.fori_loop(0, d * nb, blk, 0)
            o_ref[...] = _rope_t(dq_s[...], c_ref[...], s_ref[...], half, HEAD_DIM_A).astype(o_ref.dtype)

        @pl.when(part == 1)
        def _():
            o_ref[...] = _rope_t(dk_s[...], c_ref[...], s_ref[...], half, HEAD_DIM_A).astype(o_ref.dtype)

        @pl.when(part == 2)
        def _():
            o_ref[...] = dv_s[...].astype(o_ref.dtype)

    col = lambda off: pl.BlockSpec((S, LANES), lambda hd, p: (0, off + hd))
    full = pl.BlockSpec((S, LANES), lambda hd, p: (0, 0))
    per_head = pl.BlockSpec((S, LANES), lambda hd, p: (0, hd))
    return pl.pallas_call(
        body, name=name, grid=(n_heads, 3),
        in_specs=[col(0), col(n_heads), col(2 * n_heads), full, full, per_head, per_head, per_head],
        out_specs=pl.BlockSpec((S, LANES), lambda hd, p: (0, p * n_heads + hd)),
        out_shape=jax.ShapeDtypeStruct((S, 3 * n_heads * LANES), BF16),
        scratch_shapes=[pltpu.VMEM((S, LANES), F32) for _ in range(5)],
        compiler_params=_params(("arbitrary", "arbitrary")),
    )(h, h, h, ca, sa, ya, lse, dya)


def _both_halves(t, g):
    low = _lane(t.shape) < HEAD_DIM_B
    return jnp.where(low == (g == 0), t, pltpu.roll(t, HEAD_DIM_B, 1))


def _stack_heads(t):
    low = _lane(t.shape) < HEAD_DIM_B
    return jnp.concatenate([jnp.where(low, t, 0.0), jnp.where(low, 0.0, t)], axis=0)


def _unstack_heads(t2):
    low = _lane((BLOCK, LANES)) < HEAD_DIM_B
    return jnp.where(low, t2[:BLOCK], t2[BLOCK:])


def _fold_halves(t):
    return t + pltpu.roll(t, HEAD_DIM_B, 1)


def _head_rows(ref, tile):
    a = ref[pl.ds(2 * tile, 1), :][:, :1]
    b = ref[pl.ds(2 * tile + 1, 1), :][:, :1]
    return jnp.concatenate([jnp.broadcast_to(a, (BLOCK, 1)), jnp.broadcast_to(b, (BLOCK, 1))], axis=0)


def attn_b_fwd(h, cb, sb, sinks_l, *, off_q, n_qtiles, name):
    S = h.shape[0]
    nblk = S // BLOCK
    scale = HEAD_DIM_B ** -0.5
    half = HEAD_DIM_B // 8
    tiles_per_group = n_qtiles // 2

    def body(q_ref, k_ref, v_ref, c_ref, s_ref, sink_ref, y_ref, lse_ref, qs, kg, vg):
        t = pl.program_id(0)
        g = t // tiles_per_group
        qs[...] = _rope(q_ref[...], c_ref[...], s_ref[...], half, HEAD_DIM_B)
        kg[...] = _both_halves(_rope(k_ref[...], c_ref[...], s_ref[...], half, HEAD_DIM_B), g)
        vg[...] = _both_halves(v_ref[...], g)
        sink = _head_rows(sink_ref, t)

        def blk(n, carry):
            rc = pl.ds(pl.multiple_of(n * BLOCK, BLOCK), BLOCK)
            rp = pl.ds(pl.multiple_of(jnp.maximum(n - 1, 0) * BLOCK, BLOCK), BLOCK)
            q2 = _stack_heads(qs[rc, :]).astype(BF16)
            cur, prev = _band_masks(n, True)
            cur2, prev2 = jnp.concatenate([cur, cur], 0), jnp.concatenate([prev, prev], 0)
            sc = jnp.where(cur2, _dot_nt(q2, kg[rc, :].astype(BF16)) * scale, NEG_INF)
            sp = jnp.where(prev2, _dot_nt(q2, kg[rp, :].astype(BF16)) * scale, NEG_INF)
            m = jnp.maximum(jnp.max(sc, axis=-1, keepdims=True), jnp.max(sp, axis=-1, keepdims=True))
            pc, pp = jnp.exp(sc - m), jnp.exp(sp - m)
            l = jnp.sum(pc, axis=-1, keepdims=True) + jnp.sum(pp, axis=-1, keepdims=True)
            acc = _dot(pc.astype(BF16), vg[rc, :].astype(BF16)) + _dot(pp.astype(BF16), vg[rp, :].astype(BF16))
            m2 = jnp.maximum(m, sink)
            c = jnp.exp(m - m2)
            den = l * c + jnp.exp(sink - m2)
            y_ref[rc, :] = _unstack_heads(acc * (c / den))
            lse_ref[rc, :] = _unstack_heads(jnp.broadcast_to(m2 + jnp.log(den), (2 * BLOCK, LANES)))
            return carry

        lax.fori_loop(0, nblk, blk, 0)

    full = lambda col: pl.BlockSpec((S, LANES), lambda t: (0, col))
    out = pl.BlockSpec((S, LANES), lambda t: (0, t))
    o_shape = jax.ShapeDtypeStruct((S, n_qtiles * LANES), F32)
    return pl.pallas_call(
        body, name=name, grid=(n_qtiles,),
        in_specs=[pl.BlockSpec((S, LANES), lambda t: (0, off_q + t)), full(off_q + n_qtiles), full(off_q + n_qtiles + 1),
                  full(0), full(0), pl.BlockSpec(sinks_l.shape, lambda t: (0, 0))],
        out_specs=(out, out), out_shape=(o_shape, o_shape),
        scratch_shapes=[pltpu.VMEM((S, LANES), F32) for _ in range(3)],
        compiler_params=_params(("arbitrary",)),
    )(h, h, h, cb, sb, sinks_l)


def attn_b_bwd(h, cb, sb, sinks_l, yb, lse, dyb, *, off_q, n_qtiles, name):
    S = h.shape[0]
    nblk = S // BLOCK
    scale = HEAD_DIM_B ** -0.5
    half = HEAD_DIM_B // 8
    tiles_per_group = n_qtiles // 2
    n_steps = n_qtiles + 2

    def body(q_ref, k_ref, v_ref, c_ref, s_ref, sink_ref, y_ref, lse_ref, dy_ref,
             dq_ref, dkv_ref, dsink_ref, qs, kg, vg, dk_acc, dv_acc):
        t = pl.program_id(0)

        @pl.when(t == 0)
        def _():
            dk_acc[...] = jnp.zeros_like(dk_acc)
            dv_acc[...] = jnp.zeros_like(dv_acc)
            dsink_ref[...] = jnp.zeros_like(dsink_ref)

        @pl.when(t < n_qtiles)
        def _():
            g = t // tiles_per_group
            qs[...] = _rope(q_ref[...], c_ref[...], s_ref[...], half, HEAD_DIM_B)
            kg[...] = _both_halves(_rope(k_ref[...], c_ref[...], s_ref[...], half, HEAD_DIM_B), g)
            vg[...] = _both_halves(v_ref[...], g)
            sink = _head_rows(sink_ref, t)

            def blk(n, dsink):
                rc = pl.ds(pl.multiple_of(n * BLOCK, BLOCK), BLOCK)
                rp = pl.ds(pl.multiple_of(jnp.maximum(n - 1, 0) * BLOCK, BLOCK), BLOCK)
                q2 = _stack_heads(qs[rc, :]).astype(BF16)
                dy2 = _stack_heads(dy_ref[rc, :])
                dsum = jnp.sum(dy2 * _stack_heads(y_ref[rc, :]), axis=-1, keepdims=True)
                dy2b = dy2.astype(BF16)
                lse_t = lse_ref[rc, :]
                lse2 = jnp.concatenate([lse_t[:, :1], lse_t[:, HEAD_DIM_B:HEAD_DIM_B + 1]], axis=0)
                cur, prev = _band_masks(n, True)
                dq2 = jnp.zeros((2 * BLOCK, LANES), F32)
                for rows, mask in ((rc, cur), (rp, prev)):
                    kb = kg[rows, :].astype(BF16)
                    vb = vg[rows, :].astype(BF16)
                    mask2 = jnp.concatenate([mask, mask], 0)
                    s = jnp.where(mask2, _dot_nt(q2, kb) * scale, NEG_INF)
                    p = jnp.exp(s - lse2)
                    ds = (p * (_dot_nt(dy2b, vb) - dsum) * scale).astype(BF16)
                    dv_acc[g, rows, :] += _fold_halves(_dot_tn(p.astype(BF16), dy2b))
                    dk_acc[g, rows, :] += _fold_halves(_dot_tn(ds, q2))
                    dq2 = dq2 + _dot(ds, kb)
                dq_ref[rc, :] = _unstack_heads(dq2)
                return dsink - jnp.exp(sink - lse2) * dsum

            dsink = lax.fori_loop(0, nblk, blk, jnp.zeros((2 * BLOCK, 1), F32))
            dq_ref[...] = _rope_t(dq_ref[...], c_ref[...], s_ref[...], half, HEAD_DIM_B)
            d0 = jnp.sum(dsink[:BLOCK], axis=0, keepdims=True)
            d1 = jnp.sum(dsink[BLOCK:], axis=0, keepdims=True)
            dsink_ref[pl.ds(2 * t, 1), :] = jnp.broadcast_to(d0, (1, LANES))
            dsink_ref[pl.ds(2 * t + 1, 1), :] = jnp.broadcast_to(d1, (1, LANES))

        low = _lane((S, LANES)) < HEAD_DIM_B

        @pl.when(t == n_qtiles)
        def _():
            dk = jnp.where(low, dk_acc[0], dk_acc[1])
            dkv_ref[...] = _rope_t(dk, c_ref[...], s_ref[...], half, HEAD_DIM_B)

        @pl.when(t == n_qtiles + 1)
        def _():
            dkv_ref[...] = jnp.where(low, dv_acc[0], dv_acc[1])

    qt = lambda t: jnp.minimum(t, n_qtiles - 1)
    full = lambda col: pl.BlockSpec((S, LANES), lambda t: (0, col))
    per_tile = pl.BlockSpec((S, LANES), lambda t: (0, qt(t)))
    return pl.pallas_call(
        body, name=name, grid=(n_steps,),
        in_specs=[pl.BlockSpec((S, LANES), lambda t: (0, off_q + qt(t))), full(off_q + n_qtiles),
                  full(off_q + n_qtiles + 1), full(0), full(0), pl.BlockSpec(sinks_l.shape, lambda t: (0, 0)),
                  per_tile, per_tile, per_tile],
        out_specs=(per_tile, pl.BlockSpec((S, LANES), lambda t: (0, jnp.maximum(t - n_qtiles, 0))),
                   pl.BlockSpec(sinks_l.shape, lambda t: (0, 0))),
        out_shape=(jax.ShapeDtypeStruct((S, n_qtiles * LANES), F32), jax.ShapeDtypeStruct((S, 2 * LANES), F32),
                   jax.ShapeDtypeStruct(sinks_l.shape, F32)),
        scratch_shapes=[pltpu.VMEM((S, LANES), F32) for _ in range(3)]
        + [pltpu.VMEM((2, S, LANES), F32), pltpu.VMEM((2, S, LANES), F32)],
        compiler_params=_params(("arbitrary",)),
    )(h, h, h, cb, sb, sinks_l, yb, lse, dyb)


def _win(r, nw, d):
    if d > 1:
        return pl.ds(r + nw * (BLOCK * d), 2 * BLOCK, stride=d)
    return pl.ds(pl.multiple_of(nw * BLOCK, BLOCK), 2 * BLOCK)


def _fwd_bias(strict_prev):
    qi = lax.broadcasted_iota(jnp.int32, (BLOCK, 2 * BLOCK), 0)
    kj = lax.broadcasted_iota(jnp.int32, (BLOCK, 2 * BLOCK), 1)
    first = kj < BLOCK
    kk = jnp.where(first, kj, kj - BLOCK)
    prev_ok = (kk > qi) if strict_prev else (kk >= qi)
    zero = first & (kk <= qi)
    mid = (first & prev_ok) | (jnp.logical_not(first) & (kk <= qi))
    return jnp.stack([jnp.where(zero, 0.0, NEG_INF), jnp.where(mid, 0.0, NEG_INF)])


def _bwd_bias(strict_prev):
    qi = lax.broadcasted_iota(jnp.int32, (2 * BLOCK, BLOCK), 0)
    kj = lax.broadcasted_iota(jnp.int32, (2 * BLOCK, BLOCK), 1)
    first = qi < BLOCK
    qq = jnp.where(first, qi, qi - BLOCK)
    prev_ok = (kj > qq) if strict_prev else (kj >= qq)
    mid = (first & (kj <= qq)) | (jnp.logical_not(first) & prev_ok)
    last = jnp.logical_not(first) & (kj <= qq)
    return jnp.stack([jnp.where(mid, 0.0, NEG_INF), jnp.where(last, 0.0, NEG_INF)])


def _pair_bias():
    qi = lax.broadcasted_iota(jnp.int32, (2 * BLOCK, 2 * BLOCK), 0)
    kj = lax.broadcasted_iota(jnp.int32, (2 * BLOCK, 2 * BLOCK), 1)
    return jnp.where((kj <= qi) & (qi - kj <= BLOCK), 0.0, NEG_INF)


UNROLL_PAIR = 2


UNROLL = 4


def attn_a_fwd(h, ca, sa, *, n_heads, name):
    S = h.shape[0]
    scale = HEAD_DIM_A ** -0.5
    half = HEAD_DIM_A // 8
    branches = _branch_blocks(S)

    def body(q_ref, k_ref, v_ref, c_ref, s_ref, y_ref, lse_ref, qs, ks, m_s, l_s, acc_s, bias, pbias):
        qs[...] = _rope(q_ref[...], c_ref[...], s_ref[...], half, HEAD_DIM_A)
        ks[...] = _rope(k_ref[...], c_ref[...], s_ref[...], half, HEAD_DIM_A)
        bias[...] = _fwd_bias(False)
        pbias[...] = _pair_bias()
        for bi, (d, nb) in enumerate(branches):
            assert nb >= 2 and (d * nb) % UNROLL == 0
            if nb == 2 and bi > 0:
                assert d % UNROLL_PAIR == 0

                def pair(it, carry, d=d):
                    rws = [_win(it * UNROLL_PAIR + u, 0, d) for u in range(UNROLL_PAIR)]
                    ss = [_dot_nt(qs[rw, :].astype(BF16), ks[rw, :].astype(BF16)) * scale + pbias[...] for rw in rws]
                    ms = [jnp.max(s, axis=-1, keepdims=True) for s in ss]
                    ps = [jnp.exp(s - m) for s, m in zip(ss, ms)]
                    ls = [jnp.sum(p, axis=-1, keepdims=True) for p in ps]
                    accs = [_dot(p.astype(BF16), v_ref[rw, :].astype(BF16)) for p, rw in zip(ps, rws)]
                    for rw, m, l, acc in zip(rws, ms, ls, accs):
                        mb = jnp.broadcast_to(m, (2 * BLOCK, LANES))
                        m0 = m_s[rw, :]
                        mn = jnp.maximum(m0, mb)
                        a0, a1 = jnp.exp(m0 - mn), jnp.exp(mb - mn)
                        m_s[rw, :] = mn
                        l_s[rw, :] = l_s[rw, :] * a0 + jnp.broadcast_to(l, (2 * BLOCK, LANES)) * a1
                        acc_s[rw, :] = acc_s[rw, :] * a0 + acc * a1
                    return carry

                lax.fori_loop(0, d // UNROLL_PAIR, pair, 0)
                continue

            def blk(it, carry, bi=bi, d=d, nb=nb):
                rn = [((it * UNROLL + u) // nb, (it * UNROLL + u) % nb) for u in range(UNROLL)]
                rcs = [_rows(r, n, d) for r, n in rn]
                rws = [_win(r, jnp.maximum(n - 1, 0), d) for r, n in rn]
                ss = [_dot_nt(qs[rc, :].astype(BF16), ks[rw, :].astype(BF16)) * scale + bias[jnp.minimum(n, 1)]
                      for (r, n), rc, rw in zip(rn, rcs, rws)]
                ms = [jnp.max(s, axis=-1, keepdims=True) for s in ss]
                ps = [jnp.exp(s - m) for s, m in zip(ss, ms)]
                ls = [jnp.sum(p, axis=-1, keepdims=True) for p in ps]
                accs = [_dot(p.astype(BF16), v_ref[rw, :].astype(BF16)) for p, rw in zip(ps, rws)]
                for rc, m, l, acc in zip(rcs, ms, ls, accs):
                    mb = jnp.broadcast_to(m, (BLOCK, LANES))
                    lb = jnp.broadcast_to(l, (BLOCK, LANES))
                    if bi == 0:
                        m_s[rc, :], l_s[rc, :], acc_s[rc, :] = mb, lb, acc
                    else:
                        m0 = m_s[rc, :]
                        mn = jnp.maximum(m0, mb)
                        a0, a1 = jnp.exp(m0 - mn), jnp.exp(mb - mn)
                        m_s[rc, :] = mn
                        l_s[rc, :] = l_s[rc, :] * a0 + lb * a1
                        acc_s[rc, :] = acc_s[rc, :] * a0 + acc * a1
                return carry

            lax.fori_loop(0, d * nb // UNROLL, blk, 0)
        y_ref[...] = acc_s[...] / l_s[...]
        lse_ref[...] = m_s[...] + jnp.log(l_s[...])

    col = lambda off: pl.BlockSpec((S, LANES), lambda hd: (0, off + hd))
    full = pl.BlockSpec((S, LANES), lambda hd: (0, 0))
    out = pl.BlockSpec((S, LANES), lambda hd: (0, hd))
    o_shape = jax.ShapeDtypeStruct((S, n_heads * LANES), F32)
    return pl.pallas_call(
        body, name=name, grid=(n_heads,),
        in_specs=[col(0), col(n_heads), col(2 * n_heads), full, full],
        out_specs=(out, out), out_shape=(o_shape, o_shape),
        scratch_shapes=[pltpu.VMEM((S, LANES), F32) for _ in range(5)]
        + [pltpu.VMEM((2, BLOCK, 2 * BLOCK), F32), pltpu.VMEM((2 * BLOCK, 2 * BLOCK), F32)],
        compiler_params=_params(("arbitrary",)),
    )(h, h, h, ca, sa)


def attn_a_bwd(h, ca, sa, ya, lse, dya, *, n_heads, name):
    S = h.shape[0]
    scale = HEAD_DIM_A ** -0.5
    half = HEAD_DIM_A // 8
    branches = _branch_blocks(S)

    def body(q_ref, k_ref, v_ref, c_ref, s_ref, y_ref, lse_ref, dy_ref, o_ref, qs, ks, dq_s, dk_s, dv_s, bias, pbias):
        part = pl.program_id(1)

        @pl.when(part == 0)
        def _():
            qs[...] = _rope(q_ref[...], c_ref[...], s_ref[...], half, HEAD_DIM_A)
            ks[...] = _rope(k_ref[...], c_ref[...], s_ref[...], half, HEAD_DIM_A)
            dq_s[...] = jnp.zeros_like(dq_s)
            bias[...] = _bwd_bias(False)
            pbias[...] = _pair_bias()
            for bi, (d, nb) in enumerate(branches):
                assert nb >= 2 and (d * nb) % UNROLL == 0
                if nb == 2 and bi > 0:
                    assert d % UNROLL_PAIR == 0

                    def pair(it, carry, d=d):
                        rws = [_win(it * UNROLL_PAIR + u, 0, d) for u in range(UNROLL_PAIR)]
                        q2 = [qs[rw, :].astype(BF16) for rw in rws]
                        k2 = [ks[rw, :].astype(BF16) for rw in rws]
                        dy2 = [dy_ref[rw, :] for rw in rws]
                        dsum = [jnp.sum(dy * y_ref[rw, :], axis=-1, keepdims=True) for dy, rw in zip(dy2, rws)]
                        dy2b = [dy.astype(BF16) for dy in dy2]
                        ss = [_dot_nt(q, k) * scale + pbias[...] for q, k in zip(q2, k2)]
                        dps = [_dot_nt(dy, v_ref[rw, :].astype(BF16)) for dy, rw in zip(dy2b, rws)]
                        ps = [jnp.exp(s - lse_ref[rw, :][:, :1]) for s, rw in zip(ss, rws)]
                        dss = [(p * (dp - dm) * scale).astype(BF16) for p, dp, dm in zip(ps, dps, dsum)]
                        dvs = [_dot_tn(p.astype(BF16), dy) for p, dy in zip(ps, dy2b)]
                        dks = [_dot_tn(ds, q) for ds, q in zip(dss, q2)]
                        dqs = [_dot(ds, k) for ds, k in zip(dss, k2)]
                        for rw, dv, dk, dq in zip(rws, dvs, dks, dqs):
                            dv_s[rw, :] += dv
                            dk_s[rw, :] += dk
                            dq_s[rw, :] += dq
                        return carry

                    lax.fori_loop(0, d // UNROLL_PAIR, pair, 0)
                    continue

                def blk(it, carry, bi=bi, d=d, nb=nb):
                    rj = [((it * UNROLL + u) // nb, (it * UNROLL + u) % nb) for u in range(UNROLL)]
                    rks = [_rows(r, j, d) for r, j in rj]
                    rws = [_win(r, jnp.minimum(j, nb - 2), d) for r, j in rj]
                    q2 = [qs[rw, :].astype(BF16) for rw in rws]
                    kb = [ks[rk, :].astype(BF16) for rk in rks]
                    dy2 = [dy_ref[rw, :] for rw in rws]
                    dsum = [jnp.sum(dy * y_ref[rw, :], axis=-1, keepdims=True) for dy, rw in zip(dy2, rws)]
                    dy2b = [dy.astype(BF16) for dy in dy2]
                    ss = [_dot_nt(q, k) * scale + bias[(j == nb - 1).astype(jnp.int32)] for q, k, (r, j) in zip(q2, kb, rj)]
                    dps = [_dot_nt(dy, v_ref[rk, :].astype(BF16)) for dy, rk in zip(dy2b, rks)]
                    ps = [jnp.exp(s - lse_ref[rw, :]) for s, rw in zip(ss, rws)]
                    dss = [(p * (dp - dm) * scale).astype(BF16) for p, dp, dm in zip(ps, dps, dsum)]
                    dvs = [_dot_tn(p.astype(BF16), dy) for p, dy in zip(ps, dy2b)]
                    dks = [_dot_tn(ds, q) for ds, q in zip(dss, q2)]
                    dqs = [_dot(ds, k) for ds, k in zip(dss, kb)]
                    for rk, rw, dv, dk, dq in zip(rks, rws, dvs, dks, dqs):
                        if bi == 0:
                            dv_s[rk, :], dk_s[rk, :] = dv, dk
                        else:
                            dv_s[rk, :] += dv
                            dk_s[rk, :] += dk
                        dq_s[rw, :] += dq
                    return carry

                lax.fori_loop(0, d * nb // UNROLL, blk, 0)
            o_ref[...] = _rope_t(dq_s[...], c_ref[...], s_ref[...], half, HEAD_DIM_A).astype(o_ref.dtype)

        @pl.when(part == 1)
        def _():
            o_ref[...] = _rope_t(dk_s[...], c_ref[...], s_ref[...], half, HEAD_DIM_A).astype(o_ref.dtype)

        @pl.when(part == 2)
        def _():
            o_ref[...] = dv_s[...].astype(o_ref.dtype)

    col = lambda off: pl.BlockSpec((S, LANES), lambda hd, p: (0, off + hd))
    full = pl.BlockSpec((S, LANES), lambda hd, p: (0, 0))
    per_head = pl.BlockSpec((S, LANES), lambda hd, p: (0, hd))
    return pl.pallas_call(
        body, name=name, grid=(n_heads, 3),
        in_specs=[col(0), col(n_heads), col(2 * n_heads), full, full, per_head, per_head, per_head],
        out_specs=pl.BlockSpec((S, LANES), lambda hd, p: (0, p * n_heads + hd)),
        out_shape=jax.ShapeDtypeStruct((S, h.shape[1]), BF16),
        scratch_shapes=[pltpu.VMEM((S, LANES), F32) for _ in range(5)]
        + [pltpu.VMEM((2, 2 * BLOCK, BLOCK), F32), pltpu.VMEM((2 * BLOCK, 2 * BLOCK), F32)],
        compiler_params=_params(("arbitrary", "arbitrary")),
    )(h, h, h, ca, sa, ya, lse, dya)


def _unstack_heads(t2):
    rows = t2.shape[0] // 2
    low = _lane((rows, LANES)) < HEAD_DIM_B
    return jnp.where(low, t2[:rows], t2[rows:])


def _head_rows(ref, tile, rows):
    a = ref[pl.ds(2 * tile, 1), :][:, :1]
    b = ref[pl.ds(2 * tile + 1, 1), :][:, :1]
    return jnp.concatenate([jnp.broadcast_to(a, (rows, 1)), jnp.broadcast_to(b, (rows, 1))], axis=0)


UNROLL_B = 2
UNROLL_B_BWD = 4


def attn_b_fwd(h, cb, sb, sinks_l, *, off_q, n_qtiles, name):
    S = h.shape[0]
    nblk = S // BLOCK
    scale = HEAD_DIM_B ** -0.5
    half = HEAD_DIM_B // 8
    tiles_per_group = n_qtiles // 2
    assert nblk >= 2 and nblk % UNROLL_B == 0

    def body(q_ref, k_ref, v_ref, c_ref, s_ref, sink_ref, y_ref, lse_ref, qs, kg, vg, bias):
        t = pl.program_id(0)
        g = t // tiles_per_group
        qs[...] = _rope(q_ref[...], c_ref[...], s_ref[...], half, HEAD_DIM_B)

        @pl.when(t % tiles_per_group == 0)
        def _():
            kg[...] = _both_halves(_rope(k_ref[...], c_ref[...], s_ref[...], half, HEAD_DIM_B), g)
            vg[...] = _both_halves(v_ref[...], g)

        @pl.when(t == 0)
        def _():
            fb = _fwd_bias(True)
            bias[...] = jnp.concatenate([fb, fb], axis=1)

        sink = _head_rows(sink_ref, t, BLOCK)

        def blk(it, carry):
            ns = [it * UNROLL_B + u for u in range(UNROLL_B)]
            rcs = [pl.ds(pl.multiple_of(n * BLOCK, BLOCK), BLOCK) for n in ns]
            rws = [pl.ds(pl.multiple_of(jnp.maximum(n - 1, 0) * BLOCK, BLOCK), 2 * BLOCK) for n in ns]
            ss = [_dot_nt(_stack_heads(qs[rc, :]).astype(BF16), kg[rw, :].astype(BF16)) * scale + bias[jnp.minimum(n, 1)]
                  for n, rc, rw in zip(ns, rcs, rws)]
            ms = [jnp.max(s, axis=-1, keepdims=True) for s in ss]
            ps = [jnp.exp(s - m) for s, m in zip(ss, ms)]
            ls = [jnp.sum(p, axis=-1, keepdims=True) for p in ps]
            accs = [_dot(p.astype(BF16), vg[rw, :].astype(BF16)) for p, rw in zip(ps, rws)]
            for rc, m, l, acc in zip(rcs, ms, ls, accs):
                m2 = jnp.maximum(m, sink)
                c = jnp.exp(m - m2)
                den = l * c + jnp.exp(sink - m2)
                y_ref[rc, :] = _unstack_heads(acc * (c / den))
                lse_ref[rc, :] = _unstack_heads(jnp.broadcast_to(m2 + jnp.log(den), (2 * BLOCK, LANES)))
            return carry

        lax.fori_loop(0, nblk // UNROLL_B, blk, 0)

    full = lambda col: pl.BlockSpec((S, LANES), lambda t: (0, col))
    out = pl.BlockSpec((S, LANES), lambda t: (0, t))
    o_shape = jax.ShapeDtypeStruct((S, n_qtiles * LANES), F32)
    return pl.pallas_call(
        body, name=name, grid=(n_qtiles,),
        in_specs=[pl.BlockSpec((S, LANES), lambda t: (0, off_q + t)), full(off_q + n_qtiles), full(off_q + n_qtiles + 1),
                  full(0), full(0), pl.BlockSpec(sinks_l.shape, lambda t: (0, 0))],
        out_specs=(out, out), out_shape=(o_shape, o_shape),
        scratch_shapes=[pltpu.VMEM((S, LANES), F32) for _ in range(3)] + [pltpu.VMEM((2, 2 * BLOCK, 2 * BLOCK), F32)],
        compiler_params=_params(("arbitrary",)),
    )(h, h, h, cb, sb, sinks_l)


def attn_b_bwd(h, cb, sb, sinks_l, yb, lse, dyb, dh, *, off_q, n_qtiles, name):
    S = h.shape[0]
    nblk = S // BLOCK
    scale = HEAD_DIM_B ** -0.5
    half = HEAD_DIM_B // 8
    tiles_per_group = n_qtiles // 2
    n_steps = n_qtiles + 2
    W = 2 * BLOCK
    assert nblk >= 2 and nblk % UNROLL_B == 0

    def body(q_ref, k_ref, v_ref, c_ref, s_ref, sink_ref, y_ref, lse_ref, dy_ref, dh_in,
             o_ref, dsink_ref, qs, kg, vg, dk_acc, dv_acc, bias, dq_ref):
        t = pl.program_id(0)

        @pl.when(t == 0)
        def _():
            dk_acc[...] = jnp.zeros_like(dk_acc)
            dv_acc[...] = jnp.zeros_like(dv_acc)
            dsink_ref[...] = jnp.zeros_like(dsink_ref)
            bb = _bwd_bias(True)
            bias[...] = jnp.concatenate([bb, bb], axis=1)

        @pl.when(t < n_qtiles)
        def _():
            g = t // tiles_per_group
            qs[...] = _rope(q_ref[...], c_ref[...], s_ref[...], half, HEAD_DIM_B)

            @pl.when(t % tiles_per_group == 0)
            def _():
                kg[...] = _both_halves(_rope(k_ref[...], c_ref[...], s_ref[...], half, HEAD_DIM_B), g)
                vg[...] = _both_halves(v_ref[...], g)

            dq_ref[...] = jnp.zeros_like(dq_ref)
            sink = _head_rows(sink_ref, t, W)
            row = lax.broadcasted_iota(jnp.int32, (2 * W, 1), 0) % W
            low = _lane((W, LANES)) < HEAD_DIM_B

            def blk(it, dsink):
                js = [it * UNROLL_B_BWD + u for u in range(UNROLL_B_BWD)]
                rks = [pl.ds(pl.multiple_of(j * BLOCK, BLOCK), BLOCK) for j in js]
                rws = [pl.ds(pl.multiple_of(jnp.minimum(j, nblk - 2) * BLOCK, BLOCK), W) for j in js]
                q2 = [_stack_heads(qs[rw, :]).astype(BF16) for rw in rws]
                dy2 = [_stack_heads(dy_ref[rw, :]) for rw in rws]
                dsum = [jnp.sum(dy * _stack_heads(y_ref[rw, :]), axis=-1, keepdims=True) for dy, rw in zip(dy2, rws)]
                dy2b = [dy.astype(BF16) for dy in dy2]
                lse2 = []
                for rw in rws:
                    lt = lse_ref[rw, :]
                    lr = pltpu.roll(lt, HEAD_DIM_B, 1)
                    lse2.append(jnp.concatenate([jnp.where(low, lt, lr), jnp.where(low, lr, lt)], axis=0))
                kb = [kg[rk, :].astype(BF16) for rk in rks]
                ss = [_dot_nt(q, k) * scale + bias[(j == nblk - 1).astype(jnp.int32)] for q, k, j in zip(q2, kb, js)]
                dps = [_dot_nt(dy, vg[rk, :].astype(BF16)) for dy, rk in zip(dy2b, rks)]
                ps = [jnp.exp(s - l2) for s, l2 in zip(ss, lse2)]
                dss = [(p * (dp - dm) * scale).astype(BF16) for p, dp, dm in zip(ps, dps, dsum)]
                dvs = [_fold_halves(_dot_tn(p.astype(BF16), dy)) for p, dy in zip(ps, dy2b)]
                dks = [_fold_halves(_dot_tn(ds, q)) for ds, q in zip(dss, q2)]
                dqs = [_dot(ds, k) for ds, k in zip(dss, kb)]
                for j, rk, rw, dv, dk, dq, l2, dm in zip(js, rks, rws, dvs, dks, dqs, lse2, dsum):
                    dv_acc[g, rk, :] += dv
                    dk_acc[g, rk, :] += dk
                    dq_ref[rw, :] += _unstack_heads(dq)
                    diag = (row >= BLOCK).astype(jnp.int32) == (j == nblk - 1).astype(jnp.int32)
                    dsink = dsink - jnp.where(diag, jnp.exp(sink - l2[:, :1]) * dm, 0.0)
                return dsink

            assert nblk % UNROLL_B_BWD == 0
            dsink = lax.fori_loop(0, nblk // UNROLL_B_BWD, blk, jnp.zeros((2 * W, 1), F32))
            o_ref[...] = _rope_t(dq_ref[...], c_ref[...], s_ref[...], half, HEAD_DIM_B).astype(o_ref.dtype)
            d0 = jnp.sum(dsink[:W], axis=0, keepdims=True)
            d1 = jnp.sum(dsink[W:], axis=0, keepdims=True)
            dsink_ref[pl.ds(2 * t, 1), :] = jnp.broadcast_to(d0, (1, LANES))
            dsink_ref[pl.ds(2 * t + 1, 1), :] = jnp.broadcast_to(d1, (1, LANES))

        low_s = _lane((S, LANES)) < HEAD_DIM_B

        @pl.when(t == n_qtiles)
        def _():
            dk = jnp.where(low_s, dk_acc[0], dk_acc[1])
            o_ref[...] = _rope_t(dk, c_ref[...], s_ref[...], half, HEAD_DIM_B).astype(o_ref.dtype)

        @pl.when(t == n_qtiles + 1)
        def _():
            o_ref[...] = jnp.where(low_s, dv_acc[0], dv_acc[1]).astype(o_ref.dtype)

    qt = lambda t: jnp.minimum(t, n_qtiles - 1)
    full = lambda col: pl.BlockSpec((S, LANES), lambda t: (0, col))
    per_tile = pl.BlockSpec((S, LANES), lambda t: (0, qt(t)))
    return pl.pallas_call(
        body, name=name, grid=(n_steps,),
        in_specs=[pl.BlockSpec((S, LANES), lambda t: (0, off_q + qt(t))), full(off_q + n_qtiles),
                  full(off_q + n_qtiles + 1), full(0), full(0), pl.BlockSpec(sinks_l.shape, lambda t: (0, 0)),
                  per_tile, per_tile, per_tile, pl.BlockSpec(memory_space=pl.ANY)],
        out_specs=(pl.BlockSpec((S, LANES), lambda t: (0, off_q + t)), pl.BlockSpec(sinks_l.shape, lambda t: (0, 0))),
        out_shape=(jax.ShapeDtypeStruct(dh.shape, dh.dtype), jax.ShapeDtypeStruct(sinks_l.shape, F32)),
        input_output_aliases={9: 0},
        scratch_shapes=[pltpu.VMEM((S, LANES), F32) for _ in range(3)]
        + [pltpu.VMEM((2, S, LANES), F32), pltpu.VMEM((2, S, LANES), F32), pltpu.VMEM((2, 2 * W, BLOCK), F32),
           pltpu.VMEM((S, LANES), F32)],
        compiler_params=_params(("arbitrary",)),
    )(h, h, h, cb, sb, sinks_l, yb, lse, dyb, dh)


def mem_attn_fwd(q, kv, *, name, tm=512):
    S, D = q.shape
    n_mem = kv.shape[0]
    hd = D // N_MEM_HEADS
    scale = hd ** -0.5
    tm = _tile(S, tm, 8)

    def body(q_ref, kv_ref, o_ref):
        for hh in range(N_MEM_HEADS):
            cols = slice(hh * hd, (hh + 1) * hd)
            s = _dot_nt(q_ref[:, cols], kv_ref[:, cols]) * scale
            s = s - jnp.max(s, axis=-1, keepdims=True)
            e = jnp.exp(s)
            p = e / jnp.sum(e, axis=-1, keepdims=True)
            o_ref[:, cols] = _dot(p.astype(BF16), kv_ref[:, D + hh * hd:D + (hh + 1) * hd]).astype(o_ref.dtype)

    return pl.pallas_call(
        body, name=name, grid=(S // tm,),
        in_specs=[pl.BlockSpec((tm, D), lambda i: (i, 0)), pl.BlockSpec((n_mem, 2 * D), lambda i: (0, 0))],
        out_specs=pl.BlockSpec((tm, D), lambda i: (i, 0)),
        out_shape=jax.ShapeDtypeStruct((S, D), BF16),
        compiler_params=_params(("parallel",)),
    )(q, kv)


def mem_attn_bwd(q, kv, do, *, name, tm=512):
    S, D = q.shape
    n_mem = kv.shape[0]
    hd = D // N_MEM_HEADS
    scale = hd ** -0.5
    tm = _tile(S, tm, 8)

    def body(q_ref, kv_ref, do_ref, dq_ref, dkv_ref):
        i = pl.program_id(0)

        @pl.when(i == 0)
        def _():
            dkv_ref[...] = jnp.zeros_like(dkv_ref)

        for hh in range(N_MEM_HEADS):
            cols = slice(hh * hd, (hh + 1) * hd)
            vcols = slice(D + hh * hd, D + (hh + 1) * hd)
            qh, kh, vh, doh = q_ref[:, cols], kv_ref[:, cols], kv_ref[:, vcols], do_ref[:, cols]
            s = _dot_nt(qh, kh) * scale
            s = s - jnp.max(s, axis=-1, keepdims=True)
            e = jnp.exp(s)
            p = e / jnp.sum(e, axis=-1, keepdims=True)
            dp = _dot_nt(doh, vh)
            ds = (p * (dp - jnp.sum(dp * p, axis=-1, keepdims=True)) * scale).astype(BF16)
            dq_ref[:, cols] = _dot(ds, kh).astype(dq_ref.dtype)
            dkv_ref[:, cols] += _dot_tn(ds, qh)
            dkv_ref[:, vcols] += _dot_tn(p.astype(BF16), doh)

    row = pl.BlockSpec((tm, D), lambda i: (i, 0))
    kvs = pl.BlockSpec((n_mem, 2 * D), lambda i: (0, 0))
    return pl.pallas_call(
        body, name=name, grid=(S // tm,),
        in_specs=[row, kvs, row], out_specs=(row, kvs),
        out_shape=(jax.ShapeDtypeStruct((S, D), BF16), jax.ShapeDtypeStruct((n_mem, 2 * D), F32)),
        compiler_params=_params(("arbitrary",)),
    )(q, kv, do)


def _rows_view(t):
    return t.reshape(-1, t.shape[-1])


def _row_tile(rows, cols, target_elems=512 * 1024):
    return _tile(rows, max(8, target_elems // cols), 8)


def cast_bf16(w, *, name):
    v = _rows_view(w)
    R, C = v.shape
    tr = _row_tile(R, C)

    def body(w_ref, o_ref):
        o_ref[...] = w_ref[...].astype(BF16)

    spec = pl.BlockSpec((tr, C), lambda i: (i, 0))
    out = pl.pallas_call(body, name=name, grid=(R // tr,), in_specs=[spec], out_specs=spec,
                         out_shape=jax.ShapeDtypeStruct((R, C), BF16), compiler_params=_params(("parallel",)))(v)
    return out.reshape(w.shape)


def mesh_place():
    return tuple(lax.axis_index(a).astype(jnp.int32).reshape(1) for a in ("x", "y", "c"))


def pair_sum(p, r1, place, *, name):
    nsh, r, c = p.shape
    hr = r // 2
    tr = _row_tile(hr, c)
    nt = hr // tr

    def body(x_ref, y_ref, c_ref, p_ref, r_ref, o_ref):
        o_ref[...] = (p_ref[...].astype(F32) + r_ref[...].astype(F32)).astype(BF16)

    return pl.pallas_call(
        body, name=name,
        grid_spec=pltpu.PrefetchScalarGridSpec(
            num_scalar_prefetch=3, grid=(nsh, nt),
            in_specs=[pl.BlockSpec((None, tr, c), lambda s, i, x, y, cc: (s, cc[0] * nt + i, 0)),
                      pl.BlockSpec((None, tr, c), lambda s, i, x, y, cc: (s, i, 0))],
            out_specs=pl.BlockSpec((None, tr, c), lambda s, i, x, y, cc: (s, i, 0))),
        out_shape=jax.ShapeDtypeStruct((nsh, hr, c), BF16),
        compiler_params=_params(("parallel", "parallel")),
    )(*place, p, r1)


def cast_into_slot(w, li, place, *, name):
    _, R, C = w.shape
    tr = _row_tile(R, C)

    def body(x_ref, y_ref, c_ref, w_ref, o_ref):
        o_ref[...] = w_ref[...].astype(BF16)

    return pl.pallas_call(
        body, name=name,
        grid_spec=pltpu.PrefetchScalarGridSpec(
            num_scalar_prefetch=3, grid=(R // tr,),
            in_specs=[pl.BlockSpec((None, tr, C), lambda i, x, y, cc: (li, i, 0))],
            out_specs=pl.BlockSpec((None, tr, C), lambda i, x, y, cc: (2 * x[0] + y[0], i, 0))),
        out_shape=jax.ShapeDtypeStruct((N_CHIPS, R, C), BF16),
        compiler_params=_params(("parallel",)),
    )(*place, w)


def chip_sum(q, r2, place, gbuf, li, *, name):
    _, hr, c = q.shape
    tr = _row_tile(hr, c, 256 * 1024)
    nt = hr // tr

    def body(x_ref, y_ref, c_ref, q_ref, r_ref, g_in, o_ref):
        acc = q_ref[...].astype(F32)
        for k in range(r_ref.shape[0]):
            acc = acc + r_ref[k].astype(F32)
        o_ref[...] = acc

    return pl.pallas_call(
        body, name=name,
        grid_spec=pltpu.PrefetchScalarGridSpec(
            num_scalar_prefetch=3, grid=(nt,),
            in_specs=[pl.BlockSpec((None, tr, c), lambda i, x, y, cc: (2 * x[0] + y[0], i, 0)),
                      pl.BlockSpec((r2.shape[0], tr, c), lambda i, x, y, cc: (0, i, 0)),
                      pl.BlockSpec(memory_space=pl.ANY)],
            out_specs=pl.BlockSpec((None, tr, c), lambda i, x, y, cc: (li, cc[0] * nt + i, 0))),
        out_shape=jax.ShapeDtypeStruct(gbuf.shape, F32),
        input_output_aliases={5: 0},
        compiler_params=_params(("parallel",)),
    )(*place, q, r2, gbuf)


def adamw(w, g, m, v, *, name, emit_g=False):
    shape = w.shape
    wv, gv, mv, vv = (_rows_view(t) for t in (w, g, m, v))
    R, C = wv.shape
    tr = _row_tile(R, C, 256 * 1024)
    c1 = 1.0 / (1.0 - ADAM_B1 ** ADAM_STEP)
    c2 = 1.0 / (1.0 - ADAM_B2 ** ADAM_STEP)
    n_out = 4 if emit_g else 3

    def body(w_ref, g_ref, m_ref, v_ref, d_ref, nm_ref, nv_ref, *go_ref):
        g_ = g_ref[...]
        nm = ADAM_B1 * m_ref[...] + (1.0 - ADAM_B1) * g_
        nv = ADAM_B2 * v_ref[...] + (1.0 - ADAM_B2) * (g_ * g_)
        m_hat = nm * c1
        v_hat = nv * c2
        d_ref[...] = -ADAM_LR * (m_hat / (jnp.sqrt(v_hat) + ADAM_EPS) + ADAM_WD * w_ref[...])
        nm_ref[...] = nm
        nv_ref[...] = nv
        if emit_g:
            go_ref[0][...] = g_

    spec = pl.BlockSpec((tr, C), lambda i: (i, 0))
    o = jax.ShapeDtypeStruct((R, C), F32)
    outs = pl.pallas_call(body, name=name, grid=(R // tr,), in_specs=[spec] * 4, out_specs=(spec,) * n_out,
                          out_shape=(o,) * n_out, compiler_params=_params(("parallel",)))(wv, gv, mv, vv)
    return tuple(t.reshape(shape) for t in outs)


def _place():
    x, y, c = lax.axis_index("x"), lax.axis_index("y"), lax.axis_index("c")
    others = [(1 - x, y), (x, 1 - y), (1 - x, 1 - y)]
    return x, y, c, others


def _any_specs(n):
    return [pl.BlockSpec(memory_space=pl.ANY) for _ in range(n)]


HBM_SPEC = pl.BlockSpec(memory_space=pltpu.HBM)
SEM_SPEC = pl.BlockSpec(memory_space=pltpu.SEMAPHORE)
DATAFLOW = pltpu.SideEffectType.DATAFLOW_SIDE_EFFECTING
TOKEN = jax.ShapeDtypeStruct((8, LANES), F32)


def _in_hbm(arrays):
    return [pltpu.with_memory_space_constraint(a, pltpu.HBM) for a in arrays]


def _gather_copy(g, t, j, slot, px, py, c, send, recv):
    hr = g[t].shape[1] // 2
    rows = g[t].at[slot, pl.ds(c * hr, hr)]
    return pltpu.make_async_remote_copy(rows, rows, send.at[3 * t + j], recv.at[3 * t + j], device_id=(px, py, c), device_id_type=MESH)


def gather_start(gs, after, *, name):
    n = len(gs)

    def body(*refs):
        g, token = refs[:n], refs[-1]
        send, recv = refs[n + 1], refs[n + 2]
        x, y, c, others = _place()
        for t in range(n):
            for j, (px, py) in enumerate(others):
                _gather_copy(g, t, j, 2 * x + y, px, py, c, send, recv).start()
        token[...] = jnp.zeros_like(token)

    outs = pl.pallas_call(
        body, name=name,
        in_specs=[HBM_SPEC] * n + [pl.BlockSpec(memory_space=pl.ANY)],
        out_specs=(SEM_SPEC, SEM_SPEC, *[HBM_SPEC] * n, pl.BlockSpec(memory_space=pltpu.VMEM)),
        out_shape=(pltpu.SemaphoreType.DMA((3 * n,)), pltpu.SemaphoreType.DMA((3 * n,)),
                   *[pltpu.HBM(g.shape, g.dtype) for g in gs], TOKEN),
        input_output_aliases={t: 2 + t for t in range(n)},
        compiler_params=pltpu.CompilerParams(has_side_effects=DATAFLOW),
    )(*_in_hbm(gs), after)
    return outs[0], outs[1], list(outs[2:2 + n]), outs[-1]


def gather_wait(send, recv, gs, after, *, name):
    n = len(gs)

    def body(*refs):
        g, token = refs[:n], refs[-1]
        send, recv = refs[n], refs[n + 1]
        x, y, c, others = _place()
        for t in range(n):
            for j, (px, py) in enumerate(others):
                _gather_copy(g, t, j, 2 * x + y, px, py, c, send, recv).wait_send()
                _gather_copy(g, t, j, 2 * px + py, px, py, c, send, recv).wait_recv()
        token[...] = jnp.zeros_like(token)

    outs = pl.pallas_call(
        body, name=name,
        in_specs=[HBM_SPEC] * n + [SEM_SPEC, SEM_SPEC, pl.BlockSpec(memory_space=pl.ANY)],
        out_specs=tuple([HBM_SPEC] * n) + (pl.BlockSpec(memory_space=pltpu.VMEM),),
        out_shape=tuple(pltpu.HBM(g.shape, g.dtype) for g in gs) + (TOKEN,),
        input_output_aliases={t: t for t in range(n)},
        compiler_params=pltpu.CompilerParams(has_side_effects=DATAFLOW),
    )(*gs, send, recv, after)
    return list(outs[:n]), outs[n]


def gather_forward(gs, *, name):
    n = len(gs)

    def body(*refs):
        g = refs[n:2 * n]
        send, recv = refs[2 * n:]
        x, y, c, others = _place()
        cps = []
        for t in range(n):
            hr = g[t].shape[1] // 2
            for j, (px, py) in enumerate(others):
                rows = g[t].at[2 * px + py, pl.ds(c * hr, hr)]
                cp = pltpu.make_async_remote_copy(rows, rows, send.at[3 * t + j], recv.at[3 * t + j],
                                                  device_id=(x, y, 1 - c), device_id_type=MESH)
                cp.start()
                cps.append(cp)
        for t in range(n):
            hr = g[t].shape[1] // 2
            for j, (px, py) in enumerate(others):
                rows = g[t].at[2 * px + py, pl.ds((1 - c) * hr, hr)]
                pltpu.make_async_remote_copy(rows, rows, send.at[3 * t + j], recv.at[3 * t + j],
                                             device_id=(x, y, 1 - c), device_id_type=MESH).wait_recv()
        for cp in cps:
            cp.wait_send()

    return pl.pallas_call(
        body, name=name,
        in_specs=_any_specs(n), out_specs=_any_specs(n),
        out_shape=[jax.ShapeDtypeStruct(g.shape, g.dtype) for g in gs],
        input_output_aliases={t: t for t in range(n)},
        scratch_shapes=[pltpu.SemaphoreType.DMA((3 * n,)), pltpu.SemaphoreType.DMA((3 * n,))],
        compiler_params=pltpu.CompilerParams(has_side_effects=True),
    )(*gs)


def sibling_halves(parts, *, name):
    n = len(parts)

    def body(*refs):
        src, dst = refs[:n], refs[n:2 * n]
        send, recv = refs[2 * n:]
        x, y, c, _ = _place()
        cps = []
        for t in range(n):
            hr = src[t].shape[1] // 2
            cp = pltpu.make_async_remote_copy(src[t].at[:, pl.ds((1 - c) * hr, hr)], dst[t], send.at[t], recv.at[t],
                                              device_id=(x, y, 1 - c), device_id_type=MESH)
            cp.start()
            cps.append(cp)
        for cp in cps:
            cp.wait()

    return pl.pallas_call(
        body, name=name,
        in_specs=_any_specs(n), out_specs=_any_specs(n),
        out_shape=[jax.ShapeDtypeStruct((p.shape[0], p.shape[1] // 2, p.shape[2]), p.dtype) for p in parts],
        scratch_shapes=[pltpu.SemaphoreType.DMA((n,)), pltpu.SemaphoreType.DMA((n,))],
        compiler_params=pltpu.CompilerParams(has_side_effects=True),
    )(*parts)


def _chips_copy(q, land, t, j, px, py, c, send, recv):
    return pltpu.make_async_remote_copy(q[t].at[2 * px + py], land[t].at[j], send.at[3 * t + j], recv.at[3 * t + j],
                                        device_id=(px, py, c), device_id_type=MESH)


def chips_start(qs, *, name):
    n = len(qs)

    def body(*refs):
        q, land, token = refs[:n], refs[n:2 * n], refs[-1]
        send, recv = refs[2 * n], refs[2 * n + 1]
        x, y, c, others = _place()
        for t in range(n):
            for j, (px, py) in enumerate(others):
                _chips_copy(q, land, t, j, px, py, c, send, recv).start()
        token[...] = jnp.zeros_like(token)

    lands = [lax.empty((3,) + q.shape[1:], q.dtype) for q in qs]
    outs = pl.pallas_call(
        body, name=name,
        in_specs=[HBM_SPEC] * (2 * n),
        out_specs=(SEM_SPEC, SEM_SPEC, *[HBM_SPEC] * (2 * n), pl.BlockSpec(memory_space=pltpu.VMEM)),
        out_shape=(pltpu.SemaphoreType.DMA((3 * n,)), pltpu.SemaphoreType.DMA((3 * n,)),
                   *[pltpu.HBM(a.shape, a.dtype) for a in qs + lands], TOKEN),
        input_output_aliases={t: 2 + t for t in range(2 * n)},
        compiler_params=pltpu.CompilerParams(has_side_effects=DATAFLOW),
    )(*_in_hbm(qs + lands))
    return outs[0], outs[1], list(outs[2:2 + n]), list(outs[2 + n:2 + 2 * n]), outs[-1]


def chips_wait(send, recv, qs, lands, after, *, name):
    n = len(qs)

    def body(*refs):
        q, land = refs[:n], refs[n:2 * n]
        send, recv = refs[2 * n], refs[2 * n + 1]
        x, y, c, others = _place()
        for t in range(n):
            for j, (px, py) in enumerate(others):
                cp = _chips_copy(q, land, t, j, px, py, c, send, recv)
                cp.wait_send()
                cp.wait_recv()

    outs = pl.pallas_call(
        body, name=name,
        in_specs=[HBM_SPEC] * (2 * n) + [SEM_SPEC, SEM_SPEC, pl.BlockSpec(memory_space=pl.ANY)],
        out_specs=tuple([HBM_SPEC] * (2 * n)),
        out_shape=tuple(pltpu.HBM(a.shape, a.dtype) for a in qs + lands),
        input_output_aliases={t: t for t in range(2 * n)},
        compiler_params=pltpu.CompilerParams(has_side_effects=DATAFLOW),
    )(*qs, *lands, send, recv, after)
    return list(outs[:n]), list(outs[n:])


def join_halves(fulls, li, *, name):
    n = len(fulls)

    def body(*refs):
        g = refs[n:2 * n]
        send, recv = refs[2 * n:]
        x, y, c, _ = _place()
        cps = []
        for t in range(n):
            hr = g[t].shape[1] // 2
            rows = g[t].at[li, pl.ds(c * hr, hr)]
            cp = pltpu.make_async_remote_copy(rows, rows, send.at[t], recv.at[t], device_id=(x, y, 1 - c), device_id_type=MESH)
            cp.start()
            cps.append(cp)
        for t in range(n):
            hr = g[t].shape[1] // 2
            rows = g[t].at[li, pl.ds((1 - c) * hr, hr)]
            pltpu.make_async_remote_copy(rows, rows, send.at[t], recv.at[t],
                                         device_id=(x, y, 1 - c), device_id_type=MESH).wait_recv()
        for cp in cps:
            cp.wait_send()

    return pl.pallas_call(
        body, name=name,
        in_specs=_any_specs(n), out_specs=_any_specs(n),
        out_shape=[jax.ShapeDtypeStruct(g.shape, g.dtype) for g in fulls],
        input_output_aliases={t: t for t in range(n)},
        scratch_shapes=[pltpu.SemaphoreType.DMA((n,)), pltpu.SemaphoreType.DMA((n,))],
        compiler_params=pltpu.CompilerParams(has_side_effects=True),
    )(*fulls)


def _join_copy(g, t, li, half, c, x, y, send, recv):
    hr = g[t].shape[1] // 2
    rows = g[t].at[li, pl.ds(half * hr, hr)]
    return pltpu.make_async_remote_copy(rows, rows, send.at[t], recv.at[t], device_id=(x, y, 1 - c), device_id_type=MESH)


def join_start(fulls, li, *, name):
    n = len(fulls)

    def body(*refs):
        g, token = refs[:n], refs[-1]
        send, recv = refs[n], refs[n + 1]
        x, y, c, _ = _place()
        for t in range(n):
            _join_copy(g, t, li, c, c, x, y, send, recv).start()
        token[...] = jnp.zeros_like(token)

    outs = pl.pallas_call(
        body, name=name,
        in_specs=[HBM_SPEC] * n,
        out_specs=(SEM_SPEC, SEM_SPEC, *[HBM_SPEC] * n, pl.BlockSpec(memory_space=pltpu.VMEM)),
        out_shape=(pltpu.SemaphoreType.DMA((n,)), pltpu.SemaphoreType.DMA((n,)),
                   *[pltpu.HBM(g.shape, g.dtype) for g in fulls], TOKEN),
        input_output_aliases={t: 2 + t for t in range(n)},
        compiler_params=pltpu.CompilerParams(has_side_effects=DATAFLOW),
    )(*_in_hbm(fulls))
    return outs[0], outs[1], list(outs[2:2 + n]), outs[-1]


def join_wait(send, recv, fulls, li, after, *, name):
    n = len(fulls)

    def body(*refs):
        g = refs[:n]
        send, recv = refs[n], refs[n + 1]
        x, y, c, _ = _place()
        for t in range(n):
            _join_copy(g, t, li, c, c, x, y, send, recv).wait_send()
            _join_copy(g, t, li, 1 - c, c, x, y, send, recv).wait_recv()

    outs = pl.pallas_call(
        body, name=name,
        in_specs=[HBM_SPEC] * n + [SEM_SPEC, SEM_SPEC, pl.BlockSpec(memory_space=pl.ANY)],
        out_specs=tuple([HBM_SPEC] * n),
        out_shape=tuple(pltpu.HBM(g.shape, g.dtype) for g in fulls),
        input_output_aliases={t: t for t in range(n)},
        compiler_params=pltpu.CompilerParams(has_side_effects=DATAFLOW),
    )(*fulls, send, recv, after)
    return list(outs)


def allreduce_small(t, *, name):
    R, C = t.shape

    def body(t_ref, o_ref, land, send, recv):
        x, y, c, _ = _place()
        me = 4 * x + 2 * y + c
        land[me] = t_ref[...]
        cps = []
        for j in range(1, 8):
            px, py, pc = (x + (j >> 2)) % 2, (y + ((j >> 1) & 1)) % 2, (c + (j & 1)) % 2
            cp = pltpu.make_async_remote_copy(t_ref, land.at[me], send.at[j - 1], recv.at[j - 1],
                                              device_id=(px, py, pc), device_id_type=MESH)
            cp.start()
            cps.append(cp)
        for j in range(1, 8):
            px, py, pc = (x + (j >> 2)) % 2, (y + ((j >> 1) & 1)) % 2, (c + (j & 1)) % 2
            pltpu.make_async_remote_copy(t_ref, land.at[4 * px + 2 * py + pc], send.at[j - 1], recv.at[j - 1],
                                         device_id=(px, py, pc), device_id_type=MESH).wait_recv()
        for cp in cps:
            cp.wait_send()
        acc = land[0]
        for k in range(1, 8):
            acc = acc + land[k]
        o_ref[...] = acc

    return pl.pallas_call(
        body, name=name,
        in_specs=[pl.BlockSpec(memory_space=pltpu.VMEM)], out_specs=pl.BlockSpec(memory_space=pltpu.VMEM),
        out_shape=jax.ShapeDtypeStruct((R, C), F32),
        scratch_shapes=[pltpu.VMEM((8, R, C), F32), pltpu.SemaphoreType.DMA((7,)), pltpu.SemaphoreType.DMA((7,))],
        compiler_params=pltpu.CompilerParams(has_side_effects=True),
    )(t)


def _layer_fwd(x, xb, memb, w_in, rest, P, tabs, alpha, li):
    ca, sa, cb, sb = tabs
    nA = P["gn_a"].shape[1] // HEAD_DIM_A
    nQ = P["gn_b"].shape[1] // LANES
    nm = lambda s: f"L{li}_{s}"
    h = mm_nn(xb, w_in, name=nm("h"), out_dtype=F32, tn=2304)
    ya, lse_a = attn_a_fwd(h, ca, sa, n_heads=nA, name=nm("attn_a"))
    yb, lse_b = attn_b_fwd(h, cb, sb, P["sinks_l"], off_q=3 * nA, n_qtiles=nQ, name=nm("attn_b"))
    ymix = rms_fwd(ya, yb, P["gn_a"], P["gn_b"], name=nm("rms"))
    W, P = rest(ymix, P)
    z1, x1, x1b = mm_ln(ymix, W["w_out"][0], x, P["ln_mix_g"], P["ln_mix_b"], name=nm("out_ln"), alpha=alpha,
                        tm=256, tk=ymix.shape[1])
    qm = mm_nn(x1b, W["w_mq"], name=nm("mq"), out_dtype=BF16)
    kv = mm_nn(memb, W["w_mkv"], name=nm("mkv"), out_dtype=BF16, tm=256)
    o = mem_attn_fwd(qm, kv, name=nm("mem_attn"))
    z2, x2, x2b = mm_ln(o, W["w_mo"][0], x1, P["ln_mem_g"], P["ln_mem_b"], name=nm("mo_ln"), alpha=alpha,
                        tm=256, tk=o.shape[1])
    u, a = mm_nn(x2b, W["w_up"], name=nm("up"), out_dtype=BF16, relu2=True)
    z3, x3, x3b = mm_ln(a, W["w_down"][0], x2, P["ln_ff_g"], P["ln_ff_b"], name=nm("down_ln"), alpha=alpha)
    saved = dict(xb=xb, h=h, ya=ya, lse_a=lse_a, yb=yb, lse_b=lse_b, ymix=ymix, z1=z1, x1b=x1b, qm=qm, kv=kv, o=o,
                 z2=z2, x2b=x2b, u=u, a=a, z3=z3)
    return x3, x3b, saved


def _layer_bwd(dx3, sv, memb, W, P, tabs, alpha, li, hook=None):
    ca, sa, cb, sb = tabs
    nA = P["gn_a"].shape[1] // HEAD_DIM_A
    nQ = P["gn_b"].shape[1] // LANES
    nm = lambda s: f"L{li}_b_{s}"
    nsh = lambda k: W[k].shape[0]
    gw, gs = {}, {}
    dz3, dz3b, gs["ln_ff_g"], gs["ln_ff_b"] = ln_bwd(dx3, sv["z3"], P["ln_ff_g"], name=nm("ln_ff"))
    gw["w_down"] = mm_tn(sv["a"], dz3b, nsh("w_down"), name=nm("dw_down"))
    du = mm_nt(dz3b, W["w_down"], name=nm("du"), out_dtype=BF16, umul=sv["u"])
    gw["w_up"] = mm_tn(sv["x2b"], du, nsh("w_up"), name=nm("dw_up"))
    dx2 = mm_nt(du, W["w_up"], name=nm("dx2"), out_dtype=F32, resid=dz3, alpha=alpha)
    dz2, dz2b, gs["ln_mem_g"], gs["ln_mem_b"] = ln_bwd(dx2, sv["z2"], P["ln_mem_g"], name=nm("ln_mem"))
    gw["w_mo"] = mm_tn(sv["o"], dz2b, nsh("w_mo"), name=nm("dw_mo"))
    do = mm_nt(dz2b, W["w_mo"], name=nm("do"), out_dtype=BF16)
    dqm, dkv = mem_attn_bwd(sv["qm"], sv["kv"], do, name=nm("mem_attn"))
    gw["w_mq"] = mm_tn(sv["x1b"], dqm, nsh("w_mq"), name=nm("dw_mq"))
    gw["w_mkv"] = mm_tn(memb, cast_bf16(dkv, name=nm("dkv_cast")), nsh("w_mkv"), name=nm("dw_mkv"), tm=256)
    dx1 = mm_nt(dqm, W["w_mq"], name=nm("dx1"), out_dtype=F32, resid=dz2, alpha=alpha)
    if hook is not None:
        P = hook(gw, dx1, P)
    dz1, dz1b, gs["ln_mix_g"], gs["ln_mix_b"] = ln_bwd(dx1, sv["z1"], P["ln_mix_g"], name=nm("ln_mix"))
    gw["w_out"] = mm_tn(sv["ymix"], dz1b, nsh("w_out"), name=nm("dw_out"))
    dymix = mm_nt(dz1b, W["w_out"], name=nm("dymix"), out_dtype=F32)
    dya, dyb, gs["gn_a"], gs["gn_b"] = rms_bwd(dymix, sv["ya"], sv["yb"], P["gn_a"], P["gn_b"], name=nm("rms"))
    dh = attn_a_bwd(sv["h"], ca, sa, sv["ya"], sv["lse_a"], dya, n_heads=nA, name=nm("attn_a"))
    dh, gs["sinks"] = attn_b_bwd(sv["h"], cb, sb, P["sinks_l"], sv["yb"], sv["lse_b"], dyb, dh,
                                 off_q=3 * nA, n_qtiles=nQ, name=nm("attn_b"))
    gw["w_in"] = mm_tn(sv["xb"], dh, nsh("w_in"), name=nm("dw_in"), tn=2304)
    dx0 = mm_nt(dh, W["w_in"], name=nm("dx0"), out_dtype=F32, resid=dz1, alpha=alpha, tr=2304)
    return dx0, gw, gs


def _gathered_view(name, g):
    if name == "w_in":
        return jnp.concatenate([g[k] for k in range(N_CHIPS)], axis=1)[None]
    if name in COL_SHARDED:
        return g
    return g.reshape(1, g.shape[0] * g.shape[1], g.shape[2])


def _to_shards(name, gw):
    if name == "w_in":
        n = gw.shape[2] // N_CHIPS
        return jnp.stack([gw[0, :, k * n:(k + 1) * n] for k in range(N_CHIPS)])
    if name in COL_SHARDED:
        return gw
    return gw.reshape(N_CHIPS, gw.shape[1] // N_CHIPS, gw.shape[2])


def _step(x, mem, positions, loss_target, w, m, v):
    S, D = x.shape[1], x.shape[2]
    depth = w["w_in"].shape[0]
    alpha = (2 * depth) ** 0.25
    x0 = x[0]
    memb = cast_bf16(mem[0], name="mem_cast")
    pos = positions[0]
    tabs = rope_tables(pos, HEAD_DIM_A // 4, HEAD_DIM_A) + rope_tables(pos, HEAD_DIM_B // 4, HEAD_DIM_B)
    place = mesh_place()

    def small(li):
        P = {k: w[k][li][None] for k in ("gn_a", "gn_b", "ln_mix_g", "ln_mix_b", "ln_mem_g", "ln_mem_b", "ln_ff_g", "ln_ff_b")}
        P["sinks_l"] = jnp.broadcast_to(w["sinks"][li][:, None], (w["sinks"].shape[1], LANES))
        return P

    rest_names = tuple(k for k in BIG if k != "w_in")
    chain = [(0, ("w_in",)), (0, rest_names)] + [(li, BIG) for li in range(1, depth)]
    casts = [[cast_into_slot(w[k], li, place, name=f"L{li}_cast_{k}") for k in names] for li, names in chain]
    started = {0: gather_start(casts[0], tabs[1], name="G0_gather_start")}

    def land(gi, after):
        send, recv, gs, tok0 = started.pop(gi)
        gs, landed = gather_wait(send, recv, gs, tok0 if after is None else after, name=f"G{gi}_gather_wait")
        token = None
        if gi + 1 < len(chain):
            started[gi + 1] = gather_start(casts[gi + 1], landed, name=f"G{gi + 1}_gather_start")
            token = started[gi + 1][3]
        gs = gather_forward(gs, name=f"G{gi}_gather_fwd")
        return dict(zip(chain[gi][1], gs)), token

    def ordered(a, token):
        return a if token is None else a + token[:1, :1].astype(a.dtype)

    xs, xbs, saved, Ws = x0, cast_bf16(x0, name="x_cast"), [], []
    for li in range(depth):
        gi = 0 if li == 0 else li + 1
        got, token = land(gi, None if li == 0 else xs)
        W = {"w_in": _gathered_view("w_in", got["w_in"])}
        tabs_l = (ordered(tabs[0], token),) + tabs[1:]

        def rest(after, P, li=li, got=got, W=W):
            if li == 0:
                got, token = land(1, after)
                P = dict(P, ln_mix_g=ordered(P["ln_mix_g"], token))
            W.update({k: _gathered_view(k, got[k]) for k in rest_names})
            return W, P

        xs, xbs, sv = _layer_fwd(xs, xbs, memb, W["w_in"], rest, small(li), tabs_l, alpha, li)
        saved.append(sv)
        Ws.append(W)
    dy, loss_part = loss_head(xs, loss_target[0], name="loss")
    loss = lax.psum(0.5 / D * jnp.sum(loss_part), ("x", "y", "c"))

    g_big = {k: lax.empty(w[k].shape, F32) for k in BIG}
    g_small = [None] * depth

    def begin(li, names, gw, tag):
        parts = [_to_shards(k, gw[k]) for k in names]
        r1 = sibling_halves(parts, name=f"L{li}{tag}_rs_sibling")
        qs = [pair_sum(p, r, place, name=f"L{li}_rs_pair_{k}") for k, p, r in zip(names, parts, r1)]
        send, recv, qs, lands, token = chips_start(qs, name=f"L{li}{tag}_rs_chips_start")
        return (li, names, tag, send, recv, qs, lands), token

    joins = []

    def drain_joins(after):
        while joins:
            li, names, tag, send, recv = joins.pop(0)
            bufs = join_wait(send, recv, [g_big[k] for k in names], li, after, name=f"L{li}{tag}_rs_join_wait")
            g_big.update(zip(names, bufs))

    def finish(pending, after):
        li, names, tag, send, recv, qs, lands = pending
        drain_joins(after)
        qs, lands = chips_wait(send, recv, qs, lands, after, name=f"L{li}{tag}_rs_chips_wait")
        fulls = [chip_sum(q, r, place, g_big[k], li, name=f"L{li}_rs_sum_{k}") for k, q, r in zip(names, qs, lands)]
        send, recv, bufs, token = join_start(fulls, li, name=f"L{li}{tag}_rs_join_start")
        g_big.update(zip(names, bufs))
        joins.append((li, names, tag, send, recv))
        return token

    early = ("w_mq", "w_mkv", "w_mo", "w_up", "w_down")
    late = tuple(k for k in BIG if k not in early)
    pendings, token = [], None
    for li in reversed(range(depth)):
        P = small(li)
        P["ln_ff_g"] = ordered(P["ln_ff_g"], token)
        hook = None
        if li == 0:
            def hook(gw, dx1, P):
                toks = [finish(pendings.pop(), dx1) for _ in range(len(pendings))]
                pend, tok = begin(0, early, gw, "a")
                pendings.append(pend)
                return dict(P, ln_mix_g=ordered(P["ln_mix_g"], sum(toks, tok)))
        dy, gw, g_small[li] = _layer_bwd(dy, saved[li], memb, Ws[li], P, tabs, alpha, li, hook)
        toks = [finish(pendings.pop(), dy) for _ in range(len(pendings))]
        pend, token = begin(li, late if li == 0 else BIG, gw, "b" if li == 0 else "")
        token = sum(toks, token)
        pendings.append(pend)
    finish(pendings.pop(), token)
    drain_joins(token)
    grad_x = dy[None]

    rows = []
    for li in range(depth):
        gs = g_small[li]
        for k in ("ln_mix_g", "ln_mix_b", "ln_mem_g", "ln_mem_b", "ln_ff_g", "ln_ff_b"):
            rows.append(jnp.sum(gs[k], axis=0, keepdims=True))
        rows.append(jnp.concatenate([jnp.sum(gs["gn_a"], axis=0, keepdims=True), jnp.sum(gs["gn_b"], axis=0, keepdims=True)], axis=1))
        sk = gs["sinks"][:, 0][None]
        rows.append(jnp.pad(sk, ((0, 0), (0, D - sk.shape[1]))))
    red = allreduce_small(jnp.concatenate(rows, axis=0), name="small_allreduce").reshape(depth, 8, D)
    wa = w["gn_a"].shape[1]
    grads = dict(g_big)
    for j, k in enumerate(("ln_mix_g", "ln_mix_b", "ln_mem_g", "ln_mem_b", "ln_ff_g", "ln_ff_b")):
        grads[k] = red[:, j]
    grads["gn_a"] = red[:, 6, :wa]
    grads["gn_b"] = red[:, 6, wa:]
    grads["sinks"] = red[:, 7, :w["sinks"].shape[1]]

    delta, new_m, new_v = {}, {}, {}
    small_names = [k for k in w if k not in BIG]
    for k in BIG:
        delta[k], new_m[k], new_v[k], grads[k] = adamw(w[k], grads[k], m[k], v[k], name=f"adamw_{k}", emit_g=True)
    pack = lambda d: jnp.concatenate([jnp.pad(d[k], ((0, 0), (0, D - d[k].shape[1]))) for k in small_names], axis=0)
    ds, ms, vs = adamw(pack(w), pack(grads), pack(m), pack(v), name="adamw_small")
    for j, k in enumerate(small_names):
        sl = (slice(j * depth, (j + 1) * depth), slice(0, w[k].shape[1]))
        delta[k], new_m[k], new_v[k] = ds[sl], ms[sl], vs[sl]
    return loss, grad_x, grads, delta, new_m, new_v


WEIGHTS = ("w_in", "gn_a", "gn_b", "sinks", "w_out", "ln_mix_g", "ln_mix_b", "w_mq", "w_mkv", "w_mo",
           "ln_mem_g", "ln_mem_b", "w_up", "w_down", "ln_ff_g", "ln_ff_b")


def kernel(x, mem, positions, w_in, gn_a, gn_b, sinks, w_out, ln_mix_g, ln_mix_b, w_mq, w_mkv, w_mo, ln_mem_g, ln_mem_b, w_up, w_down, ln_ff_g, ln_ff_b, loss_target, m_w_in, m_gn_a, m_gn_b, m_sinks, m_w_out, m_ln_mix_g, m_ln_mix_b, m_w_mq, m_w_mkv, m_w_mo, m_ln_mem_g, m_ln_mem_b, m_w_up, m_w_down, m_ln_ff_g, m_ln_ff_b, v_w_in, v_gn_a, v_gn_b, v_sinks, v_w_out, v_ln_mix_g, v_ln_mix_b, v_w_mq, v_w_mkv, v_w_mo, v_ln_mem_g, v_ln_mem_b, v_w_up, v_w_down, v_ln_ff_g, v_ln_ff_b):
    w = dict(zip(WEIGHTS, (w_in, gn_a, gn_b, sinks, w_out, ln_mix_g, ln_mix_b, w_mq, w_mkv, w_mo, ln_mem_g, ln_mem_b, w_up, w_down, ln_ff_g, ln_ff_b)))
    m = dict(zip(WEIGHTS, (m_w_in, m_gn_a, m_gn_b, m_sinks, m_w_out, m_ln_mix_g, m_ln_mix_b, m_w_mq, m_w_mkv, m_w_mo, m_ln_mem_g, m_ln_mem_b, m_w_up, m_w_down, m_ln_ff_g, m_ln_ff_b)))
    v = dict(zip(WEIGHTS, (v_w_in, v_gn_a, v_gn_b, v_sinks, v_w_out, v_ln_mix_g, v_ln_mix_b, v_w_mq, v_w_mkv, v_w_mo, v_ln_mem_g, v_ln_mem_b, v_w_up, v_w_down, v_ln_ff_g, v_ln_ff_b)))
    loss, grad_x, grads, delta, new_m, new_v = _step(x, mem, positions, loss_target, w, m, v)
    return (loss, grad_x, *[grads[k] for k in WEIGHTS], *[delta[k] for k in WEIGHTS],
            *[new_m[k] for k in WEIGHTS], *[new_v[k] for k in WEIGHTS])
```

```python
import functools

import jax
import jax.numpy as jnp
from jax import lax
from jax.experimental import pallas as pl
from jax.experimental.pallas import tpu as pltpu

F32 = jnp.float32
BF16 = jnp.bfloat16
MESH = pl.DeviceIdType.MESH

HEAD_DIM_A = 128
HEAD_DIM_B = 64
LANES = 128
BLOCK = 128
DILATED_BRANCHES = ((128, 1), (512, 4), (2048, 16))
WINDOW_B = 128
N_MEM_HEADS = 4
ROPE_THETA = 500000.0
LN_EPS = 1e-5
RMS_EPS = 1e-6
NEG_INF = -1e30
ADAM_LR = 0.001
ADAM_B1 = 0.9
ADAM_B2 = 0.999
ADAM_EPS = 1e-08
ADAM_WD = 0.01
ADAM_STEP = 10
N_CHIPS = 4
VMEM_LIMIT = 56 * 1024 * 1024

BIG = ("w_in", "w_out", "w_mq", "w_mkv", "w_mo", "w_up", "w_down")
COL_SHARDED = ("w_in", "w_mkv", "w_up")


def _tile(n, target, mult=LANES):
    best = None
    t = mult
    while t <= min(n, target):
        if n % t == 0:
            best = t
        t += mult
    return best if best is not None else n


def _params(sem=None):
    return pltpu.CompilerParams(dimension_semantics=sem, vmem_limit_bytes=VMEM_LIMIT)


def _dot(a, b):
    return jnp.dot(a, b, preferred_element_type=F32)


def _dot_nt(a, b):
    return lax.dot_general(a, b, (((1,), (1,)), ((), ())), preferred_element_type=F32)


def _dot_tn(a, b):
    return lax.dot_general(a, b, (((0,), (0,)), ((), ())), preferred_element_type=F32)


def mm_nn(a, b3, *, name, out_dtype, relu2=False, tm=1024, tn=1024, tk=2048):
    M, K = a.shape
    nsh, _, nk = b3.shape
    tm, tn, tk = _tile(M, tm, 8), _tile(nk, tn), _tile(K, tk)
    nb, ksteps = nk // tn, K // tk

    def body(a_ref, b_ref, *rest):
        outs, scr = rest[:2 if relu2 else 1], rest[2 if relu2 else 1:]

        def finish(acc):
            if relu2:
                outs[0][...] = acc.astype(outs[0].dtype)
                r = jnp.maximum(acc, 0.0)
                outs[1][...] = (r * r).astype(outs[1].dtype)
            else:
                outs[0][...] = acc.astype(outs[0].dtype)

        if ksteps == 1:
            finish(_dot(a_ref[...], b_ref[...]))
        else:
            acc_ref = scr[0]
            k = pl.program_id(2)

            @pl.when(k == 0)
            def _():
                acc_ref[...] = jnp.zeros_like(acc_ref)

            acc_ref[...] += _dot(a_ref[...], b_ref[...])

            @pl.when(k == ksteps - 1)
            def _():
                finish(acc_ref[...])

    o_spec = pl.BlockSpec((tm, tn), lambda i, j, k: (i, j))
    o_shape = jax.ShapeDtypeStruct((M, nsh * nk), out_dtype)
    return pl.pallas_call(
        body, name=name,
        grid=(M // tm, nsh * nb, ksteps),
        in_specs=[pl.BlockSpec((tm, tk), lambda i, j, k: (i, k)),
                  pl.BlockSpec((None, tk, tn), lambda i, j, k: (j // nb, k, j % nb))],
        out_specs=(o_spec, o_spec) if relu2 else o_spec,
        out_shape=(o_shape, o_shape) if relu2 else o_shape,
        scratch_shapes=[] if ksteps == 1 else [pltpu.VMEM((tm, tn), F32)],
        compiler_params=_params(("parallel", "parallel", "arbitrary")),
    )(a, b3)


def mm_ln(a, w, resid, g, b, *, name, alpha, tm=512, tk=1024):
    M, K = a.shape
    D = w.shape[1]
    tm, tk = _tile(M, tm, 8), _tile(K, tk)
    ksteps = K // tk

    def body(a_ref, w_ref, r_ref, g_ref, b_ref, z_ref, xn_ref, xb_ref, *scr):
        def finish(acc):
            z = alpha * r_ref[...] + acc
            mu = jnp.mean(z, axis=-1, keepdims=True)
            zc = z - mu
            var = jnp.mean(zc * zc, axis=-1, keepdims=True)
            xn = zc * lax.rsqrt(var + LN_EPS) * g_ref[...] + b_ref[...]
            z_ref[...] = z
            xn_ref[...] = xn
            xb_ref[...] = xn.astype(BF16)

        if ksteps == 1:
            finish(_dot(a_ref[...], w_ref[...]))
            return
        acc_ref = scr[0]
        k = pl.program_id(1)

        @pl.when(k == 0)
        def _():
            acc_ref[...] = jnp.zeros_like(acc_ref)

        acc_ref[...] += _dot(a_ref[...], w_ref[...])

        @pl.when(k == ksteps - 1)
        def _():
            finish(acc_ref[...])

    row = pl.BlockSpec((tm, D), lambda i, k: (i, 0))
    vec = pl.BlockSpec((1, D), lambda i, k: (0, 0))
    return pl.pallas_call(
        body, name=name,
        grid=(M // tm, ksteps),
        in_specs=[pl.BlockSpec((tm, tk), lambda i, k: (i, k)),
                  pl.BlockSpec((tk, D), lambda i, k: (k, 0)), row, vec, vec],
        out_specs=(row, row, row),
        out_shape=(jax.ShapeDtypeStruct((M, D), F32), jax.ShapeDtypeStruct((M, D), F32),
                   jax.ShapeDtypeStruct((M, D), BF16)),
        scratch_shapes=[] if ksteps == 1 else [pltpu.VMEM((tm, D), F32)],
        compiler_params=_params(("parallel", "arbitrary")),
    )(a, w, resid, g, b)


def mm_nt(a, b3, *, name, out_dtype, resid=None, alpha=1.0, umul=None, tm=1024, tko=1024, tr=2048):
    M, N = a.shape
    nsh, K, nk = b3.shape
    tm, tko, tr = _tile(M, tm, 8), _tile(K, tko), _tile(nk, tr)
    nb = nk // tr
    rsteps = nsh * nb

    def body(a_ref, b_ref, *rest):
        rest = list(rest)
        r_ref = rest.pop(0) if resid is not None else None
        u_ref = rest.pop(0) if umul is not None else None
        o_ref = rest.pop(0)

        def finish(acc):
            if r_ref is not None:
                acc = acc + alpha * r_ref[...]
            if u_ref is not None:
                acc = acc * (2.0 * jnp.maximum(u_ref[...].astype(F32), 0.0))
            o_ref[...] = acc.astype(o_ref.dtype)

        if rsteps == 1:
            finish(_dot_nt(a_ref[...], b_ref[...]))
        else:
            acc_ref = rest[0]
            r = pl.program_id(2)

            @pl.when(r == 0)
            def _():
                acc_ref[...] = jnp.zeros_like(acc_ref)

            acc_ref[...] += _dot_nt(a_ref[...], b_ref[...])

            @pl.when(r == rsteps - 1)
            def _():
                finish(acc_ref[...])

    o_spec = pl.BlockSpec((tm, tko), lambda i, j, r: (i, j))
    in_specs = [pl.BlockSpec((tm, tr), lambda i, j, r: (i, r)),
                pl.BlockSpec((None, tko, tr), lambda i, j, r: (r // nb, j, r % nb))]
    args = [a, b3]
    for extra in (resid, umul):
        if extra is not None:
            in_specs.append(o_spec)
            args.append(extra)
    return pl.pallas_call(
        body, name=name,
        grid=(M // tm, K // tko, rsteps),
        in_specs=in_specs, out_specs=o_spec,
        out_shape=jax.ShapeDtypeStruct((M, K), out_dtype),
        scratch_shapes=[] if rsteps == 1 else [pltpu.VMEM((tm, tko), F32)],
        compiler_params=_params(("parallel", "parallel", "arbitrary")),
    )(*args)


def mm_tn(a, g, nsh, *, name, tk=1024, tn=1024, tm=2048):
    M, K = a.shape
    N = g.shape[1]
    nk = N // nsh
    tk, tn, tm = _tile(K, tk), _tile(nk, tn), _tile(M, tm, 8)
    nb, msteps = nk // tn, M // tm

    def body(a_ref, g_ref, o_ref, acc_ref):
        m = pl.program_id(2)

        @pl.when(m == 0)
        def _():
            acc_ref[...] = jnp.zeros_like(acc_ref)

        acc_ref[...] += _dot_tn(a_ref[...], g_ref[...])

        @pl.when(m == msteps - 1)
        def _():
            o_ref[...] = acc_ref[...].astype(o_ref.dtype)

    return pl.pallas_call(
        body, name=name,
        grid=(K // tk, nsh * nb, msteps),
        in_specs=[pl.BlockSpec((tm, tk), lambda i, j, m: (m, i)),
                  pl.BlockSpec((tm, tn), lambda i, j, m: (m, j))],
        out_specs=pl.BlockSpec((None, tk, tn), lambda i, j, m: (j // nb, i, j % nb)),
        out_shape=jax.ShapeDtypeStruct((nsh, K, nk), BF16),
        scratch_shapes=[pltpu.VMEM((tk, tn), F32)],
        compiler_params=_params(("parallel", "parallel", "arbitrary")),
    )(a, g)


def _fold8(t):
    return t.reshape(t.shape[0] // 8, 8, t.shape[1]).sum(axis=0)


def ln_bwd(dy, z, g, *, name, tm=256):
    M, D = z.shape
    tm = _tile(M, tm, 8)

    def body(dy_ref, z_ref, g_ref, dz_ref, dzb_ref, dg_ref, db_ref):
        i = pl.program_id(0)
        z_ = z_ref[...]
        dy_ = dy_ref[...]
        mu = jnp.mean(z_, axis=-1, keepdims=True)
        zc = z_ - mu
        var = jnp.mean(zc * zc, axis=-1, keepdims=True)
        rstd = lax.rsqrt(var + LN_EPS)
        xh = zc * rstd
        dxh = dy_ * g_ref[...]
        m1 = jnp.mean(dxh, axis=-1, keepdims=True)
        m2 = jnp.mean(dxh * xh, axis=-1, keepdims=True)
        dz = rstd * (dxh - m1 - xh * m2)
        dz_ref[...] = dz
        dzb_ref[...] = dz.astype(BF16)

        @pl.when(i == 0)
        def _():
            dg_ref[...] = jnp.zeros_like(dg_ref)
            db_ref[...] = jnp.zeros_like(db_ref)

        dg_ref[...] += _fold8(dy_ * xh)
        db_ref[...] += _fold8(dy_)

    row = pl.BlockSpec((tm, D), lambda i: (i, 0))
    acc = pl.BlockSpec((8, D), lambda i: (0, 0))
    return pl.pallas_call(
        body, name=name, grid=(M // tm,),
        in_specs=[row, row, pl.BlockSpec((1, D), lambda i: (0, 0))],
        out_specs=(row, row, acc, acc),
        out_shape=(jax.ShapeDtypeStruct((M, D), F32), jax.ShapeDtypeStruct((M, D), BF16),
                   jax.ShapeDtypeStruct((8, D), F32), jax.ShapeDtypeStruct((8, D), F32)),
        compiler_params=_params(("arbitrary",)),
    )(dy, z, g)


def rms_fwd(ya, yb, ga, gb, *, name, tm=512):
    M, WA = ya.shape
    WB = yb.shape[1]
    tm = _tile(M, tm, 8)

    def body(ya_ref, yb_ref, ga_ref, gb_ref, o_ref):
        for y_ref, g_ref, lo, w in ((ya_ref, ga_ref, 0, WA), (yb_ref, gb_ref, WA, WB)):
            y = y_ref[...]
            r = lax.rsqrt(jnp.mean(y * y, axis=-1, keepdims=True) + RMS_EPS)
            o_ref[:, lo:lo + w] = (y * r * g_ref[...]).astype(o_ref.dtype)

    return pl.pallas_call(
        body, name=name, grid=(M // tm,),
        in_specs=[pl.BlockSpec((tm, WA), lambda i: (i, 0)), pl.BlockSpec((tm, WB), lambda i: (i, 0)),
                  pl.BlockSpec((1, WA), lambda i: (0, 0)), pl.BlockSpec((1, WB), lambda i: (0, 0))],
        out_specs=pl.BlockSpec((tm, WA + WB), lambda i: (i, 0)),
        out_shape=jax.ShapeDtypeStruct((M, WA + WB), BF16),
        compiler_params=_params(("parallel",)),
    )(ya, yb, ga, gb)


def rms_bwd(dy, ya, yb, ga, gb, *, name, tm=512):
    M, WA = ya.shape
    WB = yb.shape[1]
    tm = _tile(M, tm, 8)

    def body(dy_ref, ya_ref, yb_ref, ga_ref, gb_ref, dya_ref, dyb_ref, dga_ref, dgb_ref):
        i = pl.program_id(0)

        @pl.when(i == 0)
        def _():
            dga_ref[...] = jnp.zeros_like(dga_ref)
            dgb_ref[...] = jnp.zeros_like(dgb_ref)

        for y_ref, g_ref, d_ref, dgr, lo, w in ((ya_ref, ga_ref, dya_ref, dga_ref, 0, WA),
                                                (yb_ref, gb_ref, dyb_ref, dgb_ref, WA, WB)):
            y = y_ref[...]
            d = dy_ref[:, lo:lo + w]
            r = lax.rsqrt(jnp.mean(y * y, axis=-1, keepdims=True) + RMS_EPS)
            n = y * r
            dn = d * g_ref[...]
            d_ref[...] = r * (dn - n * jnp.mean(dn * n, axis=-1, keepdims=True))
            dgr[...] += _fold8(d * n)

    return pl.pallas_call(
        body, name=name, grid=(M // tm,),
        in_specs=[pl.BlockSpec((tm, WA + WB), lambda i: (i, 0)),
                  pl.BlockSpec((tm, WA), lambda i: (i, 0)), pl.BlockSpec((tm, WB), lambda i: (i, 0)),
                  pl.BlockSpec((1, WA), lambda i: (0, 0)), pl.BlockSpec((1, WB), lambda i: (0, 0))],
        out_specs=(pl.BlockSpec((tm, WA), lambda i: (i, 0)), pl.BlockSpec((tm, WB), lambda i: (i, 0)),
                   pl.BlockSpec((8, WA), lambda i: (0, 0)), pl.BlockSpec((8, WB), lambda i: (0, 0))),
        out_shape=(jax.ShapeDtypeStruct((M, WA), F32), jax.ShapeDtypeStruct((M, WB), F32),
                   jax.ShapeDtypeStruct((8, WA), F32), jax.ShapeDtypeStruct((8, WB), F32)),
        compiler_params=_params(("arbitrary",)),
    )(dy, ya, yb, ga, gb)


def loss_head(y, target, *, name, tm=512):
    M, D = y.shape
    tm = _tile(M, tm, 8)

    def body(y_ref, t_ref, dy_ref, l_ref):
        i = pl.program_id(0)

        @pl.when(i == 0)
        def _():
            l_ref[...] = jnp.zeros_like(l_ref)

        e = y_ref[...] - t_ref[...]
        dy_ref[...] = e * (1.0 / D)
        l_ref[...] += _fold8(e * e)

    row = pl.BlockSpec((tm, D), lambda i: (i, 0))
    return pl.pallas_call(
        body, name=name, grid=(M // tm,),
        in_specs=[row, row],
        out_specs=(row, pl.BlockSpec((8, D), lambda i: (0, 0))),
        out_shape=(jax.ShapeDtypeStruct((M, D), F32), jax.ShapeDtypeStruct((8, D), F32)),
        compiler_params=_params(("arbitrary",)),
    )(y, target)


def _lane(shape):
    return lax.broadcasted_iota(jnp.int32, shape, len(shape) - 1)


def _swap(t, half, period):
    first = (_lane(t.shape) % period) < half
    return jnp.where(first, pltpu.roll(t, LANES - half, 1), pltpu.roll(t, half, 1))


def _rope(t, c, s, half, period):
    return t * c + _swap(t, half, period) * s


def _rope_t(g, c, s, half, period):
    return g * c - _swap(g, half, period) * s


def rope_tables(positions, rot_dim, period):
    half = rot_dim // 2
    inv_freq = ROPE_THETA ** (-jnp.arange(0, rot_dim, 2, dtype=F32) / rot_dim)
    ang = positions.astype(F32)[:, None] * inv_freq
    cos, sin = jnp.cos(ang), jnp.sin(ang)
    ones = jnp.ones((positions.shape[0], period - rot_dim), F32)
    c = jnp.concatenate([cos, cos, ones], axis=1)
    s = jnp.concatenate([-sin, sin, 0.0 * ones], axis=1)
    reps = LANES // period
    return jnp.tile(c, (1, reps)), jnp.tile(s, (1, reps))


def _branch_blocks(S):
    out = []
    for window, d in DILATED_BRANCHES:
        assert window // d == BLOCK and S % (d * BLOCK) == 0
        out.append((d, (S // d) // BLOCK))
    return out


def _rows(r, n, d):
    return pl.ds(r + n * (BLOCK * d), BLOCK, stride=d) if d > 1 else pl.ds(pl.multiple_of(n * BLOCK, BLOCK), BLOCK)


def _band_masks(n, strict_prev):
    qi = lax.broadcasted_iota(jnp.int32, (BLOCK, BLOCK), 0)
    kj = lax.broadcasted_iota(jnp.int32, (BLOCK, BLOCK), 1)
    cur = kj <= qi
    prev = ((kj > qi) if strict_prev else (kj >= qi)) & (n > 0)
    return cur, prev


def attn_a_fwd(h, ca, sa, *, n_heads, name):
    S = h.shape[0]
    scale = HEAD_DIM_A ** -0.5
    half = HEAD_DIM_A // 8
    branches = _branch_blocks(S)

    def body(q_ref, k_ref, v_ref, c_ref, s_ref, y_ref, lse_ref, qs, ks, m_s, l_s, acc_s):
        qs[...] = _rope(q_ref[...], c_ref[...], s_ref[...], half, HEAD_DIM_A)
        ks[...] = _rope(k_ref[...], c_ref[...], s_ref[...], half, HEAD_DIM_A)
        for bi, (d, nb) in enumerate(branches):
            def blk(idx, carry, bi=bi, d=d, nb=nb):
                r, n = idx // nb, idx % nb
                rc, rp = _rows(r, n, d), _rows(r, jnp.maximum(n - 1, 0), d)
                q = qs[rc, :].astype(BF16)
                cur, prev = _band_masks(n, False)
                sc = jnp.where(cur, _dot_nt(q, ks[rc, :].astype(BF16)) * scale, NEG_INF)
                sp = jnp.where(prev, _dot_nt(q, ks[rp, :].astype(BF16)) * scale, NEG_INF)
                m = jnp.maximum(jnp.max(sc, axis=-1, keepdims=True), jnp.max(sp, axis=-1, keepdims=True))
                pc, pp = jnp.exp(sc - m), jnp.exp(sp - m)
                l = jnp.sum(pc, axis=-1, keepdims=True) + jnp.sum(pp, axis=-1, keepdims=True)
                acc = _dot(pc.astype(BF16), v_ref[rc, :].astype(BF16)) + _dot(pp.astype(BF16), v_ref[rp, :].astype(BF16))
                mb = jnp.broadcast_to(m, (BLOCK, LANES))
                lb = jnp.broadcast_to(l, (BLOCK, LANES))
                if bi == 0:
                    m_s[rc, :], l_s[rc, :], acc_s[rc, :] = mb, lb, acc
                else:
                    m0 = m_s[rc, :]
                    mn = jnp.maximum(m0, mb)
                    a0, a1 = jnp.exp(m0 - mn), jnp.exp(mb - mn)
                    m_s[rc, :] = mn
                    l_s[rc, :] = l_s[rc, :] * a0 + lb * a1
                    acc_s[rc, :] = acc_s[rc, :] * a0 + acc * a1
                return carry

            lax.fori_loop(0, d * nb, blk, 0)
        y_ref[...] = acc_s[...] / l_s[...]
        lse_ref[...] = m_s[...] + jnp.log(l_s[...])

    col = lambda off: pl.BlockSpec((S, LANES), lambda hd: (0, off + hd))
    full = pl.BlockSpec((S, LANES), lambda hd: (0, 0))
    out = pl.BlockSpec((S, LANES), lambda hd: (0, hd))
    o_shape = jax.ShapeDtypeStruct((S, n_heads * LANES), F32)
    return pl.pallas_call(
        body, name=name, grid=(n_heads,),
        in_specs=[col(0), col(n_heads), col(2 * n_heads), full, full],
        out_specs=(out, out), out_shape=(o_shape, o_shape),
        scratch_shapes=[pltpu.VMEM((S, LANES), F32) for _ in range(5)],
        compiler_params=_params(("arbitrary",)),
    )(h, h, h, ca, sa)


def attn_a_bwd(h, ca, sa, ya, lse, dya, *, n_heads, name):
    S = h.shape[0]
    scale = HEAD_DIM_A ** -0.5
    half = HEAD_DIM_A // 8
    branches = _branch_blocks(S)

    def body(q_ref, k_ref, v_ref, c_ref, s_ref, y_ref, lse_ref, dy_ref, o_ref, qs, ks, dq_s, dk_s, dv_s):
        part = pl.program_id(1)

        @pl.when(part == 0)
        def _():
            qs[...] = _rope(q_ref[...], c_ref[...], s_ref[...], half, HEAD_DIM_A)
            ks[...] = _rope(k_ref[...], c_ref[...], s_ref[...], half, HEAD_DIM_A)
            dq_s[...] = jnp.zeros_like(dq_s)
            dk_s[...] = jnp.zeros_like(dk_s)
            dv_s[...] = jnp.zeros_like(dv_s)
            for d, nb in branches:
                def blk(idx, carry, d=d, nb=nb):
                    r, n = idx // nb, idx % nb
                    rc, rp = _rows(r, n, d), _rows(r, jnp.maximum(n - 1, 0), d)
                    q = qs[rc, :].astype(BF16)
                    dy = dy_ref[rc, :]
                    dsum = jnp.sum(dy * y_ref[rc, :], axis=-1, keepdims=True)
                    dyb = dy.astype(BF16)
                    lse_b = lse_ref[rc, :]
                    cur, prev = _band_masks(n, False)
                    dq = jnp.zeros((BLOCK, LANES), F32)
                    for rows, mask in ((rc, cur), (rp, prev)):
                        kb = ks[rows, :].astype(BF16)
                        vb = v_ref[rows, :].astype(BF16)
                        s = jnp.where(mask, _dot_nt(q, kb) * scale, NEG_INF)
                        p = jnp.exp(s - lse_b)
                        ds = (p * (_dot_nt(dyb, vb) - dsum) * scale).astype(BF16)
                        dv_s[rows, :] += _dot_tn(p.astype(BF16), dyb)
                        dk_s[rows, :] += _dot_tn(ds, q)
                        dq = dq + _dot(ds, kb)
                    dq_s[rc, :] += dq
                    return carry

                lax.fori_loop(0, d * nb, blk, 0)
            o_ref[...] = _rope_t(dq_s[...], c_ref[...], s_ref[...], half, HEAD_DIM_A).astype(o_ref.dtype)

        @pl.when(part == 1)
        def _():
            o_ref[...] = _rope_t(dk_s[...], c_ref[...], s_ref[...], half, HEAD_DIM_A).astype(o_ref.dtype)

        @pl.when(part == 2)
        def _():
            o_ref[...] = dv_s[...].astype(o_ref.dtype)

    col = lambda off: pl.BlockSpec((S, LANES), lambda hd, p: (0, off + hd))
    full = pl.BlockSpec((S, LANES), lambda hd, p: (0, 0))
    per_head = pl.BlockSpec((S, LANES), lambda hd, p: (0, hd))
    return pl.pallas_call(
        body, name=name, grid=(n_heads, 3),
        in_specs=[col(0), col(n_heads), col(2 * n_heads), full, full, per_head, per_head, per_head],
        out_specs=pl.BlockSpec((S, LANES), lambda hd, p: (0, p * n_heads + hd)),
        out_shape=jax.ShapeDtypeStruct((S, 3 * n_heads * LANES), BF16),
        scratch_shapes=[pltpu.VMEM((S, LANES), F32) for _ in range(5)],
        compiler_params=_params(("arbitrary", "arbitrary")),
    )(h, h, h, ca, sa, ya, lse, dya)


def _both_halves(t, g):
    low = _lane(t.shape) < HEAD_DIM_B
    return jnp.where(low == (g == 0), t, pltpu.roll(t, HEAD_DIM_B, 1))


def _stack_heads(t):
    low = _lane(t.shape) < HEAD_DIM_B
    return jnp.concatenate([jnp.where(low, t, 0.0), jnp.where(low, 0.0, t)], axis=0)


def _unstack_heads(t2):
    low = _lane((BLOCK, LANES)) < HEAD_DIM_B
    return jnp.where(low, t2[:BLOCK], t2[BLOCK:])


def _fold_halves(t):
    return t + pltpu.roll(t, HEAD_DIM_B, 1)


def _head_rows(ref, tile):
    a = ref[pl.ds(2 * tile, 1), :][:, :1]
    b = ref[pl.ds(2 * tile + 1, 1), :][:, :1]
    return jnp.concatenate([jnp.broadcast_to(a, (BLOCK, 1)), jnp.broadcast_to(b, (BLOCK, 1))], axis=0)


def attn_b_fwd(h, cb, sb, sinks_l, *, off_q, n_qtiles, name):
    S = h.shape[0]
    nblk = S // BLOCK
    scale = HEAD_DIM_B ** -0.5
    half = HEAD_DIM_B // 8
    tiles_per_group = n_qtiles // 2

    def body(q_ref, k_ref, v_ref, c_ref, s_ref, sink_ref, y_ref, lse_ref, qs, kg, vg):
        t = pl.program_id(0)
        g = t // tiles_per_group
        qs[...] = _rope(q_ref[...], c_ref[...], s_ref[...], half, HEAD_DIM_B)
        kg[...] = _both_halves(_rope(k_ref[...], c_ref[...], s_ref[...], half, HEAD_DIM_B), g)
        vg[...] = _both_halves(v_ref[...], g)
        sink = _head_rows(sink_ref, t)

        def blk(n, carry):
            rc = pl.ds(pl.multiple_of(n * BLOCK, BLOCK), BLOCK)
            rp = pl.ds(pl.multiple_of(jnp.maximum(n - 1, 0) * BLOCK, BLOCK), BLOCK)
            q2 = _stack_heads(qs[rc, :]).astype(BF16)
            cur, prev = _band_masks(n, True)
            cur2, prev2 = jnp.concatenate([cur, cur], 0), jnp.concatenate([prev, prev], 0)
            sc = jnp.where(cur2, _dot_nt(q2, kg[rc, :].astype(BF16)) * scale, NEG_INF)
            sp = jnp.where(prev2, _dot_nt(q2, kg[rp, :].astype(BF16)) * scale, NEG_INF)
            m = jnp.maximum(jnp.max(sc, axis=-1, keepdims=True), jnp.max(sp, axis=-1, keepdims=True))
            pc, pp = jnp.exp(sc - m), jnp.exp(sp - m)
            l = jnp.sum(pc, axis=-1, keepdims=True) + jnp.sum(pp, axis=-1, keepdims=True)
            acc = _dot(pc.astype(BF16), vg[rc, :].astype(BF16)) + _dot(pp.astype(BF16), vg[rp, :].astype(BF16))
            m2 = jnp.maximum(m, sink)
            c = jnp.exp(m - m2)
            den = l * c + jnp.exp(sink - m2)
            y_ref[rc, :] = _unstack_heads(acc * (c / den))
            lse_ref[rc, :] = _unstack_heads(jnp.broadcast_to(m2 + jnp.log(den), (2 * BLOCK, LANES)))
            return carry

        lax.fori_loop(0, nblk, blk, 0)

    full = lambda col: pl.BlockSpec((S, LANES), lambda t: (0, col))
    out = pl.BlockSpec((S, LANES), lambda t: (0, t))
    o_shape = jax.ShapeDtypeStruct((S, n_qtiles * LANES), F32)
    return pl.pallas_call(
        body, name=name, grid=(n_qtiles,),
        in_specs=[pl.BlockSpec((S, LANES), lambda t: (0, off_q + t)), full(off_q + n_qtiles), full(off_q + n_qtiles + 1),
                  full(0), full(0), pl.BlockSpec(sinks_l.shape, lambda t: (0, 0))],
        out_specs=(out, out), out_shape=(o_shape, o_shape),
        scratch_shapes=[pltpu.VMEM((S, LANES), F32) for _ in range(3)],
        compiler_params=_params(("arbitrary",)),
    )(h, h, h, cb, sb, sinks_l)


def attn_b_bwd(h, cb, sb, sinks_l, yb, lse, dyb, *, off_q, n_qtiles, name):
    S = h.shape[0]
    nblk = S // BLOCK
    scale = HEAD_DIM_B ** -0.5
    half = HEAD_DIM_B // 8
    tiles_per_group = n_qtiles // 2
    n_steps = n_qtiles + 2

    def body(q_ref, k_ref, v_ref, c_ref, s_ref, sink_ref, y_ref, lse_ref, dy_ref,
             dq_ref, dkv_ref, dsink_ref, qs, kg, vg, dk_acc, dv_acc):
        t = pl.program_id(0)

        @pl.when(t == 0)
        def _():
            dk_acc[...] = jnp.zeros_like(dk_acc)
            dv_acc[...] = jnp.zeros_like(dv_acc)
            dsink_ref[...] = jnp.zeros_like(dsink_ref)

        @pl.when(t < n_qtiles)
        def _():
            g = t // tiles_per_group
            qs[...] = _rope(q_ref[...], c_ref[...], s_ref[...], half, HEAD_DIM_B)
            kg[...] = _both_halves(_rope(k_ref[...], c_ref[...], s_ref[...], half, HEAD_DIM_B), g)
            vg[...] = _both_halves(v_ref[...], g)
            sink = _head_rows(sink_ref, t)

            def blk(n, dsink):
                rc = pl.ds(pl.multiple_of(n * BLOCK, BLOCK), BLOCK)
                rp = pl.ds(pl.multiple_of(jnp.maximum(n - 1, 0) * BLOCK, BLOCK), BLOCK)
                q2 = _stack_heads(qs[rc, :]).astype(BF16)
                dy2 = _stack_heads(dy_ref[rc, :])
                dsum = jnp.sum(dy2 * _stack_heads(y_ref[rc, :]), axis=-1, keepdims=True)
                dy2b = dy2.astype(BF16)
                lse_t = lse_ref[rc, :]
                lse2 = jnp.concatenate([lse_t[:, :1], lse_t[:, HEAD_DIM_B:HEAD_DIM_B + 1]], axis=0)
                cur, prev = _band_masks(n, True)
                dq2 = jnp.zeros((2 * BLOCK, LANES), F32)
                for rows, mask in ((rc, cur), (rp, prev)):
                    kb = kg[rows, :].astype(BF16)
                    vb = vg[rows, :].astype(BF16)
                    mask2 = jnp.concatenate([mask, mask], 0)
                    s = jnp.where(mask2, _dot_nt(q2, kb) * scale, NEG_INF)
                    p = jnp.exp(s - lse2)
                    ds = (p * (_dot_nt(dy2b, vb) - dsum) * scale).astype(BF16)
                    dv_acc[g, rows, :] += _fold_halves(_dot_tn(p.astype(BF16), dy2b))
                    dk_acc[g, rows, :] += _fold_halves(_dot_tn(ds, q2))
                    dq2 = dq2 + _dot(ds, kb)
                dq_ref[rc, :] = _unstack_heads(dq2)
                return dsink - jnp.exp(sink - lse2) * dsum

            dsink = lax.fori_loop(0, nblk, blk, jnp.zeros((2 * BLOCK, 1), F32))
            dq_ref[...] = _rope_t(dq_ref[...], c_ref[...], s_ref[...], half, HEAD_DIM_B)
            d0 = jnp.sum(dsink[:BLOCK], axis=0, keepdims=True)
            d1 = jnp.sum(dsink[BLOCK:], axis=0, keepdims=True)
            dsink_ref[pl.ds(2 * t, 1), :] = jnp.broadcast_to(d0, (1, LANES))
            dsink_ref[pl.ds(2 * t + 1, 1), :] = jnp.broadcast_to(d1, (1, LANES))

        low = _lane((S, LANES)) < HEAD_DIM_B

        @pl.when(t == n_qtiles)
        def _():
            dk = jnp.where(low, dk_acc[0], dk_acc[1])
            dkv_ref[...] = _rope_t(dk, c_ref[...], s_ref[...], half, HEAD_DIM_B)

        @pl.when(t == n_qtiles + 1)
        def _():
            dkv_ref[...] = jnp.where(low, dv_acc[0], dv_acc[1])

    qt = lambda t: jnp.minimum(t, n_qtiles - 1)
    full = lambda col: pl.BlockSpec((S, LANES), lambda t: (0, col))
    per_tile = pl.BlockSpec((S, LANES), lambda t: (0, qt(t)))
    return pl.pallas_call(
        body, name=name, grid=(n_steps,),
        in_specs=[pl.BlockSpec((S, LANES), lambda t: (0, off_q + qt(t))), full(off_q + n_qtiles),
                  full(off_q + n_qtiles + 1), full(0), full(0), pl.BlockSpec(sinks_l.shape, lambda t: (0, 0)),
                  per_tile, per_tile, per_tile],
        out_specs=(per_tile, pl.BlockSpec((S, LANES), lambda t: (0, jnp.maximum(t - n_qtiles, 0))),
                   pl.BlockSpec(sinks_l.shape, lambda t: (0, 0))),
        out_shape=(jax.ShapeDtypeStruct((S, n_qtiles * LANES), F32), jax.ShapeDtypeStruct((S, 2 * LANES), F32),
                   jax.ShapeDtypeStruct(sinks_l.shape, F32)),
        scratch_shapes=[pltpu.VMEM((S, LANES), F32) for _ in range(3)]
        + [pltpu.VMEM((2, S, LANES), F32), pltpu.VMEM((2, S, LANES), F32)],
        compiler_params=_params(("arbitrary",)),
    )(h, h, h, cb, sb, sinks_l, yb, lse, dyb)


def _win(r, nw, d):
    if d > 1:
        return pl.ds(r + nw * (BLOCK * d), 2 * BLOCK, stride=d)
    return pl.ds(pl.multiple_of(nw * BLOCK, BLOCK), 2 * BLOCK)


def _fwd_bias(strict_prev):
    qi = lax.broadcasted_iota(jnp.int32, (BLOCK, 2 * BLOCK), 0)
    kj = lax.broadcasted_iota(jnp.int32, (BLOCK, 2 * BLOCK), 1)
    first = kj < BLOCK
    kk = jnp.where(first, kj, kj - BLOCK)
    prev_ok = (kk > qi) if strict_prev else (kk >= qi)
    zero = first & (kk <= qi)
    mid = (first & prev_ok) | (jnp.logical_not(first) & (kk <= qi))
    return jnp.stack([jnp.where(zero, 0.0, NEG_INF), jnp.where(mid, 0.0, NEG_INF)])


def _bwd_bias(strict_prev):
    qi = lax.broadcasted_iota(jnp.int32, (2 * BLOCK, BLOCK), 0)
    kj = lax.broadcasted_iota(jnp.int32, (2 * BLOCK, BLOCK), 1)
    first = qi < BLOCK
    qq = jnp.where(first, qi, qi - BLOCK)
    prev_ok = (kj > qq) if strict_prev else (kj >= qq)
    mid = (first & (kj <= qq)) | (jnp.logical_not(first) & prev_ok)
    last = jnp.logical_not(first) & (kj <= qq)
    return jnp.stack([jnp.where(mid, 0.0, NEG_INF), jnp.where(last, 0.0, NEG_INF)])


def _pair_bias():
    qi = lax.broadcasted_iota(jnp.int32, (2 * BLOCK, 2 * BLOCK), 0)
    kj = lax.broadcasted_iota(jnp.int32, (2 * BLOCK, 2 * BLOCK), 1)
    return jnp.where((kj <= qi) & (qi - kj <= BLOCK), 0.0, NEG_INF)


UNROLL_PAIR = 2


UNROLL = 4


def attn_a_fwd(h, ca, sa, *, n_heads, name):
    S = h.shape[0]
    scale = HEAD_DIM_A ** -0.5
    half = HEAD_DIM_A // 8
    branches = _branch_blocks(S)

    def body(q_ref, k_ref, v_ref, c_ref, s_ref, y_ref, lse_ref, qs, ks, m_s, l_s, acc_s, bias, pbias):
        qs[...] = _rope(q_ref[...], c_ref[...], s_ref[...], half, HEAD_DIM_A)
        ks[...] = _rope(k_ref[...], c_ref[...], s_ref[...], half, HEAD_DIM_A)
        bias[...] = _fwd_bias(False)
        pbias[...] = _pair_bias()
        for bi, (d, nb) in enumerate(branches):
            assert nb >= 2 and (d * nb) % UNROLL == 0
            if nb == 2 and bi > 0:
                assert d % UNROLL_PAIR == 0

                def pair(it, carry, d=d):
                    rws = [_win(it * UNROLL_PAIR + u, 0, d) for u in range(UNROLL_PAIR)]
                    ss = [_dot_nt(qs[rw, :].astype(BF16), ks[rw, :].astype(BF16)) * scale + pbias[...] for rw in rws]
                    ms = [jnp.max(s, axis=-1, keepdims=True) for s in ss]
                    ps = [jnp.exp(s - m) for s, m in zip(ss, ms)]
                    ls = [jnp.sum(p, axis=-1, keepdims=True) for p in ps]
                    accs = [_dot(p.astype(BF16), v_ref[rw, :].astype(BF16)) for p, rw in zip(ps, rws)]
                    for rw, m, l, acc in zip(rws, ms, ls, accs):
                        mb = jnp.broadcast_to(m, (2 * BLOCK, LANES))
                        m0 = m_s[rw, :]
                        mn = jnp.maximum(m0, mb)
                        a0, a1 = jnp.exp(m0 - mn), jnp.exp(mb - mn)
                        m_s[rw, :] = mn
                        l_s[rw, :] = l_s[rw, :] * a0 + jnp.broadcast_to(l, (2 * BLOCK, LANES)) * a1
                        acc_s[rw, :] = acc_s[rw, :] * a0 + acc * a1
                    return carry

                lax.fori_loop(0, d // UNROLL_PAIR, pair, 0)
                continue

            def blk(it, carry, bi=bi, d=d, nb=nb):
                rn = [((it * UNROLL + u) // nb, (it * UNROLL + u) % nb) for u in range(UNROLL)]
                rcs = [_rows(r, n, d) for r, n in rn]
                rws = [_win(r, jnp.maximum(n - 1, 0), d) for r, n in rn]
                ss = [_dot_nt(qs[rc, :].astype(BF16), ks[rw, :].astype(BF16)) * scale + bias[jnp.minimum(n, 1)]
                      for (r, n), rc, rw in zip(rn, rcs, rws)]
                ms = [jnp.max(s, axis=-1, keepdims=True) for s in ss]
                ps = [jnp.exp(s - m) for s, m in zip(ss, ms)]
                ls = [jnp.sum(p, axis=-1, keepdims=True) for p in ps]
                accs = [_dot(p.astype(BF16), v_ref[rw, :].astype(BF16)) for p, rw in zip(ps, rws)]
                for rc, m, l, acc in zip(rcs, ms, ls, accs):
                    mb = jnp.broadcast_to(m, (BLOCK, LANES))
                    lb = jnp.broadcast_to(l, (BLOCK, LANES))
                    if bi == 0:
                        m_s[rc, :], l_s[rc, :], acc_s[rc, :] = mb, lb, acc
                    else:
                        m0 = m_s[rc, :]
                        mn = jnp.maximum(m0, mb)
                        a0, a1 = jnp.exp(m0 - mn), jnp.exp(mb - mn)
                        m_s[rc, :] = mn
                        l_s[rc, :] = l_s[rc, :] * a0 + lb * a1
                        acc_s[rc, :] = acc_s[rc, :] * a0 + acc * a1
                return carry

            lax.fori_loop(0, d * nb // UNROLL, blk, 0)
        y_ref[...] = acc_s[...] / l_s[...]
        lse_ref[...] = m_s[...] + jnp.log(l_s[...])

    col = lambda off: pl.BlockSpec((S, LANES), lambda hd: (0, off + hd))
    full = pl.BlockSpec((S, LANES), lambda hd: (0, 0))
    out = pl.BlockSpec((S, LANES), lambda hd: (0, hd))
    o_shape = jax.ShapeDtypeStruct((S, n_heads * LANES), F32)
    return pl.pallas_call(
        body, name=name, grid=(n_heads,),
        in_specs=[col(0), col(n_heads), col(2 * n_heads), full, full],
        out_specs=(out, out), out_shape=(o_shape, o_shape),
        scratch_shapes=[pltpu.VMEM((S, LANES), F32) for _ in range(5)]
        + [pltpu.VMEM((2, BLOCK, 2 * BLOCK), F32), pltpu.VMEM((2 * BLOCK, 2 * BLOCK), F32)],
        compiler_params=_params(("arbitrary",)),
    )(h, h, h, ca, sa)


def attn_a_bwd(h, ca, sa, ya, lse, dya, *, n_heads, name):
    S = h.shape[0]
    scale = HEAD_DIM_A ** -0.5
    half = HEAD_DIM_A // 8
    branches = _branch_blocks(S)

    def body(q_ref, k_ref, v_ref, c_ref, s_ref, y_ref, lse_ref, dy_ref, o_ref, qs, ks, dq_s, dk_s, dv_s, bias, pbias):
        part = pl.program_id(1)

        @pl.when(part == 0)
        def _():
            qs[...] = _rope(q_ref[...], c_ref[...], s_ref[...], half, HEAD_DIM_A)
            ks[...] = _rope(k_ref[...], c_ref[...], s_ref[...], half, HEAD_DIM_A)
            dq_s[...] = jnp.zeros_like(dq_s)
            bias[...] = _bwd_bias(False)
            pbias[...] = _pair_bias()
            for bi, (d, nb) in enumerate(branches):
                assert nb >= 2 and (d * nb) % UNROLL == 0
                if nb == 2 and bi > 0:
                    assert d % UNROLL_PAIR == 0

                    def pair(it, carry, d=d):
                        rws = [_win(it * UNROLL_PAIR + u, 0, d) for u in range(UNROLL_PAIR)]
                        q2 = [qs[rw, :].astype(BF16) for rw in rws]
                        k2 = [ks[rw, :].astype(BF16) for rw in rws]
                        dy2 = [dy_ref[rw, :] for rw in rws]
                        dsum = [jnp.sum(dy * y_ref[rw, :], axis=-1, keepdims=True) for dy, rw in zip(dy2, rws)]
                        dy2b = [dy.astype(BF16) for dy in dy2]
                        ss = [_dot_nt(q, k) * scale + pbias[...] for q, k in zip(q2, k2)]
                        dps = [_dot_nt(dy, v_ref[rw, :].astype(BF16)) for dy, rw in zip(dy2b, rws)]
                        ps = [jnp.exp(s - lse_ref[rw, :][:, :1]) for s, rw in zip(ss, rws)]
                        dss = [(p * (dp - dm) * scale).astype(BF16) for p, dp, dm in zip(ps, dps, dsum)]
                        dvs = [_dot_tn(p.astype(BF16), dy) for p, dy in zip(ps, dy2b)]
                        dks = [_dot_tn(ds, q) for ds, q in zip(dss, q2)]
                        dqs = [_dot(ds, k) for ds, k in zip(dss, k2)]
                        for rw, dv, dk, dq in zip(rws, dvs, dks, dqs):
                            dv_s[rw, :] += dv
                            dk_s[rw, :] += dk
                            dq_s[rw, :] += dq
                        return carry

                    lax.fori_loop(0, d // UNROLL_PAIR, pair, 0)
                    continue

                def blk(it, carry, bi=bi, d=d, nb=nb):
                    rj = [((it * UNROLL + u) // nb, (it * UNROLL + u) % nb) for u in range(UNROLL)]
                    rks = [_rows(r, j, d) for r, j in rj]
                    rws = [_win(r, jnp.minimum(j, nb - 2), d) for r, j in rj]
                    q2 = [qs[rw, :].astype(BF16) for rw in rws]
                    kb = [ks[rk, :].astype(BF16) for rk in rks]
                    dy2 = [dy_ref[rw, :] for rw in rws]
                    dsum = [jnp.sum(dy * y_ref[rw, :], axis=-1, keepdims=True) for dy, rw in zip(dy2, rws)]
                    dy2b = [dy.astype(BF16) for dy in dy2]
                    ss = [_dot_nt(q, k) * scale + bias[(j == nb - 1).astype(jnp.int32)] for q, k, (r, j) in zip(q2, kb, rj)]
                    dps = [_dot_nt(dy, v_ref[rk, :].astype(BF16)) for dy, rk in zip(dy2b, rks)]
                    ps = [jnp.exp(s - lse_ref[rw, :]) for s, rw in zip(ss, rws)]
                    dss = [(p * (dp - dm) * scale).astype(BF16) for p, dp, dm in zip(ps, dps, dsum)]
                    dvs = [_dot_tn(p.astype(BF16), dy) for p, dy in zip(ps, dy2b)]
                    dks = [_dot_tn(ds, q) for ds, q in zip(dss, q2)]
                    dqs = [_dot(ds, k) for ds, k in zip(dss, kb)]
                    for rk, rw, dv, dk, dq in zip(rks, rws, dvs, dks, dqs):
                        if bi == 0:
                            dv_s[rk, :], dk_s[rk, :] = dv, dk
                        else:
                            dv_s[rk, :] += dv
                            dk_s[rk, :] += dk
                        dq_s[rw, :] += dq
                    return carry

                lax.fori_loop(0, d * nb // UNROLL, blk, 0)
            o_ref[...] = _rope_t(dq_s[...], c_ref[...], s_ref[...], half, HEAD_DIM_A).astype(o_ref.dtype)

        @pl.when(part == 1)
        def _():
            o_ref[...] = _rope_t(dk_s[...], c_ref[...], s_ref[...], half, HEAD_DIM_A).astype(o_ref.dtype)

        @pl.when(part == 2)
        def _():
            o_ref[...] = dv_s[...].astype(o_ref.dtype)

    col = lambda off: pl.BlockSpec((S, LANES), lambda hd, p: (0, off + hd))
    full = pl.BlockSpec((S, LANES), lambda hd, p: (0, 0))
    per_head = pl.BlockSpec((S, LANES), lambda hd, p: (0, hd))
    return pl.pallas_call(
        body, name=name, grid=(n_heads, 3),
        in_specs=[col(0), col(n_heads), col(2 * n_heads), full, full, per_head, per_head, per_head],
        out_specs=pl.BlockSpec((S, LANES), lambda hd, p: (0, p * n_heads + hd)),
        out_shape=jax.ShapeDtypeStruct((S, h.shape[1]), BF16),
        scratch_shapes=[pltpu.VMEM((S, LANES), F32) for _ in range(5)]
        + [pltpu.VMEM((2, 2 * BLOCK, BLOCK), F32), pltpu.VMEM((2 * BLOCK, 2 * BLOCK), F32)],
        compiler_params=_params(("arbitrary", "arbitrary")),
    )(h, h, h, ca, sa, ya, lse, dya)


def _unstack_heads(t2):
    rows = t2.shape[0] // 2
    low = _lane((rows, LANES)) < HEAD_DIM_B
    return jnp.where(low, t2[:rows], t2[rows:])


def _head_rows(ref, tile, rows):
    a = ref[pl.ds(2 * tile, 1), :][:, :1]
    b = ref[pl.ds(2 * tile + 1, 1), :][:, :1]
    return jnp.concatenate([jnp.broadcast_to(a, (rows, 1)), jnp.broadcast_to(b, (rows, 1))], axis=0)


UNROLL_B = 2
UNROLL_B_BWD = 4


def attn_b_fwd(h, cb, sb, sinks_l, *, off_q, n_qtiles, name):
    S = h.shape[0]
    nblk = S // BLOCK
    scale = HEAD_DIM_B ** -0.5
    half = HEAD_DIM_B // 8
    tiles_per_group = n_qtiles // 2
    assert nblk >= 2 and nblk % UNROLL_B == 0

    def body(q_ref, k_ref, v_ref, c_ref, s_ref, sink_ref, y_ref, lse_ref, qs, kg, vg, bias):
        t = pl.program_id(0)
        g = t // tiles_per_group
        qs[...] = _rope(q_ref[...], c_ref[...], s_ref[...], half, HEAD_DIM_B)

        @pl.when(t % tiles_per_group == 0)
        def _():
            kg[...] = _both_halves(_rope(k_ref[...], c_ref[...], s_ref[...], half, HEAD_DIM_B), g)
            vg[...] = _both_halves(v_ref[...], g)

        @pl.when(t == 0)
        def _():
            fb = _fwd_bias(True)
            bias[...] = jnp.concatenate([fb, fb], axis=1)

        sink = _head_rows(sink_ref, t, BLOCK)

        def blk(it, carry):
            ns = [it * UNROLL_B + u for u in range(UNROLL_B)]
            rcs = [pl.ds(pl.multiple_of(n * BLOCK, BLOCK), BLOCK) for n in ns]
            rws = [pl.ds(pl.multiple_of(jnp.maximum(n - 1, 0) * BLOCK, BLOCK), 2 * BLOCK) for n in ns]
            ss = [_dot_nt(_stack_heads(qs[rc, :]).astype(BF16), kg[rw, :].astype(BF16)) * scale + bias[jnp.minimum(n, 1)]
                  for n, rc, rw in zip(ns, rcs, rws)]
            ms = [jnp.max(s, axis=-1, keepdims=True) for s in ss]
            ps = [jnp.exp(s - m) for s, m in zip(ss, ms)]
            ls = [jnp.sum(p, axis=-1, keepdims=True) for p in ps]
            accs = [_dot(p.astype(BF16), vg[rw, :].astype(BF16)) for p, rw in zip(ps, rws)]
            for rc, m, l, acc in zip(rcs, ms, ls, accs):
                m2 = jnp.maximum(m, sink)
                c = jnp.exp(m - m2)
                den = l * c + jnp.exp(sink - m2)
                y_ref[rc, :] = _unstack_heads(acc * (c / den))
                lse_ref[rc, :] = _unstack_heads(jnp.broadcast_to(m2 + jnp.log(den), (2 * BLOCK, LANES)))
            return carry

        lax.fori_loop(0, nblk // UNROLL_B, blk, 0)

    full = lambda col: pl.BlockSpec((S, LANES), lambda t: (0, col))
    out = pl.BlockSpec((S, LANES), lambda t: (0, t))
    o_shape = jax.ShapeDtypeStruct((S, n_qtiles * LANES), F32)
    return pl.pallas_call(
        body, name=name, grid=(n_qtiles,),
        in_specs=[pl.BlockSpec((S, LANES), lambda t: (0, off_q + t)), full(off_q + n_qtiles), full(off_q + n_qtiles + 1),
                  full(0), full(0), pl.BlockSpec(sinks_l.shape, lambda t: (0, 0))],
        out_specs=(out, out), out_shape=(o_shape, o_shape),
        scratch_shapes=[pltpu.VMEM((S, LANES), F32) for _ in range(3)] + [pltpu.VMEM((2, 2 * BLOCK, 2 * BLOCK), F32)],
        compiler_params=_params(("arbitrary",)),
    )(h, h, h, cb, sb, sinks_l)


def attn_b_bwd(h, cb, sb, sinks_l, yb, lse, dyb, dh, *, off_q, n_qtiles, name):
    S = h.shape[0]
    nblk = S // BLOCK
    scale = HEAD_DIM_B ** -0.5
    half = HEAD_DIM_B // 8
    tiles_per_group = n_qtiles // 2
    n_steps = n_qtiles + 2
    W = 2 * BLOCK
    assert nblk >= 2 and nblk % UNROLL_B == 0

    def body(q_ref, k_ref, v_ref, c_ref, s_ref, sink_ref, y_ref, lse_ref, dy_ref, dh_in,
             o_ref, dsink_ref, qs, kg, vg, dk_acc, dv_acc, bias, dq_ref):
        t = pl.program_id(0)

        @pl.when(t == 0)
        def _():
            dk_acc[...] = jnp.zeros_like(dk_acc)
            dv_acc[...] = jnp.zeros_like(dv_acc)
            dsink_ref[...] = jnp.zeros_like(dsink_ref)
            bb = _bwd_bias(True)
            bias[...] = jnp.concatenate([bb, bb], axis=1)

        @pl.when(t < n_qtiles)
        def _():
            g = t // tiles_per_group
            qs[...] = _rope(q_ref[...], c_ref[...], s_ref[...], half, HEAD_DIM_B)

            @pl.when(t % tiles_per_group == 0)
            def _():
                kg[...] = _both_halves(_rope(k_ref[...], c_ref[...], s_ref[...], half, HEAD_DIM_B), g)
                vg[...] = _both_halves(v_ref[...], g)

            dq_ref[...] = jnp.zeros_like(dq_ref)
            sink = _head_rows(sink_ref, t, W)
            row = lax.broadcasted_iota(jnp.int32, (2 * W, 1), 0) % W
            low = _lane((W, LANES)) < HEAD_DIM_B

            def blk(it, dsink):
                js = [it * UNROLL_B_BWD + u for u in range(UNROLL_B_BWD)]
                rks = [pl.ds(pl.multiple_of(j * BLOCK, BLOCK), BLOCK) for j in js]
                rws = [pl.ds(pl.multiple_of(jnp.minimum(j, nblk - 2) * BLOCK, BLOCK), W) for j in js]
                q2 = [_stack_heads(qs[rw, :]).astype(BF16) for rw in rws]
                dy2 = [_stack_heads(dy_ref[rw, :]) for rw in rws]
                dsum = [jnp.sum(dy * _stack_heads(y_ref[rw, :]), axis=-1, keepdims=True) for dy, rw in zip(dy2, rws)]
                dy2b = [dy.astype(BF16) for dy in dy2]
                lse2 = []
                for rw in rws:
                    lt = lse_ref[rw, :]
                    lr = pltpu.roll(lt, HEAD_DIM_B, 1)
                    lse2.append(jnp.concatenate([jnp.where(low, lt, lr), jnp.where(low, lr, lt)], axis=0))
                kb = [kg[rk, :].astype(BF16) for rk in rks]
                ss = [_dot_nt(q, k) * scale + bias[(j == nblk - 1).astype(jnp.int32)] for q, k, j in zip(q2, kb, js)]
                dps = [_dot_nt(dy, vg[rk, :].astype(BF16)) for dy, rk in zip(dy2b, rks)]
                ps = [jnp.exp(s - l2) for s, l2 in zip(ss, lse2)]
                dss = [(p * (dp - dm) * scale).astype(BF16) for p, dp, dm in zip(ps, dps, dsum)]
                dvs = [_fold_halves(_dot_tn(p.astype(BF16), dy)) for p, dy in zip(ps, dy2b)]
                dks = [_fold_halves(_dot_tn(ds, q)) for ds, q in zip(dss, q2)]
                dqs = [_dot(ds, k) for ds, k in zip(dss, kb)]
                for j, rk, rw, dv, dk, dq, l2, dm in zip(js, rks, rws, dvs, dks, dqs, lse2, dsum):
                    dv_acc[g, rk, :] += dv
                    dk_acc[g, rk, :] += dk
                    dq_ref[rw, :] += _unstack_heads(dq)
                    diag = (row >= BLOCK).astype(jnp.int32) == (j == nblk - 1).astype(jnp.int32)
                    dsink = dsink - jnp.where(diag, jnp.exp(sink - l2[:, :1]) * dm, 0.0)
                return dsink

            assert nblk % UNROLL_B_BWD == 0
            dsink = lax.fori_loop(0, nblk // UNROLL_B_BWD, blk, jnp.zeros((2 * W, 1), F32))
            o_ref[...] = _rope_t(dq_ref[...], c_ref[...], s_ref[...], half, HEAD_DIM_B).astype(o_ref.dtype)
            d0 = jnp.sum(dsink[:W], axis=0, keepdims=True)
            d1 = jnp.sum(dsink[W:], axis=0, keepdims=True)
            dsink_ref[pl.ds(2 * t, 1), :] = jnp.broadcast_to(d0, (1, LANES))
            dsink_ref[pl.ds(2 * t + 1, 1), :] = jnp.broadcast_to(d1, (1, LANES))

        low_s = _lane((S, LANES)) < HEAD_DIM_B

        @pl.when(t == n_qtiles)
        def _():
            dk = jnp.where(low_s, dk_acc[0], dk_acc[1])
            o_ref[...] = _rope_t(dk, c_ref[...], s_ref[...], half, HEAD_DIM_B).astype(o_ref.dtype)

        @pl.when(t == n_qtiles + 1)
        def _():
            o_ref[...] = jnp.where(low_s, dv_acc[0], dv_acc[1]).astype(o_ref.dtype)

    qt = lambda t: jnp.minimum(t, n_qtiles - 1)
    full = lambda col: pl.BlockSpec((S, LANES), lambda t: (0, col))
    per_tile = pl.BlockSpec((S, LANES), lambda t: (0, qt(t)))
    return pl.pallas_call(
        body, name=name, grid=(n_steps,),
        in_specs=[pl.BlockSpec((S, LANES), lambda t: (0, off_q + qt(t))), full(off_q + n_qtiles),
                  full(off_q + n_qtiles + 1), full(0), full(0), pl.BlockSpec(sinks_l.shape, lambda t: (0, 0)),
                  per_tile, per_tile, per_tile, pl.BlockSpec(memory_space=pl.ANY)],
        out_specs=(pl.BlockSpec((S, LANES), lambda t: (0, off_q + t)), pl.BlockSpec(sinks_l.shape, lambda t: (0, 0))),
        out_shape=(jax.ShapeDtypeStruct(dh.shape, dh.dtype), jax.ShapeDtypeStruct(sinks_l.shape, F32)),
        input_output_aliases={9: 0},
        scratch_shapes=[pltpu.VMEM((S, LANES), F32) for _ in range(3)]
        + [pltpu.VMEM((2, S, LANES), F32), pltpu.VMEM((2, S, LANES), F32), pltpu.VMEM((2, 2 * W, BLOCK), F32),
           pltpu.VMEM((S, LANES), F32)],
        compiler_params=_params(("arbitrary",)),
    )(h, h, h, cb, sb, sinks_l, yb, lse, dyb, dh)


def mem_attn_fwd(q, kv, *, name, tm=512):
    S, D = q.shape
    n_mem = kv.shape[0]
    hd = D // N_MEM_HEADS
    scale = hd ** -0.5
    tm = _tile(S, tm, 8)

    def body(q_ref, kv_ref, o_ref):
        for hh in range(N_MEM_HEADS):
            cols = slice(hh * hd, (hh + 1) * hd)
            s = _dot_nt(q_ref[:, cols], kv_ref[:, cols]) * scale
            s = s - jnp.max(s, axis=-1, keepdims=True)
            e = jnp.exp(s)
            p = e / jnp.sum(e, axis=-1, keepdims=True)
            o_ref[:, cols] = _dot(p.astype(BF16), kv_ref[:, D + hh * hd:D + (hh + 1) * hd]).astype(o_ref.dtype)

    return pl.pallas_call(
        body, name=name, grid=(S // tm,),
        in_specs=[pl.BlockSpec((tm, D), lambda i: (i, 0)), pl.BlockSpec((n_mem, 2 * D), lambda i: (0, 0))],
        out_specs=pl.BlockSpec((tm, D), lambda i: (i, 0)),
        out_shape=jax.ShapeDtypeStruct((S, D), BF16),
        compiler_params=_params(("parallel",)),
    )(q, kv)


def mem_attn_bwd(q, kv, do, *, name, tm=512):
    S, D = q.shape
    n_mem = kv.shape[0]
    hd = D // N_MEM_HEADS
    scale = hd ** -0.5
    tm = _tile(S, tm, 8)

    def body(q_ref, kv_ref, do_ref, dq_ref, dkv_ref):
        i = pl.program_id(0)

        @pl.when(i == 0)
        def _():
            dkv_ref[...] = jnp.zeros_like(dkv_ref)

        for hh in range(N_MEM_HEADS):
            cols = slice(hh * hd, (hh + 1) * hd)
            vcols = slice(D + hh * hd, D + (hh + 1) * hd)
            qh, kh, vh, doh = q_ref[:, cols], kv_ref[:, cols], kv_ref[:, vcols], do_ref[:, cols]
            s = _dot_nt(qh, kh) * scale
            s = s - jnp.max(s, axis=-1, keepdims=True)
            e = jnp.exp(s)
            p = e / jnp.sum(e, axis=-1, keepdims=True)
            dp = _dot_nt(doh, vh)
            ds = (p * (dp - jnp.sum(dp * p, axis=-1, keepdims=True)) * scale).astype(BF16)
            dq_ref[:, cols] = _dot(ds, kh).astype(dq_ref.dtype)
            dkv_ref[:, cols] += _dot_tn(ds, qh)
            dkv_ref[:, vcols] += _dot_tn(p.astype(BF16), doh)

    row = pl.BlockSpec((tm, D), lambda i: (i, 0))
    kvs = pl.BlockSpec((n_mem, 2 * D), lambda i: (0, 0))
    return pl.pallas_call(
        body, name=name, grid=(S // tm,),
        in_specs=[row, kvs, row], out_specs=(row, kvs),
        out_shape=(jax.ShapeDtypeStruct((S, D), BF16), jax.ShapeDtypeStruct((n_mem, 2 * D), F32)),
        compiler_params=_params(("arbitrary",)),
    )(q, kv, do)


def _rows_view(t):
    return t.reshape(-1, t.shape[-1])


def _row_tile(rows, cols, target_elems=2048 * 1024):
    return _tile(rows, max(8, target_elems // cols), 8)


def cast_bf16(w, *, name):
    v = _rows_view(w)
    R, C = v.shape
    tr = _row_tile(R, C)

    def body(w_ref, o_ref):
        o_ref[...] = w_ref[...].astype(BF16)

    spec = pl.BlockSpec((tr, C), lambda i: (i, 0))
    out = pl.pallas_call(body, name=name, grid=(R // tr,), in_specs=[spec], out_specs=spec,
                         out_shape=jax.ShapeDtypeStruct((R, C), BF16), compiler_params=_params(("parallel",)))(v)
    return out.reshape(w.shape)


def mesh_place():
    return tuple(lax.axis_index(a).astype(jnp.int32).reshape(1) for a in ("x", "y", "c"))


def pair_sum(p, r1, place, *, name):
    nsh, r, c = p.shape
    hr = r // 2
    tr = _row_tile(hr, c)
    nt = hr // tr

    def body(x_ref, y_ref, c_ref, p_ref, r_ref, o_ref):
        o_ref[...] = (p_ref[...].astype(F32) + r_ref[...].astype(F32)).astype(BF16)

    return pl.pallas_call(
        body, name=name,
        grid_spec=pltpu.PrefetchScalarGridSpec(
            num_scalar_prefetch=3, grid=(nsh, nt),
            in_specs=[pl.BlockSpec((None, tr, c), lambda s, i, x, y, cc: (s, cc[0] * nt + i, 0)),
                      pl.BlockSpec((None, tr, c), lambda s, i, x, y, cc: (s, i, 0))],
            out_specs=pl.BlockSpec((None, tr, c), lambda s, i, x, y, cc: (s, i, 0))),
        out_shape=jax.ShapeDtypeStruct((nsh, hr, c), BF16),
        compiler_params=_params(("parallel", "parallel")),
    )(*place, p, r1)


def cast_into_slot(w, li, place, *, name):
    _, R, C = w.shape
    tr = _row_tile(R, C)

    def body(x_ref, y_ref, c_ref, w_ref, o_ref):
        o_ref[...] = w_ref[...].astype(BF16)

    return pl.pallas_call(
        body, name=name,
        grid_spec=pltpu.PrefetchScalarGridSpec(
            num_scalar_prefetch=3, grid=(R // tr,),
            in_specs=[pl.BlockSpec((None, tr, C), lambda i, x, y, cc: (li, i, 0))],
            out_specs=pl.BlockSpec((None, tr, C), lambda i, x, y, cc: (2 * x[0] + y[0], i, 0))),
        out_shape=jax.ShapeDtypeStruct((N_CHIPS, R, C), BF16),
        compiler_params=_params(("parallel",)),
    )(*place, w)


def chip_sum(q, r2, place, gbuf, li, *, name):
    _, hr, c = q.shape
    tr = _row_tile(hr, c, 1024 * 1024)
    nt = hr // tr

    def body(x_ref, y_ref, c_ref, q_ref, r_ref, g_in, o_ref):
        acc = q_ref[...].astype(F32)
        for k in range(r_ref.shape[0]):
            acc = acc + r_ref[k].astype(F32)
        o_ref[...] = acc

    return pl.pallas_call(
        body, name=name,
        grid_spec=pltpu.PrefetchScalarGridSpec(
            num_scalar_prefetch=3, grid=(nt,),
            in_specs=[pl.BlockSpec((None, tr, c), lambda i, x, y, cc: (2 * x[0] + y[0], i, 0)),
                      pl.BlockSpec((r2.shape[0], tr, c), lambda i, x, y, cc: (0, i, 0)),
                      pl.BlockSpec(memory_space=pl.ANY)],
            out_specs=pl.BlockSpec((None, tr, c), lambda i, x, y, cc: (li, cc[0] * nt + i, 0))),
        out_shape=jax.ShapeDtypeStruct(gbuf.shape, F32),
        input_output_aliases={5: 0},
        compiler_params=_params(("parallel",)),
    )(*place, q, r2, gbuf)


def adamw(w, g, m, v, *, name, emit_g=False):
    shape = w.shape
    wv, gv, mv, vv = (_rows_view(t) for t in (w, g, m, v))
    R, C = wv.shape
    tr = _row_tile(R, C, 512 * 1024)
    c1 = 1.0 / (1.0 - ADAM_B1 ** ADAM_STEP)
    c2 = 1.0 / (1.0 - ADAM_B2 ** ADAM_STEP)
    n_out = 4 if emit_g else 3

    def body(w_ref, g_ref, m_ref, v_ref, d_ref, nm_ref, nv_ref, *go_ref):
        g_ = g_ref[...]
        nm = ADAM_B1 * m_ref[...] + (1.0 - ADAM_B1) * g_
        nv = ADAM_B2 * v_ref[...] + (1.0 - ADAM_B2) * (g_ * g_)
        m_hat = nm * c1
        v_hat = nv * c2
        d_ref[...] = -ADAM_LR * (m_hat / (jnp.sqrt(v_hat) + ADAM_EPS) + ADAM_WD * w_ref[...])
        nm_ref[...] = nm
        nv_ref[...] = nv
        if emit_g:
            go_ref[0][...] = g_

    spec = pl.BlockSpec((tr, C), lambda i: (i, 0))
    o = jax.ShapeDtypeStruct((R, C), F32)
    outs = pl.pallas_call(body, name=name, grid=(R // tr,), in_specs=[spec] * 4, out_specs=(spec,) * n_out,
                          out_shape=(o,) * n_out, compiler_params=_params(("parallel",)))(wv, gv, mv, vv)
    return tuple(t.reshape(shape) for t in outs)


def _place():
    x, y, c = lax.axis_index("x"), lax.axis_index("y"), lax.axis_index("c")
    others = [(1 - x, y), (x, 1 - y), (1 - x, 1 - y)]
    return x, y, c, others


def _any_specs(n):
    return [pl.BlockSpec(memory_space=pl.ANY) for _ in range(n)]


HBM_SPEC = pl.BlockSpec(memory_space=pltpu.HBM)
SEM_SPEC = pl.BlockSpec(memory_space=pltpu.SEMAPHORE)
DATAFLOW = pltpu.SideEffectType.DATAFLOW_SIDE_EFFECTING
TOKEN = jax.ShapeDtypeStruct((8, LANES), F32)


def _in_hbm(arrays):
    return [pltpu.with_memory_space_constraint(a, pltpu.HBM) for a in arrays]


def _gather_copy(g, t, j, slot, px, py, c, send, recv):
    hr = g[t].shape[1] // 2
    rows = g[t].at[slot, pl.ds(c * hr, hr)]
    return pltpu.make_async_remote_copy(rows, rows, send.at[3 * t + j], recv.at[3 * t + j], device_id=(px, py, c), device_id_type=MESH)


def gather_start(gs, after, *, name):
    n = len(gs)

    def body(*refs):
        g, token = refs[:n], refs[-1]
        send, recv = refs[n + 1], refs[n + 2]
        x, y, c, others = _place()
        for t in range(n):
            for j, (px, py) in enumerate(others):
                _gather_copy(g, t, j, 2 * x + y, px, py, c, send, recv).start()
        token[...] = jnp.zeros_like(token)

    outs = pl.pallas_call(
        body, name=name,
        in_specs=[HBM_SPEC] * n + [pl.BlockSpec(memory_space=pl.ANY)],
        out_specs=(SEM_SPEC, SEM_SPEC, *[HBM_SPEC] * n, pl.BlockSpec(memory_space=pltpu.VMEM)),
        out_shape=(pltpu.SemaphoreType.DMA((3 * n,)), pltpu.SemaphoreType.DMA((3 * n,)),
                   *[pltpu.HBM(g.shape, g.dtype) for g in gs], TOKEN),
        input_output_aliases={t: 2 + t for t in range(n)},
        compiler_params=pltpu.CompilerParams(has_side_effects=DATAFLOW),
    )(*_in_hbm(gs), after)
    return outs[0], outs[1], list(outs[2:2 + n]), outs[-1]


def gather_wait(send, recv, gs, after, *, name):
    n = len(gs)

    def body(*refs):
        g, token = refs[:n], refs[-1]
        send, recv = refs[n], refs[n + 1]
        x, y, c, others = _place()
        for t in range(n):
            for j, (px, py) in enumerate(others):
                _gather_copy(g, t, j, 2 * x + y, px, py, c, send, recv).wait_send()
                _gather_copy(g, t, j, 2 * px + py, px, py, c, send, recv).wait_recv()
        token[...] = jnp.zeros_like(token)

    outs = pl.pallas_call(
        body, name=name,
        in_specs=[HBM_SPEC] * n + [SEM_SPEC, SEM_SPEC, pl.BlockSpec(memory_space=pl.ANY)],
        out_specs=tuple([HBM_SPEC] * n) + (pl.BlockSpec(memory_space=pltpu.VMEM),),
        out_shape=tuple(pltpu.HBM(g.shape, g.dtype) for g in gs) + (TOKEN,),
        input_output_aliases={t: t for t in range(n)},
        compiler_params=pltpu.CompilerParams(has_side_effects=DATAFLOW),
    )(*gs, send, recv, after)
    return list(outs[:n]), outs[n]


def gather_forward(gs, *, name):
    n = len(gs)

    def body(*refs):
        g = refs[n:2 * n]
        send, recv = refs[2 * n:]
        x, y, c, others = _place()
        cps = []
        for t in range(n):
            hr = g[t].shape[1] // 2
            for j, (px, py) in enumerate(others):
                rows = g[t].at[2 * px + py, pl.ds(c * hr, hr)]
                cp = pltpu.make_async_remote_copy(rows, rows, send.at[3 * t + j], recv.at[3 * t + j],
                                                  device_id=(x, y, 1 - c), device_id_type=MESH)
                cp.start()
                cps.append(cp)
        for t in range(n):
            hr = g[t].shape[1] // 2
            for j, (px, py) in enumerate(others):
                rows = g[t].at[2 * px + py, pl.ds((1 - c) * hr, hr)]
                pltpu.make_async_remote_copy(rows, rows, send.at[3 * t + j], recv.at[3 * t + j],
                                             device_id=(x, y, 1 - c), device_id_type=MESH).wait_recv()
        for cp in cps:
            cp.wait_send()

    return pl.pallas_call(
        body, name=name,
        in_specs=_any_specs(n), out_specs=_any_specs(n),
        out_shape=[jax.ShapeDtypeStruct(g.shape, g.dtype) for g in gs],
        input_output_aliases={t: t for t in range(n)},
        scratch_shapes=[pltpu.SemaphoreType.DMA((3 * n,)), pltpu.SemaphoreType.DMA((3 * n,))],
        compiler_params=pltpu.CompilerParams(has_side_effects=True),
    )(*gs)


def sibling_halves(parts, *, name):
    n = len(parts)

    def body(*refs):
        src, dst = refs[:n], refs[n:2 * n]
        send, recv = refs[2 * n:]
        x, y, c, _ = _place()
        cps = []
        for t in range(n):
            hr = src[t].shape[1] // 2
            cp = pltpu.make_async_remote_copy(src[t].at[:, pl.ds((1 - c) * hr, hr)], dst[t], send.at[t], recv.at[t],
                                              device_id=(x, y, 1 - c), device_id_type=MESH)
            cp.start()
            cps.append(cp)
        for cp in cps:
            cp.wait()

    return pl.pallas_call(
        body, name=name,
        in_specs=_any_specs(n), out_specs=_any_specs(n),
        out_shape=[jax.ShapeDtypeStruct((p.shape[0], p.shape[1] // 2, p.shape[2]), p.dtype) for p in parts],
        scratch_shapes=[pltpu.SemaphoreType.DMA((n,)), pltpu.SemaphoreType.DMA((n,))],
        compiler_params=pltpu.CompilerParams(has_side_effects=True),
    )(*parts)


def _chips_copy(q, land, t, j, px, py, c, send, recv):
    return pltpu.make_async_remote_copy(q[t].at[2 * px + py], land[t].at[j], send.at[3 * t + j], recv.at[3 * t + j],
                                        device_id=(px, py, c), device_id_type=MESH)


def chips_start(qs, *, name):
    n = len(qs)

    def body(*refs):
        q, land, token = refs[:n], refs[n:2 * n], refs[-1]
        send, recv = refs[2 * n], refs[2 * n + 1]
        x, y, c, others = _place()
        for t in range(n):
            for j, (px, py) in enumerate(others):
                _chips_copy(q, land, t, j, px, py, c, send, recv).start()
        token[...] = jnp.zeros_like(token)

    lands = [lax.empty((3,) + q.shape[1:], q.dtype) for q in qs]
    outs = pl.pallas_call(
        body, name=name,
        in_specs=[HBM_SPEC] * (2 * n),
        out_specs=(SEM_SPEC, SEM_SPEC, *[HBM_SPEC] * (2 * n), pl.BlockSpec(memory_space=pltpu.VMEM)),
        out_shape=(pltpu.SemaphoreType.DMA((3 * n,)), pltpu.SemaphoreType.DMA((3 * n,)),
                   *[pltpu.HBM(a.shape, a.dtype) for a in qs + lands], TOKEN),
        input_output_aliases={t: 2 + t for t in range(2 * n)},
        compiler_params=pltpu.CompilerParams(has_side_effects=DATAFLOW),
    )(*_in_hbm(qs + lands))
    return outs[0], outs[1], list(outs[2:2 + n]), list(outs[2 + n:2 + 2 * n]), outs[-1]


def chips_wait(send, recv, qs, lands, after, *, name):
    n = len(qs)

    def body(*refs):
        q, land = refs[:n], refs[n:2 * n]
        send, recv = refs[2 * n], refs[2 * n + 1]
        x, y, c, others = _place()
        for t in range(n):
            for j, (px, py) in enumerate(others):
                cp = _chips_copy(q, land, t, j, px, py, c, send, recv)
                cp.wait_send()
                cp.wait_recv()

    outs = pl.pallas_call(
        body, name=name,
        in_specs=[HBM_SPEC] * (2 * n) + [SEM_SPEC, SEM_SPEC, pl.BlockSpec(memory_space=pl.ANY)],
        out_specs=tuple([HBM_SPEC] * (2 * n)),
        out_shape=tuple(pltpu.HBM(a.shape, a.dtype) for a in qs + lands),
        input_output_aliases={t: t for t in range(2 * n)},
        compiler_params=pltpu.CompilerParams(has_side_effects=DATAFLOW),
    )(*qs, *lands, send, recv, after)
    return list(outs[:n]), list(outs[n:])


def join_halves(fulls, li, *, name):
    n = len(fulls)

    def body(*refs):
        g = refs[n:2 * n]
        send, recv = refs[2 * n:]
        x, y, c, _ = _place()
        cps = []
        for t in range(n):
            hr = g[t].shape[1] // 2
            rows = g[t].at[li, pl.ds(c * hr, hr)]
            cp = pltpu.make_async_remote_copy(rows, rows, send.at[t], recv.at[t], device_id=(x, y, 1 - c), device_id_type=MESH)
            cp.start()
            cps.append(cp)
        for t in range(n):
            hr = g[t].shape[1] // 2
            rows = g[t].at[li, pl.ds((1 - c) * hr, hr)]
            pltpu.make_async_remote_copy(rows, rows, send.at[t], recv.at[t],
                                         device_id=(x, y, 1 - c), device_id_type=MESH).wait_recv()
        for cp in cps:
            cp.wait_send()

    return pl.pallas_call(
        body, name=name,
        in_specs=_any_specs(n), out_specs=_any_specs(n),
        out_shape=[jax.ShapeDtypeStruct(g.shape, g.dtype) for g in fulls],
        input_output_aliases={t: t for t in range(n)},
        scratch_shapes=[pltpu.SemaphoreType.DMA((n,)), pltpu.SemaphoreType.DMA((n,))],
        compiler_params=pltpu.CompilerParams(has_side_effects=True),
    )(*fulls)


def _join_copy(g, t, li, half, c, x, y, send, recv):
    hr = g[t].shape[1] // 2
    rows = g[t].at[li, pl.ds(half * hr, hr)]
    return pltpu.make_async_remote_copy(rows, rows, send.at[t], recv.at[t], device_id=(x, y, 1 - c), device_id_type=MESH)


def join_start(fulls, li, *, name):
    n = len(fulls)

    def body(*refs):
        g, token = refs[:n], refs[-1]
        send, recv = refs[n], refs[n + 1]
        x, y, c, _ = _place()
        for t in range(n):
            _join_copy(g, t, li, c, c, x, y, send, recv).start()
        token[...] = jnp.zeros_like(token)

    outs = pl.pallas_call(
        body, name=name,
        in_specs=[HBM_SPEC] * n,
        out_specs=(SEM_SPEC, SEM_SPEC, *[HBM_SPEC] * n, pl.BlockSpec(memory_space=pltpu.VMEM)),
        out_shape=(pltpu.SemaphoreType.DMA((n,)), pltpu.SemaphoreType.DMA((n,)),
                   *[pltpu.HBM(g.shape, g.dtype) for g in fulls], TOKEN),
        input_output_aliases={t: 2 + t for t in range(n)},
        compiler_params=pltpu.CompilerParams(has_side_effects=DATAFLOW),
    )(*_in_hbm(fulls))
    return outs[0], outs[1], list(outs[2:2 + n]), outs[-1]


def join_wait(send, recv, fulls, li, after, *, name):
    n = len(fulls)

    def body(*refs):
        g = refs[:n]
        send, recv = refs[n], refs[n + 1]
        x, y, c, _ = _place()
        for t in range(n):
            _join_copy(g, t, li, c, c, x, y, send, recv).wait_send()
            _join_copy(g, t, li, 1 - c, c, x, y, send, recv).wait_recv()

    outs = pl.pallas_call(
        body, name=name,
        in_specs=[HBM_SPEC] * n + [SEM_SPEC, SEM_SPEC, pl.BlockSpec(memory_space=pl.ANY)],
        out_specs=tuple([HBM_SPEC] * n),
        out_shape=tuple(pltpu.HBM(g.shape, g.dtype) for g in fulls),
        input_output_aliases={t: t for t in range(n)},
        compiler_params=pltpu.CompilerParams(has_side_effects=DATAFLOW),
    )(*fulls, send, recv, after)
    return list(outs)


def allreduce_small(t, *, name):
    R, C = t.shape

    def body(t_ref, o_ref, land, send, recv):
        x, y, c, _ = _place()
        me = 4 * x + 2 * y + c
        land[me] = t_ref[...]
        cps = []
        for j in range(1, 8):
            px, py, pc = (x + (j >> 2)) % 2, (y + ((j >> 1) & 1)) % 2, (c + (j & 1)) % 2
            cp = pltpu.make_async_remote_copy(t_ref, land.at[me], send.at[j - 1], recv.at[j - 1],
                                              device_id=(px, py, pc), device_id_type=MESH)
            cp.start()
            cps.append(cp)
        for j in range(1, 8):
            px, py, pc = (x + (j >> 2)) % 2, (y + ((j >> 1) & 1)) % 2, (c + (j & 1)) % 2
            pltpu.make_async_remote_copy(t_ref, land.at[4 * px + 2 * py + pc], send.at[j - 1], recv.at[j - 1],
                                         device_id=(px, py, pc), device_id_type=MESH).wait_recv()
        for cp in cps:
            cp.wait_send()
        acc = land[0]
        for k in range(1, 8):
            acc = acc + land[k]
        o_ref[...] = acc

    return pl.pallas_call(
        body, name=name,
        in_specs=[pl.BlockSpec(memory_space=pltpu.VMEM)], out_specs=pl.BlockSpec(memory_space=pltpu.VMEM),
        out_shape=jax.ShapeDtypeStruct((R, C), F32),
        scratch_shapes=[pltpu.VMEM((8, R, C), F32), pltpu.SemaphoreType.DMA((7,)), pltpu.SemaphoreType.DMA((7,))],
        compiler_params=pltpu.CompilerParams(has_side_effects=True),
    )(t)


def _layer_fwd(x, xb, memb, w_in, rest, P, tabs, alpha, li):
    ca, sa, cb, sb = tabs
    nA = P["gn_a"].shape[1] // HEAD_DIM_A
    nQ = P["gn_b"].shape[1] // LANES
    nm = lambda s: f"L{li}_{s}"
    h = mm_nn(xb, w_in, name=nm("h"), out_dtype=F32, tn=2304)
    ya, lse_a = attn_a_fwd(h, ca, sa, n_heads=nA, name=nm("attn_a"))
    yb, lse_b = attn_b_fwd(h, cb, sb, P["sinks_l"], off_q=3 * nA, n_qtiles=nQ, name=nm("attn_b"))
    ymix = rms_fwd(ya, yb, P["gn_a"], P["gn_b"], name=nm("rms"))
    W, P = rest(ymix, P)
    z1, x1, x1b = mm_ln(ymix, W["w_out"][0], x, P["ln_mix_g"], P["ln_mix_b"], name=nm("out_ln"), alpha=alpha,
                        tm=256, tk=ymix.shape[1])
    qm = mm_nn(x1b, W["w_mq"], name=nm("mq"), out_dtype=BF16)
    kv = mm_nn(memb, W["w_mkv"], name=nm("mkv"), out_dtype=BF16, tm=256)
    o = mem_attn_fwd(qm, kv, name=nm("mem_attn"))
    z2, x2, x2b = mm_ln(o, W["w_mo"][0], x1, P["ln_mem_g"], P["ln_mem_b"], name=nm("mo_ln"), alpha=alpha,
                        tm=256, tk=o.shape[1])
    u, a = mm_nn(x2b, W["w_up"], name=nm("up"), out_dtype=BF16, relu2=True)
    z3, x3, x3b = mm_ln(a, W["w_down"][0], x2, P["ln_ff_g"], P["ln_ff_b"], name=nm("down_ln"), alpha=alpha)
    saved = dict(xb=xb, h=h, ya=ya, lse_a=lse_a, yb=yb, lse_b=lse_b, ymix=ymix, z1=z1, x1b=x1b, qm=qm, kv=kv, o=o,
                 z2=z2, x2b=x2b, u=u, a=a, z3=z3)
    return x3, x3b, saved


def _layer_bwd(dx3, sv, memb, W, P, tabs, alpha, li, hook=None):
    ca, sa, cb, sb = tabs
    nA = P["gn_a"].shape[1] // HEAD_DIM_A
    nQ = P["gn_b"].shape[1] // LANES
    nm = lambda s: f"L{li}_b_{s}"
    nsh = lambda k: W[k].shape[0]
    gw, gs = {}, {}
    dz3, dz3b, gs["ln_ff_g"], gs["ln_ff_b"] = ln_bwd(dx3, sv["z3"], P["ln_ff_g"], name=nm("ln_ff"))
    gw["w_down"] = mm_tn(sv["a"], dz3b, nsh("w_down"), name=nm("dw_down"))
    du = mm_nt(dz3b, W["w_down"], name=nm("du"), out_dtype=BF16, umul=sv["u"])
    gw["w_up"] = mm_tn(sv["x2b"], du, nsh("w_up"), name=nm("dw_up"))
    dx2 = mm_nt(du, W["w_up"], name=nm("dx2"), out_dtype=F32, resid=dz3, alpha=alpha)
    dz2, dz2b, gs["ln_mem_g"], gs["ln_mem_b"] = ln_bwd(dx2, sv["z2"], P["ln_mem_g"], name=nm("ln_mem"))
    gw["w_mo"] = mm_tn(sv["o"], dz2b, nsh("w_mo"), name=nm("dw_mo"))
    do = mm_nt(dz2b, W["w_mo"], name=nm("do"), out_dtype=BF16)
    dqm, dkv = mem_attn_bwd(sv["qm"], sv["kv"], do, name=nm("mem_attn"))
    gw["w_mq"] = mm_tn(sv["x1b"], dqm, nsh("w_mq"), name=nm("dw_mq"))
    gw["w_mkv"] = mm_tn(memb, cast_bf16(dkv, name=nm("dkv_cast")), nsh("w_mkv"), name=nm("dw_mkv"), tm=256)
    dx1 = mm_nt(dqm, W["w_mq"], name=nm("dx1"), out_dtype=F32, resid=dz2, alpha=alpha)
    if hook is not None:
        P = hook(gw, dx1, P)
    dz1, dz1b, gs["ln_mix_g"], gs["ln_mix_b"] = ln_bwd(dx1, sv["z1"], P["ln_mix_g"], name=nm("ln_mix"))
    gw["w_out"] = mm_tn(sv["ymix"], dz1b, nsh("w_out"), name=nm("dw_out"))
    dymix = mm_nt(dz1b, W["w_out"], name=nm("dymix"), out_dtype=F32)
    dya, dyb, gs["gn_a"], gs["gn_b"] = rms_bwd(dymix, sv["ya"], sv["yb"], P["gn_a"], P["gn_b"], name=nm("rms"))
    dh = attn_a_bwd(sv["h"], ca, sa, sv["ya"], sv["lse_a"], dya, n_heads=nA, name=nm("attn_a"))
    dh, gs["sinks"] = attn_b_bwd(sv["h"], cb, sb, P["sinks_l"], sv["yb"], sv["lse_b"], dyb, dh,
                                 off_q=3 * nA, n_qtiles=nQ, name=nm("attn_b"))
    gw["w_in"] = mm_tn(sv["xb"], dh, nsh("w_in"), name=nm("dw_in"), tn=2304)
    dx0 = mm_nt(dh, W["w_in"], name=nm("dx0"), out_dtype=F32, resid=dz1, alpha=alpha, tr=2304)
    return dx0, gw, gs


def _gathered_view(name, g):
    if name == "w_in":
        return jnp.concatenate([g[k] for k in range(N_CHIPS)], axis=1)[None]
    if name in COL_SHARDED:
        return g
    return g.reshape(1, g.shape[0] * g.shape[1], g.shape[2])


def _to_shards(name, gw):
    if name == "w_in":
        n = gw.shape[2] // N_CHIPS
        return jnp.stack([gw[0, :, k * n:(k + 1) * n] for k in range(N_CHIPS)])
    if name in COL_SHARDED:
        return gw
    return gw.reshape(N_CHIPS, gw.shape[1] // N_CHIPS, gw.shape[2])


def _step(x, mem, positions, loss_target, w, m, v):
    S, D = x.shape[1], x.shape[2]
    depth = w["w_in"].shape[0]
    alpha = (2 * depth) ** 0.25
    x0 = x[0]
    memb = cast_bf16(mem[0], name="mem_cast")
    pos = positions[0]
    tabs = rope_tables(pos, HEAD_DIM_A // 4, HEAD_DIM_A) + rope_tables(pos, HEAD_DIM_B // 4, HEAD_DIM_B)
    place = mesh_place()

    def small(li):
        P = {k: w[k][li][None] for k in ("gn_a", "gn_b", "ln_mix_g", "ln_mix_b", "ln_mem_g", "ln_mem_b", "ln_ff_g", "ln_ff_b")}
        P["sinks_l"] = jnp.broadcast_to(w["sinks"][li][:, None], (w["sinks"].shape[1], LANES))
        return P

    rest_names = tuple(k for k in BIG if k != "w_in")
    chain = [(0, ("w_in",)), (0, rest_names)] + [(li, BIG) for li in range(1, depth)]
    casts = [[cast_into_slot(w[k], li, place, name=f"L{li}_cast_{k}") for k in names] for li, names in chain]
    started = {0: gather_start(casts[0], tabs[1], name="G0_gather_start")}

    def land(gi, after):
        send, recv, gs, tok0 = started.pop(gi)
        gs, landed = gather_wait(send, recv, gs, tok0 if after is None else after, name=f"G{gi}_gather_wait")
        token = None
        if gi + 1 < len(chain):
            started[gi + 1] = gather_start(casts[gi + 1], landed, name=f"G{gi + 1}_gather_start")
            token = started[gi + 1][3]
        gs = gather_forward(gs, name=f"G{gi}_gather_fwd")
        return dict(zip(chain[gi][1], gs)), token

    def ordered(a, token):
        return a if token is None else a + token[:1, :1].astype(a.dtype)

    xs, xbs, saved, Ws = x0, cast_bf16(x0, name="x_cast"), [], []
    for li in range(depth):
        gi = 0 if li == 0 else li + 1
        got, token = land(gi, None if li == 0 else xs)
        W = {"w_in": _gathered_view("w_in", got["w_in"])}
        tabs_l = (ordered(tabs[0], token),) + tabs[1:]

        def rest(after, P, li=li, got=got, W=W):
            if li == 0:
                got, token = land(1, after)
                P = dict(P, ln_mix_g=ordered(P["ln_mix_g"], token))
            W.update({k: _gathered_view(k, got[k]) for k in rest_names})
            return W, P

        xs, xbs, sv = _layer_fwd(xs, xbs, memb, W["w_in"], rest, small(li), tabs_l, alpha, li)
        saved.append(sv)
        Ws.append(W)
    dy, loss_part = loss_head(xs, loss_target[0], name="loss")
    loss = lax.psum(0.5 / D * jnp.sum(loss_part), ("x", "y", "c"))

    g_big = {k: lax.empty(w[k].shape, F32) for k in BIG}
    g_small = [None] * depth

    def begin(li, names, gw, tag):
        parts = [_to_shards(k, gw[k]) for k in names]
        r1 = sibling_halves(parts, name=f"L{li}{tag}_rs_sibling")
        qs = [pair_sum(p, r, place, name=f"L{li}_rs_pair_{k}") for k, p, r in zip(names, parts, r1)]
        send, recv, qs, lands, token = chips_start(qs, name=f"L{li}{tag}_rs_chips_start")
        return (li, names, tag, send, recv, qs, lands), token

    joins = []

    def drain_joins(after):
        while joins:
            li, names, tag, send, recv = joins.pop(0)
            bufs = join_wait(send, recv, [g_big[k] for k in names], li, after, name=f"L{li}{tag}_rs_join_wait")
            g_big.update(zip(names, bufs))

    def finish(pending, after):
        li, names, tag, send, recv, qs, lands = pending
        drain_joins(after)
        qs, lands = chips_wait(send, recv, qs, lands, after, name=f"L{li}{tag}_rs_chips_wait")
        fulls = [chip_sum(q, r, place, g_big[k], li, name=f"L{li}_rs_sum_{k}") for k, q, r in zip(names, qs, lands)]
        send, recv, bufs, token = join_start(fulls, li, name=f"L{li}{tag}_rs_join_start")
        g_big.update(zip(names, bufs))
        joins.append((li, names, tag, send, recv))
        return token

    early = ("w_mq", "w_mkv", "w_mo", "w_up", "w_down")
    late = tuple(k for k in BIG if k not in early)
    pendings, token = [], None
    for li in reversed(range(depth)):
        P = small(li)
        P["ln_ff_g"] = ordered(P["ln_ff_g"], token)
        hook = None
        if li == 0:
            def hook(gw, dx1, P):
                toks = [finish(pendings.pop(), dx1) for _ in range(len(pendings))]
                pend, tok = begin(0, early, gw, "a")
                pendings.append(pend)
                return dict(P, ln_mix_g=ordered(P["ln_mix_g"], sum(toks, tok)))
        dy, gw, g_small[li] = _layer_bwd(dy, saved[li], memb, Ws[li], P, tabs, alpha, li, hook)
        toks = [finish(pendings.pop(), dy) for _ in range(len(pendings))]
        pend, token = begin(li, late if li == 0 else BIG, gw, "b" if li == 0 else "")
        token = sum(toks, token)
        pendings.append(pend)
    finish(pendings.pop(), token)
    drain_joins(token)
    grad_x = dy[None]

    rows = []
    for li in range(depth):
        gs = g_small[li]
        for k in ("ln_mix_g", "ln_mix_b", "ln_mem_g", "ln_mem_b", "ln_ff_g", "ln_ff_b"):
            rows.append(jnp.sum(gs[k], axis=0, keepdims=True))
        rows.append(jnp.concatenate([jnp.sum(gs["gn_a"], axis=0, keepdims=True), jnp.sum(gs["gn_b"], axis=0, keepdims=True)], axis=1))
        sk = gs["sinks"][:, 0][None]
        rows.append(jnp.pad(sk, ((0, 0), (0, D - sk.shape[1]))))
    red = allreduce_small(jnp.concatenate(rows, axis=0), name="small_allreduce").reshape(depth, 8, D)
    wa = w["gn_a"].shape[1]
    grads = dict(g_big)
    for j, k in enumerate(("ln_mix_g", "ln_mix_b", "ln_mem_g", "ln_mem_b", "ln_ff_g", "ln_ff_b")):
        grads[k] = red[:, j]
    grads["gn_a"] = red[:, 6, :wa]
    grads["gn_b"] = red[:, 6, wa:]
    grads["sinks"] = red[:, 7, :w["sinks"].shape[1]]

    delta, new_m, new_v = {}, {}, {}
    small_names = [k for k in w if k not in BIG]
    for k in BIG:
        delta[k], new_m[k], new_v[k], grads[k] = adamw(w[k], grads[k], m[k], v[k], name=f"adamw_{k}", emit_g=True)
    pack = lambda d: jnp.concatenate([jnp.pad(d[k], ((0, 0), (0, D - d[k].shape[1]))) for k in small_names], axis=0)
    ds, ms, vs = adamw(pack(w), pack(grads), pack(m), pack(v), name="adamw_small")
    for j, k in enumerate(small_names):
        sl = (slice(j * depth, (j + 1) * depth), slice(0, w[k].shape[1]))
        delta[k], new_m[k], new_v[k] = ds[sl], ms[sl], vs[sl]
    return loss, grad_x, grads, delta, new_m, new_v


WEIGHTS = ("w_in", "gn_a", "gn_b", "sinks", "w_out", "ln_mix_g", "ln_mix_b", "w_mq", "w_mkv", "w_mo",
           "ln_mem_g", "ln_mem_b", "w_up", "w_down", "ln_ff_g", "ln_ff_b")


def kernel(x, mem, positions, w_in, gn_a, gn_b, sinks, w_out, ln_mix_g, ln_mix_b, w_mq, w_mkv, w_mo, ln_mem_g, ln_mem_b, w_up, w_down, ln_ff_g, ln_ff_b, loss_target, m_w_in, m_gn_a, m_gn_b, m_sinks, m_w_out, m_ln_mix_g, m_ln_mix_b, m_w_mq, m_w_mkv, m_w_mo, m_ln_mem_g, m_ln_mem_b, m_w_up, m_w_down, m_ln_ff_g, m_ln_ff_b, v_w_in, v_gn_a, v_gn_b, v_sinks, v_w_out, v_ln_mix_g, v_ln_mix_b, v_w_mq, v_w_mkv, v_w_mo, v_ln_mem_g, v_ln_mem_b, v_w_up, v_w_down, v_ln_ff_g, v_ln_ff_b):
    w = dict(zip(WEIGHTS, (w_in, gn_a, gn_b, sinks, w_out, ln_mix_g, ln_mix_b, w_mq, w_mkv, w_mo, ln_mem_g, ln_mem_b, w_up, w_down, ln_ff_g, ln_ff_b)))
    m = dict(zip(WEIGHTS, (m_w_in, m_gn_a, m_gn_b, m_sinks, m_w_out, m_ln_mix_g, m_ln_mix_b, m_w_mq, m_w_mkv, m_w_mo, m_ln_mem_g, m_ln_mem_b, m_w_up, m_w_down, m_ln_ff_g, m_ln_ff_b)))
    v = dict(zip(WEIGHTS, (v_w_in, v_gn_a, v_gn_b, v_sinks, v_w_out, v_ln_mix_g, v_ln_mix_b, v_w_mq, v_w_mkv, v_w_mo, v_ln_mem_g, v_ln_mem_b, v_w_up, v_w_down, v_ln_ff_g, v_ln_ff_b)))
    loss, grad_x, grads, delta, new_m, new_v = _step(x, mem, positions, loss_target, w, m, v)
    return (loss, grad_x, *[grads[k] for k in WEIGHTS], *[delta[k] for k in WEIGHTS],
            *[new_m[k] for k in WEIGHTS], *[new_v[k] for k in WEIGHTS])
```
